```python
import math
import jax, jax.numpy as jnp
from jax import lax
import numpy as np

D_MODEL = 2048
BATCH = 8
SEQ = 2048
DEPTH = 2

N_MIXERS = 2
CHUNK = 128
A_WIDTH = D_MODEL
A_GROUPS = 8
A_HEAD = A_WIDTH // A_GROUPS
B_WIDTH = D_MODEL
B_WINDOWS = (2, 4, 8, 16)
B_GROUPS = len(B_WINDOWS)
B_HEAD = B_WIDTH // B_GROUPS
D_FF = 4 * D_MODEL
N_A = (DEPTH + 1) // 2
N_B = DEPTH // 2
EPS = 1e-6

kernel_name = "hybrid_chunked_gmlp_multiscale_pool"


def rmsnorm(x, g):
    x32 = x.astype(jnp.float32)
    y = x32 * lax.rsqrt(jnp.mean(x32 * x32, axis=-1, keepdims=True) + EPS)
    return (y * g.astype(jnp.float32)).astype(x.dtype)


def layernorm(x, g, b):
    x32 = x.astype(jnp.float32)
    mu = jnp.mean(x32, axis=-1, keepdims=True)
    xc = x32 - mu
    y = xc * lax.rsqrt(jnp.mean(xc * xc, axis=-1, keepdims=True) + EPS)
    return (y * g.astype(jnp.float32) + b.astype(jnp.float32)).astype(x.dtype)


def chunked_gmlp_mixer(h, w_in, ln_g, ln_b, w_s, b_s, w_out):
    bsz, seq, _ = h.shape
    n_chunks = seq // CHUNK
    z = jax.nn.gelu(h @ w_in, approximate=False)
    u, v = jnp.split(z, 2, axis=-1)
    v = layernorm(v, ln_g, ln_b)
    v = v.reshape(bsz, n_chunks, CHUNK, A_GROUPS, A_HEAD)
    causal = jnp.tril(jnp.ones((CHUNK, CHUNK), dtype=w_s.dtype))
    w_masked = w_s * causal[None]
    s = jnp.einsum('gts,bcsgd->bctgd', w_masked, v) + jnp.transpose(b_s)[None, None, :, :, None]
    gated = u * s.reshape(bsz, seq, A_WIDTH)
    return gated @ w_out


def causal_window_mean(v, w):
    seq = v.shape[1]
    v32 = v.astype(jnp.float32)
    c = jnp.cumsum(v32, axis=1)
    c_prev = jnp.pad(c, ((0, 0), (w, 0), (0, 0)))[:, :seq]
    count = jnp.minimum(jnp.arange(1, seq + 1), w).astype(jnp.float32)
    return ((c - c_prev) / count[None, :, None]).astype(v.dtype)


def multiscale_pool_mixer(h, w_in, w_grp, scale, w_out):
    bsz, seq, _ = h.shape
    v = (h @ w_in).reshape(bsz, seq, B_GROUPS, B_HEAD)
    pooled = jnp.stack(
        [causal_window_mean(v[:, :, g], w) - v[:, :, g] for g, w in enumerate(B_WINDOWS)],
        axis=2)
    mixed = jnp.einsum('bsgc,gcd->bsgd', pooled, w_grp).reshape(bsz, seq, B_WIDTH)
    return (mixed * scale) @ w_out


def sqrelu_mlp(h, w1, w2):
    a = jax.nn.relu(h @ w1)
    return (a * a) @ w2


def _fwd_setup_inputs(seed: int = 0) -> dict:
    key = jax.random.key(seed)
    ks = jax.random.split(key, 20)
    f32 = jnp.float32

    def nrm(k, shape, scale):
        return jax.random.normal(k, shape, f32) * scale

    x = jax.random.normal(ks[0], (BATCH, SEQ, D_MODEL), f32)
    a_w_in = nrm(ks[1], (N_A, D_MODEL, 2 * A_WIDTH), D_MODEL ** -0.5)
    a_ln_g = 1.0 + nrm(ks[2], (N_A, A_WIDTH), 0.02)
    a_ln_b = nrm(ks[3], (N_A, A_WIDTH), 0.02)
    a_w_s = nrm(ks[4], (N_A, A_GROUPS, CHUNK, CHUNK), 0.5 * CHUNK ** -0.5)
    a_b_s = 1.0 + nrm(ks[5], (N_A, A_GROUPS, CHUNK), 0.02)
    a_w_out = nrm(ks[6], (N_A, A_WIDTH, D_MODEL), A_WIDTH ** -0.5)
    b_w_in = nrm(ks[7], (N_B, D_MODEL, B_WIDTH), D_MODEL ** -0.5)
    b_w_grp = nrm(ks[8], (N_B, B_GROUPS, B_HEAD, B_HEAD), B_HEAD ** -0.5)
    b_scale = 0.5 + nrm(ks[9], (N_B, B_WIDTH), 0.02)
    b_w_out = nrm(ks[10], (N_B, B_WIDTH, D_MODEL), B_WIDTH ** -0.5)
    norm_mix = 1.0 + nrm(ks[11], (DEPTH, D_MODEL), 0.02)
    norm_mlp = 1.0 + nrm(ks[12], (DEPTH, D_MODEL), 0.02)
    mlp_w1 = nrm(ks[13], (DEPTH, D_MODEL, D_FF), D_MODEL ** -0.5)
    mlp_w2 = nrm(ks[14], (DEPTH, D_FF, D_MODEL), D_FF ** -0.5)
    final_norm = 1.0 + nrm(ks[15], (D_MODEL,), 0.02)
    return {"x": x,
            "a_w_in": a_w_in, "a_ln_g": a_ln_g, "a_ln_b": a_ln_b, "a_w_s": a_w_s,
            "a_b_s": a_b_s, "a_w_out": a_w_out,
            "b_w_in": b_w_in, "b_w_grp": b_w_grp, "b_scale": b_scale, "b_w_out": b_w_out,
            "norm_mix": norm_mix, "norm_mlp": norm_mlp, "mlp_w1": mlp_w1, "mlp_w2": mlp_w2,
            "final_norm": final_norm}


def _fwd_reference(x, a_w_in, a_ln_g, a_ln_b, a_w_s, a_b_s, a_w_out,
              b_w_in, b_w_grp, b_scale, b_w_out,
              norm_mix, norm_mlp, mlp_w1, mlp_w2, final_norm):
    h = x
    for i in range(DEPTH):
        hn = rmsnorm(h, norm_mix[i])
        if i % N_MIXERS == 0:
            j = i // N_MIXERS
            mix = chunked_gmlp_mixer(hn, a_w_in[j], a_ln_g[j], a_ln_b[j],
                                     a_w_s[j], a_b_s[j], a_w_out[j])
        else:
            j = i // N_MIXERS
            mix = multiscale_pool_mixer(hn, b_w_in[j], b_w_grp[j], b_scale[j], b_w_out[j])
        h = h + mix
        h = h + sqrelu_mlp(rmsnorm(h, norm_mlp[i]), mlp_w1[i], mlp_w2[i])
    return rmsnorm(h, final_norm)


import jax as _jax
import jax.numpy as _jnp

TWIN_FORMAT = 'train_step'
FWD_PARAMS = ['x', 'a_w_in', 'a_ln_g', 'a_ln_b', 'a_w_s', 'a_b_s', 'a_w_out', 'b_w_in', 'b_w_grp', 'b_scale', 'b_w_out', 'norm_mix', 'norm_mlp', 'mlp_w1', 'mlp_w2', 'final_norm']
TWIN_WEIGHTS = ['a_w_in', 'a_ln_g', 'a_ln_b', 'a_w_s', 'a_b_s', 'a_w_out', 'b_w_in', 'b_w_grp', 'b_scale', 'b_w_out', 'norm_mix', 'norm_mlp', 'mlp_w1', 'mlp_w2', 'final_norm']
TWIN_DIFF_INPUT = 'x'
TWIN_INPUTS = ['x', 'a_w_in', 'a_ln_g', 'a_ln_b', 'a_w_s', 'a_b_s', 'a_w_out', 'b_w_in', 'b_w_grp', 'b_scale', 'b_w_out', 'norm_mix', 'norm_mlp', 'mlp_w1', 'mlp_w2', 'final_norm', 'loss_target', 'm_a_w_in', 'm_a_ln_g', 'm_a_ln_b', 'm_a_w_s', 'm_a_b_s', 'm_a_w_out', 'm_b_w_in', 'm_b_w_grp', 'm_b_scale', 'm_b_w_out', 'm_norm_mix', 'm_norm_mlp', 'm_mlp_w1', 'm_mlp_w2', 'm_final_norm', 'v_a_w_in', 'v_a_ln_g', 'v_a_ln_b', 'v_a_w_s', 'v_a_b_s', 'v_a_w_out', 'v_b_w_in', 'v_b_w_grp', 'v_b_scale', 'v_b_w_out', 'v_norm_mix', 'v_norm_mlp', 'v_mlp_w1', 'v_mlp_w2', 'v_final_norm']
TWIN_OUTPUTS = ['loss', 'grad_x', 'grad_a_w_in', 'grad_a_ln_g', 'grad_a_ln_b', 'grad_a_w_s', 'grad_a_b_s', 'grad_a_w_out', 'grad_b_w_in', 'grad_b_w_grp', 'grad_b_scale', 'grad_b_w_out', 'grad_norm_mix', 'grad_norm_mlp', 'grad_mlp_w1', 'grad_mlp_w2', 'grad_final_norm', 'delta_a_w_in', 'delta_a_ln_g', 'delta_a_ln_b', 'delta_a_w_s', 'delta_a_b_s', 'delta_a_w_out', 'delta_b_w_in', 'delta_b_w_grp', 'delta_b_scale', 'delta_b_w_out', 'delta_norm_mix', 'delta_norm_mlp', 'delta_mlp_w1', 'delta_mlp_w2', 'delta_final_norm', 'new_m_a_w_in', 'new_m_a_ln_g', 'new_m_a_ln_b', 'new_m_a_w_s', 'new_m_a_b_s', 'new_m_a_w_out', 'new_m_b_w_in', 'new_m_b_w_grp', 'new_m_b_scale', 'new_m_b_w_out', 'new_m_norm_mix', 'new_m_norm_mlp', 'new_m_mlp_w1', 'new_m_mlp_w2', 'new_m_final_norm', 'new_v_a_w_in', 'new_v_a_ln_g', 'new_v_a_ln_b', 'new_v_a_w_s', 'new_v_a_b_s', 'new_v_a_w_out', 'new_v_b_w_in', 'new_v_b_w_grp', 'new_v_b_scale', 'new_v_b_w_out', 'new_v_norm_mix', 'new_v_norm_mlp', 'new_v_mlp_w1', 'new_v_mlp_w2', 'new_v_final_norm']
TWIN_LEAF_KINDS = {'loss': 'loss', 'grad_x': 'grad_x', 'grad_a_w_in': 'grad_w', 'grad_a_ln_g': 'grad_w', 'grad_a_ln_b': 'grad_w', 'grad_a_w_s': 'grad_w', 'grad_a_b_s': 'grad_w', 'grad_a_w_out': 'grad_w', 'grad_b_w_in': 'grad_w', 'grad_b_w_grp': 'grad_w', 'grad_b_scale': 'grad_w', 'grad_b_w_out': 'grad_w', 'grad_norm_mix': 'grad_w', 'grad_norm_mlp': 'grad_w', 'grad_mlp_w1': 'grad_w', 'grad_mlp_w2': 'grad_w', 'grad_final_norm': 'grad_w', 'delta_a_w_in': 'delta_w', 'delta_a_ln_g': 'delta_w', 'delta_a_ln_b': 'delta_w', 'delta_a_w_s': 'delta_w', 'delta_a_b_s': 'delta_w', 'delta_a_w_out': 'delta_w', 'delta_b_w_in': 'delta_w', 'delta_b_w_grp': 'delta_w', 'delta_b_scale': 'delta_w', 'delta_b_w_out': 'delta_w', 'delta_norm_mix': 'delta_w', 'delta_norm_mlp': 'delta_w', 'delta_mlp_w1': 'delta_w', 'delta_mlp_w2': 'delta_w', 'delta_final_norm': 'delta_w', 'new_m_a_w_in': 'new_m', 'new_m_a_ln_g': 'new_m', 'new_m_a_ln_b': 'new_m', 'new_m_a_w_s': 'new_m', 'new_m_a_b_s': 'new_m', 'new_m_a_w_out': 'new_m', 'new_m_b_w_in': 'new_m', 'new_m_b_w_grp': 'new_m', 'new_m_b_scale': 'new_m', 'new_m_b_w_out': 'new_m', 'new_m_norm_mix': 'new_m', 'new_m_norm_mlp': 'new_m', 'new_m_mlp_w1': 'new_m', 'new_m_mlp_w2': 'new_m', 'new_m_final_norm': 'new_m', 'new_v_a_w_in': 'new_v', 'new_v_a_ln_g': 'new_v', 'new_v_a_ln_b': 'new_v', 'new_v_a_w_s': 'new_v', 'new_v_a_b_s': 'new_v', 'new_v_a_w_out': 'new_v', 'new_v_b_w_in': 'new_v', 'new_v_b_w_grp': 'new_v', 'new_v_b_scale': 'new_v', 'new_v_b_w_out': 'new_v', 'new_v_norm_mix': 'new_v', 'new_v_norm_mlp': 'new_v', 'new_v_mlp_w1': 'new_v', 'new_v_mlp_w2': 'new_v', 'new_v_final_norm': 'new_v'}


def _forward(args):
    return _fwd_reference(*[args[k] for k in FWD_PARAMS])


def _output_shape():
    out = _jax.eval_shape(lambda: _forward(_fwd_setup_inputs(0)))
    return out.shape, out.dtype

N_MICROBATCH = 1
ADAM_LR = 0.001
ADAM_B1 = 0.9
ADAM_B2 = 0.999
ADAM_EPS = 1e-08
ADAM_WD = 0.01
ADAM_STEP = 10
PER_EXAMPLE_BATCH_AXIS = {'x': 0, 'loss_target': 0}
SHARED_INPUTS = []
_WEIGHT_DTYPES = {'a_w_in': _jnp.float32, 'a_ln_g': _jnp.float32, 'a_ln_b': _jnp.float32, 'a_w_s': _jnp.float32, 'a_b_s': _jnp.float32, 'a_w_out': _jnp.float32, 'b_w_in': _jnp.float32, 'b_w_grp': _jnp.float32, 'b_scale': _jnp.float32, 'b_w_out': _jnp.float32, 'norm_mix': _jnp.float32, 'norm_mlp': _jnp.float32, 'mlp_w1': _jnp.float32, 'mlp_w2': _jnp.float32, 'final_norm': _jnp.float32}
MOMENT_SCALE = {'a_w_in': 3.170082e-02, 'a_ln_g': 1.387411e-02, 'a_ln_b': 1.350465e-02, 'a_w_s': 3.818399e-02, 'a_b_s': 5.583567e-02, 'a_w_out': 4.341207e-02, 'b_w_in': 1.528111e-02, 'b_w_grp': 1.526080e-02, 'b_scale': 3.085415e-02, 'b_w_out': 1.528508e-02, 'norm_mix': 3.414293e-02, 'norm_mlp': 5.040600e-02, 'mlp_w1': 2.452402e-02, 'mlp_w2': 4.769127e-02, 'final_norm': 8.154864e+00}


def _to_microbatches(a, axis):
    t = _jnp.moveaxis(a, axis, 0)
    t = t.reshape((N_MICROBATCH, t.shape[0] // N_MICROBATCH) + t.shape[1:])
    return _jnp.moveaxis(t, 1, axis + 1)


def setup_inputs(seed: int = 0) -> dict:
    inp = _fwd_setup_inputs(seed)
    key = _jax.random.fold_in(_jax.random.key(seed), 7919)
    shape, _ = _output_shape()
    out = dict(inp)
    out["loss_target"] = _jax.random.normal(_jax.random.fold_in(key, 0), shape, _jnp.float32)
    for i, name in enumerate(TWIN_WEIGHTS):
        w = inp[name].astype(_jnp.float32)
        if MOMENT_SCALE is None:
            s = _jnp.sqrt(_jnp.mean(_jnp.square(w)) + 1e-30)
        else:
            s = MOMENT_SCALE[name]
        km, kv = _jax.random.split(_jax.random.fold_in(key, i + 1))
        out[name] = w
        out["m_" + name] = s * _jax.random.normal(km, w.shape, _jnp.float32)
        out["v_" + name] = (s * s) * _jax.random.uniform(kv, w.shape, _jnp.float32, 0.5, 1.5)
    if N_MICROBATCH > 1:
        for name, axis in PER_EXAMPLE_BATCH_AXIS.items():
            out[name] = _to_microbatches(out[name], axis)
    return {'x': out['x'], 'a_w_in': out['a_w_in'], 'a_ln_g': out['a_ln_g'], 'a_ln_b': out['a_ln_b'], 'a_w_s': out['a_w_s'], 'a_b_s': out['a_b_s'], 'a_w_out': out['a_w_out'], 'b_w_in': out['b_w_in'], 'b_w_grp': out['b_w_grp'], 'b_scale': out['b_scale'], 'b_w_out': out['b_w_out'], 'norm_mix': out['norm_mix'], 'norm_mlp': out['norm_mlp'], 'mlp_w1': out['mlp_w1'], 'mlp_w2': out['mlp_w2'], 'final_norm': out['final_norm'], 'loss_target': out['loss_target'], 'm_a_w_in': out['m_a_w_in'], 'm_a_ln_g': out['m_a_ln_g'], 'm_a_ln_b': out['m_a_ln_b'], 'm_a_w_s': out['m_a_w_s'], 'm_a_b_s': out['m_a_b_s'], 'm_a_w_out': out['m_a_w_out'], 'm_b_w_in': out['m_b_w_in'], 'm_b_w_grp': out['m_b_w_grp'], 'm_b_scale': out['m_b_scale'], 'm_b_w_out': out['m_b_w_out'], 'm_norm_mix': out['m_norm_mix'], 'm_norm_mlp': out['m_norm_mlp'], 'm_mlp_w1': out['m_mlp_w1'], 'm_mlp_w2': out['m_mlp_w2'], 'm_final_norm': out['m_final_norm'], 'v_a_w_in': out['v_a_w_in'], 'v_a_ln_g': out['v_a_ln_g'], 'v_a_ln_b': out['v_a_ln_b'], 'v_a_w_s': out['v_a_w_s'], 'v_a_b_s': out['v_a_b_s'], 'v_a_w_out': out['v_a_w_out'], 'v_b_w_in': out['v_b_w_in'], 'v_b_w_grp': out['v_b_w_grp'], 'v_b_scale': out['v_b_scale'], 'v_b_w_out': out['v_b_w_out'], 'v_norm_mix': out['v_norm_mix'], 'v_norm_mlp': out['v_norm_mlp'], 'v_mlp_w1': out['v_mlp_w1'], 'v_mlp_w2': out['v_mlp_w2'], 'v_final_norm': out['v_final_norm']}


def _loss(weights, diff, rest, loss_target):
    with _jax.named_scope("forward"):
        args = {**rest, TWIN_DIFF_INPUT: diff, **{k: w.astype(_WEIGHT_DTYPES[k]) for k, w in weights.items()}}
        y = _forward(args)
    with _jax.named_scope("loss_head"):
        err = _jnp.square(y.astype(_jnp.float32) - loss_target)
        return 0.5 * _jnp.sum(_jnp.mean(err, axis=-1)) if err.ndim else 0.5 * err


def _adamw(w, g, m, v):
    m = ADAM_B1 * m + (1.0 - ADAM_B1) * g
    v = ADAM_B2 * v + (1.0 - ADAM_B2) * _jnp.square(g)
    m_hat = m / (1.0 - ADAM_B1 ** ADAM_STEP)
    v_hat = v / (1.0 - ADAM_B2 ** ADAM_STEP)
    delta = -ADAM_LR * (m_hat / (_jnp.sqrt(v_hat) + ADAM_EPS) + ADAM_WD * w)
    return delta, m, v


def reference(x, a_w_in, a_ln_g, a_ln_b, a_w_s, a_b_s, a_w_out, b_w_in, b_w_grp, b_scale, b_w_out, norm_mix, norm_mlp, mlp_w1, mlp_w2, final_norm, loss_target, m_a_w_in, m_a_ln_g, m_a_ln_b, m_a_w_s, m_a_b_s, m_a_w_out, m_b_w_in, m_b_w_grp, m_b_scale, m_b_w_out, m_norm_mix, m_norm_mlp, m_mlp_w1, m_mlp_w2, m_final_norm, v_a_w_in, v_a_ln_g, v_a_ln_b, v_a_w_s, v_a_b_s, v_a_w_out, v_b_w_in, v_b_w_grp, v_b_scale, v_b_w_out, v_norm_mix, v_norm_mlp, v_mlp_w1, v_mlp_w2, v_final_norm):
    given = dict(x=x, a_w_in=a_w_in, a_ln_g=a_ln_g, a_ln_b=a_ln_b, a_w_s=a_w_s, a_b_s=a_b_s, a_w_out=a_w_out, b_w_in=b_w_in, b_w_grp=b_w_grp, b_scale=b_scale, b_w_out=b_w_out, norm_mix=norm_mix, norm_mlp=norm_mlp, mlp_w1=mlp_w1, mlp_w2=mlp_w2, final_norm=final_norm, loss_target=loss_target, m_a_w_in=m_a_w_in, m_a_ln_g=m_a_ln_g, m_a_ln_b=m_a_ln_b, m_a_w_s=m_a_w_s, m_a_b_s=m_a_b_s, m_a_w_out=m_a_w_out, m_b_w_in=m_b_w_in, m_b_w_grp=m_b_w_grp, m_b_scale=m_b_scale, m_b_w_out=m_b_w_out, m_norm_mix=m_norm_mix, m_norm_mlp=m_norm_mlp, m_mlp_w1=m_mlp_w1, m_mlp_w2=m_mlp_w2, m_final_norm=m_final_norm, v_a_w_in=v_a_w_in, v_a_ln_g=v_a_ln_g, v_a_ln_b=v_a_ln_b, v_a_w_s=v_a_w_s, v_a_b_s=v_a_b_s, v_a_w_out=v_a_w_out, v_b_w_in=v_b_w_in, v_b_w_grp=v_b_w_grp, v_b_scale=v_b_scale, v_b_w_out=v_b_w_out, v_norm_mix=v_norm_mix, v_norm_mlp=v_norm_mlp, v_mlp_w1=v_mlp_w1, v_mlp_w2=v_mlp_w2, v_final_norm=v_final_norm)
    weights = {n: given[n] for n in TWIN_WEIGHTS}
    shared = {n: given[n] for n in SHARED_INPUTS}
    per_example = {n: given[n] for n in ['x']}
    grad_fn = _jax.value_and_grad(_loss, argnums=(0, 1))

    def one_microbatch(ex, loss_target):
        ex = dict(ex)
        diff = ex.pop(TWIN_DIFF_INPUT)
        return grad_fn(weights, diff, {**shared, **ex}, loss_target)

    if N_MICROBATCH == 1:
        loss, (grad_w, grad_x) = one_microbatch(per_example, given["loss_target"])
    else:
        def body(carry, xs):
            loss_sum, grad_sum = carry
            l_k, (gw_k, gx_k) = one_microbatch(xs[0], xs[1])
            with _jax.named_scope("update"):
                return (loss_sum + l_k, _jax.tree.map(_jnp.add, grad_sum, gw_k)), gx_k

        init = (_jnp.zeros((), _jnp.float32), _jax.tree.map(_jnp.zeros_like, weights))
        (loss, grad_w), grad_x = _jax.lax.scan(body, init, (per_example, given["loss_target"]))
    with _jax.named_scope("update"):
        delta_w, new_m, new_v = {}, {}, {}
        for n in TWIN_WEIGHTS:
            delta_w[n], new_m[n], new_v[n] = _adamw(weights[n], grad_w[n], given["m_" + n], given["v_" + n])
    return (loss, grad_x, *[grad_w[n] for n in TWIN_WEIGHTS], *[delta_w[n] for n in TWIN_WEIGHTS],
            *[new_m[n] for n in TWIN_WEIGHTS], *[new_v[n] for n in TWIN_WEIGHTS])
```

```python
import functools
import math

import jax
import jax.numpy as jnp
from jax import lax
from jax.experimental import pallas as pl
from jax.experimental.pallas import tpu as pltpu

F32 = jnp.float32
BF16 = jnp.bfloat16
MESH = pl.DeviceIdType.MESH

N_DEV = 8
N_CHIPS = 4
CHUNK = 128
A_GROUPS = 8
B_WINDOWS = (2, 4, 8, 16)
B_GROUPS = len(B_WINDOWS)
EPS = 1e-6
ADAM_LR = 0.001
ADAM_B1 = 0.9
ADAM_B2 = 0.999
ADAM_EPS = 1e-08
ADAM_WD = 0.01
ADAM_STEP = 10

VMEM_LIMIT = 48 * 1024 * 1024

NN = (((1,), (0,)), ((), ()))
NT = (((1,), (1,)), ((), ()))
TN = (((0,), (0,)), ((), ()))


def _tile(n, pref):
    return pref if n % pref == 0 else n


def _params(sem):
    return pltpu.CompilerParams(dimension_semantics=sem, vmem_limit_bytes=VMEM_LIMIT)


def _matmul(name, a, b, dims, grid, a_spec, b_spec, out_shape, out_specs, acc_shape,
            epilogue, extras=(), extra_specs=()):
    nk = grid[2]
    n_extra = len(extras)
    n_out = len(out_shape)

    def body(*refs):
        a_ref, b_ref = refs[0], refs[1]
        extra_refs = refs[2:2 + n_extra]
        out_refs = refs[2 + n_extra:2 + n_extra + n_out]

        def finish(acc):
            outs = epilogue(acc, *[r[...] for r in extra_refs])
            for o_ref, o in zip(out_refs, outs):
                o_ref[...] = o.astype(o_ref.dtype)

        part = lax.dot_general(a_ref[...], b_ref[...], dims, preferred_element_type=F32)
        if nk == 1:
            finish(part)
        else:
            acc_ref = refs[-1]
            k = pl.program_id(2)

            @pl.when(k == 0)
            def _():
                acc_ref[...] = part

            @pl.when(k > 0)
            def _():
                acc_ref[...] += part

            @pl.when(k == nk - 1)
            def _():
                finish(acc_ref[...])

    scratch = [] if nk == 1 else [pltpu.VMEM(acc_shape, F32)]
    return pl.pallas_call(
        body, grid=grid, in_specs=[a_spec, b_spec, *extra_specs], out_specs=out_specs,
        out_shape=out_shape, scratch_shapes=scratch,
        compiler_params=_params(("parallel", "parallel", "arbitrary")), name=name,
    )(a, b, *extras)


def _sds(shape, dtype):
    return jax.ShapeDtypeStruct(shape, dtype)


def _mm_nn(name, a, b, epilogue, out_dtypes, extras=(), extra_kinds=(), slab=False, tm=1024, tn=1024, tk=1024):
    m, kd = a.shape
    if slab:
        n_slab, _, w = b.shape
        n = n_slab * w
        tn = _tile(w, min(tn, w))
        per = w // tn
        tk = _tile(kd, tk)
        b_spec = pl.BlockSpec((None, tk, tn), lambda i, j, k: (j // per, k, j % per))
    else:
        n = b.shape[1]
        tn = _tile(n, tn)
        tk = _tile(kd, tk)
        b_spec = pl.BlockSpec((tk, tn), lambda i, j, k: (k, j))
    tm = _tile(m, tm)
    grid = (m // tm, n // tn, kd // tk)
    a_spec = pl.BlockSpec((tm, tk), lambda i, j, k: (i, k))
    tile_spec = pl.BlockSpec((tm, tn), lambda i, j, k: (i, j))
    row_spec = pl.BlockSpec((1, tn), lambda i, j, k: (0, j))
    extra_specs = [tile_spec if kind == "tile" else row_spec for kind in extra_kinds]
    return _matmul(name, a, b, NN, grid, a_spec, b_spec,
                   [_sds((m, n), d) for d in out_dtypes], [tile_spec for _ in out_dtypes],
                   (tm, tn), epilogue, extras, extra_specs)


def _mm_nt(name, a, b, epilogue, out_dtypes, extras=(), extra_kinds=(), slab=False, tm=1024, tn=1024, tk=1024):
    m, kd = a.shape
    if slab:
        n_slab, n, w = b.shape
        tk = _tile(w, min(tk, w))
        per = w // tk
        tn = _tile(n, tn)
        b_spec = pl.BlockSpec((None, tn, tk), lambda i, j, k: (k // per, j, k % per))
    else:
        n = b.shape[0]
        tn = _tile(n, tn)
        tk = _tile(kd, tk)
        b_spec = pl.BlockSpec((tn, tk), lambda i, j, k: (j, k))
    tm = _tile(m, tm)
    grid = (m // tm, n // tn, kd // tk)
    a_spec = pl.BlockSpec((tm, tk), lambda i, j, k: (i, k))
    tile_spec = pl.BlockSpec((tm, tn), lambda i, j, k: (i, j))
    row_spec = pl.BlockSpec((1, tn), lambda i, j, k: (0, j))
    extra_specs = [tile_spec if kind == "tile" else row_spec for kind in extra_kinds]
    return _matmul(name, a, b, NT, grid, a_spec, b_spec,
                   [_sds((m, n), d) for d in out_dtypes], [tile_spec for _ in out_dtypes],
                   (tm, tn), epilogue, extras, extra_specs)


def _mm_tn(name, a, b, slab_w=None, tm=1024, tn=1024, tk=1024):
    s, m = a.shape
    n = b.shape[1]
    tm = _tile(m, tm)
    tk = _tile(s, tk)
    if slab_w is None:
        tn = _tile(n, tn)
        out_shape = [_sds((m, n), BF16)]
        out_specs = [pl.BlockSpec((tm, tn), lambda i, j, k: (i, j))]
    else:
        tn = _tile(slab_w, min(tn, slab_w))
        per = slab_w // tn
        out_shape = [_sds((n // slab_w, m, slab_w), BF16)]
        out_specs = [pl.BlockSpec((None, tm, tn), lambda i, j, k: (j // per, i, j % per))]
    grid = (m // tm, n // tn, s // tk)
    a_spec = pl.BlockSpec((tk, tm), lambda i, j, k: (k, i))
    b_spec = pl.BlockSpec((tk, tn), lambda i, j, k: (k, j))
    return _matmul(name, a, b, TN, grid, a_spec, b_spec, out_shape, out_specs, (tm, tn),
                   lambda acc: (acc,))[0]


def _rms_fwd(name, h, g):
    s, d = h.shape
    tr = _tile(s, 256)

    def body(h_ref, g_ref, o_ref):
        x = h_ref[...]
        r = lax.rsqrt(jnp.mean(x * x, axis=-1, keepdims=True) + EPS)
        o_ref[...] = (x * r * g_ref[...]).astype(o_ref.dtype)

    row = pl.BlockSpec((tr, d), lambda i: (i, 0))
    vec = pl.BlockSpec((1, d), lambda i: (0, 0))
    return pl.pallas_call(body, grid=(s // tr,), in_specs=[row, vec], out_specs=row,
                          out_shape=_sds((s, d), BF16), compiler_params=_params(("parallel",)),
                          name=name)(h, g)


def _accumulate(ref, part, step):
    @pl.when(step == 0)
    def _():
        ref[...] = part

    @pl.when(step > 0)
    def _():
        ref[...] += part


def _rms_bwd(name, dhn, h, g, dres):
    s, d = h.shape
    tr = _tile(s, 256)

    def body(dhn_ref, h_ref, g_ref, dres_ref, dh_ref, dhb_ref, gp_ref):
        x = h_ref[...]
        r = lax.rsqrt(jnp.mean(x * x, axis=-1, keepdims=True) + EPS)
        n = x * r
        dy = dhn_ref[...]
        dn = dy * g_ref[...]
        dh = dres_ref[...] + r * (dn - n * jnp.mean(dn * n, axis=-1, keepdims=True))
        dh_ref[...] = dh
        dhb_ref[...] = dh.astype(BF16)
        _accumulate(gp_ref, jnp.sum(dy * n, axis=0, keepdims=True), pl.program_id(0))

    row = pl.BlockSpec((tr, d), lambda i: (i, 0))
    vec = pl.BlockSpec((1, d), lambda i: (0, 0))
    return pl.pallas_call(
        body, grid=(s // tr,), in_specs=[row, row, vec, row], out_specs=[row, row, vec],
        out_shape=[_sds((s, d), F32), _sds((s, d), BF16), _sds((1, d), F32)],
        compiler_params=_params(("arbitrary",)), name=name)(dhn, h, g, dres)


def _loss_head(name, h, g, target):
    s, d = h.shape
    tr = _tile(s, 256)

    def body(h_ref, g_ref, t_ref, dh_ref, dhb_ref, gp_ref, loss_ref):
        x = h_ref[...]
        gg = g_ref[...]
        r = lax.rsqrt(jnp.mean(x * x, axis=-1, keepdims=True) + EPS)
        n = x * r
        e = n * gg - t_ref[...]
        dy = e * (1.0 / d)
        dn = dy * gg
        dh = r * (dn - n * jnp.mean(dn * n, axis=-1, keepdims=True))
        dh_ref[...] = dh
        dhb_ref[...] = dh.astype(BF16)
        step = pl.program_id(0)
        _accumulate(gp_ref, jnp.sum(dy * n, axis=0, keepdims=True), step)
        row_loss = jnp.mean(e * e, axis=-1, keepdims=True)
        _accumulate(loss_ref, 0.5 * jnp.sum(row_loss, axis=0, keepdims=True), step)

    row = pl.BlockSpec((tr, d), lambda i: (i, 0))
    vec = pl.BlockSpec((1, d), lambda i: (0, 0))
    one = pl.BlockSpec((1, 1), lambda i: (0, 0))
    return pl.pallas_call(
        body, grid=(s // tr,), in_specs=[row, vec, row], out_specs=[row, row, vec, one],
        out_shape=[_sds((s, d), F32), _sds((s, d), BF16), _sds((1, d), F32), _sds((1, 1), F32)],
        compiler_params=_params(("arbitrary",)), name=name)(h, g, target)


_SQRT_HALF = math.sqrt(0.5)
_INV_SQRT_2PI = 1.0 / math.sqrt(2.0 * math.pi)


def _gelu(x):
    return 0.5 * x * (1.0 + lax.erf(x * _SQRT_HALF))


def _gelu_grad(x):
    return 0.5 * (1.0 + lax.erf(x * _SQRT_HALF)) + x * jnp.exp(-0.5 * x * x) * _INV_SQRT_2PI


def _causal_mask():
    row = lax.broadcasted_iota(jnp.int32, (CHUNK, CHUNK), 0)
    col = lax.broadcasted_iota(jnp.int32, (CHUNK, CHUNK), 1)
    return row >= col


def _layernorm_parts(v):
    mu = jnp.mean(v, axis=-1, keepdims=True)
    xc = v - mu
    rstd = lax.rsqrt(jnp.mean(xc * xc, axis=-1, keepdims=True) + EPS)
    return xc * rstd, rstd


def _amix_fwd(name, pre, ln_g, ln_b, w_s, b_s_col):
    s, w2 = pre.shape
    w = w2 // 2
    head = w // A_GROUPS

    def body(pre_ref, g_ref, b_ref, ws_ref, bs_ref, o_ref):
        u = _gelu(pre_ref[:, :w])
        v = _gelu(pre_ref[:, w:])
        vhat, _ = _layernorm_parts(v)
        vn = (vhat * g_ref[...] + b_ref[...]).astype(BF16)
        mask = _causal_mask()
        for grp in range(A_GROUPS):
            cols = slice(grp * head, (grp + 1) * head)
            wm = jnp.where(mask, ws_ref[grp], 0.0).astype(BF16)
            sg = jnp.dot(wm, vn[:, cols], preferred_element_type=F32) + bs_ref[grp]
            o_ref[:, cols] = (u[:, cols] * sg).astype(o_ref.dtype)

    vec = pl.BlockSpec((1, w), lambda i: (0, 0))
    return pl.pallas_call(
        body, grid=(s // CHUNK,),
        in_specs=[pl.BlockSpec((CHUNK, w2), lambda i: (i, 0)), vec, vec,
                  pl.BlockSpec((A_GROUPS, CHUNK, CHUNK), lambda i: (0, 0, 0)),
                  pl.BlockSpec((A_GROUPS, CHUNK, 1), lambda i: (0, 0, 0))],
        out_specs=pl.BlockSpec((CHUNK, w), lambda i: (i, 0)),
        out_shape=_sds((s, w), BF16), compiler_params=_params(("parallel",)), name=name,
    )(pre, ln_g, ln_b, w_s, b_s_col)


def _amix_bwd(name, pre, dgated, ln_g, ln_b, w_s, b_s_col):
    s, w2 = pre.shape
    w = w2 // 2
    head = w // A_GROUPS

    def body(pre_ref, dg_ref, g_ref, b_ref, ws_ref, bs_ref, dpre_ref, glg_ref, glb_ref, gws_ref, gbs_ref):
        step = pl.program_id(0)
        pre_u = pre_ref[:, :w]
        pre_v = pre_ref[:, w:]
        u = _gelu(pre_u)
        v = _gelu(pre_v)
        vhat, rstd = _layernorm_parts(v)
        gain = g_ref[...]
        vn = (vhat * gain + b_ref[...]).astype(BF16)
        dgated = dg_ref[...]
        ds = dgated * u
        dsb = ds.astype(BF16)
        mask = _causal_mask()
        du_parts = []
        dvn_parts = []
        for grp in range(A_GROUPS):
            cols = slice(grp * head, (grp + 1) * head)
            wm = jnp.where(mask, ws_ref[grp], 0.0).astype(BF16)
            sg = jnp.dot(wm, vn[:, cols], preferred_element_type=F32) + bs_ref[grp]
            du_parts.append(dgated[:, cols] * sg)
            gws = lax.dot_general(dsb[:, cols], vn[:, cols], NT, preferred_element_type=F32)
            gws = jnp.where(mask, gws, 0.0)
            gbs = jnp.sum(ds[:, cols], axis=-1, keepdims=True)

            @pl.when(step == 0)
            def _():
                gws_ref[grp] = gws
                gbs_ref[grp] = gbs

            @pl.when(step > 0)
            def _():
                gws_ref[grp] += gws
                gbs_ref[grp] += gbs

            dvn_parts.append(lax.dot_general(wm, dsb[:, cols], TN, preferred_element_type=F32))
        du = jnp.concatenate(du_parts, axis=-1)
        dvn = jnp.concatenate(dvn_parts, axis=-1)
        _accumulate(glg_ref, jnp.sum(dvn * vhat, axis=0, keepdims=True), step)
        _accumulate(glb_ref, jnp.sum(dvn, axis=0, keepdims=True), step)
        dvhat = dvn * gain
        dv = rstd * (dvhat - jnp.mean(dvhat, axis=-1, keepdims=True)
                     - vhat * jnp.mean(dvhat * vhat, axis=-1, keepdims=True))
        dpre_ref[:, :w] = (du * _gelu_grad(pre_u)).astype(dpre_ref.dtype)
        dpre_ref[:, w:] = (dv * _gelu_grad(pre_v)).astype(dpre_ref.dtype)

    vec = pl.BlockSpec((1, w), lambda i: (0, 0))
    ws_spec = pl.BlockSpec((A_GROUPS, CHUNK, CHUNK), lambda i: (0, 0, 0))
    bs_spec = pl.BlockSpec((A_GROUPS, CHUNK, 1), lambda i: (0, 0, 0))
    return pl.pallas_call(
        body, grid=(s // CHUNK,),
        in_specs=[pl.BlockSpec((CHUNK, w2), lambda i: (i, 0)), pl.BlockSpec((CHUNK, w), lambda i: (i, 0)),
                  vec, vec, ws_spec, bs_spec],
        out_specs=[pl.BlockSpec((CHUNK, w2), lambda i: (i, 0)), vec, vec, ws_spec, bs_spec],
        out_shape=[_sds((s, w2), BF16), _sds((1, w), F32), _sds((1, w), F32),
                   _sds((A_GROUPS, CHUNK, CHUNK), F32), _sds((A_GROUPS, CHUNK, 1), F32)],
        compiler_params=_params(("arbitrary",)), name=name,
    )(pre, dgated, ln_g, ln_b, w_s, b_s_col)


def _shift_rows(x, k, forward):
    n = x.shape[0]
    row = lax.broadcasted_iota(jnp.int32, x.shape, 0)
    if forward:
        return jnp.where(row >= k, pltpu.roll(x, k, 0), 0.0)
    return jnp.where(row < n - k, pltpu.roll(x, n - k, 0), 0.0)


def _window_sum(x, window, forward):
    k = 1
    while k < window:
        x = x + _shift_rows(x, k, forward)
        k *= 2
    return x


def _pool(name, v, backward):
    s, w = v.shape
    head = w // B_GROUPS
    lane = _tile(head, 128)

    def body(v_ref, o_ref):
        grp = pl.program_id(0)
        x = v_ref[...]
        t = lax.broadcasted_iota(jnp.int32, x.shape, 0)
        for idx, window in enumerate(B_WINDOWS):
            @pl.when(grp == idx)
            def _():
                inv_count = 1.0 / jnp.minimum(t + 1, window).astype(F32)
                if backward:
                    out = _window_sum(x * inv_count, window, False) - x
                else:
                    out = _window_sum(x, window, True) * inv_count - x
                o_ref[...] = out.astype(o_ref.dtype)

    per = head // lane
    spec = pl.BlockSpec((s, lane), lambda g, j: (0, g * per + j))
    return pl.pallas_call(body, grid=(B_GROUPS, per), in_specs=[spec], out_specs=spec,
                          out_shape=_sds((s, w), BF16),
                          compiler_params=_params(("parallel", "parallel")), name=name)(v)


def _colsum(name, a):
    s, d = a.shape
    tr = _tile(s, 256)

    def body(a_ref, o_ref):
        _accumulate(o_ref, jnp.sum(a_ref[...], axis=0, keepdims=True), pl.program_id(0))

    return pl.pallas_call(body, grid=(s // tr,), in_specs=[pl.BlockSpec((tr, d), lambda i: (i, 0))],
                          out_specs=pl.BlockSpec((1, d), lambda i: (0, 0)), out_shape=_sds((1, d), F32),
                          compiler_params=_params(("arbitrary",)), name=name)(a)


def _adamw(w, g, m, v):
    m = ADAM_B1 * m + (1.0 - ADAM_B1) * g
    v = ADAM_B2 * v + (1.0 - ADAM_B2) * (g * g)
    m_hat = m / (1.0 - ADAM_B1 ** ADAM_STEP)
    v_hat = v / (1.0 - ADAM_B2 ** ADAM_STEP)
    delta = -ADAM_LR * (m_hat / (jnp.sqrt(v_hat) + ADAM_EPS) + ADAM_WD * w)
    return delta, m, v


def _adam_rows(name, g, w, m, v):
    r, c = g.shape
    tr = _tile(r, 256)

    def body(g_ref, w_ref, m_ref, v_ref, d_ref, nm_ref, nv_ref):
        d_ref[...], nm_ref[...], nv_ref[...] = _adamw(w_ref[...], g_ref[...], m_ref[...], v_ref[...])

    spec = pl.BlockSpec((tr, c), lambda i: (i, 0))
    return pl.pallas_call(body, grid=(r // tr,), in_specs=[spec] * 4, out_specs=[spec] * 3,
                          out_shape=[_sds((r, c), F32)] * 3, compiler_params=_params(("parallel",)),
                          name=name)(g, w, m, v)


def _position():
    return lax.axis_index("x"), lax.axis_index("y"), lax.axis_index("c")


def _other_chips(x, y):
    return [(1 - x, y), (x, 1 - y), (1 - x, 1 - y)]


_ANY = pl.BlockSpec(memory_space=pl.ANY)


def _all_gather(name, shards):
    n = len(shards)

    def body(*refs):
        src = refs[:n]
        out = refs[n:2 * n]
        send_sems, recv_sems, local_sems = refs[2 * n:]
        x, y, c = _position()
        me, sibling = (x, y, c), (x, y, 1 - c)
        chips = _other_chips(x, y)

        def slot(px, py, pc):
            return 4 * px + 2 * py + pc

        def copy(t, k, block, to, from_input=False):
            dst = out[t].at[slot(*block)]
            return pltpu.make_async_remote_copy(
                src_ref=src[t] if from_input else dst, dst_ref=dst,
                send_sem=send_sems.at[t, k], recv_sem=recv_sems.at[t, k],
                device_id=to, device_id_type=MESH)

        local = [pltpu.make_async_copy(src[t], out[t].at[slot(*me)], local_sems.at[t]) for t in range(n)]
        for cp in local:
            cp.start()
        first = []
        for t in range(n):
            first.append(copy(t, 0, me, sibling, from_input=True))
            first += [copy(t, 1 + j, me, (*chip, c), from_input=True) for j, chip in enumerate(chips)]
        for cp in first:
            cp.start()
        passed = []
        for j, chip in enumerate(chips):
            for t in range(n):
                copy(t, 1 + j, (*chip, c), me).wait_recv()
                fwd = copy(t, 4 + j, (*chip, c), sibling)
                fwd.start()
                passed.append(fwd)
        for t in range(n):
            copy(t, 0, sibling, me).wait_recv()
            for j, chip in enumerate(chips):
                copy(t, 4 + j, (*chip, 1 - c), me).wait_recv()
        for cp in first + passed:
            cp.wait_send()
        for cp in local:
            cp.wait()

    return pl.pallas_call(
        body, in_specs=[_ANY] * n, out_specs=[_ANY] * n,
        out_shape=[_sds((N_DEV,) + s.shape, s.dtype) for s in shards],
        scratch_shapes=[pltpu.SemaphoreType.DMA((n, 7)), pltpu.SemaphoreType.DMA((n, 7)),
                        pltpu.SemaphoreType.DMA((n,))],
        name=name)(*shards)


def _exchange_sibling(name, fulls):
    n = len(fulls)

    def body(*refs):
        src = refs[:n]
        out = refs[n:2 * n]
        send_sems, recv_sems = refs[2 * n:]
        x, y, c = _position()
        copies = [pltpu.make_async_remote_copy(
            src_ref=src[t].at[:, 1 - c], dst_ref=out[t], send_sem=send_sems.at[t], recv_sem=recv_sems.at[t],
            device_id=(x, y, 1 - c), device_id_type=MESH) for t in range(n)]
        for cp in copies:
            cp.start()
        for cp in copies:
            cp.wait()

    return pl.pallas_call(
        body, in_specs=[_ANY] * n, out_specs=[_ANY] * n,
        out_shape=[_sds((N_CHIPS,) + f.shape[2:], f.dtype) for f in fulls],
        scratch_shapes=[pltpu.SemaphoreType.DMA((n,)), pltpu.SemaphoreType.DMA((n,))],
        name=name)(*fulls)


def _add_sibling(name, full, recv, core):
    _, _, r, c = full.shape
    tr = _tile(r, max(8, (256 * 1024) // c))

    def body(core_ref, f_ref, r_ref, o_ref):
        o_ref[...] = (f_ref[...].astype(F32) + r_ref[...].astype(F32)).astype(o_ref.dtype)

    grid_spec = pltpu.PrefetchScalarGridSpec(
        num_scalar_prefetch=1, grid=(N_CHIPS, r // tr),
        in_specs=[pl.BlockSpec((None, None, tr, c), lambda p, i, core_ref: (p, core_ref[0], i, 0)),
                  pl.BlockSpec((None, tr, c), lambda p, i, core_ref: (p, i, 0))],
        out_specs=pl.BlockSpec((None, tr, c), lambda p, i, core_ref: (p, i, 0)))
    return pl.pallas_call(body, grid_spec=grid_spec, out_shape=_sds((N_CHIPS, r, c), BF16),
                          compiler_params=_params(("parallel", "parallel")), name=name)(core, full, recv)


def _exchange_chips(name, partials):
    n = len(partials)

    def body(*refs):
        src = refs[:n]
        out = refs[n:2 * n]
        send_sems, recv_sems = refs[2 * n:]
        x, y, c = _position()
        copies = []
        for t in range(n):
            for j, (px, py) in enumerate(_other_chips(x, y)):
                copies.append(pltpu.make_async_remote_copy(
                    src_ref=src[t].at[2 * px + py], dst_ref=out[t].at[j],
                    send_sem=send_sems.at[t, j], recv_sem=recv_sems.at[t, j],
                    device_id=(px, py, c), device_id_type=MESH))
        for cp in copies:
            cp.start()
        for cp in copies:
            cp.wait()

    return pl.pallas_call(
        body, in_specs=[_ANY] * n, out_specs=[_ANY] * n,
        out_shape=[_sds((N_CHIPS - 1,) + p.shape[1:], p.dtype) for p in partials],
        scratch_shapes=[pltpu.SemaphoreType.DMA((n, 3)), pltpu.SemaphoreType.DMA((n, 3))],
        name=name)(*partials)


def _reduce_adam(name, partial, recv, chip, w, m, v):
    _, r, c = partial.shape
    tr = _tile(r, max(8, (128 * 1024) // c))

    def body(chip_ref, p_ref, r_ref, w_ref, m_ref, v_ref, g_ref, d_ref, nm_ref, nv_ref):
        g = p_ref[...].astype(F32)
        for j in range(N_CHIPS - 1):
            g = g + r_ref[j].astype(F32)
        g_ref[...] = g
        d_ref[...], nm_ref[...], nv_ref[...] = _adamw(w_ref[...], g, m_ref[...], v_ref[...])

    flat = pl.BlockSpec((tr, c), lambda i, chip_ref: (i, 0))
    grid_spec = pltpu.PrefetchScalarGridSpec(
        num_scalar_prefetch=1, grid=(r // tr,),
        in_specs=[pl.BlockSpec((None, tr, c), lambda i, chip_ref: (chip_ref[0], i, 0)),
                  pl.BlockSpec((N_CHIPS - 1, tr, c), lambda i, chip_ref: (0, i, 0)),
                  flat, flat, flat],
        out_specs=[flat] * 4)
    return pl.pallas_call(body, grid_spec=grid_spec, out_shape=[_sds((r, c), F32)] * 4,
                          compiler_params=_params(("parallel",)), name=name)(chip, partial, recv, w, m, v)


def _all_reduce_small(name, packed):
    r, lanes = packed.shape

    def body(in_ref, out_ref, gather_ref, send_sems, recv_sems):
        x, y, c = _position()
        me = 4 * x + 2 * y + c
        gather_ref[me] = in_ref[...]
        copies = []
        for k in range(1, N_DEV):
            fx, fy, fc = (k >> 2) & 1, (k >> 1) & 1, k & 1
            peer = (x ^ fx, y ^ fy, c ^ fc)
            copies.append(pltpu.make_async_remote_copy(
                src_ref=in_ref, dst_ref=gather_ref.at[me],
                send_sem=send_sems.at[k - 1], recv_sem=recv_sems.at[k - 1],
                device_id=peer, device_id_type=MESH))
        for cp in copies:
            cp.start()
        for cp in copies:
            cp.wait()
        total = gather_ref[0]
        for d in range(1, N_DEV):
            total = total + gather_ref[d]
        out_ref[...] = total

    vmem = pl.BlockSpec(memory_space=pltpu.VMEM)
    return pl.pallas_call(
        body, in_specs=[vmem], out_specs=vmem, out_shape=_sds((r, lanes), F32),
        scratch_shapes=[pltpu.VMEM((N_DEV, r, lanes), F32), pltpu.SemaphoreType.DMA((N_DEV - 1,)),
                        pltpu.SemaphoreType.DMA((N_DEV - 1,))],
        compiler_params=pltpu.CompilerParams(vmem_limit_bytes=VMEM_LIMIT), name=name)(packed)


def _pack(arrays):
    return jnp.concatenate([a.reshape(-1, 128) for a in arrays], axis=0)


def _unpack(packed, shapes):
    out, row = [], 0
    for shape in shapes:
        rows = math.prod(shape) // 128
        out.append(packed[row:row + rows].reshape(shape))
        row += rows
    return out


def kernel(x, a_w_in, a_ln_g, a_ln_b, a_w_s, a_b_s, a_w_out, b_w_in, b_w_grp, b_scale, b_w_out, norm_mix, norm_mlp, mlp_w1, mlp_w2, final_norm, loss_target, m_a_w_in, m_a_ln_g, m_a_ln_b, m_a_w_s, m_a_b_s, m_a_w_out, m_b_w_in, m_b_w_grp, m_b_scale, m_b_w_out, m_norm_mix, m_norm_mlp, m_mlp_w1, m_mlp_w2, m_final_norm, v_a_w_in, v_a_ln_g, v_a_ln_b, v_a_w_s, v_a_b_s, v_a_w_out, v_b_w_in, v_b_w_grp, v_b_scale, v_b_w_out, v_norm_mix, v_norm_mlp, v_mlp_w1, v_mlp_w2, v_final_norm):
    s, d = x.shape[1], x.shape[2]
    depth = mlp_w1.shape[0]
    a_slab = a_w_in.shape[2]
    ff_slab = mlp_w1.shape[2]
    bh = b_w_grp.shape[3]
    my_x, my_y, my_c = _position()
    core = jnp.reshape(my_c, (1,)).astype(jnp.int32)
    chip = jnp.reshape(2 * my_x + my_y, (1,)).astype(jnp.int32)
    device = 4 * my_x + 2 * my_y + my_c

    shards = [a_w_in[0].astype(BF16), a_w_out[0].astype(BF16), b_w_in[0].astype(BF16),
              b_w_grp[0].astype(BF16), b_w_out[0].astype(BF16)]
    shards += [mlp_w1[l].astype(BF16) for l in range(depth)]
    shards += [mlp_w2[l].astype(BF16) for l in range(depth)]
    shards += [b_scale]
    gathered = _all_gather("all_gather_weights", shards)
    wa_in = gathered[0]
    wa_out = gathered[1].reshape(d, d)
    wb_in = gathered[2].reshape(d, d)
    wb_grp = jnp.transpose(gathered[3], (1, 0, 2, 3)).reshape(B_GROUPS, bh, bh)
    wb_out = gathered[4].reshape(d, d)
    w1 = gathered[5:5 + depth]
    w2 = [g.reshape(-1, d) for g in gathered[5 + depth:5 + 2 * depth]]
    scale = gathered[5 + 2 * depth].reshape(1, d)

    h0 = x[0]
    target = loss_target[0]
    ln_g, ln_b = a_ln_g, a_ln_b
    w_s = a_w_s[0]
    b_s_col = a_b_s[0][:, :, None]
    nmix = [norm_mix[l][None, :] for l in range(depth)]
    nmlp = [norm_mlp[l][None, :] for l in range(depth)]

    def mlp_forward(l, h):
        hn = _rms_fwd(f"mlp{l}_norm", h, nmlp[l])
        act, act_sq = _mm_nn(f"mlp{l}_up", hn, w1[l],
                             lambda acc: (jnp.maximum(acc, 0.0), jnp.square(jnp.maximum(acc, 0.0))),
                             (BF16, BF16), slab=True)
        (h_out,) = _mm_nn(f"mlp{l}_down", act_sq, w2[l], lambda acc, res: (acc + res,), (F32,),
                          extras=(h,), extra_kinds=("tile",))
        return h_out, (h, hn, act, act_sq)

    def mlp_backward(l, saved, dh, dhb):
        h, hn, act, act_sq = saved
        (dpre,) = _mm_nt(f"mlp{l}_down_dx", dhb, w2[l], lambda acc, a: (2.0 * a.astype(F32) * acc,),
                         (BF16,), extras=(act,), extra_kinds=("tile",))
        g_w2 = _mm_tn(f"mlp{l}_down_dw", act_sq, dhb)
        g_w1 = _mm_tn(f"mlp{l}_up_dw", hn, dpre, slab_w=ff_slab)
        (dhn,) = _mm_nt(f"mlp{l}_up_dx", dpre, w1[l], lambda acc: (acc,), (F32,), slab=True)
        dh, dhb, g_norm = _rms_bwd(f"mlp{l}_norm_bwd", dhn, h, nmlp[l], dh)
        return dh, dhb, g_w1, g_w2, g_norm

    hn0 = _rms_fwd("mix0_norm", h0, nmix[0])
    (pre,) = _mm_nn("mixa_in", hn0, wa_in, lambda acc: (acc,), (F32,), slab=True)
    gated = _amix_fwd("mixa_gate", pre, ln_g, ln_b, w_s, b_s_col)
    (h1,) = _mm_nn("mixa_out", gated, wa_out, lambda acc, res: (acc + res,), (F32,),
                   extras=(h0,), extra_kinds=("tile",))
    h2, saved_mlp0 = mlp_forward(0, h1)
    hn2 = _rms_fwd("mix1_norm", h2, nmix[1])
    (vb,) = _mm_nn("mixb_in", hn2, wb_in, lambda acc: (acc,), (F32,))
    pooled = _pool("mixb_pool", vb, backward=False)
    tm = _tile(s, 1024)
    mixed, mixed_scaled = _matmul(
        "mixb_grp", pooled, wb_grp, NN, (s // tm, B_GROUPS, 1),
        pl.BlockSpec((tm, bh), lambda i, j, k: (i, j)), pl.BlockSpec((None, bh, bh), lambda i, j, k: (j, 0, 0)),
        [_sds((s, d), BF16), _sds((s, d), BF16)], [pl.BlockSpec((tm, bh), lambda i, j, k: (i, j))] * 2,
        (tm, bh), lambda acc, sc: (acc, acc * sc), (scale,), [pl.BlockSpec((1, bh), lambda i, j, k: (0, j))])
    (h3,) = _mm_nn("mixb_out", mixed_scaled, wb_out, lambda acc, res: (acc + res,), (F32,),
                   extras=(h2,), extra_kinds=("tile",))
    h4, saved_mlp1 = mlp_forward(1, h3)
    dh, dhb, g_final, loss_part = _loss_head("loss_head", h4, final_norm[None, :], target)

    dh, dhb, g_w1_1, g_w2_1, g_nmlp1 = mlp_backward(1, saved_mlp1, dh, dhb)
    dms_scaled, dms_mixed = _mm_nt(
        "mixb_out_dx", dhb, wb_out,
        lambda acc, sc, mx: (acc * sc, acc * mx.astype(F32)), (BF16, F32),
        extras=(scale, mixed), extra_kinds=("row", "tile"))
    g_scale = _colsum("mixb_scale_dw", dms_mixed)
    g_wb_out = _mm_tn("mixb_out_dw", mixed_scaled, dhb)
    (dpooled,) = _matmul(
        "mixb_grp_dx", dms_scaled, wb_grp, NT, (s // tm, B_GROUPS, 1),
        pl.BlockSpec((tm, bh), lambda i, j, k: (i, j)), pl.BlockSpec((None, bh, bh), lambda i, j, k: (j, 0, 0)),
        [_sds((s, d), F32)], [pl.BlockSpec((tm, bh), lambda i, j, k: (i, j))], (tm, bh), lambda acc: (acc,))
    tks = _tile(s, 1024)
    (g_wb_grp,) = _matmul(
        "mixb_grp_dw", pooled, dms_scaled, TN, (1, B_GROUPS, s // tks),
        pl.BlockSpec((tks, bh), lambda i, j, k: (k, j)), pl.BlockSpec((tks, bh), lambda i, j, k: (k, j)),
        [_sds((B_GROUPS, bh, bh), BF16)], [pl.BlockSpec((None, bh, bh), lambda i, j, k: (j, 0, 0))],
        (bh, bh), lambda acc: (acc,))
    dvb = _pool("mixb_pool_bwd", dpooled, backward=True)
    g_wb_in = _mm_tn("mixb_in_dw", hn2, dvb)
    (dhn2,) = _mm_nt("mixb_in_dx", dvb, wb_in, lambda acc: (acc,), (F32,))
    dh, dhb, g_nmix1 = _rms_bwd("mix1_norm_bwd", dhn2, h2, nmix[1], dh)
    dh, dhb, g_w1_0, g_w2_0, g_nmlp0 = mlp_backward(0, saved_mlp0, dh, dhb)
    (dgated,) = _mm_nt("mixa_out_dx", dhb, wa_out, lambda acc: (acc,), (F32,))
    g_wa_out = _mm_tn("mixa_out_dw", gated, dhb)
    dpre, g_ln_g, g_ln_b, g_w_s, g_b_s = _amix_bwd("mixa_gate_bwd", pre, dgated, ln_g, ln_b, w_s, b_s_col)
    g_wa_in = _mm_tn("mixa_in_dw", hn0, dpre, slab_w=a_slab)
    (dhn0,) = _mm_nt("mixa_in_dx", dpre, wa_in, lambda acc: (acc,), (F32,), slab=True)
    grad_x, _, g_nmix0 = _rms_bwd("mix0_norm_bwd", dhn0, h0, nmix[0], dh)

    g_wb_grp_slabs = jnp.transpose(g_wb_grp.reshape(B_GROUPS, N_DEV, bh // N_DEV, bh), (1, 0, 2, 3))
    big = [
        (g_wa_in, d, a_slab, a_w_in, m_a_w_in, v_a_w_in),
        (g_wa_out, d // N_DEV, d, a_w_out, m_a_w_out, v_a_w_out),
        (g_wb_in, d // N_DEV, d, b_w_in, m_b_w_in, v_b_w_in),
        (g_wb_grp_slabs, B_GROUPS * bh // N_DEV, bh, b_w_grp, m_b_w_grp, v_b_w_grp),
        (g_wb_out, d // N_DEV, d, b_w_out, m_b_w_out, v_b_w_out),
    ]
    for l, g_full in enumerate((g_w1_0, g_w1_1)):
        big.append((g_full, d, ff_slab, mlp_w1[l], m_mlp_w1[l], v_mlp_w1[l]))
    for l, g_full in enumerate((g_w2_0, g_w2_1)):
        big.append((g_full, mlp_w2.shape[1], d, mlp_w2[l], m_mlp_w2[l], v_mlp_w2[l]))
    fulls = [g.reshape(N_CHIPS, 2, r, c) for g, r, c, _, _, _ in big]
    from_sibling = _exchange_sibling("grads_to_sibling", fulls)
    partials = [_add_sibling(f"grads_add_sibling_{t}", fulls[t], from_sibling[t], core) for t in range(len(big))]
    from_chips = _exchange_chips("grads_to_chips", partials)
    updated = []
    for t, (_, r, c, w, m, v) in enumerate(big):
        updated.append(_reduce_adam(f"grads_reduce_adam_{t}", partials[t], from_chips[t], chip,
                                    w.reshape(r, c), m.reshape(r, c), v.reshape(r, c)))

    def big_result(t, like):
        return [o.reshape(like.shape) for o in updated[t]]

    r_a_w_in = big_result(0, a_w_in)
    r_a_w_out = big_result(1, a_w_out)
    r_b_w_in = big_result(2, b_w_in)
    r_b_w_grp = big_result(3, b_w_grp)
    r_b_w_out = big_result(4, b_w_out)
    r_mlp_w1 = [jnp.stack([updated[5 + l][o] for l in range(depth)]) for o in range(4)]
    r_mlp_w2 = [jnp.stack([updated[5 + depth + l][o] for l in range(depth)]) for o in range(4)]

    g_norm_mix = jnp.concatenate([g_nmix0, g_nmix1], axis=0)
    g_norm_mlp = jnp.concatenate([g_nmlp0, g_nmlp1], axis=0)
    small_parts = [g_ln_g, g_ln_b, g_w_s, g_b_s, g_norm_mix, g_norm_mlp, g_final, g_scale]
    small_sum = _all_reduce_small("small_grads_all_reduce", _pack(small_parts))
    sg = _unpack(small_sum, [a_ln_g.shape, a_ln_b.shape, a_w_s.shape, a_b_s.shape, norm_mix.shape,
                             norm_mlp.shape, final_norm.shape, (1, d)])
    shard = b_scale.shape[1]
    sg[7] = lax.dynamic_slice(sg[7], (0, device * shard), (1, shard))
    small_w = [a_ln_g, a_ln_b, a_w_s, a_b_s, norm_mix, norm_mlp, final_norm, b_scale]
    small_m = [m_a_ln_g, m_a_ln_b, m_a_w_s, m_a_b_s, m_norm_mix, m_norm_mlp, m_final_norm, m_b_scale]
    small_v = [v_a_ln_g, v_a_ln_b, v_a_w_s, v_a_b_s, v_norm_mix, v_norm_mlp, v_final_norm, v_b_scale]
    small_out = _adam_rows("small_adam", _pack(sg), _pack(small_w), _pack(small_m), _pack(small_v))
    shapes = [w.shape for w in small_w]
    small_res = [sg] + [_unpack(o, shapes) for o in small_out]

    loss = lax.psum(loss_part[0, 0], ("x", "y", "c"))

    def leaf(o):
        return (r_a_w_in[o], small_res[o][0], small_res[o][1], small_res[o][2], small_res[o][3], r_a_w_out[o],
                r_b_w_in[o], r_b_w_grp[o], small_res[o][7], r_b_w_out[o], small_res[o][4], small_res[o][5],
                r_mlp_w1[o], r_mlp_w2[o], small_res[o][6])

    return (loss, grad_x[None], *leaf(0), *leaf(1), *leaf(2), *leaf(3))
```

```python
import math

import jax
import jax.numpy as jnp
from jax import lax
from jax.experimental import pallas as pl
from jax.experimental.pallas import tpu as pltpu

F32 = jnp.float32
BF16 = jnp.bfloat16
MESH = pl.DeviceIdType.MESH

N_DEV = 8
N_CHIPS = 4
CHUNK = 128
A_GROUPS = 8
B_WINDOWS = (2, 4, 8, 16)
B_GROUPS = len(B_WINDOWS)
EPS = 1e-6
ADAM_LR = 0.001
ADAM_B1 = 0.9
ADAM_B2 = 0.999
ADAM_EPS = 1e-08
ADAM_WD = 0.01
ADAM_STEP = 10

VMEM_LIMIT = 48 * 1024 * 1024

NN = (((1,), (0,)), ((), ()))
NT = (((1,), (1,)), ((), ()))
TN = (((0,), (0,)), ((), ()))

_ANY = pl.BlockSpec(memory_space=pl.ANY)
_HBM = pl.BlockSpec(memory_space=pltpu.HBM)
_SEM = pl.BlockSpec(memory_space=pltpu.SEMAPHORE)
_EFFECT = pltpu.SideEffectType.DATAFLOW_SIDE_EFFECTING


def _tile(n, pref):
    return pref if n % pref == 0 else n


def _sds(shape, dtype):
    return jax.ShapeDtypeStruct(shape, dtype)


def _pcall(name, body, operands, in_specs, out_shape, out_specs, *, grid=None, sem=None, scratch=(),
           prefetch=(), after=None, aliases=None):
    after = [] if after is None else [after]
    n_lead = len(prefetch) + len(operands)
    n_after = len(after)

    def wrapped(*refs):
        body(*refs[:n_lead], *refs[n_lead + n_after:])

    in_specs = list(in_specs) + [_ANY] * n_after
    params = pltpu.CompilerParams(vmem_limit_bytes=VMEM_LIMIT) if sem is None else \
        pltpu.CompilerParams(dimension_semantics=sem, vmem_limit_bytes=VMEM_LIMIT)
    kwargs = dict(out_shape=out_shape, scratch_shapes=list(scratch), compiler_params=params, name=name,
                  input_output_aliases=aliases or {})
    if prefetch:
        kwargs["grid_spec"] = pltpu.PrefetchScalarGridSpec(
            num_scalar_prefetch=len(prefetch), grid=grid, in_specs=in_specs, out_specs=out_specs,
            scratch_shapes=list(scratch))
        kwargs.pop("scratch_shapes")
    else:
        kwargs.update(in_specs=in_specs, out_specs=out_specs)
        if grid is not None:
            kwargs["grid"] = grid
    return pl.pallas_call(wrapped, **kwargs)(*prefetch, *operands, *after)


def _matmul(name, a, b, dims, grid, a_spec, b_spec, out_shape, out_specs, acc_shape,
            epilogue, extras=(), extra_specs=(), after=None):
    nk = grid[2]
    n_extra = len(extras)
    n_out = len(out_shape)

    def body(*refs):
        a_ref, b_ref = refs[0], refs[1]
        extra_refs = refs[2:2 + n_extra]
        out_refs = refs[2 + n_extra:2 + n_extra + n_out]

        def finish(acc):
            outs = epilogue(acc, *[r[...] for r in extra_refs])
            for o_ref, o in zip(out_refs, outs):
                o_ref[...] = o.astype(o_ref.dtype)

        part = lax.dot_general(a_ref[...], b_ref[...], dims, preferred_element_type=F32)
        if nk == 1:
            finish(part)
        else:
            acc_ref = refs[-1]
            k = pl.program_id(2)

            @pl.when(k == 0)
            def _():
                acc_ref[...] = part

            @pl.when(k > 0)
            def _():
                acc_ref[...] += part

            @pl.when(k == nk - 1)
            def _():
                finish(acc_ref[...])

    scratch = [] if nk == 1 else [pltpu.VMEM(acc_shape, F32)]
    return _pcall(name, body, [a, b, *extras], [a_spec, b_spec, *extra_specs], out_shape, out_specs,
                  grid=grid, sem=("parallel", "parallel", "arbitrary"), scratch=scratch, after=after)


def _mm_nn(name, a, b, epilogue, out_dtypes, extras=(), extra_kinds=(), slab=False, after=None,
           tm=1024, tn=1024, tk=1024):
    m, kd = a.shape
    if slab:
        n_slab, _, w = b.shape
        n = n_slab * w
        tn = _tile(w, min(tn, w))
        per = w // tn
        tk = _tile(kd, tk)
        b_spec = pl.BlockSpec((None, tk, tn), lambda i, j, k: (j // per, k, j % per))
    else:
        n = b.shape[1]
        tn = _tile(n, tn)
        tk = _tile(kd, tk)
        b_spec = pl.BlockSpec((tk, tn), lambda i, j, k: (k, j))
    tm = _tile(m, tm)
    grid = (m // tm, n // tn, kd // tk)
    a_spec = pl.BlockSpec((tm, tk), lambda i, j, k: (i, k))
    tile_spec = pl.BlockSpec((tm, tn), lambda i, j, k: (i, j))
    row_spec = pl.BlockSpec((1, tn), lambda i, j, k: (0, j))
    extra_specs = [tile_spec if kind == "tile" else row_spec for kind in extra_kinds]
    return _matmul(name, a, b, NN, grid, a_spec, b_spec,
                   [_sds((m, n), d) for d in out_dtypes], [tile_spec for _ in out_dtypes],
                   (tm, tn), epilogue, extras, extra_specs, after=after)


def _mm_nt(name, a, b, epilogue, out_dtypes, extras=(), extra_kinds=(), slab=False, after=None,
           tm=1024, tn=1024, tk=1024):
    m, kd = a.shape
    if slab:
        n_slab, n, w = b.shape
        tk = _tile(w, min(tk, w))
        per = w // tk
        tn = _tile(n, tn)
        b_spec = pl.BlockSpec((None, tn, tk), lambda i, j, k: (k // per, j, k % per))
    else:
        n = b.shape[0]
        tn = _tile(n, tn)
        tk = _tile(kd, tk)
        b_spec = pl.BlockSpec((tn, tk), lambda i, j, k: (j, k))
    tm = _tile(m, tm)
    grid = (m // tm, n // tn, kd // tk)
    a_spec = pl.BlockSpec((tm, tk), lambda i, j, k: (i, k))
    tile_spec = pl.BlockSpec((tm, tn), lambda i, j, k: (i, j))
    row_spec = pl.BlockSpec((1, tn), lambda i, j, k: (0, j))
    extra_specs = [tile_spec if kind == "tile" else row_spec for kind in extra_kinds]
    return _matmul(name, a, b, NT, grid, a_spec, b_spec,
                   [_sds((m, n), d) for d in out_dtypes], [tile_spec for _ in out_dtypes],
                   (tm, tn), epilogue, extras, extra_specs, after=after)


def _mm_tn(name, a, b, slab_w=None, after=None, tm=1024, tn=1024, tk=1024):
    s, m = a.shape
    n = b.shape[1]
    tm = _tile(m, tm)
    tk = _tile(s, tk)
    if slab_w is None:
        tn = _tile(n, tn)
        out_shape = [_sds((m, n), BF16)]
        out_specs = [pl.BlockSpec((tm, tn), lambda i, j, k: (i, j))]
    else:
        tn = _tile(slab_w, min(tn, slab_w))
        per = slab_w // tn
        out_shape = [_sds((n // slab_w, m, slab_w), BF16)]
        out_specs = [pl.BlockSpec((None, tm, tn), lambda i, j, k: (j // per, i, j % per))]
    grid = (m // tm, n // tn, s // tk)
    a_spec = pl.BlockSpec((tk, tm), lambda i, j, k: (k, i))
    b_spec = pl.BlockSpec((tk, tn), lambda i, j, k: (k, j))
    return _matmul(name, a, b, TN, grid, a_spec, b_spec, out_shape, out_specs, (tm, tn),
                   lambda acc: (acc,), after=after)[0]


def _rms_fwd(name, h, g, after=None):
    s, d = h.shape
    tr = _tile(s, 256)

    def body(h_ref, g_ref, o_ref):
        x = h_ref[...]
        r = lax.rsqrt(jnp.mean(x * x, axis=-1, keepdims=True) + EPS)
        o_ref[...] = (x * r * g_ref[...]).astype(o_ref.dtype)

    row = pl.BlockSpec((tr, d), lambda i: (i, 0))
    vec = pl.BlockSpec((1, d), lambda i: (0, 0))
    return _pcall(name, body, [h, g], [row, vec], _sds((s, d), BF16), row, grid=(s // tr,),
                  sem=("parallel",), after=after)


def _accumulate(ref, part, step):
    @pl.when(step == 0)
    def _():
        ref[...] = part

    @pl.when(step > 0)
    def _():
        ref[...] += part


def _rms_bwd(name, dhn, h, g, dres, after=None):
    s, d = h.shape
    tr = _tile(s, 256)

    def body(dhn_ref, h_ref, g_ref, dres_ref, dh_ref, dhb_ref, gp_ref):
        x = h_ref[...]
        r = lax.rsqrt(jnp.mean(x * x, axis=-1, keepdims=True) + EPS)
        n = x * r
        dy = dhn_ref[...]
        dn = dy * g_ref[...]
        dh = dres_ref[...] + r * (dn - n * jnp.mean(dn * n, axis=-1, keepdims=True))
        dh_ref[...] = dh
        dhb_ref[...] = dh.astype(BF16)
        _accumulate(gp_ref, jnp.sum(dy * n, axis=0, keepdims=True), pl.program_id(0))

    row = pl.BlockSpec((tr, d), lambda i: (i, 0))
    vec = pl.BlockSpec((1, d), lambda i: (0, 0))
    return _pcall(name, body, [dhn, h, g, dres], [row, row, vec, row],
                  [_sds((s, d), F32), _sds((s, d), BF16), _sds((1, d), F32)], [row, row, vec],
                  grid=(s // tr,), sem=("arbitrary",), after=after)


def _loss_head(name, h, g, target, after=None):
    s, d = h.shape
    tr = _tile(s, 256)

    def body(h_ref, g_ref, t_ref, dh_ref, dhb_ref, gp_ref, loss_ref):
        x = h_ref[...]
        gg = g_ref[...]
        r = lax.rsqrt(jnp.mean(x * x, axis=-1, keepdims=True) + EPS)
        n = x * r
        e = n * gg - t_ref[...]
        dy = e * (1.0 / d)
        dn = dy * gg
        dh = r * (dn - n * jnp.mean(dn * n, axis=-1, keepdims=True))
        dh_ref[...] = dh
        dhb_ref[...] = dh.astype(BF16)
        step = pl.program_id(0)
        _accumulate(gp_ref, jnp.sum(dy * n, axis=0, keepdims=True), step)
        row_loss = jnp.mean(e * e, axis=-1, keepdims=True)
        _accumulate(loss_ref, 0.5 * jnp.sum(row_loss, axis=0, keepdims=True), step)

    row = pl.BlockSpec((tr, d), lambda i: (i, 0))
    vec = pl.BlockSpec((1, d), lambda i: (0, 0))
    one = pl.BlockSpec((1, 1), lambda i: (0, 0))
    return _pcall(name, body, [h, g, target], [row, vec, row],
                  [_sds((s, d), F32), _sds((s, d), BF16), _sds((1, d), F32), _sds((1, 1), F32)],
                  [row, row, vec, one], grid=(s // tr,), sem=("arbitrary",), after=after)


_SQRT_HALF = math.sqrt(0.5)
_INV_SQRT_2PI = 1.0 / math.sqrt(2.0 * math.pi)


def _gelu(x):
    return 0.5 * x * (1.0 + lax.erf(x * _SQRT_HALF))


def _gelu_grad(x):
    return 0.5 * (1.0 + lax.erf(x * _SQRT_HALF)) + x * jnp.exp(-0.5 * x * x) * _INV_SQRT_2PI


def _causal_mask():
    row = lax.broadcasted_iota(jnp.int32, (CHUNK, CHUNK), 0)
    col = lax.broadcasted_iota(jnp.int32, (CHUNK, CHUNK), 1)
    return row >= col


def _layernorm_parts(v):
    mu = jnp.mean(v, axis=-1, keepdims=True)
    xc = v - mu
    rstd = lax.rsqrt(jnp.mean(xc * xc, axis=-1, keepdims=True) + EPS)
    return xc * rstd, rstd


def _amix_fwd(name, pre, ln_g, ln_b, w_s, b_s_col, after=None):
    s, w2 = pre.shape
    w = w2 // 2
    head = w // A_GROUPS

    def body(pre_ref, g_ref, b_ref, ws_ref, bs_ref, o_ref):
        u = _gelu(pre_ref[:, :w])
        v = _gelu(pre_ref[:, w:])
        vhat, _ = _layernorm_parts(v)
        vn = (vhat * g_ref[...] + b_ref[...]).astype(BF16)
        mask = _causal_mask()
        for grp in range(A_GROUPS):
            cols = slice(grp * head, (grp + 1) * head)
            wm = jnp.where(mask, ws_ref[grp], 0.0).astype(BF16)
            sg = jnp.dot(wm, vn[:, cols], preferred_element_type=F32) + bs_ref[grp]
            o_ref[:, cols] = (u[:, cols] * sg).astype(o_ref.dtype)

    vec = pl.BlockSpec((1, w), lambda i: (0, 0))
    return _pcall(
        name, body, [pre, ln_g, ln_b, w_s, b_s_col],
        [pl.BlockSpec((CHUNK, w2), lambda i: (i, 0)), vec, vec,
         pl.BlockSpec((A_GROUPS, CHUNK, CHUNK), lambda i: (0, 0, 0)),
         pl.BlockSpec((A_GROUPS, CHUNK, 1), lambda i: (0, 0, 0))],
        _sds((s, w), BF16), pl.BlockSpec((CHUNK, w), lambda i: (i, 0)),
        grid=(s // CHUNK,), sem=("parallel",), after=after)


def _amix_bwd(name, pre, dgated, ln_g, ln_b, w_s, b_s_col, after=None):
    s, w2 = pre.shape
    w = w2 // 2
    head = w // A_GROUPS

    def body(pre_ref, dg_ref, g_ref, b_ref, ws_ref, bs_ref, dpre_ref, glg_ref, glb_ref, gws_ref, gbs_ref):
        step = pl.program_id(0)
        pre_u = pre_ref[:, :w]
        pre_v = pre_ref[:, w:]
        u = _gelu(pre_u)
        v = _gelu(pre_v)
        vhat, rstd = _layernorm_parts(v)
        gain = g_ref[...]
        vn = (vhat * gain + b_ref[...]).astype(BF16)
        dgated = dg_ref[...]
        ds = dgated * u
        dsb = ds.astype(BF16)
        mask = _causal_mask()
        du_parts = []
        dvn_parts = []
        for grp in range(A_GROUPS):
            cols = slice(grp * head, (grp + 1) * head)
            wm = jnp.where(mask, ws_ref[grp], 0.0).astype(BF16)
            sg = jnp.dot(wm, vn[:, cols], preferred_element_type=F32) + bs_ref[grp]
            du_parts.append(dgated[:, cols] * sg)
            gws = lax.dot_general(dsb[:, cols], vn[:, cols], NT, preferred_element_type=F32)
            gws = jnp.where(mask, gws, 0.0)
            gbs = jnp.sum(ds[:, cols], axis=-1, keepdims=True)

            @pl.when(step == 0)
            def _():
                gws_ref[grp] = gws
                gbs_ref[grp] = gbs

            @pl.when(step > 0)
            def _():
                gws_ref[grp] += gws
                gbs_ref[grp] += gbs

            dvn_parts.append(lax.dot_general(wm, dsb[:, cols], TN, preferred_element_type=F32))
        du = jnp.concatenate(du_parts, axis=-1)
        dvn = jnp.concatenate(dvn_parts, axis=-1)
        _accumulate(glg_ref, jnp.sum(dvn * vhat, axis=0, keepdims=True), step)
        _accumulate(glb_ref, jnp.sum(dvn, axis=0, keepdims=True), step)
        dvhat = dvn * gain
        dv = rstd * (dvhat - jnp.mean(dvhat, axis=-1, keepdims=True)
                     - vhat * jnp.mean(dvhat * vhat, axis=-1, keepdims=True))
        dpre_ref[:, :w] = (du * _gelu_grad(pre_u)).astype(dpre_ref.dtype)
        dpre_ref[:, w:] = (dv * _gelu_grad(pre_v)).astype(dpre_ref.dtype)

    vec = pl.BlockSpec((1, w), lambda i: (0, 0))
    ws_spec = pl.BlockSpec((A_GROUPS, CHUNK, CHUNK), lambda i: (0, 0, 0))
    bs_spec = pl.BlockSpec((A_GROUPS, CHUNK, 1), lambda i: (0, 0, 0))
    return _pcall(
        name, body, [pre, dgated, ln_g, ln_b, w_s, b_s_col],
        [pl.BlockSpec((CHUNK, w2), lambda i: (i, 0)), pl.BlockSpec((CHUNK, w), lambda i: (i, 0)),
         vec, vec, ws_spec, bs_spec],
        [_sds((s, w2), BF16), _sds((1, w), F32), _sds((1, w), F32),
         _sds((A_GROUPS, CHUNK, CHUNK), F32), _sds((A_GROUPS, CHUNK, 1), F32)],
        [pl.BlockSpec((CHUNK, w2), lambda i: (i, 0)), vec, vec, ws_spec, bs_spec],
        grid=(s // CHUNK,), sem=("arbitrary",), after=after)


def _shift_rows(x, k, forward):
    n = x.shape[0]
    row = lax.broadcasted_iota(jnp.int32, x.shape, 0)
    if forward:
        return jnp.where(row >= k, pltpu.roll(x, k, 0), 0.0)
    return jnp.where(row < n - k, pltpu.roll(x, n - k, 0), 0.0)


def _window_sum(x, window, forward):
    k = 1
    while k < window:
        x = x + _shift_rows(x, k, forward)
        k *= 2
    return x


def _pool(name, v, backward, after=None):
    s, w = v.shape
    head = w // B_GROUPS
    lane = _tile(head, 128)

    def body(v_ref, o_ref):
        grp = pl.program_id(0)
        x = v_ref[...]
        t = lax.broadcasted_iota(jnp.int32, x.shape, 0)
        for idx, window in enumerate(B_WINDOWS):
            @pl.when(grp == idx)
            def _():
                inv_count = 1.0 / jnp.minimum(t + 1, window).astype(F32)
                if backward:
                    out = _window_sum(x * inv_count, window, False) - x
                else:
                    out = _window_sum(x, window, True) * inv_count - x
                o_ref[...] = out.astype(o_ref.dtype)

    per = head // lane
    spec = pl.BlockSpec((s, lane), lambda g, j: (0, g * per + j))
    return _pcall(name, body, [v], [spec], _sds((s, w), BF16), spec, grid=(B_GROUPS, per),
                  sem=("parallel", "parallel"), after=after)


def _colsum(name, a, after=None):
    s, d = a.shape
    tr = _tile(s, 256)

    def body(a_ref, o_ref):
        _accumulate(o_ref, jnp.sum(a_ref[...], axis=0, keepdims=True), pl.program_id(0))

    return _pcall(name, body, [a], [pl.BlockSpec((tr, d), lambda i: (i, 0))], _sds((1, d), F32),
                  pl.BlockSpec((1, d), lambda i: (0, 0)), grid=(s // tr,), sem=("arbitrary",), after=after)


def _adamw(w, g, m, v):
    m = ADAM_B1 * m + (1.0 - ADAM_B1) * g
    v = ADAM_B2 * v + (1.0 - ADAM_B2) * (g * g)
    m_hat = m / (1.0 - ADAM_B1 ** ADAM_STEP)
    v_hat = v / (1.0 - ADAM_B2 ** ADAM_STEP)
    delta = -ADAM_LR * (m_hat / (jnp.sqrt(v_hat) + ADAM_EPS) + ADAM_WD * w)
    return delta, m, v


def _adam_rows(name, g, w, m, v, after=None):
    r, c = g.shape
    tr = _tile(r, 256)

    def body(g_ref, w_ref, m_ref, v_ref, d_ref, nm_ref, nv_ref):
        d_ref[...], nm_ref[...], nv_ref[...] = _adamw(w_ref[...], g_ref[...], m_ref[...], v_ref[...])

    spec = pl.BlockSpec((tr, c), lambda i: (i, 0))
    return _pcall(name, body, [g, w, m, v], [spec] * 4, [_sds((r, c), F32)] * 3, [spec] * 3,
                  grid=(r // tr,), sem=("parallel",), after=after)


def _position():
    return lax.axis_index("x"), lax.axis_index("y"), lax.axis_index("c")


def _other_chips(x, y):
    return [(1 - x, y), (x, 1 - y), (1 - x, 1 - y)]


def _slot(px, py, pc):
    return 4 * px + 2 * py + pc


def _hbm(a):
    return pltpu.with_memory_space_constraint(a, pltpu.HBM)


def _gather_copies(srcs, lands, send_sems, recv_sems):
    x, y, c = _position()
    peers = [(x, y, 1 - c)] + [(*chip, c) for chip in _other_chips(x, y)]
    mine = _slot(x, y, c)
    return [[pltpu.make_async_remote_copy(
        src_ref=srcs[t], dst_ref=lands[t].at[mine], send_sem=send_sems[t].at[k], recv_sem=recv_sems[t].at[k],
        device_id=peer, device_id_type=MESH) for k, peer in enumerate(peers)] for t in range(len(srcs))]


def _gather_start(name, shards):
    n = len(shards)

    def body(*refs):
        srcs, lands = refs[:n], refs[n:2 * n]
        send_sems, recv_sems = refs[2 * n:3 * n], refs[3 * n:4 * n]
        token = refs[-1]
        for per_tensor in _gather_copies(srcs, lands, send_sems, recv_sems):
            for cp in per_tensor:
                cp.start()
        token[...] = jnp.zeros_like(token)

    lands = [lax.empty((N_DEV,) + s.shape, s.dtype) for s in shards]
    out_shape = ([pltpu.SemaphoreType.DMA((4,)) for _ in range(2 * n)]
                 + [pltpu.HBM(s.shape, s.dtype) for s in shards]
                 + [pltpu.HBM(l.shape, l.dtype) for l in lands]
                 + [_sds((8, 128), F32)])
    out = pl.pallas_call(
        body, name=name, out_shape=out_shape, in_specs=[_HBM] * (2 * n),
        out_specs=[_SEM] * (2 * n) + [_HBM] * (2 * n) + [pl.BlockSpec(memory_space=pltpu.VMEM)],
        input_output_aliases={i: 2 * n + i for i in range(2 * n)},
        compiler_params=pltpu.CompilerParams(has_side_effects=_EFFECT),
    )(*[_hbm(s) for s in shards], *[_hbm(l) for l in lands])
    send_sems, recv_sems = out[:n], out[n:2 * n]
    srcs, lands = out[2 * n:3 * n], out[3 * n:4 * n]
    return [(send_sems[t], recv_sems[t], srcs[t], lands[t]) for t in range(n)], out[-1]


def _gather_wait(name, started, after):
    n = len(started)

    def body(*refs):
        srcs, lands = refs[:n], refs[n:2 * n]
        send_sems, recv_sems = refs[2 * n:3 * n], refs[3 * n:4 * n]
        for per_tensor in _gather_copies(srcs, lands, send_sems, recv_sems):
            for cp in per_tensor:
                cp.wait_send()
                cp.wait_recv()

    srcs = [e[2] for e in started]
    lands = [e[3] for e in started]
    out = pl.pallas_call(
        body, name=name, out_shape=[pltpu.HBM(a.shape, a.dtype) for a in srcs + lands],
        in_specs=[_HBM] * (2 * n) + [_SEM] * (2 * n) + [_ANY], out_specs=[_HBM] * (2 * n),
        input_output_aliases={i: i for i in range(2 * n)},
        compiler_params=pltpu.CompilerParams(has_side_effects=_EFFECT),
    )(*srcs, *lands, *[e[0] for e in started], *[e[1] for e in started], after)
    return out[:n], out[n:]


def _gather_finish(name, shards, lands, after):
    n = len(shards)

    def body(*refs):
        srcs, lands_in, outs = refs[:n], refs[n:2 * n], refs[2 * n:3 * n]
        send_sems, recv_sems, local_sems = refs[3 * n:]
        x, y, c = _position()
        local = [pltpu.make_async_copy(srcs[t], outs[t].at[_slot(x, y, c)], local_sems.at[t]) for t in range(n)]
        copies = []
        for t in range(n):
            for j, chip in enumerate(_other_chips(x, y)):
                block = outs[t].at[_slot(*chip, c)]
                copies.append(pltpu.make_async_remote_copy(
                    src_ref=block, dst_ref=block, send_sem=send_sems.at[t, j], recv_sem=recv_sems.at[t, j],
                    device_id=(x, y, 1 - c), device_id_type=MESH))
        for cp in local + copies:
            cp.start()
        for cp in copies:
            cp.wait_send()
        for t in range(n):
            for j, chip in enumerate(_other_chips(x, y)):
                block = outs[t].at[_slot(*chip, 1 - c)]
                pltpu.make_async_remote_copy(
                    src_ref=block, dst_ref=block, send_sem=send_sems.at[t, j], recv_sem=recv_sems.at[t, j],
                    device_id=(x, y, 1 - c), device_id_type=MESH).wait_recv()
        for cp in local:
            cp.wait()

    return _pcall(name, body, [*shards, *lands], [_ANY] * (2 * n),
                  [_sds(l.shape, l.dtype) for l in lands], [_ANY] * n,
                  scratch=[pltpu.SemaphoreType.DMA((n, 3)), pltpu.SemaphoreType.DMA((n, 3)),
                           pltpu.SemaphoreType.DMA((n,))],
                  after=after, aliases={n + t: t for t in range(n)})


def _exchange_sibling(name, fulls, after):
    n = len(fulls)

    def body(*refs):
        src = refs[:n]
        out = refs[n:2 * n]
        send_sems, recv_sems = refs[2 * n:]
        x, y, c = _position()
        copies = [pltpu.make_async_remote_copy(
            src_ref=src[t].at[:, 1 - c], dst_ref=out[t], send_sem=send_sems.at[t], recv_sem=recv_sems.at[t],
            device_id=(x, y, 1 - c), device_id_type=MESH) for t in range(n)]
        for cp in copies:
            cp.start()
        for cp in copies:
            cp.wait()

    return _pcall(name, body, fulls, [_ANY] * n, [_sds((N_CHIPS,) + f.shape[2:], f.dtype) for f in fulls],
                  [_ANY] * n, scratch=[pltpu.SemaphoreType.DMA((n,)), pltpu.SemaphoreType.DMA((n,))],
                  after=after)


def _add_sibling(name, full, recv, core, after):
    _, _, r, c = full.shape
    tr = _tile(r, max(8, (256 * 1024) // c))

    def body(core_ref, f_ref, r_ref, o_ref):
        o_ref[...] = (f_ref[...].astype(F32) + r_ref[...].astype(F32)).astype(o_ref.dtype)

    return _pcall(
        name, body, [full, recv],
        [pl.BlockSpec((None, None, tr, c), lambda p, i, core_ref: (p, core_ref[0], i, 0)),
         pl.BlockSpec((None, tr, c), lambda p, i, core_ref: (p, i, 0))],
        _sds((N_CHIPS, r, c), BF16), pl.BlockSpec((None, tr, c), lambda p, i, core_ref: (p, i, 0)),
        grid=(N_CHIPS, r // tr), sem=("parallel", "parallel"), prefetch=[core], after=after)


def _scatter_copies(srcs, lands, send_sems, recv_sems):
    x, y, c = _position()
    return [[pltpu.make_async_remote_copy(
        src_ref=srcs[t].at[2 * px + py], dst_ref=lands[t].at[j],
        send_sem=send_sems[t].at[j], recv_sem=recv_sems[t].at[j],
        device_id=(px, py, c), device_id_type=MESH) for j, (px, py) in enumerate(_other_chips(x, y))]
        for t in range(len(srcs))]


def _scatter_start(name, partials, after):
    n = len(partials)

    def body(*refs):
        srcs, lands = refs[:n], refs[n:2 * n]
        send_sems, recv_sems = refs[2 * n + 1:3 * n + 1], refs[3 * n + 1:4 * n + 1]
        token = refs[-1]
        for per_tensor in _scatter_copies(srcs, lands, send_sems, recv_sems):
            for cp in per_tensor:
                cp.start()
        token[...] = jnp.zeros_like(token)

    lands = [lax.empty((N_CHIPS - 1,) + p.shape[1:], p.dtype) for p in partials]
    out_shape = ([pltpu.SemaphoreType.DMA((3,)) for _ in range(2 * n)]
                 + [pltpu.HBM(p.shape, p.dtype) for p in partials]
                 + [pltpu.HBM(l.shape, l.dtype) for l in lands]
                 + [_sds((8, 128), F32)])
    out = pl.pallas_call(
        body, name=name, out_shape=out_shape, in_specs=[_HBM] * (2 * n) + [_ANY],
        out_specs=[_SEM] * (2 * n) + [_HBM] * (2 * n) + [pl.BlockSpec(memory_space=pltpu.VMEM)],
        input_output_aliases={i: 2 * n + i for i in range(2 * n)},
        compiler_params=pltpu.CompilerParams(has_side_effects=_EFFECT),
    )(*[_hbm(p) for p in partials], *[_hbm(l) for l in lands], after)
    send_sems, recv_sems = out[:n], out[n:2 * n]
    srcs, lands = out[2 * n:3 * n], out[3 * n:4 * n]
    return [(send_sems[t], recv_sems[t], srcs[t], lands[t]) for t in range(n)], out[-1]


def _scatter_wait(name, started, after):
    n = len(started)

    def body(*refs):
        srcs, lands = refs[:n], refs[n:2 * n]
        send_sems, recv_sems = refs[2 * n:3 * n], refs[3 * n:4 * n]
        for per_tensor in _scatter_copies(srcs, lands, send_sems, recv_sems):
            for cp in per_tensor:
                cp.wait_send()
                cp.wait_recv()

    srcs = [e[2] for e in started]
    lands = [e[3] for e in started]
    out = pl.pallas_call(
        body, name=name, out_shape=[pltpu.HBM(a.shape, a.dtype) for a in srcs + lands],
        in_specs=[_HBM] * (2 * n) + [_SEM] * (2 * n) + [_ANY], out_specs=[_HBM] * (2 * n),
        input_output_aliases={i: i for i in range(2 * n)},
        compiler_params=pltpu.CompilerParams(has_side_effects=_EFFECT),
    )(*srcs, *lands, *[e[0] for e in started], *[e[1] for e in started], after)
    return out[:n], out[n:]


def _reduce_adam(name, partial, recv, chip, w, m, v, layer, carried, after):
    n_layers, r, c = w.shape
    tr = _tile(r, max(8, (128 * 1024) // c))

    def body(chip_ref, p_ref, r_ref, w_ref, m_ref, v_ref, *rest):
        g_ref, d_ref, nm_ref, nv_ref = rest[-4:]
        g = p_ref[...].astype(F32)
        for j in range(N_CHIPS - 1):
            g = g + r_ref[j].astype(F32)
        g_ref[...] = g
        d_ref[...], nm_ref[...], nv_ref[...] = _adamw(w_ref[...], g, m_ref[...], v_ref[...])

    layered = pl.BlockSpec((None, tr, c), lambda i, chip_ref: (layer, i, 0))
    in_specs = [pl.BlockSpec((None, tr, c), lambda i, chip_ref: (chip_ref[0], i, 0)),
                pl.BlockSpec((N_CHIPS - 1, tr, c), lambda i, chip_ref: (0, i, 0)),
                layered, layered, layered]
    operands = [partial, recv, w, m, v]
    aliases = {}
    if carried is not None:
        operands += list(carried)
        in_specs += [_ANY] * 4
        aliases = {1 + 5 + o: o for o in range(4)}
    return _pcall(name, body, operands, in_specs, [_sds((n_layers, r, c), F32)] * 4, [layered] * 4,
                  grid=(r // tr,), sem=("parallel",), prefetch=[chip], after=after, aliases=aliases)


def _all_reduce_small(name, packed, after):
    r, lanes = packed.shape

    def body(in_ref, out_ref, gather_ref, send_sems, recv_sems):
        x, y, c = _position()
        me = _slot(x, y, c)
        gather_ref[me] = in_ref[...]
        copies = []
        for k in range(1, N_DEV):
            fx, fy, fc = (k >> 2) & 1, (k >> 1) & 1, k & 1
            peer = (x ^ fx, y ^ fy, c ^ fc)
            copies.append(pltpu.make_async_remote_copy(
                src_ref=in_ref, dst_ref=gather_ref.at[me],
                send_sem=send_sems.at[k - 1], recv_sem=recv_sems.at[k - 1],
                device_id=peer, device_id_type=MESH))
        for cp in copies:
            cp.start()
        for cp in copies:
            cp.wait()
        total = gather_ref[0]
        for d in range(1, N_DEV):
            total = total + gather_ref[d]
        out_ref[...] = total

    vmem = pl.BlockSpec(memory_space=pltpu.VMEM)
    return _pcall(name, body, [packed], [vmem], _sds((r, lanes), F32), vmem,
                  scratch=[pltpu.VMEM((N_DEV, r, lanes), F32), pltpu.SemaphoreType.DMA((N_DEV - 1,)),
                           pltpu.SemaphoreType.DMA((N_DEV - 1,))], after=after)


def _pack(arrays):
    return jnp.concatenate([a.reshape(-1, 128) for a in arrays], axis=0)


def _unpack(packed, shapes):
    out, row = [], 0
    for shape in shapes:
        rows = math.prod(shape) // 128
        out.append(packed[row:row + rows].reshape(shape))
        row += rows
    return out


class _Order:
    def __init__(self):
        self.last = None

    def __call__(self, fn, *args, **kwargs):
        out = fn(*args, after=self.last, **kwargs)
        self.last = out[0] if isinstance(out, (list, tuple)) else out
        return out


def kernel(x, a_w_in, a_ln_g, a_ln_b, a_w_s, a_b_s, a_w_out, b_w_in, b_w_grp, b_scale, b_w_out, norm_mix, norm_mlp, mlp_w1, mlp_w2, final_norm, loss_target, m_a_w_in, m_a_ln_g, m_a_ln_b, m_a_w_s, m_a_b_s, m_a_w_out, m_b_w_in, m_b_w_grp, m_b_scale, m_b_w_out, m_norm_mix, m_norm_mlp, m_mlp_w1, m_mlp_w2, m_final_norm, v_a_w_in, v_a_ln_g, v_a_ln_b, v_a_w_s, v_a_b_s, v_a_w_out, v_b_w_in, v_b_w_grp, v_b_scale, v_b_w_out, v_norm_mix, v_norm_mlp, v_mlp_w1, v_mlp_w2, v_final_norm):
    s, d = x.shape[1], x.shape[2]
    depth = mlp_w1.shape[0]
    a_slab = a_w_in.shape[2]
    ff_slab = mlp_w1.shape[2]
    ff_rows = mlp_w2.shape[1]
    bh = b_w_grp.shape[3]
    my_x, my_y, my_c = _position()
    core = jnp.reshape(my_c, (1,)).astype(jnp.int32)
    chip = jnp.reshape(2 * my_x + my_y, (1,)).astype(jnp.int32)
    device = _slot(my_x, my_y, my_c)
    run = _Order()

    w1_b, w2_b = mlp_w1.astype(BF16), mlp_w2.astype(BF16)
    shards = [a_w_in[0].astype(BF16), a_w_out[0].astype(BF16), b_scale,
              w1_b[0], w2_b[0],
              b_w_in[0].astype(BF16), b_w_grp[0].astype(BF16), b_w_out[0].astype(BF16),
              w1_b[1], w2_b[1]]
    started, token = _gather_start("weights_gather_start", shards)
    run.last = token

    def gathered(name, indices):
        srcs, lands = _gather_wait(name + "_wait", [started[t] for t in indices], run.last)
        run.last = lands[0]
        return run(_gather_finish, name + "_finish", srcs, lands)

    h0 = x[0]
    target = loss_target[0]
    ln_g, ln_b = a_ln_g, a_ln_b
    w_s = a_w_s[0]
    b_s_col = a_b_s[0][:, :, None]
    nmix = [norm_mix[l][None, :] for l in range(depth)]
    nmlp = [norm_mlp[l][None, :] for l in range(depth)]

    def mlp_forward(l, h, gather_indices):
        hn = run(_rms_fwd, f"mlp{l}_norm", h, nmlp[l])
        (w1,) = gathered(f"mlp{l}_up_weights", gather_indices[:1])
        act, act_sq = run(_mm_nn, f"mlp{l}_up", hn, w1,
                          lambda acc: (jnp.maximum(acc, 0.0), jnp.square(jnp.maximum(acc, 0.0))),
                          (BF16, BF16), slab=True)
        (w2,) = gathered(f"mlp{l}_down_weights", gather_indices[1:])
        w2 = w2.reshape(-1, d)
        (h_out,) = run(_mm_nn, f"mlp{l}_down", act_sq, w2, lambda acc, res: (acc + res,), (F32,),
                       extras=(h,), extra_kinds=("tile",))
        return h_out, (h, hn, act, act_sq, w1, w2)

    scattered = []

    def scatter(name, group):
        fulls = [g.reshape(N_CHIPS, 2, r, c) for g, r, c, _ in group]
        from_sibling = run(_exchange_sibling, name + "_to_sibling", fulls)
        partials = [run(_add_sibling, f"{name}_add_sibling_{t}", fulls[t], from_sibling[t], core)
                    for t in range(len(group))]
        in_flight, tok = _scatter_start(name + "_scatter_start", partials, run.last)
        run.last = tok
        scattered.append((name, in_flight, [spec for _, _, _, spec in group]))

    def mlp_backward(l, saved, dh, dhb):
        h, hn, act, act_sq, w1, w2 = saved
        g_w2 = run(_mm_tn, f"mlp{l}_down_dw", act_sq, dhb)
        scatter(f"mlp{l}_down_grads", [(g_w2, ff_rows, d, ("mlp_w2", l))])
        (dpre,) = run(_mm_nt, f"mlp{l}_down_dx", dhb, w2, lambda acc, a: (2.0 * a.astype(F32) * acc,),
                      (BF16,), extras=(act,), extra_kinds=("tile",))
        g_w1 = run(_mm_tn, f"mlp{l}_up_dw", hn, dpre, slab_w=ff_slab)
        scatter(f"mlp{l}_up_grads", [(g_w1, d, ff_slab, ("mlp_w1", l))])
        (dhn,) = run(_mm_nt, f"mlp{l}_up_dx", dpre, w1, lambda acc: (acc,), (F32,), slab=True)
        dh, dhb, g_norm = run(_rms_bwd, f"mlp{l}_norm_bwd", dhn, h, nmlp[l], dh)
        return dh, dhb, g_norm

    hn0 = run(_rms_fwd, "mix0_norm", h0, nmix[0])
    (wa_in,) = gathered("mixa_in_weights", [0])
    (pre,) = run(_mm_nn, "mixa_in", hn0, wa_in, lambda acc: (acc,), (F32,), slab=True)
    wa_out, scale = gathered("mixa_out_weights", [1, 2])
    wa_out, scale = wa_out.reshape(d, d), scale.reshape(1, d)
    gated = run(_amix_fwd, "mixa_gate", pre, ln_g, ln_b, w_s, b_s_col)
    (h1,) = run(_mm_nn, "mixa_out", gated, wa_out, lambda acc, res: (acc + res,), (F32,),
                extras=(h0,), extra_kinds=("tile",))
    h2, saved_mlp0 = mlp_forward(0, h1, [3, 4])
    hn2 = run(_rms_fwd, "mix1_norm", h2, nmix[1])
    wb_in, wb_grp, wb_out = gathered("mixb_weights", [5, 6, 7])
    wb_in, wb_out = wb_in.reshape(d, d), wb_out.reshape(d, d)
    wb_grp = jnp.transpose(wb_grp, (1, 0, 2, 3)).reshape(B_GROUPS, bh, bh)
    (vb,) = run(_mm_nn, "mixb_in", hn2, wb_in, lambda acc: (acc,), (F32,))
    pooled = run(_pool, "mixb_pool", vb, backward=False)
    tm = _tile(s, 1024)
    grp_tile = pl.BlockSpec((tm, bh), lambda i, j, k: (i, j))
    grp_weight = pl.BlockSpec((None, bh, bh), lambda i, j, k: (j, 0, 0))
    mixed, mixed_scaled = run(
        _matmul, "mixb_grp", pooled, wb_grp, NN, (s // tm, B_GROUPS, 1), grp_tile, grp_weight,
        [_sds((s, d), BF16), _sds((s, d), BF16)], [grp_tile] * 2,
        (tm, bh), lambda acc, sc: (acc, acc * sc), (scale,), [pl.BlockSpec((1, bh), lambda i, j, k: (0, j))])
    (h3,) = run(_mm_nn, "mixb_out", mixed_scaled, wb_out, lambda acc, res: (acc + res,), (F32,),
                extras=(h2,), extra_kinds=("tile",))
    h4, saved_mlp1 = mlp_forward(1, h3, [8, 9])
    dh, dhb, g_final, loss_part = run(_loss_head, "loss_head", h4, final_norm[None, :], target)

    dh, dhb, g_nmlp1 = mlp_backward(1, saved_mlp1, dh, dhb)
    g_wb_out = run(_mm_tn, "mixb_out_dw", mixed_scaled, dhb)
    dms_scaled, dms_mixed = run(
        _mm_nt, "mixb_out_dx", dhb, wb_out,
        lambda acc, sc, mx: (acc * sc, acc * mx.astype(F32)), (BF16, F32),
        extras=(scale, mixed), extra_kinds=("row", "tile"))
    g_scale = run(_colsum, "mixb_scale_dw", dms_mixed)
    tks = _tile(s, 1024)
    grp_rows = pl.BlockSpec((tks, bh), lambda i, j, k: (k, j))
    (g_wb_grp,) = run(
        _matmul, "mixb_grp_dw", pooled, dms_scaled, TN, (1, B_GROUPS, s // tks), grp_rows, grp_rows,
        [_sds((B_GROUPS, bh, bh), BF16)], [grp_weight], (bh, bh), lambda acc: (acc,))
    (dpooled,) = run(
        _matmul, "mixb_grp_dx", dms_scaled, wb_grp, NT, (s // tm, B_GROUPS, 1), grp_tile, grp_weight,
        [_sds((s, d), F32)], [grp_tile], (tm, bh), lambda acc: (acc,))
    dvb = run(_pool, "mixb_pool_bwd", dpooled, backward=True)
    g_wb_in = run(_mm_tn, "mixb_in_dw", hn2, dvb)
    g_wb_grp_slabs = jnp.transpose(g_wb_grp.reshape(B_GROUPS, N_DEV, bh // N_DEV, bh), (1, 0, 2, 3))
    scatter("mixb_grads", [(g_wb_out, d // N_DEV, d, ("b_w_out", 0)),
                           (g_wb_grp_slabs, B_GROUPS * bh // N_DEV, bh, ("b_w_grp", 0)),
                           (g_wb_in, d // N_DEV, d, ("b_w_in", 0))])
    (dhn2,) = run(_mm_nt, "mixb_in_dx", dvb, wb_in, lambda acc: (acc,), (F32,))
    dh, dhb, g_nmix1 = run(_rms_bwd, "mix1_norm_bwd", dhn2, h2, nmix[1], dh)
    dh, dhb, g_nmlp0 = mlp_backward(0, saved_mlp0, dh, dhb)
    g_wa_out = run(_mm_tn, "mixa_out_dw", gated, dhb)
    (dgated,) = run(_mm_nt, "mixa_out_dx", dhb, wa_out, lambda acc: (acc,), (F32,))
    dpre, g_ln_g, g_ln_b, g_w_s, g_b_s = run(_amix_bwd, "mixa_gate_bwd", pre, dgated, ln_g, ln_b, w_s, b_s_col)
    g_wa_in = run(_mm_tn, "mixa_in_dw", hn0, dpre, slab_w=a_slab)
    scatter("mixa_grads", [(g_wa_in, d, a_slab, ("a_w_in", 0)), (g_wa_out, d // N_DEV, d, ("a_w_out", 0))])
    (dhn0,) = run(_mm_nt, "mixa_in_dx", dpre, wa_in, lambda acc: (acc,), (F32,), slab=True)
    grad_x, _, g_nmix0 = run(_rms_bwd, "mix0_norm_bwd", dhn0, h0, nmix[0], dh)

    g_norm_mix = jnp.concatenate([g_nmix0, g_nmix1], axis=0)
    g_norm_mlp = jnp.concatenate([g_nmlp0, g_nmlp1], axis=0)
    small_parts = [g_ln_g, g_ln_b, g_w_s, g_b_s, g_norm_mix, g_norm_mlp, g_final, g_scale]
    small_sum = run(_all_reduce_small, "small_grads_all_reduce", _pack(small_parts))
    sg = _unpack(small_sum, [a_ln_g.shape, a_ln_b.shape, a_w_s.shape, a_b_s.shape, norm_mix.shape,
                             norm_mlp.shape, final_norm.shape, (1, d)])
    shard = b_scale.shape[1]
    sg[7] = lax.dynamic_slice(sg[7], (0, device * shard), (1, shard))
    small_w = [a_ln_g, a_ln_b, a_w_s, a_b_s, norm_mix, norm_mlp, final_norm, b_scale]
    small_m = [m_a_ln_g, m_a_ln_b, m_a_w_s, m_a_b_s, m_norm_mix, m_norm_mlp, m_final_norm, m_b_scale]
    small_v = [v_a_ln_g, v_a_ln_b, v_a_w_s, v_a_b_s, v_norm_mix, v_norm_mlp, v_final_norm, v_b_scale]
    small_out = run(_adam_rows, "small_adam", _pack(sg), _pack(small_w), _pack(small_m), _pack(small_v))
    shapes = [w.shape for w in small_w]
    small_res = [sg] + [_unpack(o, shapes) for o in small_out]

    weights = {"a_w_in": (a_w_in, m_a_w_in, v_a_w_in), "a_w_out": (a_w_out, m_a_w_out, v_a_w_out),
               "b_w_in": (b_w_in, m_b_w_in, v_b_w_in), "b_w_grp": (b_w_grp, m_b_w_grp, v_b_w_grp),
               "b_w_out": (b_w_out, m_b_w_out, v_b_w_out), "mlp_w1": (mlp_w1, m_mlp_w1, v_mlp_w1),
               "mlp_w2": (mlp_w2, m_mlp_w2, v_mlp_w2)}
    results = {}
    for name, in_flight, specs in scattered:
        partials, lands = _scatter_wait(name + "_scatter_wait", in_flight, run.last)
        run.last = lands[0]
        for t, (wname, layer) in enumerate(specs):
            w, m, v = weights[wname]
            _, r, c = partials[t].shape
            layers = w.shape[0]
            results[wname] = run(_reduce_adam, f"{name}_reduce_adam_{t}", partials[t], lands[t], chip,
                                 w.reshape(layers, r, c), m.reshape(layers, r, c), v.reshape(layers, r, c),
                                 layer, results.get(wname))
    big = {wname: [o.reshape(weights[wname][0].shape) for o in outs] for wname, outs in results.items()}

    loss = lax.psum(loss_part[0, 0], ("x", "y", "c"))

    def leaf(o):
        return (big["a_w_in"][o], small_res[o][0], small_res[o][1], small_res[o][2], small_res[o][3],
                big["a_w_out"][o], big["b_w_in"][o], big["b_w_grp"][o], small_res[o][7], big["b_w_out"][o],
                small_res[o][4], small_res[o][5], big["mlp_w1"][o], big["mlp_w2"][o], small_res[o][6])

    return (loss, grad_x[None], *leaf(0), *leaf(1), *leaf(2), *leaf(3))
```

```python
import math

import jax
import jax.numpy as jnp
from jax import lax
from jax.experimental import pallas as pl
from jax.experimental.pallas import tpu as pltpu

F32 = jnp.float32
BF16 = jnp.bfloat16
MESH = pl.DeviceIdType.MESH

N_DEV = 8
N_CHIPS = 4
CHUNK = 128
A_GROUPS = 8
B_WINDOWS = (2, 4, 8, 16)
B_GROUPS = len(B_WINDOWS)
EPS = 1e-6
ADAM_LR = 0.001
ADAM_B1 = 0.9
ADAM_B2 = 0.999
ADAM_EPS = 1e-08
ADAM_WD = 0.01
ADAM_STEP = 10

VMEM_LIMIT = 48 * 1024 * 1024

NN = (((1,), (0,)), ((), ()))
NT = (((1,), (1,)), ((), ()))
TN = (((0,), (0,)), ((), ()))

_ANY = pl.BlockSpec(memory_space=pl.ANY)
_HBM = pl.BlockSpec(memory_space=pltpu.HBM)
_SEM = pl.BlockSpec(memory_space=pltpu.SEMAPHORE)
_EFFECT = pltpu.SideEffectType.DATAFLOW_SIDE_EFFECTING


def _tile(n, pref):
    return pref if n % pref == 0 else n


def _sds(shape, dtype):
    return jax.ShapeDtypeStruct(shape, dtype)


def _pcall(name, body, operands, in_specs, out_shape, out_specs, *, grid=None, sem=None, scratch=(),
           prefetch=(), after=None, aliases=None):
    after = [] if after is None else [after]
    n_lead = len(prefetch) + len(operands)
    n_after = len(after)

    def wrapped(*refs):
        body(*refs[:n_lead], *refs[n_lead + n_after:])

    in_specs = list(in_specs) + [_ANY] * n_after
    params = pltpu.CompilerParams(vmem_limit_bytes=VMEM_LIMIT) if sem is None else \
        pltpu.CompilerParams(dimension_semantics=sem, vmem_limit_bytes=VMEM_LIMIT)
    kwargs = dict(out_shape=out_shape, scratch_shapes=list(scratch), compiler_params=params, name=name,
                  input_output_aliases=aliases or {})
    if prefetch:
        kwargs["grid_spec"] = pltpu.PrefetchScalarGridSpec(
            num_scalar_prefetch=len(prefetch), grid=grid, in_specs=in_specs, out_specs=out_specs,
            scratch_shapes=list(scratch))
        kwargs.pop("scratch_shapes")
    else:
        kwargs.update(in_specs=in_specs, out_specs=out_specs)
        if grid is not None:
            kwargs["grid"] = grid
    return pl.pallas_call(wrapped, **kwargs)(*prefetch, *operands, *after)


def _matmul(name, a, b, dims, grid, a_spec, b_spec, out_shape, out_specs, acc_shape,
            epilogue, extras=(), extra_specs=(), after=None):
    nk = grid[2]
    n_extra = len(extras)
    n_out = len(out_shape)

    def body(*refs):
        a_ref, b_ref = refs[0], refs[1]
        extra_refs = refs[2:2 + n_extra]
        out_refs = refs[2 + n_extra:2 + n_extra + n_out]

        def finish(acc):
            outs = epilogue(acc, *[r[...] for r in extra_refs])
            for o_ref, o in zip(out_refs, outs):
                o_ref[...] = o.astype(o_ref.dtype)

        def product():
            return lax.dot_general(a_ref[...], b_ref[...], dims, preferred_element_type=F32)

        if nk == 1:
            finish(product())
        else:
            acc_ref = refs[-1]
            k = pl.program_id(2)

            @pl.when(k == 0)
            def _():
                acc_ref[...] = product()

            if nk > 2:
                @pl.when(jnp.logical_and(k > 0, k < nk - 1))
                def _():
                    acc_ref[...] += product()

            @pl.when(k == nk - 1)
            def _():
                finish(acc_ref[...] + product())

    scratch = [] if nk == 1 else [pltpu.VMEM(acc_shape, F32)]
    return _pcall(name, body, [a, b, *extras], [a_spec, b_spec, *extra_specs], out_shape, out_specs,
                  grid=grid, sem=("parallel", "parallel", "arbitrary"), scratch=scratch, after=after)


def _mm_nn(name, a, b, epilogue, out_dtypes, extras=(), extra_kinds=(), slab=False, after=None,
           tm=1024, tn=1024, tk=2048):
    m, kd = a.shape
    if slab:
        n_slab, _, w = b.shape
        n = n_slab * w
        tn = _tile(w, min(tn, w))
        per = w // tn
        tk = _tile(kd, tk)
        b_spec = pl.BlockSpec((None, tk, tn), lambda i, j, k: (j // per, k, j % per))
    else:
        n = b.shape[1]
        tn = _tile(n, tn)
        tk = _tile(kd, tk)
        b_spec = pl.BlockSpec((tk, tn), lambda i, j, k: (k, j))
    tm = _tile(m, tm)
    grid = (m // tm, n // tn, kd // tk)
    a_spec = pl.BlockSpec((tm, tk), lambda i, j, k: (i, k))
    tile_spec = pl.BlockSpec((tm, tn), lambda i, j, k: (i, j))
    row_spec = pl.BlockSpec((1, tn), lambda i, j, k: (0, j))
    extra_specs = [tile_spec if kind == "tile" else row_spec for kind in extra_kinds]
    return _matmul(name, a, b, NN, grid, a_spec, b_spec,
                   [_sds((m, n), d) for d in out_dtypes], [tile_spec for _ in out_dtypes],
                   (tm, tn), epilogue, extras, extra_specs, after=after)


def _mm_nt(name, a, b, epilogue, out_dtypes, extras=(), extra_kinds=(), slab=False, after=None,
           tm=1024, tn=1024, tk=2048):
    m, kd = a.shape
    if slab:
        n_slab, n, w = b.shape
        tk = _tile(w, min(tk, w))
        per = w // tk
        tn = _tile(n, tn)
        b_spec = pl.BlockSpec((None, tn, tk), lambda i, j, k: (k // per, j, k % per))
    else:
        n = b.shape[0]
        tn = _tile(n, tn)
        tk = _tile(kd, tk)
        b_spec = pl.BlockSpec((tn, tk), lambda i, j, k: (j, k))
    tm = _tile(m, tm)
    grid = (m // tm, n // tn, kd // tk)
    a_spec = pl.BlockSpec((tm, tk), lambda i, j, k: (i, k))
    tile_spec = pl.BlockSpec((tm, tn), lambda i, j, k: (i, j))
    row_spec = pl.BlockSpec((1, tn), lambda i, j, k: (0, j))
    extra_specs = [tile_spec if kind == "tile" else row_spec for kind in extra_kinds]
    return _matmul(name, a, b, NT, grid, a_spec, b_spec,
                   [_sds((m, n), d) for d in out_dtypes], [tile_spec for _ in out_dtypes],
                   (tm, tn), epilogue, extras, extra_specs, after=after)


def _mm_tn(name, a, b, slab_w=None, after=None, tm=1024, tn=1024, tk=2048):
    s, m = a.shape
    n = b.shape[1]
    tm = _tile(m, tm)
    tk = _tile(s, tk)
    if slab_w is None:
        tn = _tile(n, tn)
        out_shape = [_sds((m, n), BF16)]
        out_specs = [pl.BlockSpec((tm, tn), lambda i, j, k: (i, j))]
    else:
        tn = _tile(slab_w, min(tn, slab_w))
        per = slab_w // tn
        out_shape = [_sds((n // slab_w, m, slab_w), BF16)]
        out_specs = [pl.BlockSpec((None, tm, tn), lambda i, j, k: (j // per, i, j % per))]
    grid = (m // tm, n // tn, s // tk)
    a_spec = pl.BlockSpec((tk, tm), lambda i, j, k: (k, i))
    b_spec = pl.BlockSpec((tk, tn), lambda i, j, k: (k, j))
    return _matmul(name, a, b, TN, grid, a_spec, b_spec, out_shape, out_specs, (tm, tn),
                   lambda acc: (acc,), after=after)[0]


def _rms_fwd(name, h, g, after=None):
    s, d = h.shape
    tr = _tile(s, 256)

    def body(h_ref, g_ref, o_ref):
        x = h_ref[...]
        r = lax.rsqrt(jnp.mean(x * x, axis=-1, keepdims=True) + EPS)
        o_ref[...] = (x * r * g_ref[...]).astype(o_ref.dtype)

    row = pl.BlockSpec((tr, d), lambda i: (i, 0))
    vec = pl.BlockSpec((1, d), lambda i: (0, 0))
    return _pcall(name, body, [h, g], [row, vec], _sds((s, d), BF16), row, grid=(s // tr,),
                  sem=("parallel",), after=after)


def _accumulate(ref, part, step):
    @pl.when(step == 0)
    def _():
        ref[...] = part

    @pl.when(step > 0)
    def _():
        ref[...] += part


def _rms_bwd(name, dhn, h, g, dres, after=None):
    s, d = h.shape
    tr = _tile(s, 256)

    def body(dhn_ref, h_ref, g_ref, dres_ref, dh_ref, dhb_ref, gp_ref):
        x = h_ref[...]
        r = lax.rsqrt(jnp.mean(x * x, axis=-1, keepdims=True) + EPS)
        n = x * r
        dy = dhn_ref[...]
        dn = dy * g_ref[...]
        dh = dres_ref[...] + r * (dn - n * jnp.mean(dn * n, axis=-1, keepdims=True))
        dh_ref[...] = dh
        dhb_ref[...] = dh.astype(BF16)
        _accumulate(gp_ref, jnp.sum(dy * n, axis=0, keepdims=True), pl.program_id(0))

    row = pl.BlockSpec((tr, d), lambda i: (i, 0))
    vec = pl.BlockSpec((1, d), lambda i: (0, 0))
    return _pcall(name, body, [dhn, h, g, dres], [row, row, vec, row],
                  [_sds((s, d), F32), _sds((s, d), BF16), _sds((1, d), F32)], [row, row, vec],
                  grid=(s // tr,), sem=("arbitrary",), after=after)


def _loss_head(name, h, g, target, after=None):
    s, d = h.shape
    tr = _tile(s, 256)

    def body(h_ref, g_ref, t_ref, dh_ref, dhb_ref, gp_ref, loss_ref):
        x = h_ref[...]
        gg = g_ref[...]
        r = lax.rsqrt(jnp.mean(x * x, axis=-1, keepdims=True) + EPS)
        n = x * r
        e = n * gg - t_ref[...]
        dy = e * (1.0 / d)
        dn = dy * gg
        dh = r * (dn - n * jnp.mean(dn * n, axis=-1, keepdims=True))
        dh_ref[...] = dh
        dhb_ref[...] = dh.astype(BF16)
        step = pl.program_id(0)
        _accumulate(gp_ref, jnp.sum(dy * n, axis=0, keepdims=True), step)
        row_loss = jnp.mean(e * e, axis=-1, keepdims=True)
        _accumulate(loss_ref, 0.5 * jnp.sum(row_loss, axis=0, keepdims=True), step)

    row = pl.BlockSpec((tr, d), lambda i: (i, 0))
    vec = pl.BlockSpec((1, d), lambda i: (0, 0))
    one = pl.BlockSpec((1, 1), lambda i: (0, 0))
    return _pcall(name, body, [h, g, target], [row, vec, row],
                  [_sds((s, d), F32), _sds((s, d), BF16), _sds((1, d), F32), _sds((1, 1), F32)],
                  [row, row, vec, one], grid=(s // tr,), sem=("arbitrary",), after=after)


_SQRT_HALF = math.sqrt(0.5)
_INV_SQRT_2PI = 1.0 / math.sqrt(2.0 * math.pi)


def _gelu(x):
    return 0.5 * x * (1.0 + lax.erf(x * _SQRT_HALF))


def _gelu_grad(x):
    return 0.5 * (1.0 + lax.erf(x * _SQRT_HALF)) + x * jnp.exp(-0.5 * x * x) * _INV_SQRT_2PI


def _causal_mask():
    row = lax.broadcasted_iota(jnp.int32, (CHUNK, CHUNK), 0)
    col = lax.broadcasted_iota(jnp.int32, (CHUNK, CHUNK), 1)
    return row >= col


def _layernorm_parts(v):
    mu = jnp.mean(v, axis=-1, keepdims=True)
    xc = v - mu
    rstd = lax.rsqrt(jnp.mean(xc * xc, axis=-1, keepdims=True) + EPS)
    return xc * rstd, rstd


def _amix_fwd(name, pre, ln_g, ln_b, w_s, b_s_col, after=None):
    s, w2 = pre.shape
    w = w2 // 2
    head = w // A_GROUPS

    def body(pre_ref, g_ref, b_ref, ws_ref, bs_ref, o_ref):
        u = _gelu(pre_ref[:, :w])
        v = _gelu(pre_ref[:, w:])
        vhat, _ = _layernorm_parts(v)
        vn = (vhat * g_ref[...] + b_ref[...]).astype(BF16)
        mask = _causal_mask()
        for grp in range(A_GROUPS):
            cols = slice(grp * head, (grp + 1) * head)
            wm = jnp.where(mask, ws_ref[grp], 0.0).astype(BF16)
            sg = jnp.dot(wm, vn[:, cols], preferred_element_type=F32) + bs_ref[grp]
            o_ref[:, cols] = (u[:, cols] * sg).astype(o_ref.dtype)

    vec = pl.BlockSpec((1, w), lambda i: (0, 0))
    return _pcall(
        name, body, [pre, ln_g, ln_b, w_s, b_s_col],
        [pl.BlockSpec((CHUNK, w2), lambda i: (i, 0)), vec, vec,
         pl.BlockSpec((A_GROUPS, CHUNK, CHUNK), lambda i: (0, 0, 0)),
         pl.BlockSpec((A_GROUPS, CHUNK, 1), lambda i: (0, 0, 0))],
        _sds((s, w), BF16), pl.BlockSpec((CHUNK, w), lambda i: (i, 0)),
        grid=(s // CHUNK,), sem=("parallel",), after=after)


def _amix_bwd(name, pre, dgated, ln_g, ln_b, w_s, b_s_col, after=None):
    s, w2 = pre.shape
    w = w2 // 2
    head = w // A_GROUPS

    def body(pre_ref, dg_ref, g_ref, b_ref, ws_ref, bs_ref, dpre_ref, glg_ref, glb_ref, gws_ref, gbs_ref):
        step = pl.program_id(0)
        pre_u = pre_ref[:, :w]
        pre_v = pre_ref[:, w:]
        u = _gelu(pre_u)
        v = _gelu(pre_v)
        vhat, rstd = _layernorm_parts(v)
        gain = g_ref[...]
        vn = (vhat * gain + b_ref[...]).astype(BF16)
        dgated = dg_ref[...]
        ds = dgated * u
        dsb = ds.astype(BF16)
        mask = _causal_mask()
        du_parts = []
        dvn_parts = []
        for grp in range(A_GROUPS):
            cols = slice(grp * head, (grp + 1) * head)
            wm = jnp.where(mask, ws_ref[grp], 0.0).astype(BF16)
            sg = jnp.dot(wm, vn[:, cols], preferred_element_type=F32) + bs_ref[grp]
            du_parts.append(dgated[:, cols] * sg)
            gws = lax.dot_general(dsb[:, cols], vn[:, cols], NT, preferred_element_type=F32)
            gws = jnp.where(mask, gws, 0.0)
            gbs = jnp.sum(ds[:, cols], axis=-1, keepdims=True)

            @pl.when(step == 0)
            def _():
                gws_ref[grp] = gws
                gbs_ref[grp] = gbs

            @pl.when(step > 0)
            def _():
                gws_ref[grp] += gws
                gbs_ref[grp] += gbs

            dvn_parts.append(lax.dot_general(wm, dsb[:, cols], TN, preferred_element_type=F32))
        du = jnp.concatenate(du_parts, axis=-1)
        dvn = jnp.concatenate(dvn_parts, axis=-1)
        _accumulate(glg_ref, jnp.sum(dvn * vhat, axis=0, keepdims=True), step)
        _accumulate(glb_ref, jnp.sum(dvn, axis=0, keepdims=True), step)
        dvhat = dvn * gain
        dv = rstd * (dvhat - jnp.mean(dvhat, axis=-1, keepdims=True)
                     - vhat * jnp.mean(dvhat * vhat, axis=-1, keepdims=True))
        dpre_ref[:, :w] = (du * _gelu_grad(pre_u)).astype(dpre_ref.dtype)
        dpre_ref[:, w:] = (dv * _gelu_grad(pre_v)).astype(dpre_ref.dtype)

    vec = pl.BlockSpec((1, w), lambda i: (0, 0))
    ws_spec = pl.BlockSpec((A_GROUPS, CHUNK, CHUNK), lambda i: (0, 0, 0))
    bs_spec = pl.BlockSpec((A_GROUPS, CHUNK, 1), lambda i: (0, 0, 0))
    return _pcall(
        name, body, [pre, dgated, ln_g, ln_b, w_s, b_s_col],
        [pl.BlockSpec((CHUNK, w2), lambda i: (i, 0)), pl.BlockSpec((CHUNK, w), lambda i: (i, 0)),
         vec, vec, ws_spec, bs_spec],
        [_sds((s, w2), BF16), _sds((1, w), F32), _sds((1, w), F32),
         _sds((A_GROUPS, CHUNK, CHUNK), F32), _sds((A_GROUPS, CHUNK, 1), F32)],
        [pl.BlockSpec((CHUNK, w2), lambda i: (i, 0)), vec, vec, ws_spec, bs_spec],
        grid=(s // CHUNK,), sem=("arbitrary",), after=after)


def _shift_rows(x, k, forward):
    n = x.shape[0]
    row = lax.broadcasted_iota(jnp.int32, x.shape, 0)
    if forward:
        return jnp.where(row >= k, pltpu.roll(x, k, 0), 0.0)
    return jnp.where(row < n - k, pltpu.roll(x, n - k, 0), 0.0)


def _window_sum(x, window, forward):
    k = 1
    while k < window:
        x = x + _shift_rows(x, k, forward)
        k *= 2
    return x


def _pool(name, v, backward, after=None):
    s, w = v.shape
    head = w // B_GROUPS
    lane = _tile(head, 128)

    def body(v_ref, o_ref):
        grp = pl.program_id(0)
        x = v_ref[...]
        t = lax.broadcasted_iota(jnp.int32, x.shape, 0)
        for idx, window in enumerate(B_WINDOWS):
            @pl.when(grp == idx)
            def _():
                inv_count = 1.0 / jnp.minimum(t + 1, window).astype(F32)
                if backward:
                    out = _window_sum(x * inv_count, window, False) - x
                else:
                    out = _window_sum(x, window, True) * inv_count - x
                o_ref[...] = out.astype(o_ref.dtype)

    per = head // lane
    spec = pl.BlockSpec((s, lane), lambda g, j: (0, g * per + j))
    return _pcall(name, body, [v], [spec], _sds((s, w), BF16), spec, grid=(B_GROUPS, per),
                  sem=("parallel", "parallel"), after=after)


def _colsum(name, a, after=None):
    s, d = a.shape
    tr = _tile(s, 256)

    def body(a_ref, o_ref):
        _accumulate(o_ref, jnp.sum(a_ref[...], axis=0, keepdims=True), pl.program_id(0))

    return _pcall(name, body, [a], [pl.BlockSpec((tr, d), lambda i: (i, 0))], _sds((1, d), F32),
                  pl.BlockSpec((1, d), lambda i: (0, 0)), grid=(s // tr,), sem=("arbitrary",), after=after)


def _adamw(w, g, m, v):
    m = ADAM_B1 * m + (1.0 - ADAM_B1) * g
    v = ADAM_B2 * v + (1.0 - ADAM_B2) * (g * g)
    m_hat = m / (1.0 - ADAM_B1 ** ADAM_STEP)
    v_hat = v / (1.0 - ADAM_B2 ** ADAM_STEP)
    delta = -ADAM_LR * (m_hat / (jnp.sqrt(v_hat) + ADAM_EPS) + ADAM_WD * w)
    return delta, m, v


def _adam_rows(name, g, w, m, v, after=None):
    r, c = g.shape
    tr = _tile(r, 256)

    def body(g_ref, w_ref, m_ref, v_ref, d_ref, nm_ref, nv_ref):
        d_ref[...], nm_ref[...], nv_ref[...] = _adamw(w_ref[...], g_ref[...], m_ref[...], v_ref[...])

    spec = pl.BlockSpec((tr, c), lambda i: (i, 0))
    return _pcall(name, body, [g, w, m, v], [spec] * 4, [_sds((r, c), F32)] * 3, [spec] * 3,
                  grid=(r // tr,), sem=("parallel",), after=after)


def _position():
    return lax.axis_index("x"), lax.axis_index("y"), lax.axis_index("c")


def _other_chips(x, y):
    return [(1 - x, y), (x, 1 - y), (1 - x, 1 - y)]


def _slot(px, py, pc):
    return 4 * px + 2 * py + pc


def _hbm(a):
    return pltpu.with_memory_space_constraint(a, pltpu.HBM)


def _gather_copies(srcs, lands, send_sems, recv_sems):
    x, y, c = _position()
    peers = [(x, y, 1 - c)] + [(*chip, c) for chip in _other_chips(x, y)]
    mine = _slot(x, y, c)
    return [[pltpu.make_async_remote_copy(
        src_ref=srcs[t], dst_ref=lands[t].at[mine], send_sem=send_sems[t].at[k], recv_sem=recv_sems[t].at[k],
        device_id=peer, device_id_type=MESH) for k, peer in enumerate(peers)] for t in range(len(srcs))]


def _gather_start(name, shards):
    n = len(shards)

    def body(*refs):
        srcs, lands = refs[:n], refs[n:2 * n]
        send_sems, recv_sems = refs[2 * n:3 * n], refs[3 * n:4 * n]
        token = refs[-1]
        for per_tensor in _gather_copies(srcs, lands, send_sems, recv_sems):
            for cp in per_tensor:
                cp.start()
        token[...] = jnp.zeros_like(token)

    lands = [lax.empty((N_DEV,) + s.shape, s.dtype) for s in shards]
    out_shape = ([pltpu.SemaphoreType.DMA((4,)) for _ in range(2 * n)]
                 + [pltpu.HBM(s.shape, s.dtype) for s in shards]
                 + [pltpu.HBM(l.shape, l.dtype) for l in lands]
                 + [_sds((8, 128), F32)])
    out = pl.pallas_call(
        body, name=name, out_shape=out_shape, in_specs=[_HBM] * (2 * n),
        out_specs=[_SEM] * (2 * n) + [_HBM] * (2 * n) + [pl.BlockSpec(memory_space=pltpu.VMEM)],
        input_output_aliases={i: 2 * n + i for i in range(2 * n)},
        compiler_params=pltpu.CompilerParams(has_side_effects=_EFFECT),
    )(*[_hbm(s) for s in shards], *[_hbm(l) for l in lands])
    send_sems, recv_sems = out[:n], out[n:2 * n]
    srcs, lands = out[2 * n:3 * n], out[3 * n:4 * n]
    return [(send_sems[t], recv_sems[t], srcs[t], lands[t]) for t in range(n)], out[-1]


def _gather_wait(name, started, after):
    n = len(started)

    def body(*refs):
        srcs, lands = refs[:n], refs[n:2 * n]
        send_sems, recv_sems = refs[2 * n:3 * n], refs[3 * n:4 * n]
        for per_tensor in _gather_copies(srcs, lands, send_sems, recv_sems):
            for cp in per_tensor:
                cp.wait_send()
                cp.wait_recv()

    srcs = [e[2] for e in started]
    lands = [e[3] for e in started]
    out = pl.pallas_call(
        body, name=name, out_shape=[pltpu.HBM(a.shape, a.dtype) for a in srcs + lands],
        in_specs=[_HBM] * (2 * n) + [_SEM] * (2 * n) + [_ANY], out_specs=[_HBM] * (2 * n),
        input_output_aliases={i: i for i in range(2 * n)},
        compiler_params=pltpu.CompilerParams(has_side_effects=_EFFECT),
    )(*srcs, *lands, *[e[0] for e in started], *[e[1] for e in started], after)
    return out[:n], out[n:]


def _gather_finish(name, shards, lands, after):
    n = len(shards)

    def body(*refs):
        srcs, lands_in, outs = refs[:n], refs[n:2 * n], refs[2 * n:3 * n]
        send_sems, recv_sems, local_sems = refs[3 * n:]
        x, y, c = _position()
        local = [pltpu.make_async_copy(srcs[t], outs[t].at[_slot(x, y, c)], local_sems.at[t]) for t in range(n)]
        copies = []
        for t in range(n):
            for j, chip in enumerate(_other_chips(x, y)):
                block = outs[t].at[_slot(*chip, c)]
                copies.append(pltpu.make_async_remote_copy(
                    src_ref=block, dst_ref=block, send_sem=send_sems.at[t, j], recv_sem=recv_sems.at[t, j],
                    device_id=(x, y, 1 - c), device_id_type=MESH))
        for cp in local + copies:
            cp.start()
        for cp in copies:
            cp.wait_send()
        for t in range(n):
            for j, chip in enumerate(_other_chips(x, y)):
                block = outs[t].at[_slot(*chip, 1 - c)]
                pltpu.make_async_remote_copy(
                    src_ref=block, dst_ref=block, send_sem=send_sems.at[t, j], recv_sem=recv_sems.at[t, j],
                    device_id=(x, y, 1 - c), device_id_type=MESH).wait_recv()
        for cp in local:
            cp.wait()

    return _pcall(name, body, [*shards, *lands], [_ANY] * (2 * n),
                  [_sds(l.shape, l.dtype) for l in lands], [_ANY] * n,
                  scratch=[pltpu.SemaphoreType.DMA((n, 3)), pltpu.SemaphoreType.DMA((n, 3)),
                           pltpu.SemaphoreType.DMA((n,))],
                  after=after, aliases={n + t: t for t in range(n)})


def _exchange_sibling(name, fulls, after):
    n = len(fulls)

    def body(*refs):
        src = refs[:n]
        out = refs[n:2 * n]
        send_sems, recv_sems = refs[2 * n:]
        x, y, c = _position()
        copies = [pltpu.make_async_remote_copy(
            src_ref=src[t].at[:, 1 - c], dst_ref=out[t], send_sem=send_sems.at[t], recv_sem=recv_sems.at[t],
            device_id=(x, y, 1 - c), device_id_type=MESH) for t in range(n)]
        for cp in copies:
            cp.start()
        for cp in copies:
            cp.wait()

    return _pcall(name, body, fulls, [_ANY] * n, [_sds((N_CHIPS,) + f.shape[2:], f.dtype) for f in fulls],
                  [_ANY] * n, scratch=[pltpu.SemaphoreType.DMA((n,)), pltpu.SemaphoreType.DMA((n,))],
                  after=after)


def _add_sibling(name, full, recv, core, after):
    _, _, r, c = full.shape
    tr = _tile(r, max(8, (256 * 1024) // c))

    def body(core_ref, f_ref, r_ref, o_ref):
        o_ref[...] = (f_ref[...].astype(F32) + r_ref[...].astype(F32)).astype(o_ref.dtype)

    return _pcall(
        name, body, [full, recv],
        [pl.BlockSpec((None, None, tr, c), lambda p, i, core_ref: (p, core_ref[0], i, 0)),
         pl.BlockSpec((None, tr, c), lambda p, i, core_ref: (p, i, 0))],
        _sds((N_CHIPS, r, c), BF16), pl.BlockSpec((None, tr, c), lambda p, i, core_ref: (p, i, 0)),
        grid=(N_CHIPS, r // tr), sem=("parallel", "parallel"), prefetch=[core], after=after)


def _scatter_copies(srcs, lands, send_sems, recv_sems):
    x, y, c = _position()
    return [[pltpu.make_async_remote_copy(
        src_ref=srcs[t].at[2 * px + py], dst_ref=lands[t].at[j],
        send_sem=send_sems[t].at[j], recv_sem=recv_sems[t].at[j],
        device_id=(px, py, c), device_id_type=MESH) for j, (px, py) in enumerate(_other_chips(x, y))]
        for t in range(len(srcs))]


def _scatter_start(name, partials):
    n = len(partials)

    def body(*refs):
        srcs, lands = refs[:n], refs[n:2 * n]
        send_sems, recv_sems = refs[2 * n:3 * n], refs[3 * n:4 * n]
        token = refs[-1]
        for per_tensor in _scatter_copies(srcs, lands, send_sems, recv_sems):
            for cp in per_tensor:
                cp.start()
        token[...] = jnp.zeros_like(token)

    lands = [lax.empty((N_CHIPS - 1,) + p.shape[1:], p.dtype) for p in partials]
    out_shape = ([pltpu.SemaphoreType.DMA((3,)) for _ in range(2 * n)]
                 + [pltpu.HBM(p.shape, p.dtype) for p in partials]
                 + [pltpu.HBM(l.shape, l.dtype) for l in lands]
                 + [_sds((8, 128), F32)])
    out = pl.pallas_call(
        body, name=name, out_shape=out_shape, in_specs=[_HBM] * (2 * n),
        out_specs=[_SEM] * (2 * n) + [_HBM] * (2 * n) + [pl.BlockSpec(memory_space=pltpu.VMEM)],
        input_output_aliases={i: 2 * n + i for i in range(2 * n)},
        compiler_params=pltpu.CompilerParams(has_side_effects=_EFFECT),
    )(*[_hbm(p) for p in partials], *[_hbm(l) for l in lands])
    send_sems, recv_sems = out[:n], out[n:2 * n]
    srcs, lands = out[2 * n:3 * n], out[3 * n:4 * n]
    return [(send_sems[t], recv_sems[t], srcs[t], lands[t]) for t in range(n)], out[-1]


def _scatter_wait(name, started, after):
    n = len(started)

    def body(*refs):
        srcs, lands = refs[:n], refs[n:2 * n]
        send_sems, recv_sems = refs[2 * n:3 * n], refs[3 * n:4 * n]
        for per_tensor in _scatter_copies(srcs, lands, send_sems, recv_sems):
            for cp in per_tensor:
                cp.wait_send()
                cp.wait_recv()

    srcs = [e[2] for e in started]
    lands = [e[3] for e in started]
    out = pl.pallas_call(
        body, name=name, out_shape=[pltpu.HBM(a.shape, a.dtype) for a in srcs + lands],
        in_specs=[_HBM] * (2 * n) + [_SEM] * (2 * n) + [_ANY], out_specs=[_HBM] * (2 * n),
        input_output_aliases={i: i for i in range(2 * n)},
        compiler_params=pltpu.CompilerParams(has_side_effects=_EFFECT),
    )(*srcs, *lands, *[e[0] for e in started], *[e[1] for e in started], after)
    return out[:n], out[n:]


def _reduce_adam(name, partial, recv, chip, w, m, v, layer, carried, after):
    n_layers, r, c = w.shape
    tr = _tile(r, max(8, (128 * 1024) // c))

    def body(chip_ref, p_ref, r_ref, w_ref, m_ref, v_ref, *rest):
        g_ref, d_ref, nm_ref, nv_ref = rest[-4:]
        g = p_ref[...].astype(F32)
        for j in range(N_CHIPS - 1):
            g = g + r_ref[j].astype(F32)
        g_ref[...] = g
        d_ref[...], nm_ref[...], nv_ref[...] = _adamw(w_ref[...], g, m_ref[...], v_ref[...])

    layered = pl.BlockSpec((None, tr, c), lambda i, chip_ref: (layer, i, 0))
    in_specs = [pl.BlockSpec((None, tr, c), lambda i, chip_ref: (chip_ref[0], i, 0)),
                pl.BlockSpec((N_CHIPS - 1, tr, c), lambda i, chip_ref: (0, i, 0)),
                layered, layered, layered]
    operands = [partial, recv, w, m, v]
    aliases = {}
    if carried is not None:
        operands += list(carried)
        in_specs += [_ANY] * 4
        aliases = {1 + 5 + o: o for o in range(4)}
    return _pcall(name, body, operands, in_specs, [_sds((n_layers, r, c), F32)] * 4, [layered] * 4,
                  grid=(r // tr,), sem=("parallel",), prefetch=[chip], after=after, aliases=aliases)


def _all_reduce_small(name, packed, after):
    r, lanes = packed.shape

    def body(in_ref, out_ref, gather_ref, send_sems, recv_sems):
        x, y, c = _position()
        me = _slot(x, y, c)
        gather_ref[me] = in_ref[...]
        copies = []
        for k in range(1, N_DEV):
            fx, fy, fc = (k >> 2) & 1, (k >> 1) & 1, k & 1
            peer = (x ^ fx, y ^ fy, c ^ fc)
            copies.append(pltpu.make_async_remote_copy(
                src_ref=in_ref, dst_ref=gather_ref.at[me],
                send_sem=send_sems.at[k - 1], recv_sem=recv_sems.at[k - 1],
                device_id=peer, device_id_type=MESH))
        for cp in copies:
            cp.start()
        for cp in copies:
            cp.wait()
        total = gather_ref[0]
        for d in range(1, N_DEV):
            total = total + gather_ref[d]
        out_ref[...] = total

    vmem = pl.BlockSpec(memory_space=pltpu.VMEM)
    return _pcall(name, body, [packed], [vmem], _sds((r, lanes), F32), vmem,
                  scratch=[pltpu.VMEM((N_DEV, r, lanes), F32), pltpu.SemaphoreType.DMA((N_DEV - 1,)),
                           pltpu.SemaphoreType.DMA((N_DEV - 1,))], after=after)


def _pack(arrays):
    return jnp.concatenate([a.reshape(-1, 128) for a in arrays], axis=0)


def _unpack(packed, shapes):
    out, row = [], 0
    for shape in shapes:
        rows = math.prod(shape) // 128
        out.append(packed[row:row + rows].reshape(shape))
        row += rows
    return out


class _Order:
    def __init__(self):
        self.last = None

    def __call__(self, fn, *args, **kwargs):
        out = fn(*args, after=self.last, **kwargs)
        self.last = out[0] if isinstance(out, (list, tuple)) else out
        return out


def kernel(x, a_w_in, a_ln_g, a_ln_b, a_w_s, a_b_s, a_w_out, b_w_in, b_w_grp, b_scale, b_w_out, norm_mix, norm_mlp, mlp_w1, mlp_w2, final_norm, loss_target, m_a_w_in, m_a_ln_g, m_a_ln_b, m_a_w_s, m_a_b_s, m_a_w_out, m_b_w_in, m_b_w_grp, m_b_scale, m_b_w_out, m_norm_mix, m_norm_mlp, m_mlp_w1, m_mlp_w2, m_final_norm, v_a_w_in, v_a_ln_g, v_a_ln_b, v_a_w_s, v_a_b_s, v_a_w_out, v_b_w_in, v_b_w_grp, v_b_scale, v_b_w_out, v_norm_mix, v_norm_mlp, v_mlp_w1, v_mlp_w2, v_final_norm):
    s, d = x.shape[1], x.shape[2]
    depth = mlp_w1.shape[0]
    a_slab = a_w_in.shape[2]
    ff_slab = mlp_w1.shape[2]
    ff_rows = mlp_w2.shape[1]
    bh = b_w_grp.shape[3]
    my_x, my_y, my_c = _position()
    core = jnp.reshape(my_c, (1,)).astype(jnp.int32)
    chip = jnp.reshape(2 * my_x + my_y, (1,)).astype(jnp.int32)
    device = _slot(my_x, my_y, my_c)
    run = _Order()

    w1_b, w2_b = mlp_w1.astype(BF16), mlp_w2.astype(BF16)
    shards = [a_w_in[0].astype(BF16), a_w_out[0].astype(BF16), b_scale,
              w1_b[0], w2_b[0],
              b_w_in[0].astype(BF16), b_w_grp[0].astype(BF16), b_w_out[0].astype(BF16),
              w1_b[1], w2_b[1]]
    started, token = _gather_start("weights_gather_start", shards)
    run.last = token

    def gathered(name, indices):
        srcs, lands = _gather_wait(name + "_wait", [started[t] for t in indices], run.last)
        run.last = srcs[0]
        return run(_gather_finish, name + "_finish", srcs, lands)

    h0 = x[0]
    target = loss_target[0]
    ln_g, ln_b = a_ln_g, a_ln_b
    w_s = a_w_s[0]
    b_s_col = a_b_s[0][:, :, None]
    nmix = [norm_mix[l][None, :] for l in range(depth)]
    nmlp = [norm_mlp[l][None, :] for l in range(depth)]

    def mlp_forward(l, h, gather_indices):
        hn = run(_rms_fwd, f"mlp{l}_norm", h, nmlp[l])
        (w1,) = gathered(f"mlp{l}_up_weights", gather_indices[:1])
        act, act_sq = run(_mm_nn, f"mlp{l}_up", hn, w1,
                          lambda acc: (jnp.maximum(acc, 0.0), jnp.square(jnp.maximum(acc, 0.0))),
                          (BF16, BF16), slab=True)
        (w2,) = gathered(f"mlp{l}_down_weights", gather_indices[1:])
        w2 = w2.reshape(-1, d)
        (h_out,) = run(_mm_nn, f"mlp{l}_down", act_sq, w2, lambda acc, res: (acc + res,), (F32,),
                       extras=(h,), extra_kinds=("tile",))
        return h_out, (h, hn, act, act_sq, w1, w2)

    scattered = []

    def scatter(name, group):
        fulls = [g.reshape(N_CHIPS, 2, r, c) for g, r, c, _ in group]
        from_sibling = run(_exchange_sibling, name + "_to_sibling", fulls)
        partials = [run(_add_sibling, f"{name}_add_sibling_{t}", fulls[t], from_sibling[t], core)
                    for t in range(len(group))]
        in_flight, tok = _scatter_start(name + "_scatter_start", partials)
        run.last = tok
        scattered.append((name, in_flight, [spec for _, _, _, spec in group]))

    def mlp_backward(l, saved, dh, dhb):
        h, hn, act, act_sq, w1, w2 = saved
        g_w2 = run(_mm_tn, f"mlp{l}_down_dw", act_sq, dhb)
        scatter(f"mlp{l}_down_grads", [(g_w2, ff_rows, d, ("mlp_w2", l))])
        (dpre,) = run(_mm_nt, f"mlp{l}_down_dx", dhb, w2, lambda acc, a: (2.0 * a.astype(F32) * acc,),
                      (BF16,), extras=(act,), extra_kinds=("tile",))
        g_w1 = run(_mm_tn, f"mlp{l}_up_dw", hn, dpre, slab_w=ff_slab)
        scatter(f"mlp{l}_up_grads", [(g_w1, d, ff_slab, ("mlp_w1", l))])
        (dhn,) = run(_mm_nt, f"mlp{l}_up_dx", dpre, w1, lambda acc: (acc,), (F32,), slab=True)
        dh, dhb, g_norm = run(_rms_bwd, f"mlp{l}_norm_bwd", dhn, h, nmlp[l], dh)
        return dh, dhb, g_norm

    hn0 = run(_rms_fwd, "mix0_norm", h0, nmix[0])
    (wa_in,) = gathered("mixa_in_weights", [0])
    (pre,) = run(_mm_nn, "mixa_in", hn0, wa_in, lambda acc: (acc,), (F32,), slab=True)
    wa_out, scale = gathered("mixa_out_weights", [1, 2])
    wa_out, scale = wa_out.reshape(d, d), scale.reshape(1, d)
    gated = run(_amix_fwd, "mixa_gate", pre, ln_g, ln_b, w_s, b_s_col)
    (h1,) = run(_mm_nn, "mixa_out", gated, wa_out, lambda acc, res: (acc + res,), (F32,),
                extras=(h0,), extra_kinds=("tile",))
    h2, saved_mlp0 = mlp_forward(0, h1, [3, 4])
    hn2 = run(_rms_fwd, "mix1_norm", h2, nmix[1])
    wb_in, wb_grp, wb_out = gathered("mixb_weights", [5, 6, 7])
    wb_in, wb_out = wb_in.reshape(d, d), wb_out.reshape(d, d)
    wb_grp = jnp.transpose(wb_grp, (1, 0, 2, 3)).reshape(B_GROUPS, bh, bh)
    (vb,) = run(_mm_nn, "mixb_in", hn2, wb_in, lambda acc: (acc,), (F32,))
    pooled = run(_pool, "mixb_pool", vb, backward=False)
    tm = _tile(s, 1024)
    grp_tile = pl.BlockSpec((tm, bh), lambda i, j, k: (i, j))
    grp_weight = pl.BlockSpec((None, bh, bh), lambda i, j, k: (j, 0, 0))
    mixed, mixed_scaled = run(
        _matmul, "mixb_grp", pooled, wb_grp, NN, (s // tm, B_GROUPS, 1), grp_tile, grp_weight,
        [_sds((s, d), BF16), _sds((s, d), BF16)], [grp_tile] * 2,
        (tm, bh), lambda acc, sc: (acc, acc * sc), (scale,), [pl.BlockSpec((1, bh), lambda i, j, k: (0, j))])
    (h3,) = run(_mm_nn, "mixb_out", mixed_scaled, wb_out, lambda acc, res: (acc + res,), (F32,),
                extras=(h2,), extra_kinds=("tile",))
    h4, saved_mlp1 = mlp_forward(1, h3, [8, 9])
    dh, dhb, g_final, loss_part = run(_loss_head, "loss_head", h4, final_norm[None, :], target)

    dh, dhb, g_nmlp1 = mlp_backward(1, saved_mlp1, dh, dhb)
    g_wb_out = run(_mm_tn, "mixb_out_dw", mixed_scaled, dhb)
    dms_scaled, dms_mixed = run(
        _mm_nt, "mixb_out_dx", dhb, wb_out,
        lambda acc, sc, mx: (acc * sc, acc * mx.astype(F32)), (BF16, F32),
        extras=(scale, mixed), extra_kinds=("row", "tile"))
    g_scale = run(_colsum, "mixb_scale_dw", dms_mixed)
    tks = _tile(s, 1024)
    grp_rows = pl.BlockSpec((tks, bh), lambda i, j, k: (k, j))
    (g_wb_grp,) = run(
        _matmul, "mixb_grp_dw", pooled, dms_scaled, TN, (1, B_GROUPS, s // tks), grp_rows, grp_rows,
        [_sds((B_GROUPS, bh, bh), BF16)], [grp_weight], (bh, bh), lambda acc: (acc,))
    (dpooled,) = run(
        _matmul, "mixb_grp_dx", dms_scaled, wb_grp, NT, (s // tm, B_GROUPS, 1), grp_tile, grp_weight,
        [_sds((s, d), F32)], [grp_tile], (tm, bh), lambda acc: (acc,))
    dvb = run(_pool, "mixb_pool_bwd", dpooled, backward=True)
    g_wb_in = run(_mm_tn, "mixb_in_dw", hn2, dvb)
    g_wb_grp_slabs = jnp.transpose(g_wb_grp.reshape(B_GROUPS, N_DEV, bh // N_DEV, bh), (1, 0, 2, 3))
    scatter("mixb_grads", [(g_wb_out, d // N_DEV, d, ("b_w_out", 0)),
                           (g_wb_grp_slabs, B_GROUPS * bh // N_DEV, bh, ("b_w_grp", 0)),
                           (g_wb_in, d // N_DEV, d, ("b_w_in", 0))])
    (dhn2,) = run(_mm_nt, "mixb_in_dx", dvb, wb_in, lambda acc: (acc,), (F32,))
    dh, dhb, g_nmix1 = run(_rms_bwd, "mix1_norm_bwd", dhn2, h2, nmix[1], dh)
    dh, dhb, g_nmlp0 = mlp_backward(0, saved_mlp0, dh, dhb)
    g_wa_out = run(_mm_tn, "mixa_out_dw", gated, dhb)
    (dgated,) = run(_mm_nt, "mixa_out_dx", dhb, wa_out, lambda acc: (acc,), (F32,))
    dpre, g_ln_g, g_ln_b, g_w_s, g_b_s = run(_amix_bwd, "mixa_gate_bwd", pre, dgated, ln_g, ln_b, w_s, b_s_col)
    g_wa_in = run(_mm_tn, "mixa_in_dw", hn0, dpre, slab_w=a_slab)
    scatter("mixa_grads", [(g_wa_in, d, a_slab, ("a_w_in", 0)), (g_wa_out, d // N_DEV, d, ("a_w_out", 0))])
    (dhn0,) = run(_mm_nt, "mixa_in_dx", dpre, wa_in, lambda acc: (acc,), (F32,), slab=True)
    grad_x, _, g_nmix0 = run(_rms_bwd, "mix0_norm_bwd", dhn0, h0, nmix[0], dh)

    g_norm_mix = jnp.concatenate([g_nmix0, g_nmix1], axis=0)
    g_norm_mlp = jnp.concatenate([g_nmlp0, g_nmlp1], axis=0)
    small_parts = [g_ln_g, g_ln_b, g_w_s, g_b_s, g_norm_mix, g_norm_mlp, g_final, g_scale]
    small_sum = run(_all_reduce_small, "small_grads_all_reduce", _pack(small_parts))
    sg = _unpack(small_sum, [a_ln_g.shape, a_ln_b.shape, a_w_s.shape, a_b_s.shape, norm_mix.shape,
                             norm_mlp.shape, final_norm.shape, (1, d)])
    shard = b_scale.shape[1]
    sg[7] = lax.dynamic_slice(sg[7], (0, device * shard), (1, shard))
    small_w = [a_ln_g, a_ln_b, a_w_s, a_b_s, norm_mix, norm_mlp, final_norm, b_scale]
    small_m = [m_a_ln_g, m_a_ln_b, m_a_w_s, m_a_b_s, m_norm_mix, m_norm_mlp, m_final_norm, m_b_scale]
    small_v = [v_a_ln_g, v_a_ln_b, v_a_w_s, v_a_b_s, v_norm_mix, v_norm_mlp, v_final_norm, v_b_scale]
    small_out = run(_adam_rows, "small_adam", _pack(sg), _pack(small_w), _pack(small_m), _pack(small_v))
    shapes = [w.shape for w in small_w]
    small_res = [sg] + [_unpack(o, shapes) for o in small_out]

    weights = {"a_w_in": (a_w_in, m_a_w_in, v_a_w_in), "a_w_out": (a_w_out, m_a_w_out, v_a_w_out),
               "b_w_in": (b_w_in, m_b_w_in, v_b_w_in), "b_w_grp": (b_w_grp, m_b_w_grp, v_b_w_grp),
               "b_w_out": (b_w_out, m_b_w_out, v_b_w_out), "mlp_w1": (mlp_w1, m_mlp_w1, v_mlp_w1),
               "mlp_w2": (mlp_w2, m_mlp_w2, v_mlp_w2)}
    results = {}
    for name, in_flight, specs in scattered:
        partials, lands = _scatter_wait(name + "_scatter_wait", in_flight, run.last)
        run.last = lands[0]
        for t, (wname, layer) in enumerate(specs):
            w, m, v = weights[wname]
            _, r, c = partials[t].shape
            layers = w.shape[0]
            results[wname] = run(_reduce_adam, f"{name}_reduce_adam_{t}", partials[t], lands[t], chip,
                                 w.reshape(layers, r, c), m.reshape(layers, r, c), v.reshape(layers, r, c),
                                 layer, results.get(wname))
    big = {wname: [o.reshape(weights[wname][0].shape) for o in outs] for wname, outs in results.items()}

    loss = lax.psum(loss_part[0, 0], ("x", "y", "c"))

    def leaf(o):
        return (big["a_w_in"][o], small_res[o][0], small_res[o][1], small_res[o][2], small_res[o][3],
                big["a_w_out"][o], big["b_w_in"][o], big["b_w_grp"][o], small_res[o][7], big["b_w_out"][o],
                small_res[o][4], small_res[o][5], big["mlp_w1"][o], big["mlp_w2"][o], small_res[o][6])

    return (loss, grad_x[None], *leaf(0), *leaf(1), *leaf(2), *leaf(3))
```

```python
import math

import jax
import jax.numpy as jnp
from jax import lax
from jax.experimental import pallas as pl
from jax.experimental.pallas import tpu as pltpu

F32 = jnp.float32
BF16 = jnp.bfloat16
MESH = pl.DeviceIdType.MESH

N_DEV = 8
N_CHIPS = 4
CHUNK = 128
A_GROUPS = 8
B_WINDOWS = (2, 4, 8, 16)
B_GROUPS = len(B_WINDOWS)
EPS = 1e-6
ADAM_LR = 0.001
ADAM_B1 = 0.9
ADAM_B2 = 0.999
ADAM_EPS = 1e-08
ADAM_WD = 0.01
ADAM_STEP = 10

VMEM_LIMIT = 48 * 1024 * 1024

NN = (((1,), (0,)), ((), ()))
NT = (((1,), (1,)), ((), ()))
TN = (((0,), (0,)), ((), ()))

_ANY = pl.BlockSpec(memory_space=pl.ANY)
_HBM = pl.BlockSpec(memory_space=pltpu.HBM)
_SEM = pl.BlockSpec(memory_space=pltpu.SEMAPHORE)
_EFFECT = pltpu.SideEffectType.DATAFLOW_SIDE_EFFECTING


def _tile(n, pref):
    return pref if n % pref == 0 else n


def _sds(shape, dtype):
    return jax.ShapeDtypeStruct(shape, dtype)


def _pcall(name, body, operands, in_specs, out_shape, out_specs, *, grid=None, sem=None, scratch=(),
           prefetch=(), after=None, aliases=None):
    after = [] if after is None else [after]
    n_lead = len(prefetch) + len(operands)
    n_after = len(after)

    def wrapped(*refs):
        body(*refs[:n_lead], *refs[n_lead + n_after:])

    in_specs = list(in_specs) + [_ANY] * n_after
    params = pltpu.CompilerParams(vmem_limit_bytes=VMEM_LIMIT) if sem is None else \
        pltpu.CompilerParams(dimension_semantics=sem, vmem_limit_bytes=VMEM_LIMIT)
    kwargs = dict(out_shape=out_shape, scratch_shapes=list(scratch), compiler_params=params, name=name,
                  input_output_aliases=aliases or {})
    if prefetch:
        kwargs["grid_spec"] = pltpu.PrefetchScalarGridSpec(
            num_scalar_prefetch=len(prefetch), grid=grid, in_specs=in_specs, out_specs=out_specs,
            scratch_shapes=list(scratch))
        kwargs.pop("scratch_shapes")
    else:
        kwargs.update(in_specs=in_specs, out_specs=out_specs)
        if grid is not None:
            kwargs["grid"] = grid
    return pl.pallas_call(wrapped, **kwargs)(*prefetch, *operands, *after)


def _matmul(name, a, b, dims, grid, a_spec, b_spec, out_shape, out_specs, acc_shape,
            epilogue, extras=(), extra_specs=(), after=None, prefetch=()):
    nk = grid[2]
    n_extra = len(extras)
    n_out = len(out_shape)
    n_pre = len(prefetch)

    def body(*refs):
        refs = refs[n_pre:]
        a_ref, b_ref = refs[0], refs[1]
        extra_refs = refs[2:2 + n_extra]
        out_refs = refs[2 + n_extra:2 + n_extra + n_out]

        def finish(acc):
            outs = epilogue(acc, *[r[...] for r in extra_refs])
            for o_ref, o in zip(out_refs, outs):
                o_ref[...] = o.astype(o_ref.dtype)

        def product():
            return lax.dot_general(a_ref[...], b_ref[...], dims, preferred_element_type=F32)

        if nk == 1:
            finish(product())
        else:
            acc_ref = refs[-1]
            k = pl.program_id(2)

            @pl.when(k == 0)
            def _():
                acc_ref[...] = product()

            if nk > 2:
                @pl.when(jnp.logical_and(k > 0, k < nk - 1))
                def _():
                    acc_ref[...] += product()

            @pl.when(k == nk - 1)
            def _():
                finish(acc_ref[...] + product())

    scratch = [] if nk == 1 else [pltpu.VMEM(acc_shape, F32)]
    return _pcall(name, body, [a, b, *extras], [a_spec, b_spec, *extra_specs], out_shape, out_specs,
                  grid=grid, sem=("parallel", "parallel", "arbitrary"), scratch=scratch, after=after,
                  prefetch=prefetch)


def _mm_nn(name, a, b, epilogue, out_dtypes, extras=(), extra_kinds=(), slab=False, after=None,
           tm=1024, tn=1024, tk=2048):
    m, kd = a.shape
    if slab:
        n_slab, _, w = b.shape
        n = n_slab * w
        tn = _tile(w, min(tn, w))
        per = w // tn
        tk = _tile(kd, tk)
        b_spec = pl.BlockSpec((None, tk, tn), lambda i, j, k: (j // per, k, j % per))
    else:
        n = b.shape[1]
        tn = _tile(n, tn)
        tk = _tile(kd, tk)
        b_spec = pl.BlockSpec((tk, tn), lambda i, j, k: (k, j))
    tm = _tile(m, tm)
    grid = (m // tm, n // tn, kd // tk)
    a_spec = pl.BlockSpec((tm, tk), lambda i, j, k: (i, k))
    tile_spec = pl.BlockSpec((tm, tn), lambda i, j, k: (i, j))
    row_spec = pl.BlockSpec((1, tn), lambda i, j, k: (0, j))
    extra_specs = [tile_spec if kind == "tile" else row_spec for kind in extra_kinds]
    return _matmul(name, a, b, NN, grid, a_spec, b_spec,
                   [_sds((m, n), d) for d in out_dtypes], [tile_spec for _ in out_dtypes],
                   (tm, tn), epilogue, extras, extra_specs, after=after)


def _mm_nt(name, a, b, epilogue, out_dtypes, extras=(), extra_kinds=(), slab=False, after=None,
           tm=1024, tn=1024, tk=2048):
    m, kd = a.shape
    if slab:
        n_slab, n, w = b.shape
        tk = _tile(w, min(tk, w))
        per = w // tk
        tn = _tile(n, tn)
        b_spec = pl.BlockSpec((None, tn, tk), lambda i, j, k: (k // per, j, k % per))
    else:
        n = b.shape[0]
        tn = _tile(n, tn)
        tk = _tile(kd, tk)
        b_spec = pl.BlockSpec((tn, tk), lambda i, j, k: (j, k))
    tm = _tile(m, tm)
    grid = (m // tm, n // tn, kd // tk)
    a_spec = pl.BlockSpec((tm, tk), lambda i, j, k: (i, k))
    tile_spec = pl.BlockSpec((tm, tn), lambda i, j, k: (i, j))
    row_spec = pl.BlockSpec((1, tn), lambda i, j, k: (0, j))
    extra_specs = [tile_spec if kind == "tile" else row_spec for kind in extra_kinds]
    return _matmul(name, a, b, NT, grid, a_spec, b_spec,
                   [_sds((m, n), d) for d in out_dtypes], [tile_spec for _ in out_dtypes],
                   (tm, tn), epilogue, extras, extra_specs, after=after)


def _mm_tn(name, a, b, slab_w=None, after=None, tm=1024, tn=1024, tk=2048):
    s, m = a.shape
    n = b.shape[1]
    tm = _tile(m, tm)
    tk = _tile(s, tk)
    if slab_w is None:
        tn = _tile(n, tn)
        out_shape = [_sds((m, n), BF16)]
        out_specs = [pl.BlockSpec((tm, tn), lambda i, j, k: (i, j))]
    else:
        tn = _tile(slab_w, min(tn, slab_w))
        per = slab_w // tn
        out_shape = [_sds((n // slab_w, m, slab_w), BF16)]
        out_specs = [pl.BlockSpec((None, tm, tn), lambda i, j, k: (j // per, i, j % per))]
    grid = (m // tm, n // tn, s // tk)
    a_spec = pl.BlockSpec((tk, tm), lambda i, j, k: (k, i))
    b_spec = pl.BlockSpec((tk, tn), lambda i, j, k: (k, j))
    return _matmul(name, a, b, TN, grid, a_spec, b_spec, out_shape, out_specs, (tm, tn),
                   lambda acc: (acc,), after=after)[0]


def _mm_tn_half(name, a, b, core, own, by_rows, block, recv=None, after=None, tm=1024, tn=1024, tk=2048):
    s, m = a.shape
    n = b.shape[1]
    tk = _tile(s, tk)

    def owner(chip, core_ref):
        return 2 * chip + (core_ref[0] if own else 1 - core_ref[0])

    if by_rows:
        r, c = block, n
        tm, tn = _tile(r, min(tm, r)), _tile(c, tn)
        per = r // tm
        grid = (N_CHIPS * per, c // tn, s // tk)
        a_spec = pl.BlockSpec((tk, tm), lambda i, j, k, cr: (k, owner(i // per, cr) * per + i % per))
        b_spec = pl.BlockSpec((tk, tn), lambda i, j, k, cr: (k, j))
        o_spec = pl.BlockSpec((None, tm, tn), lambda i, j, k, cr: (i // per, i % per, j))
    else:
        r, c = m, block
        tm, tn = _tile(r, tm), _tile(c, min(tn, c))
        per = c // tn
        grid = (r // tm, N_CHIPS * per, s // tk)
        a_spec = pl.BlockSpec((tk, tm), lambda i, j, k, cr: (k, i))
        b_spec = pl.BlockSpec((tk, tn), lambda i, j, k, cr: (k, owner(j // per, cr) * per + j % per))
        o_spec = pl.BlockSpec((None, tm, tn), lambda i, j, k, cr: (j // per, i, j % per))
    if recv is None:
        extras, epilogue = (), lambda acc: (acc,)
    else:
        extras, epilogue = (recv,), lambda acc, other: (acc + other.astype(F32),)
    return _matmul(name, a, b, TN, grid, a_spec, b_spec, [_sds((N_CHIPS, r, c), BF16)], [o_spec], (tm, tn),
                   epilogue, extras, [o_spec] * len(extras), after=after, prefetch=[core])[0]


def _rms_fwd(name, h, g, after=None):
    s, d = h.shape
    tr = _tile(s, 256)

    def body(h_ref, g_ref, o_ref):
        x = h_ref[...]
        r = lax.rsqrt(jnp.mean(x * x, axis=-1, keepdims=True) + EPS)
        o_ref[...] = (x * r * g_ref[...]).astype(o_ref.dtype)

    row = pl.BlockSpec((tr, d), lambda i: (i, 0))
    vec = pl.BlockSpec((1, d), lambda i: (0, 0))
    return _pcall(name, body, [h, g], [row, vec], _sds((s, d), BF16), row, grid=(s // tr,),
                  sem=("parallel",), after=after)


def _accumulate(ref, part, step):
    @pl.when(step == 0)
    def _():
        ref[...] = part

    @pl.when(step > 0)
    def _():
        ref[...] += part


def _rms_bwd(name, dhn, h, g, dres, after=None):
    s, d = h.shape
    tr = _tile(s, 256)

    def body(dhn_ref, h_ref, g_ref, dres_ref, dh_ref, dhb_ref, gp_ref):
        x = h_ref[...]
        r = lax.rsqrt(jnp.mean(x * x, axis=-1, keepdims=True) + EPS)
        n = x * r
        dy = dhn_ref[...]
        dn = dy * g_ref[...]
        dh = dres_ref[...] + r * (dn - n * jnp.mean(dn * n, axis=-1, keepdims=True))
        dh_ref[...] = dh
        dhb_ref[...] = dh.astype(BF16)
        _accumulate(gp_ref, jnp.sum(dy * n, axis=0, keepdims=True), pl.program_id(0))

    row = pl.BlockSpec((tr, d), lambda i: (i, 0))
    vec = pl.BlockSpec((1, d), lambda i: (0, 0))
    return _pcall(name, body, [dhn, h, g, dres], [row, row, vec, row],
                  [_sds((s, d), F32), _sds((s, d), BF16), _sds((1, d), F32)], [row, row, vec],
                  grid=(s // tr,), sem=("arbitrary",), after=after)


def _loss_head(name, h, g, target, after=None):
    s, d = h.shape
    tr = _tile(s, 256)

    def body(h_ref, g_ref, t_ref, dh_ref, dhb_ref, gp_ref, loss_ref):
        x = h_ref[...]
        gg = g_ref[...]
        r = lax.rsqrt(jnp.mean(x * x, axis=-1, keepdims=True) + EPS)
        n = x * r
        e = n * gg - t_ref[...]
        dy = e * (1.0 / d)
        dn = dy * gg
        dh = r * (dn - n * jnp.mean(dn * n, axis=-1, keepdims=True))
        dh_ref[...] = dh
        dhb_ref[...] = dh.astype(BF16)
        step = pl.program_id(0)
        _accumulate(gp_ref, jnp.sum(dy * n, axis=0, keepdims=True), step)
        row_loss = jnp.mean(e * e, axis=-1, keepdims=True)
        _accumulate(loss_ref, 0.5 * jnp.sum(row_loss, axis=0, keepdims=True), step)

    row = pl.BlockSpec((tr, d), lambda i: (i, 0))
    vec = pl.BlockSpec((1, d), lambda i: (0, 0))
    one = pl.BlockSpec((1, 1), lambda i: (0, 0))
    return _pcall(name, body, [h, g, target], [row, vec, row],
                  [_sds((s, d), F32), _sds((s, d), BF16), _sds((1, d), F32), _sds((1, 1), F32)],
                  [row, row, vec, one], grid=(s // tr,), sem=("arbitrary",), after=after)


_SQRT_HALF = math.sqrt(0.5)
_INV_SQRT_2PI = 1.0 / math.sqrt(2.0 * math.pi)


def _gelu(x):
    return 0.5 * x * (1.0 + lax.erf(x * _SQRT_HALF))


def _gelu_grad(x):
    return 0.5 * (1.0 + lax.erf(x * _SQRT_HALF)) + x * jnp.exp(-0.5 * x * x) * _INV_SQRT_2PI


def _causal_mask():
    row = lax.broadcasted_iota(jnp.int32, (CHUNK, CHUNK), 0)
    col = lax.broadcasted_iota(jnp.int32, (CHUNK, CHUNK), 1)
    return row >= col


def _layernorm_parts(v):
    mu = jnp.mean(v, axis=-1, keepdims=True)
    xc = v - mu
    rstd = lax.rsqrt(jnp.mean(xc * xc, axis=-1, keepdims=True) + EPS)
    return xc * rstd, rstd


def _amix_fwd(name, pre, ln_g, ln_b, w_s, b_s_col, after=None):
    s, w2 = pre.shape
    w = w2 // 2
    head = w // A_GROUPS

    def body(pre_ref, g_ref, b_ref, ws_ref, bs_ref, o_ref):
        u = _gelu(pre_ref[:, :w])
        v = _gelu(pre_ref[:, w:])
        vhat, _ = _layernorm_parts(v)
        vn = (vhat * g_ref[...] + b_ref[...]).astype(BF16)
        mask = _causal_mask()
        for grp in range(A_GROUPS):
            cols = slice(grp * head, (grp + 1) * head)
            wm = jnp.where(mask, ws_ref[grp], 0.0).astype(BF16)
            sg = jnp.dot(wm, vn[:, cols], preferred_element_type=F32) + bs_ref[grp]
            o_ref[:, cols] = (u[:, cols] * sg).astype(o_ref.dtype)

    vec = pl.BlockSpec((1, w), lambda i: (0, 0))
    return _pcall(
        name, body, [pre, ln_g, ln_b, w_s, b_s_col],
        [pl.BlockSpec((CHUNK, w2), lambda i: (i, 0)), vec, vec,
         pl.BlockSpec((A_GROUPS, CHUNK, CHUNK), lambda i: (0, 0, 0)),
         pl.BlockSpec((A_GROUPS, CHUNK, 1), lambda i: (0, 0, 0))],
        _sds((s, w), BF16), pl.BlockSpec((CHUNK, w), lambda i: (i, 0)),
        grid=(s // CHUNK,), sem=("parallel",), after=after)


def _amix_bwd(name, pre, dgated, ln_g, ln_b, w_s, b_s_col, after=None):
    s, w2 = pre.shape
    w = w2 // 2
    head = w // A_GROUPS

    def body(pre_ref, dg_ref, g_ref, b_ref, ws_ref, bs_ref, dpre_ref, glg_ref, glb_ref, gws_ref, gbs_ref):
        step = pl.program_id(0)
        pre_u = pre_ref[:, :w]
        pre_v = pre_ref[:, w:]
        u = _gelu(pre_u)
        v = _gelu(pre_v)
        vhat, rstd = _layernorm_parts(v)
        gain = g_ref[...]
        vn = (vhat * gain + b_ref[...]).astype(BF16)
        dgated = dg_ref[...]
        ds = dgated * u
        dsb = ds.astype(BF16)
        mask = _causal_mask()
        du_parts = []
        dvn_parts = []
        for grp in range(A_GROUPS):
            cols = slice(grp * head, (grp + 1) * head)
            wm = jnp.where(mask, ws_ref[grp], 0.0).astype(BF16)
            sg = jnp.dot(wm, vn[:, cols], preferred_element_type=F32) + bs_ref[grp]
            du_parts.append(dgated[:, cols] * sg)
            gws = lax.dot_general(dsb[:, cols], vn[:, cols], NT, preferred_element_type=F32)
            gws = jnp.where(mask, gws, 0.0)
            gbs = jnp.sum(ds[:, cols], axis=-1, keepdims=True)

            @pl.when(step == 0)
            def _():
                gws_ref[grp] = gws
                gbs_ref[grp] = gbs

            @pl.when(step > 0)
            def _():
                gws_ref[grp] += gws
                gbs_ref[grp] += gbs

            dvn_parts.append(lax.dot_general(wm, dsb[:, cols], TN, preferred_element_type=F32))
        du = jnp.concatenate(du_parts, axis=-1)
        dvn = jnp.concatenate(dvn_parts, axis=-1)
        _accumulate(glg_ref, jnp.sum(dvn * vhat, axis=0, keepdims=True), step)
        _accumulate(glb_ref, jnp.sum(dvn, axis=0, keepdims=True), step)
        dvhat = dvn * gain
        dv = rstd * (dvhat - jnp.mean(dvhat, axis=-1, keepdims=True)
                     - vhat * jnp.mean(dvhat * vhat, axis=-1, keepdims=True))
        dpre_ref[:, :w] = (du * _gelu_grad(pre_u)).astype(dpre_ref.dtype)
        dpre_ref[:, w:] = (dv * _gelu_grad(pre_v)).astype(dpre_ref.dtype)

    vec = pl.BlockSpec((1, w), lambda i: (0, 0))
    ws_spec = pl.BlockSpec((A_GROUPS, CHUNK, CHUNK), lambda i: (0, 0, 0))
    bs_spec = pl.BlockSpec((A_GROUPS, CHUNK, 1), lambda i: (0, 0, 0))
    return _pcall(
        name, body, [pre, dgated, ln_g, ln_b, w_s, b_s_col],
        [pl.BlockSpec((CHUNK, w2), lambda i: (i, 0)), pl.BlockSpec((CHUNK, w), lambda i: (i, 0)),
         vec, vec, ws_spec, bs_spec],
        [_sds((s, w2), BF16), _sds((1, w), F32), _sds((1, w), F32),
         _sds((A_GROUPS, CHUNK, CHUNK), F32), _sds((A_GROUPS, CHUNK, 1), F32)],
        [pl.BlockSpec((CHUNK, w2), lambda i: (i, 0)), vec, vec, ws_spec, bs_spec],
        grid=(s // CHUNK,), sem=("arbitrary",), after=after)


def _shift_rows(x, k, forward):
    n = x.shape[0]
    row = lax.broadcasted_iota(jnp.int32, x.shape, 0)
    if forward:
        return jnp.where(row >= k, pltpu.roll(x, k, 0), 0.0)
    return jnp.where(row < n - k, pltpu.roll(x, n - k, 0), 0.0)


def _window_sum(x, window, forward):
    k = 1
    while k < window:
        x = x + _shift_rows(x, k, forward)
        k *= 2
    return x


def _pool(name, v, backward, after=None):
    s, w = v.shape
    head = w // B_GROUPS
    lane = _tile(head, 128)

    def body(v_ref, o_ref):
        grp = pl.program_id(0)
        x = v_ref[...]
        t = lax.broadcasted_iota(jnp.int32, x.shape, 0)
        for idx, window in enumerate(B_WINDOWS):
            @pl.when(grp == idx)
            def _():
                inv_count = 1.0 / jnp.minimum(t + 1, window).astype(F32)
                if backward:
                    out = _window_sum(x * inv_count, window, False) - x
                else:
                    out = _window_sum(x, window, True) * inv_count - x
                o_ref[...] = out.astype(o_ref.dtype)

    per = head // lane
    spec = pl.BlockSpec((s, lane), lambda g, j: (0, g * per + j))
    return _pcall(name, body, [v], [spec], _sds((s, w), BF16), spec, grid=(B_GROUPS, per),
                  sem=("parallel", "parallel"), after=after)


def _colsum(name, a, after=None):
    s, d = a.shape
    tr = _tile(s, 256)

    def body(a_ref, o_ref):
        _accumulate(o_ref, jnp.sum(a_ref[...], axis=0, keepdims=True), pl.program_id(0))

    return _pcall(name, body, [a], [pl.BlockSpec((tr, d), lambda i: (i, 0))], _sds((1, d), F32),
                  pl.BlockSpec((1, d), lambda i: (0, 0)), grid=(s // tr,), sem=("arbitrary",), after=after)


def _adamw(w, g, m, v):
    m = ADAM_B1 * m + (1.0 - ADAM_B1) * g
    v = ADAM_B2 * v + (1.0 - ADAM_B2) * (g * g)
    m_hat = m / (1.0 - ADAM_B1 ** ADAM_STEP)
    v_hat = v / (1.0 - ADAM_B2 ** ADAM_STEP)
    delta = -ADAM_LR * (m_hat / (jnp.sqrt(v_hat) + ADAM_EPS) + ADAM_WD * w)
    return delta, m, v


def _adam_rows(name, g, w, m, v, after=None):
    r, c = g.shape
    tr = _tile(r, 256)

    def body(g_ref, w_ref, m_ref, v_ref, d_ref, nm_ref, nv_ref):
        d_ref[...], nm_ref[...], nv_ref[...] = _adamw(w_ref[...], g_ref[...], m_ref[...], v_ref[...])

    spec = pl.BlockSpec((tr, c), lambda i: (i, 0))
    return _pcall(name, body, [g, w, m, v], [spec] * 4, [_sds((r, c), F32)] * 3, [spec] * 3,
                  grid=(r // tr,), sem=("parallel",), after=after)


def _position():
    return lax.axis_index("x"), lax.axis_index("y"), lax.axis_index("c")


def _other_chips(x, y):
    return [(1 - x, y), (x, 1 - y), (1 - x, 1 - y)]


def _slot(px, py, pc):
    return 4 * px + 2 * py + pc


def _hbm(a):
    return pltpu.with_memory_space_constraint(a, pltpu.HBM)


def _gather_copies(srcs, lands, send_sems, recv_sems):
    x, y, c = _position()
    peers = [(x, y, 1 - c)] + [(*chip, c) for chip in _other_chips(x, y)]
    mine = _slot(x, y, c)
    return [[pltpu.make_async_remote_copy(
        src_ref=srcs[t], dst_ref=lands[t].at[mine], send_sem=send_sems[t].at[k], recv_sem=recv_sems[t].at[k],
        device_id=peer, device_id_type=MESH) for k, peer in enumerate(peers)] for t in range(len(srcs))]


def _split_start(name, srcs, lands, copies, n_sems):
    n = len(srcs)

    def body(*refs):
        for per_tensor in copies(refs[:n], refs[n:2 * n], refs[2 * n:3 * n], refs[3 * n:4 * n]):
            for cp in per_tensor:
                cp.start()
        refs[-1][...] = jnp.zeros_like(refs[-1])

    out_shape = ([pltpu.SemaphoreType.DMA((n_sems,)) for _ in range(2 * n)]
                 + [pltpu.HBM(a.shape, a.dtype) for a in list(srcs) + list(lands)]
                 + [_sds((8, 128), F32)])
    out = pl.pallas_call(
        body, name=name, out_shape=out_shape, in_specs=[_HBM] * (2 * n),
        out_specs=[_SEM] * (2 * n) + [_HBM] * (2 * n) + [pl.BlockSpec(memory_space=pltpu.VMEM)],
        input_output_aliases={i: 2 * n + i for i in range(2 * n)},
        compiler_params=pltpu.CompilerParams(has_side_effects=_EFFECT),
    )(*[_hbm(a) for a in srcs], *[_hbm(a) for a in lands])
    return [(out[t], out[n + t], out[2 * n + t], out[3 * n + t]) for t in range(n)], out[-1]


def _split_wait(name, started, copies, after):
    n = len(started)

    def body(*refs):
        for per_tensor in copies(refs[:n], refs[n:2 * n], refs[2 * n:3 * n], refs[3 * n:4 * n]):
            for cp in per_tensor:
                cp.wait_send()
                cp.wait_recv()

    srcs = [e[2] for e in started]
    lands = [e[3] for e in started]
    out = pl.pallas_call(
        body, name=name, out_shape=[pltpu.HBM(a.shape, a.dtype) for a in srcs + lands],
        in_specs=[_HBM] * (2 * n) + [_SEM] * (2 * n) + [_ANY], out_specs=[_HBM] * (2 * n),
        input_output_aliases={i: i for i in range(2 * n)},
        compiler_params=pltpu.CompilerParams(has_side_effects=_EFFECT),
    )(*srcs, *lands, *[e[0] for e in started], *[e[1] for e in started], after)
    return out[:n], out[n:]


def _gather_start(name, shards):
    lands = [lax.empty((N_DEV,) + s.shape, s.dtype) for s in shards]
    return _split_start(name, shards, lands, _gather_copies, 4)


def _gather_wait(name, started, after):
    return _split_wait(name, started, _gather_copies, after)


def _sibling_copies(srcs, lands, send_sems, recv_sems):
    x, y, c = _position()
    return [[pltpu.make_async_remote_copy(
        src_ref=srcs[t], dst_ref=lands[t], send_sem=send_sems[t].at[0], recv_sem=recv_sems[t].at[0],
        device_id=(x, y, 1 - c), device_id_type=MESH)] for t in range(len(srcs))]


def _sibling_start(name, arrays):
    lands = [lax.empty(a.shape, a.dtype) for a in arrays]
    return _split_start(name, arrays, lands, _sibling_copies, 1)


def _sibling_wait(name, started, after):
    return _split_wait(name, started, _sibling_copies, after)[1]


def _small_copies(srcs, lands, send_sems, recv_sems):
    x, y, c = _position()
    mine = _slot(x, y, c)
    peers = [(x ^ ((k >> 2) & 1), y ^ ((k >> 1) & 1), c ^ (k & 1)) for k in range(1, N_DEV)]
    return [[pltpu.make_async_remote_copy(
        src_ref=srcs[t], dst_ref=lands[t].at[mine], send_sem=send_sems[t].at[k], recv_sem=recv_sems[t].at[k],
        device_id=peer, device_id_type=MESH) for k, peer in enumerate(peers)] for t in range(len(srcs))]


def _gather_finish(name, shards, lands, after):
    n = len(shards)

    def body(*refs):
        srcs, lands_in, outs = refs[:n], refs[n:2 * n], refs[2 * n:3 * n]
        send_sems, recv_sems, local_sems = refs[3 * n:]
        x, y, c = _position()
        local = [pltpu.make_async_copy(srcs[t], outs[t].at[_slot(x, y, c)], local_sems.at[t]) for t in range(n)]
        copies = []
        for t in range(n):
            for j, chip in enumerate(_other_chips(x, y)):
                block = outs[t].at[_slot(*chip, c)]
                copies.append(pltpu.make_async_remote_copy(
                    src_ref=block, dst_ref=block, send_sem=send_sems.at[t, j], recv_sem=recv_sems.at[t, j],
                    device_id=(x, y, 1 - c), device_id_type=MESH))
        for cp in local + copies:
            cp.start()
        for cp in copies:
            cp.wait_send()
        for t in range(n):
            for j, chip in enumerate(_other_chips(x, y)):
                block = outs[t].at[_slot(*chip, 1 - c)]
                pltpu.make_async_remote_copy(
                    src_ref=block, dst_ref=block, send_sem=send_sems.at[t, j], recv_sem=recv_sems.at[t, j],
                    device_id=(x, y, 1 - c), device_id_type=MESH).wait_recv()
        for cp in local:
            cp.wait()

    return _pcall(name, body, [*shards, *lands], [_ANY] * (2 * n),
                  [_sds(l.shape, l.dtype) for l in lands], [_ANY] * n,
                  scratch=[pltpu.SemaphoreType.DMA((n, 3)), pltpu.SemaphoreType.DMA((n, 3)),
                           pltpu.SemaphoreType.DMA((n,))],
                  after=after, aliases={n + t: t for t in range(n)})


def _exchange_sibling(name, fulls, after):
    n = len(fulls)

    def body(*refs):
        src = refs[:n]
        out = refs[n:2 * n]
        send_sems, recv_sems = refs[2 * n:]
        x, y, c = _position()
        copies = [pltpu.make_async_remote_copy(
            src_ref=src[t].at[:, 1 - c], dst_ref=out[t], send_sem=send_sems.at[t], recv_sem=recv_sems.at[t],
            device_id=(x, y, 1 - c), device_id_type=MESH) for t in range(n)]
        for cp in copies:
            cp.start()
        for cp in copies:
            cp.wait()

    return _pcall(name, body, fulls, [_ANY] * n, [_sds((N_CHIPS,) + f.shape[2:], f.dtype) for f in fulls],
                  [_ANY] * n, scratch=[pltpu.SemaphoreType.DMA((n,)), pltpu.SemaphoreType.DMA((n,))],
                  after=after)


def _add_sibling(name, full, recv, core, after):
    _, _, r, c = full.shape
    tr = _tile(r, max(8, (256 * 1024) // c))

    def body(core_ref, f_ref, r_ref, o_ref):
        o_ref[...] = (f_ref[...].astype(F32) + r_ref[...].astype(F32)).astype(o_ref.dtype)

    return _pcall(
        name, body, [full, recv],
        [pl.BlockSpec((None, None, tr, c), lambda p, i, core_ref: (p, core_ref[0], i, 0)),
         pl.BlockSpec((None, tr, c), lambda p, i, core_ref: (p, i, 0))],
        _sds((N_CHIPS, r, c), BF16), pl.BlockSpec((None, tr, c), lambda p, i, core_ref: (p, i, 0)),
        grid=(N_CHIPS, r // tr), sem=("parallel", "parallel"), prefetch=[core], after=after)


def _scatter_copies(srcs, lands, send_sems, recv_sems):
    x, y, c = _position()
    return [[pltpu.make_async_remote_copy(
        src_ref=srcs[t].at[2 * px + py], dst_ref=lands[t].at[j],
        send_sem=send_sems[t].at[j], recv_sem=recv_sems[t].at[j],
        device_id=(px, py, c), device_id_type=MESH) for j, (px, py) in enumerate(_other_chips(x, y))]
        for t in range(len(srcs))]


def _scatter_start(name, partials):
    lands = [lax.empty((N_CHIPS - 1,) + p.shape[1:], p.dtype) for p in partials]
    return _split_start(name, partials, lands, _scatter_copies, 3)


def _scatter_wait(name, started, after):
    return _split_wait(name, started, _scatter_copies, after)


def _reduce_adam(name, partial, recv, chip, w, m, v, layer, carried, after):
    n_layers, r, c = w.shape
    tr = _tile(r, max(8, (128 * 1024) // c))

    def body(chip_ref, p_ref, r_ref, w_ref, m_ref, v_ref, *rest):
        g_ref, d_ref, nm_ref, nv_ref = rest[-4:]
        g = p_ref[...].astype(F32)
        for j in range(N_CHIPS - 1):
            g = g + r_ref[j].astype(F32)
        g_ref[...] = g
        d_ref[...], nm_ref[...], nv_ref[...] = _adamw(w_ref[...], g, m_ref[...], v_ref[...])

    layered = pl.BlockSpec((None, tr, c), lambda i, chip_ref: (layer, i, 0))
    in_specs = [pl.BlockSpec((None, tr, c), lambda i, chip_ref: (chip_ref[0], i, 0)),
                pl.BlockSpec((N_CHIPS - 1, tr, c), lambda i, chip_ref: (0, i, 0)),
                layered, layered, layered]
    operands = [partial, recv, w, m, v]
    aliases = {}
    if carried is not None:
        operands += list(carried)
        in_specs += [_ANY] * 4
        aliases = {1 + 5 + o: o for o in range(4)}
    return _pcall(name, body, operands, in_specs, [_sds((n_layers, r, c), F32)] * 4, [layered] * 4,
                  grid=(r // tr,), sem=("parallel",), prefetch=[chip], after=after, aliases=aliases)


def _small_sum(name, gathered, own, device, after=None):
    r, lanes = own.shape

    def body(dev_ref, g_ref, own_ref, out_ref):
        dev = dev_ref[0]
        mine = own_ref[...]
        total = jnp.where(dev == 0, mine, g_ref[0])
        for d in range(1, N_DEV):
            total = total + jnp.where(dev == d, mine, g_ref[d])
        out_ref[...] = total

    return _pcall(name, body, [gathered, own],
                  [pl.BlockSpec((N_DEV, r, lanes), lambda i, dev_ref: (0, 0, 0)),
                   pl.BlockSpec((r, lanes), lambda i, dev_ref: (0, 0))],
                  _sds((r, lanes), F32), pl.BlockSpec((r, lanes), lambda i, dev_ref: (0, 0)),
                  grid=(1,), sem=("arbitrary",), prefetch=[device], after=after)


def _pack(arrays):
    return jnp.concatenate([a.reshape(-1, 128) for a in arrays], axis=0)


def _unpack(packed, shapes):
    out, row = [], 0
    for shape in shapes:
        rows = math.prod(shape) // 128
        out.append(packed[row:row + rows].reshape(shape))
        row += rows
    return out


class _Order:
    def __init__(self):
        self.last = None

    def __call__(self, fn, *args, **kwargs):
        out = fn(*args, after=self.last, **kwargs)
        self.last = out[0] if isinstance(out, (list, tuple)) else out
        return out


def kernel(x, a_w_in, a_ln_g, a_ln_b, a_w_s, a_b_s, a_w_out, b_w_in, b_w_grp, b_scale, b_w_out, norm_mix, norm_mlp, mlp_w1, mlp_w2, final_norm, loss_target, m_a_w_in, m_a_ln_g, m_a_ln_b, m_a_w_s, m_a_b_s, m_a_w_out, m_b_w_in, m_b_w_grp, m_b_scale, m_b_w_out, m_norm_mix, m_norm_mlp, m_mlp_w1, m_mlp_w2, m_final_norm, v_a_w_in, v_a_ln_g, v_a_ln_b, v_a_w_s, v_a_b_s, v_a_w_out, v_b_w_in, v_b_w_grp, v_b_scale, v_b_w_out, v_norm_mix, v_norm_mlp, v_mlp_w1, v_mlp_w2, v_final_norm):
    s, d = x.shape[1], x.shape[2]
    depth = mlp_w1.shape[0]
    a_slab = a_w_in.shape[2]
    ff_slab = mlp_w1.shape[2]
    ff_rows = mlp_w2.shape[1]
    bh = b_w_grp.shape[3]
    my_x, my_y, my_c = _position()
    core = jnp.reshape(my_c, (1,)).astype(jnp.int32)
    chip = jnp.reshape(2 * my_x + my_y, (1,)).astype(jnp.int32)
    device = _slot(my_x, my_y, my_c)
    run = _Order()

    w1_b, w2_b = mlp_w1.astype(BF16), mlp_w2.astype(BF16)
    shards = [a_w_in[0].astype(BF16), a_w_out[0].astype(BF16), b_scale,
              w1_b[0], w2_b[0],
              b_w_in[0].astype(BF16), b_w_grp[0].astype(BF16), b_w_out[0].astype(BF16),
              w1_b[1], w2_b[1]]
    started, token = _gather_start("weights_gather_start", shards)
    run.last = token

    def gathered(name, indices):
        srcs, lands = _gather_wait(name + "_wait", [started[t] for t in indices], run.last)
        run.last = srcs[0]
        return run(_gather_finish, name + "_finish", srcs, lands)

    h0 = x[0]
    target = loss_target[0]
    ln_g, ln_b = a_ln_g, a_ln_b
    w_s = a_w_s[0]
    b_s_col = a_b_s[0][:, :, None]
    nmix = [norm_mix[l][None, :] for l in range(depth)]
    nmlp = [norm_mlp[l][None, :] for l in range(depth)]

    def mlp_forward(l, h, gather_indices):
        hn = run(_rms_fwd, f"mlp{l}_norm", h, nmlp[l])
        (w1,) = gathered(f"mlp{l}_up_weights", gather_indices[:1])
        act, act_sq = run(_mm_nn, f"mlp{l}_up", hn, w1,
                          lambda acc: (jnp.maximum(acc, 0.0), jnp.square(jnp.maximum(acc, 0.0))),
                          (BF16, BF16), slab=True)
        (w2,) = gathered(f"mlp{l}_down_weights", gather_indices[1:])
        w2 = w2.reshape(-1, d)
        (h_out,) = run(_mm_nn, f"mlp{l}_down", act_sq, w2, lambda acc, res: (acc + res,), (F32,),
                       extras=(h,), extra_kinds=("tile",))
        return h_out, (h, hn, act, act_sq, w1, w2)

    scattered = []

    def scatter_partials(name, partials, specs):
        in_flight, tok = _scatter_start(name + "_scatter_start", partials)
        run.last = tok
        scattered.append((name, in_flight, specs))

    def scatter(name, group):
        fulls = [g.reshape(N_CHIPS, 2, r, c) for g, r, c, _ in group]
        from_sibling = run(_exchange_sibling, name + "_to_sibling", fulls)
        partials = [run(_add_sibling, f"{name}_add_sibling_{t}", fulls[t], from_sibling[t], core)
                    for t in range(len(group))]
        scatter_partials(name, partials, [spec for _, _, _, spec in group])

    def weight_grad(name, a, b, by_rows, block, between):
        other = run(_mm_tn_half, name + "_other", a, b, core, False, by_rows, block)
        sent, tok = _sibling_start(name + "_sibling_start", [other])
        run.last = tok
        middle = between()
        (recv,) = _sibling_wait(name + "_sibling_wait", sent, run.last)
        run.last = recv
        return run(_mm_tn_half, name + "_own", a, b, core, True, by_rows, block, recv=recv), middle

    def mlp_backward(l, saved, dh, dhb):
        h, hn, act, act_sq, w1, w2 = saved
        part_w2, (dpre,) = weight_grad(
            f"mlp{l}_down_dw", act_sq, dhb, True, ff_rows,
            lambda: run(_mm_nt, f"mlp{l}_down_dx", dhb, w2, lambda acc, a: (2.0 * a.astype(F32) * acc,),
                        (BF16,), extras=(act,), extra_kinds=("tile",)))
        scatter_partials(f"mlp{l}_down_grads", [part_w2], [("mlp_w2", l)])
        part_w1, (dhn,) = weight_grad(
            f"mlp{l}_up_dw", hn, dpre, False, ff_slab,
            lambda: run(_mm_nt, f"mlp{l}_up_dx", dpre, w1, lambda acc: (acc,), (F32,), slab=True))
        scatter_partials(f"mlp{l}_up_grads", [part_w1], [("mlp_w1", l)])
        dh, dhb, g_norm = run(_rms_bwd, f"mlp{l}_norm_bwd", dhn, h, nmlp[l], dh)
        return dh, dhb, g_norm

    hn0 = run(_rms_fwd, "mix0_norm", h0, nmix[0])
    (wa_in,) = gathered("mixa_in_weights", [0])
    (pre,) = run(_mm_nn, "mixa_in", hn0, wa_in, lambda acc: (acc,), (F32,), slab=True)
    wa_out, scale = gathered("mixa_out_weights", [1, 2])
    wa_out, scale = wa_out.reshape(d, d), scale.reshape(1, d)
    gated = run(_amix_fwd, "mixa_gate", pre, ln_g, ln_b, w_s, b_s_col)
    (h1,) = run(_mm_nn, "mixa_out", gated, wa_out, lambda acc, res: (acc + res,), (F32,),
                extras=(h0,), extra_kinds=("tile",))
    h2, saved_mlp0 = mlp_forward(0, h1, [3, 4])
    hn2 = run(_rms_fwd, "mix1_norm", h2, nmix[1])
    wb_in, wb_grp, wb_out = gathered("mixb_weights", [5, 6, 7])
    wb_in, wb_out = wb_in.reshape(d, d), wb_out.reshape(d, d)
    wb_grp = jnp.transpose(wb_grp, (1, 0, 2, 3)).reshape(B_GROUPS, bh, bh)
    (vb,) = run(_mm_nn, "mixb_in", hn2, wb_in, lambda acc: (acc,), (F32,))
    pooled = run(_pool, "mixb_pool", vb, backward=False)
    tm = _tile(s, 1024)
    grp_tile = pl.BlockSpec((tm, bh), lambda i, j, k: (i, j))
    grp_weight = pl.BlockSpec((None, bh, bh), lambda i, j, k: (j, 0, 0))
    mixed, mixed_scaled = run(
        _matmul, "mixb_grp", pooled, wb_grp, NN, (s // tm, B_GROUPS, 1), grp_tile, grp_weight,
        [_sds((s, d), BF16), _sds((s, d), BF16)], [grp_tile] * 2,
        (tm, bh), lambda acc, sc: (acc, acc * sc), (scale,), [pl.BlockSpec((1, bh), lambda i, j, k: (0, j))])
    (h3,) = run(_mm_nn, "mixb_out", mixed_scaled, wb_out, lambda acc, res: (acc + res,), (F32,),
                extras=(h2,), extra_kinds=("tile",))
    h4, saved_mlp1 = mlp_forward(1, h3, [8, 9])
    dh, dhb, g_final, loss_part = run(_loss_head, "loss_head", h4, final_norm[None, :], target)

    dh, dhb, g_nmlp1 = mlp_backward(1, saved_mlp1, dh, dhb)
    g_wb_out = run(_mm_tn, "mixb_out_dw", mixed_scaled, dhb)
    dms_scaled, dms_mixed = run(
        _mm_nt, "mixb_out_dx", dhb, wb_out,
        lambda acc, sc, mx: (acc * sc, acc * mx.astype(F32)), (BF16, F32),
        extras=(scale, mixed), extra_kinds=("row", "tile"))
    g_scale = run(_colsum, "mixb_scale_dw", dms_mixed)
    tks = _tile(s, 1024)
    grp_rows = pl.BlockSpec((tks, bh), lambda i, j, k: (k, j))
    (g_wb_grp,) = run(
        _matmul, "mixb_grp_dw", pooled, dms_scaled, TN, (1, B_GROUPS, s // tks), grp_rows, grp_rows,
        [_sds((B_GROUPS, bh, bh), BF16)], [grp_weight], (bh, bh), lambda acc: (acc,))
    (dpooled,) = run(
        _matmul, "mixb_grp_dx", dms_scaled, wb_grp, NT, (s // tm, B_GROUPS, 1), grp_tile, grp_weight,
        [_sds((s, d), F32)], [grp_tile], (tm, bh), lambda acc: (acc,))
    dvb = run(_pool, "mixb_pool_bwd", dpooled, backward=True)
    g_wb_in = run(_mm_tn, "mixb_in_dw", hn2, dvb)
    g_wb_grp_slabs = jnp.transpose(g_wb_grp.reshape(B_GROUPS, N_DEV, bh // N_DEV, bh), (1, 0, 2, 3))
    scatter("mixb_grads", [(g_wb_out, d // N_DEV, d, ("b_w_out", 0)),
                           (g_wb_grp_slabs, B_GROUPS * bh // N_DEV, bh, ("b_w_grp", 0)),
                           (g_wb_in, d // N_DEV, d, ("b_w_in", 0))])
    (dhn2,) = run(_mm_nt, "mixb_in_dx", dvb, wb_in, lambda acc: (acc,), (F32,))
    dh, dhb, g_nmix1 = run(_rms_bwd, "mix1_norm_bwd", dhn2, h2, nmix[1], dh)
    dh, dhb, g_nmlp0 = mlp_backward(0, saved_mlp0, dh, dhb)
    g_wa_out = run(_mm_tn, "mixa_out_dw", gated, dhb)
    (dgated,) = run(_mm_nt, "mixa_out_dx", dhb, wa_out, lambda acc: (acc,), (F32,))
    dpre, g_ln_g, g_ln_b, g_w_s, g_b_s = run(_amix_bwd, "mixa_gate_bwd", pre, dgated, ln_g, ln_b, w_s, b_s_col)
    g_wa_in = run(_mm_tn, "mixa_in_dw", hn0, dpre, slab_w=a_slab)
    scatter("mixa_grads", [(g_wa_in, d, a_slab, ("a_w_in", 0)), (g_wa_out, d // N_DEV, d, ("a_w_out", 0))])
    (dhn0,) = run(_mm_nt, "mixa_in_dx", dpre, wa_in, lambda acc: (acc,), (F32,), slab=True)
    grad_x, _, g_nmix0 = run(_rms_bwd, "mix0_norm_bwd", dhn0, h0, nmix[0], dh)

    g_norm_mix = jnp.concatenate([g_nmix0, g_nmix1], axis=0)
    g_norm_mlp = jnp.concatenate([g_nmlp0, g_nmlp1], axis=0)
    small_parts = [g_ln_g, g_ln_b, g_w_s, g_b_s, g_norm_mix, g_norm_mlp, g_final, g_scale]
    packed = _pack(small_parts)
    small_sent, tok = _split_start("small_grads_start", [packed], [lax.empty((N_DEV,) + packed.shape, F32)],
                                   _small_copies, N_DEV - 1)
    run.last = tok

    weights = {"a_w_in": (a_w_in, m_a_w_in, v_a_w_in), "a_w_out": (a_w_out, m_a_w_out, v_a_w_out),
               "b_w_in": (b_w_in, m_b_w_in, v_b_w_in), "b_w_grp": (b_w_grp, m_b_w_grp, v_b_w_grp),
               "b_w_out": (b_w_out, m_b_w_out, v_b_w_out), "mlp_w1": (mlp_w1, m_mlp_w1, v_mlp_w1),
               "mlp_w2": (mlp_w2, m_mlp_w2, v_mlp_w2)}
    results = {}

    def finish_group(name, in_flight, specs):
        partials, lands = _scatter_wait(name + "_scatter_wait", in_flight, run.last)
        run.last = lands[0]
        for t, (wname, layer) in enumerate(specs):
            w, m, v = weights[wname]
            _, r, c = partials[t].shape
            layers = w.shape[0]
            results[wname] = run(_reduce_adam, f"{name}_reduce_adam_{t}", partials[t], lands[t], chip,
                                 w.reshape(layers, r, c), m.reshape(layers, r, c), v.reshape(layers, r, c),
                                 layer, results.get(wname))

    for group in scattered[:-1]:
        finish_group(*group)
    own_packed, small_gathered = _split_wait("small_grads_wait", small_sent, _small_copies, run.last)
    run.last = small_gathered[0]
    small_sum = run(_small_sum, "small_grads_sum", small_gathered[0], own_packed[0],
                    jnp.reshape(device, (1,)).astype(jnp.int32))
    sg = _unpack(small_sum, [a_ln_g.shape, a_ln_b.shape, a_w_s.shape, a_b_s.shape, norm_mix.shape,
                             norm_mlp.shape, final_norm.shape, (1, d)])
    shard = b_scale.shape[1]
    sg[7] = lax.dynamic_slice(sg[7], (0, device * shard), (1, shard))
    small_w = [a_ln_g, a_ln_b, a_w_s, a_b_s, norm_mix, norm_mlp, final_norm, b_scale]
    small_m = [m_a_ln_g, m_a_ln_b, m_a_w_s, m_a_b_s, m_norm_mix, m_norm_mlp, m_final_norm, m_b_scale]
    small_v = [v_a_ln_g, v_a_ln_b, v_a_w_s, v_a_b_s, v_norm_mix, v_norm_mlp, v_final_norm, v_b_scale]
    small_out = run(_adam_rows, "small_adam", _pack(sg), _pack(small_w), _pack(small_m), _pack(small_v))
    shapes = [w.shape for w in small_w]
    small_res = [sg] + [_unpack(o, shapes) for o in small_out]

    finish_group(*scattered[-1])
    big = {wname: [o.reshape(weights[wname][0].shape) for o in outs] for wname, outs in results.items()}

    loss = lax.psum(loss_part[0, 0], ("x", "y", "c"))

    def leaf(o):
        return (big["a_w_in"][o], small_res[o][0], small_res[o][1], small_res[o][2], small_res[o][3],
                big["a_w_out"][o], big["b_w_in"][o], big["b_w_grp"][o], small_res[o][7], big["b_w_out"][o],
                small_res[o][4], small_res[o][5], big["mlp_w1"][o], big["mlp_w2"][o], small_res[o][6])

    return (loss, grad_x[None], *leaf(0), *leaf(1), *leaf(2), *leaf(3))
```

```python
import math

import jax
import jax.numpy as jnp
from jax import lax
from jax.experimental import pallas as pl
from jax.experimental.pallas import tpu as pltpu

F32 = jnp.float32
BF16 = jnp.bfloat16
MESH = pl.DeviceIdType.MESH

N_DEV = 8
N_CHIPS = 4
CHUNK = 128
A_GROUPS = 8
B_WINDOWS = (2, 4, 8, 16)
B_GROUPS = len(B_WINDOWS)
EPS = 1e-6
ADAM_LR = 0.001
ADAM_B1 = 0.9
ADAM_B2 = 0.999
ADAM_EPS = 1e-08
ADAM_WD = 0.01
ADAM_STEP = 10

VMEM_LIMIT = 48 * 1024 * 1024

NN = (((1,), (0,)), ((), ()))
NT = (((1,), (1,)), ((), ()))
TN = (((0,), (0,)), ((), ()))

_ANY = pl.BlockSpec(memory_space=pl.ANY)
_HBM = pl.BlockSpec(memory_space=pltpu.HBM)
_SEM = pl.BlockSpec(memory_space=pltpu.SEMAPHORE)
_EFFECT = pltpu.SideEffectType.DATAFLOW_SIDE_EFFECTING


def _tile(n, pref):
    return pref if n % pref == 0 else n


def _sds(shape, dtype):
    return jax.ShapeDtypeStruct(shape, dtype)


def _pcall(name, body, operands, in_specs, out_shape, out_specs, *, grid=None, sem=None, scratch=(),
           prefetch=(), after=None, aliases=None):
    after = [] if after is None else [after]
    n_lead = len(prefetch) + len(operands)
    n_after = len(after)

    def wrapped(*refs):
        body(*refs[:n_lead], *refs[n_lead + n_after:])

    in_specs = list(in_specs) + [_ANY] * n_after
    params = pltpu.CompilerParams(vmem_limit_bytes=VMEM_LIMIT) if sem is None else \
        pltpu.CompilerParams(dimension_semantics=sem, vmem_limit_bytes=VMEM_LIMIT)
    kwargs = dict(out_shape=out_shape, scratch_shapes=list(scratch), compiler_params=params, name=name,
                  input_output_aliases=aliases or {})
    if prefetch:
        kwargs["grid_spec"] = pltpu.PrefetchScalarGridSpec(
            num_scalar_prefetch=len(prefetch), grid=grid, in_specs=in_specs, out_specs=out_specs,
            scratch_shapes=list(scratch))
        kwargs.pop("scratch_shapes")
    else:
        kwargs.update(in_specs=in_specs, out_specs=out_specs)
        if grid is not None:
            kwargs["grid"] = grid
    return pl.pallas_call(wrapped, **kwargs)(*prefetch, *operands, *after)


def _matmul(name, a, b, dims, grid, a_spec, b_spec, out_shape, out_specs, acc_shape,
            epilogue, extras=(), extra_specs=(), after=None, prefetch=()):
    nk = grid[2]
    n_extra = len(extras)
    n_out = len(out_shape)
    n_pre = len(prefetch)

    def body(*refs):
        refs = refs[n_pre:]
        a_ref, b_ref = refs[0], refs[1]
        extra_refs = refs[2:2 + n_extra]
        out_refs = refs[2 + n_extra:2 + n_extra + n_out]

        def finish(acc):
            outs = epilogue(acc, *[r[...] for r in extra_refs])
            for o_ref, o in zip(out_refs, outs):
                o_ref[...] = o.astype(o_ref.dtype)

        def product():
            return lax.dot_general(a_ref[...], b_ref[...], dims, preferred_element_type=F32)

        if nk == 1:
            finish(product())
        else:
            acc_ref = refs[-1]
            k = pl.program_id(2)

            @pl.when(k == 0)
            def _():
                acc_ref[...] = product()

            if nk > 2:
                @pl.when(jnp.logical_and(k > 0, k < nk - 1))
                def _():
                    acc_ref[...] += product()

            @pl.when(k == nk - 1)
            def _():
                finish(acc_ref[...] + product())

    scratch = [] if nk == 1 else [pltpu.VMEM(acc_shape, F32)]
    return _pcall(name, body, [a, b, *extras], [a_spec, b_spec, *extra_specs], out_shape, out_specs,
                  grid=grid, sem=("parallel", "parallel", "arbitrary"), scratch=scratch, after=after,
                  prefetch=prefetch)


def _mm_nn(name, a, b, epilogue, out_dtypes, extras=(), extra_kinds=(), slab=False, after=None,
           tm=1024, tn=1024, tk=2048):
    m, kd = a.shape
    if slab:
        n_slab, _, w = b.shape
        n = n_slab * w
        tn = _tile(w, min(tn, w))
        per = w // tn
        tk = _tile(kd, tk)
        b_spec = pl.BlockSpec((None, tk, tn), lambda i, j, k: (j // per, k, j % per))
    else:
        n = b.shape[1]
        tn = _tile(n, tn)
        tk = _tile(kd, tk)
        b_spec = pl.BlockSpec((tk, tn), lambda i, j, k: (k, j))
    tm = _tile(m, tm)
    grid = (m // tm, n // tn, kd // tk)
    a_spec = pl.BlockSpec((tm, tk), lambda i, j, k: (i, k))
    tile_spec = pl.BlockSpec((tm, tn), lambda i, j, k: (i, j))
    row_spec = pl.BlockSpec((1, tn), lambda i, j, k: (0, j))
    extra_specs = [tile_spec if kind == "tile" else row_spec for kind in extra_kinds]
    return _matmul(name, a, b, NN, grid, a_spec, b_spec,
                   [_sds((m, n), d) for d in out_dtypes], [tile_spec for _ in out_dtypes],
                   (tm, tn), epilogue, extras, extra_specs, after=after)


def _mm_nt(name, a, b, epilogue, out_dtypes, extras=(), extra_kinds=(), slab=False, after=None,
           tm=1024, tn=1024, tk=2048):
    m, kd = a.shape
    if slab:
        n_slab, n, w = b.shape
        tk = _tile(w, min(tk, w))
        per = w // tk
        tn = _tile(n, tn)
        b_spec = pl.BlockSpec((None, tn, tk), lambda i, j, k: (k // per, j, k % per))
    else:
        n = b.shape[0]
        tn = _tile(n, tn)
        tk = _tile(kd, tk)
        b_spec = pl.BlockSpec((tn, tk), lambda i, j, k: (j, k))
    tm = _tile(m, tm)
    grid = (m // tm, n // tn, kd // tk)
    a_spec = pl.BlockSpec((tm, tk), lambda i, j, k: (i, k))
    tile_spec = pl.BlockSpec((tm, tn), lambda i, j, k: (i, j))
    row_spec = pl.BlockSpec((1, tn), lambda i, j, k: (0, j))
    extra_specs = [tile_spec if kind == "tile" else row_spec for kind in extra_kinds]
    return _matmul(name, a, b, NT, grid, a_spec, b_spec,
                   [_sds((m, n), d) for d in out_dtypes], [tile_spec for _ in out_dtypes],
                   (tm, tn), epilogue, extras, extra_specs, after=after)


def _mm_tn_half(name, a, b, core, own, by_rows, block, recv=None, after=None, tm=1024, tn=1024, tk=2048):
    s, m = a.shape
    n = b.shape[1]
    tk = _tile(s, tk)

    def owner(chip, core_ref):
        return 2 * chip + (core_ref[0] if own else 1 - core_ref[0])

    if by_rows:
        r, c = block, n
        tm, tn = _tile(r, min(tm, r)), _tile(c, tn)
        per = r // tm
        grid = (N_CHIPS * per, c // tn, s // tk)
        a_spec = pl.BlockSpec((tk, tm), lambda i, j, k, cr: (k, owner(i // per, cr) * per + i % per))
        b_spec = pl.BlockSpec((tk, tn), lambda i, j, k, cr: (k, j))
        o_spec = pl.BlockSpec((None, tm, tn), lambda i, j, k, cr: (i // per, i % per, j))
    else:
        r, c = m, block
        tm, tn = _tile(r, tm), _tile(c, min(tn, c))
        per = c // tn
        grid = (r // tm, N_CHIPS * per, s // tk)
        a_spec = pl.BlockSpec((tk, tm), lambda i, j, k, cr: (k, i))
        b_spec = pl.BlockSpec((tk, tn), lambda i, j, k, cr: (k, owner(j // per, cr) * per + j % per))
        o_spec = pl.BlockSpec((None, tm, tn), lambda i, j, k, cr: (j // per, i, j % per))
    if recv is None:
        extras, epilogue = (), lambda acc: (acc,)
    else:
        extras, epilogue = (recv,), lambda acc, other: (acc + other.astype(F32),)
    return _matmul(name, a, b, TN, grid, a_spec, b_spec, [_sds((N_CHIPS, r, c), BF16)], [o_spec], (tm, tn),
                   epilogue, extras, [o_spec] * len(extras), after=after, prefetch=[core])[0]


def _rms_fwd(name, h, g, after=None):
    s, d = h.shape
    tr = _tile(s, 256)

    def body(h_ref, g_ref, o_ref):
        x = h_ref[...]
        r = lax.rsqrt(jnp.mean(x * x, axis=-1, keepdims=True) + EPS)
        o_ref[...] = (x * r * g_ref[...]).astype(o_ref.dtype)

    row = pl.BlockSpec((tr, d), lambda i: (i, 0))
    vec = pl.BlockSpec((1, d), lambda i: (0, 0))
    return _pcall(name, body, [h, g], [row, vec], _sds((s, d), BF16), row, grid=(s // tr,),
                  sem=("parallel",), after=after)


def _accumulate(ref, part, step):
    @pl.when(step == 0)
    def _():
        ref[...] = part

    @pl.when(step > 0)
    def _():
        ref[...] += part


def _rms_bwd(name, dhn, h, g, dres, after=None):
    s, d = h.shape
    tr = _tile(s, 256)

    def body(dhn_ref, h_ref, g_ref, dres_ref, dh_ref, dhb_ref, gp_ref):
        x = h_ref[...]
        r = lax.rsqrt(jnp.mean(x * x, axis=-1, keepdims=True) + EPS)
        n = x * r
        dy = dhn_ref[...]
        dn = dy * g_ref[...]
        dh = dres_ref[...] + r * (dn - n * jnp.mean(dn * n, axis=-1, keepdims=True))
        dh_ref[...] = dh
        dhb_ref[...] = dh.astype(BF16)
        _accumulate(gp_ref, jnp.sum(dy * n, axis=0, keepdims=True), pl.program_id(0))

    row = pl.BlockSpec((tr, d), lambda i: (i, 0))
    vec = pl.BlockSpec((1, d), lambda i: (0, 0))
    return _pcall(name, body, [dhn, h, g, dres], [row, row, vec, row],
                  [_sds((s, d), F32), _sds((s, d), BF16), _sds((1, d), F32)], [row, row, vec],
                  grid=(s // tr,), sem=("arbitrary",), after=after)


def _loss_head(name, h, g, target, after=None):
    s, d = h.shape
    tr = _tile(s, 256)

    def body(h_ref, g_ref, t_ref, dh_ref, dhb_ref, gp_ref, loss_ref):
        x = h_ref[...]
        gg = g_ref[...]
        r = lax.rsqrt(jnp.mean(x * x, axis=-1, keepdims=True) + EPS)
        n = x * r
        e = n * gg - t_ref[...]
        dy = e * (1.0 / d)
        dn = dy * gg
        dh = r * (dn - n * jnp.mean(dn * n, axis=-1, keepdims=True))
        dh_ref[...] = dh
        dhb_ref[...] = dh.astype(BF16)
        step = pl.program_id(0)
        _accumulate(gp_ref, jnp.sum(dy * n, axis=0, keepdims=True), step)
        row_loss = jnp.mean(e * e, axis=-1, keepdims=True)
        _accumulate(loss_ref, 0.5 * jnp.sum(row_loss, axis=0, keepdims=True), step)

    row = pl.BlockSpec((tr, d), lambda i: (i, 0))
    vec = pl.BlockSpec((1, d), lambda i: (0, 0))
    one = pl.BlockSpec((1, 1), lambda i: (0, 0))
    return _pcall(name, body, [h, g, target], [row, vec, row],
                  [_sds((s, d), F32), _sds((s, d), BF16), _sds((1, d), F32), _sds((1, 1), F32)],
                  [row, row, vec, one], grid=(s // tr,), sem=("arbitrary",), after=after)


_SQRT_HALF = math.sqrt(0.5)
_INV_SQRT_2PI = 1.0 / math.sqrt(2.0 * math.pi)


def _gelu(x):
    return 0.5 * x * (1.0 + lax.erf(x * _SQRT_HALF))


def _gelu_grad(x):
    return 0.5 * (1.0 + lax.erf(x * _SQRT_HALF)) + x * jnp.exp(-0.5 * x * x) * _INV_SQRT_2PI


def _causal_mask():
    row = lax.broadcasted_iota(jnp.int32, (CHUNK, CHUNK), 0)
    col = lax.broadcasted_iota(jnp.int32, (CHUNK, CHUNK), 1)
    return row >= col


def _layernorm_parts(v):
    mu = jnp.mean(v, axis=-1, keepdims=True)
    xc = v - mu
    rstd = lax.rsqrt(jnp.mean(xc * xc, axis=-1, keepdims=True) + EPS)
    return xc * rstd, rstd


def _amix_fwd(name, pre, ln_g, ln_b, w_s, b_s_col, after=None):
    s, w2 = pre.shape
    w = w2 // 2
    head = w // A_GROUPS

    def body(pre_ref, g_ref, b_ref, ws_ref, bs_ref, o_ref):
        u = _gelu(pre_ref[:, :w])
        v = _gelu(pre_ref[:, w:])
        vhat, _ = _layernorm_parts(v)
        vn = (vhat * g_ref[...] + b_ref[...]).astype(BF16)
        mask = _causal_mask()
        for grp in range(A_GROUPS):
            cols = slice(grp * head, (grp + 1) * head)
            wm = jnp.where(mask, ws_ref[grp], 0.0).astype(BF16)
            sg = jnp.dot(wm, vn[:, cols], preferred_element_type=F32) + bs_ref[grp]
            o_ref[:, cols] = (u[:, cols] * sg).astype(o_ref.dtype)

    vec = pl.BlockSpec((1, w), lambda i: (0, 0))
    return _pcall(
        name, body, [pre, ln_g, ln_b, w_s, b_s_col],
        [pl.BlockSpec((CHUNK, w2), lambda i: (i, 0)), vec, vec,
         pl.BlockSpec((A_GROUPS, CHUNK, CHUNK), lambda i: (0, 0, 0)),
         pl.BlockSpec((A_GROUPS, CHUNK, 1), lambda i: (0, 0, 0))],
        _sds((s, w), BF16), pl.BlockSpec((CHUNK, w), lambda i: (i, 0)),
        grid=(s // CHUNK,), sem=("parallel",), after=after)


def _amix_bwd(name, pre, dgated, ln_g, ln_b, w_s, b_s_col, after=None):
    s, w2 = pre.shape
    w = w2 // 2
    head = w // A_GROUPS

    def body(pre_ref, dg_ref, g_ref, b_ref, ws_ref, bs_ref, dpre_ref, glg_ref, glb_ref, gws_ref, gbs_ref):
        step = pl.program_id(0)
        pre_u = pre_ref[:, :w]
        pre_v = pre_ref[:, w:]
        u = _gelu(pre_u)
        v = _gelu(pre_v)
        vhat, rstd = _layernorm_parts(v)
        gain = g_ref[...]
        vn = (vhat * gain + b_ref[...]).astype(BF16)
        dgated = dg_ref[...]
        ds = dgated * u
        dsb = ds.astype(BF16)
        mask = _causal_mask()
        du_parts = []
        dvn_parts = []
        for grp in range(A_GROUPS):
            cols = slice(grp * head, (grp + 1) * head)
            wm = jnp.where(mask, ws_ref[grp], 0.0).astype(BF16)
            sg = jnp.dot(wm, vn[:, cols], preferred_element_type=F32) + bs_ref[grp]
            du_parts.append(dgated[:, cols] * sg)
            gws = lax.dot_general(dsb[:, cols], vn[:, cols], NT, preferred_element_type=F32)
            gws = jnp.where(mask, gws, 0.0)
            gbs = jnp.sum(ds[:, cols], axis=-1, keepdims=True)

            @pl.when(step == 0)
            def _():
                gws_ref[grp] = gws
                gbs_ref[grp] = gbs

            @pl.when(step > 0)
            def _():
                gws_ref[grp] += gws
                gbs_ref[grp] += gbs

            dvn_parts.append(lax.dot_general(wm, dsb[:, cols], TN, preferred_element_type=F32))
        du = jnp.concatenate(du_parts, axis=-1)
        dvn = jnp.concatenate(dvn_parts, axis=-1)
        _accumulate(glg_ref, jnp.sum(dvn * vhat, axis=0, keepdims=True), step)
        _accumulate(glb_ref, jnp.sum(dvn, axis=0, keepdims=True), step)
        dvhat = dvn * gain
        dv = rstd * (dvhat - jnp.mean(dvhat, axis=-1, keepdims=True)
                     - vhat * jnp.mean(dvhat * vhat, axis=-1, keepdims=True))
        dpre_ref[:, :w] = (du * _gelu_grad(pre_u)).astype(dpre_ref.dtype)
        dpre_ref[:, w:] = (dv * _gelu_grad(pre_v)).astype(dpre_ref.dtype)

    vec = pl.BlockSpec((1, w), lambda i: (0, 0))
    ws_spec = pl.BlockSpec((A_GROUPS, CHUNK, CHUNK), lambda i: (0, 0, 0))
    bs_spec = pl.BlockSpec((A_GROUPS, CHUNK, 1), lambda i: (0, 0, 0))
    return _pcall(
        name, body, [pre, dgated, ln_g, ln_b, w_s, b_s_col],
        [pl.BlockSpec((CHUNK, w2), lambda i: (i, 0)), pl.BlockSpec((CHUNK, w), lambda i: (i, 0)),
         vec, vec, ws_spec, bs_spec],
        [_sds((s, w2), BF16), _sds((1, w), F32), _sds((1, w), F32),
         _sds((A_GROUPS, CHUNK, CHUNK), F32), _sds((A_GROUPS, CHUNK, 1), F32)],
        [pl.BlockSpec((CHUNK, w2), lambda i: (i, 0)), vec, vec, ws_spec, bs_spec],
        grid=(s // CHUNK,), sem=("arbitrary",), after=after)


def _shift_rows(x, k, forward):
    n = x.shape[0]
    row = lax.broadcasted_iota(jnp.int32, x.shape, 0)
    if forward:
        return jnp.where(row >= k, pltpu.roll(x, k, 0), 0.0)
    return jnp.where(row < n - k, pltpu.roll(x, n - k, 0), 0.0)


def _window_sum(x, window, forward):
    k = 1
    while k < window:
        x = x + _shift_rows(x, k, forward)
        k *= 2
    return x


def _pool(name, v, backward, after=None):
    s, w = v.shape
    head = w // B_GROUPS
    lane = _tile(head, 128)

    def body(v_ref, o_ref):
        grp = pl.program_id(0)
        x = v_ref[...]
        t = lax.broadcasted_iota(jnp.int32, x.shape, 0)
        for idx, window in enumerate(B_WINDOWS):
            @pl.when(grp == idx)
            def _():
                inv_count = 1.0 / jnp.minimum(t + 1, window).astype(F32)
                if backward:
                    out = _window_sum(x * inv_count, window, False) - x
                else:
                    out = _window_sum(x, window, True) * inv_count - x
                o_ref[...] = out.astype(o_ref.dtype)

    per = head // lane
    spec = pl.BlockSpec((s, lane), lambda g, j: (0, g * per + j))
    return _pcall(name, body, [v], [spec], _sds((s, w), BF16), spec, grid=(B_GROUPS, per),
                  sem=("parallel", "parallel"), after=after)


def _colsum(name, a, after=None):
    s, d = a.shape
    tr = _tile(s, 256)

    def body(a_ref, o_ref):
        _accumulate(o_ref, jnp.sum(a_ref[...], axis=0, keepdims=True), pl.program_id(0))

    return _pcall(name, body, [a], [pl.BlockSpec((tr, d), lambda i: (i, 0))], _sds((1, d), F32),
                  pl.BlockSpec((1, d), lambda i: (0, 0)), grid=(s // tr,), sem=("arbitrary",), after=after)


def _adamw(w, g, m, v):
    m = ADAM_B1 * m + (1.0 - ADAM_B1) * g
    v = ADAM_B2 * v + (1.0 - ADAM_B2) * (g * g)
    m_hat = m / (1.0 - ADAM_B1 ** ADAM_STEP)
    v_hat = v / (1.0 - ADAM_B2 ** ADAM_STEP)
    delta = -ADAM_LR * (m_hat / (jnp.sqrt(v_hat) + ADAM_EPS) + ADAM_WD * w)
    return delta, m, v


def _adam_rows(name, g, w, m, v, after=None):
    r, c = g.shape
    tr = _tile(r, 256)

    def body(g_ref, w_ref, m_ref, v_ref, d_ref, nm_ref, nv_ref):
        d_ref[...], nm_ref[...], nv_ref[...] = _adamw(w_ref[...], g_ref[...], m_ref[...], v_ref[...])

    spec = pl.BlockSpec((tr, c), lambda i: (i, 0))
    return _pcall(name, body, [g, w, m, v], [spec] * 4, [_sds((r, c), F32)] * 3, [spec] * 3,
                  grid=(r // tr,), sem=("parallel",), after=after)


def _position():
    return lax.axis_index("x"), lax.axis_index("y"), lax.axis_index("c")


def _other_chips(x, y):
    return [(1 - x, y), (x, 1 - y), (1 - x, 1 - y)]


def _slot(px, py, pc):
    return 4 * px + 2 * py + pc


def _hbm(a):
    return pltpu.with_memory_space_constraint(a, pltpu.HBM)


def _gather_copies(srcs, lands, send_sems, recv_sems):
    x, y, c = _position()
    peers = [(x, y, 1 - c)] + [(*chip, c) for chip in _other_chips(x, y)]
    mine = _slot(x, y, c)
    return [[pltpu.make_async_remote_copy(
        src_ref=srcs[t], dst_ref=lands[t].at[mine], send_sem=send_sems[t].at[k], recv_sem=recv_sems[t].at[k],
        device_id=peer, device_id_type=MESH) for k, peer in enumerate(peers)] for t in range(len(srcs))]


def _split_start(name, srcs, lands, copies, n_sems, after=None):
    n = len(srcs)
    order = [] if after is None else [after]
    n_in = 2 * n + len(order)

    def body(*refs):
        for per_tensor in copies(refs[:n], refs[n:2 * n], refs[n_in:n_in + n], refs[n_in + n:n_in + 2 * n]):
            for cp in per_tensor:
                cp.start()
        refs[-1][...] = jnp.zeros_like(refs[-1])

    out_shape = ([pltpu.SemaphoreType.DMA((n_sems,)) for _ in range(2 * n)]
                 + [pltpu.HBM(a.shape, a.dtype) for a in list(srcs) + list(lands)]
                 + [_sds((8, 128), F32)])
    out = pl.pallas_call(
        body, name=name, out_shape=out_shape, in_specs=[_HBM] * (2 * n) + [_ANY] * len(order),
        out_specs=[_SEM] * (2 * n) + [_HBM] * (2 * n) + [pl.BlockSpec(memory_space=pltpu.VMEM)],
        input_output_aliases={i: 2 * n + i for i in range(2 * n)},
        compiler_params=pltpu.CompilerParams(has_side_effects=_EFFECT),
    )(*[_hbm(a) for a in srcs], *[_hbm(a) for a in lands], *order)
    return [(out[t], out[n + t], out[2 * n + t], out[3 * n + t]) for t in range(n)], out[-1]


def _split_wait(name, started, copies, after):
    n = len(started)

    def body(*refs):
        for per_tensor in copies(refs[:n], refs[n:2 * n], refs[2 * n:3 * n], refs[3 * n:4 * n]):
            for cp in per_tensor:
                cp.wait_send()
                cp.wait_recv()

    srcs = [e[2] for e in started]
    lands = [e[3] for e in started]
    out = pl.pallas_call(
        body, name=name, out_shape=[pltpu.HBM(a.shape, a.dtype) for a in srcs + lands],
        in_specs=[_HBM] * (2 * n) + [_SEM] * (2 * n) + [_ANY], out_specs=[_HBM] * (2 * n),
        input_output_aliases={i: i for i in range(2 * n)},
        compiler_params=pltpu.CompilerParams(has_side_effects=_EFFECT),
    )(*srcs, *lands, *[e[0] for e in started], *[e[1] for e in started], after)
    return out[:n], out[n:]


def _gather_start(name, shards, after=None):
    lands = [lax.empty((N_DEV,) + s.shape, s.dtype) for s in shards]
    return _split_start(name, shards, lands, _gather_copies, 4, after)


def _gather_wait(name, started, after):
    return _split_wait(name, started, _gather_copies, after)


def _sibling_copies(srcs, lands, send_sems, recv_sems):
    x, y, c = _position()
    return [[pltpu.make_async_remote_copy(
        src_ref=srcs[t], dst_ref=lands[t], send_sem=send_sems[t].at[0], recv_sem=recv_sems[t].at[0],
        device_id=(x, y, 1 - c), device_id_type=MESH)] for t in range(len(srcs))]


def _sibling_start(name, arrays):
    lands = [lax.empty(a.shape, a.dtype) for a in arrays]
    return _split_start(name, arrays, lands, _sibling_copies, 1)


def _sibling_wait(name, started, after):
    return _split_wait(name, started, _sibling_copies, after)[1]


def _small_copies(srcs, lands, send_sems, recv_sems):
    x, y, c = _position()
    mine = _slot(x, y, c)
    peers = [(x ^ ((k >> 2) & 1), y ^ ((k >> 1) & 1), c ^ (k & 1)) for k in range(1, N_DEV)]
    return [[pltpu.make_async_remote_copy(
        src_ref=srcs[t], dst_ref=lands[t].at[mine], send_sem=send_sems[t].at[k], recv_sem=recv_sems[t].at[k],
        device_id=peer, device_id_type=MESH) for k, peer in enumerate(peers)] for t in range(len(srcs))]


def _gather_finish(name, shards, lands, after):
    n = len(shards)

    def body(*refs):
        srcs, lands_in, outs = refs[:n], refs[n:2 * n], refs[2 * n:3 * n]
        send_sems, recv_sems, local_sems = refs[3 * n:]
        x, y, c = _position()
        local = [pltpu.make_async_copy(srcs[t], outs[t].at[_slot(x, y, c)], local_sems.at[t]) for t in range(n)]
        copies = []
        for t in range(n):
            for j, chip in enumerate(_other_chips(x, y)):
                block = outs[t].at[_slot(*chip, c)]
                copies.append(pltpu.make_async_remote_copy(
                    src_ref=block, dst_ref=block, send_sem=send_sems.at[t, j], recv_sem=recv_sems.at[t, j],
                    device_id=(x, y, 1 - c), device_id_type=MESH))
        for cp in local + copies:
            cp.start()
        for cp in copies:
            cp.wait_send()
        for t in range(n):
            for j, chip in enumerate(_other_chips(x, y)):
                block = outs[t].at[_slot(*chip, 1 - c)]
                pltpu.make_async_remote_copy(
                    src_ref=block, dst_ref=block, send_sem=send_sems.at[t, j], recv_sem=recv_sems.at[t, j],
                    device_id=(x, y, 1 - c), device_id_type=MESH).wait_recv()
        for cp in local:
            cp.wait()

    return _pcall(name, body, [*shards, *lands], [_ANY] * (2 * n),
                  [_sds(l.shape, l.dtype) for l in lands], [_ANY] * n,
                  scratch=[pltpu.SemaphoreType.DMA((n, 3)), pltpu.SemaphoreType.DMA((n, 3)),
                           pltpu.SemaphoreType.DMA((n,))],
                  after=after, aliases={n + t: t for t in range(n)})


def _exchange_sibling(name, fulls, after):
    n = len(fulls)

    def body(*refs):
        src = refs[:n]
        out = refs[n:2 * n]
        send_sems, recv_sems = refs[2 * n:]
        x, y, c = _position()
        copies = [pltpu.make_async_remote_copy(
            src_ref=src[t].at[:, 1 - c], dst_ref=out[t], send_sem=send_sems.at[t], recv_sem=recv_sems.at[t],
            device_id=(x, y, 1 - c), device_id_type=MESH) for t in range(n)]
        for cp in copies:
            cp.start()
        for cp in copies:
            cp.wait()

    return _pcall(name, body, fulls, [_ANY] * n, [_sds((N_CHIPS,) + f.shape[2:], f.dtype) for f in fulls],
                  [_ANY] * n, scratch=[pltpu.SemaphoreType.DMA((n,)), pltpu.SemaphoreType.DMA((n,))],
                  after=after)


def _add_sibling(name, full, recv, core, after):
    _, _, r, c = full.shape
    tr = _tile(r, max(8, (256 * 1024) // c))

    def body(core_ref, f_ref, r_ref, o_ref):
        o_ref[...] = (f_ref[...].astype(F32) + r_ref[...].astype(F32)).astype(o_ref.dtype)

    return _pcall(
        name, body, [full, recv],
        [pl.BlockSpec((None, None, tr, c), lambda p, i, core_ref: (p, core_ref[0], i, 0)),
         pl.BlockSpec((None, tr, c), lambda p, i, core_ref: (p, i, 0))],
        _sds((N_CHIPS, r, c), BF16), pl.BlockSpec((None, tr, c), lambda p, i, core_ref: (p, i, 0)),
        grid=(N_CHIPS, r // tr), sem=("parallel", "parallel"), prefetch=[core], after=after)


def _scatter_copies(srcs, lands, send_sems, recv_sems):
    x, y, c = _position()
    return [[pltpu.make_async_remote_copy(
        src_ref=srcs[t].at[2 * px + py], dst_ref=lands[t].at[j],
        send_sem=send_sems[t].at[j], recv_sem=recv_sems[t].at[j],
        device_id=(px, py, c), device_id_type=MESH) for j, (px, py) in enumerate(_other_chips(x, y))]
        for t in range(len(srcs))]


def _scatter_start(name, partials):
    lands = [lax.empty((N_CHIPS - 1,) + p.shape[1:], p.dtype) for p in partials]
    return _split_start(name, partials, lands, _scatter_copies, 3)


def _scatter_wait(name, started, after):
    return _split_wait(name, started, _scatter_copies, after)


def _reduce_adam(name, partial, recv, chip, w, m, v, layer, carried, after):
    n_layers, r, c = w.shape
    tr = _tile(r, max(8, (128 * 1024) // c))

    def body(chip_ref, p_ref, r_ref, w_ref, m_ref, v_ref, *rest):
        g_ref, d_ref, nm_ref, nv_ref = rest[-4:]
        g = p_ref[...].astype(F32)
        for j in range(N_CHIPS - 1):
            g = g + r_ref[j].astype(F32)
        g_ref[...] = g
        d_ref[...], nm_ref[...], nv_ref[...] = _adamw(w_ref[...], g, m_ref[...], v_ref[...])

    layered = pl.BlockSpec((None, tr, c), lambda i, chip_ref: (layer, i, 0))
    in_specs = [pl.BlockSpec((None, tr, c), lambda i, chip_ref: (chip_ref[0], i, 0)),
                pl.BlockSpec((N_CHIPS - 1, tr, c), lambda i, chip_ref: (0, i, 0)),
                layered, layered, layered]
    operands = [partial, recv, w, m, v]
    aliases = {}
    if carried is not None:
        operands += list(carried)
        in_specs += [_ANY] * 4
        aliases = {1 + 5 + o: o for o in range(4)}
    return _pcall(name, body, operands, in_specs, [_sds((n_layers, r, c), F32)] * 4, [layered] * 4,
                  grid=(r // tr,), sem=("parallel",), prefetch=[chip], after=after, aliases=aliases)


def _small_sum(name, gathered, own, device, after=None):
    r, lanes = own.shape

    def body(dev_ref, g_ref, own_ref, out_ref):
        dev = dev_ref[0]
        mine = own_ref[...]
        total = jnp.where(dev == 0, mine, g_ref[0])
        for d in range(1, N_DEV):
            total = total + jnp.where(dev == d, mine, g_ref[d])
        out_ref[...] = total

    return _pcall(name, body, [gathered, own],
                  [pl.BlockSpec((N_DEV, r, lanes), lambda i, dev_ref: (0, 0, 0)),
                   pl.BlockSpec((r, lanes), lambda i, dev_ref: (0, 0))],
                  _sds((r, lanes), F32), pl.BlockSpec((r, lanes), lambda i, dev_ref: (0, 0)),
                  grid=(1,), sem=("arbitrary",), prefetch=[device], after=after)


def _pack(arrays):
    return jnp.concatenate([a.reshape(-1, 128) for a in arrays], axis=0)


def _unpack(packed, shapes):
    out, row = [], 0
    for shape in shapes:
        rows = math.prod(shape) // 128
        out.append(packed[row:row + rows].reshape(shape))
        row += rows
    return out


class _Order:
    def __init__(self):
        self.last = None

    def __call__(self, fn, *args, **kwargs):
        out = fn(*args, after=self.last, **kwargs)
        self.last = out[0] if isinstance(out, (list, tuple)) else out
        return out


def kernel(x, a_w_in, a_ln_g, a_ln_b, a_w_s, a_b_s, a_w_out, b_w_in, b_w_grp, b_scale, b_w_out, norm_mix, norm_mlp, mlp_w1, mlp_w2, final_norm, loss_target, m_a_w_in, m_a_ln_g, m_a_ln_b, m_a_w_s, m_a_b_s, m_a_w_out, m_b_w_in, m_b_w_grp, m_b_scale, m_b_w_out, m_norm_mix, m_norm_mlp, m_mlp_w1, m_mlp_w2, m_final_norm, v_a_w_in, v_a_ln_g, v_a_ln_b, v_a_w_s, v_a_b_s, v_a_w_out, v_b_w_in, v_b_w_grp, v_b_scale, v_b_w_out, v_norm_mix, v_norm_mlp, v_mlp_w1, v_mlp_w2, v_final_norm):
    s, d = x.shape[1], x.shape[2]
    depth = mlp_w1.shape[0]
    a_slab = a_w_in.shape[2]
    ff_slab = mlp_w1.shape[2]
    ff_rows = mlp_w2.shape[1]
    bh = b_w_grp.shape[3]
    my_x, my_y, my_c = _position()
    core = jnp.reshape(my_c, (1,)).astype(jnp.int32)
    chip = jnp.reshape(2 * my_x + my_y, (1,)).astype(jnp.int32)
    device = _slot(my_x, my_y, my_c)
    run = _Order()

    w1_b, w2_b = mlp_w1.astype(BF16), mlp_w2.astype(BF16)
    shards = [a_w_in[0].astype(BF16), a_w_out[0].astype(BF16), b_scale,
              w1_b[0], w2_b[0],
              b_w_in[0].astype(BF16), b_w_grp[0].astype(BF16), b_w_out[0].astype(BF16),
              w1_b[1], w2_b[1]]
    started, token = _gather_start("first_weights_gather_start", shards[:1])
    rest, token = _gather_start("weights_gather_start", shards[1:], token)
    started += rest
    run.last = token

    def gathered(name, indices):
        srcs, lands = _gather_wait(name + "_wait", [started[t] for t in indices], run.last)
        run.last = srcs[0]
        return run(_gather_finish, name + "_finish", srcs, lands)

    h0 = x[0]
    target = loss_target[0]
    ln_g, ln_b = a_ln_g, a_ln_b
    w_s = a_w_s[0]
    b_s_col = a_b_s[0][:, :, None]
    nmix = [norm_mix[l][None, :] for l in range(depth)]
    nmlp = [norm_mlp[l][None, :] for l in range(depth)]

    def mlp_forward(l, h, gather_indices):
        hn = run(_rms_fwd, f"mlp{l}_norm", h, nmlp[l])
        (w1,) = gathered(f"mlp{l}_up_weights", gather_indices[:1])
        act, act_sq = run(_mm_nn, f"mlp{l}_up", hn, w1,
                          lambda acc: (jnp.maximum(acc, 0.0), jnp.square(jnp.maximum(acc, 0.0))),
                          (BF16, BF16), slab=True)
        (w2,) = gathered(f"mlp{l}_down_weights", gather_indices[1:])
        w2 = w2.reshape(-1, d)
        (h_out,) = run(_mm_nn, f"mlp{l}_down", act_sq, w2, lambda acc, res: (acc + res,), (F32,),
                       extras=(h,), extra_kinds=("tile",))
        return h_out, (h, hn, act, act_sq, w1, w2)

    scattered = []

    def scatter_partials(name, partials, specs):
        in_flight, tok = _scatter_start(name + "_scatter_start", partials)
        run.last = tok
        scattered.append((name, in_flight, specs))

    def weight_grad(name, a, b, by_rows, block, between):
        other = run(_mm_tn_half, name + "_other", a, b, core, False, by_rows, block)
        sent, tok = _sibling_start(name + "_sibling_start", [other])
        run.last = tok
        middle = between()
        (recv,) = _sibling_wait(name + "_sibling_wait", sent, run.last)
        run.last = recv
        return run(_mm_tn_half, name + "_own", a, b, core, True, by_rows, block, recv=recv), middle

    def mlp_backward(l, saved, dh, dhb):
        h, hn, act, act_sq, w1, w2 = saved
        part_w2, (dpre,) = weight_grad(
            f"mlp{l}_down_dw", act_sq, dhb, True, ff_rows,
            lambda: run(_mm_nt, f"mlp{l}_down_dx", dhb, w2, lambda acc, a: (2.0 * a.astype(F32) * acc,),
                        (BF16,), extras=(act,), extra_kinds=("tile",)))
        scatter_partials(f"mlp{l}_down_grads", [part_w2], [("mlp_w2", l)])
        part_w1, (dhn,) = weight_grad(
            f"mlp{l}_up_dw", hn, dpre, False, ff_slab,
            lambda: run(_mm_nt, f"mlp{l}_up_dx", dpre, w1, lambda acc: (acc,), (F32,), slab=True))
        scatter_partials(f"mlp{l}_up_grads", [part_w1], [("mlp_w1", l)])
        dh, dhb, g_norm = run(_rms_bwd, f"mlp{l}_norm_bwd", dhn, h, nmlp[l], dh)
        return dh, dhb, g_norm

    hn0 = run(_rms_fwd, "mix0_norm", h0, nmix[0])
    (wa_in,) = gathered("mixa_in_weights", [0])
    (pre,) = run(_mm_nn, "mixa_in", hn0, wa_in, lambda acc: (acc,), (F32,), slab=True)
    wa_out, scale = gathered("mixa_out_weights", [1, 2])
    wa_out, scale = wa_out.reshape(d, d), scale.reshape(1, d)
    gated = run(_amix_fwd, "mixa_gate", pre, ln_g, ln_b, w_s, b_s_col)
    (h1,) = run(_mm_nn, "mixa_out", gated, wa_out, lambda acc, res: (acc + res,), (F32,),
                extras=(h0,), extra_kinds=("tile",))
    h2, saved_mlp0 = mlp_forward(0, h1, [3, 4])
    hn2 = run(_rms_fwd, "mix1_norm", h2, nmix[1])
    wb_in, wb_grp, wb_out = gathered("mixb_weights", [5, 6, 7])
    wb_in, wb_out = wb_in.reshape(d, d), wb_out.reshape(d, d)
    wb_grp = jnp.transpose(wb_grp, (1, 0, 2, 3)).reshape(B_GROUPS, bh, bh)
    (vb,) = run(_mm_nn, "mixb_in", hn2, wb_in, lambda acc: (acc,), (F32,))
    pooled = run(_pool, "mixb_pool", vb, backward=False)
    tm = _tile(s, 1024)
    grp_tile = pl.BlockSpec((tm, bh), lambda i, j, k: (i, j))
    grp_weight = pl.BlockSpec((None, bh, bh), lambda i, j, k: (j, 0, 0))
    mixed, mixed_scaled = run(
        _matmul, "mixb_grp", pooled, wb_grp, NN, (s // tm, B_GROUPS, 1), grp_tile, grp_weight,
        [_sds((s, d), BF16), _sds((s, d), BF16)], [grp_tile] * 2,
        (tm, bh), lambda acc, sc: (acc, acc * sc), (scale,), [pl.BlockSpec((1, bh), lambda i, j, k: (0, j))])
    (h3,) = run(_mm_nn, "mixb_out", mixed_scaled, wb_out, lambda acc, res: (acc + res,), (F32,),
                extras=(h2,), extra_kinds=("tile",))
    h4, saved_mlp1 = mlp_forward(1, h3, [8, 9])
    dh, dhb, g_final, loss_part = run(_loss_head, "loss_head", h4, final_norm[None, :], target)

    dh, dhb, g_nmlp1 = mlp_backward(1, saved_mlp1, dh, dhb)
    tks = _tile(s, 1024)
    grp_rows = pl.BlockSpec((tks, bh), lambda i, j, k: (k, j))

    def mixb_middle():
        dms_scaled, dms_mixed = run(
            _mm_nt, "mixb_out_dx", dhb, wb_out,
            lambda acc, sc, mx: (acc * sc, acc * mx.astype(F32)), (BF16, F32),
            extras=(scale, mixed), extra_kinds=("row", "tile"))
        g_scale = run(_colsum, "mixb_scale_dw", dms_mixed)
        (g_wb_grp,) = run(
            _matmul, "mixb_grp_dw", pooled, dms_scaled, TN, (1, B_GROUPS, s // tks), grp_rows, grp_rows,
            [_sds((B_GROUPS, bh, bh), BF16)], [grp_weight], (bh, bh), lambda acc: (acc,))
        (dpooled,) = run(
            _matmul, "mixb_grp_dx", dms_scaled, wb_grp, NT, (s // tm, B_GROUPS, 1), grp_tile, grp_weight,
            [_sds((s, d), F32)], [grp_tile], (tm, bh), lambda acc: (acc,))
        return g_scale, g_wb_grp, run(_pool, "mixb_pool_bwd", dpooled, backward=True)

    part_wb_out, (g_scale, g_wb_grp, dvb) = weight_grad("mixb_out_dw", mixed_scaled, dhb, True, d // N_DEV,
                                                        mixb_middle)
    part_wb_in, (dhn2,) = weight_grad(
        "mixb_in_dw", hn2, dvb, True, d // N_DEV,
        lambda: run(_mm_nt, "mixb_in_dx", dvb, wb_in, lambda acc: (acc,), (F32,)))
    grp_full = jnp.transpose(g_wb_grp.reshape(B_GROUPS, N_DEV, bh // N_DEV, bh), (1, 0, 2, 3))
    grp_full = grp_full.reshape(N_CHIPS, 2, B_GROUPS * bh // N_DEV, bh)
    (grp_sibling,) = run(_exchange_sibling, "mixb_grp_dw_to_sibling", [grp_full])
    part_wb_grp = run(_add_sibling, "mixb_grp_dw_add_sibling", grp_full, grp_sibling, core)
    scatter_partials("mixb_grads", [part_wb_out, part_wb_grp, part_wb_in],
                     [("b_w_out", 0), ("b_w_grp", 0), ("b_w_in", 0)])
    dh, dhb, g_nmix1 = run(_rms_bwd, "mix1_norm_bwd", dhn2, h2, nmix[1], dh)
    dh, dhb, g_nmlp0 = mlp_backward(0, saved_mlp0, dh, dhb)
    def mixa_middle():
        (dgated,) = run(_mm_nt, "mixa_out_dx", dhb, wa_out, lambda acc: (acc,), (F32,))
        return run(_amix_bwd, "mixa_gate_bwd", pre, dgated, ln_g, ln_b, w_s, b_s_col)

    part_wa_out, (dpre, g_ln_g, g_ln_b, g_w_s, g_b_s) = weight_grad("mixa_out_dw", gated, dhb, True, d // N_DEV,
                                                                     mixa_middle)
    part_wa_in, (dhn0,) = weight_grad(
        "mixa_in_dw", hn0, dpre, False, a_slab,
        lambda: run(_mm_nt, "mixa_in_dx", dpre, wa_in, lambda acc: (acc,), (F32,), slab=True))
    scatter_partials("mixa_grads", [part_wa_in, part_wa_out], [("a_w_in", 0), ("a_w_out", 0)])
    grad_x, _, g_nmix0 = run(_rms_bwd, "mix0_norm_bwd", dhn0, h0, nmix[0], dh)

    g_norm_mix = jnp.concatenate([g_nmix0, g_nmix1], axis=0)
    g_norm_mlp = jnp.concatenate([g_nmlp0, g_nmlp1], axis=0)
    small_parts = [g_ln_g, g_ln_b, g_w_s, g_b_s, g_norm_mix, g_norm_mlp, g_final, g_scale]
    packed = _pack(small_parts)
    small_sent, tok = _split_start("small_grads_start", [packed], [lax.empty((N_DEV,) + packed.shape, F32)],
                                   _small_copies, N_DEV - 1)
    run.last = tok

    weights = {"a_w_in": (a_w_in, m_a_w_in, v_a_w_in), "a_w_out": (a_w_out, m_a_w_out, v_a_w_out),
               "b_w_in": (b_w_in, m_b_w_in, v_b_w_in), "b_w_grp": (b_w_grp, m_b_w_grp, v_b_w_grp),
               "b_w_out": (b_w_out, m_b_w_out, v_b_w_out), "mlp_w1": (mlp_w1, m_mlp_w1, v_mlp_w1),
               "mlp_w2": (mlp_w2, m_mlp_w2, v_mlp_w2)}
    results = {}

    def finish_group(name, in_flight, specs):
        partials, lands = _scatter_wait(name + "_scatter_wait", in_flight, run.last)
        run.last = lands[0]
        for t, (wname, layer) in enumerate(specs):
            w, m, v = weights[wname]
            _, r, c = partials[t].shape
            layers = w.shape[0]
            results[wname] = run(_reduce_adam, f"{name}_reduce_adam_{t}", partials[t], lands[t], chip,
                                 w.reshape(layers, r, c), m.reshape(layers, r, c), v.reshape(layers, r, c),
                                 layer, results.get(wname))

    for group in scattered[:-1]:
        finish_group(*group)
    own_packed, small_gathered = _split_wait("small_grads_wait", small_sent, _small_copies, run.last)
    run.last = small_gathered[0]
    small_sum = run(_small_sum, "small_grads_sum", small_gathered[0], own_packed[0],
                    jnp.reshape(device, (1,)).astype(jnp.int32))
    sg = _unpack(small_sum, [a_ln_g.shape, a_ln_b.shape, a_w_s.shape, a_b_s.shape, norm_mix.shape,
                             norm_mlp.shape, final_norm.shape, (1, d)])
    shard = b_scale.shape[1]
    sg[7] = lax.dynamic_slice(sg[7], (0, device * shard), (1, shard))
    small_w = [a_ln_g, a_ln_b, a_w_s, a_b_s, norm_mix, norm_mlp, final_norm, b_scale]
    small_m = [m_a_ln_g, m_a_ln_b, m_a_w_s, m_a_b_s, m_norm_mix, m_norm_mlp, m_final_norm, m_b_scale]
    small_v = [v_a_ln_g, v_a_ln_b, v_a_w_s, v_a_b_s, v_norm_mix, v_norm_mlp, v_final_norm, v_b_scale]
    small_out = run(_adam_rows, "small_adam", _pack(sg), _pack(small_w), _pack(small_m), _pack(small_v))
    shapes = [w.shape for w in small_w]
    small_res = [sg] + [_unpack(o, shapes) for o in small_out]

    finish_group(*scattered[-1])
    big = {wname: [o.reshape(weights[wname][0].shape) for o in outs] for wname, outs in results.items()}

    loss = lax.psum(loss_part[0, 0], ("x", "y", "c"))

    def leaf(o):
        return (big["a_w_in"][o], small_res[o][0], small_res[o][1], small_res[o][2], small_res[o][3],
                big["a_w_out"][o], big["b_w_in"][o], big["b_w_grp"][o], small_res[o][7], big["b_w_out"][o],
                small_res[o][4], small_res[o][5], big["mlp_w1"][o], big["mlp_w2"][o], small_res[o][6])

    return (loss, grad_x[None], *leaf(0), *leaf(1), *leaf(2), *leaf(3))
```

```python
import math

import jax
import jax.numpy as jnp
from jax import lax
from jax.experimental import pallas as pl
from jax.experimental.pallas import tpu as pltpu

F32 = jnp.float32
BF16 = jnp.bfloat16
MESH = pl.DeviceIdType.MESH

N_DEV = 8
N_CHIPS = 4
CHUNK = 128
A_GROUPS = 8
B_WINDOWS = (2, 4, 8, 16)
B_GROUPS = len(B_WINDOWS)
EPS = 1e-6
ADAM_LR = 0.001
ADAM_B1 = 0.9
ADAM_B2 = 0.999
ADAM_EPS = 1e-08
ADAM_WD = 0.01
ADAM_STEP = 10

VMEM_LIMIT = 48 * 1024 * 1024

NN = (((1,), (0,)), ((), ()))
NT = (((1,), (1,)), ((), ()))
TN = (((0,), (0,)), ((), ()))

_ANY = pl.BlockSpec(memory_space=pl.ANY)
_HBM = pl.BlockSpec(memory_space=pltpu.HBM)
_SEM = pl.BlockSpec(memory_space=pltpu.SEMAPHORE)
_EFFECT = pltpu.SideEffectType.DATAFLOW_SIDE_EFFECTING


def _tile(n, pref):
    return pref if n % pref == 0 else n


def _sds(shape, dtype):
    return jax.ShapeDtypeStruct(shape, dtype)


def _pcall(name, body, operands, in_specs, out_shape, out_specs, *, grid=None, sem=None, scratch=(),
           prefetch=(), after=None, aliases=None):
    after = [] if after is None else [after]
    n_lead = len(prefetch) + len(operands)
    n_after = len(after)

    def wrapped(*refs):
        body(*refs[:n_lead], *refs[n_lead + n_after:])

    in_specs = list(in_specs) + [_ANY] * n_after
    params = pltpu.CompilerParams(vmem_limit_bytes=VMEM_LIMIT) if sem is None else \
        pltpu.CompilerParams(dimension_semantics=sem, vmem_limit_bytes=VMEM_LIMIT)
    kwargs = dict(out_shape=out_shape, scratch_shapes=list(scratch), compiler_params=params, name=name,
                  input_output_aliases=aliases or {})
    if prefetch:
        kwargs["grid_spec"] = pltpu.PrefetchScalarGridSpec(
            num_scalar_prefetch=len(prefetch), grid=grid, in_specs=in_specs, out_specs=out_specs,
            scratch_shapes=list(scratch))
        kwargs.pop("scratch_shapes")
    else:
        kwargs.update(in_specs=in_specs, out_specs=out_specs)
        if grid is not None:
            kwargs["grid"] = grid
    return pl.pallas_call(wrapped, **kwargs)(*prefetch, *operands, *after)


def _matmul(name, a, b, dims, grid, a_spec, b_spec, out_shape, out_specs, acc_shape,
            epilogue, extras=(), extra_specs=(), after=None, prefetch=()):
    nk = grid[2]
    n_extra = len(extras)
    n_out = len(out_shape)
    n_pre = len(prefetch)

    def body(*refs):
        refs = refs[n_pre:]
        a_ref, b_ref = refs[0], refs[1]
        extra_refs = refs[2:2 + n_extra]
        out_refs = refs[2 + n_extra:2 + n_extra + n_out]

        def finish(acc):
            outs = epilogue(acc, *[r[...] for r in extra_refs])
            for o_ref, o in zip(out_refs, outs):
                o_ref[...] = o.astype(o_ref.dtype)

        def product():
            return lax.dot_general(a_ref[...], b_ref[...], dims, preferred_element_type=F32)

        if nk == 1:
            finish(product())
        else:
            acc_ref = refs[-1]
            k = pl.program_id(2)

            @pl.when(k == 0)
            def _():
                acc_ref[...] = product()

            if nk > 2:
                @pl.when(jnp.logical_and(k > 0, k < nk - 1))
                def _():
                    acc_ref[...] += product()

            @pl.when(k == nk - 1)
            def _():
                finish(acc_ref[...] + product())

    scratch = [] if nk == 1 else [pltpu.VMEM(acc_shape, F32)]
    return _pcall(name, body, [a, b, *extras], [a_spec, b_spec, *extra_specs], out_shape, out_specs,
                  grid=grid, sem=("parallel", "parallel", "arbitrary"), scratch=scratch, after=after,
                  prefetch=prefetch)


def _mm_nn(name, a, b, epilogue, out_dtypes, extras=(), extra_kinds=(), slab=False, after=None,
           tm=1024, tn=1024, tk=2048):
    m, kd = a.shape
    if slab:
        n_slab, _, w = b.shape
        n = n_slab * w
        tn = _tile(w, min(tn, w))
        per = w // tn
        tk = _tile(kd, tk)
        b_spec = pl.BlockSpec((None, tk, tn), lambda i, j, k: (j // per, k, j % per))
    else:
        n = b.shape[1]
        tn = _tile(n, tn)
        tk = _tile(kd, tk)
        b_spec = pl.BlockSpec((tk, tn), lambda i, j, k: (k, j))
    tm = _tile(m, tm)
    grid = (m // tm, n // tn, kd // tk)
    a_spec = pl.BlockSpec((tm, tk), lambda i, j, k: (i, k))
    tile_spec = pl.BlockSpec((tm, tn), lambda i, j, k: (i, j))
    row_spec = pl.BlockSpec((1, tn), lambda i, j, k: (0, j))
    extra_specs = [tile_spec if kind == "tile" else row_spec for kind in extra_kinds]
    return _matmul(name, a, b, NN, grid, a_spec, b_spec,
                   [_sds((m, n), d) for d in out_dtypes], [tile_spec for _ in out_dtypes],
                   (tm, tn), epilogue, extras, extra_specs, after=after)


def _mm_nt(name, a, b, epilogue, out_dtypes, extras=(), extra_kinds=(), slab=False, after=None,
           tm=1024, tn=1024, tk=2048):
    m, kd = a.shape
    if slab:
        n_slab, n, w = b.shape
        tk = _tile(w, min(tk, w))
        per = w // tk
        tn = _tile(n, tn)
        b_spec = pl.BlockSpec((None, tn, tk), lambda i, j, k: (k // per, j, k % per))
    else:
        n = b.shape[0]
        tn = _tile(n, tn)
        tk = _tile(kd, tk)
        b_spec = pl.BlockSpec((tn, tk), lambda i, j, k: (j, k))
    tm = _tile(m, tm)
    grid = (m // tm, n // tn, kd // tk)
    a_spec = pl.BlockSpec((tm, tk), lambda i, j, k: (i, k))
    tile_spec = pl.BlockSpec((tm, tn), lambda i, j, k: (i, j))
    row_spec = pl.BlockSpec((1, tn), lambda i, j, k: (0, j))
    extra_specs = [tile_spec if kind == "tile" else row_spec for kind in extra_kinds]
    return _matmul(name, a, b, NT, grid, a_spec, b_spec,
                   [_sds((m, n), d) for d in out_dtypes], [tile_spec for _ in out_dtypes],
                   (tm, tn), epilogue, extras, extra_specs, after=after)


def _mm_tn_half(name, a, b, core, own, by_rows, block, recv=None, after=None, tm=1024, tn=1024, tk=2048):
    s, m = a.shape
    n = b.shape[1]
    tk = _tile(s, tk)

    def owner(chip, core_ref):
        return 2 * chip + (core_ref[0] if own else 1 - core_ref[0])

    if by_rows:
        r, c = block, n
        tm, tn = _tile(r, min(tm, r)), _tile(c, tn)
        per = r // tm
        grid = (N_CHIPS * per, c // tn, s // tk)
        a_spec = pl.BlockSpec((tk, tm), lambda i, j, k, cr: (k, owner(i // per, cr) * per + i % per))
        b_spec = pl.BlockSpec((tk, tn), lambda i, j, k, cr: (k, j))
        o_spec = pl.BlockSpec((None, tm, tn), lambda i, j, k, cr: (i // per, i % per, j))
    else:
        r, c = m, block
        tm, tn = _tile(r, tm), _tile(c, min(tn, c))
        per = c // tn
        grid = (r // tm, N_CHIPS * per, s // tk)
        a_spec = pl.BlockSpec((tk, tm), lambda i, j, k, cr: (k, i))
        b_spec = pl.BlockSpec((tk, tn), lambda i, j, k, cr: (k, owner(j // per, cr) * per + j % per))
        o_spec = pl.BlockSpec((None, tm, tn), lambda i, j, k, cr: (j // per, i, j % per))
    if recv is None:
        extras, epilogue = (), lambda acc: (acc,)
    else:
        extras, epilogue = (recv,), lambda acc, other: (acc + other.astype(F32),)
    return _matmul(name, a, b, TN, grid, a_spec, b_spec, [_sds((N_CHIPS, r, c), BF16)], [o_spec], (tm, tn),
                   epilogue, extras, [o_spec] * len(extras), after=after, prefetch=[core])[0]


def _rms_fwd(name, h, g, after=None):
    s, d = h.shape
    tr = _tile(s, 256)

    def body(h_ref, g_ref, o_ref):
        x = h_ref[...]
        r = lax.rsqrt(jnp.mean(x * x, axis=-1, keepdims=True) + EPS)
        o_ref[...] = (x * r * g_ref[...]).astype(o_ref.dtype)

    row = pl.BlockSpec((tr, d), lambda i: (i, 0))
    vec = pl.BlockSpec((1, d), lambda i: (0, 0))
    return _pcall(name, body, [h, g], [row, vec], _sds((s, d), BF16), row, grid=(s // tr,),
                  sem=("parallel",), after=after)


def _accumulate(ref, part, step):
    @pl.when(step == 0)
    def _():
        ref[...] = part

    @pl.when(step > 0)
    def _():
        ref[...] += part


def _rms_bwd(name, dhn, h, g, dres, after=None):
    s, d = h.shape
    tr = _tile(s, 256)

    def body(dhn_ref, h_ref, g_ref, dres_ref, dh_ref, dhb_ref, gp_ref):
        x = h_ref[...]
        r = lax.rsqrt(jnp.mean(x * x, axis=-1, keepdims=True) + EPS)
        n = x * r
        dy = dhn_ref[...]
        dn = dy * g_ref[...]
        dh = dres_ref[...] + r * (dn - n * jnp.mean(dn * n, axis=-1, keepdims=True))
        dh_ref[...] = dh
        dhb_ref[...] = dh.astype(BF16)
        _accumulate(gp_ref, jnp.sum(dy * n, axis=0, keepdims=True), pl.program_id(0))

    row = pl.BlockSpec((tr, d), lambda i: (i, 0))
    vec = pl.BlockSpec((1, d), lambda i: (0, 0))
    return _pcall(name, body, [dhn, h, g, dres], [row, row, vec, row],
                  [_sds((s, d), F32), _sds((s, d), BF16), _sds((1, d), F32)], [row, row, vec],
                  grid=(s // tr,), sem=("arbitrary",), after=after)


def _loss_head(name, h, g, target, after=None):
    s, d = h.shape
    tr = _tile(s, 256)

    def body(h_ref, g_ref, t_ref, dh_ref, dhb_ref, gp_ref, loss_ref):
        x = h_ref[...]
        gg = g_ref[...]
        r = lax.rsqrt(jnp.mean(x * x, axis=-1, keepdims=True) + EPS)
        n = x * r
        e = n * gg - t_ref[...]
        dy = e * (1.0 / d)
        dn = dy * gg
        dh = r * (dn - n * jnp.mean(dn * n, axis=-1, keepdims=True))
        dh_ref[...] = dh
        dhb_ref[...] = dh.astype(BF16)
        step = pl.program_id(0)
        _accumulate(gp_ref, jnp.sum(dy * n, axis=0, keepdims=True), step)
        row_loss = jnp.mean(e * e, axis=-1, keepdims=True)
        _accumulate(loss_ref, 0.5 * jnp.sum(row_loss, axis=0, keepdims=True), step)

    row = pl.BlockSpec((tr, d), lambda i: (i, 0))
    vec = pl.BlockSpec((1, d), lambda i: (0, 0))
    one = pl.BlockSpec((1, 1), lambda i: (0, 0))
    return _pcall(name, body, [h, g, target], [row, vec, row],
                  [_sds((s, d), F32), _sds((s, d), BF16), _sds((1, d), F32), _sds((1, 1), F32)],
                  [row, row, vec, one], grid=(s // tr,), sem=("arbitrary",), after=after)


_SQRT_HALF = math.sqrt(0.5)
_INV_SQRT_2PI = 1.0 / math.sqrt(2.0 * math.pi)


def _gelu(x):
    return 0.5 * x * (1.0 + lax.erf(x * _SQRT_HALF))


def _gelu_grad(x):
    return 0.5 * (1.0 + lax.erf(x * _SQRT_HALF)) + x * jnp.exp(-0.5 * x * x) * _INV_SQRT_2PI


def _causal_mask():
    row = lax.broadcasted_iota(jnp.int32, (CHUNK, CHUNK), 0)
    col = lax.broadcasted_iota(jnp.int32, (CHUNK, CHUNK), 1)
    return row >= col


def _layernorm_parts(v):
    mu = jnp.mean(v, axis=-1, keepdims=True)
    xc = v - mu
    rstd = lax.rsqrt(jnp.mean(xc * xc, axis=-1, keepdims=True) + EPS)
    return xc * rstd, rstd


def _amix_fwd(name, pre, ln_g, ln_b, w_s, b_s_col, after=None):
    s, w2 = pre.shape
    w = w2 // 2
    head = w // A_GROUPS

    def body(pre_ref, g_ref, b_ref, ws_ref, bs_ref, o_ref):
        u = _gelu(pre_ref[:, :w])
        v = _gelu(pre_ref[:, w:])
        vhat, _ = _layernorm_parts(v)
        vn = (vhat * g_ref[...] + b_ref[...]).astype(BF16)
        mask = _causal_mask()
        for grp in range(A_GROUPS):
            cols = slice(grp * head, (grp + 1) * head)
            wm = jnp.where(mask, ws_ref[grp], 0.0).astype(BF16)
            sg = jnp.dot(wm, vn[:, cols], preferred_element_type=F32) + bs_ref[grp]
            o_ref[:, cols] = (u[:, cols] * sg).astype(o_ref.dtype)

    vec = pl.BlockSpec((1, w), lambda i: (0, 0))
    return _pcall(
        name, body, [pre, ln_g, ln_b, w_s, b_s_col],
        [pl.BlockSpec((CHUNK, w2), lambda i: (i, 0)), vec, vec,
         pl.BlockSpec((A_GROUPS, CHUNK, CHUNK), lambda i: (0, 0, 0)),
         pl.BlockSpec((A_GROUPS, CHUNK, 1), lambda i: (0, 0, 0))],
        _sds((s, w), BF16), pl.BlockSpec((CHUNK, w), lambda i: (i, 0)),
        grid=(s // CHUNK,), sem=("parallel",), after=after)


def _amix_bwd(name, pre, dgated, ln_g, ln_b, w_s, b_s_col, after=None):
    s, w2 = pre.shape
    w = w2 // 2
    head = w // A_GROUPS

    def body(pre_ref, dg_ref, g_ref, b_ref, ws_ref, bs_ref, dpre_ref, glg_ref, glb_ref, gws_ref, gbs_ref):
        step = pl.program_id(0)
        pre_u = pre_ref[:, :w]
        pre_v = pre_ref[:, w:]
        u = _gelu(pre_u)
        v = _gelu(pre_v)
        vhat, rstd = _layernorm_parts(v)
        gain = g_ref[...]
        vn = (vhat * gain + b_ref[...]).astype(BF16)
        dgated = dg_ref[...]
        ds = dgated * u
        dsb = ds.astype(BF16)
        mask = _causal_mask()
        du_parts = []
        dvn_parts = []
        for grp in range(A_GROUPS):
            cols = slice(grp * head, (grp + 1) * head)
            wm = jnp.where(mask, ws_ref[grp], 0.0).astype(BF16)
            sg = jnp.dot(wm, vn[:, cols], preferred_element_type=F32) + bs_ref[grp]
            du_parts.append(dgated[:, cols] * sg)
            gws = lax.dot_general(dsb[:, cols], vn[:, cols], NT, preferred_element_type=F32)
            gws = jnp.where(mask, gws, 0.0)
            gbs = jnp.sum(ds[:, cols], axis=-1, keepdims=True)

            @pl.when(step == 0)
            def _():
                gws_ref[grp] = gws
                gbs_ref[grp] = gbs

            @pl.when(step > 0)
            def _():
                gws_ref[grp] += gws
                gbs_ref[grp] += gbs

            dvn_parts.append(lax.dot_general(wm, dsb[:, cols], TN, preferred_element_type=F32))
        du = jnp.concatenate(du_parts, axis=-1)
        dvn = jnp.concatenate(dvn_parts, axis=-1)
        _accumulate(glg_ref, jnp.sum(dvn * vhat, axis=0, keepdims=True), step)
        _accumulate(glb_ref, jnp.sum(dvn, axis=0, keepdims=True), step)
        dvhat = dvn * gain
        dv = rstd * (dvhat - jnp.mean(dvhat, axis=-1, keepdims=True)
                     - vhat * jnp.mean(dvhat * vhat, axis=-1, keepdims=True))
        dpre_ref[:, :w] = (du * _gelu_grad(pre_u)).astype(dpre_ref.dtype)
        dpre_ref[:, w:] = (dv * _gelu_grad(pre_v)).astype(dpre_ref.dtype)

    vec = pl.BlockSpec((1, w), lambda i: (0, 0))
    ws_spec = pl.BlockSpec((A_GROUPS, CHUNK, CHUNK), lambda i: (0, 0, 0))
    bs_spec = pl.BlockSpec((A_GROUPS, CHUNK, 1), lambda i: (0, 0, 0))
    return _pcall(
        name, body, [pre, dgated, ln_g, ln_b, w_s, b_s_col],
        [pl.BlockSpec((CHUNK, w2), lambda i: (i, 0)), pl.BlockSpec((CHUNK, w), lambda i: (i, 0)),
         vec, vec, ws_spec, bs_spec],
        [_sds((s, w2), BF16), _sds((1, w), F32), _sds((1, w), F32),
         _sds((A_GROUPS, CHUNK, CHUNK), F32), _sds((A_GROUPS, CHUNK, 1), F32)],
        [pl.BlockSpec((CHUNK, w2), lambda i: (i, 0)), vec, vec, ws_spec, bs_spec],
        grid=(s // CHUNK,), sem=("arbitrary",), after=after)


def _shift_rows(x, k, forward):
    n = x.shape[0]
    row = lax.broadcasted_iota(jnp.int32, x.shape, 0)
    if forward:
        return jnp.where(row >= k, pltpu.roll(x, k, 0), 0.0)
    return jnp.where(row < n - k, pltpu.roll(x, n - k, 0), 0.0)


def _window_sum(x, window, forward):
    k = 1
    while k < window:
        x = x + _shift_rows(x, k, forward)
        k *= 2
    return x


def _pool(name, v, backward, after=None):
    s, w = v.shape
    head = w // B_GROUPS
    lane = _tile(head, 128)

    def body(v_ref, o_ref):
        grp = pl.program_id(0)
        x = v_ref[...]
        t = lax.broadcasted_iota(jnp.int32, x.shape, 0)
        for idx, window in enumerate(B_WINDOWS):
            @pl.when(grp == idx)
            def _():
                inv_count = 1.0 / jnp.minimum(t + 1, window).astype(F32)
                if backward:
                    out = _window_sum(x * inv_count, window, False) - x
                else:
                    out = _window_sum(x, window, True) * inv_count - x
                o_ref[...] = out.astype(o_ref.dtype)

    per = head // lane
    spec = pl.BlockSpec((s, lane), lambda g, j: (0, g * per + j))
    return _pcall(name, body, [v], [spec], _sds((s, w), BF16), spec, grid=(B_GROUPS, per),
                  sem=("parallel", "parallel"), after=after)


def _colsum(name, a, after=None):
    s, d = a.shape
    tr = _tile(s, 256)

    def body(a_ref, o_ref):
        _accumulate(o_ref, jnp.sum(a_ref[...], axis=0, keepdims=True), pl.program_id(0))

    return _pcall(name, body, [a], [pl.BlockSpec((tr, d), lambda i: (i, 0))], _sds((1, d), F32),
                  pl.BlockSpec((1, d), lambda i: (0, 0)), grid=(s // tr,), sem=("arbitrary",), after=after)


def _adamw(w, g, m, v):
    m = ADAM_B1 * m + (1.0 - ADAM_B1) * g
    v = ADAM_B2 * v + (1.0 - ADAM_B2) * (g * g)
    m_hat = m / (1.0 - ADAM_B1 ** ADAM_STEP)
    v_hat = v / (1.0 - ADAM_B2 ** ADAM_STEP)
    delta = -ADAM_LR * (m_hat / (jnp.sqrt(v_hat) + ADAM_EPS) + ADAM_WD * w)
    return delta, m, v


def _adam_rows(name, g, w, m, v, after=None):
    r, c = g.shape
    tr = _tile(r, 256)

    def body(g_ref, w_ref, m_ref, v_ref, d_ref, nm_ref, nv_ref):
        d_ref[...], nm_ref[...], nv_ref[...] = _adamw(w_ref[...], g_ref[...], m_ref[...], v_ref[...])

    spec = pl.BlockSpec((tr, c), lambda i: (i, 0))
    return _pcall(name, body, [g, w, m, v], [spec] * 4, [_sds((r, c), F32)] * 3, [spec] * 3,
                  grid=(r // tr,), sem=("parallel",), after=after)


def _position():
    return lax.axis_index("x"), lax.axis_index("y"), lax.axis_index("c")


def _other_chips(x, y):
    return [(1 - x, y), (x, 1 - y), (1 - x, 1 - y)]


def _slot(px, py, pc):
    return 4 * px + 2 * py + pc


def _hbm(a):
    return pltpu.with_memory_space_constraint(a, pltpu.HBM)


def _hop1_copies(srcs, lands, send_sems, recv_sems):
    x, y, c = _position()
    peers = [(x, y, 1 - c), (1 - x, y, c), (x, 1 - y, c)]
    mine = _slot(x, y, c)
    return [[pltpu.make_async_remote_copy(
        src_ref=srcs[t], dst_ref=lands[t].at[mine], send_sem=send_sems[t].at[k], recv_sem=recv_sems[t].at[k],
        device_id=peer, device_id_type=MESH) for k, peer in enumerate(peers)] for t in range(len(srcs))]


def _hop2_copies(lands, send_sems, recv_sems):
    x, y, c = _position()
    routes = [(_slot(1 - x, y, c), (x, 1 - y, c)), (_slot(x, 1 - y, c), (1 - x, y, c))]
    out = []
    for t in range(len(lands)):
        rows = lands[t].shape[1]
        halves = [(0, rows // 2), (rows // 2, rows - rows // 2)]
        per_tensor = []
        for h, ((slot, peer), (start, size)) in enumerate(zip(routes, halves)):
            if size:
                block = lands[t].at[slot, pl.ds(start, size)]
                per_tensor.append(pltpu.make_async_remote_copy(
                    src_ref=block, dst_ref=block, send_sem=send_sems[t].at[h], recv_sem=recv_sems[t].at[h],
                    device_id=peer, device_id_type=MESH))
        out.append(per_tensor)
    return out


def _split_start(name, srcs, lands, copies, n_sems, after=None):
    n = len(srcs)
    order = [] if after is None else [after]
    n_in = 2 * n + len(order)

    def body(*refs):
        for per_tensor in copies(refs[:n], refs[n:2 * n], refs[n_in:n_in + n], refs[n_in + n:n_in + 2 * n]):
            for cp in per_tensor:
                cp.start()
        refs[-1][...] = jnp.zeros_like(refs[-1])

    out_shape = ([pltpu.SemaphoreType.DMA((n_sems,)) for _ in range(2 * n)]
                 + [pltpu.HBM(a.shape, a.dtype) for a in list(srcs) + list(lands)]
                 + [_sds((8, 128), F32)])
    out = pl.pallas_call(
        body, name=name, out_shape=out_shape, in_specs=[_HBM] * (2 * n) + [_ANY] * len(order),
        out_specs=[_SEM] * (2 * n) + [_HBM] * (2 * n) + [pl.BlockSpec(memory_space=pltpu.VMEM)],
        input_output_aliases={i: 2 * n + i for i in range(2 * n)},
        compiler_params=pltpu.CompilerParams(has_side_effects=_EFFECT),
    )(*[_hbm(a) for a in srcs], *[_hbm(a) for a in lands], *order)
    return [(out[t], out[n + t], out[2 * n + t], out[3 * n + t]) for t in range(n)], out[-1]


def _split_wait(name, started, copies, after):
    n = len(started)

    def body(*refs):
        for per_tensor in copies(refs[:n], refs[n:2 * n], refs[2 * n:3 * n], refs[3 * n:4 * n]):
            for cp in per_tensor:
                cp.wait_send()
                cp.wait_recv()

    srcs = [e[2] for e in started]
    lands = [e[3] for e in started]
    out = pl.pallas_call(
        body, name=name, out_shape=[pltpu.HBM(a.shape, a.dtype) for a in srcs + lands],
        in_specs=[_HBM] * (2 * n) + [_SEM] * (2 * n) + [_ANY], out_specs=[_HBM] * (2 * n),
        input_output_aliases={i: i for i in range(2 * n)},
        compiler_params=pltpu.CompilerParams(has_side_effects=_EFFECT),
    )(*srcs, *lands, *[e[0] for e in started], *[e[1] for e in started], after)
    return out[:n], out[n:]


def _gather_step(name, arrived, fresh, after=None):
    n, m = len(arrived), len(fresh)
    order = [] if after is None else [after]
    fresh_lands = [lax.empty((N_DEV,) + s.shape, s.dtype) for s in fresh]
    buffers = [e[2] for e in arrived] + [e[3] for e in arrived] + list(fresh) + fresh_lands
    old_sems = [e[0] for e in arrived] + [e[1] for e in arrived]
    n_buf, n_old = len(buffers), len(old_sems)
    first_new = n_buf + n_old + len(order)

    def body(*refs):
        bufs, old = refs[:n_buf], refs[n_buf:n_buf + n_old]
        new = refs[first_new:first_new + 2 * n + 2 * m]
        for per_tensor in _hop1_copies(bufs[:n], bufs[n:2 * n], old[:n], old[n:]):
            for cp in per_tensor:
                cp.wait_send()
                cp.wait_recv()
        second = _hop2_copies(bufs[n:2 * n], new[:n], new[n:2 * n])
        first = _hop1_copies(bufs[2 * n:2 * n + m], bufs[2 * n + m:], new[2 * n:2 * n + m], new[2 * n + m:])
        for per_tensor in second + first:
            for cp in per_tensor:
                cp.start()
        refs[-1][...] = jnp.zeros_like(refs[-1])

    n_new = 2 * n + 2 * m
    out_shape = ([pltpu.SemaphoreType.DMA((2,)) for _ in range(2 * n)]
                 + [pltpu.SemaphoreType.DMA((3,)) for _ in range(2 * m)]
                 + [pltpu.HBM(a.shape, a.dtype) for a in buffers] + [_sds((8, 128), F32)])
    out = pl.pallas_call(
        body, name=name, out_shape=out_shape,
        in_specs=[_HBM] * n_buf + [_SEM] * n_old + [_ANY] * len(order),
        out_specs=[_SEM] * n_new + [_HBM] * n_buf + [pl.BlockSpec(memory_space=pltpu.VMEM)],
        input_output_aliases={i: n_new + i for i in range(n_buf)},
        compiler_params=pltpu.CompilerParams(has_side_effects=_EFFECT),
    )(*[_hbm(a) for a in buffers], *old_sems, *order)
    sems, bufs = out[:n_new], out[n_new:n_new + n_buf]
    second = [(sems[t], sems[n + t], bufs[t], bufs[n + t]) for t in range(n)]
    first = [(sems[2 * n + t], sems[2 * n + m + t], bufs[2 * n + t], bufs[2 * n + m + t]) for t in range(m)]
    return second, first, out[-1]


def _gather_wait(name, second, after):
    return _split_wait(name, second, lambda srcs, lands, send, recv: _hop2_copies(lands, send, recv), after)


def _sibling_copies(srcs, lands, send_sems, recv_sems):
    x, y, c = _position()
    return [[pltpu.make_async_remote_copy(
        src_ref=srcs[t], dst_ref=lands[t], send_sem=send_sems[t].at[0], recv_sem=recv_sems[t].at[0],
        device_id=(x, y, 1 - c), device_id_type=MESH)] for t in range(len(srcs))]


def _sibling_start(name, arrays):
    lands = [lax.empty(a.shape, a.dtype) for a in arrays]
    return _split_start(name, arrays, lands, _sibling_copies, 1)


def _sibling_wait(name, started, after):
    return _split_wait(name, started, _sibling_copies, after)[1]


def _small_copies(srcs, lands, send_sems, recv_sems):
    x, y, c = _position()
    mine = _slot(x, y, c)
    peers = [(x ^ ((k >> 2) & 1), y ^ ((k >> 1) & 1), c ^ (k & 1)) for k in range(1, N_DEV)]
    return [[pltpu.make_async_remote_copy(
        src_ref=srcs[t], dst_ref=lands[t].at[mine], send_sem=send_sems[t].at[k], recv_sem=recv_sems[t].at[k],
        device_id=peer, device_id_type=MESH) for k, peer in enumerate(peers)] for t in range(len(srcs))]


def _gather_finish(name, shards, lands, after):
    n = len(shards)

    def body(*refs):
        srcs, lands_in, outs = refs[:n], refs[n:2 * n], refs[2 * n:3 * n]
        send_sems, recv_sems, local_sems = refs[3 * n:]
        x, y, c = _position()
        local = [pltpu.make_async_copy(srcs[t], outs[t].at[_slot(x, y, c)], local_sems.at[t]) for t in range(n)]
        copies = []
        for t in range(n):
            for j, chip in enumerate(_other_chips(x, y)):
                block = outs[t].at[_slot(*chip, c)]
                copies.append(pltpu.make_async_remote_copy(
                    src_ref=block, dst_ref=block, send_sem=send_sems.at[t, j], recv_sem=recv_sems.at[t, j],
                    device_id=(x, y, 1 - c), device_id_type=MESH))
        for cp in local + copies:
            cp.start()
        for cp in copies:
            cp.wait_send()
        for t in range(n):
            for j, chip in enumerate(_other_chips(x, y)):
                block = outs[t].at[_slot(*chip, 1 - c)]
                pltpu.make_async_remote_copy(
                    src_ref=block, dst_ref=block, send_sem=send_sems.at[t, j], recv_sem=recv_sems.at[t, j],
                    device_id=(x, y, 1 - c), device_id_type=MESH).wait_recv()
        for cp in local:
            cp.wait()

    return _pcall(name, body, [*shards, *lands], [_ANY] * (2 * n),
                  [_sds(l.shape, l.dtype) for l in lands], [_ANY] * n,
                  scratch=[pltpu.SemaphoreType.DMA((n, 3)), pltpu.SemaphoreType.DMA((n, 3)),
                           pltpu.SemaphoreType.DMA((n,))],
                  after=after, aliases={n + t: t for t in range(n)})


def _exchange_sibling(name, fulls, after):
    n = len(fulls)

    def body(*refs):
        src = refs[:n]
        out = refs[n:2 * n]
        send_sems, recv_sems = refs[2 * n:]
        x, y, c = _position()
        copies = [pltpu.make_async_remote_copy(
            src_ref=src[t].at[:, 1 - c], dst_ref=out[t], send_sem=send_sems.at[t], recv_sem=recv_sems.at[t],
            device_id=(x, y, 1 - c), device_id_type=MESH) for t in range(n)]
        for cp in copies:
            cp.start()
        for cp in copies:
            cp.wait()

    return _pcall(name, body, fulls, [_ANY] * n, [_sds((N_CHIPS,) + f.shape[2:], f.dtype) for f in fulls],
                  [_ANY] * n, scratch=[pltpu.SemaphoreType.DMA((n,)), pltpu.SemaphoreType.DMA((n,))],
                  after=after)


def _add_sibling(name, full, recv, core, after):
    _, _, r, c = full.shape
    tr = _tile(r, max(8, (256 * 1024) // c))

    def body(core_ref, f_ref, r_ref, o_ref):
        o_ref[...] = (f_ref[...].astype(F32) + r_ref[...].astype(F32)).astype(o_ref.dtype)

    return _pcall(
        name, body, [full, recv],
        [pl.BlockSpec((None, None, tr, c), lambda p, i, core_ref: (p, core_ref[0], i, 0)),
         pl.BlockSpec((None, tr, c), lambda p, i, core_ref: (p, i, 0))],
        _sds((N_CHIPS, r, c), BF16), pl.BlockSpec((None, tr, c), lambda p, i, core_ref: (p, i, 0)),
        grid=(N_CHIPS, r // tr), sem=("parallel", "parallel"), prefetch=[core], after=after)


def _scatter_copies(srcs, lands, send_sems, recv_sems):
    x, y, c = _position()
    return [[pltpu.make_async_remote_copy(
        src_ref=srcs[t].at[2 * px + py], dst_ref=lands[t].at[j],
        send_sem=send_sems[t].at[j], recv_sem=recv_sems[t].at[j],
        device_id=(px, py, c), device_id_type=MESH) for j, (px, py) in enumerate(_other_chips(x, y))]
        for t in range(len(srcs))]


def _scatter_start(name, partials):
    lands = [lax.empty((N_CHIPS - 1,) + p.shape[1:], p.dtype) for p in partials]
    return _split_start(name, partials, lands, _scatter_copies, 3)


def _scatter_wait(name, started, after):
    return _split_wait(name, started, _scatter_copies, after)


def _reduce_adam(name, partial, recv, chip, w, m, v, layer, carried, after):
    n_layers, r, c = w.shape
    tr = _tile(r, max(8, (128 * 1024) // c))

    def body(chip_ref, p_ref, r_ref, w_ref, m_ref, v_ref, *rest):
        g_ref, d_ref, nm_ref, nv_ref = rest[-4:]
        g = p_ref[...].astype(F32)
        for j in range(N_CHIPS - 1):
            g = g + r_ref[j].astype(F32)
        g_ref[...] = g
        d_ref[...], nm_ref[...], nv_ref[...] = _adamw(w_ref[...], g, m_ref[...], v_ref[...])

    layered = pl.BlockSpec((None, tr, c), lambda i, chip_ref: (layer, i, 0))
    in_specs = [pl.BlockSpec((None, tr, c), lambda i, chip_ref: (chip_ref[0], i, 0)),
                pl.BlockSpec((N_CHIPS - 1, tr, c), lambda i, chip_ref: (0, i, 0)),
                layered, layered, layered]
    operands = [partial, recv, w, m, v]
    aliases = {}
    if carried is not None:
        operands += list(carried)
        in_specs += [_ANY] * 4
        aliases = {1 + 5 + o: o for o in range(4)}
    return _pcall(name, body, operands, in_specs, [_sds((n_layers, r, c), F32)] * 4, [layered] * 4,
                  grid=(r // tr,), sem=("parallel",), prefetch=[chip], after=after, aliases=aliases)


def _small_sum(name, gathered, own, device, after=None):
    r, lanes = own.shape

    def body(dev_ref, g_ref, own_ref, out_ref):
        dev = dev_ref[0]
        mine = own_ref[...]
        total = jnp.where(dev == 0, mine, g_ref[0])
        for d in range(1, N_DEV):
            total = total + jnp.where(dev == d, mine, g_ref[d])
        out_ref[...] = total

    return _pcall(name, body, [gathered, own],
                  [pl.BlockSpec((N_DEV, r, lanes), lambda i, dev_ref: (0, 0, 0)),
                   pl.BlockSpec((r, lanes), lambda i, dev_ref: (0, 0))],
                  _sds((r, lanes), F32), pl.BlockSpec((r, lanes), lambda i, dev_ref: (0, 0)),
                  grid=(1,), sem=("arbitrary",), prefetch=[device], after=after)


def _pack(arrays):
    return jnp.concatenate([a.reshape(-1, 128) for a in arrays], axis=0)


def _unpack(packed, shapes):
    out, row = [], 0
    for shape in shapes:
        rows = math.prod(shape) // 128
        out.append(packed[row:row + rows].reshape(shape))
        row += rows
    return out


class _Order:
    def __init__(self):
        self.last = None

    def __call__(self, fn, *args, **kwargs):
        out = fn(*args, after=self.last, **kwargs)
        self.last = out[0] if isinstance(out, (list, tuple)) else out
        return out


def kernel(x, a_w_in, a_ln_g, a_ln_b, a_w_s, a_b_s, a_w_out, b_w_in, b_w_grp, b_scale, b_w_out, norm_mix, norm_mlp, mlp_w1, mlp_w2, final_norm, loss_target, m_a_w_in, m_a_ln_g, m_a_ln_b, m_a_w_s, m_a_b_s, m_a_w_out, m_b_w_in, m_b_w_grp, m_b_scale, m_b_w_out, m_norm_mix, m_norm_mlp, m_mlp_w1, m_mlp_w2, m_final_norm, v_a_w_in, v_a_ln_g, v_a_ln_b, v_a_w_s, v_a_b_s, v_a_w_out, v_b_w_in, v_b_w_grp, v_b_scale, v_b_w_out, v_norm_mix, v_norm_mlp, v_mlp_w1, v_mlp_w2, v_final_norm):
    s, d = x.shape[1], x.shape[2]
    depth = mlp_w1.shape[0]
    a_slab = a_w_in.shape[2]
    ff_slab = mlp_w1.shape[2]
    ff_rows = mlp_w2.shape[1]
    bh = b_w_grp.shape[3]
    my_x, my_y, my_c = _position()
    core = jnp.reshape(my_c, (1,)).astype(jnp.int32)
    chip = jnp.reshape(2 * my_x + my_y, (1,)).astype(jnp.int32)
    device = _slot(my_x, my_y, my_c)
    run = _Order()

    w1_b, w2_b = mlp_w1.astype(BF16), mlp_w2.astype(BF16)
    shards = [a_w_in[0].astype(BF16), a_w_out[0].astype(BF16), b_scale,
              w1_b[0], w2_b[0],
              b_w_in[0].astype(BF16), b_w_grp[0].astype(BF16), b_w_out[0].astype(BF16),
              w1_b[1], w2_b[1]]
    groups = [[0], [1, 2], [3], [4], [5, 6, 7], [8], [9]]
    hop1, hop2 = {}, {}
    _, hop1[0], token = _gather_step("weights_group0_hop1", [], [shards[t] for t in groups[0]])
    _, hop1[1], token = _gather_step("weights_group1_hop1", [], [shards[t] for t in groups[1]], token)
    run.last = token

    def advance(g):
        if g not in hop1:
            return
        fresh = [shards[t] for t in groups[g + 1]] if g + 1 < len(groups) and g + 1 not in hop1 else []
        hop2[g], started, tok = _gather_step(f"weights_group{g}_hop2", hop1.pop(g), fresh, run.last)
        if fresh:
            hop1[g + 1] = started
        run.last = tok

    def gathered(g):
        advance(g)
        if g == 0:
            advance(1)
        srcs, lands = _gather_wait(f"weights_group{g}_wait", hop2.pop(g), run.last)
        run.last = srcs[0]
        out = run(_gather_finish, f"weights_group{g}_finish", srcs, lands)
        advance(g + 1)
        return out

    h0 = x[0]
    target = loss_target[0]
    ln_g, ln_b = a_ln_g, a_ln_b
    w_s = a_w_s[0]
    b_s_col = a_b_s[0][:, :, None]
    nmix = [norm_mix[l][None, :] for l in range(depth)]
    nmlp = [norm_mlp[l][None, :] for l in range(depth)]

    def mlp_forward(l, h, up_group):
        hn = run(_rms_fwd, f"mlp{l}_norm", h, nmlp[l])
        (w1,) = gathered(up_group)
        act, act_sq = run(_mm_nn, f"mlp{l}_up", hn, w1,
                          lambda acc: (jnp.maximum(acc, 0.0), jnp.square(jnp.maximum(acc, 0.0))),
                          (BF16, BF16), slab=True)
        (w2,) = gathered(up_group + 1)
        w2 = w2.reshape(-1, d)
        (h_out,) = run(_mm_nn, f"mlp{l}_down", act_sq, w2, lambda acc, res: (acc + res,), (F32,),
                       extras=(h,), extra_kinds=("tile",))
        return h_out, (h, hn, act, act_sq, w1, w2)

    scattered = []

    def scatter_partials(name, partials, specs):
        in_flight, tok = _scatter_start(name + "_scatter_start", partials)
        run.last = tok
        scattered.append((name, in_flight, specs))

    def weight_grad(name, a, b, by_rows, block, between):
        other = run(_mm_tn_half, name + "_other", a, b, core, False, by_rows, block)
        sent, tok = _sibling_start(name + "_sibling_start", [other])
        run.last = tok
        middle = between()
        (recv,) = _sibling_wait(name + "_sibling_wait", sent, run.last)
        run.last = recv
        return run(_mm_tn_half, name + "_own", a, b, core, True, by_rows, block, recv=recv), middle

    def mlp_backward(l, saved, dh, dhb):
        h, hn, act, act_sq, w1, w2 = saved
        part_w2, (dpre,) = weight_grad(
            f"mlp{l}_down_dw", act_sq, dhb, True, ff_rows,
            lambda: run(_mm_nt, f"mlp{l}_down_dx", dhb, w2, lambda acc, a: (2.0 * a.astype(F32) * acc,),
                        (BF16,), extras=(act,), extra_kinds=("tile",)))
        scatter_partials(f"mlp{l}_down_grads", [part_w2], [("mlp_w2", l)])
        part_w1, (dhn,) = weight_grad(
            f"mlp{l}_up_dw", hn, dpre, False, ff_slab,
            lambda: run(_mm_nt, f"mlp{l}_up_dx", dpre, w1, lambda acc: (acc,), (F32,), slab=True))
        scatter_partials(f"mlp{l}_up_grads", [part_w1], [("mlp_w1", l)])
        dh, dhb, g_norm = run(_rms_bwd, f"mlp{l}_norm_bwd", dhn, h, nmlp[l], dh)
        return dh, dhb, g_norm

    hn0 = run(_rms_fwd, "mix0_norm", h0, nmix[0])
    (wa_in,) = gathered(0)
    (pre,) = run(_mm_nn, "mixa_in", hn0, wa_in, lambda acc: (acc,), (F32,), slab=True)
    wa_out, scale = gathered(1)
    wa_out, scale = wa_out.reshape(d, d), scale.reshape(1, d)
    gated = run(_amix_fwd, "mixa_gate", pre, ln_g, ln_b, w_s, b_s_col)
    (h1,) = run(_mm_nn, "mixa_out", gated, wa_out, lambda acc, res: (acc + res,), (F32,),
                extras=(h0,), extra_kinds=("tile",))
    h2, saved_mlp0 = mlp_forward(0, h1, 2)
    hn2 = run(_rms_fwd, "mix1_norm", h2, nmix[1])
    wb_in, wb_grp, wb_out = gathered(4)
    wb_in, wb_out = wb_in.reshape(d, d), wb_out.reshape(d, d)
    wb_grp = jnp.transpose(wb_grp, (1, 0, 2, 3)).reshape(B_GROUPS, bh, bh)
    (vb,) = run(_mm_nn, "mixb_in", hn2, wb_in, lambda acc: (acc,), (F32,))
    pooled = run(_pool, "mixb_pool", vb, backward=False)
    tm = _tile(s, 1024)
    grp_tile = pl.BlockSpec((tm, bh), lambda i, j, k: (i, j))
    grp_weight = pl.BlockSpec((None, bh, bh), lambda i, j, k: (j, 0, 0))
    mixed, mixed_scaled = run(
        _matmul, "mixb_grp", pooled, wb_grp, NN, (s // tm, B_GROUPS, 1), grp_tile, grp_weight,
        [_sds((s, d), BF16), _sds((s, d), BF16)], [grp_tile] * 2,
        (tm, bh), lambda acc, sc: (acc, acc * sc), (scale,), [pl.BlockSpec((1, bh), lambda i, j, k: (0, j))])
    (h3,) = run(_mm_nn, "mixb_out", mixed_scaled, wb_out, lambda acc, res: (acc + res,), (F32,),
                extras=(h2,), extra_kinds=("tile",))
    h4, saved_mlp1 = mlp_forward(1, h3, 5)
    dh, dhb, g_final, loss_part = run(_loss_head, "loss_head", h4, final_norm[None, :], target)

    dh, dhb, g_nmlp1 = mlp_backward(1, saved_mlp1, dh, dhb)
    tks = _tile(s, 1024)
    grp_rows = pl.BlockSpec((tks, bh), lambda i, j, k: (k, j))

    def mixb_middle():
        dms_scaled, dms_mixed = run(
            _mm_nt, "mixb_out_dx", dhb, wb_out,
            lambda acc, sc, mx: (acc * sc, acc * mx.astype(F32)), (BF16, F32),
            extras=(scale, mixed), extra_kinds=("row", "tile"))
        g_scale = run(_colsum, "mixb_scale_dw", dms_mixed)
        (g_wb_grp,) = run(
            _matmul, "mixb_grp_dw", pooled, dms_scaled, TN, (1, B_GROUPS, s // tks), grp_rows, grp_rows,
            [_sds((B_GROUPS, bh, bh), BF16)], [grp_weight], (bh, bh), lambda acc: (acc,))
        (dpooled,) = run(
            _matmul, "mixb_grp_dx", dms_scaled, wb_grp, NT, (s // tm, B_GROUPS, 1), grp_tile, grp_weight,
            [_sds((s, d), F32)], [grp_tile], (tm, bh), lambda acc: (acc,))
        return g_scale, g_wb_grp, run(_pool, "mixb_pool_bwd", dpooled, backward=True)

    part_wb_out, (g_scale, g_wb_grp, dvb) = weight_grad("mixb_out_dw", mixed_scaled, dhb, True, d // N_DEV,
                                                        mixb_middle)
    part_wb_in, (dhn2,) = weight_grad(
        "mixb_in_dw", hn2, dvb, True, d // N_DEV,
        lambda: run(_mm_nt, "mixb_in_dx", dvb, wb_in, lambda acc: (acc,), (F32,)))
    grp_full = jnp.transpose(g_wb_grp.reshape(B_GROUPS, N_DEV, bh // N_DEV, bh), (1, 0, 2, 3))
    grp_full = grp_full.reshape(N_CHIPS, 2, B_GROUPS * bh // N_DEV, bh)
    (grp_sibling,) = run(_exchange_sibling, "mixb_grp_dw_to_sibling", [grp_full])
    part_wb_grp = run(_add_sibling, "mixb_grp_dw_add_sibling", grp_full, grp_sibling, core)
    scatter_partials("mixb_grads", [part_wb_out, part_wb_grp, part_wb_in],
                     [("b_w_out", 0), ("b_w_grp", 0), ("b_w_in", 0)])
    dh, dhb, g_nmix1 = run(_rms_bwd, "mix1_norm_bwd", dhn2, h2, nmix[1], dh)
    dh, dhb, g_nmlp0 = mlp_backward(0, saved_mlp0, dh, dhb)
    def mixa_middle():
        (dgated,) = run(_mm_nt, "mixa_out_dx", dhb, wa_out, lambda acc: (acc,), (F32,))
        return run(_amix_bwd, "mixa_gate_bwd", pre, dgated, ln_g, ln_b, w_s, b_s_col)

    part_wa_out, (dpre, g_ln_g, g_ln_b, g_w_s, g_b_s) = weight_grad("mixa_out_dw", gated, dhb, True, d // N_DEV,
                                                                     mixa_middle)
    part_wa_in, (dhn0,) = weight_grad(
        "mixa_in_dw", hn0, dpre, False, a_slab,
        lambda: run(_mm_nt, "mixa_in_dx", dpre, wa_in, lambda acc: (acc,), (F32,), slab=True))
    scatter_partials("mixa_grads", [part_wa_in, part_wa_out], [("a_w_in", 0), ("a_w_out", 0)])
    grad_x, _, g_nmix0 = run(_rms_bwd, "mix0_norm_bwd", dhn0, h0, nmix[0], dh)

    g_norm_mix = jnp.concatenate([g_nmix0, g_nmix1], axis=0)
    g_norm_mlp = jnp.concatenate([g_nmlp0, g_nmlp1], axis=0)
    small_parts = [g_ln_g, g_ln_b, g_w_s, g_b_s, g_norm_mix, g_norm_mlp, g_final, g_scale]
    packed = _pack(small_parts)
    small_sent, tok = _split_start("small_grads_start", [packed], [lax.empty((N_DEV,) + packed.shape, F32)],
                                   _small_copies, N_DEV - 1)
    run.last = tok

    weights = {"a_w_in": (a_w_in, m_a_w_in, v_a_w_in), "a_w_out": (a_w_out, m_a_w_out, v_a_w_out),
               "b_w_in": (b_w_in, m_b_w_in, v_b_w_in), "b_w_grp": (b_w_grp, m_b_w_grp, v_b_w_grp),
               "b_w_out": (b_w_out, m_b_w_out, v_b_w_out), "mlp_w1": (mlp_w1, m_mlp_w1, v_mlp_w1),
               "mlp_w2": (mlp_w2, m_mlp_w2, v_mlp_w2)}
    results = {}

    def finish_group(name, in_flight, specs):
        partials, lands = _scatter_wait(name + "_scatter_wait", in_flight, run.last)
        run.last = lands[0]
        for t, (wname, layer) in enumerate(specs):
            w, m, v = weights[wname]
            _, r, c = partials[t].shape
            layers = w.shape[0]
            results[wname] = run(_reduce_adam, f"{name}_reduce_adam_{t}", partials[t], lands[t], chip,
                                 w.reshape(layers, r, c), m.reshape(layers, r, c), v.reshape(layers, r, c),
                                 layer, results.get(wname))

    for group in scattered[:-1]:
        finish_group(*group)
    own_packed, small_gathered = _split_wait("small_grads_wait", small_sent, _small_copies, run.last)
    run.last = small_gathered[0]
    small_sum = run(_small_sum, "small_grads_sum", small_gathered[0], own_packed[0],
                    jnp.reshape(device, (1,)).astype(jnp.int32))
    sg = _unpack(small_sum, [a_ln_g.shape, a_ln_b.shape, a_w_s.shape, a_b_s.shape, norm_mix.shape,
                             norm_mlp.shape, final_norm.shape, (1, d)])
    shard = b_scale.shape[1]
    sg[7] = lax.dynamic_slice(sg[7], (0, device * shard), (1, shard))
    small_w = [a_ln_g, a_ln_b, a_w_s, a_b_s, norm_mix, norm_mlp, final_norm, b_scale]
    small_m = [m_a_ln_g, m_a_ln_b, m_a_w_s, m_a_b_s, m_norm_mix, m_norm_mlp, m_final_norm, m_b_scale]
    small_v = [v_a_ln_g, v_a_ln_b, v_a_w_s, v_a_b_s, v_norm_mix, v_norm_mlp, v_final_norm, v_b_scale]
    small_out = run(_adam_rows, "small_adam", _pack(sg), _pack(small_w), _pack(small_m), _pack(small_v))
    shapes = [w.shape for w in small_w]
    small_res = [sg] + [_unpack(o, shapes) for o in small_out]

    finish_group(*scattered[-1])
    big = {wname: [o.reshape(weights[wname][0].shape) for o in outs] for wname, outs in results.items()}

    loss = lax.psum(loss_part[0, 0], ("x", "y", "c"))

    def leaf(o):
        return (big["a_w_in"][o], small_res[o][0], small_res[o][1], small_res[o][2], small_res[o][3],
                big["a_w_out"][o], big["b_w_in"][o], big["b_w_grp"][o], small_res[o][7], big["b_w_out"][o],
                small_res[o][4], small_res[o][5], big["mlp_w1"][o], big["mlp_w2"][o], small_res[o][6])

    return (loss, grad_x[None], *leaf(0), *leaf(1), *leaf(2), *leaf(3))
```

```python
import math

import jax
import jax.numpy as jnp
from jax import lax
from jax.experimental import pallas as pl
from jax.experimental.pallas import tpu as pltpu

F32 = jnp.float32
BF16 = jnp.bfloat16
MESH = pl.DeviceIdType.MESH

N_DEV = 8
N_CHIPS = 4
CHUNK = 128
A_GROUPS = 8
B_WINDOWS = (2, 4, 8, 16)
B_GROUPS = len(B_WINDOWS)
EPS = 1e-6
ADAM_LR = 0.001
ADAM_B1 = 0.9
ADAM_B2 = 0.999
ADAM_EPS = 1e-08
ADAM_WD = 0.01
ADAM_STEP = 10

VMEM_LIMIT = 48 * 1024 * 1024

NN = (((1,), (0,)), ((), ()))
NT = (((1,), (1,)), ((), ()))
TN = (((0,), (0,)), ((), ()))

_ANY = pl.BlockSpec(memory_space=pl.ANY)
_HBM = pl.BlockSpec(memory_space=pltpu.HBM)
_SEM = pl.BlockSpec(memory_space=pltpu.SEMAPHORE)
_EFFECT = pltpu.SideEffectType.DATAFLOW_SIDE_EFFECTING


def _tile(n, pref):
    return pref if n % pref == 0 else n


def _sds(shape, dtype):
    return jax.ShapeDtypeStruct(shape, dtype)


def _pcall(name, body, operands, in_specs, out_shape, out_specs, *, grid=None, sem=None, scratch=(),
           prefetch=(), after=None, aliases=None):
    after = [] if after is None else [after]
    n_lead = len(prefetch) + len(operands)
    n_after = len(after)

    def wrapped(*refs):
        body(*refs[:n_lead], *refs[n_lead + n_after:])

    in_specs = list(in_specs) + [_ANY] * n_after
    params = pltpu.CompilerParams(vmem_limit_bytes=VMEM_LIMIT) if sem is None else \
        pltpu.CompilerParams(dimension_semantics=sem, vmem_limit_bytes=VMEM_LIMIT)
    kwargs = dict(out_shape=out_shape, scratch_shapes=list(scratch), compiler_params=params, name=name,
                  input_output_aliases=aliases or {})
    if prefetch:
        kwargs["grid_spec"] = pltpu.PrefetchScalarGridSpec(
            num_scalar_prefetch=len(prefetch), grid=grid, in_specs=in_specs, out_specs=out_specs,
            scratch_shapes=list(scratch))
        kwargs.pop("scratch_shapes")
    else:
        kwargs.update(in_specs=in_specs, out_specs=out_specs)
        if grid is not None:
            kwargs["grid"] = grid
    return pl.pallas_call(wrapped, **kwargs)(*prefetch, *operands, *after)


def _matmul(name, a, b, dims, grid, a_spec, b_spec, out_shape, out_specs, acc_shape,
            epilogue, extras=(), extra_specs=(), after=None, prefetch=()):
    nk = grid[2]
    n_extra = len(extras)
    n_out = len(out_shape)
    n_pre = len(prefetch)

    def body(*refs):
        refs = refs[n_pre:]
        a_ref, b_ref = refs[0], refs[1]
        extra_refs = refs[2:2 + n_extra]
        out_refs = refs[2 + n_extra:2 + n_extra + n_out]

        def finish(acc):
            outs = epilogue(acc, *[r[...] for r in extra_refs])
            for o_ref, o in zip(out_refs, outs):
                o_ref[...] = o.astype(o_ref.dtype)

        def product():
            return lax.dot_general(a_ref[...], b_ref[...], dims, preferred_element_type=F32)

        if nk == 1:
            finish(product())
        else:
            acc_ref = refs[-1]
            k = pl.program_id(2)

            @pl.when(k == 0)
            def _():
                acc_ref[...] = product()

            if nk > 2:
                @pl.when(jnp.logical_and(k > 0, k < nk - 1))
                def _():
                    acc_ref[...] += product()

            @pl.when(k == nk - 1)
            def _():
                finish(acc_ref[...] + product())

    scratch = [] if nk == 1 else [pltpu.VMEM(acc_shape, F32)]
    return _pcall(name, body, [a, b, *extras], [a_spec, b_spec, *extra_specs], out_shape, out_specs,
                  grid=grid, sem=("parallel", "parallel", "arbitrary"), scratch=scratch, after=after,
                  prefetch=prefetch)


def _mm_nn(name, a, b, epilogue, out_dtypes, extras=(), extra_kinds=(), slab=False, after=None,
           tm=1024, tn=1024, tk=2048):
    m, kd = a.shape
    if slab:
        n_slab, _, w = b.shape
        n = n_slab * w
        tn = _tile(w, min(tn, w))
        per = w // tn
        tk = _tile(kd, tk)
        b_spec = pl.BlockSpec((None, tk, tn), lambda i, j, k: (j // per, k, j % per))
    else:
        n = b.shape[1]
        tn = _tile(n, tn)
        tk = _tile(kd, tk)
        b_spec = pl.BlockSpec((tk, tn), lambda i, j, k: (k, j))
    tm = _tile(m, tm)
    grid = (m // tm, n // tn, kd // tk)
    a_spec = pl.BlockSpec((tm, tk), lambda i, j, k: (i, k))
    tile_spec = pl.BlockSpec((tm, tn), lambda i, j, k: (i, j))
    row_spec = pl.BlockSpec((1, tn), lambda i, j, k: (0, j))
    extra_specs = [tile_spec if kind == "tile" else row_spec for kind in extra_kinds]
    return _matmul(name, a, b, NN, grid, a_spec, b_spec,
                   [_sds((m, n), d) for d in out_dtypes], [tile_spec for _ in out_dtypes],
                   (tm, tn), epilogue, extras, extra_specs, after=after)


def _mm_nt(name, a, b, epilogue, out_dtypes, extras=(), extra_kinds=(), slab=False, after=None,
           tm=1024, tn=1024, tk=2048):
    m, kd = a.shape
    if slab:
        n_slab, n, w = b.shape
        tk = _tile(w, min(tk, w))
        per = w // tk
        tn = _tile(n, tn)
        b_spec = pl.BlockSpec((None, tn, tk), lambda i, j, k: (k // per, j, k % per))
    else:
        n = b.shape[0]
        tn = _tile(n, tn)
        tk = _tile(kd, tk)
        b_spec = pl.BlockSpec((tn, tk), lambda i, j, k: (j, k))
    tm = _tile(m, tm)
    grid = (m // tm, n // tn, kd // tk)
    a_spec = pl.BlockSpec((tm, tk), lambda i, j, k: (i, k))
    tile_spec = pl.BlockSpec((tm, tn), lambda i, j, k: (i, j))
    row_spec = pl.BlockSpec((1, tn), lambda i, j, k: (0, j))
    extra_specs = [tile_spec if kind == "tile" else row_spec for kind in extra_kinds]
    return _matmul(name, a, b, NT, grid, a_spec, b_spec,
                   [_sds((m, n), d) for d in out_dtypes], [tile_spec for _ in out_dtypes],
                   (tm, tn), epilogue, extras, extra_specs, after=after)


def _mm_tn_half(name, a, b, core, own, by_rows, block, recv=None, after=None, tm=1024, tn=1024, tk=2048):
    s, m = a.shape
    n = b.shape[1]
    tk = _tile(s, tk)

    def owner(chip, core_ref):
        return 2 * chip + (core_ref[0] if own else 1 - core_ref[0])

    if by_rows:
        r, c = block, n
        tm, tn = _tile(r, min(tm, r)), _tile(c, tn)
        per = r // tm
        grid = (N_CHIPS * per, c // tn, s // tk)
        a_spec = pl.BlockSpec((tk, tm), lambda i, j, k, cr: (k, owner(i // per, cr) * per + i % per))
        b_spec = pl.BlockSpec((tk, tn), lambda i, j, k, cr: (k, j))
        o_spec = pl.BlockSpec((None, tm, tn), lambda i, j, k, cr: (i // per, i % per, j))
    else:
        r, c = m, block
        tm, tn = _tile(r, tm), _tile(c, min(tn, c))
        per = c // tn
        grid = (r // tm, N_CHIPS * per, s // tk)
        a_spec = pl.BlockSpec((tk, tm), lambda i, j, k, cr: (k, i))
        b_spec = pl.BlockSpec((tk, tn), lambda i, j, k, cr: (k, owner(j // per, cr) * per + j % per))
        o_spec = pl.BlockSpec((None, tm, tn), lambda i, j, k, cr: (j // per, i, j % per))
    if recv is None:
        extras, epilogue = (), lambda acc: (acc,)
    else:
        extras, epilogue = (recv,), lambda acc, other: (acc + other.astype(F32),)
    return _matmul(name, a, b, TN, grid, a_spec, b_spec, [_sds((N_CHIPS, r, c), BF16)], [o_spec], (tm, tn),
                   epilogue, extras, [o_spec] * len(extras), after=after, prefetch=[core])[0]


def _rms_fwd(name, h, g, after=None):
    s, d = h.shape
    tr = _tile(s, 256)

    def body(h_ref, g_ref, o_ref):
        x = h_ref[...]
        r = lax.rsqrt(jnp.mean(x * x, axis=-1, keepdims=True) + EPS)
        o_ref[...] = (x * r * g_ref[...]).astype(o_ref.dtype)

    row = pl.BlockSpec((tr, d), lambda i: (i, 0))
    vec = pl.BlockSpec((1, d), lambda i: (0, 0))
    return _pcall(name, body, [h, g], [row, vec], _sds((s, d), BF16), row, grid=(s // tr,),
                  sem=("parallel",), after=after)


def _accumulate(ref, part, step):
    @pl.when(step == 0)
    def _():
        ref[...] = part

    @pl.when(step > 0)
    def _():
        ref[...] += part


def _rms_bwd(name, dhn, h, g, dres, after=None):
    s, d = h.shape
    tr = _tile(s, 256)

    def body(dhn_ref, h_ref, g_ref, dres_ref, dh_ref, dhb_ref, gp_ref):
        x = h_ref[...]
        r = lax.rsqrt(jnp.mean(x * x, axis=-1, keepdims=True) + EPS)
        n = x * r
        dy = dhn_ref[...]
        dn = dy * g_ref[...]
        dh = dres_ref[...] + r * (dn - n * jnp.mean(dn * n, axis=-1, keepdims=True))
        dh_ref[...] = dh
        dhb_ref[...] = dh.astype(BF16)
        _accumulate(gp_ref, jnp.sum(dy * n, axis=0, keepdims=True), pl.program_id(0))

    row = pl.BlockSpec((tr, d), lambda i: (i, 0))
    vec = pl.BlockSpec((1, d), lambda i: (0, 0))
    return _pcall(name, body, [dhn, h, g, dres], [row, row, vec, row],
                  [_sds((s, d), F32), _sds((s, d), BF16), _sds((1, d), F32)], [row, row, vec],
                  grid=(s // tr,), sem=("arbitrary",), after=after)


def _loss_head(name, h, g, target, after=None):
    s, d = h.shape
    tr = _tile(s, 256)

    def body(h_ref, g_ref, t_ref, dh_ref, dhb_ref, gp_ref, loss_ref):
        x = h_ref[...]
        gg = g_ref[...]
        r = lax.rsqrt(jnp.mean(x * x, axis=-1, keepdims=True) + EPS)
        n = x * r
        e = n * gg - t_ref[...]
        dy = e * (1.0 / d)
        dn = dy * gg
        dh = r * (dn - n * jnp.mean(dn * n, axis=-1, keepdims=True))
        dh_ref[...] = dh
        dhb_ref[...] = dh.astype(BF16)
        step = pl.program_id(0)
        _accumulate(gp_ref, jnp.sum(dy * n, axis=0, keepdims=True), step)
        row_loss = jnp.mean(e * e, axis=-1, keepdims=True)
        _accumulate(loss_ref, 0.5 * jnp.sum(row_loss, axis=0, keepdims=True), step)

    row = pl.BlockSpec((tr, d), lambda i: (i, 0))
    vec = pl.BlockSpec((1, d), lambda i: (0, 0))
    one = pl.BlockSpec((1, 1), lambda i: (0, 0))
    return _pcall(name, body, [h, g, target], [row, vec, row],
                  [_sds((s, d), F32), _sds((s, d), BF16), _sds((1, d), F32), _sds((1, 1), F32)],
                  [row, row, vec, one], grid=(s // tr,), sem=("arbitrary",), after=after)


_SQRT_HALF = math.sqrt(0.5)
_INV_SQRT_2PI = 1.0 / math.sqrt(2.0 * math.pi)


def _gelu(x):
    return 0.5 * x * (1.0 + lax.erf(x * _SQRT_HALF))


def _gelu_grad(x):
    return 0.5 * (1.0 + lax.erf(x * _SQRT_HALF)) + x * jnp.exp(-0.5 * x * x) * _INV_SQRT_2PI


def _causal_mask():
    row = lax.broadcasted_iota(jnp.int32, (CHUNK, CHUNK), 0)
    col = lax.broadcasted_iota(jnp.int32, (CHUNK, CHUNK), 1)
    return row >= col


def _layernorm_parts(v):
    mu = jnp.mean(v, axis=-1, keepdims=True)
    xc = v - mu
    rstd = lax.rsqrt(jnp.mean(xc * xc, axis=-1, keepdims=True) + EPS)
    return xc * rstd, rstd


def _amix_fwd(name, pre, ln_g, ln_b, w_s, b_s_col, after=None):
    s, w2 = pre.shape
    w = w2 // 2
    head = w // A_GROUPS

    def body(pre_ref, g_ref, b_ref, ws_ref, bs_ref, o_ref):
        u = _gelu(pre_ref[:, :w])
        v = _gelu(pre_ref[:, w:])
        vhat, _ = _layernorm_parts(v)
        vn = (vhat * g_ref[...] + b_ref[...]).astype(BF16)
        mask = _causal_mask()
        for grp in range(A_GROUPS):
            cols = slice(grp * head, (grp + 1) * head)
            wm = jnp.where(mask, ws_ref[grp], 0.0).astype(BF16)
            sg = jnp.dot(wm, vn[:, cols], preferred_element_type=F32) + bs_ref[grp]
            o_ref[:, cols] = (u[:, cols] * sg).astype(o_ref.dtype)

    vec = pl.BlockSpec((1, w), lambda i: (0, 0))
    return _pcall(
        name, body, [pre, ln_g, ln_b, w_s, b_s_col],
        [pl.BlockSpec((CHUNK, w2), lambda i: (i, 0)), vec, vec,
         pl.BlockSpec((A_GROUPS, CHUNK, CHUNK), lambda i: (0, 0, 0)),
         pl.BlockSpec((A_GROUPS, CHUNK, 1), lambda i: (0, 0, 0))],
        _sds((s, w), BF16), pl.BlockSpec((CHUNK, w), lambda i: (i, 0)),
        grid=(s // CHUNK,), sem=("parallel",), after=after)


def _amix_bwd(name, pre, dgated, ln_g, ln_b, w_s, b_s_col, after=None):
    s, w2 = pre.shape
    w = w2 // 2
    head = w // A_GROUPS

    def body(pre_ref, dg_ref, g_ref, b_ref, ws_ref, bs_ref, dpre_ref, glg_ref, glb_ref, gws_ref, gbs_ref):
        step = pl.program_id(0)
        pre_u = pre_ref[:, :w]
        pre_v = pre_ref[:, w:]
        u = _gelu(pre_u)
        v = _gelu(pre_v)
        vhat, rstd = _layernorm_parts(v)
        gain = g_ref[...]
        vn = (vhat * gain + b_ref[...]).astype(BF16)
        dgated = dg_ref[...]
        ds = dgated * u
        dsb = ds.astype(BF16)
        mask = _causal_mask()
        du_parts = []
        dvn_parts = []
        for grp in range(A_GROUPS):
            cols = slice(grp * head, (grp + 1) * head)
            wm = jnp.where(mask, ws_ref[grp], 0.0).astype(BF16)
            sg = jnp.dot(wm, vn[:, cols], preferred_element_type=F32) + bs_ref[grp]
            du_parts.append(dgated[:, cols] * sg)
            gws = lax.dot_general(dsb[:, cols], vn[:, cols], NT, preferred_element_type=F32)
            gws = jnp.where(mask, gws, 0.0)
            gbs = jnp.sum(ds[:, cols], axis=-1, keepdims=True)

            @pl.when(step == 0)
            def _():
                gws_ref[grp] = gws
                gbs_ref[grp] = gbs

            @pl.when(step > 0)
            def _():
                gws_ref[grp] += gws
                gbs_ref[grp] += gbs

            dvn_parts.append(lax.dot_general(wm, dsb[:, cols], TN, preferred_element_type=F32))
        du = jnp.concatenate(du_parts, axis=-1)
        dvn = jnp.concatenate(dvn_parts, axis=-1)
        _accumulate(glg_ref, jnp.sum(dvn * vhat, axis=0, keepdims=True), step)
        _accumulate(glb_ref, jnp.sum(dvn, axis=0, keepdims=True), step)
        dvhat = dvn * gain
        dv = rstd * (dvhat - jnp.mean(dvhat, axis=-1, keepdims=True)
                     - vhat * jnp.mean(dvhat * vhat, axis=-1, keepdims=True))
        dpre_ref[:, :w] = (du * _gelu_grad(pre_u)).astype(dpre_ref.dtype)
        dpre_ref[:, w:] = (dv * _gelu_grad(pre_v)).astype(dpre_ref.dtype)

    vec = pl.BlockSpec((1, w), lambda i: (0, 0))
    ws_spec = pl.BlockSpec((A_GROUPS, CHUNK, CHUNK), lambda i: (0, 0, 0))
    bs_spec = pl.BlockSpec((A_GROUPS, CHUNK, 1), lambda i: (0, 0, 0))
    return _pcall(
        name, body, [pre, dgated, ln_g, ln_b, w_s, b_s_col],
        [pl.BlockSpec((CHUNK, w2), lambda i: (i, 0)), pl.BlockSpec((CHUNK, w), lambda i: (i, 0)),
         vec, vec, ws_spec, bs_spec],
        [_sds((s, w2), BF16), _sds((1, w), F32), _sds((1, w), F32),
         _sds((A_GROUPS, CHUNK, CHUNK), F32), _sds((A_GROUPS, CHUNK, 1), F32)],
        [pl.BlockSpec((CHUNK, w2), lambda i: (i, 0)), vec, vec, ws_spec, bs_spec],
        grid=(s // CHUNK,), sem=("arbitrary",), after=after)


def _shift_rows(x, k, forward):
    n = x.shape[0]
    row = lax.broadcasted_iota(jnp.int32, x.shape, 0)
    if forward:
        return jnp.where(row >= k, pltpu.roll(x, k, 0), 0.0)
    return jnp.where(row < n - k, pltpu.roll(x, n - k, 0), 0.0)


def _window_sum(x, window, forward):
    k = 1
    while k < window:
        x = x + _shift_rows(x, k, forward)
        k *= 2
    return x


def _pool(name, v, backward, after=None):
    s, w = v.shape
    head = w // B_GROUPS
    lane = _tile(head, 128)

    def body(v_ref, o_ref):
        grp = pl.program_id(0)
        x = v_ref[...]
        t = lax.broadcasted_iota(jnp.int32, x.shape, 0)
        for idx, window in enumerate(B_WINDOWS):
            @pl.when(grp == idx)
            def _():
                inv_count = 1.0 / jnp.minimum(t + 1, window).astype(F32)
                if backward:
                    out = _window_sum(x * inv_count, window, False) - x
                else:
                    out = _window_sum(x, window, True) * inv_count - x
                o_ref[...] = out.astype(o_ref.dtype)

    per = head // lane
    spec = pl.BlockSpec((s, lane), lambda g, j: (0, g * per + j))
    return _pcall(name, body, [v], [spec], _sds((s, w), BF16), spec, grid=(B_GROUPS, per),
                  sem=("parallel", "parallel"), after=after)


def _colsum(name, a, after=None):
    s, d = a.shape
    tr = _tile(s, 256)

    def body(a_ref, o_ref):
        _accumulate(o_ref, jnp.sum(a_ref[...], axis=0, keepdims=True), pl.program_id(0))

    return _pcall(name, body, [a], [pl.BlockSpec((tr, d), lambda i: (i, 0))], _sds((1, d), F32),
                  pl.BlockSpec((1, d), lambda i: (0, 0)), grid=(s // tr,), sem=("arbitrary",), after=after)


def _adamw(w, g, m, v):
    m = ADAM_B1 * m + (1.0 - ADAM_B1) * g
    v = ADAM_B2 * v + (1.0 - ADAM_B2) * (g * g)
    m_hat = m / (1.0 - ADAM_B1 ** ADAM_STEP)
    v_hat = v / (1.0 - ADAM_B2 ** ADAM_STEP)
    delta = -ADAM_LR * (m_hat / (jnp.sqrt(v_hat) + ADAM_EPS) + ADAM_WD * w)
    return delta, m, v


def _adam_rows(name, g, w, m, v, after=None):
    r, c = g.shape
    tr = _tile(r, 256)

    def body(g_ref, w_ref, m_ref, v_ref, d_ref, nm_ref, nv_ref):
        d_ref[...], nm_ref[...], nv_ref[...] = _adamw(w_ref[...], g_ref[...], m_ref[...], v_ref[...])

    spec = pl.BlockSpec((tr, c), lambda i: (i, 0))
    return _pcall(name, body, [g, w, m, v], [spec] * 4, [_sds((r, c), F32)] * 3, [spec] * 3,
                  grid=(r // tr,), sem=("parallel",), after=after)


def _position():
    return lax.axis_index("x"), lax.axis_index("y"), lax.axis_index("c")


def _other_chips(x, y):
    return [(1 - x, y), (x, 1 - y), (1 - x, 1 - y)]


def _slot(px, py, pc):
    return 4 * px + 2 * py + pc


def _hbm(a):
    return pltpu.with_memory_space_constraint(a, pltpu.HBM)


def _hop1_copies(srcs, lands, send_sems, recv_sems):
    x, y, c = _position()
    peers = [(x, y, 1 - c), (1 - x, y, c), (x, 1 - y, c)]
    mine = _slot(x, y, c)
    return [[pltpu.make_async_remote_copy(
        src_ref=srcs[t], dst_ref=lands[t].at[mine], send_sem=send_sems[t].at[k], recv_sem=recv_sems[t].at[k],
        device_id=peer, device_id_type=MESH) for k, peer in enumerate(peers)] for t in range(len(srcs))]


def _hop2_copies(lands, send_sems, recv_sems):
    x, y, c = _position()
    routes = [(_slot(1 - x, y, c), (x, 1 - y, c)), (_slot(x, 1 - y, c), (1 - x, y, c))]
    out = []
    for t in range(len(lands)):
        rows = lands[t].shape[1]
        halves = [(0, rows // 2), (rows // 2, rows - rows // 2)]
        per_tensor = []
        for h, ((slot, peer), (start, size)) in enumerate(zip(routes, halves)):
            if size:
                block = lands[t].at[slot, pl.ds(start, size)]
                per_tensor.append(pltpu.make_async_remote_copy(
                    src_ref=block, dst_ref=block, send_sem=send_sems[t].at[h], recv_sem=recv_sems[t].at[h],
                    device_id=peer, device_id_type=MESH))
        for j, (slot, _) in enumerate(routes):
            block = lands[t].at[slot]
            per_tensor.append(pltpu.make_async_remote_copy(
                src_ref=block, dst_ref=block, send_sem=send_sems[t].at[2 + j], recv_sem=recv_sems[t].at[2 + j],
                device_id=(x, y, 1 - c), device_id_type=MESH))
        out.append(per_tensor)
    return out


def _split_start(name, srcs, lands, copies, n_sems, after=None):
    n = len(srcs)
    order = [] if after is None else [after]
    n_in = 2 * n + len(order)

    def body(*refs):
        for per_tensor in copies(refs[:n], refs[n:2 * n], refs[n_in:n_in + n], refs[n_in + n:n_in + 2 * n]):
            for cp in per_tensor:
                cp.start()
        refs[-1][...] = jnp.zeros_like(refs[-1])

    out_shape = ([pltpu.SemaphoreType.DMA((n_sems,)) for _ in range(2 * n)]
                 + [pltpu.HBM(a.shape, a.dtype) for a in list(srcs) + list(lands)]
                 + [_sds((8, 128), F32)])
    out = pl.pallas_call(
        body, name=name, out_shape=out_shape, in_specs=[_HBM] * (2 * n) + [_ANY] * len(order),
        out_specs=[_SEM] * (2 * n) + [_HBM] * (2 * n) + [pl.BlockSpec(memory_space=pltpu.VMEM)],
        input_output_aliases={i: 2 * n + i for i in range(2 * n)},
        compiler_params=pltpu.CompilerParams(has_side_effects=_EFFECT),
    )(*[_hbm(a) for a in srcs], *[_hbm(a) for a in lands], *order)
    return [(out[t], out[n + t], out[2 * n + t], out[3 * n + t]) for t in range(n)], out[-1]


def _split_wait(name, started, copies, after):
    n = len(started)

    def body(*refs):
        for per_tensor in copies(refs[:n], refs[n:2 * n], refs[2 * n:3 * n], refs[3 * n:4 * n]):
            for cp in per_tensor:
                cp.wait_send()
                cp.wait_recv()

    srcs = [e[2] for e in started]
    lands = [e[3] for e in started]
    out = pl.pallas_call(
        body, name=name, out_shape=[pltpu.HBM(a.shape, a.dtype) for a in srcs + lands],
        in_specs=[_HBM] * (2 * n) + [_SEM] * (2 * n) + [_ANY], out_specs=[_HBM] * (2 * n),
        input_output_aliases={i: i for i in range(2 * n)},
        compiler_params=pltpu.CompilerParams(has_side_effects=_EFFECT),
    )(*srcs, *lands, *[e[0] for e in started], *[e[1] for e in started], after)
    return out[:n], out[n:]


def _gather_step(name, arrived, fresh, after=None):
    n, m = len(arrived), len(fresh)
    order = [] if after is None else [after]
    fresh_lands = [lax.empty((N_DEV,) + s.shape, s.dtype) for s in fresh]
    buffers = [e[2] for e in arrived] + [e[3] for e in arrived] + list(fresh) + fresh_lands
    old_sems = [e[0] for e in arrived] + [e[1] for e in arrived]
    n_buf, n_old = len(buffers), len(old_sems)
    first_new = n_buf + n_old + len(order)

    def body(*refs):
        bufs, old = refs[:n_buf], refs[n_buf:n_buf + n_old]
        new = refs[first_new:first_new + 2 * n + 2 * m]
        for per_tensor in _hop1_copies(bufs[:n], bufs[n:2 * n], old[:n], old[n:]):
            for cp in per_tensor:
                cp.wait_send()
                cp.wait_recv()
        second = _hop2_copies(bufs[n:2 * n], new[:n], new[n:2 * n])
        first = _hop1_copies(bufs[2 * n:2 * n + m], bufs[2 * n + m:], new[2 * n:2 * n + m], new[2 * n + m:])
        for per_tensor in second + first:
            for cp in per_tensor:
                cp.start()
        refs[-1][...] = jnp.zeros_like(refs[-1])

    n_new = 2 * n + 2 * m
    out_shape = ([pltpu.SemaphoreType.DMA((4,)) for _ in range(2 * n)]
                 + [pltpu.SemaphoreType.DMA((3,)) for _ in range(2 * m)]
                 + [pltpu.HBM(a.shape, a.dtype) for a in buffers] + [_sds((8, 128), F32)])
    out = pl.pallas_call(
        body, name=name, out_shape=out_shape,
        in_specs=[_HBM] * n_buf + [_SEM] * n_old + [_ANY] * len(order),
        out_specs=[_SEM] * n_new + [_HBM] * n_buf + [pl.BlockSpec(memory_space=pltpu.VMEM)],
        input_output_aliases={i: n_new + i for i in range(n_buf)},
        compiler_params=pltpu.CompilerParams(has_side_effects=_EFFECT),
    )(*[_hbm(a) for a in buffers], *old_sems, *order)
    sems, bufs = out[:n_new], out[n_new:n_new + n_buf]
    second = [(sems[t], sems[n + t], bufs[t], bufs[n + t]) for t in range(n)]
    first = [(sems[2 * n + t], sems[2 * n + m + t], bufs[2 * n + t], bufs[2 * n + m + t]) for t in range(m)]
    return second, first, out[-1]


def _gather_wait(name, second, after):
    return _split_wait(name, second, lambda srcs, lands, send, recv: _hop2_copies(lands, send, recv), after)


def _sibling_copies(srcs, lands, send_sems, recv_sems):
    x, y, c = _position()
    return [[pltpu.make_async_remote_copy(
        src_ref=srcs[t], dst_ref=lands[t], send_sem=send_sems[t].at[0], recv_sem=recv_sems[t].at[0],
        device_id=(x, y, 1 - c), device_id_type=MESH)] for t in range(len(srcs))]


def _sibling_start(name, arrays):
    lands = [lax.empty(a.shape, a.dtype) for a in arrays]
    return _split_start(name, arrays, lands, _sibling_copies, 1)


def _sibling_wait(name, started, after):
    return _split_wait(name, started, _sibling_copies, after)[1]


def _small_copies(srcs, lands, send_sems, recv_sems):
    x, y, c = _position()
    mine = _slot(x, y, c)
    peers = [(x ^ ((k >> 2) & 1), y ^ ((k >> 1) & 1), c ^ (k & 1)) for k in range(1, N_DEV)]
    return [[pltpu.make_async_remote_copy(
        src_ref=srcs[t], dst_ref=lands[t].at[mine], send_sem=send_sems[t].at[k], recv_sem=recv_sems[t].at[k],
        device_id=peer, device_id_type=MESH) for k, peer in enumerate(peers)] for t in range(len(srcs))]


def _gather_finish(name, shards, lands, after):
    n = len(shards)

    def body(*refs):
        srcs, lands_in, outs = refs[:n], refs[n:2 * n], refs[2 * n:3 * n]
        send_sems, recv_sems, local_sems = refs[3 * n:]
        x, y, c = _position()
        local = [pltpu.make_async_copy(srcs[t], outs[t].at[_slot(x, y, c)], local_sems.at[t]) for t in range(n)]

        def diagonal(t, core):
            block = outs[t].at[_slot(1 - x, 1 - y, core)]
            return pltpu.make_async_remote_copy(
                src_ref=block, dst_ref=block, send_sem=send_sems.at[t], recv_sem=recv_sems.at[t],
                device_id=(x, y, 1 - c), device_id_type=MESH)

        for cp in local:
            cp.start()
        for t in range(n):
            diagonal(t, c).start()
        for t in range(n):
            diagonal(t, c).wait_send()
            diagonal(t, 1 - c).wait_recv()
        for cp in local:
            cp.wait()

    return _pcall(name, body, [*shards, *lands], [_ANY] * (2 * n),
                  [_sds(l.shape, l.dtype) for l in lands], [_ANY] * n,
                  scratch=[pltpu.SemaphoreType.DMA((n,)), pltpu.SemaphoreType.DMA((n,)),
                           pltpu.SemaphoreType.DMA((n,))],
                  after=after, aliases={n + t: t for t in range(n)})


def _exchange_sibling(name, fulls, after):
    n = len(fulls)

    def body(*refs):
        src = refs[:n]
        out = refs[n:2 * n]
        send_sems, recv_sems = refs[2 * n:]
        x, y, c = _position()
        copies = [pltpu.make_async_remote_copy(
            src_ref=src[t].at[:, 1 - c], dst_ref=out[t], send_sem=send_sems.at[t], recv_sem=recv_sems.at[t],
            device_id=(x, y, 1 - c), device_id_type=MESH) for t in range(n)]
        for cp in copies:
            cp.start()
        for cp in copies:
            cp.wait()

    return _pcall(name, body, fulls, [_ANY] * n, [_sds((N_CHIPS,) + f.shape[2:], f.dtype) for f in fulls],
                  [_ANY] * n, scratch=[pltpu.SemaphoreType.DMA((n,)), pltpu.SemaphoreType.DMA((n,))],
                  after=after)


def _add_sibling(name, full, recv, core, after):
    _, _, r, c = full.shape
    tr = _tile(r, max(8, (256 * 1024) // c))

    def body(core_ref, f_ref, r_ref, o_ref):
        o_ref[...] = (f_ref[...].astype(F32) + r_ref[...].astype(F32)).astype(o_ref.dtype)

    return _pcall(
        name, body, [full, recv],
        [pl.BlockSpec((None, None, tr, c), lambda p, i, core_ref: (p, core_ref[0], i, 0)),
         pl.BlockSpec((None, tr, c), lambda p, i, core_ref: (p, i, 0))],
        _sds((N_CHIPS, r, c), BF16), pl.BlockSpec((None, tr, c), lambda p, i, core_ref: (p, i, 0)),
        grid=(N_CHIPS, r // tr), sem=("parallel", "parallel"), prefetch=[core], after=after)


def _scatter_copies(srcs, lands, send_sems, recv_sems):
    x, y, c = _position()
    return [[pltpu.make_async_remote_copy(
        src_ref=srcs[t].at[2 * px + py], dst_ref=lands[t].at[j],
        send_sem=send_sems[t].at[j], recv_sem=recv_sems[t].at[j],
        device_id=(px, py, c), device_id_type=MESH) for j, (px, py) in enumerate(_other_chips(x, y))]
        for t in range(len(srcs))]


def _scatter_start(name, partials):
    lands = [lax.empty((N_CHIPS - 1,) + p.shape[1:], p.dtype) for p in partials]
    return _split_start(name, partials, lands, _scatter_copies, 3)


def _scatter_wait(name, started, after):
    return _split_wait(name, started, _scatter_copies, after)


def _reduce_adam(name, partial, recv, chip, w, m, v, layer, carried, after):
    n_layers, r, c = w.shape
    tr = _tile(r, max(8, (128 * 1024) // c))

    def body(chip_ref, p_ref, r_ref, w_ref, m_ref, v_ref, *rest):
        g_ref, d_ref, nm_ref, nv_ref = rest[-4:]
        g = p_ref[...].astype(F32)
        for j in range(N_CHIPS - 1):
            g = g + r_ref[j].astype(F32)
        g_ref[...] = g
        d_ref[...], nm_ref[...], nv_ref[...] = _adamw(w_ref[...], g, m_ref[...], v_ref[...])

    layered = pl.BlockSpec((None, tr, c), lambda i, chip_ref: (layer, i, 0))
    in_specs = [pl.BlockSpec((None, tr, c), lambda i, chip_ref: (chip_ref[0], i, 0)),
                pl.BlockSpec((N_CHIPS - 1, tr, c), lambda i, chip_ref: (0, i, 0)),
                layered, layered, layered]
    operands = [partial, recv, w, m, v]
    aliases = {}
    if carried is not None:
        operands += list(carried)
        in_specs += [_ANY] * 4
        aliases = {1 + 5 + o: o for o in range(4)}
    return _pcall(name, body, operands, in_specs, [_sds((n_layers, r, c), F32)] * 4, [layered] * 4,
                  grid=(r // tr,), sem=("parallel",), prefetch=[chip], after=after, aliases=aliases)


def _small_sum(name, gathered, own, device, after=None):
    r, lanes = own.shape

    def body(dev_ref, g_ref, own_ref, out_ref):
        dev = dev_ref[0]
        mine = own_ref[...]
        total = jnp.where(dev == 0, mine, g_ref[0])
        for d in range(1, N_DEV):
            total = total + jnp.where(dev == d, mine, g_ref[d])
        out_ref[...] = total

    return _pcall(name, body, [gathered, own],
                  [pl.BlockSpec((N_DEV, r, lanes), lambda i, dev_ref: (0, 0, 0)),
                   pl.BlockSpec((r, lanes), lambda i, dev_ref: (0, 0))],
                  _sds((r, lanes), F32), pl.BlockSpec((r, lanes), lambda i, dev_ref: (0, 0)),
                  grid=(1,), sem=("arbitrary",), prefetch=[device], after=after)


def _pack(arrays):
    return jnp.concatenate([a.reshape(-1, 128) for a in arrays], axis=0)


def _unpack(packed, shapes):
    out, row = [], 0
    for shape in shapes:
        rows = math.prod(shape) // 128
        out.append(packed[row:row + rows].reshape(shape))
        row += rows
    return out


class _Order:
    def __init__(self):
        self.last = None

    def __call__(self, fn, *args, **kwargs):
        out = fn(*args, after=self.last, **kwargs)
        self.last = out[0] if isinstance(out, (list, tuple)) else out
        return out


def kernel(x, a_w_in, a_ln_g, a_ln_b, a_w_s, a_b_s, a_w_out, b_w_in, b_w_grp, b_scale, b_w_out, norm_mix, norm_mlp, mlp_w1, mlp_w2, final_norm, loss_target, m_a_w_in, m_a_ln_g, m_a_ln_b, m_a_w_s, m_a_b_s, m_a_w_out, m_b_w_in, m_b_w_grp, m_b_scale, m_b_w_out, m_norm_mix, m_norm_mlp, m_mlp_w1, m_mlp_w2, m_final_norm, v_a_w_in, v_a_ln_g, v_a_ln_b, v_a_w_s, v_a_b_s, v_a_w_out, v_b_w_in, v_b_w_grp, v_b_scale, v_b_w_out, v_norm_mix, v_norm_mlp, v_mlp_w1, v_mlp_w2, v_final_norm):
    s, d = x.shape[1], x.shape[2]
    depth = mlp_w1.shape[0]
    a_slab = a_w_in.shape[2]
    ff_slab = mlp_w1.shape[2]
    ff_rows = mlp_w2.shape[1]
    bh = b_w_grp.shape[3]
    my_x, my_y, my_c = _position()
    core = jnp.reshape(my_c, (1,)).astype(jnp.int32)
    chip = jnp.reshape(2 * my_x + my_y, (1,)).astype(jnp.int32)
    device = _slot(my_x, my_y, my_c)
    run = _Order()

    w1_b, w2_b = mlp_w1.astype(BF16), mlp_w2.astype(BF16)
    shards = [a_w_in[0].astype(BF16), a_w_out[0].astype(BF16), b_scale,
              w1_b[0], w2_b[0],
              b_w_in[0].astype(BF16), b_w_grp[0].astype(BF16), b_w_out[0].astype(BF16),
              w1_b[1], w2_b[1]]
    groups = [[0], [1, 2], [3], [4], [5, 6, 7], [8], [9]]
    hop1, hop2 = {}, {}
    _, hop1[0], token = _gather_step("weights_group0_hop1", [], [shards[t] for t in groups[0]])
    _, hop1[1], token = _gather_step("weights_group1_hop1", [], [shards[t] for t in groups[1]], token)
    run.last = token

    def advance(g):
        if g not in hop1:
            return
        fresh = [shards[t] for t in groups[g + 1]] if g + 1 < len(groups) and g + 1 not in hop1 else []
        hop2[g], started, tok = _gather_step(f"weights_group{g}_hop2", hop1.pop(g), fresh, run.last)
        if fresh:
            hop1[g + 1] = started
        run.last = tok

    def gathered(g):
        advance(g)
        if g == 0:
            advance(1)
        srcs, lands = _gather_wait(f"weights_group{g}_wait", hop2.pop(g), run.last)
        run.last = srcs[0]
        out = run(_gather_finish, f"weights_group{g}_finish", srcs, lands)
        advance(g + 1)
        return out

    h0 = x[0]
    target = loss_target[0]
    ln_g, ln_b = a_ln_g, a_ln_b
    w_s = a_w_s[0]
    b_s_col = a_b_s[0][:, :, None]
    nmix = [norm_mix[l][None, :] for l in range(depth)]
    nmlp = [norm_mlp[l][None, :] for l in range(depth)]

    def mlp_forward(l, h, up_group):
        hn = run(_rms_fwd, f"mlp{l}_norm", h, nmlp[l])
        (w1,) = gathered(up_group)
        act, act_sq = run(_mm_nn, f"mlp{l}_up", hn, w1,
                          lambda acc: (jnp.maximum(acc, 0.0), jnp.square(jnp.maximum(acc, 0.0))),
                          (BF16, BF16), slab=True)
        (w2,) = gathered(up_group + 1)
        w2 = w2.reshape(-1, d)
        (h_out,) = run(_mm_nn, f"mlp{l}_down", act_sq, w2, lambda acc, res: (acc + res,), (F32,),
                       extras=(h,), extra_kinds=("tile",))
        return h_out, (h, hn, act, act_sq, w1, w2)

    scattered = []

    def scatter_partials(name, partials, specs):
        in_flight, tok = _scatter_start(name + "_scatter_start", partials)
        run.last = tok
        scattered.append((name, in_flight, specs))

    def weight_grad(name, a, b, by_rows, block, between):
        other = run(_mm_tn_half, name + "_other", a, b, core, False, by_rows, block)
        sent, tok = _sibling_start(name + "_sibling_start", [other])
        run.last = tok
        middle = between()
        (recv,) = _sibling_wait(name + "_sibling_wait", sent, run.last)
        run.last = recv
        return run(_mm_tn_half, name + "_own", a, b, core, True, by_rows, block, recv=recv), middle

    def mlp_backward(l, saved, dh, dhb):
        h, hn, act, act_sq, w1, w2 = saved
        part_w2, (dpre,) = weight_grad(
            f"mlp{l}_down_dw", act_sq, dhb, True, ff_rows,
            lambda: run(_mm_nt, f"mlp{l}_down_dx", dhb, w2, lambda acc, a: (2.0 * a.astype(F32) * acc,),
                        (BF16,), extras=(act,), extra_kinds=("tile",)))
        scatter_partials(f"mlp{l}_down_grads", [part_w2], [("mlp_w2", l)])
        part_w1, (dhn,) = weight_grad(
            f"mlp{l}_up_dw", hn, dpre, False, ff_slab,
            lambda: run(_mm_nt, f"mlp{l}_up_dx", dpre, w1, lambda acc: (acc,), (F32,), slab=True))
        scatter_partials(f"mlp{l}_up_grads", [part_w1], [("mlp_w1", l)])
        dh, dhb, g_norm = run(_rms_bwd, f"mlp{l}_norm_bwd", dhn, h, nmlp[l], dh)
        return dh, dhb, g_norm

    hn0 = run(_rms_fwd, "mix0_norm", h0, nmix[0])
    (wa_in,) = gathered(0)
    (pre,) = run(_mm_nn, "mixa_in", hn0, wa_in, lambda acc: (acc,), (F32,), slab=True)
    wa_out, scale = gathered(1)
    wa_out, scale = wa_out.reshape(d, d), scale.reshape(1, d)
    gated = run(_amix_fwd, "mixa_gate", pre, ln_g, ln_b, w_s, b_s_col)
    (h1,) = run(_mm_nn, "mixa_out", gated, wa_out, lambda acc, res: (acc + res,), (F32,),
                extras=(h0,), extra_kinds=("tile",))
    h2, saved_mlp0 = mlp_forward(0, h1, 2)
    hn2 = run(_rms_fwd, "mix1_norm", h2, nmix[1])
    wb_in, wb_grp, wb_out = gathered(4)
    wb_in, wb_out = wb_in.reshape(d, d), wb_out.reshape(d, d)
    wb_grp = jnp.transpose(wb_grp, (1, 0, 2, 3)).reshape(B_GROUPS, bh, bh)
    (vb,) = run(_mm_nn, "mixb_in", hn2, wb_in, lambda acc: (acc,), (F32,))
    pooled = run(_pool, "mixb_pool", vb, backward=False)
    tm = _tile(s, 1024)
    grp_tile = pl.BlockSpec((tm, bh), lambda i, j, k: (i, j))
    grp_weight = pl.BlockSpec((None, bh, bh), lambda i, j, k: (j, 0, 0))
    mixed, mixed_scaled = run(
        _matmul, "mixb_grp", pooled, wb_grp, NN, (s // tm, B_GROUPS, 1), grp_tile, grp_weight,
        [_sds((s, d), BF16), _sds((s, d), BF16)], [grp_tile] * 2,
        (tm, bh), lambda acc, sc: (acc, acc * sc), (scale,), [pl.BlockSpec((1, bh), lambda i, j, k: (0, j))])
    (h3,) = run(_mm_nn, "mixb_out", mixed_scaled, wb_out, lambda acc, res: (acc + res,), (F32,),
                extras=(h2,), extra_kinds=("tile",))
    h4, saved_mlp1 = mlp_forward(1, h3, 5)
    dh, dhb, g_final, loss_part = run(_loss_head, "loss_head", h4, final_norm[None, :], target)

    dh, dhb, g_nmlp1 = mlp_backward(1, saved_mlp1, dh, dhb)
    tks = _tile(s, 1024)
    grp_rows = pl.BlockSpec((tks, bh), lambda i, j, k: (k, j))

    def mixb_middle():
        dms_scaled, dms_mixed = run(
            _mm_nt, "mixb_out_dx", dhb, wb_out,
            lambda acc, sc, mx: (acc * sc, acc * mx.astype(F32)), (BF16, F32),
            extras=(scale, mixed), extra_kinds=("row", "tile"))
        g_scale = run(_colsum, "mixb_scale_dw", dms_mixed)
        (g_wb_grp,) = run(
            _matmul, "mixb_grp_dw", pooled, dms_scaled, TN, (1, B_GROUPS, s // tks), grp_rows, grp_rows,
            [_sds((B_GROUPS, bh, bh), BF16)], [grp_weight], (bh, bh), lambda acc: (acc,))
        (dpooled,) = run(
            _matmul, "mixb_grp_dx", dms_scaled, wb_grp, NT, (s // tm, B_GROUPS, 1), grp_tile, grp_weight,
            [_sds((s, d), F32)], [grp_tile], (tm, bh), lambda acc: (acc,))
        return g_scale, g_wb_grp, run(_pool, "mixb_pool_bwd", dpooled, backward=True)

    part_wb_out, (g_scale, g_wb_grp, dvb) = weight_grad("mixb_out_dw", mixed_scaled, dhb, True, d // N_DEV,
                                                        mixb_middle)
    part_wb_in, (dhn2,) = weight_grad(
        "mixb_in_dw", hn2, dvb, True, d // N_DEV,
        lambda: run(_mm_nt, "mixb_in_dx", dvb, wb_in, lambda acc: (acc,), (F32,)))
    grp_full = jnp.transpose(g_wb_grp.reshape(B_GROUPS, N_DEV, bh // N_DEV, bh), (1, 0, 2, 3))
    grp_full = grp_full.reshape(N_CHIPS, 2, B_GROUPS * bh // N_DEV, bh)
    (grp_sibling,) = run(_exchange_sibling, "mixb_grp_dw_to_sibling", [grp_full])
    part_wb_grp = run(_add_sibling, "mixb_grp_dw_add_sibling", grp_full, grp_sibling, core)
    scatter_partials("mixb_grads", [part_wb_out, part_wb_grp, part_wb_in],
                     [("b_w_out", 0), ("b_w_grp", 0), ("b_w_in", 0)])
    dh, dhb, g_nmix1 = run(_rms_bwd, "mix1_norm_bwd", dhn2, h2, nmix[1], dh)
    dh, dhb, g_nmlp0 = mlp_backward(0, saved_mlp0, dh, dhb)
    def mixa_middle():
        (dgated,) = run(_mm_nt, "mixa_out_dx", dhb, wa_out, lambda acc: (acc,), (F32,))
        return run(_amix_bwd, "mixa_gate_bwd", pre, dgated, ln_g, ln_b, w_s, b_s_col)

    part_wa_out, (dpre, g_ln_g, g_ln_b, g_w_s, g_b_s) = weight_grad("mixa_out_dw", gated, dhb, True, d // N_DEV,
                                                                     mixa_middle)
    part_wa_in, (dhn0,) = weight_grad(
        "mixa_in_dw", hn0, dpre, False, a_slab,
        lambda: run(_mm_nt, "mixa_in_dx", dpre, wa_in, lambda acc: (acc,), (F32,), slab=True))
    scatter_partials("mixa_grads", [part_wa_in, part_wa_out], [("a_w_in", 0), ("a_w_out", 0)])
    grad_x, _, g_nmix0 = run(_rms_bwd, "mix0_norm_bwd", dhn0, h0, nmix[0], dh)

    g_norm_mix = jnp.concatenate([g_nmix0, g_nmix1], axis=0)
    g_norm_mlp = jnp.concatenate([g_nmlp0, g_nmlp1], axis=0)
    small_parts = [g_ln_g, g_ln_b, g_w_s, g_b_s, g_norm_mix, g_norm_mlp, g_final, g_scale]
    packed = _pack(small_parts)
    small_sent, tok = _split_start("small_grads_start", [packed], [lax.empty((N_DEV,) + packed.shape, F32)],
                                   _small_copies, N_DEV - 1)
    run.last = tok

    weights = {"a_w_in": (a_w_in, m_a_w_in, v_a_w_in), "a_w_out": (a_w_out, m_a_w_out, v_a_w_out),
               "b_w_in": (b_w_in, m_b_w_in, v_b_w_in), "b_w_grp": (b_w_grp, m_b_w_grp, v_b_w_grp),
               "b_w_out": (b_w_out, m_b_w_out, v_b_w_out), "mlp_w1": (mlp_w1, m_mlp_w1, v_mlp_w1),
               "mlp_w2": (mlp_w2, m_mlp_w2, v_mlp_w2)}
    results = {}

    def finish_group(name, in_flight, specs):
        partials, lands = _scatter_wait(name + "_scatter_wait", in_flight, run.last)
        run.last = lands[0]
        for t, (wname, layer) in enumerate(specs):
            w, m, v = weights[wname]
            _, r, c = partials[t].shape
            layers = w.shape[0]
            results[wname] = run(_reduce_adam, f"{name}_reduce_adam_{t}", partials[t], lands[t], chip,
                                 w.reshape(layers, r, c), m.reshape(layers, r, c), v.reshape(layers, r, c),
                                 layer, results.get(wname))

    for group in scattered[:-1]:
        finish_group(*group)
    own_packed, small_gathered = _split_wait("small_grads_wait", small_sent, _small_copies, run.last)
    run.last = small_gathered[0]
    small_sum = run(_small_sum, "small_grads_sum", small_gathered[0], own_packed[0],
                    jnp.reshape(device, (1,)).astype(jnp.int32))
    sg = _unpack(small_sum, [a_ln_g.shape, a_ln_b.shape, a_w_s.shape, a_b_s.shape, norm_mix.shape,
                             norm_mlp.shape, final_norm.shape, (1, d)])
    shard = b_scale.shape[1]
    sg[7] = lax.dynamic_slice(sg[7], (0, device * shard), (1, shard))
    small_w = [a_ln_g, a_ln_b, a_w_s, a_b_s, norm_mix, norm_mlp, final_norm, b_scale]
    small_m = [m_a_ln_g, m_a_ln_b, m_a_w_s, m_a_b_s, m_norm_mix, m_norm_mlp, m_final_norm, m_b_scale]
    small_v = [v_a_ln_g, v_a_ln_b, v_a_w_s, v_a_b_s, v_norm_mix, v_norm_mlp, v_final_norm, v_b_scale]
    small_out = run(_adam_rows, "small_adam", _pack(sg), _pack(small_w), _pack(small_m), _pack(small_v))
    shapes = [w.shape for w in small_w]
    small_res = [sg] + [_unpack(o, shapes) for o in small_out]

    finish_group(*scattered[-1])
    big = {wname: [o.reshape(weights[wname][0].shape) for o in outs] for wname, outs in results.items()}

    loss = lax.psum(loss_part[0, 0], ("x", "y", "c"))

    def leaf(o):
        return (big["a_w_in"][o], small_res[o][0], small_res[o][1], small_res[o][2], small_res[o][3],
                big["a_w_out"][o], big["b_w_in"][o], big["b_w_grp"][o], small_res[o][7], big["b_w_out"][o],
                small_res[o][4], small_res[o][5], big["mlp_w1"][o], big["mlp_w2"][o], small_res[o][6])

    return (loss, grad_x[None], *leaf(0), *leaf(1), *leaf(2), *leaf(3))
```

```python
import math

import jax
import jax.numpy as jnp
from jax import lax
from jax.experimental import pallas as pl
from jax.experimental.pallas import tpu as pltpu

F32 = jnp.float32
BF16 = jnp.bfloat16
MESH = pl.DeviceIdType.MESH

N_DEV = 8
N_CHIPS = 4
CHUNK = 128
A_GROUPS = 8
B_WINDOWS = (2, 4, 8, 16)
B_GROUPS = len(B_WINDOWS)
EPS = 1e-6
ADAM_LR = 0.001
ADAM_B1 = 0.9
ADAM_B2 = 0.999
ADAM_EPS = 1e-08
ADAM_WD = 0.01
ADAM_STEP = 10

VMEM_LIMIT = 48 * 1024 * 1024

NN = (((1,), (0,)), ((), ()))
NT = (((1,), (1,)), ((), ()))
TN = (((0,), (0,)), ((), ()))

_ANY = pl.BlockSpec(memory_space=pl.ANY)
_HBM = pl.BlockSpec(memory_space=pltpu.HBM)
_SEM = pl.BlockSpec(memory_space=pltpu.SEMAPHORE)
_EFFECT = pltpu.SideEffectType.DATAFLOW_SIDE_EFFECTING


def _tile(n, pref):
    return pref if n % pref == 0 else n


def _sds(shape, dtype):
    return jax.ShapeDtypeStruct(shape, dtype)


def _pcall(name, body, operands, in_specs, out_shape, out_specs, *, grid=None, sem=None, scratch=(),
           prefetch=(), after=None, aliases=None):
    after = [] if after is None else [after]
    n_lead = len(prefetch) + len(operands)
    n_after = len(after)

    def wrapped(*refs):
        body(*refs[:n_lead], *refs[n_lead + n_after:])

    in_specs = list(in_specs) + [_ANY] * n_after
    params = pltpu.CompilerParams(vmem_limit_bytes=VMEM_LIMIT) if sem is None else \
        pltpu.CompilerParams(dimension_semantics=sem, vmem_limit_bytes=VMEM_LIMIT)
    kwargs = dict(out_shape=out_shape, scratch_shapes=list(scratch), compiler_params=params, name=name,
                  input_output_aliases=aliases or {})
    if prefetch:
        kwargs["grid_spec"] = pltpu.PrefetchScalarGridSpec(
            num_scalar_prefetch=len(prefetch), grid=grid, in_specs=in_specs, out_specs=out_specs,
            scratch_shapes=list(scratch))
        kwargs.pop("scratch_shapes")
    else:
        kwargs.update(in_specs=in_specs, out_specs=out_specs)
        if grid is not None:
            kwargs["grid"] = grid
    return pl.pallas_call(wrapped, **kwargs)(*prefetch, *operands, *after)


def _matmul(name, a, b, dims, grid, a_spec, b_spec, out_shape, out_specs, acc_shape,
            epilogue, extras=(), extra_specs=(), after=None, prefetch=()):
    nk = grid[2]
    n_extra = len(extras)
    n_out = len(out_shape)
    n_pre = len(prefetch)

    def body(*refs):
        refs = refs[n_pre:]
        a_ref, b_ref = refs[0], refs[1]
        extra_refs = refs[2:2 + n_extra]
        out_refs = refs[2 + n_extra:2 + n_extra + n_out]

        def finish(acc):
            outs = epilogue(acc, *[r[...] for r in extra_refs])
            for o_ref, o in zip(out_refs, outs):
                o_ref[...] = o.astype(o_ref.dtype)

        def product():
            return lax.dot_general(a_ref[...], b_ref[...], dims, preferred_element_type=F32)

        if nk == 1:
            finish(product())
        else:
            acc_ref = refs[-1]
            k = pl.program_id(2)

            @pl.when(k == 0)
            def _():
                acc_ref[...] = product()

            if nk > 2:
                @pl.when(jnp.logical_and(k > 0, k < nk - 1))
                def _():
                    acc_ref[...] += product()

            @pl.when(k == nk - 1)
            def _():
                finish(acc_ref[...] + product())

    scratch = [] if nk == 1 else [pltpu.VMEM(acc_shape, F32)]
    return _pcall(name, body, [a, b, *extras], [a_spec, b_spec, *extra_specs], out_shape, out_specs,
                  grid=grid, sem=("parallel", "parallel", "arbitrary"), scratch=scratch, after=after,
                  prefetch=prefetch)


def _mm_nn(name, a, b, epilogue, out_dtypes, extras=(), extra_kinds=(), slab=False, after=None,
           tm=1024, tn=1024, tk=2048):
    m, kd = a.shape
    if slab:
        n_slab, _, w = b.shape
        n = n_slab * w
        tn = _tile(w, min(tn, w))
        per = w // tn
        tk = _tile(kd, tk)
        b_spec = pl.BlockSpec((None, tk, tn), lambda i, j, k: (j // per, k, j % per))
    else:
        n = b.shape[1]
        tn = _tile(n, tn)
        tk = _tile(kd, tk)
        b_spec = pl.BlockSpec((tk, tn), lambda i, j, k: (k, j))
    tm = _tile(m, tm)
    grid = (m // tm, n // tn, kd // tk)
    a_spec = pl.BlockSpec((tm, tk), lambda i, j, k: (i, k))
    tile_spec = pl.BlockSpec((tm, tn), lambda i, j, k: (i, j))
    row_spec = pl.BlockSpec((1, tn), lambda i, j, k: (0, j))
    extra_specs = [tile_spec if kind == "tile" else row_spec for kind in extra_kinds]
    return _matmul(name, a, b, NN, grid, a_spec, b_spec,
                   [_sds((m, n), d) for d in out_dtypes], [tile_spec for _ in out_dtypes],
                   (tm, tn), epilogue, extras, extra_specs, after=after)


def _mm_nt(name, a, b, epilogue, out_dtypes, extras=(), extra_kinds=(), slab=False, after=None,
           tm=1024, tn=1024, tk=2048):
    m, kd = a.shape
    if slab:
        n_slab, n, w = b.shape
        tk = _tile(w, min(tk, w))
        per = w // tk
        tn = _tile(n, tn)
        b_spec = pl.BlockSpec((None, tn, tk), lambda i, j, k: (k // per, j, k % per))
    else:
        n = b.shape[0]
        tn = _tile(n, tn)
        tk = _tile(kd, tk)
        b_spec = pl.BlockSpec((tn, tk), lambda i, j, k: (j, k))
    tm = _tile(m, tm)
    grid = (m // tm, n // tn, kd // tk)
    a_spec = pl.BlockSpec((tm, tk), lambda i, j, k: (i, k))
    tile_spec = pl.BlockSpec((tm, tn), lambda i, j, k: (i, j))
    row_spec = pl.BlockSpec((1, tn), lambda i, j, k: (0, j))
    extra_specs = [tile_spec if kind == "tile" else row_spec for kind in extra_kinds]
    return _matmul(name, a, b, NT, grid, a_spec, b_spec,
                   [_sds((m, n), d) for d in out_dtypes], [tile_spec for _ in out_dtypes],
                   (tm, tn), epilogue, extras, extra_specs, after=after)


def _mm_tn_half(name, a, b, core, own, by_rows, block, recv=None, after=None, tm=1024, tn=1024, tk=2048):
    s, m = a.shape
    n = b.shape[1]
    tk = _tile(s, tk)

    def owner(chip, core_ref):
        return 2 * chip + (core_ref[0] if own else 1 - core_ref[0])

    if by_rows:
        r, c = block, n
        tm, tn = _tile(r, min(tm, r)), _tile(c, tn)
        per = r // tm
        grid = (N_CHIPS * per, c // tn, s // tk)
        a_spec = pl.BlockSpec((tk, tm), lambda i, j, k, cr: (k, owner(i // per, cr) * per + i % per))
        b_spec = pl.BlockSpec((tk, tn), lambda i, j, k, cr: (k, j))
        o_spec = pl.BlockSpec((None, tm, tn), lambda i, j, k, cr: (i // per, i % per, j))
    else:
        r, c = m, block
        tm, tn = _tile(r, tm), _tile(c, min(tn, c))
        per = c // tn
        grid = (r // tm, N_CHIPS * per, s // tk)
        a_spec = pl.BlockSpec((tk, tm), lambda i, j, k, cr: (k, i))
        b_spec = pl.BlockSpec((tk, tn), lambda i, j, k, cr: (k, owner(j // per, cr) * per + j % per))
        o_spec = pl.BlockSpec((None, tm, tn), lambda i, j, k, cr: (j // per, i, j % per))
    if recv is None:
        extras, epilogue = (), lambda acc: (acc,)
    else:
        extras, epilogue = (recv,), lambda acc, other: (acc + other.astype(F32),)
    return _matmul(name, a, b, TN, grid, a_spec, b_spec, [_sds((N_CHIPS, r, c), BF16)], [o_spec], (tm, tn),
                   epilogue, extras, [o_spec] * len(extras), after=after, prefetch=[core])[0]


def _rms_fwd(name, h, g, after=None):
    s, d = h.shape
    tr = _tile(s, 256)

    def body(h_ref, g_ref, o_ref):
        x = h_ref[...]
        r = lax.rsqrt(jnp.mean(x * x, axis=-1, keepdims=True) + EPS)
        o_ref[...] = (x * r * g_ref[...]).astype(o_ref.dtype)

    row = pl.BlockSpec((tr, d), lambda i: (i, 0))
    vec = pl.BlockSpec((1, d), lambda i: (0, 0))
    return _pcall(name, body, [h, g], [row, vec], _sds((s, d), BF16), row, grid=(s // tr,),
                  sem=("parallel",), after=after)


def _accumulate(ref, part, step):
    @pl.when(step == 0)
    def _():
        ref[...] = part

    @pl.when(step > 0)
    def _():
        ref[...] += part


def _rms_bwd(name, dhn, h, g, dres, after=None):
    s, d = h.shape
    tr = _tile(s, 256)

    def body(dhn_ref, h_ref, g_ref, dres_ref, dh_ref, dhb_ref, gp_ref):
        x = h_ref[...]
        r = lax.rsqrt(jnp.mean(x * x, axis=-1, keepdims=True) + EPS)
        n = x * r
        dy = dhn_ref[...]
        dn = dy * g_ref[...]
        dh = dres_ref[...] + r * (dn - n * jnp.mean(dn * n, axis=-1, keepdims=True))
        dh_ref[...] = dh
        dhb_ref[...] = dh.astype(BF16)
        _accumulate(gp_ref, jnp.sum(dy * n, axis=0, keepdims=True), pl.program_id(0))

    row = pl.BlockSpec((tr, d), lambda i: (i, 0))
    vec = pl.BlockSpec((1, d), lambda i: (0, 0))
    return _pcall(name, body, [dhn, h, g, dres], [row, row, vec, row],
                  [_sds((s, d), F32), _sds((s, d), BF16), _sds((1, d), F32)], [row, row, vec],
                  grid=(s // tr,), sem=("arbitrary",), after=after)


def _loss_head(name, h, g, target, after=None):
    s, d = h.shape
    tr = _tile(s, 256)

    def body(h_ref, g_ref, t_ref, dh_ref, dhb_ref, gp_ref, loss_ref):
        x = h_ref[...]
        gg = g_ref[...]
        r = lax.rsqrt(jnp.mean(x * x, axis=-1, keepdims=True) + EPS)
        n = x * r
        e = n * gg - t_ref[...]
        dy = e * (1.0 / d)
        dn = dy * gg
        dh = r * (dn - n * jnp.mean(dn * n, axis=-1, keepdims=True))
        dh_ref[...] = dh
        dhb_ref[...] = dh.astype(BF16)
        step = pl.program_id(0)
        _accumulate(gp_ref, jnp.sum(dy * n, axis=0, keepdims=True), step)
        row_loss = jnp.mean(e * e, axis=-1, keepdims=True)
        _accumulate(loss_ref, 0.5 * jnp.sum(row_loss, axis=0, keepdims=True), step)

    row = pl.BlockSpec((tr, d), lambda i: (i, 0))
    vec = pl.BlockSpec((1, d), lambda i: (0, 0))
    one = pl.BlockSpec((1, 1), lambda i: (0, 0))
    return _pcall(name, body, [h, g, target], [row, vec, row],
                  [_sds((s, d), F32), _sds((s, d), BF16), _sds((1, d), F32), _sds((1, 1), F32)],
                  [row, row, vec, one], grid=(s // tr,), sem=("arbitrary",), after=after)


_SQRT_HALF = math.sqrt(0.5)
_INV_SQRT_2PI = 1.0 / math.sqrt(2.0 * math.pi)


def _gelu(x):
    return 0.5 * x * (1.0 + lax.erf(x * _SQRT_HALF))


def _gelu_grad(x):
    return 0.5 * (1.0 + lax.erf(x * _SQRT_HALF)) + x * jnp.exp(-0.5 * x * x) * _INV_SQRT_2PI


def _causal_mask():
    row = lax.broadcasted_iota(jnp.int32, (CHUNK, CHUNK), 0)
    col = lax.broadcasted_iota(jnp.int32, (CHUNK, CHUNK), 1)
    return row >= col


def _layernorm_parts(v):
    mu = jnp.mean(v, axis=-1, keepdims=True)
    xc = v - mu
    rstd = lax.rsqrt(jnp.mean(xc * xc, axis=-1, keepdims=True) + EPS)
    return xc * rstd, rstd


def _amix_fwd(name, pre, ln_g, ln_b, w_s, b_s_col, after=None):
    s, w2 = pre.shape
    w = w2 // 2
    head = w // A_GROUPS

    def body(pre_ref, g_ref, b_ref, ws_ref, bs_ref, o_ref):
        u = _gelu(pre_ref[:, :w])
        v = _gelu(pre_ref[:, w:])
        vhat, _ = _layernorm_parts(v)
        vn = (vhat * g_ref[...] + b_ref[...]).astype(BF16)
        mask = _causal_mask()
        for grp in range(A_GROUPS):
            cols = slice(grp * head, (grp + 1) * head)
            wm = jnp.where(mask, ws_ref[grp], 0.0).astype(BF16)
            sg = jnp.dot(wm, vn[:, cols], preferred_element_type=F32) + bs_ref[grp]
            o_ref[:, cols] = (u[:, cols] * sg).astype(o_ref.dtype)

    vec = pl.BlockSpec((1, w), lambda i: (0, 0))
    return _pcall(
        name, body, [pre, ln_g, ln_b, w_s, b_s_col],
        [pl.BlockSpec((CHUNK, w2), lambda i: (i, 0)), vec, vec,
         pl.BlockSpec((A_GROUPS, CHUNK, CHUNK), lambda i: (0, 0, 0)),
         pl.BlockSpec((A_GROUPS, CHUNK, 1), lambda i: (0, 0, 0))],
        _sds((s, w), BF16), pl.BlockSpec((CHUNK, w), lambda i: (i, 0)),
        grid=(s // CHUNK,), sem=("parallel",), after=after)


def _amix_bwd(name, pre, dgated, ln_g, ln_b, w_s, b_s_col, after=None):
    s, w2 = pre.shape
    w = w2 // 2
    head = w // A_GROUPS

    def body(pre_ref, dg_ref, g_ref, b_ref, ws_ref, bs_ref, dpre_ref, glg_ref, glb_ref, gws_ref, gbs_ref):
        step = pl.program_id(0)
        pre_u = pre_ref[:, :w]
        pre_v = pre_ref[:, w:]
        u = _gelu(pre_u)
        v = _gelu(pre_v)
        vhat, rstd = _layernorm_parts(v)
        gain = g_ref[...]
        vn = (vhat * gain + b_ref[...]).astype(BF16)
        dgated = dg_ref[...]
        ds = dgated * u
        dsb = ds.astype(BF16)
        mask = _causal_mask()
        du_parts = []
        dvn_parts = []
        for grp in range(A_GROUPS):
            cols = slice(grp * head, (grp + 1) * head)
            wm = jnp.where(mask, ws_ref[grp], 0.0).astype(BF16)
            sg = jnp.dot(wm, vn[:, cols], preferred_element_type=F32) + bs_ref[grp]
            du_parts.append(dgated[:, cols] * sg)
            gws = lax.dot_general(dsb[:, cols], vn[:, cols], NT, preferred_element_type=F32)
            gws = jnp.where(mask, gws, 0.0)
            gbs = jnp.sum(ds[:, cols], axis=-1, keepdims=True)

            @pl.when(step == 0)
            def _():
                gws_ref[grp] = gws
                gbs_ref[grp] = gbs

            @pl.when(step > 0)
            def _():
                gws_ref[grp] += gws
                gbs_ref[grp] += gbs

            dvn_parts.append(lax.dot_general(wm, dsb[:, cols], TN, preferred_element_type=F32))
        du = jnp.concatenate(du_parts, axis=-1)
        dvn = jnp.concatenate(dvn_parts, axis=-1)
        _accumulate(glg_ref, jnp.sum(dvn * vhat, axis=0, keepdims=True), step)
        _accumulate(glb_ref, jnp.sum(dvn, axis=0, keepdims=True), step)
        dvhat = dvn * gain
        dv = rstd * (dvhat - jnp.mean(dvhat, axis=-1, keepdims=True)
                     - vhat * jnp.mean(dvhat * vhat, axis=-1, keepdims=True))
        dpre_ref[:, :w] = (du * _gelu_grad(pre_u)).astype(dpre_ref.dtype)
        dpre_ref[:, w:] = (dv * _gelu_grad(pre_v)).astype(dpre_ref.dtype)

    vec = pl.BlockSpec((1, w), lambda i: (0, 0))
    ws_spec = pl.BlockSpec((A_GROUPS, CHUNK, CHUNK), lambda i: (0, 0, 0))
    bs_spec = pl.BlockSpec((A_GROUPS, CHUNK, 1), lambda i: (0, 0, 0))
    return _pcall(
        name, body, [pre, dgated, ln_g, ln_b, w_s, b_s_col],
        [pl.BlockSpec((CHUNK, w2), lambda i: (i, 0)), pl.BlockSpec((CHUNK, w), lambda i: (i, 0)),
         vec, vec, ws_spec, bs_spec],
        [_sds((s, w2), BF16), _sds((1, w), F32), _sds((1, w), F32),
         _sds((A_GROUPS, CHUNK, CHUNK), F32), _sds((A_GROUPS, CHUNK, 1), F32)],
        [pl.BlockSpec((CHUNK, w2), lambda i: (i, 0)), vec, vec, ws_spec, bs_spec],
        grid=(s // CHUNK,), sem=("arbitrary",), after=after)


def _shift_rows(x, k, forward):
    n = x.shape[0]
    row = lax.broadcasted_iota(jnp.int32, x.shape, 0)
    if forward:
        return jnp.where(row >= k, pltpu.roll(x, k, 0), 0.0)
    return jnp.where(row < n - k, pltpu.roll(x, n - k, 0), 0.0)


def _window_sum(x, window, forward):
    k = 1
    while k < window:
        x = x + _shift_rows(x, k, forward)
        k *= 2
    return x


def _pool(name, v, backward, after=None):
    s, w = v.shape
    head = w // B_GROUPS
    lane = _tile(head, 128)

    def body(v_ref, o_ref):
        grp = pl.program_id(0)
        x = v_ref[...]
        t = lax.broadcasted_iota(jnp.int32, x.shape, 0)
        for idx, window in enumerate(B_WINDOWS):
            @pl.when(grp == idx)
            def _():
                inv_count = 1.0 / jnp.minimum(t + 1, window).astype(F32)
                if backward:
                    out = _window_sum(x * inv_count, window, False) - x
                else:
                    out = _window_sum(x, window, True) * inv_count - x
                o_ref[...] = out.astype(o_ref.dtype)

    per = head // lane
    spec = pl.BlockSpec((s, lane), lambda g, j: (0, g * per + j))
    return _pcall(name, body, [v], [spec], _sds((s, w), BF16), spec, grid=(B_GROUPS, per),
                  sem=("parallel", "parallel"), after=after)


def _colsum(name, a, after=None):
    s, d = a.shape
    tr = _tile(s, 256)

    def body(a_ref, o_ref):
        _accumulate(o_ref, jnp.sum(a_ref[...], axis=0, keepdims=True), pl.program_id(0))

    return _pcall(name, body, [a], [pl.BlockSpec((tr, d), lambda i: (i, 0))], _sds((1, d), F32),
                  pl.BlockSpec((1, d), lambda i: (0, 0)), grid=(s // tr,), sem=("arbitrary",), after=after)


def _adamw(w, g, m, v):
    m = ADAM_B1 * m + (1.0 - ADAM_B1) * g
    v = ADAM_B2 * v + (1.0 - ADAM_B2) * (g * g)
    m_hat = m / (1.0 - ADAM_B1 ** ADAM_STEP)
    v_hat = v / (1.0 - ADAM_B2 ** ADAM_STEP)
    delta = -ADAM_LR * (m_hat / (jnp.sqrt(v_hat) + ADAM_EPS) + ADAM_WD * w)
    return delta, m, v


def _adam_rows(name, g, w, m, v, after=None):
    r, c = g.shape
    tr = _tile(r, 256)

    def body(g_ref, w_ref, m_ref, v_ref, d_ref, nm_ref, nv_ref):
        d_ref[...], nm_ref[...], nv_ref[...] = _adamw(w_ref[...], g_ref[...], m_ref[...], v_ref[...])

    spec = pl.BlockSpec((tr, c), lambda i: (i, 0))
    return _pcall(name, body, [g, w, m, v], [spec] * 4, [_sds((r, c), F32)] * 3, [spec] * 3,
                  grid=(r // tr,), sem=("parallel",), after=after)


def _position():
    return lax.axis_index("x"), lax.axis_index("y"), lax.axis_index("c")


def _other_chips(x, y):
    return [(1 - x, y), (x, 1 - y), (1 - x, 1 - y)]


def _slot(px, py, pc):
    return 4 * px + 2 * py + pc


def _hbm(a):
    return pltpu.with_memory_space_constraint(a, pltpu.HBM)


def _hop1_copies(srcs, lands, send_sems, recv_sems):
    x, y, c = _position()
    peers = [(x, y, 1 - c), (1 - x, y, c), (x, 1 - y, c)]
    mine = _slot(x, y, c)
    return [[pltpu.make_async_remote_copy(
        src_ref=srcs[t], dst_ref=lands[t].at[mine], send_sem=send_sems[t].at[k], recv_sem=recv_sems[t].at[k],
        device_id=peer, device_id_type=MESH) for k, peer in enumerate(peers)] for t in range(len(srcs))]


def _hop2_copies(lands, send_sems, recv_sems):
    x, y, c = _position()
    routes = [(_slot(1 - x, y, c), (x, 1 - y, c)), (_slot(x, 1 - y, c), (1 - x, y, c))]
    out = []
    for t in range(len(lands)):
        rows = lands[t].shape[1]
        halves = [(0, rows // 2), (rows // 2, rows - rows // 2)]
        per_tensor = []
        for h, ((slot, peer), (start, size)) in enumerate(zip(routes, halves)):
            if size:
                block = lands[t].at[slot, pl.ds(start, size)]
                per_tensor.append(pltpu.make_async_remote_copy(
                    src_ref=block, dst_ref=block, send_sem=send_sems[t].at[h], recv_sem=recv_sems[t].at[h],
                    device_id=peer, device_id_type=MESH))
        for j, (slot, _) in enumerate(routes):
            block = lands[t].at[slot]
            per_tensor.append(pltpu.make_async_remote_copy(
                src_ref=block, dst_ref=block, send_sem=send_sems[t].at[2 + j], recv_sem=recv_sems[t].at[2 + j],
                device_id=(x, y, 1 - c), device_id_type=MESH))
        out.append(per_tensor)
    return out


def _split_start(name, srcs, lands, copies, n_sems, after=None):
    n = len(srcs)
    order = [] if after is None else [after]
    n_in = 2 * n + len(order)

    def body(*refs):
        for per_tensor in copies(refs[:n], refs[n:2 * n], refs[n_in:n_in + n], refs[n_in + n:n_in + 2 * n]):
            for cp in per_tensor:
                cp.start()
        refs[-1][...] = jnp.zeros_like(refs[-1])

    out_shape = ([pltpu.SemaphoreType.DMA((n_sems,)) for _ in range(2 * n)]
                 + [pltpu.HBM(a.shape, a.dtype) for a in list(srcs) + list(lands)]
                 + [_sds((8, 128), F32)])
    out = pl.pallas_call(
        body, name=name, out_shape=out_shape, in_specs=[_HBM] * (2 * n) + [_ANY] * len(order),
        out_specs=[_SEM] * (2 * n) + [_HBM] * (2 * n) + [pl.BlockSpec(memory_space=pltpu.VMEM)],
        input_output_aliases={i: 2 * n + i for i in range(2 * n)},
        compiler_params=pltpu.CompilerParams(has_side_effects=_EFFECT),
    )(*[_hbm(a) for a in srcs], *[_hbm(a) for a in lands], *order)
    return [(out[t], out[n + t], out[2 * n + t], out[3 * n + t]) for t in range(n)], out[-1]


def _split_wait(name, started, copies, after):
    n = len(started)

    def body(*refs):
        for per_tensor in copies(refs[:n], refs[n:2 * n], refs[2 * n:3 * n], refs[3 * n:4 * n]):
            for cp in per_tensor:
                cp.wait_send()
                cp.wait_recv()

    srcs = [e[2] for e in started]
    lands = [e[3] for e in started]
    out = pl.pallas_call(
        body, name=name, out_shape=[pltpu.HBM(a.shape, a.dtype) for a in srcs + lands],
        in_specs=[_HBM] * (2 * n) + [_SEM] * (2 * n) + [_ANY], out_specs=[_HBM] * (2 * n),
        input_output_aliases={i: i for i in range(2 * n)},
        compiler_params=pltpu.CompilerParams(has_side_effects=_EFFECT),
    )(*srcs, *lands, *[e[0] for e in started], *[e[1] for e in started], after)
    return out[:n], out[n:]


def _gather_step(name, arrived, fresh, after=None):
    n, m = len(arrived), len(fresh)
    order = [] if after is None else [after]
    fresh_lands = [lax.empty((N_DEV,) + s.shape, s.dtype) for s in fresh]
    buffers = [e[2] for e in arrived] + [e[3] for e in arrived] + list(fresh) + fresh_lands
    old_sems = [e[0] for e in arrived] + [e[1] for e in arrived]
    n_buf, n_old = len(buffers), len(old_sems)
    first_new = n_buf + n_old + len(order)

    def body(*refs):
        bufs, old = refs[:n_buf], refs[n_buf:n_buf + n_old]
        new = refs[first_new:first_new + 2 * n + 2 * m]
        for per_tensor in _hop1_copies(bufs[:n], bufs[n:2 * n], old[:n], old[n:]):
            for cp in per_tensor:
                cp.wait_send()
                cp.wait_recv()
        second = _hop2_copies(bufs[n:2 * n], new[:n], new[n:2 * n])
        first = _hop1_copies(bufs[2 * n:2 * n + m], bufs[2 * n + m:], new[2 * n:2 * n + m], new[2 * n + m:])
        for per_tensor in second + first:
            for cp in per_tensor:
                cp.start()
        refs[-1][...] = jnp.zeros_like(refs[-1])

    n_new = 2 * n + 2 * m
    out_shape = ([pltpu.SemaphoreType.DMA((4,)) for _ in range(2 * n)]
                 + [pltpu.SemaphoreType.DMA((3,)) for _ in range(2 * m)]
                 + [pltpu.HBM(a.shape, a.dtype) for a in buffers] + [_sds((8, 128), F32)])
    out = pl.pallas_call(
        body, name=name, out_shape=out_shape,
        in_specs=[_HBM] * n_buf + [_SEM] * n_old + [_ANY] * len(order),
        out_specs=[_SEM] * n_new + [_HBM] * n_buf + [pl.BlockSpec(memory_space=pltpu.VMEM)],
        input_output_aliases={i: n_new + i for i in range(n_buf)},
        compiler_params=pltpu.CompilerParams(has_side_effects=_EFFECT),
    )(*[_hbm(a) for a in buffers], *old_sems, *order)
    sems, bufs = out[:n_new], out[n_new:n_new + n_buf]
    second = [(sems[t], sems[n + t], bufs[t], bufs[n + t]) for t in range(n)]
    first = [(sems[2 * n + t], sems[2 * n + m + t], bufs[2 * n + t], bufs[2 * n + m + t]) for t in range(m)]
    return second, first, out[-1]


def _gather_wait(name, second, after):
    return _split_wait(name, second, lambda srcs, lands, send, recv: _hop2_copies(lands, send, recv), after)


def _sibling_copies(srcs, lands, send_sems, recv_sems):
    x, y, c = _position()
    return [[pltpu.make_async_remote_copy(
        src_ref=srcs[t], dst_ref=lands[t], send_sem=send_sems[t].at[0], recv_sem=recv_sems[t].at[0],
        device_id=(x, y, 1 - c), device_id_type=MESH)] for t in range(len(srcs))]


def _sibling_start(name, arrays):
    lands = [lax.empty(a.shape, a.dtype) for a in arrays]
    return _split_start(name, arrays, lands, _sibling_copies, 1)


def _sibling_wait(name, started, after):
    return _split_wait(name, started, _sibling_copies, after)[1]


def _small_copies(srcs, lands, send_sems, recv_sems):
    x, y, c = _position()
    mine = _slot(x, y, c)
    peers = [(x ^ ((k >> 2) & 1), y ^ ((k >> 1) & 1), c ^ (k & 1)) for k in range(1, N_DEV)]
    return [[pltpu.make_async_remote_copy(
        src_ref=srcs[t], dst_ref=lands[t].at[mine], send_sem=send_sems[t].at[k], recv_sem=recv_sems[t].at[k],
        device_id=peer, device_id_type=MESH) for k, peer in enumerate(peers)] for t in range(len(srcs))]


def _gather_finish(name, shards, lands, after):
    n = len(shards)

    def body(*refs):
        srcs, lands_in, outs = refs[:n], refs[n:2 * n], refs[2 * n:3 * n]
        send_sems, recv_sems, local_sems = refs[3 * n:]
        x, y, c = _position()
        local = [pltpu.make_async_copy(srcs[t], outs[t].at[_slot(x, y, c)], local_sems.at[t]) for t in range(n)]

        def diagonal(t, core):
            block = outs[t].at[_slot(1 - x, 1 - y, core)]
            return pltpu.make_async_remote_copy(
                src_ref=block, dst_ref=block, send_sem=send_sems.at[t], recv_sem=recv_sems.at[t],
                device_id=(x, y, 1 - c), device_id_type=MESH)

        for cp in local:
            cp.start()
        for t in range(n):
            diagonal(t, c).start()
        for t in range(n):
            diagonal(t, c).wait_send()
            diagonal(t, 1 - c).wait_recv()
        for cp in local:
            cp.wait()

    return _pcall(name, body, [*shards, *lands], [_ANY] * (2 * n),
                  [_sds(l.shape, l.dtype) for l in lands], [_ANY] * n,
                  scratch=[pltpu.SemaphoreType.DMA((n,)), pltpu.SemaphoreType.DMA((n,)),
                           pltpu.SemaphoreType.DMA((n,))],
                  after=after, aliases={n + t: t for t in range(n)})


def _exchange_sibling(name, fulls, after):
    n = len(fulls)

    def body(*refs):
        src = refs[:n]
        out = refs[n:2 * n]
        send_sems, recv_sems = refs[2 * n:]
        x, y, c = _position()
        copies = [pltpu.make_async_remote_copy(
            src_ref=src[t].at[:, 1 - c], dst_ref=out[t], send_sem=send_sems.at[t], recv_sem=recv_sems.at[t],
            device_id=(x, y, 1 - c), device_id_type=MESH) for t in range(n)]
        for cp in copies:
            cp.start()
        for cp in copies:
            cp.wait()

    return _pcall(name, body, fulls, [_ANY] * n, [_sds((N_CHIPS,) + f.shape[2:], f.dtype) for f in fulls],
                  [_ANY] * n, scratch=[pltpu.SemaphoreType.DMA((n,)), pltpu.SemaphoreType.DMA((n,))],
                  after=after)


def _add_sibling(name, full, recv, core, after):
    _, _, r, c = full.shape
    tr = _tile(r, max(8, (256 * 1024) // c))

    def body(core_ref, f_ref, r_ref, o_ref):
        o_ref[...] = (f_ref[...].astype(F32) + r_ref[...].astype(F32)).astype(o_ref.dtype)

    return _pcall(
        name, body, [full, recv],
        [pl.BlockSpec((None, None, tr, c), lambda p, i, core_ref: (p, core_ref[0], i, 0)),
         pl.BlockSpec((None, tr, c), lambda p, i, core_ref: (p, i, 0))],
        _sds((N_CHIPS, r, c), BF16), pl.BlockSpec((None, tr, c), lambda p, i, core_ref: (p, i, 0)),
        grid=(N_CHIPS, r // tr), sem=("parallel", "parallel"), prefetch=[core], after=after)


def _scatter_copies(srcs, lands, send_sems, recv_sems):
    x, y, c = _position()
    return [[pltpu.make_async_remote_copy(
        src_ref=srcs[t].at[2 * px + py], dst_ref=lands[t].at[j],
        send_sem=send_sems[t].at[j], recv_sem=recv_sems[t].at[j],
        device_id=(px, py, c), device_id_type=MESH) for j, (px, py) in enumerate(_other_chips(x, y))]
        for t in range(len(srcs))]


def _scatter_start(name, partials):
    lands = [lax.empty((N_CHIPS - 1,) + p.shape[1:], p.dtype) for p in partials]
    return _split_start(name, partials, lands, _scatter_copies, 3)


def _scatter_wait(name, started, after):
    return _split_wait(name, started, _scatter_copies, after)


def _reduce_adam(name, partial, recv, chip, w, m, v, layer, carried, after):
    n_layers, r, c = w.shape
    tr = _tile(r, max(8, (256 * 1024) // c))

    def body(chip_ref, p_ref, r_ref, w_ref, m_ref, v_ref, *rest):
        g_ref, d_ref, nm_ref, nv_ref = rest[-4:]
        g = p_ref[...].astype(F32)
        for j in range(N_CHIPS - 1):
            g = g + r_ref[j].astype(F32)
        g_ref[...] = g
        d_ref[...], nm_ref[...], nv_ref[...] = _adamw(w_ref[...], g, m_ref[...], v_ref[...])

    layered = pl.BlockSpec((None, tr, c), lambda i, chip_ref: (layer, i, 0))
    in_specs = [pl.BlockSpec((None, tr, c), lambda i, chip_ref: (chip_ref[0], i, 0)),
                pl.BlockSpec((N_CHIPS - 1, tr, c), lambda i, chip_ref: (0, i, 0)),
                layered, layered, layered]
    operands = [partial, recv, w, m, v]
    aliases = {}
    if carried is not None:
        operands += list(carried)
        in_specs += [_ANY] * 4
        aliases = {1 + 5 + o: o for o in range(4)}
    return _pcall(name, body, operands, in_specs, [_sds((n_layers, r, c), F32)] * 4, [layered] * 4,
                  grid=(r // tr,), sem=("parallel",), prefetch=[chip], after=after, aliases=aliases)


def _small_sum(name, gathered, own, device, after=None):
    r, lanes = own.shape

    def body(dev_ref, g_ref, own_ref, out_ref):
        dev = dev_ref[0]
        mine = own_ref[...]
        total = jnp.where(dev == 0, mine, g_ref[0])
        for d in range(1, N_DEV):
            total = total + jnp.where(dev == d, mine, g_ref[d])
        out_ref[...] = total

    return _pcall(name, body, [gathered, own],
                  [pl.BlockSpec((N_DEV, r, lanes), lambda i, dev_ref: (0, 0, 0)),
                   pl.BlockSpec((r, lanes), lambda i, dev_ref: (0, 0))],
                  _sds((r, lanes), F32), pl.BlockSpec((r, lanes), lambda i, dev_ref: (0, 0)),
                  grid=(1,), sem=("arbitrary",), prefetch=[device], after=after)


def _pack(arrays):
    return jnp.concatenate([a.reshape(-1, 128) for a in arrays], axis=0)


def _unpack(packed, shapes):
    out, row = [], 0
    for shape in shapes:
        rows = math.prod(shape) // 128
        out.append(packed[row:row + rows].reshape(shape))
        row += rows
    return out


class _Order:
    def __init__(self):
        self.last = None

    def __call__(self, fn, *args, **kwargs):
        out = fn(*args, after=self.last, **kwargs)
        self.last = out[0] if isinstance(out, (list, tuple)) else out
        return out


def kernel(x, a_w_in, a_ln_g, a_ln_b, a_w_s, a_b_s, a_w_out, b_w_in, b_w_grp, b_scale, b_w_out, norm_mix, norm_mlp, mlp_w1, mlp_w2, final_norm, loss_target, m_a_w_in, m_a_ln_g, m_a_ln_b, m_a_w_s, m_a_b_s, m_a_w_out, m_b_w_in, m_b_w_grp, m_b_scale, m_b_w_out, m_norm_mix, m_norm_mlp, m_mlp_w1, m_mlp_w2, m_final_norm, v_a_w_in, v_a_ln_g, v_a_ln_b, v_a_w_s, v_a_b_s, v_a_w_out, v_b_w_in, v_b_w_grp, v_b_scale, v_b_w_out, v_norm_mix, v_norm_mlp, v_mlp_w1, v_mlp_w2, v_final_norm):
    s, d = x.shape[1], x.shape[2]
    depth = mlp_w1.shape[0]
    a_slab = a_w_in.shape[2]
    ff_slab = mlp_w1.shape[2]
    ff_rows = mlp_w2.shape[1]
    bh = b_w_grp.shape[3]
    my_x, my_y, my_c = _position()
    core = jnp.reshape(my_c, (1,)).astype(jnp.int32)
    chip = jnp.reshape(2 * my_x + my_y, (1,)).astype(jnp.int32)
    device = _slot(my_x, my_y, my_c)
    run = _Order()

    w1_b, w2_b = mlp_w1.astype(BF16), mlp_w2.astype(BF16)
    shards = [a_w_in[0].astype(BF16), a_w_out[0].astype(BF16), b_scale,
              w1_b[0], w2_b[0],
              b_w_in[0].astype(BF16), b_w_grp[0].astype(BF16), b_w_out[0].astype(BF16),
              w1_b[1], w2_b[1]]
    groups = [[0], [1, 2], [3], [4], [5, 6, 7], [8], [9]]
    hop1, hop2 = {}, {}
    _, hop1[0], token = _gather_step("weights_group0_hop1", [], [shards[t] for t in groups[0]])
    _, hop1[1], token = _gather_step("weights_group1_hop1", [], [shards[t] for t in groups[1]], token)
    run.last = token

    def advance(g):
        if g not in hop1:
            return
        fresh = [shards[t] for t in groups[g + 1]] if g + 1 < len(groups) and g + 1 not in hop1 else []
        hop2[g], started, tok = _gather_step(f"weights_group{g}_hop2", hop1.pop(g), fresh, run.last)
        if fresh:
            hop1[g + 1] = started
        run.last = tok

    def gathered(g):
        advance(g)
        if g == 0:
            advance(1)
        srcs, lands = _gather_wait(f"weights_group{g}_wait", hop2.pop(g), run.last)
        run.last = srcs[0]
        return run(_gather_finish, f"weights_group{g}_finish", srcs, lands)

    h0 = x[0]
    target = loss_target[0]
    ln_g, ln_b = a_ln_g, a_ln_b
    w_s = a_w_s[0]
    b_s_col = a_b_s[0][:, :, None]
    nmix = [norm_mix[l][None, :] for l in range(depth)]
    nmlp = [norm_mlp[l][None, :] for l in range(depth)]

    def mlp_forward(l, h, up_group):
        hn = run(_rms_fwd, f"mlp{l}_norm", h, nmlp[l])
        (w1,) = gathered(up_group)
        act, act_sq = run(_mm_nn, f"mlp{l}_up", hn, w1,
                          lambda acc: (jnp.maximum(acc, 0.0), jnp.square(jnp.maximum(acc, 0.0))),
                          (BF16, BF16), slab=True)
        advance(up_group + 1)
        (w2,) = gathered(up_group + 1)
        advance(up_group + 2)
        w2 = w2.reshape(-1, d)
        (h_out,) = run(_mm_nn, f"mlp{l}_down", act_sq, w2, lambda acc, res: (acc + res,), (F32,),
                       extras=(h,), extra_kinds=("tile",))
        return h_out, (h, hn, act, act_sq, w1, w2)

    scattered = []

    def scatter_partials(name, partials, specs):
        in_flight, tok = _scatter_start(name + "_scatter_start", partials)
        run.last = tok
        scattered.append((name, in_flight, specs))

    def weight_grad(name, a, b, by_rows, block, between):
        other = run(_mm_tn_half, name + "_other", a, b, core, False, by_rows, block)
        sent, tok = _sibling_start(name + "_sibling_start", [other])
        run.last = tok
        middle = between()
        (recv,) = _sibling_wait(name + "_sibling_wait", sent, run.last)
        run.last = recv
        return run(_mm_tn_half, name + "_own", a, b, core, True, by_rows, block, recv=recv), middle

    def mlp_backward(l, saved, dh, dhb):
        h, hn, act, act_sq, w1, w2 = saved
        part_w2, (dpre,) = weight_grad(
            f"mlp{l}_down_dw", act_sq, dhb, True, ff_rows,
            lambda: run(_mm_nt, f"mlp{l}_down_dx", dhb, w2, lambda acc, a: (2.0 * a.astype(F32) * acc,),
                        (BF16,), extras=(act,), extra_kinds=("tile",)))
        scatter_partials(f"mlp{l}_down_grads", [part_w2], [("mlp_w2", l)])
        part_w1, (dhn,) = weight_grad(
            f"mlp{l}_up_dw", hn, dpre, False, ff_slab,
            lambda: run(_mm_nt, f"mlp{l}_up_dx", dpre, w1, lambda acc: (acc,), (F32,), slab=True))
        scatter_partials(f"mlp{l}_up_grads", [part_w1], [("mlp_w1", l)])
        dh, dhb, g_norm = run(_rms_bwd, f"mlp{l}_norm_bwd", dhn, h, nmlp[l], dh)
        return dh, dhb, g_norm

    hn0 = run(_rms_fwd, "mix0_norm", h0, nmix[0])
    (wa_in,) = gathered(0)
    (pre,) = run(_mm_nn, "mixa_in", hn0, wa_in, lambda acc: (acc,), (F32,), slab=True)
    wa_out, scale = gathered(1)
    wa_out, scale = wa_out.reshape(d, d), scale.reshape(1, d)
    gated = run(_amix_fwd, "mixa_gate", pre, ln_g, ln_b, w_s, b_s_col)
    advance(2)
    (h1,) = run(_mm_nn, "mixa_out", gated, wa_out, lambda acc, res: (acc + res,), (F32,),
                extras=(h0,), extra_kinds=("tile",))
    h2, saved_mlp0 = mlp_forward(0, h1, 2)
    hn2 = run(_rms_fwd, "mix1_norm", h2, nmix[1])
    wb_in, wb_grp, wb_out = gathered(4)
    advance(5)
    wb_in, wb_out = wb_in.reshape(d, d), wb_out.reshape(d, d)
    wb_grp = jnp.transpose(wb_grp, (1, 0, 2, 3)).reshape(B_GROUPS, bh, bh)
    (vb,) = run(_mm_nn, "mixb_in", hn2, wb_in, lambda acc: (acc,), (F32,))
    pooled = run(_pool, "mixb_pool", vb, backward=False)
    tm = _tile(s, 1024)
    grp_tile = pl.BlockSpec((tm, bh), lambda i, j, k: (i, j))
    grp_weight = pl.BlockSpec((None, bh, bh), lambda i, j, k: (j, 0, 0))
    mixed, mixed_scaled = run(
        _matmul, "mixb_grp", pooled, wb_grp, NN, (s // tm, B_GROUPS, 1), grp_tile, grp_weight,
        [_sds((s, d), BF16), _sds((s, d), BF16)], [grp_tile] * 2,
        (tm, bh), lambda acc, sc: (acc, acc * sc), (scale,), [pl.BlockSpec((1, bh), lambda i, j, k: (0, j))])
    (h3,) = run(_mm_nn, "mixb_out", mixed_scaled, wb_out, lambda acc, res: (acc + res,), (F32,),
                extras=(h2,), extra_kinds=("tile",))
    h4, saved_mlp1 = mlp_forward(1, h3, 5)
    dh, dhb, g_final, loss_part = run(_loss_head, "loss_head", h4, final_norm[None, :], target)

    dh, dhb, g_nmlp1 = mlp_backward(1, saved_mlp1, dh, dhb)
    tks = _tile(s, 1024)
    grp_rows = pl.BlockSpec((tks, bh), lambda i, j, k: (k, j))

    def mixb_middle():
        dms_scaled, dms_mixed = run(
            _mm_nt, "mixb_out_dx", dhb, wb_out,
            lambda acc, sc, mx: (acc * sc, acc * mx.astype(F32)), (BF16, F32),
            extras=(scale, mixed), extra_kinds=("row", "tile"))
        g_scale = run(_colsum, "mixb_scale_dw", dms_mixed)
        (g_wb_grp,) = run(
            _matmul, "mixb_grp_dw", pooled, dms_scaled, TN, (1, B_GROUPS, s // tks), grp_rows, grp_rows,
            [_sds((B_GROUPS, bh, bh), BF16)], [grp_weight], (bh, bh), lambda acc: (acc,))
        (dpooled,) = run(
            _matmul, "mixb_grp_dx", dms_scaled, wb_grp, NT, (s // tm, B_GROUPS, 1), grp_tile, grp_weight,
            [_sds((s, d), F32)], [grp_tile], (tm, bh), lambda acc: (acc,))
        return g_scale, g_wb_grp, run(_pool, "mixb_pool_bwd", dpooled, backward=True)

    part_wb_out, (g_scale, g_wb_grp, dvb) = weight_grad("mixb_out_dw", mixed_scaled, dhb, True, d // N_DEV,
                                                        mixb_middle)
    part_wb_in, (dhn2,) = weight_grad(
        "mixb_in_dw", hn2, dvb, True, d // N_DEV,
        lambda: run(_mm_nt, "mixb_in_dx", dvb, wb_in, lambda acc: (acc,), (F32,)))
    grp_full = jnp.transpose(g_wb_grp.reshape(B_GROUPS, N_DEV, bh // N_DEV, bh), (1, 0, 2, 3))
    grp_full = grp_full.reshape(N_CHIPS, 2, B_GROUPS * bh // N_DEV, bh)
    (grp_sibling,) = run(_exchange_sibling, "mixb_grp_dw_to_sibling", [grp_full])
    part_wb_grp = run(_add_sibling, "mixb_grp_dw_add_sibling", grp_full, grp_sibling, core)
    scatter_partials("mixb_grads", [part_wb_out, part_wb_grp, part_wb_in],
                     [("b_w_out", 0), ("b_w_grp", 0), ("b_w_in", 0)])
    dh, dhb, g_nmix1 = run(_rms_bwd, "mix1_norm_bwd", dhn2, h2, nmix[1], dh)
    dh, dhb, g_nmlp0 = mlp_backward(0, saved_mlp0, dh, dhb)
    def mixa_middle():
        (dgated,) = run(_mm_nt, "mixa_out_dx", dhb, wa_out, lambda acc: (acc,), (F32,))
        return run(_amix_bwd, "mixa_gate_bwd", pre, dgated, ln_g, ln_b, w_s, b_s_col)

    part_wa_out, (dpre, g_ln_g, g_ln_b, g_w_s, g_b_s) = weight_grad("mixa_out_dw", gated, dhb, True, d // N_DEV,
                                                                     mixa_middle)
    part_wa_in, (dhn0,) = weight_grad(
        "mixa_in_dw", hn0, dpre, False, a_slab,
        lambda: run(_mm_nt, "mixa_in_dx", dpre, wa_in, lambda acc: (acc,), (F32,), slab=True))
    scatter_partials("mixa_grads", [part_wa_in, part_wa_out], [("a_w_in", 0), ("a_w_out", 0)])
    grad_x, _, g_nmix0 = run(_rms_bwd, "mix0_norm_bwd", dhn0, h0, nmix[0], dh)

    g_norm_mix = jnp.concatenate([g_nmix0, g_nmix1], axis=0)
    g_norm_mlp = jnp.concatenate([g_nmlp0, g_nmlp1], axis=0)
    small_parts = [g_ln_g, g_ln_b, g_w_s, g_b_s, g_norm_mix, g_norm_mlp, g_final, g_scale]
    packed = _pack(small_parts)
    small_sent, tok = _split_start("small_grads_start", [packed], [lax.empty((N_DEV,) + packed.shape, F32)],
                                   _small_copies, N_DEV - 1)
    run.last = tok

    weights = {"a_w_in": (a_w_in, m_a_w_in, v_a_w_in), "a_w_out": (a_w_out, m_a_w_out, v_a_w_out),
               "b_w_in": (b_w_in, m_b_w_in, v_b_w_in), "b_w_grp": (b_w_grp, m_b_w_grp, v_b_w_grp),
               "b_w_out": (b_w_out, m_b_w_out, v_b_w_out), "mlp_w1": (mlp_w1, m_mlp_w1, v_mlp_w1),
               "mlp_w2": (mlp_w2, m_mlp_w2, v_mlp_w2)}
    results = {}

    def finish_group(name, in_flight, specs):
        partials, lands = _scatter_wait(name + "_scatter_wait", in_flight, run.last)
        run.last = lands[0]
        for t, (wname, layer) in enumerate(specs):
            w, m, v = weights[wname]
            _, r, c = partials[t].shape
            layers = w.shape[0]
            results[wname] = run(_reduce_adam, f"{name}_reduce_adam_{t}", partials[t], lands[t], chip,
                                 w.reshape(layers, r, c), m.reshape(layers, r, c), v.reshape(layers, r, c),
                                 layer, results.get(wname))

    for group in scattered[:-1]:
        finish_group(*group)
    own_packed, small_gathered = _split_wait("small_grads_wait", small_sent, _small_copies, run.last)
    run.last = small_gathered[0]
    small_sum = run(_small_sum, "small_grads_sum", small_gathered[0], own_packed[0],
                    jnp.reshape(device, (1,)).astype(jnp.int32))
    sg = _unpack(small_sum, [a_ln_g.shape, a_ln_b.shape, a_w_s.shape, a_b_s.shape, norm_mix.shape,
                             norm_mlp.shape, final_norm.shape, (1, d)])
    shard = b_scale.shape[1]
    sg[7] = lax.dynamic_slice(sg[7], (0, device * shard), (1, shard))
    small_w = [a_ln_g, a_ln_b, a_w_s, a_b_s, norm_mix, norm_mlp, final_norm, b_scale]
    small_m = [m_a_ln_g, m_a_ln_b, m_a_w_s, m_a_b_s, m_norm_mix, m_norm_mlp, m_final_norm, m_b_scale]
    small_v = [v_a_ln_g, v_a_ln_b, v_a_w_s, v_a_b_s, v_norm_mix, v_norm_mlp, v_final_norm, v_b_scale]
    small_out = run(_adam_rows, "small_adam", _pack(sg), _pack(small_w), _pack(small_m), _pack(small_v))
    shapes = [w.shape for w in small_w]
    small_res = [sg] + [_unpack(o, shapes) for o in small_out]

    finish_group(*scattered[-1])
    big = {wname: [o.reshape(weights[wname][0].shape) for o in outs] for wname, outs in results.items()}

    loss = lax.psum(loss_part[0, 0], ("x", "y", "c"))

    def leaf(o):
        return (big["a_w_in"][o], small_res[o][0], small_res[o][1], small_res[o][2], small_res[o][3],
                big["a_w_out"][o], big["b_w_in"][o], big["b_w_grp"][o], small_res[o][7], big["b_w_out"][o],
                small_res[o][4], small_res[o][5], big["mlp_w1"][o], big["mlp_w2"][o], small_res[o][6])

    return (loss, grad_x[None], *leaf(0), *leaf(1), *leaf(2), *leaf(3))
```

```python
import math

import jax
import jax.numpy as jnp
from jax import lax
from jax.experimental import pallas as pl
from jax.experimental.pallas import tpu as pltpu

F32 = jnp.float32
BF16 = jnp.bfloat16
MESH = pl.DeviceIdType.MESH

N_DEV = 8
N_CHIPS = 4
CHUNK = 128
A_GROUPS = 8
B_WINDOWS = (2, 4, 8, 16)
B_GROUPS = len(B_WINDOWS)
EPS = 1e-6
ADAM_LR = 0.001
ADAM_B1 = 0.9
ADAM_B2 = 0.999
ADAM_EPS = 1e-08
ADAM_WD = 0.01
ADAM_STEP = 10

VMEM_LIMIT = 48 * 1024 * 1024

NN = (((1,), (0,)), ((), ()))
NT = (((1,), (1,)), ((), ()))
TN = (((0,), (0,)), ((), ()))

_ANY = pl.BlockSpec(memory_space=pl.ANY)
_HBM = pl.BlockSpec(memory_space=pltpu.HBM)
_SEM = pl.BlockSpec(memory_space=pltpu.SEMAPHORE)
_EFFECT = pltpu.SideEffectType.DATAFLOW_SIDE_EFFECTING


def _tile(n, pref):
    return pref if n % pref == 0 else n


def _sds(shape, dtype):
    return jax.ShapeDtypeStruct(shape, dtype)


def _pcall(name, body, operands, in_specs, out_shape, out_specs, *, grid=None, sem=None, scratch=(),
           prefetch=(), after=None, aliases=None):
    after = [] if after is None else [after]
    n_lead = len(prefetch) + len(operands)
    n_after = len(after)

    def wrapped(*refs):
        body(*refs[:n_lead], *refs[n_lead + n_after:])

    in_specs = list(in_specs) + [_ANY] * n_after
    params = pltpu.CompilerParams(vmem_limit_bytes=VMEM_LIMIT) if sem is None else \
        pltpu.CompilerParams(dimension_semantics=sem, vmem_limit_bytes=VMEM_LIMIT)
    kwargs = dict(out_shape=out_shape, scratch_shapes=list(scratch), compiler_params=params, name=name,
                  input_output_aliases=aliases or {})
    if prefetch:
        kwargs["grid_spec"] = pltpu.PrefetchScalarGridSpec(
            num_scalar_prefetch=len(prefetch), grid=grid, in_specs=in_specs, out_specs=out_specs,
            scratch_shapes=list(scratch))
        kwargs.pop("scratch_shapes")
    else:
        kwargs.update(in_specs=in_specs, out_specs=out_specs)
        if grid is not None:
            kwargs["grid"] = grid
    return pl.pallas_call(wrapped, **kwargs)(*prefetch, *operands, *after)


def _matmul(name, a, b, dims, grid, a_spec, b_spec, out_shape, out_specs, acc_shape,
            epilogue, extras=(), extra_specs=(), after=None, prefetch=(), b_parts=1):
    nk = grid[2]
    n_extra = len(extras)
    n_out = len(out_shape)
    n_pre = len(prefetch)

    def body(*refs):
        refs = refs[n_pre:]
        a_ref, b_ref = refs[0], refs[1]
        extra_refs = refs[2:2 + n_extra]
        out_refs = refs[2 + n_extra:2 + n_extra + n_out]

        def finish(acc):
            outs = epilogue(acc, *[r[...] for r in extra_refs])
            for o_ref, o in zip(out_refs, outs):
                o_ref[...] = o.astype(o_ref.dtype)

        def product():
            if b_parts == 1:
                return lax.dot_general(a_ref[...], b_ref[...], dims, preferred_element_type=F32)
            width = b_ref.shape[2]
            total = None
            for p in range(b_parts):
                part = lax.dot_general(a_ref[:, p * width:(p + 1) * width], b_ref[p], dims,
                                       preferred_element_type=F32)
                total = part if total is None else total + part
            return total

        if nk == 1:
            finish(product())
        else:
            acc_ref = refs[-1]
            k = pl.program_id(2)

            @pl.when(k == 0)
            def _():
                acc_ref[...] = product()

            if nk > 2:
                @pl.when(jnp.logical_and(k > 0, k < nk - 1))
                def _():
                    acc_ref[...] += product()

            @pl.when(k == nk - 1)
            def _():
                finish(acc_ref[...] + product())

    scratch = [] if nk == 1 else [pltpu.VMEM(acc_shape, F32)]
    return _pcall(name, body, [a, b, *extras], [a_spec, b_spec, *extra_specs], out_shape, out_specs,
                  grid=grid, sem=("parallel", "parallel", "arbitrary"), scratch=scratch, after=after,
                  prefetch=prefetch)


def _mm_nn(name, a, b, epilogue, out_dtypes, extras=(), extra_kinds=(), slab=False, after=None,
           tm=1024, tn=1024, tk=2048):
    m, kd = a.shape
    if slab:
        n_slab, _, w = b.shape
        n = n_slab * w
        tn = _tile(w, min(tn, w))
        per = w // tn
        tk = _tile(kd, tk)
        b_spec = pl.BlockSpec((None, tk, tn), lambda i, j, k: (j // per, k, j % per))
    else:
        n = b.shape[1]
        tn = _tile(n, tn)
        tk = _tile(kd, tk)
        b_spec = pl.BlockSpec((tk, tn), lambda i, j, k: (k, j))
    tm = _tile(m, tm)
    grid = (m // tm, n // tn, kd // tk)
    a_spec = pl.BlockSpec((tm, tk), lambda i, j, k: (i, k))
    tile_spec = pl.BlockSpec((tm, tn), lambda i, j, k: (i, j))
    row_spec = pl.BlockSpec((1, tn), lambda i, j, k: (0, j))
    extra_specs = [tile_spec if kind == "tile" else row_spec for kind in extra_kinds]
    return _matmul(name, a, b, NN, grid, a_spec, b_spec,
                   [_sds((m, n), d) for d in out_dtypes], [tile_spec for _ in out_dtypes],
                   (tm, tn), epilogue, extras, extra_specs, after=after)


def _mm_nt(name, a, b, epilogue, out_dtypes, extras=(), extra_kinds=(), slab=False, after=None,
           tm=1024, tn=1024, tk=2048):
    m, kd = a.shape
    parts = 1
    if slab:
        n_slab, n, w = b.shape
        tn = _tile(n, tn)
        if tk > w and tk % w == 0 and n_slab % (tk // w) == 0:
            parts = tk // w
            b_spec = pl.BlockSpec((parts, tn, w), lambda i, j, k: (k, j, 0))
        else:
            tk = _tile(w, min(tk, w))
            per = w // tk
            b_spec = pl.BlockSpec((None, tn, tk), lambda i, j, k: (k // per, j, k % per))
    else:
        n = b.shape[0]
        tn = _tile(n, tn)
        tk = _tile(kd, tk)
        b_spec = pl.BlockSpec((tn, tk), lambda i, j, k: (j, k))
    tm = _tile(m, tm)
    grid = (m // tm, n // tn, kd // tk)
    a_spec = pl.BlockSpec((tm, tk), lambda i, j, k: (i, k))
    tile_spec = pl.BlockSpec((tm, tn), lambda i, j, k: (i, j))
    row_spec = pl.BlockSpec((1, tn), lambda i, j, k: (0, j))
    extra_specs = [tile_spec if kind == "tile" else row_spec for kind in extra_kinds]
    return _matmul(name, a, b, NT, grid, a_spec, b_spec,
                   [_sds((m, n), d) for d in out_dtypes], [tile_spec for _ in out_dtypes],
                   (tm, tn), epilogue, extras, extra_specs, after=after, b_parts=parts)


def _mm_tn_half(name, a, b, core, own, by_rows, block, recv=None, after=None, tm=1024, tn=1024, tk=2048):
    s, m = a.shape
    n = b.shape[1]
    tk = _tile(s, tk)

    def owner(chip, core_ref):
        return 2 * chip + (core_ref[0] if own else 1 - core_ref[0])

    if by_rows:
        r, c = block, n
        tm, tn = _tile(r, min(tm, r)), _tile(c, tn)
        per = r // tm
        grid = (N_CHIPS * per, c // tn, s // tk)
        a_spec = pl.BlockSpec((tk, tm), lambda i, j, k, cr: (k, owner(i // per, cr) * per + i % per))
        b_spec = pl.BlockSpec((tk, tn), lambda i, j, k, cr: (k, j))
        o_spec = pl.BlockSpec((None, tm, tn), lambda i, j, k, cr: (i // per, i % per, j))
    else:
        r, c = m, block
        tm, tn = _tile(r, tm), _tile(c, min(tn, c))
        per = c // tn
        grid = (r // tm, N_CHIPS * per, s // tk)
        a_spec = pl.BlockSpec((tk, tm), lambda i, j, k, cr: (k, i))
        b_spec = pl.BlockSpec((tk, tn), lambda i, j, k, cr: (k, owner(j // per, cr) * per + j % per))
        o_spec = pl.BlockSpec((None, tm, tn), lambda i, j, k, cr: (j // per, i, j % per))
    if recv is None:
        extras, epilogue = (), lambda acc: (acc,)
    else:
        extras, epilogue = (recv,), lambda acc, other: (acc + other.astype(F32),)
    return _matmul(name, a, b, TN, grid, a_spec, b_spec, [_sds((N_CHIPS, r, c), BF16)], [o_spec], (tm, tn),
                   epilogue, extras, [o_spec] * len(extras), after=after, prefetch=[core])[0]


def _rms_fwd(name, h, g, after=None):
    s, d = h.shape
    tr = _tile(s, 256)

    def body(h_ref, g_ref, o_ref):
        x = h_ref[...]
        r = lax.rsqrt(jnp.mean(x * x, axis=-1, keepdims=True) + EPS)
        o_ref[...] = (x * r * g_ref[...]).astype(o_ref.dtype)

    row = pl.BlockSpec((tr, d), lambda i: (i, 0))
    vec = pl.BlockSpec((1, d), lambda i: (0, 0))
    return _pcall(name, body, [h, g], [row, vec], _sds((s, d), BF16), row, grid=(s // tr,),
                  sem=("parallel",), after=after)


def _accumulate(ref, part, step):
    @pl.when(step == 0)
    def _():
        ref[...] = part

    @pl.when(step > 0)
    def _():
        ref[...] += part


def _rms_bwd(name, dhn, h, g, dres, after=None):
    s, d = h.shape
    tr = _tile(s, 256)

    def body(dhn_ref, h_ref, g_ref, dres_ref, dh_ref, dhb_ref, gp_ref):
        x = h_ref[...]
        r = lax.rsqrt(jnp.mean(x * x, axis=-1, keepdims=True) + EPS)
        n = x * r
        dy = dhn_ref[...]
        dn = dy * g_ref[...]
        dh = dres_ref[...] + r * (dn - n * jnp.mean(dn * n, axis=-1, keepdims=True))
        dh_ref[...] = dh
        dhb_ref[...] = dh.astype(BF16)
        _accumulate(gp_ref, jnp.sum(dy * n, axis=0, keepdims=True), pl.program_id(0))

    row = pl.BlockSpec((tr, d), lambda i: (i, 0))
    vec = pl.BlockSpec((1, d), lambda i: (0, 0))
    return _pcall(name, body, [dhn, h, g, dres], [row, row, vec, row],
                  [_sds((s, d), F32), _sds((s, d), BF16), _sds((1, d), F32)], [row, row, vec],
                  grid=(s // tr,), sem=("arbitrary",), after=after)


def _loss_head(name, h, g, target, after=None):
    s, d = h.shape
    tr = _tile(s, 256)

    def body(h_ref, g_ref, t_ref, dh_ref, dhb_ref, gp_ref, loss_ref):
        x = h_ref[...]
        gg = g_ref[...]
        r = lax.rsqrt(jnp.mean(x * x, axis=-1, keepdims=True) + EPS)
        n = x * r
        e = n * gg - t_ref[...]
        dy = e * (1.0 / d)
        dn = dy * gg
        dh = r * (dn - n * jnp.mean(dn * n, axis=-1, keepdims=True))
        dh_ref[...] = dh
        dhb_ref[...] = dh.astype(BF16)
        step = pl.program_id(0)
        _accumulate(gp_ref, jnp.sum(dy * n, axis=0, keepdims=True), step)
        row_loss = jnp.mean(e * e, axis=-1, keepdims=True)
        _accumulate(loss_ref, 0.5 * jnp.sum(row_loss, axis=0, keepdims=True), step)

    row = pl.BlockSpec((tr, d), lambda i: (i, 0))
    vec = pl.BlockSpec((1, d), lambda i: (0, 0))
    one = pl.BlockSpec((1, 1), lambda i: (0, 0))
    return _pcall(name, body, [h, g, target], [row, vec, row],
                  [_sds((s, d), F32), _sds((s, d), BF16), _sds((1, d), F32), _sds((1, 1), F32)],
                  [row, row, vec, one], grid=(s // tr,), sem=("arbitrary",), after=after)


_SQRT_HALF = math.sqrt(0.5)
_INV_SQRT_2PI = 1.0 / math.sqrt(2.0 * math.pi)


def _gelu(x):
    return 0.5 * x * (1.0 + lax.erf(x * _SQRT_HALF))


def _gelu_grad(x):
    return 0.5 * (1.0 + lax.erf(x * _SQRT_HALF)) + x * jnp.exp(-0.5 * x * x) * _INV_SQRT_2PI


def _causal_mask():
    row = lax.broadcasted_iota(jnp.int32, (CHUNK, CHUNK), 0)
    col = lax.broadcasted_iota(jnp.int32, (CHUNK, CHUNK), 1)
    return row >= col


def _layernorm_parts(v):
    mu = jnp.mean(v, axis=-1, keepdims=True)
    xc = v - mu
    rstd = lax.rsqrt(jnp.mean(xc * xc, axis=-1, keepdims=True) + EPS)
    return xc * rstd, rstd


def _amix_fwd(name, pre, ln_g, ln_b, w_s, b_s_col, after=None):
    s, w2 = pre.shape
    w = w2 // 2
    head = w // A_GROUPS

    def body(pre_ref, g_ref, b_ref, ws_ref, bs_ref, o_ref):
        u = _gelu(pre_ref[:, :w])
        v = _gelu(pre_ref[:, w:])
        vhat, _ = _layernorm_parts(v)
        vn = (vhat * g_ref[...] + b_ref[...]).astype(BF16)
        mask = _causal_mask()
        for grp in range(A_GROUPS):
            cols = slice(grp * head, (grp + 1) * head)
            wm = jnp.where(mask, ws_ref[grp], 0.0).astype(BF16)
            sg = jnp.dot(wm, vn[:, cols], preferred_element_type=F32) + bs_ref[grp]
            o_ref[:, cols] = (u[:, cols] * sg).astype(o_ref.dtype)

    vec = pl.BlockSpec((1, w), lambda i: (0, 0))
    return _pcall(
        name, body, [pre, ln_g, ln_b, w_s, b_s_col],
        [pl.BlockSpec((CHUNK, w2), lambda i: (i, 0)), vec, vec,
         pl.BlockSpec((A_GROUPS, CHUNK, CHUNK), lambda i: (0, 0, 0)),
         pl.BlockSpec((A_GROUPS, CHUNK, 1), lambda i: (0, 0, 0))],
        _sds((s, w), BF16), pl.BlockSpec((CHUNK, w), lambda i: (i, 0)),
        grid=(s // CHUNK,), sem=("parallel",), after=after)


def _amix_bwd(name, pre, dgated, ln_g, ln_b, w_s, b_s_col, after=None):
    s, w2 = pre.shape
    w = w2 // 2
    head = w // A_GROUPS

    def body(pre_ref, dg_ref, g_ref, b_ref, ws_ref, bs_ref, dpre_ref, glg_ref, glb_ref, gws_ref, gbs_ref):
        step = pl.program_id(0)
        pre_u = pre_ref[:, :w]
        pre_v = pre_ref[:, w:]
        u = _gelu(pre_u)
        v = _gelu(pre_v)
        vhat, rstd = _layernorm_parts(v)
        gain = g_ref[...]
        vn = (vhat * gain + b_ref[...]).astype(BF16)
        dgated = dg_ref[...]
        ds = dgated * u
        dsb = ds.astype(BF16)
        mask = _causal_mask()
        du_parts = []
        dvn_parts = []
        for grp in range(A_GROUPS):
            cols = slice(grp * head, (grp + 1) * head)
            wm = jnp.where(mask, ws_ref[grp], 0.0).astype(BF16)
            sg = jnp.dot(wm, vn[:, cols], preferred_element_type=F32) + bs_ref[grp]
            du_parts.append(dgated[:, cols] * sg)
            gws = lax.dot_general(dsb[:, cols], vn[:, cols], NT, preferred_element_type=F32)
            gws = jnp.where(mask, gws, 0.0)
            gbs = jnp.sum(ds[:, cols], axis=-1, keepdims=True)

            @pl.when(step == 0)
            def _():
                gws_ref[grp] = gws
                gbs_ref[grp] = gbs

            @pl.when(step > 0)
            def _():
                gws_ref[grp] += gws
                gbs_ref[grp] += gbs

            dvn_parts.append(lax.dot_general(wm, dsb[:, cols], TN, preferred_element_type=F32))
        du = jnp.concatenate(du_parts, axis=-1)
        dvn = jnp.concatenate(dvn_parts, axis=-1)
        _accumulate(glg_ref, jnp.sum(dvn * vhat, axis=0, keepdims=True), step)
        _accumulate(glb_ref, jnp.sum(dvn, axis=0, keepdims=True), step)
        dvhat = dvn * gain
        dv = rstd * (dvhat - jnp.mean(dvhat, axis=-1, keepdims=True)
                     - vhat * jnp.mean(dvhat * vhat, axis=-1, keepdims=True))
        dpre_ref[:, :w] = (du * _gelu_grad(pre_u)).astype(dpre_ref.dtype)
        dpre_ref[:, w:] = (dv * _gelu_grad(pre_v)).astype(dpre_ref.dtype)

    vec = pl.BlockSpec((1, w), lambda i: (0, 0))
    ws_spec = pl.BlockSpec((A_GROUPS, CHUNK, CHUNK), lambda i: (0, 0, 0))
    bs_spec = pl.BlockSpec((A_GROUPS, CHUNK, 1), lambda i: (0, 0, 0))
    return _pcall(
        name, body, [pre, dgated, ln_g, ln_b, w_s, b_s_col],
        [pl.BlockSpec((CHUNK, w2), lambda i: (i, 0)), pl.BlockSpec((CHUNK, w), lambda i: (i, 0)),
         vec, vec, ws_spec, bs_spec],
        [_sds((s, w2), BF16), _sds((1, w), F32), _sds((1, w), F32),
         _sds((A_GROUPS, CHUNK, CHUNK), F32), _sds((A_GROUPS, CHUNK, 1), F32)],
        [pl.BlockSpec((CHUNK, w2), lambda i: (i, 0)), vec, vec, ws_spec, bs_spec],
        grid=(s // CHUNK,), sem=("arbitrary",), after=after)


def _shift_rows(x, k, forward):
    n = x.shape[0]
    row = lax.broadcasted_iota(jnp.int32, x.shape, 0)
    if forward:
        return jnp.where(row >= k, pltpu.roll(x, k, 0), 0.0)
    return jnp.where(row < n - k, pltpu.roll(x, n - k, 0), 0.0)


def _window_sum(x, window, forward):
    k = 1
    while k < window:
        x = x + _shift_rows(x, k, forward)
        k *= 2
    return x


def _pool(name, v, backward, after=None):
    s, w = v.shape
    head = w // B_GROUPS
    lane = _tile(head, 128)

    def body(v_ref, o_ref):
        grp = pl.program_id(0)
        x = v_ref[...]
        t = lax.broadcasted_iota(jnp.int32, x.shape, 0)
        for idx, window in enumerate(B_WINDOWS):
            @pl.when(grp == idx)
            def _():
                inv_count = 1.0 / jnp.minimum(t + 1, window).astype(F32)
                if backward:
                    out = _window_sum(x * inv_count, window, False) - x
                else:
                    out = _window_sum(x, window, True) * inv_count - x
                o_ref[...] = out.astype(o_ref.dtype)

    per = head // lane
    spec = pl.BlockSpec((s, lane), lambda g, j: (0, g * per + j))
    return _pcall(name, body, [v], [spec], _sds((s, w), BF16), spec, grid=(B_GROUPS, per),
                  sem=("parallel", "parallel"), after=after)


def _colsum(name, a, after=None):
    s, d = a.shape
    tr = _tile(s, 256)

    def body(a_ref, o_ref):
        _accumulate(o_ref, jnp.sum(a_ref[...], axis=0, keepdims=True), pl.program_id(0))

    return _pcall(name, body, [a], [pl.BlockSpec((tr, d), lambda i: (i, 0))], _sds((1, d), F32),
                  pl.BlockSpec((1, d), lambda i: (0, 0)), grid=(s // tr,), sem=("arbitrary",), after=after)


def _adamw(w, g, m, v):
    m = ADAM_B1 * m + (1.0 - ADAM_B1) * g
    v = ADAM_B2 * v + (1.0 - ADAM_B2) * (g * g)
    m_hat = m / (1.0 - ADAM_B1 ** ADAM_STEP)
    v_hat = v / (1.0 - ADAM_B2 ** ADAM_STEP)
    delta = -ADAM_LR * (m_hat / (jnp.sqrt(v_hat) + ADAM_EPS) + ADAM_WD * w)
    return delta, m, v


def _adam_rows(name, g, w, m, v, after=None):
    r, c = g.shape
    tr = _tile(r, 256)

    def body(g_ref, w_ref, m_ref, v_ref, d_ref, nm_ref, nv_ref):
        d_ref[...], nm_ref[...], nv_ref[...] = _adamw(w_ref[...], g_ref[...], m_ref[...], v_ref[...])

    spec = pl.BlockSpec((tr, c), lambda i: (i, 0))
    return _pcall(name, body, [g, w, m, v], [spec] * 4, [_sds((r, c), F32)] * 3, [spec] * 3,
                  grid=(r // tr,), sem=("parallel",), after=after)


def _position():
    return lax.axis_index("x"), lax.axis_index("y"), lax.axis_index("c")


def _other_chips(x, y):
    return [(1 - x, y), (x, 1 - y), (1 - x, 1 - y)]


def _slot(px, py, pc):
    return 4 * px + 2 * py + pc


def _hbm(a):
    return pltpu.with_memory_space_constraint(a, pltpu.HBM)


def _hop1_copies(srcs, lands, send_sems, recv_sems):
    x, y, c = _position()
    peers = [(x, y, 1 - c), (1 - x, y, c), (x, 1 - y, c)]
    mine = _slot(x, y, c)
    return [[pltpu.make_async_remote_copy(
        src_ref=srcs[t], dst_ref=lands[t].at[mine], send_sem=send_sems[t].at[k], recv_sem=recv_sems[t].at[k],
        device_id=peer, device_id_type=MESH) for k, peer in enumerate(peers)] for t in range(len(srcs))]


def _hop2_copies(lands, send_sems, recv_sems):
    x, y, c = _position()
    routes = [(_slot(1 - x, y, c), (x, 1 - y, c)), (_slot(x, 1 - y, c), (1 - x, y, c))]
    out = []
    for t in range(len(lands)):
        rows = lands[t].shape[1]
        halves = [(0, rows // 2), (rows // 2, rows - rows // 2)]
        per_tensor = []
        for h, ((slot, peer), (start, size)) in enumerate(zip(routes, halves)):
            if size:
                block = lands[t].at[slot, pl.ds(start, size)]
                per_tensor.append(pltpu.make_async_remote_copy(
                    src_ref=block, dst_ref=block, send_sem=send_sems[t].at[h], recv_sem=recv_sems[t].at[h],
                    device_id=peer, device_id_type=MESH))
        for j, (slot, _) in enumerate(routes):
            block = lands[t].at[slot]
            per_tensor.append(pltpu.make_async_remote_copy(
                src_ref=block, dst_ref=block, send_sem=send_sems[t].at[2 + j], recv_sem=recv_sems[t].at[2 + j],
                device_id=(x, y, 1 - c), device_id_type=MESH))
        out.append(per_tensor)
    return out


def _split_start(name, srcs, lands, copies, n_sems, after=None):
    n = len(srcs)
    order = [] if after is None else [after]
    n_in = 2 * n + len(order)

    def body(*refs):
        for per_tensor in copies(refs[:n], refs[n:2 * n], refs[n_in:n_in + n], refs[n_in + n:n_in + 2 * n]):
            for cp in per_tensor:
                cp.start()
        refs[-1][...] = jnp.zeros_like(refs[-1])

    out_shape = ([pltpu.SemaphoreType.DMA((n_sems,)) for _ in range(2 * n)]
                 + [pltpu.HBM(a.shape, a.dtype) for a in list(srcs) + list(lands)]
                 + [_sds((8, 128), F32)])
    out = pl.pallas_call(
        body, name=name, out_shape=out_shape, in_specs=[_HBM] * (2 * n) + [_ANY] * len(order),
        out_specs=[_SEM] * (2 * n) + [_HBM] * (2 * n) + [pl.BlockSpec(memory_space=pltpu.VMEM)],
        input_output_aliases={i: 2 * n + i for i in range(2 * n)},
        compiler_params=pltpu.CompilerParams(has_side_effects=_EFFECT),
    )(*[_hbm(a) for a in srcs], *[_hbm(a) for a in lands], *order)
    return [(out[t], out[n + t], out[2 * n + t], out[3 * n + t]) for t in range(n)], out[-1]


def _split_wait(name, started, copies, after):
    n = len(started)

    def body(*refs):
        for per_tensor in copies(refs[:n], refs[n:2 * n], refs[2 * n:3 * n], refs[3 * n:4 * n]):
            for cp in per_tensor:
                cp.wait_send()
                cp.wait_recv()

    srcs = [e[2] for e in started]
    lands = [e[3] for e in started]
    out = pl.pallas_call(
        body, name=name, out_shape=[pltpu.HBM(a.shape, a.dtype) for a in srcs + lands],
        in_specs=[_HBM] * (2 * n) + [_SEM] * (2 * n) + [_ANY], out_specs=[_HBM] * (2 * n),
        input_output_aliases={i: i for i in range(2 * n)},
        compiler_params=pltpu.CompilerParams(has_side_effects=_EFFECT),
    )(*srcs, *lands, *[e[0] for e in started], *[e[1] for e in started], after)
    return out[:n], out[n:]


def _gather_step(name, arrived, fresh, after=None):
    n, m = len(arrived), len(fresh)
    order = [] if after is None else [after]
    fresh_lands = [lax.empty((N_DEV,) + s.shape, s.dtype) for s in fresh]
    buffers = [e[2] for e in arrived] + [e[3] for e in arrived] + list(fresh) + fresh_lands
    old_sems = [e[0] for e in arrived] + [e[1] for e in arrived]
    n_buf, n_old = len(buffers), len(old_sems)
    first_new = n_buf + n_old + len(order)

    def body(*refs):
        bufs, old = refs[:n_buf], refs[n_buf:n_buf + n_old]
        new = refs[first_new:first_new + 2 * n + 2 * m]
        for per_tensor in _hop1_copies(bufs[:n], bufs[n:2 * n], old[:n], old[n:]):
            for cp in per_tensor:
                cp.wait_send()
                cp.wait_recv()
        second = _hop2_copies(bufs[n:2 * n], new[:n], new[n:2 * n])
        first = _hop1_copies(bufs[2 * n:2 * n + m], bufs[2 * n + m:], new[2 * n:2 * n + m], new[2 * n + m:])
        for per_tensor in second + first:
            for cp in per_tensor:
                cp.start()
        refs[-1][...] = jnp.zeros_like(refs[-1])

    n_new = 2 * n + 2 * m
    out_shape = ([pltpu.SemaphoreType.DMA((4,)) for _ in range(2 * n)]
                 + [pltpu.SemaphoreType.DMA((3,)) for _ in range(2 * m)]
                 + [pltpu.HBM(a.shape, a.dtype) for a in buffers] + [_sds((8, 128), F32)])
    out = pl.pallas_call(
        body, name=name, out_shape=out_shape,
        in_specs=[_HBM] * n_buf + [_SEM] * n_old + [_ANY] * len(order),
        out_specs=[_SEM] * n_new + [_HBM] * n_buf + [pl.BlockSpec(memory_space=pltpu.VMEM)],
        input_output_aliases={i: n_new + i for i in range(n_buf)},
        compiler_params=pltpu.CompilerParams(has_side_effects=_EFFECT),
    )(*[_hbm(a) for a in buffers], *old_sems, *order)
    sems, bufs = out[:n_new], out[n_new:n_new + n_buf]
    second = [(sems[t], sems[n + t], bufs[t], bufs[n + t]) for t in range(n)]
    first = [(sems[2 * n + t], sems[2 * n + m + t], bufs[2 * n + t], bufs[2 * n + m + t]) for t in range(m)]
    return second, first, out[-1]


def _gather_wait(name, second, after):
    return _split_wait(name, second, lambda srcs, lands, send, recv: _hop2_copies(lands, send, recv), after)


def _sibling_copies(srcs, lands, send_sems, recv_sems):
    x, y, c = _position()
    return [[pltpu.make_async_remote_copy(
        src_ref=srcs[t], dst_ref=lands[t], send_sem=send_sems[t].at[0], recv_sem=recv_sems[t].at[0],
        device_id=(x, y, 1 - c), device_id_type=MESH)] for t in range(len(srcs))]


def _sibling_start(name, arrays):
    lands = [lax.empty(a.shape, a.dtype) for a in arrays]
    return _split_start(name, arrays, lands, _sibling_copies, 1)


def _sibling_wait(name, started, after):
    return _split_wait(name, started, _sibling_copies, after)[1]


def _small_copies(srcs, lands, send_sems, recv_sems):
    x, y, c = _position()
    mine = _slot(x, y, c)
    peers = [(x ^ ((k >> 2) & 1), y ^ ((k >> 1) & 1), c ^ (k & 1)) for k in range(1, N_DEV)]
    return [[pltpu.make_async_remote_copy(
        src_ref=srcs[t], dst_ref=lands[t].at[mine], send_sem=send_sems[t].at[k], recv_sem=recv_sems[t].at[k],
        device_id=peer, device_id_type=MESH) for k, peer in enumerate(peers)] for t in range(len(srcs))]


def _gather_finish(name, shards, lands, after):
    n = len(shards)

    def body(*refs):
        srcs, lands_in, outs = refs[:n], refs[n:2 * n], refs[2 * n:3 * n]
        send_sems, recv_sems, local_sems = refs[3 * n:]
        x, y, c = _position()
        local = [pltpu.make_async_copy(srcs[t], outs[t].at[_slot(x, y, c)], local_sems.at[t]) for t in range(n)]

        def diagonal(t, core):
            block = outs[t].at[_slot(1 - x, 1 - y, core)]
            return pltpu.make_async_remote_copy(
                src_ref=block, dst_ref=block, send_sem=send_sems.at[t], recv_sem=recv_sems.at[t],
                device_id=(x, y, 1 - c), device_id_type=MESH)

        for cp in local:
            cp.start()
        for t in range(n):
            diagonal(t, c).start()
        for t in range(n):
            diagonal(t, c).wait_send()
            diagonal(t, 1 - c).wait_recv()
        for cp in local:
            cp.wait()

    return _pcall(name, body, [*shards, *lands], [_ANY] * (2 * n),
                  [_sds(l.shape, l.dtype) for l in lands], [_ANY] * n,
                  scratch=[pltpu.SemaphoreType.DMA((n,)), pltpu.SemaphoreType.DMA((n,)),
                           pltpu.SemaphoreType.DMA((n,))],
                  after=after, aliases={n + t: t for t in range(n)})


def _exchange_sibling(name, fulls, after):
    n = len(fulls)

    def body(*refs):
        src = refs[:n]
        out = refs[n:2 * n]
        send_sems, recv_sems = refs[2 * n:]
        x, y, c = _position()
        copies = [pltpu.make_async_remote_copy(
            src_ref=src[t].at[:, 1 - c], dst_ref=out[t], send_sem=send_sems.at[t], recv_sem=recv_sems.at[t],
            device_id=(x, y, 1 - c), device_id_type=MESH) for t in range(n)]
        for cp in copies:
            cp.start()
        for cp in copies:
            cp.wait()

    return _pcall(name, body, fulls, [_ANY] * n, [_sds((N_CHIPS,) + f.shape[2:], f.dtype) for f in fulls],
                  [_ANY] * n, scratch=[pltpu.SemaphoreType.DMA((n,)), pltpu.SemaphoreType.DMA((n,))],
                  after=after)


def _add_sibling(name, full, recv, core, after):
    _, _, r, c = full.shape
    tr = _tile(r, max(8, (256 * 1024) // c))

    def body(core_ref, f_ref, r_ref, o_ref):
        o_ref[...] = (f_ref[...].astype(F32) + r_ref[...].astype(F32)).astype(o_ref.dtype)

    return _pcall(
        name, body, [full, recv],
        [pl.BlockSpec((None, None, tr, c), lambda p, i, core_ref: (p, core_ref[0], i, 0)),
         pl.BlockSpec((None, tr, c), lambda p, i, core_ref: (p, i, 0))],
        _sds((N_CHIPS, r, c), BF16), pl.BlockSpec((None, tr, c), lambda p, i, core_ref: (p, i, 0)),
        grid=(N_CHIPS, r // tr), sem=("parallel", "parallel"), prefetch=[core], after=after)


def _scatter_copies(srcs, lands, send_sems, recv_sems):
    x, y, c = _position()
    return [[pltpu.make_async_remote_copy(
        src_ref=srcs[t].at[2 * px + py], dst_ref=lands[t].at[j],
        send_sem=send_sems[t].at[j], recv_sem=recv_sems[t].at[j],
        device_id=(px, py, c), device_id_type=MESH) for j, (px, py) in enumerate(_other_chips(x, y))]
        for t in range(len(srcs))]


def _scatter_start(name, partials):
    lands = [lax.empty((N_CHIPS - 1,) + p.shape[1:], p.dtype) for p in partials]
    return _split_start(name, partials, lands, _scatter_copies, 3)


def _scatter_wait(name, started, after):
    return _split_wait(name, started, _scatter_copies, after)


def _reduce_adam(name, partial, recv, chip, w, m, v, layer, carried, after):
    n_layers, r, c = w.shape
    tr = _tile(r, max(8, (256 * 1024) // c))

    def body(chip_ref, p_ref, r_ref, w_ref, m_ref, v_ref, *rest):
        g_ref, d_ref, nm_ref, nv_ref = rest[-4:]
        g = p_ref[...].astype(F32)
        for j in range(N_CHIPS - 1):
            g = g + r_ref[j].astype(F32)
        g_ref[...] = g
        d_ref[...], nm_ref[...], nv_ref[...] = _adamw(w_ref[...], g, m_ref[...], v_ref[...])

    layered = pl.BlockSpec((None, tr, c), lambda i, chip_ref: (layer, i, 0))
    in_specs = [pl.BlockSpec((None, tr, c), lambda i, chip_ref: (chip_ref[0], i, 0)),
                pl.BlockSpec((N_CHIPS - 1, tr, c), lambda i, chip_ref: (0, i, 0)),
                layered, layered, layered]
    operands = [partial, recv, w, m, v]
    aliases = {}
    if carried is not None:
        operands += list(carried)
        in_specs += [_ANY] * 4
        aliases = {1 + 5 + o: o for o in range(4)}
    return _pcall(name, body, operands, in_specs, [_sds((n_layers, r, c), F32)] * 4, [layered] * 4,
                  grid=(r // tr,), sem=("parallel",), prefetch=[chip], after=after, aliases=aliases)


def _small_sum(name, gathered, own, device, after=None):
    r, lanes = own.shape

    def body(dev_ref, g_ref, own_ref, out_ref):
        dev = dev_ref[0]
        mine = own_ref[...]
        total = jnp.where(dev == 0, mine, g_ref[0])
        for d in range(1, N_DEV):
            total = total + jnp.where(dev == d, mine, g_ref[d])
        out_ref[...] = total

    return _pcall(name, body, [gathered, own],
                  [pl.BlockSpec((N_DEV, r, lanes), lambda i, dev_ref: (0, 0, 0)),
                   pl.BlockSpec((r, lanes), lambda i, dev_ref: (0, 0))],
                  _sds((r, lanes), F32), pl.BlockSpec((r, lanes), lambda i, dev_ref: (0, 0)),
                  grid=(1,), sem=("arbitrary",), prefetch=[device], after=after)


def _pack(arrays):
    return jnp.concatenate([a.reshape(-1, 128) for a in arrays], axis=0)


def _unpack(packed, shapes):
    out, row = [], 0
    for shape in shapes:
        rows = math.prod(shape) // 128
        out.append(packed[row:row + rows].reshape(shape))
        row += rows
    return out


class _Order:
    def __init__(self):
        self.last = None

    def __call__(self, fn, *args, **kwargs):
        out = fn(*args, after=self.last, **kwargs)
        self.last = out[0] if isinstance(out, (list, tuple)) else out
        return out


def kernel(x, a_w_in, a_ln_g, a_ln_b, a_w_s, a_b_s, a_w_out, b_w_in, b_w_grp, b_scale, b_w_out, norm_mix, norm_mlp, mlp_w1, mlp_w2, final_norm, loss_target, m_a_w_in, m_a_ln_g, m_a_ln_b, m_a_w_s, m_a_b_s, m_a_w_out, m_b_w_in, m_b_w_grp, m_b_scale, m_b_w_out, m_norm_mix, m_norm_mlp, m_mlp_w1, m_mlp_w2, m_final_norm, v_a_w_in, v_a_ln_g, v_a_ln_b, v_a_w_s, v_a_b_s, v_a_w_out, v_b_w_in, v_b_w_grp, v_b_scale, v_b_w_out, v_norm_mix, v_norm_mlp, v_mlp_w1, v_mlp_w2, v_final_norm):
    s, d = x.shape[1], x.shape[2]
    depth = mlp_w1.shape[0]
    a_slab = a_w_in.shape[2]
    ff_slab = mlp_w1.shape[2]
    ff_rows = mlp_w2.shape[1]
    bh = b_w_grp.shape[3]
    my_x, my_y, my_c = _position()
    core = jnp.reshape(my_c, (1,)).astype(jnp.int32)
    chip = jnp.reshape(2 * my_x + my_y, (1,)).astype(jnp.int32)
    device = _slot(my_x, my_y, my_c)
    run = _Order()

    w1_b, w2_b = mlp_w1.astype(BF16), mlp_w2.astype(BF16)
    shards = [a_w_in[0].astype(BF16), a_w_out[0].astype(BF16), b_scale,
              w1_b[0], w2_b[0],
              b_w_in[0].astype(BF16), b_w_grp[0].astype(BF16), b_w_out[0].astype(BF16),
              w1_b[1], w2_b[1]]
    groups = [[0], [1, 2], [3], [4], [5, 6, 7], [8], [9]]
    hop1, hop2 = {}, {}
    _, hop1[0], token = _gather_step("weights_group0_hop1", [], [shards[t] for t in groups[0]])
    _, hop1[1], token = _gather_step("weights_group1_hop1", [], [shards[t] for t in groups[1]], token)
    run.last = token

    def advance(g):
        if g not in hop1:
            return
        fresh = [shards[t] for t in groups[g + 1]] if g + 1 < len(groups) and g + 1 not in hop1 else []
        hop2[g], started, tok = _gather_step(f"weights_group{g}_hop2", hop1.pop(g), fresh, run.last)
        if fresh:
            hop1[g + 1] = started
        run.last = tok

    def gathered(g):
        advance(g)
        if g == 0:
            advance(1)
        srcs, lands = _gather_wait(f"weights_group{g}_wait", hop2.pop(g), run.last)
        run.last = srcs[0]
        return run(_gather_finish, f"weights_group{g}_finish", srcs, lands)

    h0 = x[0]
    target = loss_target[0]
    ln_g, ln_b = a_ln_g, a_ln_b
    w_s = a_w_s[0]
    b_s_col = a_b_s[0][:, :, None]
    nmix = [norm_mix[l][None, :] for l in range(depth)]
    nmlp = [norm_mlp[l][None, :] for l in range(depth)]

    def mlp_forward(l, h, up_group):
        hn = run(_rms_fwd, f"mlp{l}_norm", h, nmlp[l])
        (w1,) = gathered(up_group)
        advance(up_group + 1)
        act, act_sq = run(_mm_nn, f"mlp{l}_up", hn, w1,
                          lambda acc: (jnp.maximum(acc, 0.0), jnp.square(jnp.maximum(acc, 0.0))),
                          (BF16, BF16), slab=True)
        (w2,) = gathered(up_group + 1)
        advance(up_group + 2)
        w2 = w2.reshape(-1, d)
        (h_out,) = run(_mm_nn, f"mlp{l}_down", act_sq, w2, lambda acc, res: (acc + res,), (F32,),
                       extras=(h,), extra_kinds=("tile",))
        return h_out, (h, hn, act, act_sq, w1, w2)

    scattered = []

    def scatter_partials(name, partials, specs):
        in_flight, tok = _scatter_start(name + "_scatter_start", partials)
        run.last = tok
        scattered.append((name, in_flight, specs))

    def weight_grad(name, a, b, by_rows, block, between):
        other = run(_mm_tn_half, name + "_other", a, b, core, False, by_rows, block)
        sent, tok = _sibling_start(name + "_sibling_start", [other])
        run.last = tok
        middle = between()
        (recv,) = _sibling_wait(name + "_sibling_wait", sent, run.last)
        run.last = recv
        return run(_mm_tn_half, name + "_own", a, b, core, True, by_rows, block, recv=recv), middle

    def mlp_backward(l, saved, dh, dhb):
        h, hn, act, act_sq, w1, w2 = saved
        part_w2, (dpre,) = weight_grad(
            f"mlp{l}_down_dw", act_sq, dhb, True, ff_rows,
            lambda: run(_mm_nt, f"mlp{l}_down_dx", dhb, w2, lambda acc, a: (2.0 * a.astype(F32) * acc,),
                        (BF16,), extras=(act,), extra_kinds=("tile",)))
        scatter_partials(f"mlp{l}_down_grads", [part_w2], [("mlp_w2", l)])
        part_w1, (dhn,) = weight_grad(
            f"mlp{l}_up_dw", hn, dpre, False, ff_slab,
            lambda: run(_mm_nt, f"mlp{l}_up_dx", dpre, w1, lambda acc: (acc,), (F32,), slab=True))
        scatter_partials(f"mlp{l}_up_grads", [part_w1], [("mlp_w1", l)])
        dh, dhb, g_norm = run(_rms_bwd, f"mlp{l}_norm_bwd", dhn, h, nmlp[l], dh)
        return dh, dhb, g_norm

    hn0 = run(_rms_fwd, "mix0_norm", h0, nmix[0])
    (wa_in,) = gathered(0)
    (pre,) = run(_mm_nn, "mixa_in", hn0, wa_in, lambda acc: (acc,), (F32,), slab=True)
    wa_out, scale = gathered(1)
    wa_out, scale = wa_out.reshape(d, d), scale.reshape(1, d)
    gated = run(_amix_fwd, "mixa_gate", pre, ln_g, ln_b, w_s, b_s_col)
    advance(2)
    (h1,) = run(_mm_nn, "mixa_out", gated, wa_out, lambda acc, res: (acc + res,), (F32,),
                extras=(h0,), extra_kinds=("tile",))
    h2, saved_mlp0 = mlp_forward(0, h1, 2)
    hn2 = run(_rms_fwd, "mix1_norm", h2, nmix[1])
    wb_in, wb_grp, wb_out = gathered(4)
    advance(5)
    wb_in, wb_out = wb_in.reshape(d, d), wb_out.reshape(d, d)
    wb_grp = jnp.transpose(wb_grp, (1, 0, 2, 3)).reshape(B_GROUPS, bh, bh)
    (vb,) = run(_mm_nn, "mixb_in", hn2, wb_in, lambda acc: (acc,), (F32,))
    pooled = run(_pool, "mixb_pool", vb, backward=False)
    tm = _tile(s, 1024)
    grp_tile = pl.BlockSpec((tm, bh), lambda i, j, k: (i, j))
    grp_weight = pl.BlockSpec((None, bh, bh), lambda i, j, k: (j, 0, 0))
    mixed, mixed_scaled = run(
        _matmul, "mixb_grp", pooled, wb_grp, NN, (s // tm, B_GROUPS, 1), grp_tile, grp_weight,
        [_sds((s, d), BF16), _sds((s, d), BF16)], [grp_tile] * 2,
        (tm, bh), lambda acc, sc: (acc, acc * sc), (scale,), [pl.BlockSpec((1, bh), lambda i, j, k: (0, j))])
    (h3,) = run(_mm_nn, "mixb_out", mixed_scaled, wb_out, lambda acc, res: (acc + res,), (F32,),
                extras=(h2,), extra_kinds=("tile",))
    h4, saved_mlp1 = mlp_forward(1, h3, 5)
    dh, dhb, g_final, loss_part = run(_loss_head, "loss_head", h4, final_norm[None, :], target)

    dh, dhb, g_nmlp1 = mlp_backward(1, saved_mlp1, dh, dhb)
    tks = _tile(s, 1024)
    grp_rows = pl.BlockSpec((tks, bh), lambda i, j, k: (k, j))

    def mixb_middle():
        dms_scaled, dms_mixed = run(
            _mm_nt, "mixb_out_dx", dhb, wb_out,
            lambda acc, sc, mx: (acc * sc, acc * mx.astype(F32)), (BF16, F32),
            extras=(scale, mixed), extra_kinds=("row", "tile"))
        g_scale = run(_colsum, "mixb_scale_dw", dms_mixed)
        (g_wb_grp,) = run(
            _matmul, "mixb_grp_dw", pooled, dms_scaled, TN, (1, B_GROUPS, s // tks), grp_rows, grp_rows,
            [_sds((B_GROUPS, bh, bh), BF16)], [grp_weight], (bh, bh), lambda acc: (acc,))
        (dpooled,) = run(
            _matmul, "mixb_grp_dx", dms_scaled, wb_grp, NT, (s // tm, B_GROUPS, 1), grp_tile, grp_weight,
            [_sds((s, d), F32)], [grp_tile], (tm, bh), lambda acc: (acc,))
        return g_scale, g_wb_grp, run(_pool, "mixb_pool_bwd", dpooled, backward=True)

    part_wb_out, (g_scale, g_wb_grp, dvb) = weight_grad("mixb_out_dw", mixed_scaled, dhb, True, d // N_DEV,
                                                        mixb_middle)
    part_wb_in, (dhn2,) = weight_grad(
        "mixb_in_dw", hn2, dvb, True, d // N_DEV,
        lambda: run(_mm_nt, "mixb_in_dx", dvb, wb_in, lambda acc: (acc,), (F32,)))
    grp_full = jnp.transpose(g_wb_grp.reshape(B_GROUPS, N_DEV, bh // N_DEV, bh), (1, 0, 2, 3))
    grp_full = grp_full.reshape(N_CHIPS, 2, B_GROUPS * bh // N_DEV, bh)
    (grp_sibling,) = run(_exchange_sibling, "mixb_grp_dw_to_sibling", [grp_full])
    part_wb_grp = run(_add_sibling, "mixb_grp_dw_add_sibling", grp_full, grp_sibling, core)
    scatter_partials("mixb_grads", [part_wb_out, part_wb_grp, part_wb_in],
                     [("b_w_out", 0), ("b_w_grp", 0), ("b_w_in", 0)])
    dh, dhb, g_nmix1 = run(_rms_bwd, "mix1_norm_bwd", dhn2, h2, nmix[1], dh)
    dh, dhb, g_nmlp0 = mlp_backward(0, saved_mlp0, dh, dhb)
    def mixa_middle():
        (dgated,) = run(_mm_nt, "mixa_out_dx", dhb, wa_out, lambda acc: (acc,), (F32,))
        return run(_amix_bwd, "mixa_gate_bwd", pre, dgated, ln_g, ln_b, w_s, b_s_col)

    part_wa_out, (dpre, g_ln_g, g_ln_b, g_w_s, g_b_s) = weight_grad("mixa_out_dw", gated, dhb, True, d // N_DEV,
                                                                     mixa_middle)
    part_wa_in, (dhn0,) = weight_grad(
        "mixa_in_dw", hn0, dpre, False, a_slab,
        lambda: run(_mm_nt, "mixa_in_dx", dpre, wa_in, lambda acc: (acc,), (F32,), slab=True))
    scatter_partials("mixa_grads", [part_wa_in, part_wa_out], [("a_w_in", 0), ("a_w_out", 0)])
    grad_x, _, g_nmix0 = run(_rms_bwd, "mix0_norm_bwd", dhn0, h0, nmix[0], dh)

    g_norm_mix = jnp.concatenate([g_nmix0, g_nmix1], axis=0)
    g_norm_mlp = jnp.concatenate([g_nmlp0, g_nmlp1], axis=0)
    small_parts = [g_ln_g, g_ln_b, g_w_s, g_b_s, g_norm_mix, g_norm_mlp, g_final, g_scale]
    packed = _pack(small_parts)
    small_sent, tok = _split_start("small_grads_start", [packed], [lax.empty((N_DEV,) + packed.shape, F32)],
                                   _small_copies, N_DEV - 1)
    run.last = tok

    weights = {"a_w_in": (a_w_in, m_a_w_in, v_a_w_in), "a_w_out": (a_w_out, m_a_w_out, v_a_w_out),
               "b_w_in": (b_w_in, m_b_w_in, v_b_w_in), "b_w_grp": (b_w_grp, m_b_w_grp, v_b_w_grp),
               "b_w_out": (b_w_out, m_b_w_out, v_b_w_out), "mlp_w1": (mlp_w1, m_mlp_w1, v_mlp_w1),
               "mlp_w2": (mlp_w2, m_mlp_w2, v_mlp_w2)}
    results = {}

    def finish_group(name, in_flight, specs):
        partials, lands = _scatter_wait(name + "_scatter_wait", in_flight, run.last)
        run.last = lands[0]
        for t, (wname, layer) in enumerate(specs):
            w, m, v = weights[wname]
            _, r, c = partials[t].shape
            layers = w.shape[0]
            results[wname] = run(_reduce_adam, f"{name}_reduce_adam_{t}", partials[t], lands[t], chip,
                                 w.reshape(layers, r, c), m.reshape(layers, r, c), v.reshape(layers, r, c),
                                 layer, results.get(wname))

    for group in scattered[:-1]:
        finish_group(*group)
    own_packed, small_gathered = _split_wait("small_grads_wait", small_sent, _small_copies, run.last)
    run.last = small_gathered[0]
    small_sum = run(_small_sum, "small_grads_sum", small_gathered[0], own_packed[0],
                    jnp.reshape(device, (1,)).astype(jnp.int32))
    sg = _unpack(small_sum, [a_ln_g.shape, a_ln_b.shape, a_w_s.shape, a_b_s.shape, norm_mix.shape,
                             norm_mlp.shape, final_norm.shape, (1, d)])
    shard = b_scale.shape[1]
    sg[7] = lax.dynamic_slice(sg[7], (0, device * shard), (1, shard))
    small_w = [a_ln_g, a_ln_b, a_w_s, a_b_s, norm_mix, norm_mlp, final_norm, b_scale]
    small_m = [m_a_ln_g, m_a_ln_b, m_a_w_s, m_a_b_s, m_norm_mix, m_norm_mlp, m_final_norm, m_b_scale]
    small_v = [v_a_ln_g, v_a_ln_b, v_a_w_s, v_a_b_s, v_norm_mix, v_norm_mlp, v_final_norm, v_b_scale]
    small_out = run(_adam_rows, "small_adam", _pack(sg), _pack(small_w), _pack(small_m), _pack(small_v))
    shapes = [w.shape for w in small_w]
    small_res = [sg] + [_unpack(o, shapes) for o in small_out]

    finish_group(*scattered[-1])
    big = {wname: [o.reshape(weights[wname][0].shape) for o in outs] for wname, outs in results.items()}

    loss = lax.psum(loss_part[0, 0], ("x", "y", "c"))

    def leaf(o):
        return (big["a_w_in"][o], small_res[o][0], small_res[o][1], small_res[o][2], small_res[o][3],
                big["a_w_out"][o], big["b_w_in"][o], big["b_w_grp"][o], small_res[o][7], big["b_w_out"][o],
                small_res[o][4], small_res[o][5], big["mlp_w1"][o], big["mlp_w2"][o], small_res[o][6])

    return (loss, grad_x[None], *leaf(0), *leaf(1), *leaf(2), *leaf(3))
```

```python
import math

import jax
import jax.numpy as jnp
from jax import lax
from jax.experimental import pallas as pl
from jax.experimental.pallas import tpu as pltpu

F32 = jnp.float32
BF16 = jnp.bfloat16
MESH = pl.DeviceIdType.MESH

N_DEV = 8
N_CHIPS = 4
CHUNK = 128
A_GROUPS = 8
B_WINDOWS = (2, 4, 8, 16)
B_GROUPS = len(B_WINDOWS)
EPS = 1e-6
ADAM_LR = 0.001
ADAM_B1 = 0.9
ADAM_B2 = 0.999
ADAM_EPS = 1e-08
ADAM_WD = 0.01
ADAM_STEP = 10

VMEM_LIMIT = 48 * 1024 * 1024

NN = (((1,), (0,)), ((), ()))
NT = (((1,), (1,)), ((), ()))
TN = (((0,), (0,)), ((), ()))

_ANY = pl.BlockSpec(memory_space=pl.ANY)
_HBM = pl.BlockSpec(memory_space=pltpu.HBM)
_SEM = pl.BlockSpec(memory_space=pltpu.SEMAPHORE)
_EFFECT = pltpu.SideEffectType.DATAFLOW_SIDE_EFFECTING


def _tile(n, pref):
    return pref if n % pref == 0 else n


def _sds(shape, dtype):
    return jax.ShapeDtypeStruct(shape, dtype)


def _pcall(name, body, operands, in_specs, out_shape, out_specs, *, grid=None, sem=None, scratch=(),
           prefetch=(), after=None, aliases=None):
    after = [] if after is None else [after]
    n_lead = len(prefetch) + len(operands)
    n_after = len(after)

    def wrapped(*refs):
        body(*refs[:n_lead], *refs[n_lead + n_after:])

    in_specs = list(in_specs) + [_ANY] * n_after
    params = pltpu.CompilerParams(vmem_limit_bytes=VMEM_LIMIT) if sem is None else \
        pltpu.CompilerParams(dimension_semantics=sem, vmem_limit_bytes=VMEM_LIMIT)
    kwargs = dict(out_shape=out_shape, scratch_shapes=list(scratch), compiler_params=params, name=name,
                  input_output_aliases=aliases or {})
    if prefetch:
        kwargs["grid_spec"] = pltpu.PrefetchScalarGridSpec(
            num_scalar_prefetch=len(prefetch), grid=grid, in_specs=in_specs, out_specs=out_specs,
            scratch_shapes=list(scratch))
        kwargs.pop("scratch_shapes")
    else:
        kwargs.update(in_specs=in_specs, out_specs=out_specs)
        if grid is not None:
            kwargs["grid"] = grid
    return pl.pallas_call(wrapped, **kwargs)(*prefetch, *operands, *after)


def _matmul(name, a, b, dims, grid, a_spec, b_spec, out_shape, out_specs, acc_shape,
            epilogue, extras=(), extra_specs=(), after=None, prefetch=(), b_parts=1):
    nk = grid[2]
    n_extra = len(extras)
    n_out = len(out_shape)
    n_pre = len(prefetch)

    def body(*refs):
        refs = refs[n_pre:]
        a_ref, b_ref = refs[0], refs[1]
        extra_refs = refs[2:2 + n_extra]
        out_refs = refs[2 + n_extra:2 + n_extra + n_out]

        def finish(acc):
            outs = epilogue(acc, *[r[...] for r in extra_refs])
            for o_ref, o in zip(out_refs, outs):
                o_ref[...] = o.astype(o_ref.dtype)

        def product():
            if b_parts == 1:
                return lax.dot_general(a_ref[...], b_ref[...], dims, preferred_element_type=F32)
            width = b_ref.shape[2]
            total = None
            for p in range(b_parts):
                part = lax.dot_general(a_ref[:, p * width:(p + 1) * width], b_ref[p], dims,
                                       preferred_element_type=F32)
                total = part if total is None else total + part
            return total

        if nk == 1:
            finish(product())
        else:
            acc_ref = refs[-1]
            k = pl.program_id(2)

            @pl.when(k == 0)
            def _():
                acc_ref[...] = product()

            if nk > 2:
                @pl.when(jnp.logical_and(k > 0, k < nk - 1))
                def _():
                    acc_ref[...] += product()

            @pl.when(k == nk - 1)
            def _():
                finish(acc_ref[...] + product())

    scratch = [] if nk == 1 else [pltpu.VMEM(acc_shape, F32)]
    return _pcall(name, body, [a, b, *extras], [a_spec, b_spec, *extra_specs], out_shape, out_specs,
                  grid=grid, sem=("parallel", "parallel", "arbitrary"), scratch=scratch, after=after,
                  prefetch=prefetch)


def _mm_nn(name, a, b, epilogue, out_dtypes, extras=(), extra_kinds=(), slab=False, after=None,
           tm=1024, tn=1024, tk=2048):
    m, kd = a.shape
    if slab:
        n_slab, _, w = b.shape
        n = n_slab * w
        tn = _tile(w, min(tn, w))
        per = w // tn
        tk = _tile(kd, tk)
        b_spec = pl.BlockSpec((None, tk, tn), lambda i, j, k: (j // per, k, j % per))
    else:
        n = b.shape[1]
        tn = _tile(n, tn)
        tk = _tile(kd, tk)
        b_spec = pl.BlockSpec((tk, tn), lambda i, j, k: (k, j))
    tm = _tile(m, tm)
    grid = (m // tm, n // tn, kd // tk)
    a_spec = pl.BlockSpec((tm, tk), lambda i, j, k: (i, k))
    tile_spec = pl.BlockSpec((tm, tn), lambda i, j, k: (i, j))
    row_spec = pl.BlockSpec((1, tn), lambda i, j, k: (0, j))
    extra_specs = [tile_spec if kind == "tile" else row_spec for kind in extra_kinds]
    return _matmul(name, a, b, NN, grid, a_spec, b_spec,
                   [_sds((m, n), d) for d in out_dtypes], [tile_spec for _ in out_dtypes],
                   (tm, tn), epilogue, extras, extra_specs, after=after)


def _mm_nt(name, a, b, epilogue, out_dtypes, extras=(), extra_kinds=(), slab=False, after=None,
           tm=1024, tn=1024, tk=2048):
    m, kd = a.shape
    parts = 1
    if slab:
        n_slab, n, w = b.shape
        tn = _tile(n, tn)
        if tk > w and tk % w == 0 and n_slab % (tk // w) == 0:
            parts = tk // w
            b_spec = pl.BlockSpec((parts, tn, w), lambda i, j, k: (k, j, 0))
        else:
            tk = _tile(w, min(tk, w))
            per = w // tk
            b_spec = pl.BlockSpec((None, tn, tk), lambda i, j, k: (k // per, j, k % per))
    else:
        n = b.shape[0]
        tn = _tile(n, tn)
        tk = _tile(kd, tk)
        b_spec = pl.BlockSpec((tn, tk), lambda i, j, k: (j, k))
    tm = _tile(m, tm)
    grid = (m // tm, n // tn, kd // tk)
    a_spec = pl.BlockSpec((tm, tk), lambda i, j, k: (i, k))
    tile_spec = pl.BlockSpec((tm, tn), lambda i, j, k: (i, j))
    row_spec = pl.BlockSpec((1, tn), lambda i, j, k: (0, j))
    extra_specs = [tile_spec if kind == "tile" else row_spec for kind in extra_kinds]
    return _matmul(name, a, b, NT, grid, a_spec, b_spec,
                   [_sds((m, n), d) for d in out_dtypes], [tile_spec for _ in out_dtypes],
                   (tm, tn), epilogue, extras, extra_specs, after=after, b_parts=parts)


def _mm_tn_half(name, a, b, core, own, by_rows, block, recv=None, after=None, tm=1024, tn=1024, tk=2048):
    s, m = a.shape
    n = b.shape[1]
    tk = _tile(s, tk)

    def owner(chip, core_ref):
        return 2 * chip + (core_ref[0] if own else 1 - core_ref[0])

    if by_rows:
        r, c = block, n
        tm, tn = _tile(r, min(tm, r)), _tile(c, tn)
        per = r // tm
        grid = (N_CHIPS * per, c // tn, s // tk)
        a_spec = pl.BlockSpec((tk, tm), lambda i, j, k, cr: (k, owner(i // per, cr) * per + i % per))
        b_spec = pl.BlockSpec((tk, tn), lambda i, j, k, cr: (k, j))
        o_spec = pl.BlockSpec((None, tm, tn), lambda i, j, k, cr: (i // per, i % per, j))
    else:
        r, c = m, block
        tm, tn = _tile(r, tm), _tile(c, min(tn, c))
        per = c // tn
        grid = (r // tm, N_CHIPS * per, s // tk)
        a_spec = pl.BlockSpec((tk, tm), lambda i, j, k, cr: (k, i))
        b_spec = pl.BlockSpec((tk, tn), lambda i, j, k, cr: (k, owner(j // per, cr) * per + j % per))
        o_spec = pl.BlockSpec((None, tm, tn), lambda i, j, k, cr: (j // per, i, j % per))
    if recv is None:
        extras, epilogue = (), lambda acc: (acc,)
    else:
        extras, epilogue = (recv,), lambda acc, other: (acc + other.astype(F32),)
    return _matmul(name, a, b, TN, grid, a_spec, b_spec, [_sds((N_CHIPS, r, c), BF16)], [o_spec], (tm, tn),
                   epilogue, extras, [o_spec] * len(extras), after=after, prefetch=[core])[0]


def _rms_fwd(name, h, g, after=None):
    s, d = h.shape
    tr = _tile(s, 256)

    def body(h_ref, g_ref, o_ref):
        x = h_ref[...]
        r = lax.rsqrt(jnp.mean(x * x, axis=-1, keepdims=True) + EPS)
        o_ref[...] = (x * r * g_ref[...]).astype(o_ref.dtype)

    row = pl.BlockSpec((tr, d), lambda i: (i, 0))
    vec = pl.BlockSpec((1, d), lambda i: (0, 0))
    return _pcall(name, body, [h, g], [row, vec], _sds((s, d), BF16), row, grid=(s // tr,),
                  sem=("parallel",), after=after)


def _accumulate(ref, part, step):
    @pl.when(step == 0)
    def _():
        ref[...] = part

    @pl.when(step > 0)
    def _():
        ref[...] += part


def _rms_bwd(name, dhn, h, g, dres, after=None):
    s, d = h.shape
    tr = _tile(s, 256)

    def body(dhn_ref, h_ref, g_ref, dres_ref, dh_ref, dhb_ref, gp_ref):
        x = h_ref[...]
        r = lax.rsqrt(jnp.mean(x * x, axis=-1, keepdims=True) + EPS)
        n = x * r
        dy = dhn_ref[...]
        dn = dy * g_ref[...]
        dh = dres_ref[...] + r * (dn - n * jnp.mean(dn * n, axis=-1, keepdims=True))
        dh_ref[...] = dh
        dhb_ref[...] = dh.astype(BF16)
        _accumulate(gp_ref, jnp.sum(dy * n, axis=0, keepdims=True), pl.program_id(0))

    row = pl.BlockSpec((tr, d), lambda i: (i, 0))
    vec = pl.BlockSpec((1, d), lambda i: (0, 0))
    return _pcall(name, body, [dhn, h, g, dres], [row, row, vec, row],
                  [_sds((s, d), F32), _sds((s, d), BF16), _sds((1, d), F32)], [row, row, vec],
                  grid=(s // tr,), sem=("arbitrary",), after=after)


def _loss_head(name, h, g, target, after=None):
    s, d = h.shape
    tr = _tile(s, 256)

    def body(h_ref, g_ref, t_ref, dh_ref, dhb_ref, gp_ref, loss_ref):
        x = h_ref[...]
        gg = g_ref[...]
        r = lax.rsqrt(jnp.mean(x * x, axis=-1, keepdims=True) + EPS)
        n = x * r
        e = n * gg - t_ref[...]
        dy = e * (1.0 / d)
        dn = dy * gg
        dh = r * (dn - n * jnp.mean(dn * n, axis=-1, keepdims=True))
        dh_ref[...] = dh
        dhb_ref[...] = dh.astype(BF16)
        step = pl.program_id(0)
        _accumulate(gp_ref, jnp.sum(dy * n, axis=0, keepdims=True), step)
        row_loss = jnp.mean(e * e, axis=-1, keepdims=True)
        _accumulate(loss_ref, 0.5 * jnp.sum(row_loss, axis=0, keepdims=True), step)

    row = pl.BlockSpec((tr, d), lambda i: (i, 0))
    vec = pl.BlockSpec((1, d), lambda i: (0, 0))
    one = pl.BlockSpec((1, 1), lambda i: (0, 0))
    return _pcall(name, body, [h, g, target], [row, vec, row],
                  [_sds((s, d), F32), _sds((s, d), BF16), _sds((1, d), F32), _sds((1, 1), F32)],
                  [row, row, vec, one], grid=(s // tr,), sem=("arbitrary",), after=after)


_SQRT_HALF = math.sqrt(0.5)
_INV_SQRT_2PI = 1.0 / math.sqrt(2.0 * math.pi)


def _gelu(x):
    return 0.5 * x * (1.0 + lax.erf(x * _SQRT_HALF))


def _gelu_grad(x):
    return 0.5 * (1.0 + lax.erf(x * _SQRT_HALF)) + x * jnp.exp(-0.5 * x * x) * _INV_SQRT_2PI


def _causal_mask():
    row = lax.broadcasted_iota(jnp.int32, (CHUNK, CHUNK), 0)
    col = lax.broadcasted_iota(jnp.int32, (CHUNK, CHUNK), 1)
    return row >= col


def _row_sum(x):
    return jnp.sum(x, axis=-1, keepdims=True)


def _masked_spatial(ws_ref, grp):
    return jnp.where(_causal_mask(), ws_ref[grp], 0.0).astype(BF16)


def _layernorm_stats(pre_ref, v_scr, w, head):
    total = jnp.zeros((CHUNK, 1), F32)
    for grp in range(A_GROUPS):
        v = _gelu(pre_ref[:, w + grp * head:w + (grp + 1) * head])
        v_scr[:, grp * head:(grp + 1) * head] = v
        total = total + _row_sum(v)
    mu = total * (1.0 / w)
    square = jnp.zeros((CHUNK, 1), F32)
    for grp in range(A_GROUPS):
        xc = v_scr[:, grp * head:(grp + 1) * head] - mu
        square = square + _row_sum(xc * xc)
    return mu, lax.rsqrt(square * (1.0 / w) + EPS)


def _amix_fwd(name, pre, ln_g, ln_b, w_s, b_s_col, after=None):
    s, w2 = pre.shape
    w = w2 // 2
    head = w // A_GROUPS

    def body(pre_ref, g_ref, b_ref, ws_ref, bs_ref, o_ref, v_scr):
        mu, rstd = _layernorm_stats(pre_ref, v_scr, w, head)
        for grp in range(A_GROUPS):
            cols = slice(grp * head, (grp + 1) * head)
            vhat = (v_scr[:, cols] - mu) * rstd
            vn = (vhat * g_ref[:, cols] + b_ref[:, cols]).astype(BF16)
            sg = jnp.dot(_masked_spatial(ws_ref, grp), vn, preferred_element_type=F32) + bs_ref[grp]
            o_ref[:, cols] = (_gelu(pre_ref[:, cols]) * sg).astype(o_ref.dtype)

    vec = pl.BlockSpec((1, w), lambda i: (0, 0))
    return _pcall(
        name, body, [pre, ln_g, ln_b, w_s, b_s_col],
        [pl.BlockSpec((CHUNK, w2), lambda i: (i, 0)), vec, vec,
         pl.BlockSpec((A_GROUPS, CHUNK, CHUNK), lambda i: (0, 0, 0)),
         pl.BlockSpec((A_GROUPS, CHUNK, 1), lambda i: (0, 0, 0))],
        _sds((s, w), BF16), pl.BlockSpec((CHUNK, w), lambda i: (i, 0)),
        grid=(s // CHUNK,), sem=("parallel",), scratch=[pltpu.VMEM((CHUNK, w), F32)], after=after)


def _amix_bwd(name, pre, dgated, ln_g, ln_b, w_s, b_s_col, after=None):
    s, w2 = pre.shape
    w = w2 // 2
    head = w // A_GROUPS

    def body(pre_ref, dg_ref, g_ref, b_ref, ws_ref, bs_ref, dpre_ref, glg_ref, glb_ref, gws_ref, gbs_ref,
             v_scr, dvn_scr):
        @pl.when(pl.program_id(0) == 0)
        def _():
            for ref in (glg_ref, glb_ref, gws_ref, gbs_ref):
                ref[...] = jnp.zeros_like(ref)

        mu, rstd = _layernorm_stats(pre_ref, v_scr, w, head)
        mask = _causal_mask()
        sum_dvhat = jnp.zeros((CHUNK, 1), F32)
        sum_dvhat_vhat = jnp.zeros((CHUNK, 1), F32)
        for grp in range(A_GROUPS):
            cols = slice(grp * head, (grp + 1) * head)
            vhat = (v_scr[:, cols] - mu) * rstd
            gain = g_ref[:, cols]
            vn = (vhat * gain + b_ref[:, cols]).astype(BF16)
            wm = _masked_spatial(ws_ref, grp)
            pre_u = pre_ref[:, cols]
            dgated = dg_ref[:, cols]
            ds = dgated * _gelu(pre_u)
            dsb = ds.astype(BF16)
            sg = jnp.dot(wm, vn, preferred_element_type=F32) + bs_ref[grp]
            dpre_ref[:, cols] = (dgated * sg * _gelu_grad(pre_u)).astype(dpre_ref.dtype)
            gws = lax.dot_general(dsb, vn, NT, preferred_element_type=F32)
            gws_ref[grp] += jnp.where(mask, gws, 0.0)
            gbs_ref[grp] += _row_sum(ds)
            dvn = lax.dot_general(wm, dsb, TN, preferred_element_type=F32)
            dvn_scr[:, cols] = dvn
            glg_ref[:, cols] += jnp.sum(dvn * vhat, axis=0, keepdims=True)
            glb_ref[:, cols] += jnp.sum(dvn, axis=0, keepdims=True)
            dvhat = dvn * gain
            sum_dvhat = sum_dvhat + _row_sum(dvhat)
            sum_dvhat_vhat = sum_dvhat_vhat + _row_sum(dvhat * vhat)
        mean_dvhat = sum_dvhat * (1.0 / w)
        mean_dvhat_vhat = sum_dvhat_vhat * (1.0 / w)
        for grp in range(A_GROUPS):
            cols = slice(grp * head, (grp + 1) * head)
            vhat = (v_scr[:, cols] - mu) * rstd
            dvhat = dvn_scr[:, cols] * g_ref[:, cols]
            dv = rstd * (dvhat - mean_dvhat - vhat * mean_dvhat_vhat)
            pre_v = pre_ref[:, w + grp * head:w + (grp + 1) * head]
            dpre_ref[:, w + grp * head:w + (grp + 1) * head] = (dv * _gelu_grad(pre_v)).astype(dpre_ref.dtype)

    vec = pl.BlockSpec((1, w), lambda i: (0, 0))
    ws_spec = pl.BlockSpec((A_GROUPS, CHUNK, CHUNK), lambda i: (0, 0, 0))
    bs_spec = pl.BlockSpec((A_GROUPS, CHUNK, 1), lambda i: (0, 0, 0))
    return _pcall(
        name, body, [pre, dgated, ln_g, ln_b, w_s, b_s_col],
        [pl.BlockSpec((CHUNK, w2), lambda i: (i, 0)), pl.BlockSpec((CHUNK, w), lambda i: (i, 0)),
         vec, vec, ws_spec, bs_spec],
        [_sds((s, w2), BF16), _sds((1, w), F32), _sds((1, w), F32),
         _sds((A_GROUPS, CHUNK, CHUNK), F32), _sds((A_GROUPS, CHUNK, 1), F32)],
        [pl.BlockSpec((CHUNK, w2), lambda i: (i, 0)), vec, vec, ws_spec, bs_spec],
        grid=(s // CHUNK,), sem=("arbitrary",),
        scratch=[pltpu.VMEM((CHUNK, w), F32), pltpu.VMEM((CHUNK, w), F32)], after=after)


def _shift_rows(x, k, forward):
    n = x.shape[0]
    row = lax.broadcasted_iota(jnp.int32, x.shape, 0)
    if forward:
        return jnp.where(row >= k, pltpu.roll(x, k, 0), 0.0)
    return jnp.where(row < n - k, pltpu.roll(x, n - k, 0), 0.0)


def _window_sum(x, window, forward):
    k = 1
    while k < window:
        x = x + _shift_rows(x, k, forward)
        k *= 2
    return x


def _pool(name, v, backward, after=None):
    s, w = v.shape
    head = w // B_GROUPS
    lane = _tile(head, 128)

    def body(v_ref, o_ref):
        grp = pl.program_id(0)
        x = v_ref[...]
        t = lax.broadcasted_iota(jnp.int32, x.shape, 0)
        for idx, window in enumerate(B_WINDOWS):
            @pl.when(grp == idx)
            def _():
                inv_count = 1.0 / jnp.minimum(t + 1, window).astype(F32)
                if backward:
                    out = _window_sum(x * inv_count, window, False) - x
                else:
                    out = _window_sum(x, window, True) * inv_count - x
                o_ref[...] = out.astype(o_ref.dtype)

    per = head // lane
    spec = pl.BlockSpec((s, lane), lambda g, j: (0, g * per + j))
    return _pcall(name, body, [v], [spec], _sds((s, w), BF16), spec, grid=(B_GROUPS, per),
                  sem=("parallel", "parallel"), after=after)


def _colsum(name, a, after=None):
    s, d = a.shape
    tr = _tile(s, 256)

    def body(a_ref, o_ref):
        _accumulate(o_ref, jnp.sum(a_ref[...], axis=0, keepdims=True), pl.program_id(0))

    return _pcall(name, body, [a], [pl.BlockSpec((tr, d), lambda i: (i, 0))], _sds((1, d), F32),
                  pl.BlockSpec((1, d), lambda i: (0, 0)), grid=(s // tr,), sem=("arbitrary",), after=after)


def _adamw(w, g, m, v):
    m = ADAM_B1 * m + (1.0 - ADAM_B1) * g
    v = ADAM_B2 * v + (1.0 - ADAM_B2) * (g * g)
    m_hat = m / (1.0 - ADAM_B1 ** ADAM_STEP)
    v_hat = v / (1.0 - ADAM_B2 ** ADAM_STEP)
    delta = -ADAM_LR * (m_hat / (jnp.sqrt(v_hat) + ADAM_EPS) + ADAM_WD * w)
    return delta, m, v


def _adam_rows(name, g, w, m, v, after=None):
    r, c = g.shape
    tr = _tile(r, 256)

    def body(g_ref, w_ref, m_ref, v_ref, d_ref, nm_ref, nv_ref):
        d_ref[...], nm_ref[...], nv_ref[...] = _adamw(w_ref[...], g_ref[...], m_ref[...], v_ref[...])

    spec = pl.BlockSpec((tr, c), lambda i: (i, 0))
    return _pcall(name, body, [g, w, m, v], [spec] * 4, [_sds((r, c), F32)] * 3, [spec] * 3,
                  grid=(r // tr,), sem=("parallel",), after=after)


def _position():
    return lax.axis_index("x"), lax.axis_index("y"), lax.axis_index("c")


def _other_chips(x, y):
    return [(1 - x, y), (x, 1 - y), (1 - x, 1 - y)]


def _slot(px, py, pc):
    return 4 * px + 2 * py + pc


def _hbm(a):
    return pltpu.with_memory_space_constraint(a, pltpu.HBM)


def _hop1_copies(srcs, lands, send_sems, recv_sems):
    x, y, c = _position()
    peers = [(x, y, 1 - c), (1 - x, y, c), (x, 1 - y, c)]
    mine = _slot(x, y, c)
    return [[pltpu.make_async_remote_copy(
        src_ref=srcs[t], dst_ref=lands[t].at[mine], send_sem=send_sems[t].at[k], recv_sem=recv_sems[t].at[k],
        device_id=peer, device_id_type=MESH) for k, peer in enumerate(peers)] for t in range(len(srcs))]


def _hop2_copies(lands, send_sems, recv_sems):
    x, y, c = _position()
    routes = [(_slot(1 - x, y, c), (x, 1 - y, c)), (_slot(x, 1 - y, c), (1 - x, y, c))]
    out = []
    for t in range(len(lands)):
        rows = lands[t].shape[1]
        halves = [(0, rows // 2), (rows // 2, rows - rows // 2)]
        per_tensor = []
        for h, ((slot, peer), (start, size)) in enumerate(zip(routes, halves)):
            if size:
                block = lands[t].at[slot, pl.ds(start, size)]
                per_tensor.append(pltpu.make_async_remote_copy(
                    src_ref=block, dst_ref=block, send_sem=send_sems[t].at[h], recv_sem=recv_sems[t].at[h],
                    device_id=peer, device_id_type=MESH))
        for j, (slot, _) in enumerate(routes):
            block = lands[t].at[slot]
            per_tensor.append(pltpu.make_async_remote_copy(
                src_ref=block, dst_ref=block, send_sem=send_sems[t].at[2 + j], recv_sem=recv_sems[t].at[2 + j],
                device_id=(x, y, 1 - c), device_id_type=MESH))
        out.append(per_tensor)
    return out


def _split_start(name, srcs, lands, copies, n_sems, after=None):
    n = len(srcs)
    order = [] if after is None else [after]
    n_in = 2 * n + len(order)

    def body(*refs):
        for per_tensor in copies(refs[:n], refs[n:2 * n], refs[n_in:n_in + n], refs[n_in + n:n_in + 2 * n]):
            for cp in per_tensor:
                cp.start()
        refs[-1][...] = jnp.zeros_like(refs[-1])

    out_shape = ([pltpu.SemaphoreType.DMA((n_sems,)) for _ in range(2 * n)]
                 + [pltpu.HBM(a.shape, a.dtype) for a in list(srcs) + list(lands)]
                 + [_sds((8, 128), F32)])
    out = pl.pallas_call(
        body, name=name, out_shape=out_shape, in_specs=[_HBM] * (2 * n) + [_ANY] * len(order),
        out_specs=[_SEM] * (2 * n) + [_HBM] * (2 * n) + [pl.BlockSpec(memory_space=pltpu.VMEM)],
        input_output_aliases={i: 2 * n + i for i in range(2 * n)},
        compiler_params=pltpu.CompilerParams(has_side_effects=_EFFECT),
    )(*[_hbm(a) for a in srcs], *[_hbm(a) for a in lands], *order)
    return [(out[t], out[n + t], out[2 * n + t], out[3 * n + t]) for t in range(n)], out[-1]


def _split_wait(name, started, copies, after):
    n = len(started)

    def body(*refs):
        for per_tensor in copies(refs[:n], refs[n:2 * n], refs[2 * n:3 * n], refs[3 * n:4 * n]):
            for cp in per_tensor:
                cp.wait_send()
                cp.wait_recv()

    srcs = [e[2] for e in started]
    lands = [e[3] for e in started]
    out = pl.pallas_call(
        body, name=name, out_shape=[pltpu.HBM(a.shape, a.dtype) for a in srcs + lands],
        in_specs=[_HBM] * (2 * n) + [_SEM] * (2 * n) + [_ANY], out_specs=[_HBM] * (2 * n),
        input_output_aliases={i: i for i in range(2 * n)},
        compiler_params=pltpu.CompilerParams(has_side_effects=_EFFECT),
    )(*srcs, *lands, *[e[0] for e in started], *[e[1] for e in started], after)
    return out[:n], out[n:]


def _gather_step(name, arrived, fresh, after=None):
    n, m = len(arrived), len(fresh)
    order = [] if after is None else [after]
    fresh_lands = [lax.empty((N_DEV,) + s.shape, s.dtype) for s in fresh]
    buffers = [e[2] for e in arrived] + [e[3] for e in arrived] + list(fresh) + fresh_lands
    old_sems = [e[0] for e in arrived] + [e[1] for e in arrived]
    n_buf, n_old = len(buffers), len(old_sems)
    first_new = n_buf + n_old + len(order)

    def body(*refs):
        bufs, old = refs[:n_buf], refs[n_buf:n_buf + n_old]
        new = refs[first_new:first_new + 2 * n + 2 * m]
        for per_tensor in _hop1_copies(bufs[:n], bufs[n:2 * n], old[:n], old[n:]):
            for cp in per_tensor:
                cp.wait_send()
                cp.wait_recv()
        second = _hop2_copies(bufs[n:2 * n], new[:n], new[n:2 * n])
        first = _hop1_copies(bufs[2 * n:2 * n + m], bufs[2 * n + m:], new[2 * n:2 * n + m], new[2 * n + m:])
        for per_tensor in second + first:
            for cp in per_tensor:
                cp.start()
        refs[-1][...] = jnp.zeros_like(refs[-1])

    n_new = 2 * n + 2 * m
    out_shape = ([pltpu.SemaphoreType.DMA((4,)) for _ in range(2 * n)]
                 + [pltpu.SemaphoreType.DMA((3,)) for _ in range(2 * m)]
                 + [pltpu.HBM(a.shape, a.dtype) for a in buffers] + [_sds((8, 128), F32)])
    out = pl.pallas_call(
        body, name=name, out_shape=out_shape,
        in_specs=[_HBM] * n_buf + [_SEM] * n_old + [_ANY] * len(order),
        out_specs=[_SEM] * n_new + [_HBM] * n_buf + [pl.BlockSpec(memory_space=pltpu.VMEM)],
        input_output_aliases={i: n_new + i for i in range(n_buf)},
        compiler_params=pltpu.CompilerParams(has_side_effects=_EFFECT),
    )(*[_hbm(a) for a in buffers], *old_sems, *order)
    sems, bufs = out[:n_new], out[n_new:n_new + n_buf]
    second = [(sems[t], sems[n + t], bufs[t], bufs[n + t]) for t in range(n)]
    first = [(sems[2 * n + t], sems[2 * n + m + t], bufs[2 * n + t], bufs[2 * n + m + t]) for t in range(m)]
    return second, first, out[-1]


def _gather_wait(name, second, after):
    return _split_wait(name, second, lambda srcs, lands, send, recv: _hop2_copies(lands, send, recv), after)


def _sibling_copies(srcs, lands, send_sems, recv_sems):
    x, y, c = _position()
    return [[pltpu.make_async_remote_copy(
        src_ref=srcs[t], dst_ref=lands[t], send_sem=send_sems[t].at[0], recv_sem=recv_sems[t].at[0],
        device_id=(x, y, 1 - c), device_id_type=MESH)] for t in range(len(srcs))]


def _sibling_start(name, arrays):
    lands = [lax.empty(a.shape, a.dtype) for a in arrays]
    return _split_start(name, arrays, lands, _sibling_copies, 1)


def _sibling_wait(name, started, after):
    return _split_wait(name, started, _sibling_copies, after)[1]


def _small_copies(srcs, lands, send_sems, recv_sems):
    x, y, c = _position()
    mine = _slot(x, y, c)
    peers = [(x ^ ((k >> 2) & 1), y ^ ((k >> 1) & 1), c ^ (k & 1)) for k in range(1, N_DEV)]
    return [[pltpu.make_async_remote_copy(
        src_ref=srcs[t], dst_ref=lands[t].at[mine], send_sem=send_sems[t].at[k], recv_sem=recv_sems[t].at[k],
        device_id=peer, device_id_type=MESH) for k, peer in enumerate(peers)] for t in range(len(srcs))]


def _gather_finish(name, shards, lands, after):
    n = len(shards)

    def body(*refs):
        srcs, lands_in, outs = refs[:n], refs[n:2 * n], refs[2 * n:3 * n]
        send_sems, recv_sems, local_sems = refs[3 * n:]
        x, y, c = _position()
        local = [pltpu.make_async_copy(srcs[t], outs[t].at[_slot(x, y, c)], local_sems.at[t]) for t in range(n)]

        def diagonal(t, core):
            block = outs[t].at[_slot(1 - x, 1 - y, core)]
            return pltpu.make_async_remote_copy(
                src_ref=block, dst_ref=block, send_sem=send_sems.at[t], recv_sem=recv_sems.at[t],
                device_id=(x, y, 1 - c), device_id_type=MESH)

        for cp in local:
            cp.start()
        for t in range(n):
            diagonal(t, c).start()
        for t in range(n):
            diagonal(t, c).wait_send()
            diagonal(t, 1 - c).wait_recv()
        for cp in local:
            cp.wait()

    return _pcall(name, body, [*shards, *lands], [_ANY] * (2 * n),
                  [_sds(l.shape, l.dtype) for l in lands], [_ANY] * n,
                  scratch=[pltpu.SemaphoreType.DMA((n,)), pltpu.SemaphoreType.DMA((n,)),
                           pltpu.SemaphoreType.DMA((n,))],
                  after=after, aliases={n + t: t for t in range(n)})


def _exchange_sibling(name, fulls, after):
    n = len(fulls)

    def body(*refs):
        src = refs[:n]
        out = refs[n:2 * n]
        send_sems, recv_sems = refs[2 * n:]
        x, y, c = _position()
        copies = [pltpu.make_async_remote_copy(
            src_ref=src[t].at[:, 1 - c], dst_ref=out[t], send_sem=send_sems.at[t], recv_sem=recv_sems.at[t],
            device_id=(x, y, 1 - c), device_id_type=MESH) for t in range(n)]
        for cp in copies:
            cp.start()
        for cp in copies:
            cp.wait()

    return _pcall(name, body, fulls, [_ANY] * n, [_sds((N_CHIPS,) + f.shape[2:], f.dtype) for f in fulls],
                  [_ANY] * n, scratch=[pltpu.SemaphoreType.DMA((n,)), pltpu.SemaphoreType.DMA((n,))],
                  after=after)


def _add_sibling(name, full, recv, core, after):
    _, _, r, c = full.shape
    tr = _tile(r, max(8, (256 * 1024) // c))

    def body(core_ref, f_ref, r_ref, o_ref):
        o_ref[...] = (f_ref[...].astype(F32) + r_ref[...].astype(F32)).astype(o_ref.dtype)

    return _pcall(
        name, body, [full, recv],
        [pl.BlockSpec((None, None, tr, c), lambda p, i, core_ref: (p, core_ref[0], i, 0)),
         pl.BlockSpec((None, tr, c), lambda p, i, core_ref: (p, i, 0))],
        _sds((N_CHIPS, r, c), BF16), pl.BlockSpec((None, tr, c), lambda p, i, core_ref: (p, i, 0)),
        grid=(N_CHIPS, r // tr), sem=("parallel", "parallel"), prefetch=[core], after=after)


def _scatter_copies(srcs, lands, send_sems, recv_sems):
    x, y, c = _position()
    return [[pltpu.make_async_remote_copy(
        src_ref=srcs[t].at[2 * px + py], dst_ref=lands[t].at[j],
        send_sem=send_sems[t].at[j], recv_sem=recv_sems[t].at[j],
        device_id=(px, py, c), device_id_type=MESH) for j, (px, py) in enumerate(_other_chips(x, y))]
        for t in range(len(srcs))]


def _scatter_start(name, partials):
    lands = [lax.empty((N_CHIPS - 1,) + p.shape[1:], p.dtype) for p in partials]
    return _split_start(name, partials, lands, _scatter_copies, 3)


def _scatter_wait(name, started, after):
    return _split_wait(name, started, _scatter_copies, after)


def _reduce_adam(name, partial, recv, chip, w, m, v, layer, carried, after):
    n_layers, r, c = w.shape
    tr = _tile(r, max(8, (256 * 1024) // c))

    def body(chip_ref, p_ref, r_ref, w_ref, m_ref, v_ref, *rest):
        g_ref, d_ref, nm_ref, nv_ref = rest[-4:]
        g = p_ref[...].astype(F32)
        for j in range(N_CHIPS - 1):
            g = g + r_ref[j].astype(F32)
        g_ref[...] = g
        d_ref[...], nm_ref[...], nv_ref[...] = _adamw(w_ref[...], g, m_ref[...], v_ref[...])

    layered = pl.BlockSpec((None, tr, c), lambda i, chip_ref: (layer, i, 0))
    in_specs = [pl.BlockSpec((None, tr, c), lambda i, chip_ref: (chip_ref[0], i, 0)),
                pl.BlockSpec((N_CHIPS - 1, tr, c), lambda i, chip_ref: (0, i, 0)),
                layered, layered, layered]
    operands = [partial, recv, w, m, v]
    aliases = {}
    if carried is not None:
        operands += list(carried)
        in_specs += [_ANY] * 4
        aliases = {1 + 5 + o: o for o in range(4)}
    return _pcall(name, body, operands, in_specs, [_sds((n_layers, r, c), F32)] * 4, [layered] * 4,
                  grid=(r // tr,), sem=("parallel",), prefetch=[chip], after=after, aliases=aliases)


def _small_sum(name, gathered, own, device, after=None):
    r, lanes = own.shape

    def body(dev_ref, g_ref, own_ref, out_ref):
        dev = dev_ref[0]
        mine = own_ref[...]
        total = jnp.where(dev == 0, mine, g_ref[0])
        for d in range(1, N_DEV):
            total = total + jnp.where(dev == d, mine, g_ref[d])
        out_ref[...] = total

    return _pcall(name, body, [gathered, own],
                  [pl.BlockSpec((N_DEV, r, lanes), lambda i, dev_ref: (0, 0, 0)),
                   pl.BlockSpec((r, lanes), lambda i, dev_ref: (0, 0))],
                  _sds((r, lanes), F32), pl.BlockSpec((r, lanes), lambda i, dev_ref: (0, 0)),
                  grid=(1,), sem=("arbitrary",), prefetch=[device], after=after)


def _pack(arrays):
    return jnp.concatenate([a.reshape(-1, 128) for a in arrays], axis=0)


def _unpack(packed, shapes):
    out, row = [], 0
    for shape in shapes:
        rows = math.prod(shape) // 128
        out.append(packed[row:row + rows].reshape(shape))
        row += rows
    return out


class _Order:
    def __init__(self):
        self.last = None

    def __call__(self, fn, *args, **kwargs):
        out = fn(*args, after=self.last, **kwargs)
        self.last = out[0] if isinstance(out, (list, tuple)) else out
        return out


def kernel(x, a_w_in, a_ln_g, a_ln_b, a_w_s, a_b_s, a_w_out, b_w_in, b_w_grp, b_scale, b_w_out, norm_mix, norm_mlp, mlp_w1, mlp_w2, final_norm, loss_target, m_a_w_in, m_a_ln_g, m_a_ln_b, m_a_w_s, m_a_b_s, m_a_w_out, m_b_w_in, m_b_w_grp, m_b_scale, m_b_w_out, m_norm_mix, m_norm_mlp, m_mlp_w1, m_mlp_w2, m_final_norm, v_a_w_in, v_a_ln_g, v_a_ln_b, v_a_w_s, v_a_b_s, v_a_w_out, v_b_w_in, v_b_w_grp, v_b_scale, v_b_w_out, v_norm_mix, v_norm_mlp, v_mlp_w1, v_mlp_w2, v_final_norm):
    s, d = x.shape[1], x.shape[2]
    depth = mlp_w1.shape[0]
    a_slab = a_w_in.shape[2]
    ff_slab = mlp_w1.shape[2]
    ff_rows = mlp_w2.shape[1]
    bh = b_w_grp.shape[3]
    my_x, my_y, my_c = _position()
    core = jnp.reshape(my_c, (1,)).astype(jnp.int32)
    chip = jnp.reshape(2 * my_x + my_y, (1,)).astype(jnp.int32)
    device = _slot(my_x, my_y, my_c)
    run = _Order()

    w1_b, w2_b = mlp_w1.astype(BF16), mlp_w2.astype(BF16)
    shards = [a_w_in[0].astype(BF16), a_w_out[0].astype(BF16), b_scale,
              w1_b[0], w2_b[0],
              b_w_in[0].astype(BF16), b_w_grp[0].astype(BF16), b_w_out[0].astype(BF16),
              w1_b[1], w2_b[1]]
    groups = [[0], [1, 2], [3], [4], [5, 6, 7], [8], [9]]
    hop1, hop2 = {}, {}
    _, hop1[0], token = _gather_step("weights_group0_hop1", [], [shards[t] for t in groups[0]])
    _, hop1[1], token = _gather_step("weights_group1_hop1", [], [shards[t] for t in groups[1]], token)
    run.last = token

    def advance(g):
        if g not in hop1:
            return
        fresh = [shards[t] for t in groups[g + 1]] if g + 1 < len(groups) and g + 1 not in hop1 else []
        hop2[g], started, tok = _gather_step(f"weights_group{g}_hop2", hop1.pop(g), fresh, run.last)
        if fresh:
            hop1[g + 1] = started
        run.last = tok

    def gathered(g):
        advance(g)
        if g == 0:
            advance(1)
        srcs, lands = _gather_wait(f"weights_group{g}_wait", hop2.pop(g), run.last)
        run.last = srcs[0]
        return run(_gather_finish, f"weights_group{g}_finish", srcs, lands)

    h0 = x[0]
    target = loss_target[0]
    ln_g, ln_b = a_ln_g, a_ln_b
    w_s = a_w_s[0]
    b_s_col = a_b_s[0][:, :, None]
    nmix = [norm_mix[l][None, :] for l in range(depth)]
    nmlp = [norm_mlp[l][None, :] for l in range(depth)]

    def mlp_forward(l, h, up_group):
        hn = run(_rms_fwd, f"mlp{l}_norm", h, nmlp[l])
        (w1,) = gathered(up_group)
        advance(up_group + 1)
        act, act_sq = run(_mm_nn, f"mlp{l}_up", hn, w1,
                          lambda acc: (jnp.maximum(acc, 0.0), jnp.square(jnp.maximum(acc, 0.0))),
                          (BF16, BF16), slab=True)
        (w2,) = gathered(up_group + 1)
        advance(up_group + 2)
        w2 = w2.reshape(-1, d)
        (h_out,) = run(_mm_nn, f"mlp{l}_down", act_sq, w2, lambda acc, res: (acc + res,), (F32,),
                       extras=(h,), extra_kinds=("tile",))
        return h_out, (h, hn, act, act_sq, w1, w2)

    scattered = []

    def scatter_partials(name, partials, specs):
        in_flight, tok = _scatter_start(name + "_scatter_start", partials)
        run.last = tok
        scattered.append((name, in_flight, specs))

    def weight_grad(name, a, b, by_rows, block, between):
        other = run(_mm_tn_half, name + "_other", a, b, core, False, by_rows, block)
        sent, tok = _sibling_start(name + "_sibling_start", [other])
        run.last = tok
        middle = between()
        (recv,) = _sibling_wait(name + "_sibling_wait", sent, run.last)
        run.last = recv
        return run(_mm_tn_half, name + "_own", a, b, core, True, by_rows, block, recv=recv), middle

    def mlp_backward(l, saved, dh, dhb):
        h, hn, act, act_sq, w1, w2 = saved
        part_w2, (dpre,) = weight_grad(
            f"mlp{l}_down_dw", act_sq, dhb, True, ff_rows,
            lambda: run(_mm_nt, f"mlp{l}_down_dx", dhb, w2, lambda acc, a: (2.0 * a.astype(F32) * acc,),
                        (BF16,), extras=(act,), extra_kinds=("tile",)))
        scatter_partials(f"mlp{l}_down_grads", [part_w2], [("mlp_w2", l)])
        part_w1, (dhn,) = weight_grad(
            f"mlp{l}_up_dw", hn, dpre, False, ff_slab,
            lambda: run(_mm_nt, f"mlp{l}_up_dx", dpre, w1, lambda acc: (acc,), (F32,), slab=True))
        scatter_partials(f"mlp{l}_up_grads", [part_w1], [("mlp_w1", l)])
        dh, dhb, g_norm = run(_rms_bwd, f"mlp{l}_norm_bwd", dhn, h, nmlp[l], dh)
        return dh, dhb, g_norm

    hn0 = run(_rms_fwd, "mix0_norm", h0, nmix[0])
    (wa_in,) = gathered(0)
    (pre,) = run(_mm_nn, "mixa_in", hn0, wa_in, lambda acc: (acc,), (F32,), slab=True)
    wa_out, scale = gathered(1)
    wa_out, scale = wa_out.reshape(d, d), scale.reshape(1, d)
    gated = run(_amix_fwd, "mixa_gate", pre, ln_g, ln_b, w_s, b_s_col)
    advance(2)
    (h1,) = run(_mm_nn, "mixa_out", gated, wa_out, lambda acc, res: (acc + res,), (F32,),
                extras=(h0,), extra_kinds=("tile",))
    h2, saved_mlp0 = mlp_forward(0, h1, 2)
    hn2 = run(_rms_fwd, "mix1_norm", h2, nmix[1])
    wb_in, wb_grp, wb_out = gathered(4)
    advance(5)
    wb_in, wb_out = wb_in.reshape(d, d), wb_out.reshape(d, d)
    wb_grp = jnp.transpose(wb_grp, (1, 0, 2, 3)).reshape(B_GROUPS, bh, bh)
    (vb,) = run(_mm_nn, "mixb_in", hn2, wb_in, lambda acc: (acc,), (F32,))
    pooled = run(_pool, "mixb_pool", vb, backward=False)
    tm = _tile(s, 1024)
    grp_tile = pl.BlockSpec((tm, bh), lambda i, j, k: (i, j))
    grp_weight = pl.BlockSpec((None, bh, bh), lambda i, j, k: (j, 0, 0))
    mixed, mixed_scaled = run(
        _matmul, "mixb_grp", pooled, wb_grp, NN, (s // tm, B_GROUPS, 1), grp_tile, grp_weight,
        [_sds((s, d), BF16), _sds((s, d), BF16)], [grp_tile] * 2,
        (tm, bh), lambda acc, sc: (acc, acc * sc), (scale,), [pl.BlockSpec((1, bh), lambda i, j, k: (0, j))])
    (h3,) = run(_mm_nn, "mixb_out", mixed_scaled, wb_out, lambda acc, res: (acc + res,), (F32,),
                extras=(h2,), extra_kinds=("tile",))
    h4, saved_mlp1 = mlp_forward(1, h3, 5)
    dh, dhb, g_final, loss_part = run(_loss_head, "loss_head", h4, final_norm[None, :], target)

    dh, dhb, g_nmlp1 = mlp_backward(1, saved_mlp1, dh, dhb)
    tks = _tile(s, 1024)
    grp_rows = pl.BlockSpec((tks, bh), lambda i, j, k: (k, j))

    def mixb_middle():
        dms_scaled, dms_mixed = run(
            _mm_nt, "mixb_out_dx", dhb, wb_out,
            lambda acc, sc, mx: (acc * sc, acc * mx.astype(F32)), (BF16, F32),
            extras=(scale, mixed), extra_kinds=("row", "tile"))
        g_scale = run(_colsum, "mixb_scale_dw", dms_mixed)
        (g_wb_grp,) = run(
            _matmul, "mixb_grp_dw", pooled, dms_scaled, TN, (1, B_GROUPS, s // tks), grp_rows, grp_rows,
            [_sds((B_GROUPS, bh, bh), BF16)], [grp_weight], (bh, bh), lambda acc: (acc,))
        (dpooled,) = run(
            _matmul, "mixb_grp_dx", dms_scaled, wb_grp, NT, (s // tm, B_GROUPS, 1), grp_tile, grp_weight,
            [_sds((s, d), F32)], [grp_tile], (tm, bh), lambda acc: (acc,))
        return g_scale, g_wb_grp, run(_pool, "mixb_pool_bwd", dpooled, backward=True)

    part_wb_out, (g_scale, g_wb_grp, dvb) = weight_grad("mixb_out_dw", mixed_scaled, dhb, True, d // N_DEV,
                                                        mixb_middle)
    part_wb_in, (dhn2,) = weight_grad(
        "mixb_in_dw", hn2, dvb, True, d // N_DEV,
        lambda: run(_mm_nt, "mixb_in_dx", dvb, wb_in, lambda acc: (acc,), (F32,)))
    grp_full = jnp.transpose(g_wb_grp.reshape(B_GROUPS, N_DEV, bh // N_DEV, bh), (1, 0, 2, 3))
    grp_full = grp_full.reshape(N_CHIPS, 2, B_GROUPS * bh // N_DEV, bh)
    (grp_sibling,) = run(_exchange_sibling, "mixb_grp_dw_to_sibling", [grp_full])
    part_wb_grp = run(_add_sibling, "mixb_grp_dw_add_sibling", grp_full, grp_sibling, core)
    scatter_partials("mixb_grads", [part_wb_out, part_wb_grp, part_wb_in],
                     [("b_w_out", 0), ("b_w_grp", 0), ("b_w_in", 0)])
    dh, dhb, g_nmix1 = run(_rms_bwd, "mix1_norm_bwd", dhn2, h2, nmix[1], dh)
    dh, dhb, g_nmlp0 = mlp_backward(0, saved_mlp0, dh, dhb)
    def mixa_middle():
        (dgated,) = run(_mm_nt, "mixa_out_dx", dhb, wa_out, lambda acc: (acc,), (F32,))
        return run(_amix_bwd, "mixa_gate_bwd", pre, dgated, ln_g, ln_b, w_s, b_s_col)

    part_wa_out, (dpre, g_ln_g, g_ln_b, g_w_s, g_b_s) = weight_grad("mixa_out_dw", gated, dhb, True, d // N_DEV,
                                                                     mixa_middle)
    part_wa_in, (dhn0,) = weight_grad(
        "mixa_in_dw", hn0, dpre, False, a_slab,
        lambda: run(_mm_nt, "mixa_in_dx", dpre, wa_in, lambda acc: (acc,), (F32,), slab=True))
    scatter_partials("mixa_grads", [part_wa_in, part_wa_out], [("a_w_in", 0), ("a_w_out", 0)])
    grad_x, _, g_nmix0 = run(_rms_bwd, "mix0_norm_bwd", dhn0, h0, nmix[0], dh)

    g_norm_mix = jnp.concatenate([g_nmix0, g_nmix1], axis=0)
    g_norm_mlp = jnp.concatenate([g_nmlp0, g_nmlp1], axis=0)
    loss_row = jnp.pad(loss_part, ((0, 0), (0, 127)))
    small_parts = [g_ln_g, g_ln_b, g_w_s, g_b_s, g_norm_mix, g_norm_mlp, g_final, g_scale, loss_row]
    packed = _pack(small_parts)
    small_sent, tok = _split_start("small_grads_start", [packed], [lax.empty((N_DEV,) + packed.shape, F32)],
                                   _small_copies, N_DEV - 1)
    run.last = tok

    weights = {"a_w_in": (a_w_in, m_a_w_in, v_a_w_in), "a_w_out": (a_w_out, m_a_w_out, v_a_w_out),
               "b_w_in": (b_w_in, m_b_w_in, v_b_w_in), "b_w_grp": (b_w_grp, m_b_w_grp, v_b_w_grp),
               "b_w_out": (b_w_out, m_b_w_out, v_b_w_out), "mlp_w1": (mlp_w1, m_mlp_w1, v_mlp_w1),
               "mlp_w2": (mlp_w2, m_mlp_w2, v_mlp_w2)}
    results = {}

    def finish_group(name, in_flight, specs):
        partials, lands = _scatter_wait(name + "_scatter_wait", in_flight, run.last)
        run.last = lands[0]
        for t, (wname, layer) in enumerate(specs):
            w, m, v = weights[wname]
            _, r, c = partials[t].shape
            layers = w.shape[0]
            results[wname] = run(_reduce_adam, f"{name}_reduce_adam_{t}", partials[t], lands[t], chip,
                                 w.reshape(layers, r, c), m.reshape(layers, r, c), v.reshape(layers, r, c),
                                 layer, results.get(wname))

    for group in scattered[:-1]:
        finish_group(*group)
    own_packed, small_gathered = _split_wait("small_grads_wait", small_sent, _small_copies, run.last)
    run.last = small_gathered[0]
    small_sum = run(_small_sum, "small_grads_sum", small_gathered[0], own_packed[0],
                    jnp.reshape(device, (1,)).astype(jnp.int32))
    sg = _unpack(small_sum, [a_ln_g.shape, a_ln_b.shape, a_w_s.shape, a_b_s.shape, norm_mix.shape,
                             norm_mlp.shape, final_norm.shape, (1, d), (1, 128)])
    loss = sg.pop()[0, 0]
    shard = b_scale.shape[1]
    sg[7] = lax.dynamic_slice(sg[7], (0, device * shard), (1, shard))
    small_w = [a_ln_g, a_ln_b, a_w_s, a_b_s, norm_mix, norm_mlp, final_norm, b_scale]
    small_m = [m_a_ln_g, m_a_ln_b, m_a_w_s, m_a_b_s, m_norm_mix, m_norm_mlp, m_final_norm, m_b_scale]
    small_v = [v_a_ln_g, v_a_ln_b, v_a_w_s, v_a_b_s, v_norm_mix, v_norm_mlp, v_final_norm, v_b_scale]
    small_out = run(_adam_rows, "small_adam", _pack(sg), _pack(small_w), _pack(small_m), _pack(small_v))
    shapes = [w.shape for w in small_w]
    small_res = [sg] + [_unpack(o, shapes) for o in small_out]

    finish_group(*scattered[-1])
    big = {wname: [o.reshape(weights[wname][0].shape) for o in outs] for wname, outs in results.items()}

    def leaf(o):
        return (big["a_w_in"][o], small_res[o][0], small_res[o][1], small_res[o][2], small_res[o][3],
                big["a_w_out"][o], big["b_w_in"][o], big["b_w_grp"][o], small_res[o][7], big["b_w_out"][o],
                small_res[o][4], small_res[o][5], big["mlp_w1"][o], big["mlp_w2"][o], small_res[o][6])

    return (loss, grad_x[None], *leaf(0), *leaf(1), *leaf(2), *leaf(3))
```

```python
import math

import jax
import jax.numpy as jnp
from jax import lax
from jax.experimental import pallas as pl
from jax.experimental.pallas import tpu as pltpu

F32 = jnp.float32
BF16 = jnp.bfloat16
MESH = pl.DeviceIdType.MESH

N_DEV = 8
N_CHIPS = 4
CHUNK = 128
A_GROUPS = 8
B_WINDOWS = (2, 4, 8, 16)
B_GROUPS = len(B_WINDOWS)
EPS = 1e-6
ADAM_LR = 0.001
ADAM_B1 = 0.9
ADAM_B2 = 0.999
ADAM_EPS = 1e-08
ADAM_WD = 0.01
ADAM_STEP = 10

VMEM_LIMIT = 48 * 1024 * 1024

NN = (((1,), (0,)), ((), ()))
NT = (((1,), (1,)), ((), ()))
TN = (((0,), (0,)), ((), ()))

_ANY = pl.BlockSpec(memory_space=pl.ANY)
_HBM = pl.BlockSpec(memory_space=pltpu.HBM)
_SEM = pl.BlockSpec(memory_space=pltpu.SEMAPHORE)
_EFFECT = pltpu.SideEffectType.DATAFLOW_SIDE_EFFECTING


def _tile(n, pref):
    return pref if n % pref == 0 else n


def _sds(shape, dtype):
    return jax.ShapeDtypeStruct(shape, dtype)


def _pcall(name, body, operands, in_specs, out_shape, out_specs, *, grid=None, sem=None, scratch=(),
           prefetch=(), after=None, aliases=None):
    after = [] if after is None else [after]
    n_lead = len(prefetch) + len(operands)
    n_after = len(after)

    def wrapped(*refs):
        body(*refs[:n_lead], *refs[n_lead + n_after:])

    in_specs = list(in_specs) + [_ANY] * n_after
    params = pltpu.CompilerParams(vmem_limit_bytes=VMEM_LIMIT) if sem is None else \
        pltpu.CompilerParams(dimension_semantics=sem, vmem_limit_bytes=VMEM_LIMIT)
    kwargs = dict(out_shape=out_shape, scratch_shapes=list(scratch), compiler_params=params, name=name,
                  input_output_aliases=aliases or {})
    if prefetch:
        kwargs["grid_spec"] = pltpu.PrefetchScalarGridSpec(
            num_scalar_prefetch=len(prefetch), grid=grid, in_specs=in_specs, out_specs=out_specs,
            scratch_shapes=list(scratch))
        kwargs.pop("scratch_shapes")
    else:
        kwargs.update(in_specs=in_specs, out_specs=out_specs)
        if grid is not None:
            kwargs["grid"] = grid
    return pl.pallas_call(wrapped, **kwargs)(*prefetch, *operands, *after)


def _matmul(name, a, b, dims, grid, a_spec, b_spec, out_shape, out_specs, acc_shape,
            epilogue, extras=(), extra_specs=(), after=None, prefetch=(), b_parts=1):
    nk = grid[2]
    n_extra = len(extras)
    n_out = len(out_shape)
    n_pre = len(prefetch)

    def body(*refs):
        refs = refs[n_pre:]
        a_ref, b_ref = refs[0], refs[1]
        extra_refs = refs[2:2 + n_extra]
        out_refs = refs[2 + n_extra:2 + n_extra + n_out]

        def finish(acc):
            outs = epilogue(acc, *[r[...] for r in extra_refs])
            for o_ref, o in zip(out_refs, outs):
                o_ref[...] = o.astype(o_ref.dtype)

        def product():
            if b_parts == 1:
                return lax.dot_general(a_ref[...], b_ref[...], dims, preferred_element_type=F32)
            width = b_ref.shape[2]
            total = None
            for p in range(b_parts):
                part = lax.dot_general(a_ref[:, p * width:(p + 1) * width], b_ref[p], dims,
                                       preferred_element_type=F32)
                total = part if total is None else total + part
            return total

        if nk == 1:
            finish(product())
        else:
            acc_ref = refs[-1]
            k = pl.program_id(2)

            @pl.when(k == 0)
            def _():
                acc_ref[...] = product()

            if nk > 2:
                @pl.when(jnp.logical_and(k > 0, k < nk - 1))
                def _():
                    acc_ref[...] += product()

            @pl.when(k == nk - 1)
            def _():
                finish(acc_ref[...] + product())

    scratch = [] if nk == 1 else [pltpu.VMEM(acc_shape, F32)]
    return _pcall(name, body, [a, b, *extras], [a_spec, b_spec, *extra_specs], out_shape, out_specs,
                  grid=grid, sem=("parallel", "parallel", "arbitrary"), scratch=scratch, after=after,
                  prefetch=prefetch)


def _mm_nn(name, a, b, epilogue, out_dtypes, extras=(), extra_kinds=(), slab=False, after=None,
           tm=1024, tn=1024, tk=2048):
    m, kd = a.shape
    if slab:
        n_slab, _, w = b.shape
        n = n_slab * w
        tn = _tile(w, min(tn, w))
        per = w // tn
        tk = _tile(kd, tk)
        b_spec = pl.BlockSpec((None, tk, tn), lambda i, j, k: (j // per, k, j % per))
    else:
        n = b.shape[1]
        tn = _tile(n, tn)
        tk = _tile(kd, tk)
        b_spec = pl.BlockSpec((tk, tn), lambda i, j, k: (k, j))
    tm = _tile(m, tm)
    grid = (m // tm, n // tn, kd // tk)
    a_spec = pl.BlockSpec((tm, tk), lambda i, j, k: (i, k))
    tile_spec = pl.BlockSpec((tm, tn), lambda i, j, k: (i, j))
    row_spec = pl.BlockSpec((1, tn), lambda i, j, k: (0, j))
    extra_specs = [tile_spec if kind == "tile" else row_spec for kind in extra_kinds]
    return _matmul(name, a, b, NN, grid, a_spec, b_spec,
                   [_sds((m, n), d) for d in out_dtypes], [tile_spec for _ in out_dtypes],
                   (tm, tn), epilogue, extras, extra_specs, after=after)


def _mm_nt(name, a, b, epilogue, out_dtypes, extras=(), extra_kinds=(), slab=False, after=None,
           tm=1024, tn=1024, tk=2048):
    m, kd = a.shape
    parts = 1
    if slab:
        n_slab, n, w = b.shape
        tn = _tile(n, tn)
        if tk > w and tk % w == 0 and n_slab % (tk // w) == 0:
            parts = tk // w
            b_spec = pl.BlockSpec((parts, tn, w), lambda i, j, k: (k, j, 0))
        else:
            tk = _tile(w, min(tk, w))
            per = w // tk
            b_spec = pl.BlockSpec((None, tn, tk), lambda i, j, k: (k // per, j, k % per))
    else:
        n = b.shape[0]
        tn = _tile(n, tn)
        tk = _tile(kd, tk)
        b_spec = pl.BlockSpec((tn, tk), lambda i, j, k: (j, k))
    tm = _tile(m, tm)
    grid = (m // tm, n // tn, kd // tk)
    a_spec = pl.BlockSpec((tm, tk), lambda i, j, k: (i, k))
    tile_spec = pl.BlockSpec((tm, tn), lambda i, j, k: (i, j))
    row_spec = pl.BlockSpec((1, tn), lambda i, j, k: (0, j))
    extra_specs = [tile_spec if kind == "tile" else row_spec for kind in extra_kinds]
    return _matmul(name, a, b, NT, grid, a_spec, b_spec,
                   [_sds((m, n), d) for d in out_dtypes], [tile_spec for _ in out_dtypes],
                   (tm, tn), epilogue, extras, extra_specs, after=after, b_parts=parts)


def _mm_tn_half(name, a, b, core, own, by_rows, block, recv=None, after=None, tm=1024, tn=1024, tk=2048):
    s, m = a.shape
    n = b.shape[1]
    tk = _tile(s, tk)

    def owner(chip, core_ref):
        return 2 * chip + (core_ref[0] if own else 1 - core_ref[0])

    if by_rows:
        r, c = block, n
        tm, tn = _tile(r, min(tm, r)), _tile(c, tn)
        per = r // tm
        grid = (N_CHIPS * per, c // tn, s // tk)
        a_spec = pl.BlockSpec((tk, tm), lambda i, j, k, cr: (k, owner(i // per, cr) * per + i % per))
        b_spec = pl.BlockSpec((tk, tn), lambda i, j, k, cr: (k, j))
        o_spec = pl.BlockSpec((None, tm, tn), lambda i, j, k, cr: (i // per, i % per, j))
    else:
        r, c = m, block
        tm, tn = _tile(r, tm), _tile(c, min(tn, c))
        per = c // tn
        grid = (r // tm, N_CHIPS * per, s // tk)
        a_spec = pl.BlockSpec((tk, tm), lambda i, j, k, cr: (k, i))
        b_spec = pl.BlockSpec((tk, tn), lambda i, j, k, cr: (k, owner(j // per, cr) * per + j % per))
        o_spec = pl.BlockSpec((None, tm, tn), lambda i, j, k, cr: (j // per, i, j % per))
    if recv is None:
        extras, epilogue = (), lambda acc: (acc,)
    else:
        extras, epilogue = (recv,), lambda acc, other: (acc + other.astype(F32),)
    return _matmul(name, a, b, TN, grid, a_spec, b_spec, [_sds((N_CHIPS, r, c), BF16)], [o_spec], (tm, tn),
                   epilogue, extras, [o_spec] * len(extras), after=after, prefetch=[core])[0]


def _rms_fwd(name, h, g, after=None):
    s, d = h.shape
    tr = _tile(s, 256)

    def body(h_ref, g_ref, o_ref):
        x = h_ref[...]
        r = lax.rsqrt(jnp.mean(x * x, axis=-1, keepdims=True) + EPS)
        o_ref[...] = (x * r * g_ref[...]).astype(o_ref.dtype)

    row = pl.BlockSpec((tr, d), lambda i: (i, 0))
    vec = pl.BlockSpec((1, d), lambda i: (0, 0))
    return _pcall(name, body, [h, g], [row, vec], _sds((s, d), BF16), row, grid=(s // tr,),
                  sem=("parallel",), after=after)


def _accumulate(ref, part, step):
    @pl.when(step == 0)
    def _():
        ref[...] = part

    @pl.when(step > 0)
    def _():
        ref[...] += part


def _rms_bwd(name, dhn, h, g, dres, after=None):
    s, d = h.shape
    tr = _tile(s, 256)

    def body(dhn_ref, h_ref, g_ref, dres_ref, dh_ref, dhb_ref, gp_ref):
        x = h_ref[...]
        r = lax.rsqrt(jnp.mean(x * x, axis=-1, keepdims=True) + EPS)
        n = x * r
        dy = dhn_ref[...]
        dn = dy * g_ref[...]
        dh = dres_ref[...] + r * (dn - n * jnp.mean(dn * n, axis=-1, keepdims=True))
        dh_ref[...] = dh
        dhb_ref[...] = dh.astype(BF16)
        _accumulate(gp_ref, jnp.sum(dy * n, axis=0, keepdims=True), pl.program_id(0))

    row = pl.BlockSpec((tr, d), lambda i: (i, 0))
    vec = pl.BlockSpec((1, d), lambda i: (0, 0))
    return _pcall(name, body, [dhn, h, g, dres], [row, row, vec, row],
                  [_sds((s, d), F32), _sds((s, d), BF16), _sds((1, d), F32)], [row, row, vec],
                  grid=(s // tr,), sem=("arbitrary",), after=after)


def _loss_head(name, h, g, target, after=None):
    s, d = h.shape
    tr = _tile(s, 256)

    def body(h_ref, g_ref, t_ref, dh_ref, dhb_ref, gp_ref, loss_ref):
        x = h_ref[...]
        gg = g_ref[...]
        r = lax.rsqrt(jnp.mean(x * x, axis=-1, keepdims=True) + EPS)
        n = x * r
        e = n * gg - t_ref[...]
        dy = e * (1.0 / d)
        dn = dy * gg
        dh = r * (dn - n * jnp.mean(dn * n, axis=-1, keepdims=True))
        dh_ref[...] = dh
        dhb_ref[...] = dh.astype(BF16)
        step = pl.program_id(0)
        _accumulate(gp_ref, jnp.sum(dy * n, axis=0, keepdims=True), step)
        row_loss = jnp.mean(e * e, axis=-1, keepdims=True)
        _accumulate(loss_ref, 0.5 * jnp.sum(row_loss, axis=0, keepdims=True), step)

    row = pl.BlockSpec((tr, d), lambda i: (i, 0))
    vec = pl.BlockSpec((1, d), lambda i: (0, 0))
    one = pl.BlockSpec((1, 1), lambda i: (0, 0))
    return _pcall(name, body, [h, g, target], [row, vec, row],
                  [_sds((s, d), F32), _sds((s, d), BF16), _sds((1, d), F32), _sds((1, 1), F32)],
                  [row, row, vec, one], grid=(s // tr,), sem=("arbitrary",), after=after)


_SQRT_HALF = math.sqrt(0.5)
_INV_SQRT_2PI = 1.0 / math.sqrt(2.0 * math.pi)


def _gelu(x):
    return 0.5 * x * (1.0 + lax.erf(x * _SQRT_HALF))


def _gelu_grad(x):
    return 0.5 * (1.0 + lax.erf(x * _SQRT_HALF)) + x * jnp.exp(-0.5 * x * x) * _INV_SQRT_2PI


def _causal_mask():
    row = lax.broadcasted_iota(jnp.int32, (CHUNK, CHUNK), 0)
    col = lax.broadcasted_iota(jnp.int32, (CHUNK, CHUNK), 1)
    return row >= col


def _row_sum(x):
    return jnp.sum(x, axis=-1, keepdims=True)


def _masked_spatial(ws_ref, grp):
    return jnp.where(_causal_mask(), ws_ref[grp], 0.0).astype(BF16)


def _layernorm_stats(pre_ref, v_scr, w, head):
    total = jnp.zeros((CHUNK, 1), F32)
    for grp in range(A_GROUPS):
        v = _gelu(pre_ref[:, w + grp * head:w + (grp + 1) * head])
        v_scr[:, grp * head:(grp + 1) * head] = v
        total = total + _row_sum(v)
    mu = total * (1.0 / w)
    square = jnp.zeros((CHUNK, 1), F32)
    for grp in range(A_GROUPS):
        xc = v_scr[:, grp * head:(grp + 1) * head] - mu
        square = square + _row_sum(xc * xc)
    return mu, lax.rsqrt(square * (1.0 / w) + EPS)


def _amix_fwd(name, pre, ln_g, ln_b, w_s, b_s_col, after=None):
    s, w2 = pre.shape
    w = w2 // 2
    head = w // A_GROUPS

    def body(pre_ref, g_ref, b_ref, ws_ref, bs_ref, o_ref, v_scr):
        mu, rstd = _layernorm_stats(pre_ref, v_scr, w, head)
        for grp in range(A_GROUPS):
            cols = slice(grp * head, (grp + 1) * head)
            vhat = (v_scr[:, cols] - mu) * rstd
            vn = (vhat * g_ref[:, cols] + b_ref[:, cols]).astype(BF16)
            sg = jnp.dot(_masked_spatial(ws_ref, grp), vn, preferred_element_type=F32) + bs_ref[grp]
            o_ref[:, cols] = (_gelu(pre_ref[:, cols]) * sg).astype(o_ref.dtype)

    vec = pl.BlockSpec((1, w), lambda i: (0, 0))
    return _pcall(
        name, body, [pre, ln_g, ln_b, w_s, b_s_col],
        [pl.BlockSpec((CHUNK, w2), lambda i: (i, 0)), vec, vec,
         pl.BlockSpec((A_GROUPS, CHUNK, CHUNK), lambda i: (0, 0, 0)),
         pl.BlockSpec((A_GROUPS, CHUNK, 1), lambda i: (0, 0, 0))],
        _sds((s, w), BF16), pl.BlockSpec((CHUNK, w), lambda i: (i, 0)),
        grid=(s // CHUNK,), sem=("parallel",), scratch=[pltpu.VMEM((CHUNK, w), F32)], after=after)


def _amix_bwd(name, pre, dgated, ln_g, ln_b, w_s, b_s_col, after=None):
    s, w2 = pre.shape
    w = w2 // 2
    head = w // A_GROUPS

    def body(pre_ref, dg_ref, g_ref, b_ref, ws_ref, bs_ref, dpre_ref, glg_ref, glb_ref, gws_ref, gbs_ref,
             v_scr, dvn_scr):
        @pl.when(pl.program_id(0) == 0)
        def _():
            for ref in (glg_ref, glb_ref, gws_ref, gbs_ref):
                ref[...] = jnp.zeros_like(ref)

        mu, rstd = _layernorm_stats(pre_ref, v_scr, w, head)
        mask = _causal_mask()
        sum_dvhat = jnp.zeros((CHUNK, 1), F32)
        sum_dvhat_vhat = jnp.zeros((CHUNK, 1), F32)
        for grp in range(A_GROUPS):
            cols = slice(grp * head, (grp + 1) * head)
            vhat = (v_scr[:, cols] - mu) * rstd
            gain = g_ref[:, cols]
            vn = (vhat * gain + b_ref[:, cols]).astype(BF16)
            wm = _masked_spatial(ws_ref, grp)
            pre_u = pre_ref[:, cols]
            dgated = dg_ref[:, cols]
            ds = dgated * _gelu(pre_u)
            dsb = ds.astype(BF16)
            sg = jnp.dot(wm, vn, preferred_element_type=F32) + bs_ref[grp]
            dpre_ref[:, cols] = (dgated * sg * _gelu_grad(pre_u)).astype(dpre_ref.dtype)
            gws = lax.dot_general(dsb, vn, NT, preferred_element_type=F32)
            gws_ref[grp] += jnp.where(mask, gws, 0.0)
            gbs_ref[grp] += _row_sum(ds)
            dvn = lax.dot_general(wm, dsb, TN, preferred_element_type=F32)
            dvn_scr[:, cols] = dvn
            glg_ref[:, cols] += jnp.sum(dvn * vhat, axis=0, keepdims=True)
            glb_ref[:, cols] += jnp.sum(dvn, axis=0, keepdims=True)
            dvhat = dvn * gain
            sum_dvhat = sum_dvhat + _row_sum(dvhat)
            sum_dvhat_vhat = sum_dvhat_vhat + _row_sum(dvhat * vhat)
        mean_dvhat = sum_dvhat * (1.0 / w)
        mean_dvhat_vhat = sum_dvhat_vhat * (1.0 / w)
        for grp in range(A_GROUPS):
            cols = slice(grp * head, (grp + 1) * head)
            vhat = (v_scr[:, cols] - mu) * rstd
            dvhat = dvn_scr[:, cols] * g_ref[:, cols]
            dv = rstd * (dvhat - mean_dvhat - vhat * mean_dvhat_vhat)
            pre_v = pre_ref[:, w + grp * head:w + (grp + 1) * head]
            dpre_ref[:, w + grp * head:w + (grp + 1) * head] = (dv * _gelu_grad(pre_v)).astype(dpre_ref.dtype)

    vec = pl.BlockSpec((1, w), lambda i: (0, 0))
    ws_spec = pl.BlockSpec((A_GROUPS, CHUNK, CHUNK), lambda i: (0, 0, 0))
    bs_spec = pl.BlockSpec((A_GROUPS, CHUNK, 1), lambda i: (0, 0, 0))
    return _pcall(
        name, body, [pre, dgated, ln_g, ln_b, w_s, b_s_col],
        [pl.BlockSpec((CHUNK, w2), lambda i: (i, 0)), pl.BlockSpec((CHUNK, w), lambda i: (i, 0)),
         vec, vec, ws_spec, bs_spec],
        [_sds((s, w2), BF16), _sds((1, w), F32), _sds((1, w), F32),
         _sds((A_GROUPS, CHUNK, CHUNK), F32), _sds((A_GROUPS, CHUNK, 1), F32)],
        [pl.BlockSpec((CHUNK, w2), lambda i: (i, 0)), vec, vec, ws_spec, bs_spec],
        grid=(s // CHUNK,), sem=("arbitrary",),
        scratch=[pltpu.VMEM((CHUNK, w), F32), pltpu.VMEM((CHUNK, w), F32)], after=after)


def _shift_rows(x, k, forward):
    n = x.shape[0]
    row = lax.broadcasted_iota(jnp.int32, x.shape, 0)
    if forward:
        return jnp.where(row >= k, pltpu.roll(x, k, 0), 0.0)
    return jnp.where(row < n - k, pltpu.roll(x, n - k, 0), 0.0)


def _window_sum(x, window, forward):
    k = 1
    while k < window:
        x = x + _shift_rows(x, k, forward)
        k *= 2
    return x


def _pool(name, v, backward, after=None):
    s, w = v.shape
    head = w // B_GROUPS
    lane = _tile(head, 128)

    def body(v_ref, o_ref):
        grp = pl.program_id(0)
        x = v_ref[...]
        t = lax.broadcasted_iota(jnp.int32, x.shape, 0)
        for idx, window in enumerate(B_WINDOWS):
            @pl.when(grp == idx)
            def _():
                inv_count = 1.0 / jnp.minimum(t + 1, window).astype(F32)
                if backward:
                    out = _window_sum(x * inv_count, window, False) - x
                else:
                    out = _window_sum(x, window, True) * inv_count - x
                o_ref[...] = out.astype(o_ref.dtype)

    per = head // lane
    spec = pl.BlockSpec((s, lane), lambda g, j: (0, g * per + j))
    return _pcall(name, body, [v], [spec], _sds((s, w), BF16), spec, grid=(B_GROUPS, per),
                  sem=("parallel", "parallel"), after=after)


def _colsum(name, a, after=None):
    s, d = a.shape
    tr = _tile(s, 256)

    def body(a_ref, o_ref):
        _accumulate(o_ref, jnp.sum(a_ref[...], axis=0, keepdims=True), pl.program_id(0))

    return _pcall(name, body, [a], [pl.BlockSpec((tr, d), lambda i: (i, 0))], _sds((1, d), F32),
                  pl.BlockSpec((1, d), lambda i: (0, 0)), grid=(s // tr,), sem=("arbitrary",), after=after)


def _adamw(w, g, m, v):
    m = ADAM_B1 * m + (1.0 - ADAM_B1) * g
    v = ADAM_B2 * v + (1.0 - ADAM_B2) * (g * g)
    m_hat = m / (1.0 - ADAM_B1 ** ADAM_STEP)
    v_hat = v / (1.0 - ADAM_B2 ** ADAM_STEP)
    delta = -ADAM_LR * (m_hat / (jnp.sqrt(v_hat) + ADAM_EPS) + ADAM_WD * w)
    return delta, m, v


def _adam_rows(name, g, w, m, v, after=None):
    r, c = g.shape
    tr = _tile(r, 256)

    def body(g_ref, w_ref, m_ref, v_ref, d_ref, nm_ref, nv_ref):
        d_ref[...], nm_ref[...], nv_ref[...] = _adamw(w_ref[...], g_ref[...], m_ref[...], v_ref[...])

    spec = pl.BlockSpec((tr, c), lambda i: (i, 0))
    return _pcall(name, body, [g, w, m, v], [spec] * 4, [_sds((r, c), F32)] * 3, [spec] * 3,
                  grid=(r // tr,), sem=("parallel",), after=after)


def _position():
    return lax.axis_index("x"), lax.axis_index("y"), lax.axis_index("c")


def _other_chips(x, y):
    return [(1 - x, y), (x, 1 - y), (1 - x, 1 - y)]


def _slot(px, py, pc):
    return 4 * px + 2 * py + pc


def _hbm(a):
    return pltpu.with_memory_space_constraint(a, pltpu.HBM)


def _hop1_copies(srcs, lands, send_sems, recv_sems):
    x, y, c = _position()
    peers = [(x, y, 1 - c), (1 - x, y, c), (x, 1 - y, c)]
    mine = _slot(x, y, c)
    return [[pltpu.make_async_remote_copy(
        src_ref=srcs[t], dst_ref=lands[t].at[mine], send_sem=send_sems[t].at[k], recv_sem=recv_sems[t].at[k],
        device_id=peer, device_id_type=MESH) for k, peer in enumerate(peers)] for t in range(len(srcs))]


def _hop2_copies(lands, send_sems, recv_sems):
    x, y, c = _position()
    routes = [(_slot(1 - x, y, c), (x, 1 - y, c)), (_slot(x, 1 - y, c), (1 - x, y, c))]
    out = []
    for t in range(len(lands)):
        rows = lands[t].shape[1]
        halves = [(0, rows // 2), (rows // 2, rows - rows // 2)]
        per_tensor = []
        for h, ((slot, peer), (start, size)) in enumerate(zip(routes, halves)):
            if size:
                block = lands[t].at[slot, pl.ds(start, size)]
                per_tensor.append(pltpu.make_async_remote_copy(
                    src_ref=block, dst_ref=block, send_sem=send_sems[t].at[h], recv_sem=recv_sems[t].at[h],
                    device_id=peer, device_id_type=MESH))
        for j, (slot, _) in enumerate(routes):
            block = lands[t].at[slot]
            per_tensor.append(pltpu.make_async_remote_copy(
                src_ref=block, dst_ref=block, send_sem=send_sems[t].at[2 + j], recv_sem=recv_sems[t].at[2 + j],
                device_id=(x, y, 1 - c), device_id_type=MESH))
        out.append(per_tensor)
    return out


def _split_start(name, srcs, lands, copies, n_sems, after=None):
    n = len(srcs)
    order = [] if after is None else [after]
    n_in = 2 * n + len(order)

    def body(*refs):
        for per_tensor in copies(refs[:n], refs[n:2 * n], refs[n_in:n_in + n], refs[n_in + n:n_in + 2 * n]):
            for cp in per_tensor:
                cp.start()
        refs[-1][...] = jnp.zeros_like(refs[-1])

    out_shape = ([pltpu.SemaphoreType.DMA((n_sems,)) for _ in range(2 * n)]
                 + [pltpu.HBM(a.shape, a.dtype) for a in list(srcs) + list(lands)]
                 + [_sds((8, 128), F32)])
    out = pl.pallas_call(
        body, name=name, out_shape=out_shape, in_specs=[_HBM] * (2 * n) + [_ANY] * len(order),
        out_specs=[_SEM] * (2 * n) + [_HBM] * (2 * n) + [pl.BlockSpec(memory_space=pltpu.VMEM)],
        input_output_aliases={i: 2 * n + i for i in range(2 * n)},
        compiler_params=pltpu.CompilerParams(has_side_effects=_EFFECT),
    )(*[_hbm(a) for a in srcs], *[_hbm(a) for a in lands], *order)
    return [(out[t], out[n + t], out[2 * n + t], out[3 * n + t]) for t in range(n)], out[-1]


def _split_wait(name, started, copies, after):
    n = len(started)

    def body(*refs):
        for per_tensor in copies(refs[:n], refs[n:2 * n], refs[2 * n:3 * n], refs[3 * n:4 * n]):
            for cp in per_tensor:
                cp.wait_send()
                cp.wait_recv()

    srcs = [e[2] for e in started]
    lands = [e[3] for e in started]
    out = pl.pallas_call(
        body, name=name, out_shape=[pltpu.HBM(a.shape, a.dtype) for a in srcs + lands],
        in_specs=[_HBM] * (2 * n) + [_SEM] * (2 * n) + [_ANY], out_specs=[_HBM] * (2 * n),
        input_output_aliases={i: i for i in range(2 * n)},
        compiler_params=pltpu.CompilerParams(has_side_effects=_EFFECT),
    )(*srcs, *lands, *[e[0] for e in started], *[e[1] for e in started], after)
    return out[:n], out[n:]


def _gather_step(name, arrived, fresh, after=None):
    n, m = len(arrived), len(fresh)
    order = [] if after is None else [after]
    fresh_lands = [lax.empty((N_DEV,) + s.shape, s.dtype) for s in fresh]
    buffers = [e[2] for e in arrived] + [e[3] for e in arrived] + list(fresh) + fresh_lands
    old_sems = [e[0] for e in arrived] + [e[1] for e in arrived]
    n_buf, n_old = len(buffers), len(old_sems)
    first_new = n_buf + n_old + len(order)

    def body(*refs):
        bufs, old = refs[:n_buf], refs[n_buf:n_buf + n_old]
        new = refs[first_new:first_new + 2 * n + 2 * m]
        for per_tensor in _hop1_copies(bufs[:n], bufs[n:2 * n], old[:n], old[n:]):
            for cp in per_tensor:
                cp.wait_send()
                cp.wait_recv()
        second = _hop2_copies(bufs[n:2 * n], new[:n], new[n:2 * n])
        first = _hop1_copies(bufs[2 * n:2 * n + m], bufs[2 * n + m:], new[2 * n:2 * n + m], new[2 * n + m:])
        for per_tensor in second + first:
            for cp in per_tensor:
                cp.start()
        refs[-1][...] = jnp.zeros_like(refs[-1])

    n_new = 2 * n + 2 * m
    out_shape = ([pltpu.SemaphoreType.DMA((4,)) for _ in range(2 * n)]
                 + [pltpu.SemaphoreType.DMA((3,)) for _ in range(2 * m)]
                 + [pltpu.HBM(a.shape, a.dtype) for a in buffers] + [_sds((8, 128), F32)])
    out = pl.pallas_call(
        body, name=name, out_shape=out_shape,
        in_specs=[_HBM] * n_buf + [_SEM] * n_old + [_ANY] * len(order),
        out_specs=[_SEM] * n_new + [_HBM] * n_buf + [pl.BlockSpec(memory_space=pltpu.VMEM)],
        input_output_aliases={i: n_new + i for i in range(n_buf)},
        compiler_params=pltpu.CompilerParams(has_side_effects=_EFFECT),
    )(*[_hbm(a) for a in buffers], *old_sems, *order)
    sems, bufs = out[:n_new], out[n_new:n_new + n_buf]
    second = [(sems[t], sems[n + t], bufs[t], bufs[n + t]) for t in range(n)]
    first = [(sems[2 * n + t], sems[2 * n + m + t], bufs[2 * n + t], bufs[2 * n + m + t]) for t in range(m)]
    return second, first, out[-1]


def _gather_wait(name, second, after):
    return _split_wait(name, second, lambda srcs, lands, send, recv: _hop2_copies(lands, send, recv), after)


def _sibling_copies(srcs, lands, send_sems, recv_sems):
    x, y, c = _position()
    return [[pltpu.make_async_remote_copy(
        src_ref=srcs[t], dst_ref=lands[t], send_sem=send_sems[t].at[0], recv_sem=recv_sems[t].at[0],
        device_id=(x, y, 1 - c), device_id_type=MESH)] for t in range(len(srcs))]


def _sibling_start(name, arrays):
    lands = [lax.empty(a.shape, a.dtype) for a in arrays]
    return _split_start(name, arrays, lands, _sibling_copies, 1)


def _sibling_wait(name, started, after):
    return _split_wait(name, started, _sibling_copies, after)[1]


def _small_copies(srcs, lands, send_sems, recv_sems):
    x, y, c = _position()
    mine = _slot(x, y, c)
    peers = [(x ^ ((k >> 2) & 1), y ^ ((k >> 1) & 1), c ^ (k & 1)) for k in range(1, N_DEV)]
    return [[pltpu.make_async_remote_copy(
        src_ref=srcs[t], dst_ref=lands[t].at[mine], send_sem=send_sems[t].at[k], recv_sem=recv_sems[t].at[k],
        device_id=peer, device_id_type=MESH) for k, peer in enumerate(peers)] for t in range(len(srcs))]


def _gather_finish(name, shards, lands, after):
    n = len(shards)

    def body(*refs):
        srcs, lands_in, outs = refs[:n], refs[n:2 * n], refs[2 * n:3 * n]
        send_sems, recv_sems, local_sems = refs[3 * n:]
        x, y, c = _position()
        local = [pltpu.make_async_copy(srcs[t], outs[t].at[_slot(x, y, c)], local_sems.at[t]) for t in range(n)]

        def diagonal(t, core):
            block = outs[t].at[_slot(1 - x, 1 - y, core)]
            return pltpu.make_async_remote_copy(
                src_ref=block, dst_ref=block, send_sem=send_sems.at[t], recv_sem=recv_sems.at[t],
                device_id=(x, y, 1 - c), device_id_type=MESH)

        for cp in local:
            cp.start()
        for t in range(n):
            diagonal(t, c).start()
        for t in range(n):
            diagonal(t, c).wait_send()
            diagonal(t, 1 - c).wait_recv()
        for cp in local:
            cp.wait()

    return _pcall(name, body, [*shards, *lands], [_ANY] * (2 * n),
                  [_sds(l.shape, l.dtype) for l in lands], [_ANY] * n,
                  scratch=[pltpu.SemaphoreType.DMA((n,)), pltpu.SemaphoreType.DMA((n,)),
                           pltpu.SemaphoreType.DMA((n,))],
                  after=after, aliases={n + t: t for t in range(n)})


def _exchange_sibling(name, fulls, after):
    n = len(fulls)

    def body(*refs):
        src = refs[:n]
        out = refs[n:2 * n]
        send_sems, recv_sems = refs[2 * n:]
        x, y, c = _position()
        copies = [pltpu.make_async_remote_copy(
            src_ref=src[t].at[:, 1 - c], dst_ref=out[t], send_sem=send_sems.at[t], recv_sem=recv_sems.at[t],
            device_id=(x, y, 1 - c), device_id_type=MESH) for t in range(n)]
        for cp in copies:
            cp.start()
        for cp in copies:
            cp.wait()

    return _pcall(name, body, fulls, [_ANY] * n, [_sds((N_CHIPS,) + f.shape[2:], f.dtype) for f in fulls],
                  [_ANY] * n, scratch=[pltpu.SemaphoreType.DMA((n,)), pltpu.SemaphoreType.DMA((n,))],
                  after=after)


def _add_sibling(name, full, recv, core, after):
    _, _, r, c = full.shape
    tr = _tile(r, max(8, (256 * 1024) // c))

    def body(core_ref, f_ref, r_ref, o_ref):
        o_ref[...] = (f_ref[...].astype(F32) + r_ref[...].astype(F32)).astype(o_ref.dtype)

    return _pcall(
        name, body, [full, recv],
        [pl.BlockSpec((None, None, tr, c), lambda p, i, core_ref: (p, core_ref[0], i, 0)),
         pl.BlockSpec((None, tr, c), lambda p, i, core_ref: (p, i, 0))],
        _sds((N_CHIPS, r, c), BF16), pl.BlockSpec((None, tr, c), lambda p, i, core_ref: (p, i, 0)),
        grid=(N_CHIPS, r // tr), sem=("parallel", "parallel"), prefetch=[core], after=after)


def _scatter_copies(srcs, lands, send_sems, recv_sems):
    x, y, c = _position()
    return [[pltpu.make_async_remote_copy(
        src_ref=srcs[t].at[2 * px + py], dst_ref=lands[t].at[j],
        send_sem=send_sems[t].at[j], recv_sem=recv_sems[t].at[j],
        device_id=(px, py, c), device_id_type=MESH) for j, (px, py) in enumerate(_other_chips(x, y))]
        for t in range(len(srcs))]


def _scatter_start(name, partials):
    lands = [lax.empty((N_CHIPS - 1,) + p.shape[1:], p.dtype) for p in partials]
    return _split_start(name, partials, lands, _scatter_copies, 3)


def _scatter_wait(name, started, after):
    return _split_wait(name, started, _scatter_copies, after)


def _reduce_adam(name, partial, recv, chip, w, m, v, layer, carried, after):
    n_layers, r, c = w.shape
    tr = _tile(r, max(8, (256 * 1024) // c))

    def body(chip_ref, p_ref, r_ref, w_ref, m_ref, v_ref, *rest):
        g_ref, d_ref, nm_ref, nv_ref = rest[-4:]
        g = p_ref[...].astype(F32)
        for j in range(N_CHIPS - 1):
            g = g + r_ref[j].astype(F32)
        g_ref[...] = g
        d_ref[...], nm_ref[...], nv_ref[...] = _adamw(w_ref[...], g, m_ref[...], v_ref[...])

    layered = pl.BlockSpec((None, tr, c), lambda i, chip_ref: (layer, i, 0))
    in_specs = [pl.BlockSpec((None, tr, c), lambda i, chip_ref: (chip_ref[0], i, 0)),
                pl.BlockSpec((N_CHIPS - 1, tr, c), lambda i, chip_ref: (0, i, 0)),
                layered, layered, layered]
    operands = [partial, recv, w, m, v]
    aliases = {}
    if carried is not None:
        operands += list(carried)
        in_specs += [_ANY] * 4
        aliases = {1 + 5 + o: o for o in range(4)}
    return _pcall(name, body, operands, in_specs, [_sds((n_layers, r, c), F32)] * 4, [layered] * 4,
                  grid=(r // tr,), sem=("parallel",), prefetch=[chip], after=after, aliases=aliases)


def _small_sum(name, gathered, own, device, after=None):
    r, lanes = own.shape

    def body(dev_ref, g_ref, own_ref, out_ref):
        dev = dev_ref[0]
        mine = own_ref[...]
        total = jnp.where(dev == 0, mine, g_ref[0])
        for d in range(1, N_DEV):
            total = total + jnp.where(dev == d, mine, g_ref[d])
        out_ref[...] = total

    return _pcall(name, body, [gathered, own],
                  [pl.BlockSpec((N_DEV, r, lanes), lambda i, dev_ref: (0, 0, 0)),
                   pl.BlockSpec((r, lanes), lambda i, dev_ref: (0, 0))],
                  _sds((r, lanes), F32), pl.BlockSpec((r, lanes), lambda i, dev_ref: (0, 0)),
                  grid=(1,), sem=("arbitrary",), prefetch=[device], after=after)


def _pack(arrays):
    return jnp.concatenate([a.reshape(-1, 128) for a in arrays], axis=0)


def _unpack(packed, shapes):
    out, row = [], 0
    for shape in shapes:
        rows = math.prod(shape) // 128
        out.append(packed[row:row + rows].reshape(shape))
        row += rows
    return out


class _Order:
    def __init__(self):
        self.last = None

    def __call__(self, fn, *args, **kwargs):
        out = fn(*args, after=self.last, **kwargs)
        self.last = out[0] if isinstance(out, (list, tuple)) else out
        return out


def kernel(x, a_w_in, a_ln_g, a_ln_b, a_w_s, a_b_s, a_w_out, b_w_in, b_w_grp, b_scale, b_w_out, norm_mix, norm_mlp, mlp_w1, mlp_w2, final_norm, loss_target, m_a_w_in, m_a_ln_g, m_a_ln_b, m_a_w_s, m_a_b_s, m_a_w_out, m_b_w_in, m_b_w_grp, m_b_scale, m_b_w_out, m_norm_mix, m_norm_mlp, m_mlp_w1, m_mlp_w2, m_final_norm, v_a_w_in, v_a_ln_g, v_a_ln_b, v_a_w_s, v_a_b_s, v_a_w_out, v_b_w_in, v_b_w_grp, v_b_scale, v_b_w_out, v_norm_mix, v_norm_mlp, v_mlp_w1, v_mlp_w2, v_final_norm):
    s, d = x.shape[1], x.shape[2]
    depth = mlp_w1.shape[0]
    a_slab = a_w_in.shape[2]
    ff_slab = mlp_w1.shape[2]
    ff_rows = mlp_w2.shape[1]
    bh = b_w_grp.shape[3]
    my_x, my_y, my_c = _position()
    core = jnp.reshape(my_c, (1,)).astype(jnp.int32)
    chip = jnp.reshape(2 * my_x + my_y, (1,)).astype(jnp.int32)
    device = _slot(my_x, my_y, my_c)
    run = _Order()

    w1_b, w2_b = mlp_w1.astype(BF16), mlp_w2.astype(BF16)
    shards = [a_w_in[0].astype(BF16), a_w_out[0].astype(BF16), b_scale,
              w1_b[0], w2_b[0],
              b_w_in[0].astype(BF16), b_w_grp[0].astype(BF16), b_w_out[0].astype(BF16),
              w1_b[1], w2_b[1]]
    groups = [[0], [1, 2], [3], [4], [5, 6, 7], [8], [9]]
    start_with = {1: [2, 3], 2: [4], 3: [5], 4: [6]}
    hop1, hop2 = {}, {}
    _, hop1[0], token = _gather_step("weights_group0_hop1", [], [shards[t] for t in groups[0]])
    _, hop1[1], token = _gather_step("weights_group1_hop1", [], [shards[t] for t in groups[1]], token)
    run.last = token

    def advance(g):
        if g not in hop1:
            return
        ahead = start_with.get(g, [])
        fresh = [shards[t] for a in ahead for t in groups[a]]
        hop2[g], started, tok = _gather_step(f"weights_group{g}_hop2", hop1.pop(g), fresh, run.last)
        for a in ahead:
            hop1[a], started = started[:len(groups[a])], started[len(groups[a]):]
        run.last = tok

    def gathered(g):
        advance(g)
        if g == 0:
            advance(1)
        srcs, lands = _gather_wait(f"weights_group{g}_wait", hop2.pop(g), run.last)
        run.last = srcs[0]
        return run(_gather_finish, f"weights_group{g}_finish", srcs, lands)

    h0 = x[0]
    target = loss_target[0]
    ln_g, ln_b = a_ln_g, a_ln_b
    w_s = a_w_s[0]
    b_s_col = a_b_s[0][:, :, None]
    nmix = [norm_mix[l][None, :] for l in range(depth)]
    nmlp = [norm_mlp[l][None, :] for l in range(depth)]

    def mlp_forward(l, h, up_group):
        hn = run(_rms_fwd, f"mlp{l}_norm", h, nmlp[l])
        (w1,) = gathered(up_group)
        advance(up_group + 1)
        act, act_sq = run(_mm_nn, f"mlp{l}_up", hn, w1,
                          lambda acc: (jnp.maximum(acc, 0.0), jnp.square(jnp.maximum(acc, 0.0))),
                          (BF16, BF16), slab=True)
        (w2,) = gathered(up_group + 1)
        advance(up_group + 2)
        w2 = w2.reshape(-1, d)
        (h_out,) = run(_mm_nn, f"mlp{l}_down", act_sq, w2, lambda acc, res: (acc + res,), (F32,),
                       extras=(h,), extra_kinds=("tile",))
        return h_out, (h, hn, act, act_sq, w1, w2)

    scattered = []

    def scatter_partials(name, partials, specs):
        in_flight, tok = _scatter_start(name + "_scatter_start", partials)
        run.last = tok
        scattered.append((name, in_flight, specs))

    def weight_grad(name, a, b, by_rows, block, between):
        other = run(_mm_tn_half, name + "_other", a, b, core, False, by_rows, block)
        sent, tok = _sibling_start(name + "_sibling_start", [other])
        run.last = tok
        middle = between()
        (recv,) = _sibling_wait(name + "_sibling_wait", sent, run.last)
        run.last = recv
        return run(_mm_tn_half, name + "_own", a, b, core, True, by_rows, block, recv=recv), middle

    def mlp_backward(l, saved, dh, dhb):
        h, hn, act, act_sq, w1, w2 = saved
        part_w2, (dpre,) = weight_grad(
            f"mlp{l}_down_dw", act_sq, dhb, True, ff_rows,
            lambda: run(_mm_nt, f"mlp{l}_down_dx", dhb, w2, lambda acc, a: (2.0 * a.astype(F32) * acc,),
                        (BF16,), extras=(act,), extra_kinds=("tile",)))
        scatter_partials(f"mlp{l}_down_grads", [part_w2], [("mlp_w2", l)])
        part_w1, (dhn,) = weight_grad(
            f"mlp{l}_up_dw", hn, dpre, False, ff_slab,
            lambda: run(_mm_nt, f"mlp{l}_up_dx", dpre, w1, lambda acc: (acc,), (F32,), slab=True))
        scatter_partials(f"mlp{l}_up_grads", [part_w1], [("mlp_w1", l)])
        dh, dhb, g_norm = run(_rms_bwd, f"mlp{l}_norm_bwd", dhn, h, nmlp[l], dh)
        return dh, dhb, g_norm

    hn0 = run(_rms_fwd, "mix0_norm", h0, nmix[0])
    (wa_in,) = gathered(0)
    (pre,) = run(_mm_nn, "mixa_in", hn0, wa_in, lambda acc: (acc,), (F32,), slab=True)
    wa_out, scale = gathered(1)
    wa_out, scale = wa_out.reshape(d, d), scale.reshape(1, d)
    gated = run(_amix_fwd, "mixa_gate", pre, ln_g, ln_b, w_s, b_s_col)
    advance(2)
    (h1,) = run(_mm_nn, "mixa_out", gated, wa_out, lambda acc, res: (acc + res,), (F32,),
                extras=(h0,), extra_kinds=("tile",))
    h2, saved_mlp0 = mlp_forward(0, h1, 2)
    hn2 = run(_rms_fwd, "mix1_norm", h2, nmix[1])
    wb_in, wb_grp, wb_out = gathered(4)
    advance(5)
    wb_in, wb_out = wb_in.reshape(d, d), wb_out.reshape(d, d)
    wb_grp = jnp.transpose(wb_grp, (1, 0, 2, 3)).reshape(B_GROUPS, bh, bh)
    (vb,) = run(_mm_nn, "mixb_in", hn2, wb_in, lambda acc: (acc,), (F32,))
    pooled = run(_pool, "mixb_pool", vb, backward=False)
    tm = _tile(s, 1024)
    grp_tile = pl.BlockSpec((tm, bh), lambda i, j, k: (i, j))
    grp_weight = pl.BlockSpec((None, bh, bh), lambda i, j, k: (j, 0, 0))
    mixed, mixed_scaled = run(
        _matmul, "mixb_grp", pooled, wb_grp, NN, (s // tm, B_GROUPS, 1), grp_tile, grp_weight,
        [_sds((s, d), BF16), _sds((s, d), BF16)], [grp_tile] * 2,
        (tm, bh), lambda acc, sc: (acc, acc * sc), (scale,), [pl.BlockSpec((1, bh), lambda i, j, k: (0, j))])
    (h3,) = run(_mm_nn, "mixb_out", mixed_scaled, wb_out, lambda acc, res: (acc + res,), (F32,),
                extras=(h2,), extra_kinds=("tile",))
    h4, saved_mlp1 = mlp_forward(1, h3, 5)
    dh, dhb, g_final, loss_part = run(_loss_head, "loss_head", h4, final_norm[None, :], target)

    dh, dhb, g_nmlp1 = mlp_backward(1, saved_mlp1, dh, dhb)
    tks = _tile(s, 1024)
    grp_rows = pl.BlockSpec((tks, bh), lambda i, j, k: (k, j))

    def mixb_middle():
        dms_scaled, dms_mixed = run(
            _mm_nt, "mixb_out_dx", dhb, wb_out,
            lambda acc, sc, mx: (acc * sc, acc * mx.astype(F32)), (BF16, F32),
            extras=(scale, mixed), extra_kinds=("row", "tile"))
        g_scale = run(_colsum, "mixb_scale_dw", dms_mixed)
        (g_wb_grp,) = run(
            _matmul, "mixb_grp_dw", pooled, dms_scaled, TN, (1, B_GROUPS, s // tks), grp_rows, grp_rows,
            [_sds((B_GROUPS, bh, bh), BF16)], [grp_weight], (bh, bh), lambda acc: (acc,))
        (dpooled,) = run(
            _matmul, "mixb_grp_dx", dms_scaled, wb_grp, NT, (s // tm, B_GROUPS, 1), grp_tile, grp_weight,
            [_sds((s, d), F32)], [grp_tile], (tm, bh), lambda acc: (acc,))
        return g_scale, g_wb_grp, run(_pool, "mixb_pool_bwd", dpooled, backward=True)

    part_wb_out, (g_scale, g_wb_grp, dvb) = weight_grad("mixb_out_dw", mixed_scaled, dhb, True, d // N_DEV,
                                                        mixb_middle)
    part_wb_in, (dhn2,) = weight_grad(
        "mixb_in_dw", hn2, dvb, True, d // N_DEV,
        lambda: run(_mm_nt, "mixb_in_dx", dvb, wb_in, lambda acc: (acc,), (F32,)))
    grp_full = jnp.transpose(g_wb_grp.reshape(B_GROUPS, N_DEV, bh // N_DEV, bh), (1, 0, 2, 3))
    grp_full = grp_full.reshape(N_CHIPS, 2, B_GROUPS * bh // N_DEV, bh)
    (grp_sibling,) = run(_exchange_sibling, "mixb_grp_dw_to_sibling", [grp_full])
    part_wb_grp = run(_add_sibling, "mixb_grp_dw_add_sibling", grp_full, grp_sibling, core)
    scatter_partials("mixb_grads", [part_wb_out, part_wb_grp, part_wb_in],
                     [("b_w_out", 0), ("b_w_grp", 0), ("b_w_in", 0)])
    dh, dhb, g_nmix1 = run(_rms_bwd, "mix1_norm_bwd", dhn2, h2, nmix[1], dh)
    dh, dhb, g_nmlp0 = mlp_backward(0, saved_mlp0, dh, dhb)
    def mixa_middle():
        (dgated,) = run(_mm_nt, "mixa_out_dx", dhb, wa_out, lambda acc: (acc,), (F32,))
        return run(_amix_bwd, "mixa_gate_bwd", pre, dgated, ln_g, ln_b, w_s, b_s_col)

    part_wa_out, (dpre, g_ln_g, g_ln_b, g_w_s, g_b_s) = weight_grad("mixa_out_dw", gated, dhb, True, d // N_DEV,
                                                                     mixa_middle)
    part_wa_in, (dhn0,) = weight_grad(
        "mixa_in_dw", hn0, dpre, False, a_slab,
        lambda: run(_mm_nt, "mixa_in_dx", dpre, wa_in, lambda acc: (acc,), (F32,), slab=True))
    scatter_partials("mixa_grads", [part_wa_in, part_wa_out], [("a_w_in", 0), ("a_w_out", 0)])
    grad_x, _, g_nmix0 = run(_rms_bwd, "mix0_norm_bwd", dhn0, h0, nmix[0], dh)

    g_norm_mix = jnp.concatenate([g_nmix0, g_nmix1], axis=0)
    g_norm_mlp = jnp.concatenate([g_nmlp0, g_nmlp1], axis=0)
    loss_row = jnp.pad(loss_part, ((0, 0), (0, 127)))
    small_parts = [g_ln_g, g_ln_b, g_w_s, g_b_s, g_norm_mix, g_norm_mlp, g_final, g_scale, loss_row]
    packed = _pack(small_parts)
    small_sent, tok = _split_start("small_grads_start", [packed], [lax.empty((N_DEV,) + packed.shape, F32)],
                                   _small_copies, N_DEV - 1)
    run.last = tok

    weights = {"a_w_in": (a_w_in, m_a_w_in, v_a_w_in), "a_w_out": (a_w_out, m_a_w_out, v_a_w_out),
               "b_w_in": (b_w_in, m_b_w_in, v_b_w_in), "b_w_grp": (b_w_grp, m_b_w_grp, v_b_w_grp),
               "b_w_out": (b_w_out, m_b_w_out, v_b_w_out), "mlp_w1": (mlp_w1, m_mlp_w1, v_mlp_w1),
               "mlp_w2": (mlp_w2, m_mlp_w2, v_mlp_w2)}
    results = {}

    def finish_group(name, in_flight, specs):
        partials, lands = _scatter_wait(name + "_scatter_wait", in_flight, run.last)
        run.last = lands[0]
        for t, (wname, layer) in enumerate(specs):
            w, m, v = weights[wname]
            _, r, c = partials[t].shape
            layers = w.shape[0]
            results[wname] = run(_reduce_adam, f"{name}_reduce_adam_{t}", partials[t], lands[t], chip,
                                 w.reshape(layers, r, c), m.reshape(layers, r, c), v.reshape(layers, r, c),
                                 layer, results.get(wname))

    for group in scattered[:-1]:
        finish_group(*group)
    own_packed, small_gathered = _split_wait("small_grads_wait", small_sent, _small_copies, run.last)
    run.last = small_gathered[0]
    small_sum = run(_small_sum, "small_grads_sum", small_gathered[0], own_packed[0],
                    jnp.reshape(device, (1,)).astype(jnp.int32))
    sg = _unpack(small_sum, [a_ln_g.shape, a_ln_b.shape, a_w_s.shape, a_b_s.shape, norm_mix.shape,
                             norm_mlp.shape, final_norm.shape, (1, d), (1, 128)])
    loss = sg.pop()[0, 0]
    shard = b_scale.shape[1]
    sg[7] = lax.dynamic_slice(sg[7], (0, device * shard), (1, shard))
    small_w = [a_ln_g, a_ln_b, a_w_s, a_b_s, norm_mix, norm_mlp, final_norm, b_scale]
    small_m = [m_a_ln_g, m_a_ln_b, m_a_w_s, m_a_b_s, m_norm_mix, m_norm_mlp, m_final_norm, m_b_scale]
    small_v = [v_a_ln_g, v_a_ln_b, v_a_w_s, v_a_b_s, v_norm_mix, v_norm_mlp, v_final_norm, v_b_scale]
    small_out = run(_adam_rows, "small_adam", _pack(sg), _pack(small_w), _pack(small_m), _pack(small_v))
    shapes = [w.shape for w in small_w]
    small_res = [sg] + [_unpack(o, shapes) for o in small_out]

    finish_group(*scattered[-1])
    big = {wname: [o.reshape(weights[wname][0].shape) for o in outs] for wname, outs in results.items()}

    def leaf(o):
        return (big["a_w_in"][o], small_res[o][0], small_res[o][1], small_res[o][2], small_res[o][3],
                big["a_w_out"][o], big["b_w_in"][o], big["b_w_grp"][o], small_res[o][7], big["b_w_out"][o],
                small_res[o][4], small_res[o][5], big["mlp_w1"][o], big["mlp_w2"][o], small_res[o][6])

    return (loss, grad_x[None], *leaf(0), *leaf(1), *leaf(2), *leaf(3))
```

```python
import math

import jax
import jax.numpy as jnp
from jax import lax
from jax.experimental import pallas as pl
from jax.experimental.pallas import tpu as pltpu

F32 = jnp.float32
BF16 = jnp.bfloat16
MESH = pl.DeviceIdType.MESH

N_DEV = 8
N_CHIPS = 4
CHUNK = 128
A_GROUPS = 8
B_WINDOWS = (2, 4, 8, 16)
B_GROUPS = len(B_WINDOWS)
EPS = 1e-6
ADAM_LR = 0.001
ADAM_B1 = 0.9
ADAM_B2 = 0.999
ADAM_EPS = 1e-08
ADAM_WD = 0.01
ADAM_STEP = 10

VMEM_LIMIT = 48 * 1024 * 1024

NN = (((1,), (0,)), ((), ()))
NT = (((1,), (1,)), ((), ()))
TN = (((0,), (0,)), ((), ()))

_ANY = pl.BlockSpec(memory_space=pl.ANY)
_HBM = pl.BlockSpec(memory_space=pltpu.HBM)
_SEM = pl.BlockSpec(memory_space=pltpu.SEMAPHORE)
_EFFECT = pltpu.SideEffectType.DATAFLOW_SIDE_EFFECTING


def _tile(n, pref):
    return pref if n % pref == 0 else n


def _sds(shape, dtype):
    return jax.ShapeDtypeStruct(shape, dtype)


def _pcall(name, body, operands, in_specs, out_shape, out_specs, *, grid=None, sem=None, scratch=(),
           prefetch=(), after=None, aliases=None):
    after = [] if after is None else [after]
    n_lead = len(prefetch) + len(operands)
    n_after = len(after)

    def wrapped(*refs):
        body(*refs[:n_lead], *refs[n_lead + n_after:])

    in_specs = list(in_specs) + [_ANY] * n_after
    params = pltpu.CompilerParams(vmem_limit_bytes=VMEM_LIMIT) if sem is None else \
        pltpu.CompilerParams(dimension_semantics=sem, vmem_limit_bytes=VMEM_LIMIT)
    kwargs = dict(out_shape=out_shape, scratch_shapes=list(scratch), compiler_params=params, name=name,
                  input_output_aliases=aliases or {})
    if prefetch:
        kwargs["grid_spec"] = pltpu.PrefetchScalarGridSpec(
            num_scalar_prefetch=len(prefetch), grid=grid, in_specs=in_specs, out_specs=out_specs,
            scratch_shapes=list(scratch))
        kwargs.pop("scratch_shapes")
    else:
        kwargs.update(in_specs=in_specs, out_specs=out_specs)
        if grid is not None:
            kwargs["grid"] = grid
    return pl.pallas_call(wrapped, **kwargs)(*prefetch, *operands, *after)


def _matmul(name, a, b, dims, grid, a_spec, b_spec, out_shape, out_specs, acc_shape,
            epilogue, extras=(), extra_specs=(), after=None, prefetch=(), b_parts=1, rider=None):
    nk = grid[2]
    n_extra = len(extras)
    n_out = len(out_shape)
    operands, in_specs = [a, b, *extras], [a_spec, b_spec, *extra_specs]
    out_shape, out_specs, aliases = list(out_shape), list(out_specs), {}
    n_ride_in = 0
    if rider is not None:
        assert nk == 1, grid
        rider = rider(grid[0] * grid[1])
        prefetch = [*prefetch, rider.prefetch]

        def ride_spec(entry):
            if entry is None:
                return _ANY
            shape, index = entry
            return pl.BlockSpec(shape, lambda i, j, k, *pre, index=index: index(i * grid[1] + j, pre[-1]))

        n_ride_in = len(rider.operands)
        aliases = {len(prefetch) + len(operands) + i: n_out + o for i, o in rider.aliases.items()}
        operands += rider.operands
        in_specs += [ride_spec(e) for e in rider.in_specs]
        out_shape += rider.out_shape
        out_specs += [ride_spec(e) for e in rider.out_specs]
    n_pre = len(prefetch)

    def body(*refs):
        refs = refs[n_pre:]
        a_ref, b_ref = refs[0], refs[1]
        extra_refs = refs[2:2 + n_extra]
        first_out = 2 + n_extra + n_ride_in
        out_refs = refs[first_out:first_out + n_out]

        def finish(acc):
            outs = epilogue(acc, *[r[...] for r in extra_refs])
            for o_ref, o in zip(out_refs, outs):
                o_ref[...] = o.astype(o_ref.dtype)

        def product():
            if b_parts == 1:
                return lax.dot_general(a_ref[...], b_ref[...], dims, preferred_element_type=F32)
            width = b_ref.shape[2]
            total = None
            for p in range(b_parts):
                part = lax.dot_general(a_ref[:, p * width:(p + 1) * width], b_ref[p], dims,
                                       preferred_element_type=F32)
                total = part if total is None else total + part
            return total

        if nk == 1:
            finish(product())
            if rider is not None:
                rider.body(*refs[2 + n_extra:first_out],
                           *refs[first_out + n_out:first_out + n_out + len(rider.out_shape)])
        else:
            acc_ref = refs[-1]
            k = pl.program_id(2)

            @pl.when(k == 0)
            def _():
                acc_ref[...] = product()

            if nk > 2:
                @pl.when(jnp.logical_and(k > 0, k < nk - 1))
                def _():
                    acc_ref[...] += product()

            @pl.when(k == nk - 1)
            def _():
                finish(acc_ref[...] + product())

    scratch = [] if nk == 1 else [pltpu.VMEM(acc_shape, F32)]
    return _pcall(name, body, operands, in_specs, out_shape, out_specs,
                  grid=grid, sem=("parallel", "parallel", "arbitrary"), scratch=scratch, after=after,
                  prefetch=prefetch, aliases=aliases)


def _mm_nn(name, a, b, epilogue, out_dtypes, extras=(), extra_kinds=(), slab=False, after=None,
           tm=1024, tn=1024, tk=2048):
    m, kd = a.shape
    if slab:
        n_slab, _, w = b.shape
        n = n_slab * w
        tn = _tile(w, min(tn, w))
        per = w // tn
        tk = _tile(kd, tk)
        b_spec = pl.BlockSpec((None, tk, tn), lambda i, j, k: (j // per, k, j % per))
    else:
        n = b.shape[1]
        tn = _tile(n, tn)
        tk = _tile(kd, tk)
        b_spec = pl.BlockSpec((tk, tn), lambda i, j, k: (k, j))
    tm = _tile(m, tm)
    grid = (m // tm, n // tn, kd // tk)
    a_spec = pl.BlockSpec((tm, tk), lambda i, j, k: (i, k))
    tile_spec = pl.BlockSpec((tm, tn), lambda i, j, k: (i, j))
    row_spec = pl.BlockSpec((1, tn), lambda i, j, k: (0, j))
    extra_specs = [tile_spec if kind == "tile" else row_spec for kind in extra_kinds]
    return _matmul(name, a, b, NN, grid, a_spec, b_spec,
                   [_sds((m, n), d) for d in out_dtypes], [tile_spec for _ in out_dtypes],
                   (tm, tn), epilogue, extras, extra_specs, after=after)


def _mm_nt(name, a, b, epilogue, out_dtypes, extras=(), extra_kinds=(), slab=False, after=None,
           tm=1024, tn=1024, tk=2048, rider=None):
    m, kd = a.shape
    parts = 1
    if slab:
        n_slab, n, w = b.shape
        tn = _tile(n, tn)
        if tk > w and tk % w == 0 and n_slab % (tk // w) == 0:
            parts = tk // w
            b_spec = pl.BlockSpec((parts, tn, w), lambda i, j, k, *_: (k, j, 0))
        else:
            tk = _tile(w, min(tk, w))
            per = w // tk
            b_spec = pl.BlockSpec((None, tn, tk), lambda i, j, k, *_: (k // per, j, k % per))
    else:
        n = b.shape[0]
        tn = _tile(n, tn)
        tk = _tile(kd, tk)
        b_spec = pl.BlockSpec((tn, tk), lambda i, j, k, *_: (j, k))
    tm = _tile(m, tm)
    grid = (m // tm, n // tn, kd // tk)
    a_spec = pl.BlockSpec((tm, tk), lambda i, j, k, *_: (i, k))
    tile_spec = pl.BlockSpec((tm, tn), lambda i, j, k, *_: (i, j))
    row_spec = pl.BlockSpec((1, tn), lambda i, j, k, *_: (0, j))
    extra_specs = [tile_spec if kind == "tile" else row_spec for kind in extra_kinds]
    return _matmul(name, a, b, NT, grid, a_spec, b_spec,
                   [_sds((m, n), d) for d in out_dtypes], [tile_spec for _ in out_dtypes],
                   (tm, tn), epilogue, extras, extra_specs, after=after, b_parts=parts, rider=rider)


def _mm_tn_half(name, a, b, core, own, by_rows, block, recv=None, after=None, tm=1024, tn=1024, tk=2048,
                rider=None):
    s, m = a.shape
    n = b.shape[1]
    tk = _tile(s, tk)

    def owner(chip, core_ref):
        return 2 * chip + (core_ref[0] if own else 1 - core_ref[0])

    if by_rows:
        r, c = block, n
        tm, tn = _tile(r, min(tm, r)), _tile(c, tn)
        per = r // tm
        grid = (N_CHIPS * per, c // tn, s // tk)
        a_spec = pl.BlockSpec((tk, tm), lambda i, j, k, cr, *_: (k, owner(i // per, cr) * per + i % per))
        b_spec = pl.BlockSpec((tk, tn), lambda i, j, k, cr, *_: (k, j))
        o_spec = pl.BlockSpec((None, tm, tn), lambda i, j, k, cr, *_: (i // per, i % per, j))
    else:
        r, c = m, block
        tm, tn = _tile(r, tm), _tile(c, min(tn, c))
        per = c // tn
        grid = (r // tm, N_CHIPS * per, s // tk)
        a_spec = pl.BlockSpec((tk, tm), lambda i, j, k, cr, *_: (k, i))
        b_spec = pl.BlockSpec((tk, tn), lambda i, j, k, cr, *_: (k, owner(j // per, cr) * per + j % per))
        o_spec = pl.BlockSpec((None, tm, tn), lambda i, j, k, cr, *_: (j // per, i, j % per))
    if recv is None:
        extras, epilogue = (), lambda acc: (acc,)
    else:
        extras, epilogue = (recv,), lambda acc, other: (acc + other.astype(F32),)
    out = _matmul(name, a, b, TN, grid, a_spec, b_spec, [_sds((N_CHIPS, r, c), BF16)], [o_spec], (tm, tn),
                  epilogue, extras, [o_spec] * len(extras), after=after, prefetch=[core], rider=rider)
    return out[0] if rider is None else out


def _rms_fwd(name, h, g, after=None):
    s, d = h.shape
    tr = _tile(s, 256)

    def body(h_ref, g_ref, o_ref):
        x = h_ref[...]
        r = lax.rsqrt(jnp.mean(x * x, axis=-1, keepdims=True) + EPS)
        o_ref[...] = (x * r * g_ref[...]).astype(o_ref.dtype)

    row = pl.BlockSpec((tr, d), lambda i: (i, 0))
    vec = pl.BlockSpec((1, d), lambda i: (0, 0))
    return _pcall(name, body, [h, g], [row, vec], _sds((s, d), BF16), row, grid=(s // tr,),
                  sem=("parallel",), after=after)


def _accumulate(ref, part, step):
    @pl.when(step == 0)
    def _():
        ref[...] = part

    @pl.when(step > 0)
    def _():
        ref[...] += part


def _rms_bwd(name, dhn, h, g, dres, after=None):
    s, d = h.shape
    tr = _tile(s, 256)

    def body(dhn_ref, h_ref, g_ref, dres_ref, dh_ref, dhb_ref, gp_ref):
        x = h_ref[...]
        r = lax.rsqrt(jnp.mean(x * x, axis=-1, keepdims=True) + EPS)
        n = x * r
        dy = dhn_ref[...]
        dn = dy * g_ref[...]
        dh = dres_ref[...] + r * (dn - n * jnp.mean(dn * n, axis=-1, keepdims=True))
        dh_ref[...] = dh
        dhb_ref[...] = dh.astype(BF16)
        _accumulate(gp_ref, jnp.sum(dy * n, axis=0, keepdims=True), pl.program_id(0))

    row = pl.BlockSpec((tr, d), lambda i: (i, 0))
    vec = pl.BlockSpec((1, d), lambda i: (0, 0))
    return _pcall(name, body, [dhn, h, g, dres], [row, row, vec, row],
                  [_sds((s, d), F32), _sds((s, d), BF16), _sds((1, d), F32)], [row, row, vec],
                  grid=(s // tr,), sem=("arbitrary",), after=after)


def _loss_head(name, h, g, target, after=None):
    s, d = h.shape
    tr = _tile(s, 256)

    def body(h_ref, g_ref, t_ref, dh_ref, dhb_ref, gp_ref, loss_ref):
        x = h_ref[...]
        gg = g_ref[...]
        r = lax.rsqrt(jnp.mean(x * x, axis=-1, keepdims=True) + EPS)
        n = x * r
        e = n * gg - t_ref[...]
        dy = e * (1.0 / d)
        dn = dy * gg
        dh = r * (dn - n * jnp.mean(dn * n, axis=-1, keepdims=True))
        dh_ref[...] = dh
        dhb_ref[...] = dh.astype(BF16)
        step = pl.program_id(0)
        _accumulate(gp_ref, jnp.sum(dy * n, axis=0, keepdims=True), step)
        row_loss = jnp.mean(e * e, axis=-1, keepdims=True)
        _accumulate(loss_ref, 0.5 * jnp.sum(row_loss, axis=0, keepdims=True), step)

    row = pl.BlockSpec((tr, d), lambda i: (i, 0))
    vec = pl.BlockSpec((1, d), lambda i: (0, 0))
    one = pl.BlockSpec((1, 1), lambda i: (0, 0))
    return _pcall(name, body, [h, g, target], [row, vec, row],
                  [_sds((s, d), F32), _sds((s, d), BF16), _sds((1, d), F32), _sds((1, 1), F32)],
                  [row, row, vec, one], grid=(s // tr,), sem=("arbitrary",), after=after)


_SQRT_HALF = math.sqrt(0.5)
_INV_SQRT_2PI = 1.0 / math.sqrt(2.0 * math.pi)


def _gelu(x):
    return 0.5 * x * (1.0 + lax.erf(x * _SQRT_HALF))


def _gelu_grad(x):
    return 0.5 * (1.0 + lax.erf(x * _SQRT_HALF)) + x * jnp.exp(-0.5 * x * x) * _INV_SQRT_2PI


def _causal_mask():
    row = lax.broadcasted_iota(jnp.int32, (CHUNK, CHUNK), 0)
    col = lax.broadcasted_iota(jnp.int32, (CHUNK, CHUNK), 1)
    return row >= col


def _row_sum(x):
    return jnp.sum(x, axis=-1, keepdims=True)


def _masked_spatial(ws_ref, grp):
    return jnp.where(_causal_mask(), ws_ref[grp], 0.0).astype(BF16)


def _layernorm_stats(pre_ref, v_scr, w, head):
    total = jnp.zeros((CHUNK, 1), F32)
    for grp in range(A_GROUPS):
        v = _gelu(pre_ref[:, w + grp * head:w + (grp + 1) * head])
        v_scr[:, grp * head:(grp + 1) * head] = v
        total = total + _row_sum(v)
    mu = total * (1.0 / w)
    square = jnp.zeros((CHUNK, 1), F32)
    for grp in range(A_GROUPS):
        xc = v_scr[:, grp * head:(grp + 1) * head] - mu
        square = square + _row_sum(xc * xc)
    return mu, lax.rsqrt(square * (1.0 / w) + EPS)


def _amix_fwd(name, pre, ln_g, ln_b, w_s, b_s_col, after=None):
    s, w2 = pre.shape
    w = w2 // 2
    head = w // A_GROUPS

    def body(pre_ref, g_ref, b_ref, ws_ref, bs_ref, o_ref, v_scr):
        mu, rstd = _layernorm_stats(pre_ref, v_scr, w, head)
        for grp in range(A_GROUPS):
            cols = slice(grp * head, (grp + 1) * head)
            vhat = (v_scr[:, cols] - mu) * rstd
            vn = (vhat * g_ref[:, cols] + b_ref[:, cols]).astype(BF16)
            sg = jnp.dot(_masked_spatial(ws_ref, grp), vn, preferred_element_type=F32) + bs_ref[grp]
            o_ref[:, cols] = (_gelu(pre_ref[:, cols]) * sg).astype(o_ref.dtype)

    vec = pl.BlockSpec((1, w), lambda i: (0, 0))
    return _pcall(
        name, body, [pre, ln_g, ln_b, w_s, b_s_col],
        [pl.BlockSpec((CHUNK, w2), lambda i: (i, 0)), vec, vec,
         pl.BlockSpec((A_GROUPS, CHUNK, CHUNK), lambda i: (0, 0, 0)),
         pl.BlockSpec((A_GROUPS, CHUNK, 1), lambda i: (0, 0, 0))],
        _sds((s, w), BF16), pl.BlockSpec((CHUNK, w), lambda i: (i, 0)),
        grid=(s // CHUNK,), sem=("parallel",), scratch=[pltpu.VMEM((CHUNK, w), F32)], after=after)


def _amix_bwd(name, pre, dgated, ln_g, ln_b, w_s, b_s_col, after=None):
    s, w2 = pre.shape
    w = w2 // 2
    head = w // A_GROUPS

    def body(pre_ref, dg_ref, g_ref, b_ref, ws_ref, bs_ref, dpre_ref, glg_ref, glb_ref, gws_ref, gbs_ref,
             v_scr, dvn_scr):
        @pl.when(pl.program_id(0) == 0)
        def _():
            for ref in (glg_ref, glb_ref, gws_ref, gbs_ref):
                ref[...] = jnp.zeros_like(ref)

        mu, rstd = _layernorm_stats(pre_ref, v_scr, w, head)
        mask = _causal_mask()
        sum_dvhat = jnp.zeros((CHUNK, 1), F32)
        sum_dvhat_vhat = jnp.zeros((CHUNK, 1), F32)
        for grp in range(A_GROUPS):
            cols = slice(grp * head, (grp + 1) * head)
            vhat = (v_scr[:, cols] - mu) * rstd
            gain = g_ref[:, cols]
            vn = (vhat * gain + b_ref[:, cols]).astype(BF16)
            wm = _masked_spatial(ws_ref, grp)
            pre_u = pre_ref[:, cols]
            dgated = dg_ref[:, cols]
            ds = dgated * _gelu(pre_u)
            dsb = ds.astype(BF16)
            sg = jnp.dot(wm, vn, preferred_element_type=F32) + bs_ref[grp]
            dpre_ref[:, cols] = (dgated * sg * _gelu_grad(pre_u)).astype(dpre_ref.dtype)
            gws = lax.dot_general(dsb, vn, NT, preferred_element_type=F32)
            gws_ref[grp] += jnp.where(mask, gws, 0.0)
            gbs_ref[grp] += _row_sum(ds)
            dvn = lax.dot_general(wm, dsb, TN, preferred_element_type=F32)
            dvn_scr[:, cols] = dvn
            glg_ref[:, cols] += jnp.sum(dvn * vhat, axis=0, keepdims=True)
            glb_ref[:, cols] += jnp.sum(dvn, axis=0, keepdims=True)
            dvhat = dvn * gain
            sum_dvhat = sum_dvhat + _row_sum(dvhat)
            sum_dvhat_vhat = sum_dvhat_vhat + _row_sum(dvhat * vhat)
        mean_dvhat = sum_dvhat * (1.0 / w)
        mean_dvhat_vhat = sum_dvhat_vhat * (1.0 / w)
        for grp in range(A_GROUPS):
            cols = slice(grp * head, (grp + 1) * head)
            vhat = (v_scr[:, cols] - mu) * rstd
            dvhat = dvn_scr[:, cols] * g_ref[:, cols]
            dv = rstd * (dvhat - mean_dvhat - vhat * mean_dvhat_vhat)
            pre_v = pre_ref[:, w + grp * head:w + (grp + 1) * head]
            dpre_ref[:, w + grp * head:w + (grp + 1) * head] = (dv * _gelu_grad(pre_v)).astype(dpre_ref.dtype)

    vec = pl.BlockSpec((1, w), lambda i: (0, 0))
    ws_spec = pl.BlockSpec((A_GROUPS, CHUNK, CHUNK), lambda i: (0, 0, 0))
    bs_spec = pl.BlockSpec((A_GROUPS, CHUNK, 1), lambda i: (0, 0, 0))
    return _pcall(
        name, body, [pre, dgated, ln_g, ln_b, w_s, b_s_col],
        [pl.BlockSpec((CHUNK, w2), lambda i: (i, 0)), pl.BlockSpec((CHUNK, w), lambda i: (i, 0)),
         vec, vec, ws_spec, bs_spec],
        [_sds((s, w2), BF16), _sds((1, w), F32), _sds((1, w), F32),
         _sds((A_GROUPS, CHUNK, CHUNK), F32), _sds((A_GROUPS, CHUNK, 1), F32)],
        [pl.BlockSpec((CHUNK, w2), lambda i: (i, 0)), vec, vec, ws_spec, bs_spec],
        grid=(s // CHUNK,), sem=("arbitrary",),
        scratch=[pltpu.VMEM((CHUNK, w), F32), pltpu.VMEM((CHUNK, w), F32)], after=after)


def _shift_rows(x, k, forward):
    n = x.shape[0]
    row = lax.broadcasted_iota(jnp.int32, x.shape, 0)
    if forward:
        return jnp.where(row >= k, pltpu.roll(x, k, 0), 0.0)
    return jnp.where(row < n - k, pltpu.roll(x, n - k, 0), 0.0)


def _window_sum(x, window, forward):
    k = 1
    while k < window:
        x = x + _shift_rows(x, k, forward)
        k *= 2
    return x


def _pool(name, v, backward, after=None):
    s, w = v.shape
    head = w // B_GROUPS
    lane = _tile(head, 128)

    def body(v_ref, o_ref):
        grp = pl.program_id(0)
        x = v_ref[...]
        t = lax.broadcasted_iota(jnp.int32, x.shape, 0)
        for idx, window in enumerate(B_WINDOWS):
            @pl.when(grp == idx)
            def _():
                inv_count = 1.0 / jnp.minimum(t + 1, window).astype(F32)
                if backward:
                    out = _window_sum(x * inv_count, window, False) - x
                else:
                    out = _window_sum(x, window, True) * inv_count - x
                o_ref[...] = out.astype(o_ref.dtype)

    per = head // lane
    spec = pl.BlockSpec((s, lane), lambda g, j: (0, g * per + j))
    return _pcall(name, body, [v], [spec], _sds((s, w), BF16), spec, grid=(B_GROUPS, per),
                  sem=("parallel", "parallel"), after=after)


def _colsum(name, a, after=None):
    s, d = a.shape
    tr = _tile(s, 256)

    def body(a_ref, o_ref):
        _accumulate(o_ref, jnp.sum(a_ref[...], axis=0, keepdims=True), pl.program_id(0))

    return _pcall(name, body, [a], [pl.BlockSpec((tr, d), lambda i: (i, 0))], _sds((1, d), F32),
                  pl.BlockSpec((1, d), lambda i: (0, 0)), grid=(s // tr,), sem=("arbitrary",), after=after)


def _adamw(w, g, m, v):
    m = ADAM_B1 * m + (1.0 - ADAM_B1) * g
    v = ADAM_B2 * v + (1.0 - ADAM_B2) * (g * g)
    m_hat = m / (1.0 - ADAM_B1 ** ADAM_STEP)
    v_hat = v / (1.0 - ADAM_B2 ** ADAM_STEP)
    delta = -ADAM_LR * (m_hat / (jnp.sqrt(v_hat) + ADAM_EPS) + ADAM_WD * w)
    return delta, m, v


def _adam_rows(name, g, w, m, v, after=None):
    r, c = g.shape
    tr = _tile(r, 256)

    def body(g_ref, w_ref, m_ref, v_ref, d_ref, nm_ref, nv_ref):
        d_ref[...], nm_ref[...], nv_ref[...] = _adamw(w_ref[...], g_ref[...], m_ref[...], v_ref[...])

    spec = pl.BlockSpec((tr, c), lambda i: (i, 0))
    return _pcall(name, body, [g, w, m, v], [spec] * 4, [_sds((r, c), F32)] * 3, [spec] * 3,
                  grid=(r // tr,), sem=("parallel",), after=after)


def _position():
    return lax.axis_index("x"), lax.axis_index("y"), lax.axis_index("c")


def _other_chips(x, y):
    return [(1 - x, y), (x, 1 - y), (1 - x, 1 - y)]


def _slot(px, py, pc):
    return 4 * px + 2 * py + pc


def _hbm(a):
    return pltpu.with_memory_space_constraint(a, pltpu.HBM)


def _hop1_copies(srcs, lands, send_sems, recv_sems):
    x, y, c = _position()
    peers = [(x, y, 1 - c), (1 - x, y, c), (x, 1 - y, c)]
    mine = _slot(x, y, c)
    return [[pltpu.make_async_remote_copy(
        src_ref=srcs[t], dst_ref=lands[t].at[mine], send_sem=send_sems[t].at[k], recv_sem=recv_sems[t].at[k],
        device_id=peer, device_id_type=MESH) for k, peer in enumerate(peers)] for t in range(len(srcs))]


def _hop2_copies(lands, send_sems, recv_sems):
    x, y, c = _position()
    routes = [(_slot(1 - x, y, c), (x, 1 - y, c)), (_slot(x, 1 - y, c), (1 - x, y, c))]
    out = []
    for t in range(len(lands)):
        rows = lands[t].shape[1]
        halves = [(0, rows // 2), (rows // 2, rows - rows // 2)]
        per_tensor = []
        for h, ((slot, peer), (start, size)) in enumerate(zip(routes, halves)):
            if size:
                block = lands[t].at[slot, pl.ds(start, size)]
                per_tensor.append(pltpu.make_async_remote_copy(
                    src_ref=block, dst_ref=block, send_sem=send_sems[t].at[h], recv_sem=recv_sems[t].at[h],
                    device_id=peer, device_id_type=MESH))
        for j, (slot, _) in enumerate(routes):
            block = lands[t].at[slot]
            per_tensor.append(pltpu.make_async_remote_copy(
                src_ref=block, dst_ref=block, send_sem=send_sems[t].at[2 + j], recv_sem=recv_sems[t].at[2 + j],
                device_id=(x, y, 1 - c), device_id_type=MESH))
        out.append(per_tensor)
    return out


def _split_start(name, srcs, lands, copies, n_sems, after=None):
    n = len(srcs)
    order = [] if after is None else [after]
    n_in = 2 * n + len(order)

    def body(*refs):
        for per_tensor in copies(refs[:n], refs[n:2 * n], refs[n_in:n_in + n], refs[n_in + n:n_in + 2 * n]):
            for cp in per_tensor:
                cp.start()
        refs[-1][...] = jnp.zeros_like(refs[-1])

    out_shape = ([pltpu.SemaphoreType.DMA((n_sems,)) for _ in range(2 * n)]
                 + [pltpu.HBM(a.shape, a.dtype) for a in list(srcs) + list(lands)]
                 + [_sds((8, 128), F32)])
    out = pl.pallas_call(
        body, name=name, out_shape=out_shape, in_specs=[_HBM] * (2 * n) + [_ANY] * len(order),
        out_specs=[_SEM] * (2 * n) + [_HBM] * (2 * n) + [pl.BlockSpec(memory_space=pltpu.VMEM)],
        input_output_aliases={i: 2 * n + i for i in range(2 * n)},
        compiler_params=pltpu.CompilerParams(has_side_effects=_EFFECT),
    )(*[_hbm(a) for a in srcs], *[_hbm(a) for a in lands], *order)
    return [(out[t], out[n + t], out[2 * n + t], out[3 * n + t]) for t in range(n)], out[-1]


def _split_wait(name, started, copies, after):
    n = len(started)

    def body(*refs):
        for per_tensor in copies(refs[:n], refs[n:2 * n], refs[2 * n:3 * n], refs[3 * n:4 * n]):
            for cp in per_tensor:
                cp.wait_send()
                cp.wait_recv()

    srcs = [e[2] for e in started]
    lands = [e[3] for e in started]
    out = pl.pallas_call(
        body, name=name, out_shape=[pltpu.HBM(a.shape, a.dtype) for a in srcs + lands],
        in_specs=[_HBM] * (2 * n) + [_SEM] * (2 * n) + [_ANY], out_specs=[_HBM] * (2 * n),
        input_output_aliases={i: i for i in range(2 * n)},
        compiler_params=pltpu.CompilerParams(has_side_effects=_EFFECT),
    )(*srcs, *lands, *[e[0] for e in started], *[e[1] for e in started], after)
    return out[:n], out[n:]


def _gather_step(name, arrived, fresh, after=None):
    n, m = len(arrived), len(fresh)
    order = [] if after is None else [after]
    fresh_lands = [lax.empty((N_DEV,) + s.shape, s.dtype) for s in fresh]
    buffers = [e[2] for e in arrived] + [e[3] for e in arrived] + list(fresh) + fresh_lands
    old_sems = [e[0] for e in arrived] + [e[1] for e in arrived]
    n_buf, n_old = len(buffers), len(old_sems)
    first_new = n_buf + n_old + len(order)

    def body(*refs):
        bufs, old = refs[:n_buf], refs[n_buf:n_buf + n_old]
        new = refs[first_new:first_new + 2 * n + 2 * m]
        for per_tensor in _hop1_copies(bufs[:n], bufs[n:2 * n], old[:n], old[n:]):
            for cp in per_tensor:
                cp.wait_send()
                cp.wait_recv()
        second = _hop2_copies(bufs[n:2 * n], new[:n], new[n:2 * n])
        first = _hop1_copies(bufs[2 * n:2 * n + m], bufs[2 * n + m:], new[2 * n:2 * n + m], new[2 * n + m:])
        for per_tensor in second + first:
            for cp in per_tensor:
                cp.start()
        refs[-1][...] = jnp.zeros_like(refs[-1])

    n_new = 2 * n + 2 * m
    out_shape = ([pltpu.SemaphoreType.DMA((4,)) for _ in range(2 * n)]
                 + [pltpu.SemaphoreType.DMA((3,)) for _ in range(2 * m)]
                 + [pltpu.HBM(a.shape, a.dtype) for a in buffers] + [_sds((8, 128), F32)])
    out = pl.pallas_call(
        body, name=name, out_shape=out_shape,
        in_specs=[_HBM] * n_buf + [_SEM] * n_old + [_ANY] * len(order),
        out_specs=[_SEM] * n_new + [_HBM] * n_buf + [pl.BlockSpec(memory_space=pltpu.VMEM)],
        input_output_aliases={i: n_new + i for i in range(n_buf)},
        compiler_params=pltpu.CompilerParams(has_side_effects=_EFFECT),
    )(*[_hbm(a) for a in buffers], *old_sems, *order)
    sems, bufs = out[:n_new], out[n_new:n_new + n_buf]
    second = [(sems[t], sems[n + t], bufs[t], bufs[n + t]) for t in range(n)]
    first = [(sems[2 * n + t], sems[2 * n + m + t], bufs[2 * n + t], bufs[2 * n + m + t]) for t in range(m)]
    return second, first, out[-1]


def _gather_wait(name, second, after):
    return _split_wait(name, second, lambda srcs, lands, send, recv: _hop2_copies(lands, send, recv), after)


def _sibling_copies(srcs, lands, send_sems, recv_sems):
    x, y, c = _position()
    return [[pltpu.make_async_remote_copy(
        src_ref=srcs[t], dst_ref=lands[t], send_sem=send_sems[t].at[0], recv_sem=recv_sems[t].at[0],
        device_id=(x, y, 1 - c), device_id_type=MESH)] for t in range(len(srcs))]


def _sibling_start(name, arrays):
    lands = [lax.empty(a.shape, a.dtype) for a in arrays]
    return _split_start(name, arrays, lands, _sibling_copies, 1)


def _sibling_wait(name, started, after):
    return _split_wait(name, started, _sibling_copies, after)[1]


def _small_copies(srcs, lands, send_sems, recv_sems):
    x, y, c = _position()
    mine = _slot(x, y, c)
    peers = [(x ^ ((k >> 2) & 1), y ^ ((k >> 1) & 1), c ^ (k & 1)) for k in range(1, N_DEV)]
    return [[pltpu.make_async_remote_copy(
        src_ref=srcs[t], dst_ref=lands[t].at[mine], send_sem=send_sems[t].at[k], recv_sem=recv_sems[t].at[k],
        device_id=peer, device_id_type=MESH) for k, peer in enumerate(peers)] for t in range(len(srcs))]


def _gather_finish(name, shards, lands, after):
    n = len(shards)

    def body(*refs):
        srcs, lands_in, outs = refs[:n], refs[n:2 * n], refs[2 * n:3 * n]
        send_sems, recv_sems, local_sems = refs[3 * n:]
        x, y, c = _position()
        local = [pltpu.make_async_copy(srcs[t], outs[t].at[_slot(x, y, c)], local_sems.at[t]) for t in range(n)]

        def diagonal(t, core):
            block = outs[t].at[_slot(1 - x, 1 - y, core)]
            return pltpu.make_async_remote_copy(
                src_ref=block, dst_ref=block, send_sem=send_sems.at[t], recv_sem=recv_sems.at[t],
                device_id=(x, y, 1 - c), device_id_type=MESH)

        for cp in local:
            cp.start()
        for t in range(n):
            diagonal(t, c).start()
        for t in range(n):
            diagonal(t, c).wait_send()
            diagonal(t, 1 - c).wait_recv()
        for cp in local:
            cp.wait()

    return _pcall(name, body, [*shards, *lands], [_ANY] * (2 * n),
                  [_sds(l.shape, l.dtype) for l in lands], [_ANY] * n,
                  scratch=[pltpu.SemaphoreType.DMA((n,)), pltpu.SemaphoreType.DMA((n,)),
                           pltpu.SemaphoreType.DMA((n,))],
                  after=after, aliases={n + t: t for t in range(n)})


def _exchange_sibling(name, fulls, after):
    n = len(fulls)

    def body(*refs):
        src = refs[:n]
        out = refs[n:2 * n]
        send_sems, recv_sems = refs[2 * n:]
        x, y, c = _position()
        copies = [pltpu.make_async_remote_copy(
            src_ref=src[t].at[:, 1 - c], dst_ref=out[t], send_sem=send_sems.at[t], recv_sem=recv_sems.at[t],
            device_id=(x, y, 1 - c), device_id_type=MESH) for t in range(n)]
        for cp in copies:
            cp.start()
        for cp in copies:
            cp.wait()

    return _pcall(name, body, fulls, [_ANY] * n, [_sds((N_CHIPS,) + f.shape[2:], f.dtype) for f in fulls],
                  [_ANY] * n, scratch=[pltpu.SemaphoreType.DMA((n,)), pltpu.SemaphoreType.DMA((n,))],
                  after=after)


def _add_sibling(name, full, recv, core, after):
    _, _, r, c = full.shape
    tr = _tile(r, max(8, (256 * 1024) // c))

    def body(core_ref, f_ref, r_ref, o_ref):
        o_ref[...] = (f_ref[...].astype(F32) + r_ref[...].astype(F32)).astype(o_ref.dtype)

    return _pcall(
        name, body, [full, recv],
        [pl.BlockSpec((None, None, tr, c), lambda p, i, core_ref: (p, core_ref[0], i, 0)),
         pl.BlockSpec((None, tr, c), lambda p, i, core_ref: (p, i, 0))],
        _sds((N_CHIPS, r, c), BF16), pl.BlockSpec((None, tr, c), lambda p, i, core_ref: (p, i, 0)),
        grid=(N_CHIPS, r // tr), sem=("parallel", "parallel"), prefetch=[core], after=after)


def _scatter_copies(srcs, lands, send_sems, recv_sems):
    x, y, c = _position()
    return [[pltpu.make_async_remote_copy(
        src_ref=srcs[t].at[2 * px + py], dst_ref=lands[t].at[j],
        send_sem=send_sems[t].at[j], recv_sem=recv_sems[t].at[j],
        device_id=(px, py, c), device_id_type=MESH) for j, (px, py) in enumerate(_other_chips(x, y))]
        for t in range(len(srcs))]


def _scatter_start(name, partials):
    lands = [lax.empty((N_CHIPS - 1,) + p.shape[1:], p.dtype) for p in partials]
    return _split_start(name, partials, lands, _scatter_copies, 3)


def _scatter_wait(name, started, after):
    return _split_wait(name, started, _scatter_copies, after)


class _Job:
    def __init__(self, **fields):
        self.__dict__.update(fields)


def _reduce_adam_job(partial, recv, chip, w, m, v, layer, carried, n_blocks=None):
    n_layers, r, c = w.shape
    tr = _tile(r, max(8, (256 * 1024) // c)) if n_blocks is None else r // n_blocks

    def body(p_ref, r_ref, w_ref, m_ref, v_ref, *rest):
        g_ref, d_ref, nm_ref, nv_ref = rest[-4:]
        g = p_ref[...].astype(F32)
        for j in range(N_CHIPS - 1):
            g = g + r_ref[j].astype(F32)
        g_ref[...] = g
        d_ref[...], nm_ref[...], nv_ref[...] = _adamw(w_ref[...], g, m_ref[...], v_ref[...])

    layered = ((None, tr, c), lambda blk, chip_ref: (layer, blk, 0))
    in_specs = [((None, tr, c), lambda blk, chip_ref: (chip_ref[0], blk, 0)),
                ((N_CHIPS - 1, tr, c), lambda blk, chip_ref: (0, blk, 0)), layered, layered, layered]
    operands = [partial, recv, w, m, v]
    aliases = {}
    if carried is not None:
        operands += list(carried)
        in_specs += [None] * 4
        aliases = {5 + o: o for o in range(4)}
    return _Job(operands=operands, in_specs=in_specs, out_shape=[_sds((n_layers, r, c), F32)] * 4,
                out_specs=[layered] * 4, body=body, aliases=aliases, prefetch=chip, n_blocks=r // tr)


def _run_job(name, job, after):
    def spec(entry):
        if entry is None:
            return _ANY
        shape, index = entry
        return pl.BlockSpec(shape, lambda blk, pre, index=index: index(blk, pre))

    def body(pre_ref, *refs):
        job.body(*refs)

    return _pcall(name, body, job.operands, [spec(e) for e in job.in_specs], job.out_shape,
                  [spec(e) for e in job.out_specs], grid=(job.n_blocks,), sem=("parallel",),
                  prefetch=[job.prefetch], after=after,
                  aliases={1 + i: o for i, o in job.aliases.items()})


def _small_sum(name, gathered, own, device, after=None):
    r, lanes = own.shape

    def body(dev_ref, g_ref, own_ref, out_ref):
        dev = dev_ref[0]
        mine = own_ref[...]
        total = jnp.where(dev == 0, mine, g_ref[0])
        for d in range(1, N_DEV):
            total = total + jnp.where(dev == d, mine, g_ref[d])
        out_ref[...] = total

    return _pcall(name, body, [gathered, own],
                  [pl.BlockSpec((N_DEV, r, lanes), lambda i, dev_ref: (0, 0, 0)),
                   pl.BlockSpec((r, lanes), lambda i, dev_ref: (0, 0))],
                  _sds((r, lanes), F32), pl.BlockSpec((r, lanes), lambda i, dev_ref: (0, 0)),
                  grid=(1,), sem=("arbitrary",), prefetch=[device], after=after)


def _pack(arrays):
    return jnp.concatenate([a.reshape(-1, 128) for a in arrays], axis=0)


def _unpack(packed, shapes):
    out, row = [], 0
    for shape in shapes:
        rows = math.prod(shape) // 128
        out.append(packed[row:row + rows].reshape(shape))
        row += rows
    return out


class _Order:
    def __init__(self):
        self.last = None

    def __call__(self, fn, *args, **kwargs):
        out = fn(*args, after=self.last, **kwargs)
        self.last = out[0] if isinstance(out, (list, tuple)) else out
        return out


def kernel(x, a_w_in, a_ln_g, a_ln_b, a_w_s, a_b_s, a_w_out, b_w_in, b_w_grp, b_scale, b_w_out, norm_mix, norm_mlp, mlp_w1, mlp_w2, final_norm, loss_target, m_a_w_in, m_a_ln_g, m_a_ln_b, m_a_w_s, m_a_b_s, m_a_w_out, m_b_w_in, m_b_w_grp, m_b_scale, m_b_w_out, m_norm_mix, m_norm_mlp, m_mlp_w1, m_mlp_w2, m_final_norm, v_a_w_in, v_a_ln_g, v_a_ln_b, v_a_w_s, v_a_b_s, v_a_w_out, v_b_w_in, v_b_w_grp, v_b_scale, v_b_w_out, v_norm_mix, v_norm_mlp, v_mlp_w1, v_mlp_w2, v_final_norm):
    s, d = x.shape[1], x.shape[2]
    depth = mlp_w1.shape[0]
    a_slab = a_w_in.shape[2]
    ff_slab = mlp_w1.shape[2]
    ff_rows = mlp_w2.shape[1]
    bh = b_w_grp.shape[3]
    my_x, my_y, my_c = _position()
    core = jnp.reshape(my_c, (1,)).astype(jnp.int32)
    chip = jnp.reshape(2 * my_x + my_y, (1,)).astype(jnp.int32)
    device = _slot(my_x, my_y, my_c)
    run = _Order()

    w1_b, w2_b = mlp_w1.astype(BF16), mlp_w2.astype(BF16)
    shards = [a_w_in[0].astype(BF16), a_w_out[0].astype(BF16), b_scale,
              w1_b[0], w2_b[0],
              b_w_in[0].astype(BF16), b_w_grp[0].astype(BF16), b_w_out[0].astype(BF16),
              w1_b[1], w2_b[1]]
    groups = [[0], [1, 2], [3], [4], [5, 6, 7], [8], [9]]
    start_with = {1: [2, 3], 2: [4], 3: [5], 4: [6]}
    hop1, hop2 = {}, {}
    _, hop1[0], token = _gather_step("weights_group0_hop1", [], [shards[t] for t in groups[0]])
    _, hop1[1], token = _gather_step("weights_group1_hop1", [], [shards[t] for t in groups[1]], token)
    run.last = token

    def advance(g):
        if g not in hop1:
            return
        ahead = start_with.get(g, [])
        fresh = [shards[t] for a in ahead for t in groups[a]]
        hop2[g], started, tok = _gather_step(f"weights_group{g}_hop2", hop1.pop(g), fresh, run.last)
        for a in ahead:
            hop1[a], started = started[:len(groups[a])], started[len(groups[a]):]
        run.last = tok

    def gathered(g):
        advance(g)
        if g == 0:
            advance(1)
        srcs, lands = _gather_wait(f"weights_group{g}_wait", hop2.pop(g), run.last)
        run.last = srcs[0]
        return run(_gather_finish, f"weights_group{g}_finish", srcs, lands)

    h0 = x[0]
    target = loss_target[0]
    ln_g, ln_b = a_ln_g, a_ln_b
    w_s = a_w_s[0]
    b_s_col = a_b_s[0][:, :, None]
    nmix = [norm_mix[l][None, :] for l in range(depth)]
    nmlp = [norm_mlp[l][None, :] for l in range(depth)]

    def mlp_forward(l, h, up_group):
        hn = run(_rms_fwd, f"mlp{l}_norm", h, nmlp[l])
        (w1,) = gathered(up_group)
        advance(up_group + 1)
        act, act_sq = run(_mm_nn, f"mlp{l}_up", hn, w1,
                          lambda acc: (jnp.maximum(acc, 0.0), jnp.square(jnp.maximum(acc, 0.0))),
                          (BF16, BF16), slab=True)
        (w2,) = gathered(up_group + 1)
        advance(up_group + 2)
        w2 = w2.reshape(-1, d)
        (h_out,) = run(_mm_nn, f"mlp{l}_down", act_sq, w2, lambda acc, res: (acc + res,), (F32,),
                       extras=(h,), extra_kinds=("tile",))
        return h_out, (h, hn, act, act_sq, w1, w2)

    scattered = []

    def scatter_partials(name, partials, specs):
        in_flight, tok = _scatter_start(name + "_scatter_start", partials)
        run.last = tok
        scattered.append((name, in_flight, specs))

    weights = {"a_w_in": (a_w_in, m_a_w_in, v_a_w_in), "a_w_out": (a_w_out, m_a_w_out, v_a_w_out),
               "b_w_in": (b_w_in, m_b_w_in, v_b_w_in), "b_w_grp": (b_w_grp, m_b_w_grp, v_b_w_grp),
               "b_w_out": (b_w_out, m_b_w_out, v_b_w_out), "mlp_w1": (mlp_w1, m_mlp_w1, v_mlp_w1),
               "mlp_w2": (mlp_w2, m_mlp_w2, v_mlp_w2)}
    results = {}

    def reduce_jobs(group):
        name, in_flight, specs = scattered.pop(next(i for i, g in enumerate(scattered) if g[0] == group))
        partials, lands = _scatter_wait(name + "_scatter_wait", in_flight, run.last)
        run.last = lands[0]
        jobs = []
        for t, (wname, layer) in enumerate(specs):
            w, m, v = weights[wname]
            shape = (w.shape[0],) + partials[t].shape[1:]
            jobs.append((wname, lambda n_blocks, t=t, w=w, m=m, v=v, layer=layer, wname=wname, shape=shape:
                         _reduce_adam_job(partials[t], lands[t], chip, w.reshape(shape), m.reshape(shape),
                                          v.reshape(shape), layer, results.get(wname), n_blocks)))
        return name, jobs

    def hosted(fn, *args, ride=None, **kwargs):
        if ride is None:
            return run(fn, *args, **kwargs)
        _, ((wname, job),) = reduce_jobs(ride)
        out = run(fn, *args, rider=job, **kwargs)
        results[wname] = out[-4:]
        return out[0] if fn is _mm_tn_half else out[:-4]

    def weight_grad(name, a, b, by_rows, block, between, ride_other=None, ride_own=None, tn=1024):
        other = hosted(_mm_tn_half, name + "_other", a, b, core, False, by_rows, block, tn=tn, ride=ride_other)
        sent, tok = _sibling_start(name + "_sibling_start", [other])
        run.last = tok
        middle = between()
        (recv,) = _sibling_wait(name + "_sibling_wait", sent, run.last)
        run.last = recv
        own = hosted(_mm_tn_half, name + "_own", a, b, core, True, by_rows, block, recv=recv, tn=tn, ride=ride_own)
        return own, middle

    def mlp_backward(l, saved, dh, dhb, ride_down=None, ride_up=None):
        h, hn, act, act_sq, w1, w2 = saved
        part_w2, (dpre,) = weight_grad(
            f"mlp{l}_down_dw", act_sq, dhb, True, ff_rows,
            lambda: hosted(_mm_nt, f"mlp{l}_down_dx", dhb, w2, lambda acc, a: (2.0 * a.astype(F32) * acc,),
                           (BF16,), extras=(act,), extra_kinds=("tile",), ride=ride_down))
        scatter_partials(f"mlp{l}_down_grads", [part_w2], [("mlp_w2", l)])
        part_w1, (dhn,) = weight_grad(
            f"mlp{l}_up_dw", hn, dpre, False, ff_slab,
            lambda: run(_mm_nt, f"mlp{l}_up_dx", dpre, w1, lambda acc: (acc,), (F32,), slab=True),
            ride_other=ride_up, tn=1024 if ride_up is None else 512)
        scatter_partials(f"mlp{l}_up_grads", [part_w1], [("mlp_w1", l)])
        dh, dhb, g_norm = run(_rms_bwd, f"mlp{l}_norm_bwd", dhn, h, nmlp[l], dh)
        return dh, dhb, g_norm

    hn0 = run(_rms_fwd, "mix0_norm", h0, nmix[0])
    (wa_in,) = gathered(0)
    (pre,) = run(_mm_nn, "mixa_in", hn0, wa_in, lambda acc: (acc,), (F32,), slab=True)
    wa_out, scale = gathered(1)
    wa_out, scale = wa_out.reshape(d, d), scale.reshape(1, d)
    gated = run(_amix_fwd, "mixa_gate", pre, ln_g, ln_b, w_s, b_s_col)
    advance(2)
    (h1,) = run(_mm_nn, "mixa_out", gated, wa_out, lambda acc, res: (acc + res,), (F32,),
                extras=(h0,), extra_kinds=("tile",))
    h2, saved_mlp0 = mlp_forward(0, h1, 2)
    hn2 = run(_rms_fwd, "mix1_norm", h2, nmix[1])
    wb_in, wb_grp, wb_out = gathered(4)
    advance(5)
    wb_in, wb_out = wb_in.reshape(d, d), wb_out.reshape(d, d)
    wb_grp = jnp.transpose(wb_grp, (1, 0, 2, 3)).reshape(B_GROUPS, bh, bh)
    (vb,) = run(_mm_nn, "mixb_in", hn2, wb_in, lambda acc: (acc,), (F32,))
    pooled = run(_pool, "mixb_pool", vb, backward=False)
    tm = _tile(s, 1024)
    grp_tile = pl.BlockSpec((tm, bh), lambda i, j, k: (i, j))
    grp_weight = pl.BlockSpec((None, bh, bh), lambda i, j, k: (j, 0, 0))
    mixed, mixed_scaled = run(
        _matmul, "mixb_grp", pooled, wb_grp, NN, (s // tm, B_GROUPS, 1), grp_tile, grp_weight,
        [_sds((s, d), BF16), _sds((s, d), BF16)], [grp_tile] * 2,
        (tm, bh), lambda acc, sc: (acc, acc * sc), (scale,), [pl.BlockSpec((1, bh), lambda i, j, k: (0, j))])
    (h3,) = run(_mm_nn, "mixb_out", mixed_scaled, wb_out, lambda acc, res: (acc + res,), (F32,),
                extras=(h2,), extra_kinds=("tile",))
    h4, saved_mlp1 = mlp_forward(1, h3, 5)
    dh, dhb, g_final, loss_part = run(_loss_head, "loss_head", h4, final_norm[None, :], target)

    dh, dhb, g_nmlp1 = mlp_backward(1, saved_mlp1, dh, dhb)
    tks = _tile(s, 1024)
    grp_rows = pl.BlockSpec((tks, bh), lambda i, j, k: (k, j))

    def mixb_middle():
        dms_scaled, dms_mixed = run(
            _mm_nt, "mixb_out_dx", dhb, wb_out,
            lambda acc, sc, mx: (acc * sc, acc * mx.astype(F32)), (BF16, F32),
            extras=(scale, mixed), extra_kinds=("row", "tile"))
        g_scale = run(_colsum, "mixb_scale_dw", dms_mixed)
        (g_wb_grp,) = run(
            _matmul, "mixb_grp_dw", pooled, dms_scaled, TN, (1, B_GROUPS, s // tks), grp_rows, grp_rows,
            [_sds((B_GROUPS, bh, bh), BF16)], [grp_weight], (bh, bh), lambda acc: (acc,))
        (dpooled,) = run(
            _matmul, "mixb_grp_dx", dms_scaled, wb_grp, NT, (s // tm, B_GROUPS, 1), grp_tile, grp_weight,
            [_sds((s, d), F32)], [grp_tile], (tm, bh), lambda acc: (acc,))
        return g_scale, g_wb_grp, run(_pool, "mixb_pool_bwd", dpooled, backward=True)

    part_wb_out, (g_scale, g_wb_grp, dvb) = weight_grad("mixb_out_dw", mixed_scaled, dhb, True, d // N_DEV,
                                                        mixb_middle)
    part_wb_in, (dhn2,) = weight_grad(
        "mixb_in_dw", hn2, dvb, True, d // N_DEV,
        lambda: run(_mm_nt, "mixb_in_dx", dvb, wb_in, lambda acc: (acc,), (F32,)))
    grp_full = jnp.transpose(g_wb_grp.reshape(B_GROUPS, N_DEV, bh // N_DEV, bh), (1, 0, 2, 3))
    grp_full = grp_full.reshape(N_CHIPS, 2, B_GROUPS * bh // N_DEV, bh)
    (grp_sibling,) = run(_exchange_sibling, "mixb_grp_dw_to_sibling", [grp_full])
    part_wb_grp = run(_add_sibling, "mixb_grp_dw_add_sibling", grp_full, grp_sibling, core)
    scatter_partials("mixb_grads", [part_wb_out, part_wb_grp, part_wb_in],
                     [("b_w_out", 0), ("b_w_grp", 0), ("b_w_in", 0)])
    dh, dhb, g_nmix1 = run(_rms_bwd, "mix1_norm_bwd", dhn2, h2, nmix[1], dh)
    dh, dhb, g_nmlp0 = mlp_backward(0, saved_mlp0, dh, dhb, "mlp1_down_grads", "mlp1_up_grads")
    def mixa_middle():
        (dgated,) = run(_mm_nt, "mixa_out_dx", dhb, wa_out, lambda acc: (acc,), (F32,))
        return run(_amix_bwd, "mixa_gate_bwd", pre, dgated, ln_g, ln_b, w_s, b_s_col)

    part_wa_out, (dpre, g_ln_g, g_ln_b, g_w_s, g_b_s) = weight_grad("mixa_out_dw", gated, dhb, True, d // N_DEV,
                                                                     mixa_middle)
    part_wa_in, (dhn0,) = weight_grad(
        "mixa_in_dw", hn0, dpre, False, a_slab,
        lambda: run(_mm_nt, "mixa_in_dx", dpre, wa_in, lambda acc: (acc,), (F32,), slab=True),
        ride_other="mlp0_down_grads", ride_own="mlp0_up_grads", tn=a_slab // 2)
    scatter_partials("mixa_grads", [part_wa_in, part_wa_out], [("a_w_in", 0), ("a_w_out", 0)])
    grad_x, _, g_nmix0 = run(_rms_bwd, "mix0_norm_bwd", dhn0, h0, nmix[0], dh)

    g_norm_mix = jnp.concatenate([g_nmix0, g_nmix1], axis=0)
    g_norm_mlp = jnp.concatenate([g_nmlp0, g_nmlp1], axis=0)
    loss_row = jnp.pad(loss_part, ((0, 0), (0, 127)))
    small_parts = [g_ln_g, g_ln_b, g_w_s, g_b_s, g_norm_mix, g_norm_mlp, g_final, g_scale, loss_row]
    packed = _pack(small_parts)
    small_sent, tok = _split_start("small_grads_start", [packed], [lax.empty((N_DEV,) + packed.shape, F32)],
                                   _small_copies, N_DEV - 1)
    run.last = tok

    def finish_group(group):
        name, jobs = reduce_jobs(group)
        for t, (wname, job) in enumerate(jobs):
            results[wname] = run(_run_job, f"{name}_reduce_adam_{t}", job(None))

    for group in [g[0] for g in scattered[:-1]]:
        finish_group(group)
    own_packed, small_gathered = _split_wait("small_grads_wait", small_sent, _small_copies, run.last)
    run.last = small_gathered[0]
    small_sum = run(_small_sum, "small_grads_sum", small_gathered[0], own_packed[0],
                    jnp.reshape(device, (1,)).astype(jnp.int32))
    sg = _unpack(small_sum, [a_ln_g.shape, a_ln_b.shape, a_w_s.shape, a_b_s.shape, norm_mix.shape,
                             norm_mlp.shape, final_norm.shape, (1, d), (1, 128)])
    loss = sg.pop()[0, 0]
    shard = b_scale.shape[1]
    sg[7] = lax.dynamic_slice(sg[7], (0, device * shard), (1, shard))
    small_w = [a_ln_g, a_ln_b, a_w_s, a_b_s, norm_mix, norm_mlp, final_norm, b_scale]
    small_m = [m_a_ln_g, m_a_ln_b, m_a_w_s, m_a_b_s, m_norm_mix, m_norm_mlp, m_final_norm, m_b_scale]
    small_v = [v_a_ln_g, v_a_ln_b, v_a_w_s, v_a_b_s, v_norm_mix, v_norm_mlp, v_final_norm, v_b_scale]
    small_out = run(_adam_rows, "small_adam", _pack(sg), _pack(small_w), _pack(small_m), _pack(small_v))
    shapes = [w.shape for w in small_w]
    small_res = [sg] + [_unpack(o, shapes) for o in small_out]

    finish_group(scattered[-1][0])
    big = {wname: [o.reshape(weights[wname][0].shape) for o in outs] for wname, outs in results.items()}

    def leaf(o):
        return (big["a_w_in"][o], small_res[o][0], small_res[o][1], small_res[o][2], small_res[o][3],
                big["a_w_out"][o], big["b_w_in"][o], big["b_w_grp"][o], small_res[o][7], big["b_w_out"][o],
                small_res[o][4], small_res[o][5], big["mlp_w1"][o], big["mlp_w2"][o], small_res[o][6])

    return (loss, grad_x[None], *leaf(0), *leaf(1), *leaf(2), *leaf(3))
```

```python
import math

import jax
import jax.numpy as jnp
from jax import lax
from jax.experimental import pallas as pl
from jax.experimental.pallas import tpu as pltpu

F32 = jnp.float32
BF16 = jnp.bfloat16
MESH = pl.DeviceIdType.MESH

N_DEV = 8
N_CHIPS = 4
CHUNK = 128
A_GROUPS = 8
B_WINDOWS = (2, 4, 8, 16)
B_GROUPS = len(B_WINDOWS)
EPS = 1e-6
ADAM_LR = 0.001
ADAM_B1 = 0.9
ADAM_B2 = 0.999
ADAM_EPS = 1e-08
ADAM_WD = 0.01
ADAM_STEP = 10

VMEM_LIMIT = 48 * 1024 * 1024

NN = (((1,), (0,)), ((), ()))
NT = (((1,), (1,)), ((), ()))
TN = (((0,), (0,)), ((), ()))

_ANY = pl.BlockSpec(memory_space=pl.ANY)
_HBM = pl.BlockSpec(memory_space=pltpu.HBM)
_SEM = pl.BlockSpec(memory_space=pltpu.SEMAPHORE)
_EFFECT = pltpu.SideEffectType.DATAFLOW_SIDE_EFFECTING


def _tile(n, pref):
    return pref if n % pref == 0 else n


def _sds(shape, dtype):
    return jax.ShapeDtypeStruct(shape, dtype)


def _pcall(name, body, operands, in_specs, out_shape, out_specs, *, grid=None, sem=None, scratch=(),
           prefetch=(), after=None, aliases=None):
    after = [] if after is None else [after]
    n_lead = len(prefetch) + len(operands)
    n_after = len(after)

    def wrapped(*refs):
        body(*refs[:n_lead], *refs[n_lead + n_after:])

    in_specs = list(in_specs) + [_ANY] * n_after
    params = pltpu.CompilerParams(vmem_limit_bytes=VMEM_LIMIT) if sem is None else \
        pltpu.CompilerParams(dimension_semantics=sem, vmem_limit_bytes=VMEM_LIMIT)
    kwargs = dict(out_shape=out_shape, scratch_shapes=list(scratch), compiler_params=params, name=name,
                  input_output_aliases=aliases or {})
    if prefetch:
        kwargs["grid_spec"] = pltpu.PrefetchScalarGridSpec(
            num_scalar_prefetch=len(prefetch), grid=grid, in_specs=in_specs, out_specs=out_specs,
            scratch_shapes=list(scratch))
        kwargs.pop("scratch_shapes")
    else:
        kwargs.update(in_specs=in_specs, out_specs=out_specs)
        if grid is not None:
            kwargs["grid"] = grid
    return pl.pallas_call(wrapped, **kwargs)(*prefetch, *operands, *after)


def _matmul(name, a, b, dims, grid, a_spec, b_spec, out_shape, out_specs, acc_shape,
            epilogue, extras=(), extra_specs=(), after=None, prefetch=(), b_parts=1):
    nk = grid[2]
    n_extra = len(extras)
    n_out = len(out_shape)
    n_pre = len(prefetch)

    def body(*refs):
        refs = refs[n_pre:]
        a_ref, b_ref = refs[0], refs[1]
        extra_refs = refs[2:2 + n_extra]
        out_refs = refs[2 + n_extra:2 + n_extra + n_out]

        def finish(acc):
            outs = epilogue(acc, *[r[...] for r in extra_refs])
            for o_ref, o in zip(out_refs, outs):
                o_ref[...] = o.astype(o_ref.dtype)

        def product():
            if b_parts == 1:
                return lax.dot_general(a_ref[...], b_ref[...], dims, preferred_element_type=F32)
            width = b_ref.shape[2]
            total = None
            for p in range(b_parts):
                part = lax.dot_general(a_ref[:, p * width:(p + 1) * width], b_ref[p], dims,
                                       preferred_element_type=F32)
                total = part if total is None else total + part
            return total

        if nk == 1:
            finish(product())
        else:
            acc_ref = refs[-1]
            k = pl.program_id(2)

            @pl.when(k == 0)
            def _():
                acc_ref[...] = product()

            if nk > 2:
                @pl.when(jnp.logical_and(k > 0, k < nk - 1))
                def _():
                    acc_ref[...] += product()

            @pl.when(k == nk - 1)
            def _():
                finish(acc_ref[...] + product())

    scratch = [] if nk == 1 else [pltpu.VMEM(acc_shape, F32)]
    return _pcall(name, body, [a, b, *extras], [a_spec, b_spec, *extra_specs], out_shape, out_specs,
                  grid=grid, sem=("parallel", "parallel", "arbitrary"), scratch=scratch, after=after,
                  prefetch=prefetch)


def _mm_nn(name, a, b, epilogue, out_dtypes, extras=(), extra_kinds=(), slab=False, after=None,
           tm=1024, tn=1024, tk=2048):
    m, kd = a.shape
    if slab:
        n_slab, _, w = b.shape
        n = n_slab * w
        tn = _tile(w, min(tn, w))
        per = w // tn
        tk = _tile(kd, tk)
        b_spec = pl.BlockSpec((None, tk, tn), lambda i, j, k: (j // per, k, j % per))
    else:
        n = b.shape[1]
        tn = _tile(n, tn)
        tk = _tile(kd, tk)
        b_spec = pl.BlockSpec((tk, tn), lambda i, j, k: (k, j))
    tm = _tile(m, tm)
    grid = (m // tm, n // tn, kd // tk)
    a_spec = pl.BlockSpec((tm, tk), lambda i, j, k: (i, k))
    tile_spec = pl.BlockSpec((tm, tn), lambda i, j, k: (i, j))
    row_spec = pl.BlockSpec((1, tn), lambda i, j, k: (0, j))
    extra_specs = [tile_spec if kind == "tile" else row_spec for kind in extra_kinds]
    return _matmul(name, a, b, NN, grid, a_spec, b_spec,
                   [_sds((m, n), d) for d in out_dtypes], [tile_spec for _ in out_dtypes],
                   (tm, tn), epilogue, extras, extra_specs, after=after)


def _mm_nt(name, a, b, epilogue, out_dtypes, extras=(), extra_kinds=(), slab=False, after=None,
           tm=1024, tn=1024, tk=2048):
    m, kd = a.shape
    parts = 1
    if slab:
        n_slab, n, w = b.shape
        tn = _tile(n, tn)
        if tk > w and tk % w == 0 and n_slab % (tk // w) == 0:
            parts = tk // w
            b_spec = pl.BlockSpec((parts, tn, w), lambda i, j, k, *_: (k, j, 0))
        else:
            tk = _tile(w, min(tk, w))
            per = w // tk
            b_spec = pl.BlockSpec((None, tn, tk), lambda i, j, k, *_: (k // per, j, k % per))
    else:
        n = b.shape[0]
        tn = _tile(n, tn)
        tk = _tile(kd, tk)
        b_spec = pl.BlockSpec((tn, tk), lambda i, j, k, *_: (j, k))
    tm = _tile(m, tm)
    grid = (m // tm, n // tn, kd // tk)
    a_spec = pl.BlockSpec((tm, tk), lambda i, j, k, *_: (i, k))
    tile_spec = pl.BlockSpec((tm, tn), lambda i, j, k, *_: (i, j))
    row_spec = pl.BlockSpec((1, tn), lambda i, j, k, *_: (0, j))
    extra_specs = [tile_spec if kind == "tile" else row_spec for kind in extra_kinds]
    return _matmul(name, a, b, NT, grid, a_spec, b_spec,
                   [_sds((m, n), d) for d in out_dtypes], [tile_spec for _ in out_dtypes],
                   (tm, tn), epilogue, extras, extra_specs, after=after, b_parts=parts)


def _mm_tn_half(name, a, b, core, own, by_rows, block, recv=None, after=None, tm=1024, tn=1024, tk=2048):
    s, m = a.shape
    n = b.shape[1]
    tk = _tile(s, tk)

    def owner(chip, core_ref):
        return 2 * chip + (core_ref[0] if own else 1 - core_ref[0])

    if by_rows:
        r, c = block, n
        tm, tn = _tile(r, min(tm, r)), _tile(c, tn)
        per = r // tm
        grid = (N_CHIPS * per, c // tn, s // tk)
        a_spec = pl.BlockSpec((tk, tm), lambda i, j, k, cr, *_: (k, owner(i // per, cr) * per + i % per))
        b_spec = pl.BlockSpec((tk, tn), lambda i, j, k, cr, *_: (k, j))
        o_spec = pl.BlockSpec((None, tm, tn), lambda i, j, k, cr, *_: (i // per, i % per, j))
    else:
        r, c = m, block
        tm, tn = _tile(r, tm), _tile(c, min(tn, c))
        per = c // tn
        grid = (r // tm, N_CHIPS * per, s // tk)
        a_spec = pl.BlockSpec((tk, tm), lambda i, j, k, cr, *_: (k, i))
        b_spec = pl.BlockSpec((tk, tn), lambda i, j, k, cr, *_: (k, owner(j // per, cr) * per + j % per))
        o_spec = pl.BlockSpec((None, tm, tn), lambda i, j, k, cr, *_: (j // per, i, j % per))
    if recv is None:
        extras, epilogue = (), lambda acc: (acc,)
    else:
        extras, epilogue = (recv,), lambda acc, other: (acc + other.astype(F32),)
    return _matmul(name, a, b, TN, grid, a_spec, b_spec, [_sds((N_CHIPS, r, c), BF16)], [o_spec], (tm, tn),
                   epilogue, extras, [o_spec] * len(extras), after=after, prefetch=[core])[0]


def _rms_fwd(name, h, g, after=None):
    s, d = h.shape
    tr = _tile(s, 256)

    def body(h_ref, g_ref, o_ref):
        x = h_ref[...]
        r = lax.rsqrt(jnp.mean(x * x, axis=-1, keepdims=True) + EPS)
        o_ref[...] = (x * r * g_ref[...]).astype(o_ref.dtype)

    row = pl.BlockSpec((tr, d), lambda i: (i, 0))
    vec = pl.BlockSpec((1, d), lambda i: (0, 0))
    return _pcall(name, body, [h, g], [row, vec], _sds((s, d), BF16), row, grid=(s // tr,),
                  sem=("parallel",), after=after)


def _accumulate(ref, part, step):
    @pl.when(step == 0)
    def _():
        ref[...] = part

    @pl.when(step > 0)
    def _():
        ref[...] += part


def _rms_bwd(name, dhn, h, g, dres, after=None):
    s, d = h.shape
    tr = _tile(s, 256)

    def body(dhn_ref, h_ref, g_ref, dres_ref, dh_ref, dhb_ref, gp_ref):
        x = h_ref[...]
        r = lax.rsqrt(jnp.mean(x * x, axis=-1, keepdims=True) + EPS)
        n = x * r
        dy = dhn_ref[...]
        dn = dy * g_ref[...]
        dh = dres_ref[...] + r * (dn - n * jnp.mean(dn * n, axis=-1, keepdims=True))
        dh_ref[...] = dh
        dhb_ref[...] = dh.astype(BF16)
        _accumulate(gp_ref, jnp.sum(dy * n, axis=0, keepdims=True), pl.program_id(0))

    row = pl.BlockSpec((tr, d), lambda i: (i, 0))
    vec = pl.BlockSpec((1, d), lambda i: (0, 0))
    return _pcall(name, body, [dhn, h, g, dres], [row, row, vec, row],
                  [_sds((s, d), F32), _sds((s, d), BF16), _sds((1, d), F32)], [row, row, vec],
                  grid=(s // tr,), sem=("arbitrary",), after=after)


def _loss_head(name, h, g, target, after=None):
    s, d = h.shape
    tr = _tile(s, 256)

    def body(h_ref, g_ref, t_ref, dh_ref, dhb_ref, gp_ref, loss_ref):
        x = h_ref[...]
        gg = g_ref[...]
        r = lax.rsqrt(jnp.mean(x * x, axis=-1, keepdims=True) + EPS)
        n = x * r
        e = n * gg - t_ref[...]
        dy = e * (1.0 / d)
        dn = dy * gg
        dh = r * (dn - n * jnp.mean(dn * n, axis=-1, keepdims=True))
        dh_ref[...] = dh
        dhb_ref[...] = dh.astype(BF16)
        step = pl.program_id(0)
        _accumulate(gp_ref, jnp.sum(dy * n, axis=0, keepdims=True), step)
        row_loss = jnp.mean(e * e, axis=-1, keepdims=True)
        _accumulate(loss_ref, 0.5 * jnp.sum(row_loss, axis=0, keepdims=True), step)

    row = pl.BlockSpec((tr, d), lambda i: (i, 0))
    vec = pl.BlockSpec((1, d), lambda i: (0, 0))
    one = pl.BlockSpec((1, 1), lambda i: (0, 0))
    return _pcall(name, body, [h, g, target], [row, vec, row],
                  [_sds((s, d), F32), _sds((s, d), BF16), _sds((1, d), F32), _sds((1, 1), F32)],
                  [row, row, vec, one], grid=(s // tr,), sem=("arbitrary",), after=after)


_SQRT_HALF = math.sqrt(0.5)
_INV_SQRT_2PI = 1.0 / math.sqrt(2.0 * math.pi)


def _gelu(x):
    return 0.5 * x * (1.0 + lax.erf(x * _SQRT_HALF))


def _gelu_grad(x):
    return 0.5 * (1.0 + lax.erf(x * _SQRT_HALF)) + x * jnp.exp(-0.5 * x * x) * _INV_SQRT_2PI


def _causal_mask():
    row = lax.broadcasted_iota(jnp.int32, (CHUNK, CHUNK), 0)
    col = lax.broadcasted_iota(jnp.int32, (CHUNK, CHUNK), 1)
    return row >= col


def _row_sum(x):
    return jnp.sum(x, axis=-1, keepdims=True)


def _masked_spatial(ws_ref, grp):
    return jnp.where(_causal_mask(), ws_ref[grp], 0.0).astype(BF16)


def _layernorm_stats(pre_ref, v_scr, w, head):
    total = jnp.zeros((CHUNK, 1), F32)
    for grp in range(A_GROUPS):
        v = _gelu(pre_ref[:, w + grp * head:w + (grp + 1) * head])
        v_scr[:, grp * head:(grp + 1) * head] = v
        total = total + _row_sum(v)
    mu = total * (1.0 / w)
    square = jnp.zeros((CHUNK, 1), F32)
    for grp in range(A_GROUPS):
        xc = v_scr[:, grp * head:(grp + 1) * head] - mu
        square = square + _row_sum(xc * xc)
    return mu, lax.rsqrt(square * (1.0 / w) + EPS)


def _amix_fwd(name, pre, ln_g, ln_b, w_s, b_s_col, after=None):
    s, w2 = pre.shape
    w = w2 // 2
    head = w // A_GROUPS

    def body(pre_ref, g_ref, b_ref, ws_ref, bs_ref, o_ref, v_scr):
        mu, rstd = _layernorm_stats(pre_ref, v_scr, w, head)
        for grp in range(A_GROUPS):
            cols = slice(grp * head, (grp + 1) * head)
            vhat = (v_scr[:, cols] - mu) * rstd
            vn = (vhat * g_ref[:, cols] + b_ref[:, cols]).astype(BF16)
            sg = jnp.dot(_masked_spatial(ws_ref, grp), vn, preferred_element_type=F32) + bs_ref[grp]
            o_ref[:, cols] = (_gelu(pre_ref[:, cols]) * sg).astype(o_ref.dtype)

    vec = pl.BlockSpec((1, w), lambda i: (0, 0))
    return _pcall(
        name, body, [pre, ln_g, ln_b, w_s, b_s_col],
        [pl.BlockSpec((CHUNK, w2), lambda i: (i, 0)), vec, vec,
         pl.BlockSpec((A_GROUPS, CHUNK, CHUNK), lambda i: (0, 0, 0)),
         pl.BlockSpec((A_GROUPS, CHUNK, 1), lambda i: (0, 0, 0))],
        _sds((s, w), BF16), pl.BlockSpec((CHUNK, w), lambda i: (i, 0)),
        grid=(s // CHUNK,), sem=("parallel",), scratch=[pltpu.VMEM((CHUNK, w), F32)], after=after)


def _amix_bwd(name, pre, dgated, ln_g, ln_b, w_s, b_s_col, after=None):
    s, w2 = pre.shape
    w = w2 // 2
    head = w // A_GROUPS

    def body(pre_ref, dg_ref, g_ref, b_ref, ws_ref, bs_ref, dpre_ref, glg_ref, glb_ref, gws_ref, gbs_ref,
             v_scr, dvn_scr):
        @pl.when(pl.program_id(0) == 0)
        def _():
            for ref in (glg_ref, glb_ref, gws_ref, gbs_ref):
                ref[...] = jnp.zeros_like(ref)

        mu, rstd = _layernorm_stats(pre_ref, v_scr, w, head)
        mask = _causal_mask()
        sum_dvhat = jnp.zeros((CHUNK, 1), F32)
        sum_dvhat_vhat = jnp.zeros((CHUNK, 1), F32)
        for grp in range(A_GROUPS):
            cols = slice(grp * head, (grp + 1) * head)
            vhat = (v_scr[:, cols] - mu) * rstd
            gain = g_ref[:, cols]
            vn = (vhat * gain + b_ref[:, cols]).astype(BF16)
            wm = _masked_spatial(ws_ref, grp)
            pre_u = pre_ref[:, cols]
            dgated = dg_ref[:, cols]
            ds = dgated * _gelu(pre_u)
            dsb = ds.astype(BF16)
            sg = jnp.dot(wm, vn, preferred_element_type=F32) + bs_ref[grp]
            dpre_ref[:, cols] = (dgated * sg * _gelu_grad(pre_u)).astype(dpre_ref.dtype)
            gws = lax.dot_general(dsb, vn, NT, preferred_element_type=F32)
            gws_ref[grp] += jnp.where(mask, gws, 0.0)
            gbs_ref[grp] += _row_sum(ds)
            dvn = lax.dot_general(wm, dsb, TN, preferred_element_type=F32)
            dvn_scr[:, cols] = dvn
            glg_ref[:, cols] += jnp.sum(dvn * vhat, axis=0, keepdims=True)
            glb_ref[:, cols] += jnp.sum(dvn, axis=0, keepdims=True)
            dvhat = dvn * gain
            sum_dvhat = sum_dvhat + _row_sum(dvhat)
            sum_dvhat_vhat = sum_dvhat_vhat + _row_sum(dvhat * vhat)
        mean_dvhat = sum_dvhat * (1.0 / w)
        mean_dvhat_vhat = sum_dvhat_vhat * (1.0 / w)
        for grp in range(A_GROUPS):
            cols = slice(grp * head, (grp + 1) * head)
            vhat = (v_scr[:, cols] - mu) * rstd
            dvhat = dvn_scr[:, cols] * g_ref[:, cols]
            dv = rstd * (dvhat - mean_dvhat - vhat * mean_dvhat_vhat)
            pre_v = pre_ref[:, w + grp * head:w + (grp + 1) * head]
            dpre_ref[:, w + grp * head:w + (grp + 1) * head] = (dv * _gelu_grad(pre_v)).astype(dpre_ref.dtype)

    vec = pl.BlockSpec((1, w), lambda i: (0, 0))
    ws_spec = pl.BlockSpec((A_GROUPS, CHUNK, CHUNK), lambda i: (0, 0, 0))
    bs_spec = pl.BlockSpec((A_GROUPS, CHUNK, 1), lambda i: (0, 0, 0))
    return _pcall(
        name, body, [pre, dgated, ln_g, ln_b, w_s, b_s_col],
        [pl.BlockSpec((CHUNK, w2), lambda i: (i, 0)), pl.BlockSpec((CHUNK, w), lambda i: (i, 0)),
         vec, vec, ws_spec, bs_spec],
        [_sds((s, w2), BF16), _sds((1, w), F32), _sds((1, w), F32),
         _sds((A_GROUPS, CHUNK, CHUNK), F32), _sds((A_GROUPS, CHUNK, 1), F32)],
        [pl.BlockSpec((CHUNK, w2), lambda i: (i, 0)), vec, vec, ws_spec, bs_spec],
        grid=(s // CHUNK,), sem=("arbitrary",),
        scratch=[pltpu.VMEM((CHUNK, w), F32), pltpu.VMEM((CHUNK, w), F32)], after=after)


def _shift_rows(x, k, forward):
    n = x.shape[0]
    row = lax.broadcasted_iota(jnp.int32, x.shape, 0)
    if forward:
        return jnp.where(row >= k, pltpu.roll(x, k, 0), 0.0)
    return jnp.where(row < n - k, pltpu.roll(x, n - k, 0), 0.0)


def _window_sum(x, window, forward):
    k = 1
    while k < window:
        x = x + _shift_rows(x, k, forward)
        k *= 2
    return x


def _pool(name, v, backward, after=None):
    s, w = v.shape
    head = w // B_GROUPS
    lane = _tile(head, 128)

    def body(v_ref, o_ref):
        grp = pl.program_id(0)
        x = v_ref[...]
        t = lax.broadcasted_iota(jnp.int32, x.shape, 0)
        for idx, window in enumerate(B_WINDOWS):
            @pl.when(grp == idx)
            def _():
                inv_count = 1.0 / jnp.minimum(t + 1, window).astype(F32)
                if backward:
                    out = _window_sum(x * inv_count, window, False) - x
                else:
                    out = _window_sum(x, window, True) * inv_count - x
                o_ref[...] = out.astype(o_ref.dtype)

    per = head // lane
    spec = pl.BlockSpec((s, lane), lambda g, j: (0, g * per + j))
    return _pcall(name, body, [v], [spec], _sds((s, w), BF16), spec, grid=(B_GROUPS, per),
                  sem=("parallel", "parallel"), after=after)


def _colsum(name, a, after=None):
    s, d = a.shape
    tr = _tile(s, 256)

    def body(a_ref, o_ref):
        _accumulate(o_ref, jnp.sum(a_ref[...], axis=0, keepdims=True), pl.program_id(0))

    return _pcall(name, body, [a], [pl.BlockSpec((tr, d), lambda i: (i, 0))], _sds((1, d), F32),
                  pl.BlockSpec((1, d), lambda i: (0, 0)), grid=(s // tr,), sem=("arbitrary",), after=after)


def _adamw(w, g, m, v):
    m = ADAM_B1 * m + (1.0 - ADAM_B1) * g
    v = ADAM_B2 * v + (1.0 - ADAM_B2) * (g * g)
    m_hat = m / (1.0 - ADAM_B1 ** ADAM_STEP)
    v_hat = v / (1.0 - ADAM_B2 ** ADAM_STEP)
    delta = -ADAM_LR * (m_hat / (jnp.sqrt(v_hat) + ADAM_EPS) + ADAM_WD * w)
    return delta, m, v


def _adam_rows(name, g, w, m, v, after=None):
    r, c = g.shape
    tr = _tile(r, 256)

    def body(g_ref, w_ref, m_ref, v_ref, d_ref, nm_ref, nv_ref):
        d_ref[...], nm_ref[...], nv_ref[...] = _adamw(w_ref[...], g_ref[...], m_ref[...], v_ref[...])

    spec = pl.BlockSpec((tr, c), lambda i: (i, 0))
    return _pcall(name, body, [g, w, m, v], [spec] * 4, [_sds((r, c), F32)] * 3, [spec] * 3,
                  grid=(r // tr,), sem=("parallel",), after=after)


def _position():
    return lax.axis_index("x"), lax.axis_index("y"), lax.axis_index("c")


def _other_chips(x, y):
    return [(1 - x, y), (x, 1 - y), (1 - x, 1 - y)]


def _slot(px, py, pc):
    return 4 * px + 2 * py + pc


def _hbm(a):
    return pltpu.with_memory_space_constraint(a, pltpu.HBM)


def _hop1_copies(srcs, lands, send_sems, recv_sems):
    x, y, c = _position()
    peers = [(x, y, 1 - c), (1 - x, y, c), (x, 1 - y, c)]
    mine = _slot(x, y, c)
    return [[pltpu.make_async_remote_copy(
        src_ref=srcs[t], dst_ref=lands[t].at[mine], send_sem=send_sems[t].at[k], recv_sem=recv_sems[t].at[k],
        device_id=peer, device_id_type=MESH) for k, peer in enumerate(peers)] for t in range(len(srcs))]


def _hop2_copies(lands, send_sems, recv_sems):
    x, y, c = _position()
    routes = [(_slot(1 - x, y, c), (x, 1 - y, c)), (_slot(x, 1 - y, c), (1 - x, y, c))]
    out = []
    for t in range(len(lands)):
        rows = lands[t].shape[1]
        halves = [(0, rows // 2), (rows // 2, rows - rows // 2)]
        per_tensor = []
        for h, ((slot, peer), (start, size)) in enumerate(zip(routes, halves)):
            if size:
                block = lands[t].at[slot, pl.ds(start, size)]
                per_tensor.append(pltpu.make_async_remote_copy(
                    src_ref=block, dst_ref=block, send_sem=send_sems[t].at[h], recv_sem=recv_sems[t].at[h],
                    device_id=peer, device_id_type=MESH))
        for j, (slot, _) in enumerate(routes):
            block = lands[t].at[slot]
            per_tensor.append(pltpu.make_async_remote_copy(
                src_ref=block, dst_ref=block, send_sem=send_sems[t].at[2 + j], recv_sem=recv_sems[t].at[2 + j],
                device_id=(x, y, 1 - c), device_id_type=MESH))
        out.append(per_tensor)
    return out


def _split_start(name, srcs, lands, copies, n_sems, after=None):
    n = len(srcs)
    order = [] if after is None else [after]
    n_in = 2 * n + len(order)

    def body(*refs):
        for per_tensor in copies(refs[:n], refs[n:2 * n], refs[n_in:n_in + n], refs[n_in + n:n_in + 2 * n]):
            for cp in per_tensor:
                cp.start()
        refs[-1][...] = jnp.zeros_like(refs[-1])

    out_shape = ([pltpu.SemaphoreType.DMA((n_sems,)) for _ in range(2 * n)]
                 + [pltpu.HBM(a.shape, a.dtype) for a in list(srcs) + list(lands)]
                 + [_sds((8, 128), F32)])
    out = pl.pallas_call(
        body, name=name, out_shape=out_shape, in_specs=[_HBM] * (2 * n) + [_ANY] * len(order),
        out_specs=[_SEM] * (2 * n) + [_HBM] * (2 * n) + [pl.BlockSpec(memory_space=pltpu.VMEM)],
        input_output_aliases={i: 2 * n + i for i in range(2 * n)},
        compiler_params=pltpu.CompilerParams(has_side_effects=_EFFECT),
    )(*[_hbm(a) for a in srcs], *[_hbm(a) for a in lands], *order)
    return [(out[t], out[n + t], out[2 * n + t], out[3 * n + t]) for t in range(n)], out[-1]


def _split_wait(name, started, copies, after):
    n = len(started)

    def body(*refs):
        for per_tensor in copies(refs[:n], refs[n:2 * n], refs[2 * n:3 * n], refs[3 * n:4 * n]):
            for cp in per_tensor:
                cp.wait_send()
                cp.wait_recv()

    srcs = [e[2] for e in started]
    lands = [e[3] for e in started]
    out = pl.pallas_call(
        body, name=name, out_shape=[pltpu.HBM(a.shape, a.dtype) for a in srcs + lands],
        in_specs=[_HBM] * (2 * n) + [_SEM] * (2 * n) + [_ANY], out_specs=[_HBM] * (2 * n),
        input_output_aliases={i: i for i in range(2 * n)},
        compiler_params=pltpu.CompilerParams(has_side_effects=_EFFECT),
    )(*srcs, *lands, *[e[0] for e in started], *[e[1] for e in started], after)
    return out[:n], out[n:]


def _gather_step(name, arrived, fresh, after=None):
    n, m = len(arrived), len(fresh)
    order = [] if after is None else [after]
    fresh_lands = [lax.empty((N_DEV,) + s.shape, s.dtype) for s in fresh]
    buffers = [e[2] for e in arrived] + [e[3] for e in arrived] + list(fresh) + fresh_lands
    old_sems = [e[0] for e in arrived] + [e[1] for e in arrived]
    n_buf, n_old = len(buffers), len(old_sems)
    first_new = n_buf + n_old + len(order)

    def body(*refs):
        bufs, old = refs[:n_buf], refs[n_buf:n_buf + n_old]
        new = refs[first_new:first_new + 2 * n + 2 * m]
        for per_tensor in _hop1_copies(bufs[:n], bufs[n:2 * n], old[:n], old[n:]):
            for cp in per_tensor:
                cp.wait_send()
                cp.wait_recv()
        second = _hop2_copies(bufs[n:2 * n], new[:n], new[n:2 * n])
        first = _hop1_copies(bufs[2 * n:2 * n + m], bufs[2 * n + m:], new[2 * n:2 * n + m], new[2 * n + m:])
        for per_tensor in second + first:
            for cp in per_tensor:
                cp.start()
        refs[-1][...] = jnp.zeros_like(refs[-1])

    n_new = 2 * n + 2 * m
    out_shape = ([pltpu.SemaphoreType.DMA((4,)) for _ in range(2 * n)]
                 + [pltpu.SemaphoreType.DMA((3,)) for _ in range(2 * m)]
                 + [pltpu.HBM(a.shape, a.dtype) for a in buffers] + [_sds((8, 128), F32)])
    out = pl.pallas_call(
        body, name=name, out_shape=out_shape,
        in_specs=[_HBM] * n_buf + [_SEM] * n_old + [_ANY] * len(order),
        out_specs=[_SEM] * n_new + [_HBM] * n_buf + [pl.BlockSpec(memory_space=pltpu.VMEM)],
        input_output_aliases={i: n_new + i for i in range(n_buf)},
        compiler_params=pltpu.CompilerParams(has_side_effects=_EFFECT),
    )(*[_hbm(a) for a in buffers], *old_sems, *order)
    sems, bufs = out[:n_new], out[n_new:n_new + n_buf]
    second = [(sems[t], sems[n + t], bufs[t], bufs[n + t]) for t in range(n)]
    first = [(sems[2 * n + t], sems[2 * n + m + t], bufs[2 * n + t], bufs[2 * n + m + t]) for t in range(m)]
    return second, first, out[-1]


def _gather_wait(name, second, after):
    return _split_wait(name, second, lambda srcs, lands, send, recv: _hop2_copies(lands, send, recv), after)


def _sibling_copies(srcs, lands, send_sems, recv_sems):
    x, y, c = _position()
    return [[pltpu.make_async_remote_copy(
        src_ref=srcs[t], dst_ref=lands[t], send_sem=send_sems[t].at[0], recv_sem=recv_sems[t].at[0],
        device_id=(x, y, 1 - c), device_id_type=MESH)] for t in range(len(srcs))]


def _sibling_start(name, arrays):
    lands = [lax.empty(a.shape, a.dtype) for a in arrays]
    return _split_start(name, arrays, lands, _sibling_copies, 1)


def _sibling_wait(name, started, after):
    return _split_wait(name, started, _sibling_copies, after)[1]


def _small_copies(srcs, lands, send_sems, recv_sems):
    x, y, c = _position()
    mine = _slot(x, y, c)
    peers = [(x ^ ((k >> 2) & 1), y ^ ((k >> 1) & 1), c ^ (k & 1)) for k in range(1, N_DEV)]
    return [[pltpu.make_async_remote_copy(
        src_ref=srcs[t], dst_ref=lands[t].at[mine], send_sem=send_sems[t].at[k], recv_sem=recv_sems[t].at[k],
        device_id=peer, device_id_type=MESH) for k, peer in enumerate(peers)] for t in range(len(srcs))]


def _gather_finish(name, shards, lands, after):
    n = len(shards)

    def body(*refs):
        srcs, lands_in, outs = refs[:n], refs[n:2 * n], refs[2 * n:3 * n]
        send_sems, recv_sems, local_sems = refs[3 * n:]
        x, y, c = _position()
        local = [pltpu.make_async_copy(srcs[t], outs[t].at[_slot(x, y, c)], local_sems.at[t]) for t in range(n)]

        def diagonal(t, core):
            block = outs[t].at[_slot(1 - x, 1 - y, core)]
            return pltpu.make_async_remote_copy(
                src_ref=block, dst_ref=block, send_sem=send_sems.at[t], recv_sem=recv_sems.at[t],
                device_id=(x, y, 1 - c), device_id_type=MESH)

        for cp in local:
            cp.start()
        for t in range(n):
            diagonal(t, c).start()
        for t in range(n):
            diagonal(t, c).wait_send()
            diagonal(t, 1 - c).wait_recv()
        for cp in local:
            cp.wait()

    return _pcall(name, body, [*shards, *lands], [_ANY] * (2 * n),
                  [_sds(l.shape, l.dtype) for l in lands], [_ANY] * n,
                  scratch=[pltpu.SemaphoreType.DMA((n,)), pltpu.SemaphoreType.DMA((n,)),
                           pltpu.SemaphoreType.DMA((n,))],
                  after=after, aliases={n + t: t for t in range(n)})


def _exchange_sibling(name, fulls, after):
    n = len(fulls)

    def body(*refs):
        src = refs[:n]
        out = refs[n:2 * n]
        send_sems, recv_sems = refs[2 * n:]
        x, y, c = _position()
        copies = [pltpu.make_async_remote_copy(
            src_ref=src[t].at[:, 1 - c], dst_ref=out[t], send_sem=send_sems.at[t], recv_sem=recv_sems.at[t],
            device_id=(x, y, 1 - c), device_id_type=MESH) for t in range(n)]
        for cp in copies:
            cp.start()
        for cp in copies:
            cp.wait()

    return _pcall(name, body, fulls, [_ANY] * n, [_sds((N_CHIPS,) + f.shape[2:], f.dtype) for f in fulls],
                  [_ANY] * n, scratch=[pltpu.SemaphoreType.DMA((n,)), pltpu.SemaphoreType.DMA((n,))],
                  after=after)


def _add_sibling(name, full, recv, core, after):
    _, _, r, c = full.shape
    tr = _tile(r, max(8, (256 * 1024) // c))

    def body(core_ref, f_ref, r_ref, o_ref):
        o_ref[...] = (f_ref[...].astype(F32) + r_ref[...].astype(F32)).astype(o_ref.dtype)

    return _pcall(
        name, body, [full, recv],
        [pl.BlockSpec((None, None, tr, c), lambda p, i, core_ref: (p, core_ref[0], i, 0)),
         pl.BlockSpec((None, tr, c), lambda p, i, core_ref: (p, i, 0))],
        _sds((N_CHIPS, r, c), BF16), pl.BlockSpec((None, tr, c), lambda p, i, core_ref: (p, i, 0)),
        grid=(N_CHIPS, r // tr), sem=("parallel", "parallel"), prefetch=[core], after=after)


def _scatter_copies(srcs, lands, send_sems, recv_sems):
    x, y, c = _position()
    return [[pltpu.make_async_remote_copy(
        src_ref=srcs[t].at[2 * px + py], dst_ref=lands[t].at[j],
        send_sem=send_sems[t].at[j], recv_sem=recv_sems[t].at[j],
        device_id=(px, py, c), device_id_type=MESH) for j, (px, py) in enumerate(_other_chips(x, y))]
        for t in range(len(srcs))]


def _scatter_start(name, partials):
    lands = [lax.empty((N_CHIPS - 1,) + p.shape[1:], p.dtype) for p in partials]
    return _split_start(name, partials, lands, _scatter_copies, 3)


def _scatter_wait(name, started, after):
    return _split_wait(name, started, _scatter_copies, after)


class _Job:
    def __init__(self, **fields):
        self.__dict__.update(fields)


def _reduce_adam_job(partial, recv, chip, w, m, v, layer, carried):
    n_layers, r, c = w.shape
    tr = _tile(r, max(8, (256 * 1024) // c))

    def body(p_ref, r_ref, w_ref, m_ref, v_ref, *rest):
        g_ref, d_ref, nm_ref, nv_ref = rest[-4:]
        g = p_ref[...].astype(F32)
        for j in range(N_CHIPS - 1):
            g = g + r_ref[j].astype(F32)
        g_ref[...] = g
        d_ref[...], nm_ref[...], nv_ref[...] = _adamw(w_ref[...], g, m_ref[...], v_ref[...])

    layered = ((None, tr, c), lambda blk, chip_ref: (layer, blk, 0))
    in_specs = [((None, tr, c), lambda blk, chip_ref: (chip_ref[0], blk, 0)),
                ((N_CHIPS - 1, tr, c), lambda blk, chip_ref: (0, blk, 0)), layered, layered, layered]
    operands = [partial, recv, w, m, v]
    aliases = {}
    if carried is not None:
        operands += list(carried)
        in_specs += [None] * 4
        aliases = {5 + o: o for o in range(4)}
    return _Job(operands=operands, in_specs=in_specs, out_shape=[_sds((n_layers, r, c), F32)] * 4,
                out_specs=[layered] * 4, body=body, aliases=aliases, prefetch=chip, n_blocks=r // tr)


def _run_job(name, job, after):
    def spec(entry):
        if entry is None:
            return _ANY
        shape, index = entry
        return pl.BlockSpec(shape, lambda blk, pre, index=index: index(blk, pre))

    def body(pre_ref, *refs):
        job.body(*refs)

    return _pcall(name, body, job.operands, [spec(e) for e in job.in_specs], job.out_shape,
                  [spec(e) for e in job.out_specs], grid=(job.n_blocks,), sem=("parallel",),
                  prefetch=[job.prefetch], after=after,
                  aliases={1 + i: o for i, o in job.aliases.items()})


def _small_sum(name, gathered, own, device, after=None):
    r, lanes = own.shape

    def body(dev_ref, g_ref, own_ref, out_ref):
        dev = dev_ref[0]
        mine = own_ref[...]
        total = jnp.where(dev == 0, mine, g_ref[0])
        for d in range(1, N_DEV):
            total = total + jnp.where(dev == d, mine, g_ref[d])
        out_ref[...] = total

    return _pcall(name, body, [gathered, own],
                  [pl.BlockSpec((N_DEV, r, lanes), lambda i, dev_ref: (0, 0, 0)),
                   pl.BlockSpec((r, lanes), lambda i, dev_ref: (0, 0))],
                  _sds((r, lanes), F32), pl.BlockSpec((r, lanes), lambda i, dev_ref: (0, 0)),
                  grid=(1,), sem=("arbitrary",), prefetch=[device], after=after)


def _pack(arrays):
    return jnp.concatenate([a.reshape(-1, 128) for a in arrays], axis=0)


def _unpack(packed, shapes):
    out, row = [], 0
    for shape in shapes:
        rows = math.prod(shape) // 128
        out.append(packed[row:row + rows].reshape(shape))
        row += rows
    return out


class _Order:
    def __init__(self):
        self.last = None

    def __call__(self, fn, *args, **kwargs):
        out = fn(*args, after=self.last, **kwargs)
        self.last = out[0] if isinstance(out, (list, tuple)) else out
        return out


def kernel(x, a_w_in, a_ln_g, a_ln_b, a_w_s, a_b_s, a_w_out, b_w_in, b_w_grp, b_scale, b_w_out, norm_mix, norm_mlp, mlp_w1, mlp_w2, final_norm, loss_target, m_a_w_in, m_a_ln_g, m_a_ln_b, m_a_w_s, m_a_b_s, m_a_w_out, m_b_w_in, m_b_w_grp, m_b_scale, m_b_w_out, m_norm_mix, m_norm_mlp, m_mlp_w1, m_mlp_w2, m_final_norm, v_a_w_in, v_a_ln_g, v_a_ln_b, v_a_w_s, v_a_b_s, v_a_w_out, v_b_w_in, v_b_w_grp, v_b_scale, v_b_w_out, v_norm_mix, v_norm_mlp, v_mlp_w1, v_mlp_w2, v_final_norm):
    s, d = x.shape[1], x.shape[2]
    depth = mlp_w1.shape[0]
    a_slab = a_w_in.shape[2]
    ff_slab = mlp_w1.shape[2]
    ff_rows = mlp_w2.shape[1]
    bh = b_w_grp.shape[3]
    my_x, my_y, my_c = _position()
    core = jnp.reshape(my_c, (1,)).astype(jnp.int32)
    chip = jnp.reshape(2 * my_x + my_y, (1,)).astype(jnp.int32)
    device = _slot(my_x, my_y, my_c)
    run = _Order()

    w1_b, w2_b = mlp_w1.astype(BF16), mlp_w2.astype(BF16)
    shards = [a_w_in[0].astype(BF16), a_w_out[0].astype(BF16), b_scale,
              w1_b[0], w2_b[0],
              b_w_in[0].astype(BF16), b_w_grp[0].astype(BF16), b_w_out[0].astype(BF16),
              w1_b[1], w2_b[1]]
    groups = [[0], [1, 2], [3], [4], [5, 6, 7], [8], [9]]
    start_with = {1: [2, 3], 2: [4], 3: [5], 4: [6]}
    hop1, hop2 = {}, {}
    _, hop1[0], token = _gather_step("weights_group0_hop1", [], [shards[t] for t in groups[0]])
    _, hop1[1], token = _gather_step("weights_group1_hop1", [], [shards[t] for t in groups[1]], token)
    run.last = token

    def advance(g):
        if g not in hop1:
            return
        ahead = start_with.get(g, [])
        fresh = [shards[t] for a in ahead for t in groups[a]]
        hop2[g], started, tok = _gather_step(f"weights_group{g}_hop2", hop1.pop(g), fresh, run.last)
        for a in ahead:
            hop1[a], started = started[:len(groups[a])], started[len(groups[a]):]
        run.last = tok

    def gathered(g):
        advance(g)
        if g == 0:
            advance(1)
        srcs, lands = _gather_wait(f"weights_group{g}_wait", hop2.pop(g), run.last)
        run.last = srcs[0]
        return run(_gather_finish, f"weights_group{g}_finish", srcs, lands)

    h0 = x[0]
    target = loss_target[0]
    ln_g, ln_b = a_ln_g, a_ln_b
    w_s = a_w_s[0]
    b_s_col = a_b_s[0][:, :, None]
    nmix = [norm_mix[l][None, :] for l in range(depth)]
    nmlp = [norm_mlp[l][None, :] for l in range(depth)]

    def mlp_forward(l, h, up_group):
        hn = run(_rms_fwd, f"mlp{l}_norm", h, nmlp[l])
        (w1,) = gathered(up_group)
        advance(up_group + 1)
        act, act_sq = run(_mm_nn, f"mlp{l}_up", hn, w1,
                          lambda acc: (jnp.maximum(acc, 0.0), jnp.square(jnp.maximum(acc, 0.0))),
                          (BF16, BF16), slab=True)
        (w2,) = gathered(up_group + 1)
        advance(up_group + 2)
        w2 = w2.reshape(-1, d)
        (h_out,) = run(_mm_nn, f"mlp{l}_down", act_sq, w2, lambda acc, res: (acc + res,), (F32,),
                       extras=(h,), extra_kinds=("tile",))
        return h_out, (h, hn, act, act_sq, w1, w2)

    scattered = []

    def scatter_partials(name, partials, specs):
        in_flight, tok = _scatter_start(name + "_scatter_start", partials)
        run.last = tok
        scattered.append((name, in_flight, specs))

    weights = {"a_w_in": (a_w_in, m_a_w_in, v_a_w_in), "a_w_out": (a_w_out, m_a_w_out, v_a_w_out),
               "b_w_in": (b_w_in, m_b_w_in, v_b_w_in), "b_w_grp": (b_w_grp, m_b_w_grp, v_b_w_grp),
               "b_w_out": (b_w_out, m_b_w_out, v_b_w_out), "mlp_w1": (mlp_w1, m_mlp_w1, v_mlp_w1),
               "mlp_w2": (mlp_w2, m_mlp_w2, v_mlp_w2)}
    results = {}

    def finish_group(name, in_flight, specs):
        partials, lands = _scatter_wait(name + "_scatter_wait", in_flight, run.last)
        run.last = lands[0]
        for t, (wname, layer) in enumerate(specs):
            w, m, v = weights[wname]
            shape = (w.shape[0],) + partials[t].shape[1:]
            job = _reduce_adam_job(partials[t], lands[t], chip, w.reshape(shape), m.reshape(shape),
                                   v.reshape(shape), layer, results.get(wname))
            results[wname] = run(_run_job, f"{name}_reduce_adam_{t}", job)

    def weight_grad(name, a, b, by_rows, block, between, tn=1024):
        other = run(_mm_tn_half, name + "_other", a, b, core, False, by_rows, block, tn=tn)
        sent, tok = _sibling_start(name + "_sibling_start", [other])
        run.last = tok
        middle = between()
        (recv,) = _sibling_wait(name + "_sibling_wait", sent, run.last)
        run.last = recv
        return run(_mm_tn_half, name + "_own", a, b, core, True, by_rows, block, recv=recv, tn=tn), middle

    def mlp_backward(l, saved, dh, dhb):
        h, hn, act, act_sq, w1, w2 = saved
        part_w2, (dpre,) = weight_grad(
            f"mlp{l}_down_dw", act_sq, dhb, True, ff_rows,
            lambda: run(_mm_nt, f"mlp{l}_down_dx", dhb, w2, lambda acc, a: (2.0 * a.astype(F32) * acc,),
                        (BF16,), extras=(act,), extra_kinds=("tile",)))
        scatter_partials(f"mlp{l}_down_grads", [part_w2], [("mlp_w2", l)])
        part_w1, (dhn,) = weight_grad(
            f"mlp{l}_up_dw", hn, dpre, False, ff_slab,
            lambda: run(_mm_nt, f"mlp{l}_up_dx", dpre, w1, lambda acc: (acc,), (F32,), slab=True))
        scatter_partials(f"mlp{l}_up_grads", [part_w1], [("mlp_w1", l)])
        dh, dhb, g_norm = run(_rms_bwd, f"mlp{l}_norm_bwd", dhn, h, nmlp[l], dh)
        return dh, dhb, g_norm

    hn0 = run(_rms_fwd, "mix0_norm", h0, nmix[0])
    (wa_in,) = gathered(0)
    (pre,) = run(_mm_nn, "mixa_in", hn0, wa_in, lambda acc: (acc,), (F32,), slab=True)
    wa_out, scale = gathered(1)
    wa_out, scale = wa_out.reshape(d, d), scale.reshape(1, d)
    gated = run(_amix_fwd, "mixa_gate", pre, ln_g, ln_b, w_s, b_s_col)
    advance(2)
    (h1,) = run(_mm_nn, "mixa_out", gated, wa_out, lambda acc, res: (acc + res,), (F32,),
                extras=(h0,), extra_kinds=("tile",))
    h2, saved_mlp0 = mlp_forward(0, h1, 2)
    hn2 = run(_rms_fwd, "mix1_norm", h2, nmix[1])
    wb_in, wb_grp, wb_out = gathered(4)
    advance(5)
    wb_in, wb_out = wb_in.reshape(d, d), wb_out.reshape(d, d)
    wb_grp = jnp.transpose(wb_grp, (1, 0, 2, 3)).reshape(B_GROUPS, bh, bh)
    (vb,) = run(_mm_nn, "mixb_in", hn2, wb_in, lambda acc: (acc,), (F32,))
    pooled = run(_pool, "mixb_pool", vb, backward=False)
    tm = _tile(s, 1024)
    grp_tile = pl.BlockSpec((tm, bh), lambda i, j, k: (i, j))
    grp_weight = pl.BlockSpec((None, bh, bh), lambda i, j, k: (j, 0, 0))
    mixed, mixed_scaled = run(
        _matmul, "mixb_grp", pooled, wb_grp, NN, (s // tm, B_GROUPS, 1), grp_tile, grp_weight,
        [_sds((s, d), BF16), _sds((s, d), BF16)], [grp_tile] * 2,
        (tm, bh), lambda acc, sc: (acc, acc * sc), (scale,), [pl.BlockSpec((1, bh), lambda i, j, k: (0, j))])
    (h3,) = run(_mm_nn, "mixb_out", mixed_scaled, wb_out, lambda acc, res: (acc + res,), (F32,),
                extras=(h2,), extra_kinds=("tile",))
    h4, saved_mlp1 = mlp_forward(1, h3, 5)
    dh, dhb, g_final, loss_part = run(_loss_head, "loss_head", h4, final_norm[None, :], target)

    dh, dhb, g_nmlp1 = mlp_backward(1, saved_mlp1, dh, dhb)
    tks = _tile(s, 1024)
    grp_rows = pl.BlockSpec((tks, bh), lambda i, j, k: (k, j))

    def mixb_middle():
        dms_scaled, dms_mixed = run(
            _mm_nt, "mixb_out_dx", dhb, wb_out,
            lambda acc, sc, mx: (acc * sc, acc * mx.astype(F32)), (BF16, F32),
            extras=(scale, mixed), extra_kinds=("row", "tile"))
        g_scale = run(_colsum, "mixb_scale_dw", dms_mixed)
        (g_wb_grp,) = run(
            _matmul, "mixb_grp_dw", pooled, dms_scaled, TN, (1, B_GROUPS, s // tks), grp_rows, grp_rows,
            [_sds((B_GROUPS, bh, bh), BF16)], [grp_weight], (bh, bh), lambda acc: (acc,))
        (dpooled,) = run(
            _matmul, "mixb_grp_dx", dms_scaled, wb_grp, NT, (s // tm, B_GROUPS, 1), grp_tile, grp_weight,
            [_sds((s, d), F32)], [grp_tile], (tm, bh), lambda acc: (acc,))
        return g_scale, g_wb_grp, run(_pool, "mixb_pool_bwd", dpooled, backward=True)

    part_wb_out, (g_scale, g_wb_grp, dvb) = weight_grad("mixb_out_dw", mixed_scaled, dhb, True, d // N_DEV,
                                                        mixb_middle, tn=d)
    part_wb_in, (dhn2,) = weight_grad(
        "mixb_in_dw", hn2, dvb, True, d // N_DEV,
        lambda: run(_mm_nt, "mixb_in_dx", dvb, wb_in, lambda acc: (acc,), (F32,)), tn=d)
    grp_full = jnp.transpose(g_wb_grp.reshape(B_GROUPS, N_DEV, bh // N_DEV, bh), (1, 0, 2, 3))
    grp_full = grp_full.reshape(N_CHIPS, 2, B_GROUPS * bh // N_DEV, bh)
    (grp_sibling,) = run(_exchange_sibling, "mixb_grp_dw_to_sibling", [grp_full])
    part_wb_grp = run(_add_sibling, "mixb_grp_dw_add_sibling", grp_full, grp_sibling, core)
    scatter_partials("mixb_grads", [part_wb_out, part_wb_grp, part_wb_in],
                     [("b_w_out", 0), ("b_w_grp", 0), ("b_w_in", 0)])
    dh, dhb, g_nmix1 = run(_rms_bwd, "mix1_norm_bwd", dhn2, h2, nmix[1], dh)
    dh, dhb, g_nmlp0 = mlp_backward(0, saved_mlp0, dh, dhb)
    def mixa_middle():
        (dgated,) = run(_mm_nt, "mixa_out_dx", dhb, wa_out, lambda acc: (acc,), (F32,))
        return run(_amix_bwd, "mixa_gate_bwd", pre, dgated, ln_g, ln_b, w_s, b_s_col)

    part_wa_out, (dpre, g_ln_g, g_ln_b, g_w_s, g_b_s) = weight_grad("mixa_out_dw", gated, dhb, True, d // N_DEV,
                                                                     mixa_middle, tn=d)
    part_wa_in, (dhn0,) = weight_grad(
        "mixa_in_dw", hn0, dpre, False, a_slab,
        lambda: run(_mm_nt, "mixa_in_dx", dpre, wa_in, lambda acc: (acc,), (F32,), slab=True))
    scatter_partials("mixa_grads", [part_wa_in, part_wa_out], [("a_w_in", 0), ("a_w_out", 0)])
    grad_x, _, g_nmix0 = run(_rms_bwd, "mix0_norm_bwd", dhn0, h0, nmix[0], dh)

    g_norm_mix = jnp.concatenate([g_nmix0, g_nmix1], axis=0)
    g_norm_mlp = jnp.concatenate([g_nmlp0, g_nmlp1], axis=0)
    loss_row = jnp.pad(loss_part, ((0, 0), (0, 127)))
    small_parts = [g_ln_g, g_ln_b, g_w_s, g_b_s, g_norm_mix, g_norm_mlp, g_final, g_scale, loss_row]
    packed = _pack(small_parts)
    small_sent, tok = _split_start("small_grads_start", [packed], [lax.empty((N_DEV,) + packed.shape, F32)],
                                   _small_copies, N_DEV - 1)
    run.last = tok

    for group in scattered[:-1]:
        finish_group(*group)
    own_packed, small_gathered = _split_wait("small_grads_wait", small_sent, _small_copies, run.last)
    run.last = small_gathered[0]
    small_sum = run(_small_sum, "small_grads_sum", small_gathered[0], own_packed[0],
                    jnp.reshape(device, (1,)).astype(jnp.int32))
    sg = _unpack(small_sum, [a_ln_g.shape, a_ln_b.shape, a_w_s.shape, a_b_s.shape, norm_mix.shape,
                             norm_mlp.shape, final_norm.shape, (1, d), (1, 128)])
    loss = sg.pop()[0, 0]
    shard = b_scale.shape[1]
    sg[7] = lax.dynamic_slice(sg[7], (0, device * shard), (1, shard))
    small_w = [a_ln_g, a_ln_b, a_w_s, a_b_s, norm_mix, norm_mlp, final_norm, b_scale]
    small_m = [m_a_ln_g, m_a_ln_b, m_a_w_s, m_a_b_s, m_norm_mix, m_norm_mlp, m_final_norm, m_b_scale]
    small_v = [v_a_ln_g, v_a_ln_b, v_a_w_s, v_a_b_s, v_norm_mix, v_norm_mlp, v_final_norm, v_b_scale]
    small_out = run(_adam_rows, "small_adam", _pack(sg), _pack(small_w), _pack(small_m), _pack(small_v))
    shapes = [w.shape for w in small_w]
    small_res = [sg] + [_unpack(o, shapes) for o in small_out]

    finish_group(*scattered[-1])
    big = {wname: [o.reshape(weights[wname][0].shape) for o in outs] for wname, outs in results.items()}

    def leaf(o):
        return (big["a_w_in"][o], small_res[o][0], small_res[o][1], small_res[o][2], small_res[o][3],
                big["a_w_out"][o], big["b_w_in"][o], big["b_w_grp"][o], small_res[o][7], big["b_w_out"][o],
                small_res[o][4], small_res[o][5], big["mlp_w1"][o], big["mlp_w2"][o], small_res[o][6])

    return (loss, grad_x[None], *leaf(0), *leaf(1), *leaf(2), *leaf(3))
```

```python
import math

import jax
import jax.numpy as jnp
from jax import lax
from jax.experimental import pallas as pl
from jax.experimental.pallas import tpu as pltpu

F32 = jnp.float32
BF16 = jnp.bfloat16
MESH = pl.DeviceIdType.MESH

N_DEV = 8
N_CHIPS = 4
CHUNK = 128
A_GROUPS = 8
B_WINDOWS = (2, 4, 8, 16)
B_GROUPS = len(B_WINDOWS)
EPS = 1e-6
ADAM_LR = 0.001
ADAM_B1 = 0.9
ADAM_B2 = 0.999
ADAM_EPS = 1e-08
ADAM_WD = 0.01
ADAM_STEP = 10

VMEM_LIMIT = 48 * 1024 * 1024

NN = (((1,), (0,)), ((), ()))
NT = (((1,), (1,)), ((), ()))
TN = (((0,), (0,)), ((), ()))

_ANY = pl.BlockSpec(memory_space=pl.ANY)
_HBM = pl.BlockSpec(memory_space=pltpu.HBM)
_SEM = pl.BlockSpec(memory_space=pltpu.SEMAPHORE)
_EFFECT = pltpu.SideEffectType.DATAFLOW_SIDE_EFFECTING


def _tile(n, pref):
    return pref if n % pref == 0 else n


def _sds(shape, dtype):
    return jax.ShapeDtypeStruct(shape, dtype)


def _pcall(name, body, operands, in_specs, out_shape, out_specs, *, grid=None, sem=None, scratch=(),
           prefetch=(), after=None, aliases=None):
    after = [] if after is None else [after]
    n_lead = len(prefetch) + len(operands)
    n_after = len(after)

    def wrapped(*refs):
        body(*refs[:n_lead], *refs[n_lead + n_after:])

    in_specs = list(in_specs) + [_ANY] * n_after
    params = pltpu.CompilerParams(vmem_limit_bytes=VMEM_LIMIT) if sem is None else \
        pltpu.CompilerParams(dimension_semantics=sem, vmem_limit_bytes=VMEM_LIMIT)
    kwargs = dict(out_shape=out_shape, scratch_shapes=list(scratch), compiler_params=params, name=name,
                  input_output_aliases=aliases or {})
    if prefetch:
        kwargs["grid_spec"] = pltpu.PrefetchScalarGridSpec(
            num_scalar_prefetch=len(prefetch), grid=grid, in_specs=in_specs, out_specs=out_specs,
            scratch_shapes=list(scratch))
        kwargs.pop("scratch_shapes")
    else:
        kwargs.update(in_specs=in_specs, out_specs=out_specs)
        if grid is not None:
            kwargs["grid"] = grid
    return pl.pallas_call(wrapped, **kwargs)(*prefetch, *operands, *after)


def _matmul(name, a, b, dims, grid, a_spec, b_spec, out_shape, out_specs, acc_shape,
            epilogue, extras=(), extra_specs=(), after=None, prefetch=(), b_parts=1):
    nk = grid[2]
    n_extra = len(extras)
    n_out = len(out_shape)
    n_pre = len(prefetch)

    def body(*refs):
        refs = refs[n_pre:]
        a_ref, b_ref = refs[0], refs[1]
        extra_refs = refs[2:2 + n_extra]
        out_refs = refs[2 + n_extra:2 + n_extra + n_out]

        def finish(acc):
            outs = epilogue(acc, *[r[...] for r in extra_refs])
            for o_ref, o in zip(out_refs, outs):
                o_ref[...] = o.astype(o_ref.dtype)

        def product():
            if b_parts == 1:
                return lax.dot_general(a_ref[...], b_ref[...], dims, preferred_element_type=F32)
            width = b_ref.shape[2]
            total = None
            for p in range(b_parts):
                part = lax.dot_general(a_ref[:, p * width:(p + 1) * width], b_ref[p], dims,
                                       preferred_element_type=F32)
                total = part if total is None else total + part
            return total

        if nk == 1:
            finish(product())
        else:
            acc_ref = refs[-1]
            k = pl.program_id(2)

            @pl.when(k == 0)
            def _():
                acc_ref[...] = product()

            if nk > 2:
                @pl.when(jnp.logical_and(k > 0, k < nk - 1))
                def _():
                    acc_ref[...] += product()

            @pl.when(k == nk - 1)
            def _():
                finish(acc_ref[...] + product())

    scratch = [] if nk == 1 else [pltpu.VMEM(acc_shape, F32)]
    return _pcall(name, body, [a, b, *extras], [a_spec, b_spec, *extra_specs], out_shape, out_specs,
                  grid=grid, sem=("parallel", "parallel", "arbitrary"), scratch=scratch, after=after,
                  prefetch=prefetch)


def _mm_nn(name, a, b, epilogue, out_dtypes, extras=(), extra_kinds=(), slab=False, after=None,
           tm=1024, tn=1024, tk=2048):
    m, kd = a.shape
    if slab:
        n_slab, _, w = b.shape
        n = n_slab * w
        tn = _tile(w, min(tn, w))
        per = w // tn
        tk = _tile(kd, tk)
        b_spec = pl.BlockSpec((None, tk, tn), lambda i, j, k: (j // per, k, j % per))
    else:
        n = b.shape[1]
        tn = _tile(n, tn)
        tk = _tile(kd, tk)
        b_spec = pl.BlockSpec((tk, tn), lambda i, j, k: (k, j))
    tm = _tile(m, tm)
    grid = (m // tm, n // tn, kd // tk)
    a_spec = pl.BlockSpec((tm, tk), lambda i, j, k: (i, k))
    tile_spec = pl.BlockSpec((tm, tn), lambda i, j, k: (i, j))
    row_spec = pl.BlockSpec((1, tn), lambda i, j, k: (0, j))
    extra_specs = [tile_spec if kind == "tile" else row_spec for kind in extra_kinds]
    return _matmul(name, a, b, NN, grid, a_spec, b_spec,
                   [_sds((m, n), d) for d in out_dtypes], [tile_spec for _ in out_dtypes],
                   (tm, tn), epilogue, extras, extra_specs, after=after)


def _mm_nt(name, a, b, epilogue, out_dtypes, extras=(), extra_kinds=(), slab=False, after=None,
           tm=1024, tn=1024, tk=2048):
    m, kd = a.shape
    parts = 1
    if slab:
        n_slab, n, w = b.shape
        tn = _tile(n, tn)
        if tk > w and tk % w == 0 and n_slab % (tk // w) == 0:
            parts = tk // w
            b_spec = pl.BlockSpec((parts, tn, w), lambda i, j, k, *_: (k, j, 0))
        else:
            tk = _tile(w, min(tk, w))
            per = w // tk
            b_spec = pl.BlockSpec((None, tn, tk), lambda i, j, k, *_: (k // per, j, k % per))
    else:
        n = b.shape[0]
        tn = _tile(n, tn)
        tk = _tile(kd, tk)
        b_spec = pl.BlockSpec((tn, tk), lambda i, j, k, *_: (j, k))
    tm = _tile(m, tm)
    grid = (m // tm, n // tn, kd // tk)
    a_spec = pl.BlockSpec((tm, tk), lambda i, j, k, *_: (i, k))
    tile_spec = pl.BlockSpec((tm, tn), lambda i, j, k, *_: (i, j))
    row_spec = pl.BlockSpec((1, tn), lambda i, j, k, *_: (0, j))
    extra_specs = [tile_spec if kind == "tile" else row_spec for kind in extra_kinds]
    return _matmul(name, a, b, NT, grid, a_spec, b_spec,
                   [_sds((m, n), d) for d in out_dtypes], [tile_spec for _ in out_dtypes],
                   (tm, tn), epilogue, extras, extra_specs, after=after, b_parts=parts)


def _mm_tn_half(name, a, b, core, own, by_rows, block, recv=None, after=None, tm=1024, tn=1024, tk=2048):
    s, m = a.shape
    n = b.shape[1]
    tk = _tile(s, tk)

    def owner(chip, core_ref):
        return 2 * chip + (core_ref[0] if own else 1 - core_ref[0])

    if by_rows:
        r, c = block, n
        tm, tn = _tile(r, min(tm, r)), _tile(c, tn)
        per = r // tm
        grid = (N_CHIPS * per, c // tn, s // tk)
        a_spec = pl.BlockSpec((tk, tm), lambda i, j, k, cr, *_: (k, owner(i // per, cr) * per + i % per))
        b_spec = pl.BlockSpec((tk, tn), lambda i, j, k, cr, *_: (k, j))
        o_spec = pl.BlockSpec((None, tm, tn), lambda i, j, k, cr, *_: (i // per, i % per, j))
    else:
        r, c = m, block
        tm, tn = _tile(r, tm), _tile(c, min(tn, c))
        per = c // tn
        grid = (r // tm, N_CHIPS * per, s // tk)
        a_spec = pl.BlockSpec((tk, tm), lambda i, j, k, cr, *_: (k, i))
        b_spec = pl.BlockSpec((tk, tn), lambda i, j, k, cr, *_: (k, owner(j // per, cr) * per + j % per))
        o_spec = pl.BlockSpec((None, tm, tn), lambda i, j, k, cr, *_: (j // per, i, j % per))
    if recv is None:
        extras, epilogue = (), lambda acc: (acc,)
    else:
        extras, epilogue = (recv,), lambda acc, other: (acc + other.astype(F32),)
    return _matmul(name, a, b, TN, grid, a_spec, b_spec, [_sds((N_CHIPS, r, c), BF16)], [o_spec], (tm, tn),
                   epilogue, extras, [o_spec] * len(extras), after=after, prefetch=[core])[0]


def _rms_fwd(name, h, g, after=None):
    s, d = h.shape
    tr = _tile(s, 256)

    def body(h_ref, g_ref, o_ref):
        x = h_ref[...]
        r = lax.rsqrt(jnp.mean(x * x, axis=-1, keepdims=True) + EPS)
        o_ref[...] = (x * r * g_ref[...]).astype(o_ref.dtype)

    row = pl.BlockSpec((tr, d), lambda i: (i, 0))
    vec = pl.BlockSpec((1, d), lambda i: (0, 0))
    return _pcall(name, body, [h, g], [row, vec], _sds((s, d), BF16), row, grid=(s // tr,),
                  sem=("parallel",), after=after)


def _accumulate(ref, part, step):
    @pl.when(step == 0)
    def _():
        ref[...] = part

    @pl.when(step > 0)
    def _():
        ref[...] += part


def _rms_bwd(name, dhn, h, g, dres, after=None):
    s, d = h.shape
    tr = _tile(s, 256)

    def body(dhn_ref, h_ref, g_ref, dres_ref, dh_ref, dhb_ref, gp_ref):
        x = h_ref[...]
        r = lax.rsqrt(jnp.mean(x * x, axis=-1, keepdims=True) + EPS)
        n = x * r
        dy = dhn_ref[...]
        dn = dy * g_ref[...]
        dh = dres_ref[...] + r * (dn - n * jnp.mean(dn * n, axis=-1, keepdims=True))
        dh_ref[...] = dh
        dhb_ref[...] = dh.astype(BF16)
        _accumulate(gp_ref, jnp.sum(dy * n, axis=0, keepdims=True), pl.program_id(0))

    row = pl.BlockSpec((tr, d), lambda i: (i, 0))
    vec = pl.BlockSpec((1, d), lambda i: (0, 0))
    return _pcall(name, body, [dhn, h, g, dres], [row, row, vec, row],
                  [_sds((s, d), F32), _sds((s, d), BF16), _sds((1, d), F32)], [row, row, vec],
                  grid=(s // tr,), sem=("arbitrary",), after=after)


def _loss_head(name, h, g, target, after=None):
    s, d = h.shape
    tr = _tile(s, 256)

    def body(h_ref, g_ref, t_ref, dh_ref, dhb_ref, gp_ref, loss_ref):
        x = h_ref[...]
        gg = g_ref[...]
        r = lax.rsqrt(jnp.mean(x * x, axis=-1, keepdims=True) + EPS)
        n = x * r
        e = n * gg - t_ref[...]
        dy = e * (1.0 / d)
        dn = dy * gg
        dh = r * (dn - n * jnp.mean(dn * n, axis=-1, keepdims=True))
        dh_ref[...] = dh
        dhb_ref[...] = dh.astype(BF16)
        step = pl.program_id(0)
        _accumulate(gp_ref, jnp.sum(dy * n, axis=0, keepdims=True), step)
        row_loss = jnp.mean(e * e, axis=-1, keepdims=True)
        _accumulate(loss_ref, 0.5 * jnp.sum(row_loss, axis=0, keepdims=True), step)

    row = pl.BlockSpec((tr, d), lambda i: (i, 0))
    vec = pl.BlockSpec((1, d), lambda i: (0, 0))
    one = pl.BlockSpec((1, 1), lambda i: (0, 0))
    return _pcall(name, body, [h, g, target], [row, vec, row],
                  [_sds((s, d), F32), _sds((s, d), BF16), _sds((1, d), F32), _sds((1, 1), F32)],
                  [row, row, vec, one], grid=(s // tr,), sem=("arbitrary",), after=after)


_SQRT_HALF = math.sqrt(0.5)
_INV_SQRT_2PI = 1.0 / math.sqrt(2.0 * math.pi)


def _gelu(x):
    return 0.5 * x * (1.0 + lax.erf(x * _SQRT_HALF))


def _gelu_grad(x):
    return 0.5 * (1.0 + lax.erf(x * _SQRT_HALF)) + x * jnp.exp(-0.5 * x * x) * _INV_SQRT_2PI


def _causal_mask():
    row = lax.broadcasted_iota(jnp.int32, (CHUNK, CHUNK), 0)
    col = lax.broadcasted_iota(jnp.int32, (CHUNK, CHUNK), 1)
    return row >= col


def _row_sum(x):
    return jnp.sum(x, axis=-1, keepdims=True)


def _masked_spatial(ws_ref, grp):
    return jnp.where(_causal_mask(), ws_ref[grp], 0.0).astype(BF16)


def _layernorm_stats(pre_ref, v_scr, w, head):
    total = jnp.zeros((CHUNK, 1), F32)
    for grp in range(A_GROUPS):
        v = _gelu(pre_ref[:, w + grp * head:w + (grp + 1) * head])
        v_scr[:, grp * head:(grp + 1) * head] = v
        total = total + _row_sum(v)
    mu = total * (1.0 / w)
    square = jnp.zeros((CHUNK, 1), F32)
    for grp in range(A_GROUPS):
        xc = v_scr[:, grp * head:(grp + 1) * head] - mu
        square = square + _row_sum(xc * xc)
    return mu, lax.rsqrt(square * (1.0 / w) + EPS)


def _amix_fwd(name, pre, ln_g, ln_b, w_s, b_s_col, after=None):
    s, w2 = pre.shape
    w = w2 // 2
    head = w // A_GROUPS

    def body(pre_ref, g_ref, b_ref, ws_ref, bs_ref, o_ref, v_scr):
        mu, rstd = _layernorm_stats(pre_ref, v_scr, w, head)
        for grp in range(A_GROUPS):
            cols = slice(grp * head, (grp + 1) * head)
            vhat = (v_scr[:, cols] - mu) * rstd
            vn = (vhat * g_ref[:, cols] + b_ref[:, cols]).astype(BF16)
            sg = jnp.dot(_masked_spatial(ws_ref, grp), vn, preferred_element_type=F32) + bs_ref[grp]
            o_ref[:, cols] = (_gelu(pre_ref[:, cols]) * sg).astype(o_ref.dtype)

    vec = pl.BlockSpec((1, w), lambda i: (0, 0))
    return _pcall(
        name, body, [pre, ln_g, ln_b, w_s, b_s_col],
        [pl.BlockSpec((CHUNK, w2), lambda i: (i, 0)), vec, vec,
         pl.BlockSpec((A_GROUPS, CHUNK, CHUNK), lambda i: (0, 0, 0)),
         pl.BlockSpec((A_GROUPS, CHUNK, 1), lambda i: (0, 0, 0))],
        _sds((s, w), BF16), pl.BlockSpec((CHUNK, w), lambda i: (i, 0)),
        grid=(s // CHUNK,), sem=("parallel",), scratch=[pltpu.VMEM((CHUNK, w), F32)], after=after)


def _amix_bwd(name, pre, dgated, ln_g, ln_b, w_s, b_s_col, after=None):
    s, w2 = pre.shape
    w = w2 // 2
    head = w // A_GROUPS

    def body(pre_ref, dg_ref, g_ref, b_ref, ws_ref, bs_ref, dpre_ref, glg_ref, glb_ref, gws_ref, gbs_ref,
             v_scr, dvn_scr):
        @pl.when(pl.program_id(0) == 0)
        def _():
            for ref in (glg_ref, glb_ref, gws_ref, gbs_ref):
                ref[...] = jnp.zeros_like(ref)

        mu, rstd = _layernorm_stats(pre_ref, v_scr, w, head)
        mask = _causal_mask()
        sum_dvhat = jnp.zeros((CHUNK, 1), F32)
        sum_dvhat_vhat = jnp.zeros((CHUNK, 1), F32)
        for grp in range(A_GROUPS):
            cols = slice(grp * head, (grp + 1) * head)
            vhat = (v_scr[:, cols] - mu) * rstd
            gain = g_ref[:, cols]
            vn = (vhat * gain + b_ref[:, cols]).astype(BF16)
            wm = _masked_spatial(ws_ref, grp)
            pre_u = pre_ref[:, cols]
            dgated = dg_ref[:, cols]
            ds = dgated * _gelu(pre_u)
            dsb = ds.astype(BF16)
            sg = jnp.dot(wm, vn, preferred_element_type=F32) + bs_ref[grp]
            dpre_ref[:, cols] = (dgated * sg * _gelu_grad(pre_u)).astype(dpre_ref.dtype)
            gws = lax.dot_general(dsb, vn, NT, preferred_element_type=F32)
            gws_ref[grp] += jnp.where(mask, gws, 0.0)
            gbs_ref[grp] += _row_sum(ds)
            dvn = lax.dot_general(wm, dsb, TN, preferred_element_type=F32)
            dvn_scr[:, cols] = dvn
            glg_ref[:, cols] += jnp.sum(dvn * vhat, axis=0, keepdims=True)
            glb_ref[:, cols] += jnp.sum(dvn, axis=0, keepdims=True)
            dvhat = dvn * gain
            sum_dvhat = sum_dvhat + _row_sum(dvhat)
            sum_dvhat_vhat = sum_dvhat_vhat + _row_sum(dvhat * vhat)
        mean_dvhat = sum_dvhat * (1.0 / w)
        mean_dvhat_vhat = sum_dvhat_vhat * (1.0 / w)
        for grp in range(A_GROUPS):
            cols = slice(grp * head, (grp + 1) * head)
            vhat = (v_scr[:, cols] - mu) * rstd
            dvhat = dvn_scr[:, cols] * g_ref[:, cols]
            dv = rstd * (dvhat - mean_dvhat - vhat * mean_dvhat_vhat)
            pre_v = pre_ref[:, w + grp * head:w + (grp + 1) * head]
            dpre_ref[:, w + grp * head:w + (grp + 1) * head] = (dv * _gelu_grad(pre_v)).astype(dpre_ref.dtype)

    vec = pl.BlockSpec((1, w), lambda i: (0, 0))
    ws_spec = pl.BlockSpec((A_GROUPS, CHUNK, CHUNK), lambda i: (0, 0, 0))
    bs_spec = pl.BlockSpec((A_GROUPS, CHUNK, 1), lambda i: (0, 0, 0))
    return _pcall(
        name, body, [pre, dgated, ln_g, ln_b, w_s, b_s_col],
        [pl.BlockSpec((CHUNK, w2), lambda i: (i, 0)), pl.BlockSpec((CHUNK, w), lambda i: (i, 0)),
         vec, vec, ws_spec, bs_spec],
        [_sds((s, w2), BF16), _sds((1, w), F32), _sds((1, w), F32),
         _sds((A_GROUPS, CHUNK, CHUNK), F32), _sds((A_GROUPS, CHUNK, 1), F32)],
        [pl.BlockSpec((CHUNK, w2), lambda i: (i, 0)), vec, vec, ws_spec, bs_spec],
        grid=(s // CHUNK,), sem=("arbitrary",),
        scratch=[pltpu.VMEM((CHUNK, w), F32), pltpu.VMEM((CHUNK, w), F32)], after=after)


def _shift_rows(x, k, forward):
    n = x.shape[0]
    row = lax.broadcasted_iota(jnp.int32, x.shape, 0)
    if forward:
        return jnp.where(row >= k, pltpu.roll(x, k, 0), 0.0)
    return jnp.where(row < n - k, pltpu.roll(x, n - k, 0), 0.0)


def _window_sum(x, window, forward):
    k = 1
    while k < window:
        x = x + _shift_rows(x, k, forward)
        k *= 2
    return x


def _pool(name, v, backward, after=None):
    s, w = v.shape
    head = w // B_GROUPS
    lane = _tile(head, 128)

    def body(v_ref, o_ref):
        grp = pl.program_id(0)
        x = v_ref[...]
        t = lax.broadcasted_iota(jnp.int32, x.shape, 0)
        for idx, window in enumerate(B_WINDOWS):
            @pl.when(grp == idx)
            def _():
                inv_count = 1.0 / jnp.minimum(t + 1, window).astype(F32)
                if backward:
                    out = _window_sum(x * inv_count, window, False) - x
                else:
                    out = _window_sum(x, window, True) * inv_count - x
                o_ref[...] = out.astype(o_ref.dtype)

    per = head // lane
    spec = pl.BlockSpec((s, lane), lambda g, j: (0, g * per + j))
    return _pcall(name, body, [v], [spec], _sds((s, w), BF16), spec, grid=(B_GROUPS, per),
                  sem=("parallel", "parallel"), after=after)


def _colsum(name, a, after=None):
    s, d = a.shape
    tr = _tile(s, 256)

    def body(a_ref, o_ref):
        _accumulate(o_ref, jnp.sum(a_ref[...], axis=0, keepdims=True), pl.program_id(0))

    return _pcall(name, body, [a], [pl.BlockSpec((tr, d), lambda i: (i, 0))], _sds((1, d), F32),
                  pl.BlockSpec((1, d), lambda i: (0, 0)), grid=(s // tr,), sem=("arbitrary",), after=after)


def _adamw(w, g, m, v):
    m = ADAM_B1 * m + (1.0 - ADAM_B1) * g
    v = ADAM_B2 * v + (1.0 - ADAM_B2) * (g * g)
    m_hat = m / (1.0 - ADAM_B1 ** ADAM_STEP)
    v_hat = v / (1.0 - ADAM_B2 ** ADAM_STEP)
    delta = -ADAM_LR * (m_hat / (jnp.sqrt(v_hat) + ADAM_EPS) + ADAM_WD * w)
    return delta, m, v


def _adam_rows(name, g, w, m, v, after=None):
    r, c = g.shape
    tr = _tile(r, 256)

    def body(g_ref, w_ref, m_ref, v_ref, d_ref, nm_ref, nv_ref):
        d_ref[...], nm_ref[...], nv_ref[...] = _adamw(w_ref[...], g_ref[...], m_ref[...], v_ref[...])

    spec = pl.BlockSpec((tr, c), lambda i: (i, 0))
    return _pcall(name, body, [g, w, m, v], [spec] * 4, [_sds((r, c), F32)] * 3, [spec] * 3,
                  grid=(r // tr,), sem=("parallel",), after=after)


def _position():
    return lax.axis_index("x"), lax.axis_index("y"), lax.axis_index("c")


def _other_chips(x, y):
    return [(1 - x, y), (x, 1 - y), (1 - x, 1 - y)]


def _slot(px, py, pc):
    return 4 * px + 2 * py + pc


def _hbm(a):
    return pltpu.with_memory_space_constraint(a, pltpu.HBM)


def _hop1_copies(srcs, lands, send_sems, recv_sems):
    x, y, c = _position()
    peers = [(x, y, 1 - c), (1 - x, y, c), (x, 1 - y, c)]
    mine = _slot(x, y, c)
    return [[pltpu.make_async_remote_copy(
        src_ref=srcs[t], dst_ref=lands[t].at[mine], send_sem=send_sems[t].at[k], recv_sem=recv_sems[t].at[k],
        device_id=peer, device_id_type=MESH) for k, peer in enumerate(peers)] for t in range(len(srcs))]


def _hop2_copies(lands, send_sems, recv_sems):
    x, y, c = _position()
    routes = [(_slot(1 - x, y, c), (x, 1 - y, c)), (_slot(x, 1 - y, c), (1 - x, y, c))]
    out = []
    for t in range(len(lands)):
        rows = lands[t].shape[1]
        halves = [(0, rows // 2), (rows // 2, rows - rows // 2)]
        per_tensor = []
        for h, ((slot, peer), (start, size)) in enumerate(zip(routes, halves)):
            if size:
                block = lands[t].at[slot, pl.ds(start, size)]
                per_tensor.append(pltpu.make_async_remote_copy(
                    src_ref=block, dst_ref=block, send_sem=send_sems[t].at[h], recv_sem=recv_sems[t].at[h],
                    device_id=peer, device_id_type=MESH))
        for j, (slot, _) in enumerate(routes):
            block = lands[t].at[slot]
            per_tensor.append(pltpu.make_async_remote_copy(
                src_ref=block, dst_ref=block, send_sem=send_sems[t].at[2 + j], recv_sem=recv_sems[t].at[2 + j],
                device_id=(x, y, 1 - c), device_id_type=MESH))
        out.append(per_tensor)
    return out


def _split_start(name, srcs, lands, copies, n_sems, after=None):
    n = len(srcs)
    order = [] if after is None else [after]
    n_in = 2 * n + len(order)

    def body(*refs):
        for per_tensor in copies(refs[:n], refs[n:2 * n], refs[n_in:n_in + n], refs[n_in + n:n_in + 2 * n]):
            for cp in per_tensor:
                cp.start()
        refs[-1][...] = jnp.zeros_like(refs[-1])

    out_shape = ([pltpu.SemaphoreType.DMA((n_sems,)) for _ in range(2 * n)]
                 + [pltpu.HBM(a.shape, a.dtype) for a in list(srcs) + list(lands)]
                 + [_sds((8, 128), F32)])
    out = pl.pallas_call(
        body, name=name, out_shape=out_shape, in_specs=[_HBM] * (2 * n) + [_ANY] * len(order),
        out_specs=[_SEM] * (2 * n) + [_HBM] * (2 * n) + [pl.BlockSpec(memory_space=pltpu.VMEM)],
        input_output_aliases={i: 2 * n + i for i in range(2 * n)},
        compiler_params=pltpu.CompilerParams(has_side_effects=_EFFECT),
    )(*[_hbm(a) for a in srcs], *[_hbm(a) for a in lands], *order)
    return [(out[t], out[n + t], out[2 * n + t], out[3 * n + t]) for t in range(n)], out[-1]


def _split_wait(name, started, copies, after):
    n = len(started)

    def body(*refs):
        for per_tensor in copies(refs[:n], refs[n:2 * n], refs[2 * n:3 * n], refs[3 * n:4 * n]):
            for cp in per_tensor:
                cp.wait_send()
                cp.wait_recv()

    srcs = [e[2] for e in started]
    lands = [e[3] for e in started]
    out = pl.pallas_call(
        body, name=name, out_shape=[pltpu.HBM(a.shape, a.dtype) for a in srcs + lands],
        in_specs=[_HBM] * (2 * n) + [_SEM] * (2 * n) + [_ANY], out_specs=[_HBM] * (2 * n),
        input_output_aliases={i: i for i in range(2 * n)},
        compiler_params=pltpu.CompilerParams(has_side_effects=_EFFECT),
    )(*srcs, *lands, *[e[0] for e in started], *[e[1] for e in started], after)
    return out[:n], out[n:]


def _gather_step(name, arrived, fresh, after=None):
    n, m = len(arrived), len(fresh)
    order = [] if after is None else [after]
    fresh_lands = [lax.empty((N_DEV,) + s.shape, s.dtype) for s in fresh]
    buffers = [e[2] for e in arrived] + [e[3] for e in arrived] + list(fresh) + fresh_lands
    old_sems = [e[0] for e in arrived] + [e[1] for e in arrived]
    n_buf, n_old = len(buffers), len(old_sems)
    first_new = n_buf + n_old + len(order)

    def body(*refs):
        bufs, old = refs[:n_buf], refs[n_buf:n_buf + n_old]
        new = refs[first_new:first_new + 2 * n + 2 * m]
        for per_tensor in _hop1_copies(bufs[:n], bufs[n:2 * n], old[:n], old[n:]):
            for cp in per_tensor:
                cp.wait_send()
                cp.wait_recv()
        second = _hop2_copies(bufs[n:2 * n], new[:n], new[n:2 * n])
        first = _hop1_copies(bufs[2 * n:2 * n + m], bufs[2 * n + m:], new[2 * n:2 * n + m], new[2 * n + m:])
        for per_tensor in second + first:
            for cp in per_tensor:
                cp.start()
        refs[-1][...] = jnp.zeros_like(refs[-1])

    n_new = 2 * n + 2 * m
    out_shape = ([pltpu.SemaphoreType.DMA((4,)) for _ in range(2 * n)]
                 + [pltpu.SemaphoreType.DMA((3,)) for _ in range(2 * m)]
                 + [pltpu.HBM(a.shape, a.dtype) for a in buffers] + [_sds((8, 128), F32)])
    out = pl.pallas_call(
        body, name=name, out_shape=out_shape,
        in_specs=[_HBM] * n_buf + [_SEM] * n_old + [_ANY] * len(order),
        out_specs=[_SEM] * n_new + [_HBM] * n_buf + [pl.BlockSpec(memory_space=pltpu.VMEM)],
        input_output_aliases={i: n_new + i for i in range(n_buf)},
        compiler_params=pltpu.CompilerParams(has_side_effects=_EFFECT),
    )(*[_hbm(a) for a in buffers], *old_sems, *order)
    sems, bufs = out[:n_new], out[n_new:n_new + n_buf]
    second = [(sems[t], sems[n + t], bufs[t], bufs[n + t]) for t in range(n)]
    first = [(sems[2 * n + t], sems[2 * n + m + t], bufs[2 * n + t], bufs[2 * n + m + t]) for t in range(m)]
    return second, first, out[-1]


def _gather_wait(name, second, after):
    return _split_wait(name, second, lambda srcs, lands, send, recv: _hop2_copies(lands, send, recv), after)


def _sibling_copies(srcs, lands, send_sems, recv_sems):
    x, y, c = _position()
    return [[pltpu.make_async_remote_copy(
        src_ref=srcs[t], dst_ref=lands[t], send_sem=send_sems[t].at[0], recv_sem=recv_sems[t].at[0],
        device_id=(x, y, 1 - c), device_id_type=MESH)] for t in range(len(srcs))]


def _sibling_and_scatter_start(name, arrays, partials):
    k = len(arrays)

    def copies(srcs, lands, send_sems, recv_sems):
        return (_sibling_copies(srcs[:k], lands[:k], send_sems[:k], recv_sems[:k])
                + _scatter_copies(srcs[k:], lands[k:], send_sems[k:], recv_sems[k:]))

    lands = ([lax.empty(a.shape, a.dtype) for a in arrays]
             + [lax.empty((N_CHIPS - 1,) + p.shape[1:], p.dtype) for p in partials])
    return _split_start(name, list(arrays) + list(partials), lands, copies, 3)


def _sibling_wait(name, started, after):
    return _split_wait(name, started, _sibling_copies, after)[1]


def _small_copies(srcs, lands, send_sems, recv_sems):
    x, y, c = _position()
    mine = _slot(x, y, c)
    peers = [(x ^ ((k >> 2) & 1), y ^ ((k >> 1) & 1), c ^ (k & 1)) for k in range(1, N_DEV)]
    return [[pltpu.make_async_remote_copy(
        src_ref=srcs[t], dst_ref=lands[t].at[mine], send_sem=send_sems[t].at[k], recv_sem=recv_sems[t].at[k],
        device_id=peer, device_id_type=MESH) for k, peer in enumerate(peers)] for t in range(len(srcs))]


def _gather_finish(name, shards, lands, after):
    n = len(shards)

    def body(*refs):
        srcs, lands_in, outs = refs[:n], refs[n:2 * n], refs[2 * n:3 * n]
        send_sems, recv_sems, local_sems = refs[3 * n:]
        x, y, c = _position()
        local = [pltpu.make_async_copy(srcs[t], outs[t].at[_slot(x, y, c)], local_sems.at[t]) for t in range(n)]

        def diagonal(t, core):
            block = outs[t].at[_slot(1 - x, 1 - y, core)]
            return pltpu.make_async_remote_copy(
                src_ref=block, dst_ref=block, send_sem=send_sems.at[t], recv_sem=recv_sems.at[t],
                device_id=(x, y, 1 - c), device_id_type=MESH)

        for cp in local:
            cp.start()
        for t in range(n):
            diagonal(t, c).start()
        for t in range(n):
            diagonal(t, c).wait_send()
            diagonal(t, 1 - c).wait_recv()
        for cp in local:
            cp.wait()

    return _pcall(name, body, [*shards, *lands], [_ANY] * (2 * n),
                  [_sds(l.shape, l.dtype) for l in lands], [_ANY] * n,
                  scratch=[pltpu.SemaphoreType.DMA((n,)), pltpu.SemaphoreType.DMA((n,)),
                           pltpu.SemaphoreType.DMA((n,))],
                  after=after, aliases={n + t: t for t in range(n)})


def _exchange_sibling(name, fulls, after):
    n = len(fulls)

    def body(*refs):
        src = refs[:n]
        out = refs[n:2 * n]
        send_sems, recv_sems = refs[2 * n:]
        x, y, c = _position()
        copies = [pltpu.make_async_remote_copy(
            src_ref=src[t].at[:, 1 - c], dst_ref=out[t], send_sem=send_sems.at[t], recv_sem=recv_sems.at[t],
            device_id=(x, y, 1 - c), device_id_type=MESH) for t in range(n)]
        for cp in copies:
            cp.start()
        for cp in copies:
            cp.wait()

    return _pcall(name, body, fulls, [_ANY] * n, [_sds((N_CHIPS,) + f.shape[2:], f.dtype) for f in fulls],
                  [_ANY] * n, scratch=[pltpu.SemaphoreType.DMA((n,)), pltpu.SemaphoreType.DMA((n,))],
                  after=after)


def _add_sibling(name, full, recv, core, after):
    _, _, r, c = full.shape
    tr = _tile(r, max(8, (256 * 1024) // c))

    def body(core_ref, f_ref, r_ref, o_ref):
        o_ref[...] = (f_ref[...].astype(F32) + r_ref[...].astype(F32)).astype(o_ref.dtype)

    return _pcall(
        name, body, [full, recv],
        [pl.BlockSpec((None, None, tr, c), lambda p, i, core_ref: (p, core_ref[0], i, 0)),
         pl.BlockSpec((None, tr, c), lambda p, i, core_ref: (p, i, 0))],
        _sds((N_CHIPS, r, c), BF16), pl.BlockSpec((None, tr, c), lambda p, i, core_ref: (p, i, 0)),
        grid=(N_CHIPS, r // tr), sem=("parallel", "parallel"), prefetch=[core], after=after)


def _scatter_copies(srcs, lands, send_sems, recv_sems):
    x, y, c = _position()
    return [[pltpu.make_async_remote_copy(
        src_ref=srcs[t].at[2 * px + py], dst_ref=lands[t].at[j],
        send_sem=send_sems[t].at[j], recv_sem=recv_sems[t].at[j],
        device_id=(px, py, c), device_id_type=MESH) for j, (px, py) in enumerate(_other_chips(x, y))]
        for t in range(len(srcs))]


def _scatter_wait(name, started, after):
    return _split_wait(name, started, _scatter_copies, after)


class _Job:
    def __init__(self, **fields):
        self.__dict__.update(fields)


def _reduce_adam_job(partial, recv, chip, w, m, v, layer, carried):
    n_layers, r, c = w.shape
    tr = _tile(r, max(8, (256 * 1024) // c))

    def body(p_ref, r_ref, w_ref, m_ref, v_ref, *rest):
        g_ref, d_ref, nm_ref, nv_ref = rest[-4:]
        g = p_ref[...].astype(F32)
        for j in range(N_CHIPS - 1):
            g = g + r_ref[j].astype(F32)
        g_ref[...] = g
        d_ref[...], nm_ref[...], nv_ref[...] = _adamw(w_ref[...], g, m_ref[...], v_ref[...])

    layered = ((None, tr, c), lambda blk, chip_ref: (layer, blk, 0))
    in_specs = [((None, tr, c), lambda blk, chip_ref: (chip_ref[0], blk, 0)),
                ((N_CHIPS - 1, tr, c), lambda blk, chip_ref: (0, blk, 0)), layered, layered, layered]
    operands = [partial, recv, w, m, v]
    aliases = {}
    if carried is not None:
        operands += list(carried)
        in_specs += [None] * 4
        aliases = {5 + o: o for o in range(4)}
    return _Job(operands=operands, in_specs=in_specs, out_shape=[_sds((n_layers, r, c), F32)] * 4,
                out_specs=[layered] * 4, body=body, aliases=aliases, prefetch=chip, n_blocks=r // tr)


def _run_job(name, job, after):
    def spec(entry):
        if entry is None:
            return _ANY
        shape, index = entry
        return pl.BlockSpec(shape, lambda blk, pre, index=index: index(blk, pre))

    def body(pre_ref, *refs):
        job.body(*refs)

    return _pcall(name, body, job.operands, [spec(e) for e in job.in_specs], job.out_shape,
                  [spec(e) for e in job.out_specs], grid=(job.n_blocks,), sem=("parallel",),
                  prefetch=[job.prefetch], after=after,
                  aliases={1 + i: o for i, o in job.aliases.items()})


def _small_sum(name, gathered, own, device, after=None):
    r, lanes = own.shape

    def body(dev_ref, g_ref, own_ref, out_ref):
        dev = dev_ref[0]
        mine = own_ref[...]
        total = jnp.where(dev == 0, mine, g_ref[0])
        for d in range(1, N_DEV):
            total = total + jnp.where(dev == d, mine, g_ref[d])
        out_ref[...] = total

    return _pcall(name, body, [gathered, own],
                  [pl.BlockSpec((N_DEV, r, lanes), lambda i, dev_ref: (0, 0, 0)),
                   pl.BlockSpec((r, lanes), lambda i, dev_ref: (0, 0))],
                  _sds((r, lanes), F32), pl.BlockSpec((r, lanes), lambda i, dev_ref: (0, 0)),
                  grid=(1,), sem=("arbitrary",), prefetch=[device], after=after)


def _pack(arrays):
    return jnp.concatenate([a.reshape(-1, 128) for a in arrays], axis=0)


def _unpack(packed, shapes):
    out, row = [], 0
    for shape in shapes:
        rows = math.prod(shape) // 128
        out.append(packed[row:row + rows].reshape(shape))
        row += rows
    return out


class _Order:
    def __init__(self):
        self.last = None

    def __call__(self, fn, *args, **kwargs):
        out = fn(*args, after=self.last, **kwargs)
        self.last = out[0] if isinstance(out, (list, tuple)) else out
        return out


def kernel(x, a_w_in, a_ln_g, a_ln_b, a_w_s, a_b_s, a_w_out, b_w_in, b_w_grp, b_scale, b_w_out, norm_mix, norm_mlp, mlp_w1, mlp_w2, final_norm, loss_target, m_a_w_in, m_a_ln_g, m_a_ln_b, m_a_w_s, m_a_b_s, m_a_w_out, m_b_w_in, m_b_w_grp, m_b_scale, m_b_w_out, m_norm_mix, m_norm_mlp, m_mlp_w1, m_mlp_w2, m_final_norm, v_a_w_in, v_a_ln_g, v_a_ln_b, v_a_w_s, v_a_b_s, v_a_w_out, v_b_w_in, v_b_w_grp, v_b_scale, v_b_w_out, v_norm_mix, v_norm_mlp, v_mlp_w1, v_mlp_w2, v_final_norm):
    s, d = x.shape[1], x.shape[2]
    depth = mlp_w1.shape[0]
    a_slab = a_w_in.shape[2]
    ff_slab = mlp_w1.shape[2]
    ff_rows = mlp_w2.shape[1]
    bh = b_w_grp.shape[3]
    my_x, my_y, my_c = _position()
    core = jnp.reshape(my_c, (1,)).astype(jnp.int32)
    chip = jnp.reshape(2 * my_x + my_y, (1,)).astype(jnp.int32)
    device = _slot(my_x, my_y, my_c)
    run = _Order()

    w1_b, w2_b = mlp_w1.astype(BF16), mlp_w2.astype(BF16)
    shards = [a_w_in[0].astype(BF16), a_w_out[0].astype(BF16), b_scale,
              w1_b[0], w2_b[0],
              b_w_in[0].astype(BF16), b_w_grp[0].astype(BF16), b_w_out[0].astype(BF16),
              w1_b[1], w2_b[1]]
    groups = [[0], [1, 2], [3], [4], [5, 6, 7], [8], [9]]
    start_with = {1: [2, 3], 2: [4], 3: [5], 4: [6]}
    hop1, hop2 = {}, {}
    _, hop1[0], token = _gather_step("weights_group0_hop1", [], [shards[t] for t in groups[0]])
    _, hop1[1], token = _gather_step("weights_group1_hop1", [], [shards[t] for t in groups[1]], token)
    run.last = token

    def advance(g):
        if g not in hop1:
            return
        ahead = start_with.get(g, [])
        fresh = [shards[t] for a in ahead for t in groups[a]]
        hop2[g], started, tok = _gather_step(f"weights_group{g}_hop2", hop1.pop(g), fresh, run.last)
        for a in ahead:
            hop1[a], started = started[:len(groups[a])], started[len(groups[a]):]
        run.last = tok

    def gathered(g):
        advance(g)
        if g == 0:
            advance(1)
        srcs, lands = _gather_wait(f"weights_group{g}_wait", hop2.pop(g), run.last)
        run.last = srcs[0]
        return run(_gather_finish, f"weights_group{g}_finish", srcs, lands)

    h0 = x[0]
    target = loss_target[0]
    ln_g, ln_b = a_ln_g, a_ln_b
    w_s = a_w_s[0]
    b_s_col = a_b_s[0][:, :, None]
    nmix = [norm_mix[l][None, :] for l in range(depth)]
    nmlp = [norm_mlp[l][None, :] for l in range(depth)]

    def mlp_forward(l, h, up_group):
        hn = run(_rms_fwd, f"mlp{l}_norm", h, nmlp[l])
        (w1,) = gathered(up_group)
        advance(up_group + 1)
        act, act_sq = run(_mm_nn, f"mlp{l}_up", hn, w1,
                          lambda acc: (jnp.maximum(acc, 0.0), jnp.square(jnp.maximum(acc, 0.0))),
                          (BF16, BF16), slab=True)
        (w2,) = gathered(up_group + 1)
        advance(up_group + 2)
        w2 = w2.reshape(-1, d)
        (h_out,) = run(_mm_nn, f"mlp{l}_down", act_sq, w2, lambda acc, res: (acc + res,), (F32,),
                       extras=(h,), extra_kinds=("tile",))
        return h_out, (h, hn, act, act_sq, w1, w2)

    scattered = []

    pending = []

    def scatter_partials(name, partials, specs):
        pending.append((name, partials, specs))

    def start_exchanges(name, to_sibling):
        partials = [p for _, group, _ in pending for p in group]
        in_flight, tok = _sibling_and_scatter_start(name, to_sibling, partials)
        run.last = tok
        first = len(to_sibling)
        for group_name, group, specs in pending:
            scattered.append((group_name, in_flight[first:first + len(group)], specs))
            first += len(group)
        pending.clear()
        return in_flight[:len(to_sibling)]

    weights = {"a_w_in": (a_w_in, m_a_w_in, v_a_w_in), "a_w_out": (a_w_out, m_a_w_out, v_a_w_out),
               "b_w_in": (b_w_in, m_b_w_in, v_b_w_in), "b_w_grp": (b_w_grp, m_b_w_grp, v_b_w_grp),
               "b_w_out": (b_w_out, m_b_w_out, v_b_w_out), "mlp_w1": (mlp_w1, m_mlp_w1, v_mlp_w1),
               "mlp_w2": (mlp_w2, m_mlp_w2, v_mlp_w2)}
    results = {}

    def finish_group(name, in_flight, specs):
        partials, lands = _scatter_wait(name + "_scatter_wait", in_flight, run.last)
        run.last = lands[0]
        for t, (wname, layer) in enumerate(specs):
            w, m, v = weights[wname]
            shape = (w.shape[0],) + partials[t].shape[1:]
            job = _reduce_adam_job(partials[t], lands[t], chip, w.reshape(shape), m.reshape(shape),
                                   v.reshape(shape), layer, results.get(wname))
            results[wname] = run(_run_job, f"{name}_reduce_adam_{t}", job)

    def weight_grad(name, a, b, by_rows, block, between, tm=1024, tn=1024):
        other = run(_mm_tn_half, name + "_other", a, b, core, False, by_rows, block, tm=tm, tn=tn)
        sent = start_exchanges(name + "_sibling_start", [other])
        middle = between()
        (recv,) = _sibling_wait(name + "_sibling_wait", sent, run.last)
        run.last = recv
        return run(_mm_tn_half, name + "_own", a, b, core, True, by_rows, block, recv=recv, tm=tm, tn=tn), middle

    def mlp_backward(l, saved, dh, dhb):
        h, hn, act, act_sq, w1, w2 = saved
        part_w2, (dpre,) = weight_grad(
            f"mlp{l}_down_dw", act_sq, dhb, True, ff_rows,
            lambda: run(_mm_nt, f"mlp{l}_down_dx", dhb, w2, lambda acc, a: (2.0 * a.astype(F32) * acc,),
                        (BF16,), extras=(act,), extra_kinds=("tile",)))
        scatter_partials(f"mlp{l}_down_grads", [part_w2], [("mlp_w2", l)])
        part_w1, (dhn,) = weight_grad(
            f"mlp{l}_up_dw", hn, dpre, False, ff_slab,
            lambda: run(_mm_nt, f"mlp{l}_up_dx", dpre, w1, lambda acc: (acc,), (F32,), slab=True))
        scatter_partials(f"mlp{l}_up_grads", [part_w1], [("mlp_w1", l)])
        dh, dhb, g_norm = run(_rms_bwd, f"mlp{l}_norm_bwd", dhn, h, nmlp[l], dh)
        return dh, dhb, g_norm

    hn0 = run(_rms_fwd, "mix0_norm", h0, nmix[0])
    (wa_in,) = gathered(0)
    (pre,) = run(_mm_nn, "mixa_in", hn0, wa_in, lambda acc: (acc,), (F32,), slab=True)
    wa_out, scale = gathered(1)
    wa_out, scale = wa_out.reshape(d, d), scale.reshape(1, d)
    gated = run(_amix_fwd, "mixa_gate", pre, ln_g, ln_b, w_s, b_s_col)
    advance(2)
    (h1,) = run(_mm_nn, "mixa_out", gated, wa_out, lambda acc, res: (acc + res,), (F32,),
                extras=(h0,), extra_kinds=("tile",))
    h2, saved_mlp0 = mlp_forward(0, h1, 2)
    hn2 = run(_rms_fwd, "mix1_norm", h2, nmix[1])
    wb_in, wb_grp, wb_out = gathered(4)
    advance(5)
    wb_in, wb_out = wb_in.reshape(d, d), wb_out.reshape(d, d)
    wb_grp = jnp.transpose(wb_grp, (1, 0, 2, 3)).reshape(B_GROUPS, bh, bh)
    (vb,) = run(_mm_nn, "mixb_in", hn2, wb_in, lambda acc: (acc,), (F32,))
    pooled = run(_pool, "mixb_pool", vb, backward=False)
    tm = _tile(s, 1024)
    grp_tile = pl.BlockSpec((tm, bh), lambda i, j, k: (i, j))
    grp_weight = pl.BlockSpec((None, bh, bh), lambda i, j, k: (j, 0, 0))
    mixed, mixed_scaled = run(
        _matmul, "mixb_grp", pooled, wb_grp, NN, (s // tm, B_GROUPS, 1), grp_tile, grp_weight,
        [_sds((s, d), BF16), _sds((s, d), BF16)], [grp_tile] * 2,
        (tm, bh), lambda acc, sc: (acc, acc * sc), (scale,), [pl.BlockSpec((1, bh), lambda i, j, k: (0, j))])
    (h3,) = run(_mm_nn, "mixb_out", mixed_scaled, wb_out, lambda acc, res: (acc + res,), (F32,),
                extras=(h2,), extra_kinds=("tile",))
    h4, saved_mlp1 = mlp_forward(1, h3, 5)
    dh, dhb, g_final, loss_part = run(_loss_head, "loss_head", h4, final_norm[None, :], target)

    dh, dhb, g_nmlp1 = mlp_backward(1, saved_mlp1, dh, dhb)
    tks = _tile(s, 1024)
    grp_rows = pl.BlockSpec((tks, bh), lambda i, j, k: (k, j))

    def mixb_middle():
        dms_scaled, dms_mixed = run(
            _mm_nt, "mixb_out_dx", dhb, wb_out,
            lambda acc, sc, mx: (acc * sc, acc * mx.astype(F32)), (BF16, F32),
            extras=(scale, mixed), extra_kinds=("row", "tile"))
        g_scale = run(_colsum, "mixb_scale_dw", dms_mixed)
        (g_wb_grp,) = run(
            _matmul, "mixb_grp_dw", pooled, dms_scaled, TN, (1, B_GROUPS, s // tks), grp_rows, grp_rows,
            [_sds((B_GROUPS, bh, bh), BF16)], [grp_weight], (bh, bh), lambda acc: (acc,))
        (dpooled,) = run(
            _matmul, "mixb_grp_dx", dms_scaled, wb_grp, NT, (s // tm, B_GROUPS, 1), grp_tile, grp_weight,
            [_sds((s, d), F32)], [grp_tile], (tm, bh), lambda acc: (acc,))
        return g_scale, g_wb_grp, run(_pool, "mixb_pool_bwd", dpooled, backward=True)

    part_wb_out, (g_scale, g_wb_grp, dvb) = weight_grad("mixb_out_dw", mixed_scaled, dhb, True, d // N_DEV,
                                                        mixb_middle, tn=d)
    part_wb_in, (dhn2,) = weight_grad(
        "mixb_in_dw", hn2, dvb, True, d // N_DEV,
        lambda: run(_mm_nt, "mixb_in_dx", dvb, wb_in, lambda acc: (acc,), (F32,)), tn=d)
    grp_full = jnp.transpose(g_wb_grp.reshape(B_GROUPS, N_DEV, bh // N_DEV, bh), (1, 0, 2, 3))
    grp_full = grp_full.reshape(N_CHIPS, 2, B_GROUPS * bh // N_DEV, bh)
    (grp_sibling,) = run(_exchange_sibling, "mixb_grp_dw_to_sibling", [grp_full])
    part_wb_grp = run(_add_sibling, "mixb_grp_dw_add_sibling", grp_full, grp_sibling, core)
    scatter_partials("mixb_grads", [part_wb_out, part_wb_grp, part_wb_in],
                     [("b_w_out", 0), ("b_w_grp", 0), ("b_w_in", 0)])
    dh, dhb, g_nmix1 = run(_rms_bwd, "mix1_norm_bwd", dhn2, h2, nmix[1], dh)
    dh, dhb, g_nmlp0 = mlp_backward(0, saved_mlp0, dh, dhb)
    def mixa_middle():
        (dgated,) = run(_mm_nt, "mixa_out_dx", dhb, wa_out, lambda acc: (acc,), (F32,))
        return run(_amix_bwd, "mixa_gate_bwd", pre, dgated, ln_g, ln_b, w_s, b_s_col)

    part_wa_out, (dpre, g_ln_g, g_ln_b, g_w_s, g_b_s) = weight_grad("mixa_out_dw", gated, dhb, True, d // N_DEV,
                                                                     mixa_middle, tn=d)
    part_wa_in, (dhn0,) = weight_grad(
        "mixa_in_dw", hn0, dpre, False, a_slab,
        lambda: run(_mm_nt, "mixa_in_dx", dpre, wa_in, lambda acc: (acc,), (F32,), slab=True), tm=d)
    scatter_partials("mixa_grads", [part_wa_in, part_wa_out], [("a_w_in", 0), ("a_w_out", 0)])
    start_exchanges("mixa_grads_scatter_start", [])
    grad_x, _, g_nmix0 = run(_rms_bwd, "mix0_norm_bwd", dhn0, h0, nmix[0], dh)

    g_norm_mix = jnp.concatenate([g_nmix0, g_nmix1], axis=0)
    g_norm_mlp = jnp.concatenate([g_nmlp0, g_nmlp1], axis=0)
    loss_row = jnp.pad(loss_part, ((0, 0), (0, 127)))
    small_parts = [g_ln_g, g_ln_b, g_w_s, g_b_s, g_norm_mix, g_norm_mlp, g_final, g_scale, loss_row]
    packed = _pack(small_parts)
    small_sent, tok = _split_start("small_grads_start", [packed], [lax.empty((N_DEV,) + packed.shape, F32)],
                                   _small_copies, N_DEV - 1)
    run.last = tok

    for group in scattered[:-1]:
        finish_group(*group)
    own_packed, small_gathered = _split_wait("small_grads_wait", small_sent, _small_copies, run.last)
    run.last = small_gathered[0]
    small_sum = run(_small_sum, "small_grads_sum", small_gathered[0], own_packed[0],
                    jnp.reshape(device, (1,)).astype(jnp.int32))
    sg = _unpack(small_sum, [a_ln_g.shape, a_ln_b.shape, a_w_s.shape, a_b_s.shape, norm_mix.shape,
                             norm_mlp.shape, final_norm.shape, (1, d), (1, 128)])
    loss = sg.pop()[0, 0]
    shard = b_scale.shape[1]
    sg[7] = lax.dynamic_slice(sg[7], (0, device * shard), (1, shard))
    small_w = [a_ln_g, a_ln_b, a_w_s, a_b_s, norm_mix, norm_mlp, final_norm, b_scale]
    small_m = [m_a_ln_g, m_a_ln_b, m_a_w_s, m_a_b_s, m_norm_mix, m_norm_mlp, m_final_norm, m_b_scale]
    small_v = [v_a_ln_g, v_a_ln_b, v_a_w_s, v_a_b_s, v_norm_mix, v_norm_mlp, v_final_norm, v_b_scale]
    small_out = run(_adam_rows, "small_adam", _pack(sg), _pack(small_w), _pack(small_m), _pack(small_v))
    shapes = [w.shape for w in small_w]
    small_res = [sg] + [_unpack(o, shapes) for o in small_out]

    finish_group(*scattered[-1])
    big = {wname: [o.reshape(weights[wname][0].shape) for o in outs] for wname, outs in results.items()}

    def leaf(o):
        return (big["a_w_in"][o], small_res[o][0], small_res[o][1], small_res[o][2], small_res[o][3],
                big["a_w_out"][o], big["b_w_in"][o], big["b_w_grp"][o], small_res[o][7], big["b_w_out"][o],
                small_res[o][4], small_res[o][5], big["mlp_w1"][o], big["mlp_w2"][o], small_res[o][6])

    return (loss, grad_x[None], *leaf(0), *leaf(1), *leaf(2), *leaf(3))
```

```python
import math

import jax
import jax.numpy as jnp
from jax import lax
from jax.experimental import pallas as pl
from jax.experimental.pallas import tpu as pltpu

F32 = jnp.float32
BF16 = jnp.bfloat16
MESH = pl.DeviceIdType.MESH

N_DEV = 8
N_CHIPS = 4
CHUNK = 128
A_GROUPS = 8
B_WINDOWS = (2, 4, 8, 16)
B_GROUPS = len(B_WINDOWS)
EPS = 1e-6
ADAM_LR = 0.001
ADAM_B1 = 0.9
ADAM_B2 = 0.999
ADAM_EPS = 1e-08
ADAM_WD = 0.01
ADAM_STEP = 10

VMEM_LIMIT = 48 * 1024 * 1024
ROW_CHUNK = 16

NN = (((1,), (0,)), ((), ()))
NT = (((1,), (1,)), ((), ()))
TN = (((0,), (0,)), ((), ()))

_ANY = pl.BlockSpec(memory_space=pl.ANY)
_HBM = pl.BlockSpec(memory_space=pltpu.HBM)
_SEM = pl.BlockSpec(memory_space=pltpu.SEMAPHORE)
_EFFECT = pltpu.SideEffectType.DATAFLOW_SIDE_EFFECTING


def _tile(n, pref):
    return pref if n % pref == 0 else n


def _sds(shape, dtype):
    return jax.ShapeDtypeStruct(shape, dtype)


def _pcall(name, body, operands, in_specs, out_shape, out_specs, *, grid=None, sem=None, scratch=(),
           prefetch=(), after=None, aliases=None):
    after = [] if after is None else [after]
    n_lead = len(prefetch) + len(operands)
    n_after = len(after)

    def wrapped(*refs):
        body(*refs[:n_lead], *refs[n_lead + n_after:])

    in_specs = list(in_specs) + [_ANY] * n_after
    params = pltpu.CompilerParams(vmem_limit_bytes=VMEM_LIMIT) if sem is None else \
        pltpu.CompilerParams(dimension_semantics=sem, vmem_limit_bytes=VMEM_LIMIT)
    kwargs = dict(out_shape=out_shape, scratch_shapes=list(scratch), compiler_params=params, name=name,
                  input_output_aliases=aliases or {})
    if prefetch:
        kwargs["grid_spec"] = pltpu.PrefetchScalarGridSpec(
            num_scalar_prefetch=len(prefetch), grid=grid, in_specs=in_specs, out_specs=out_specs,
            scratch_shapes=list(scratch))
        kwargs.pop("scratch_shapes")
    else:
        kwargs.update(in_specs=in_specs, out_specs=out_specs)
        if grid is not None:
            kwargs["grid"] = grid
    return pl.pallas_call(wrapped, **kwargs)(*prefetch, *operands, *after)


def _matmul(name, a, b, dims, grid, a_spec, b_spec, out_shape, out_specs, acc_shape,
            epilogue, extras=(), extra_specs=(), after=None, prefetch=(), b_parts=1):
    nk = grid[2]
    n_extra = len(extras)
    n_out = len(out_shape)
    n_pre = len(prefetch)

    def body(*refs):
        refs = refs[n_pre:]
        a_ref, b_ref = refs[0], refs[1]
        extra_refs = refs[2:2 + n_extra]
        out_refs = refs[2 + n_extra:2 + n_extra + n_out]

        def finish(acc):
            outs = epilogue(acc, *[r[...] for r in extra_refs])
            for o_ref, o in zip(out_refs, outs):
                o_ref[...] = o.astype(o_ref.dtype)

        def product():
            if b_parts == 1:
                return lax.dot_general(a_ref[...], b_ref[...], dims, preferred_element_type=F32)
            width = b_ref.shape[2]
            total = None
            for p in range(b_parts):
                part = lax.dot_general(a_ref[:, p * width:(p + 1) * width], b_ref[p], dims,
                                       preferred_element_type=F32)
                total = part if total is None else total + part
            return total

        if nk == 1:
            finish(product())
        else:
            acc_ref = refs[-1]
            k = pl.program_id(2)

            @pl.when(k == 0)
            def _():
                acc_ref[...] = product()

            if nk > 2:
                @pl.when(jnp.logical_and(k > 0, k < nk - 1))
                def _():
                    acc_ref[...] += product()

            @pl.when(k == nk - 1)
            def _():
                finish(acc_ref[...] + product())

    scratch = [] if nk == 1 else [pltpu.VMEM(acc_shape, F32)]
    return _pcall(name, body, [a, b, *extras], [a_spec, b_spec, *extra_specs], out_shape, out_specs,
                  grid=grid, sem=("parallel", "parallel", "arbitrary"), scratch=scratch, after=after,
                  prefetch=prefetch)


def _mm_nn(name, a, b, epilogue, out_dtypes, extras=(), extra_kinds=(), slab=False, after=None,
           tm=1024, tn=1024, tk=2048):
    m, kd = a.shape
    if slab:
        n_slab, _, w = b.shape
        n = n_slab * w
        tn = _tile(w, min(tn, w))
        per = w // tn
        tk = _tile(kd, tk)
        b_spec = pl.BlockSpec((None, tk, tn), lambda i, j, k: (j // per, k, j % per))
    else:
        n = b.shape[1]
        tn = _tile(n, tn)
        tk = _tile(kd, tk)
        b_spec = pl.BlockSpec((tk, tn), lambda i, j, k: (k, j))
    tm = _tile(m, tm)
    grid = (m // tm, n // tn, kd // tk)
    a_spec = pl.BlockSpec((tm, tk), lambda i, j, k: (i, k))
    tile_spec = pl.BlockSpec((tm, tn), lambda i, j, k: (i, j))
    row_spec = pl.BlockSpec((1, tn), lambda i, j, k: (0, j))
    extra_specs = [tile_spec if kind == "tile" else row_spec for kind in extra_kinds]
    return _matmul(name, a, b, NN, grid, a_spec, b_spec,
                   [_sds((m, n), d) for d in out_dtypes], [tile_spec for _ in out_dtypes],
                   (tm, tn), epilogue, extras, extra_specs, after=after)


def _mm_nt(name, a, b, epilogue, out_dtypes, extras=(), extra_kinds=(), slab=False, after=None,
           tm=1024, tn=1024, tk=2048):
    m, kd = a.shape
    parts = 1
    if slab:
        n_slab, n, w = b.shape
        tn = _tile(n, tn)
        if tk > w and tk % w == 0 and n_slab % (tk // w) == 0:
            parts = tk // w
            b_spec = pl.BlockSpec((parts, tn, w), lambda i, j, k, *_: (k, j, 0))
        else:
            tk = _tile(w, min(tk, w))
            per = w // tk
            b_spec = pl.BlockSpec((None, tn, tk), lambda i, j, k, *_: (k // per, j, k % per))
    else:
        n = b.shape[0]
        tn = _tile(n, tn)
        tk = _tile(kd, tk)
        b_spec = pl.BlockSpec((tn, tk), lambda i, j, k, *_: (j, k))
    tm = _tile(m, tm)
    grid = (m // tm, n // tn, kd // tk)
    a_spec = pl.BlockSpec((tm, tk), lambda i, j, k, *_: (i, k))
    tile_spec = pl.BlockSpec((tm, tn), lambda i, j, k, *_: (i, j))
    row_spec = pl.BlockSpec((1, tn), lambda i, j, k, *_: (0, j))
    extra_specs = [tile_spec if kind == "tile" else row_spec for kind in extra_kinds]
    return _matmul(name, a, b, NT, grid, a_spec, b_spec,
                   [_sds((m, n), d) for d in out_dtypes], [tile_spec for _ in out_dtypes],
                   (tm, tn), epilogue, extras, extra_specs, after=after, b_parts=parts)


def _mm_tn_half(name, a, b, core, own, by_rows, block, recv=None, after=None, tm=1024, tn=1024, tk=2048):
    s, m = a.shape
    n = b.shape[1]
    tk = _tile(s, tk)

    def owner(chip, core_ref):
        return 2 * chip + (core_ref[0] if own else 1 - core_ref[0])

    if by_rows:
        r, c = block, n
        tm, tn = _tile(r, min(tm, r)), _tile(c, tn)
        per = r // tm
        grid = (N_CHIPS * per, c // tn, s // tk)
        a_spec = pl.BlockSpec((tk, tm), lambda i, j, k, cr, *_: (k, owner(i // per, cr) * per + i % per))
        b_spec = pl.BlockSpec((tk, tn), lambda i, j, k, cr, *_: (k, j))
        o_spec = pl.BlockSpec((None, tm, tn), lambda i, j, k, cr, *_: (i // per, i % per, j))
    else:
        r, c = m, block
        tm, tn = _tile(r, tm), _tile(c, min(tn, c))
        per = c // tn
        grid = (r // tm, N_CHIPS * per, s // tk)
        a_spec = pl.BlockSpec((tk, tm), lambda i, j, k, cr, *_: (k, i))
        b_spec = pl.BlockSpec((tk, tn), lambda i, j, k, cr, *_: (k, owner(j // per, cr) * per + j % per))
        o_spec = pl.BlockSpec((None, tm, tn), lambda i, j, k, cr, *_: (j // per, i, j % per))
    if recv is None:
        extras, epilogue = (), lambda acc: (acc,)
    else:
        extras, epilogue = (recv,), lambda acc, other: (acc + other.astype(F32),)
    return _matmul(name, a, b, TN, grid, a_spec, b_spec, [_sds((N_CHIPS, r, c), BF16)], [o_spec], (tm, tn),
                   epilogue, extras, [o_spec] * len(extras), after=after, prefetch=[core])[0]


def _rms_fwd(name, h, g, after=None):
    s, d = h.shape
    tr = _tile(s, 256)

    def body(h_ref, g_ref, o_ref):
        x = h_ref[...]
        r = lax.rsqrt(jnp.mean(x * x, axis=-1, keepdims=True) + EPS)
        o_ref[...] = (x * r * g_ref[...]).astype(o_ref.dtype)

    row = pl.BlockSpec((tr, d), lambda i: (i, 0))
    vec = pl.BlockSpec((1, d), lambda i: (0, 0))
    return _pcall(name, body, [h, g], [row, vec], _sds((s, d), BF16), row, grid=(s // tr,),
                  sem=("parallel",), after=after)


def _accumulate(ref, part, step):
    @pl.when(step == 0)
    def _():
        ref[...] = part

    @pl.when(step > 0)
    def _():
        ref[...] += part


def _rms_bwd(name, dhn, h, g, dres, after=None):
    s, d = h.shape
    tr = _tile(s, 512)
    steps = s // tr

    def body(dhn_ref, h_ref, g_ref, dres_ref, dh_ref, dhb_ref, gp_ref, acc_ref):
        step = pl.program_id(0)

        @pl.when(step == 0)
        def _():
            acc_ref[...] = jnp.zeros_like(acc_ref)

        gain = g_ref[...]

        def chunk(i, carry):
            rows = pl.ds(pl.multiple_of(i * ROW_CHUNK, ROW_CHUNK), ROW_CHUNK)
            x = h_ref[rows, :]
            r = lax.rsqrt(jnp.mean(x * x, axis=-1, keepdims=True) + EPS)
            n = x * r
            dy = dhn_ref[rows, :]
            dn = dy * gain
            dh = dres_ref[rows, :] + r * (dn - n * jnp.mean(dn * n, axis=-1, keepdims=True))
            dh_ref[rows, :] = dh
            dhb_ref[rows, :] = dh.astype(BF16)
            acc_ref[...] += dy * n
            return carry

        lax.fori_loop(0, tr // ROW_CHUNK, chunk, 0, unroll=8)

        @pl.when(step == steps - 1)
        def _():
            gp_ref[...] = jnp.sum(acc_ref[...], axis=0, keepdims=True)

    row = pl.BlockSpec((tr, d), lambda i: (i, 0))
    vec = pl.BlockSpec((1, d), lambda i: (0, 0))
    return _pcall(name, body, [dhn, h, g, dres], [row, row, vec, row],
                  [_sds((s, d), F32), _sds((s, d), BF16), _sds((1, d), F32)], [row, row, vec],
                  grid=(steps,), sem=("arbitrary",), scratch=[pltpu.VMEM((ROW_CHUNK, d), F32)], after=after)


def _loss_head(name, h, g, target, after=None):
    s, d = h.shape
    tr = _tile(s, 512)
    steps = s // tr

    def body(h_ref, g_ref, t_ref, dh_ref, dhb_ref, gp_ref, loss_ref, acc_ref, loss_acc_ref):
        step = pl.program_id(0)

        @pl.when(step == 0)
        def _():
            acc_ref[...] = jnp.zeros_like(acc_ref)
            loss_acc_ref[...] = jnp.zeros_like(loss_acc_ref)

        gg = g_ref[...]

        def chunk(i, carry):
            rows = pl.ds(pl.multiple_of(i * ROW_CHUNK, ROW_CHUNK), ROW_CHUNK)
            x = h_ref[rows, :]
            r = lax.rsqrt(jnp.mean(x * x, axis=-1, keepdims=True) + EPS)
            n = x * r
            e = n * gg - t_ref[rows, :]
            dy = e * (1.0 / d)
            dn = dy * gg
            dh = r * (dn - n * jnp.mean(dn * n, axis=-1, keepdims=True))
            dh_ref[rows, :] = dh
            dhb_ref[rows, :] = dh.astype(BF16)
            acc_ref[...] += dy * n
            loss_acc_ref[...] += jnp.mean(e * e, axis=-1, keepdims=True)
            return carry

        lax.fori_loop(0, tr // ROW_CHUNK, chunk, 0, unroll=8)

        @pl.when(step == steps - 1)
        def _():
            gp_ref[...] = jnp.sum(acc_ref[...], axis=0, keepdims=True)
            loss_ref[...] = 0.5 * jnp.sum(loss_acc_ref[...], axis=0, keepdims=True)

    row = pl.BlockSpec((tr, d), lambda i: (i, 0))
    vec = pl.BlockSpec((1, d), lambda i: (0, 0))
    one = pl.BlockSpec((1, 1), lambda i: (0, 0))
    return _pcall(name, body, [h, g, target], [row, vec, row],
                  [_sds((s, d), F32), _sds((s, d), BF16), _sds((1, d), F32), _sds((1, 1), F32)],
                  [row, row, vec, one], grid=(steps,), sem=("arbitrary",),
                  scratch=[pltpu.VMEM((ROW_CHUNK, d), F32), pltpu.VMEM((ROW_CHUNK, 1), F32)], after=after)


_SQRT_HALF = math.sqrt(0.5)
_INV_SQRT_2PI = 1.0 / math.sqrt(2.0 * math.pi)


def _gelu(x):
    return 0.5 * x * (1.0 + lax.erf(x * _SQRT_HALF))


def _gelu_grad(x):
    return 0.5 * (1.0 + lax.erf(x * _SQRT_HALF)) + x * jnp.exp(-0.5 * x * x) * _INV_SQRT_2PI


def _causal_mask():
    row = lax.broadcasted_iota(jnp.int32, (CHUNK, CHUNK), 0)
    col = lax.broadcasted_iota(jnp.int32, (CHUNK, CHUNK), 1)
    return row >= col


def _row_sum(x):
    return jnp.sum(x, axis=-1, keepdims=True)


def _masked_spatial(ws_ref, grp):
    return jnp.where(_causal_mask(), ws_ref[grp], 0.0).astype(BF16)


def _layernorm_stats(pre_ref, v_scr, w, head):
    total = jnp.zeros((CHUNK, 1), F32)
    for grp in range(A_GROUPS):
        v = _gelu(pre_ref[:, w + grp * head:w + (grp + 1) * head])
        v_scr[:, grp * head:(grp + 1) * head] = v
        total = total + _row_sum(v)
    mu = total * (1.0 / w)
    square = jnp.zeros((CHUNK, 1), F32)
    for grp in range(A_GROUPS):
        xc = v_scr[:, grp * head:(grp + 1) * head] - mu
        square = square + _row_sum(xc * xc)
    return mu, lax.rsqrt(square * (1.0 / w) + EPS)


def _amix_fwd(name, pre, ln_g, ln_b, w_s, b_s_col, after=None):
    s, w2 = pre.shape
    w = w2 // 2
    head = w // A_GROUPS

    def body(pre_ref, g_ref, b_ref, ws_ref, bs_ref, o_ref, v_scr):
        mu, rstd = _layernorm_stats(pre_ref, v_scr, w, head)
        for grp in range(A_GROUPS):
            cols = slice(grp * head, (grp + 1) * head)
            vhat = (v_scr[:, cols] - mu) * rstd
            vn = (vhat * g_ref[:, cols] + b_ref[:, cols]).astype(BF16)
            sg = jnp.dot(_masked_spatial(ws_ref, grp), vn, preferred_element_type=F32) + bs_ref[grp]
            o_ref[:, cols] = (_gelu(pre_ref[:, cols]) * sg).astype(o_ref.dtype)

    vec = pl.BlockSpec((1, w), lambda i: (0, 0))
    return _pcall(
        name, body, [pre, ln_g, ln_b, w_s, b_s_col],
        [pl.BlockSpec((CHUNK, w2), lambda i: (i, 0)), vec, vec,
         pl.BlockSpec((A_GROUPS, CHUNK, CHUNK), lambda i: (0, 0, 0)),
         pl.BlockSpec((A_GROUPS, CHUNK, 1), lambda i: (0, 0, 0))],
        _sds((s, w), BF16), pl.BlockSpec((CHUNK, w), lambda i: (i, 0)),
        grid=(s // CHUNK,), sem=("parallel",), scratch=[pltpu.VMEM((CHUNK, w), F32)], after=after)


def _amix_bwd(name, pre, dgated, ln_g, ln_b, w_s, b_s_col, after=None):
    s, w2 = pre.shape
    w = w2 // 2
    head = w // A_GROUPS

    def body(pre_ref, dg_ref, g_ref, b_ref, ws_ref, bs_ref, dpre_ref, glg_ref, glb_ref, gws_ref, gbs_ref,
             v_scr, dvn_scr):
        @pl.when(pl.program_id(0) == 0)
        def _():
            for ref in (glg_ref, glb_ref, gws_ref, gbs_ref):
                ref[...] = jnp.zeros_like(ref)

        mu, rstd = _layernorm_stats(pre_ref, v_scr, w, head)
        mask = _causal_mask()
        sum_dvhat = jnp.zeros((CHUNK, 1), F32)
        sum_dvhat_vhat = jnp.zeros((CHUNK, 1), F32)
        for grp in range(A_GROUPS):
            cols = slice(grp * head, (grp + 1) * head)
            vhat = (v_scr[:, cols] - mu) * rstd
            gain = g_ref[:, cols]
            vn = (vhat * gain + b_ref[:, cols]).astype(BF16)
            wm = _masked_spatial(ws_ref, grp)
            pre_u = pre_ref[:, cols]
            dgated = dg_ref[:, cols]
            ds = dgated * _gelu(pre_u)
            dsb = ds.astype(BF16)
            sg = jnp.dot(wm, vn, preferred_element_type=F32) + bs_ref[grp]
            dpre_ref[:, cols] = (dgated * sg * _gelu_grad(pre_u)).astype(dpre_ref.dtype)
            gws = lax.dot_general(dsb, vn, NT, preferred_element_type=F32)
            gws_ref[grp] += jnp.where(mask, gws, 0.0)
            gbs_ref[grp] += _row_sum(ds)
            dvn = lax.dot_general(wm, dsb, TN, preferred_element_type=F32)
            dvn_scr[:, cols] = dvn
            glg_ref[:, cols] += jnp.sum(dvn * vhat, axis=0, keepdims=True)
            glb_ref[:, cols] += jnp.sum(dvn, axis=0, keepdims=True)
            dvhat = dvn * gain
            sum_dvhat = sum_dvhat + _row_sum(dvhat)
            sum_dvhat_vhat = sum_dvhat_vhat + _row_sum(dvhat * vhat)
        mean_dvhat = sum_dvhat * (1.0 / w)
        mean_dvhat_vhat = sum_dvhat_vhat * (1.0 / w)
        for grp in range(A_GROUPS):
            cols = slice(grp * head, (grp + 1) * head)
            vhat = (v_scr[:, cols] - mu) * rstd
            dvhat = dvn_scr[:, cols] * g_ref[:, cols]
            dv = rstd * (dvhat - mean_dvhat - vhat * mean_dvhat_vhat)
            pre_v = pre_ref[:, w + grp * head:w + (grp + 1) * head]
            dpre_ref[:, w + grp * head:w + (grp + 1) * head] = (dv * _gelu_grad(pre_v)).astype(dpre_ref.dtype)

    vec = pl.BlockSpec((1, w), lambda i: (0, 0))
    ws_spec = pl.BlockSpec((A_GROUPS, CHUNK, CHUNK), lambda i: (0, 0, 0))
    bs_spec = pl.BlockSpec((A_GROUPS, CHUNK, 1), lambda i: (0, 0, 0))
    return _pcall(
        name, body, [pre, dgated, ln_g, ln_b, w_s, b_s_col],
        [pl.BlockSpec((CHUNK, w2), lambda i: (i, 0)), pl.BlockSpec((CHUNK, w), lambda i: (i, 0)),
         vec, vec, ws_spec, bs_spec],
        [_sds((s, w2), BF16), _sds((1, w), F32), _sds((1, w), F32),
         _sds((A_GROUPS, CHUNK, CHUNK), F32), _sds((A_GROUPS, CHUNK, 1), F32)],
        [pl.BlockSpec((CHUNK, w2), lambda i: (i, 0)), vec, vec, ws_spec, bs_spec],
        grid=(s // CHUNK,), sem=("arbitrary",),
        scratch=[pltpu.VMEM((CHUNK, w), F32), pltpu.VMEM((CHUNK, w), F32)], after=after)


def _shift_rows(x, k, forward):
    n = x.shape[0]
    row = lax.broadcasted_iota(jnp.int32, x.shape, 0)
    if forward:
        return jnp.where(row >= k, pltpu.roll(x, k, 0), 0.0)
    return jnp.where(row < n - k, pltpu.roll(x, n - k, 0), 0.0)


def _window_sum(x, window, forward):
    k = 1
    while k < window:
        x = x + _shift_rows(x, k, forward)
        k *= 2
    return x


def _pool(name, v, backward, after=None):
    s, w = v.shape
    head = w // B_GROUPS
    lane = _tile(head, 128)

    def body(v_ref, o_ref):
        grp = pl.program_id(0)
        x = v_ref[...]
        t = lax.broadcasted_iota(jnp.int32, x.shape, 0)
        for idx, window in enumerate(B_WINDOWS):
            @pl.when(grp == idx)
            def _():
                inv_count = 1.0 / jnp.minimum(t + 1, window).astype(F32)
                if backward:
                    out = _window_sum(x * inv_count, window, False) - x
                else:
                    out = _window_sum(x, window, True) * inv_count - x
                o_ref[...] = out.astype(o_ref.dtype)

    per = head // lane
    spec = pl.BlockSpec((s, lane), lambda g, j: (0, g * per + j))
    return _pcall(name, body, [v], [spec], _sds((s, w), BF16), spec, grid=(B_GROUPS, per),
                  sem=("parallel", "parallel"), after=after)


def _colsum(name, a, after=None):
    s, d = a.shape
    tr = _tile(s, 256)

    def body(a_ref, o_ref):
        _accumulate(o_ref, jnp.sum(a_ref[...], axis=0, keepdims=True), pl.program_id(0))

    return _pcall(name, body, [a], [pl.BlockSpec((tr, d), lambda i: (i, 0))], _sds((1, d), F32),
                  pl.BlockSpec((1, d), lambda i: (0, 0)), grid=(s // tr,), sem=("arbitrary",), after=after)


def _adamw(w, g, m, v):
    m = ADAM_B1 * m + (1.0 - ADAM_B1) * g
    v = ADAM_B2 * v + (1.0 - ADAM_B2) * (g * g)
    m_hat = m / (1.0 - ADAM_B1 ** ADAM_STEP)
    v_hat = v / (1.0 - ADAM_B2 ** ADAM_STEP)
    delta = -ADAM_LR * (m_hat / (jnp.sqrt(v_hat) + ADAM_EPS) + ADAM_WD * w)
    return delta, m, v


def _adam_rows(name, g, w, m, v, after=None):
    r, c = g.shape
    tr = _tile(r, 256)

    def body(g_ref, w_ref, m_ref, v_ref, d_ref, nm_ref, nv_ref):
        d_ref[...], nm_ref[...], nv_ref[...] = _adamw(w_ref[...], g_ref[...], m_ref[...], v_ref[...])

    spec = pl.BlockSpec((tr, c), lambda i: (i, 0))
    return _pcall(name, body, [g, w, m, v], [spec] * 4, [_sds((r, c), F32)] * 3, [spec] * 3,
                  grid=(r // tr,), sem=("parallel",), after=after)


def _position():
    return lax.axis_index("x"), lax.axis_index("y"), lax.axis_index("c")


def _other_chips(x, y):
    return [(1 - x, y), (x, 1 - y), (1 - x, 1 - y)]


def _slot(px, py, pc):
    return 4 * px + 2 * py + pc


def _hbm(a):
    return pltpu.with_memory_space_constraint(a, pltpu.HBM)


def _hop1_copies(srcs, lands, send_sems, recv_sems):
    x, y, c = _position()
    peers = [(x, y, 1 - c), (1 - x, y, c), (x, 1 - y, c)]
    mine = _slot(x, y, c)
    return [[pltpu.make_async_remote_copy(
        src_ref=srcs[t], dst_ref=lands[t].at[mine], send_sem=send_sems[t].at[k], recv_sem=recv_sems[t].at[k],
        device_id=peer, device_id_type=MESH) for k, peer in enumerate(peers)] for t in range(len(srcs))]


def _hop2_copies(lands, send_sems, recv_sems):
    x, y, c = _position()
    routes = [(_slot(1 - x, y, c), (x, 1 - y, c)), (_slot(x, 1 - y, c), (1 - x, y, c))]
    out = []
    for t in range(len(lands)):
        rows = lands[t].shape[1]
        halves = [(0, rows // 2), (rows // 2, rows - rows // 2)]
        per_tensor = []
        for h, ((slot, peer), (start, size)) in enumerate(zip(routes, halves)):
            if size:
                block = lands[t].at[slot, pl.ds(start, size)]
                per_tensor.append(pltpu.make_async_remote_copy(
                    src_ref=block, dst_ref=block, send_sem=send_sems[t].at[h], recv_sem=recv_sems[t].at[h],
                    device_id=peer, device_id_type=MESH))
        for j, (slot, _) in enumerate(routes):
            block = lands[t].at[slot]
            per_tensor.append(pltpu.make_async_remote_copy(
                src_ref=block, dst_ref=block, send_sem=send_sems[t].at[2 + j], recv_sem=recv_sems[t].at[2 + j],
                device_id=(x, y, 1 - c), device_id_type=MESH))
        out.append(per_tensor)
    return out


def _split_start(name, srcs, lands, copies, n_sems, after=None):
    n = len(srcs)
    order = [] if after is None else [after]
    n_in = 2 * n + len(order)

    def body(*refs):
        for per_tensor in copies(refs[:n], refs[n:2 * n], refs[n_in:n_in + n], refs[n_in + n:n_in + 2 * n]):
            for cp in per_tensor:
                cp.start()
        refs[-1][...] = jnp.zeros_like(refs[-1])

    out_shape = ([pltpu.SemaphoreType.DMA((n_sems,)) for _ in range(2 * n)]
                 + [pltpu.HBM(a.shape, a.dtype) for a in list(srcs) + list(lands)]
                 + [_sds((8, 128), F32)])
    out = pl.pallas_call(
        body, name=name, out_shape=out_shape, in_specs=[_HBM] * (2 * n) + [_ANY] * len(order),
        out_specs=[_SEM] * (2 * n) + [_HBM] * (2 * n) + [pl.BlockSpec(memory_space=pltpu.VMEM)],
        input_output_aliases={i: 2 * n + i for i in range(2 * n)},
        compiler_params=pltpu.CompilerParams(has_side_effects=_EFFECT),
    )(*[_hbm(a) for a in srcs], *[_hbm(a) for a in lands], *order)
    return [(out[t], out[n + t], out[2 * n + t], out[3 * n + t]) for t in range(n)], out[-1]


def _split_wait(name, started, copies, after):
    n = len(started)

    def body(*refs):
        for per_tensor in copies(refs[:n], refs[n:2 * n], refs[2 * n:3 * n], refs[3 * n:4 * n]):
            for cp in per_tensor:
                cp.wait_send()
                cp.wait_recv()

    srcs = [e[2] for e in started]
    lands = [e[3] for e in started]
    out = pl.pallas_call(
        body, name=name, out_shape=[pltpu.HBM(a.shape, a.dtype) for a in srcs + lands],
        in_specs=[_HBM] * (2 * n) + [_SEM] * (2 * n) + [_ANY], out_specs=[_HBM] * (2 * n),
        input_output_aliases={i: i for i in range(2 * n)},
        compiler_params=pltpu.CompilerParams(has_side_effects=_EFFECT),
    )(*srcs, *lands, *[e[0] for e in started], *[e[1] for e in started], after)
    return out[:n], out[n:]


def _gather_step(name, arrived, fresh, after=None):
    n, m = len(arrived), len(fresh)
    order = [] if after is None else [after]
    fresh_lands = [lax.empty((N_DEV,) + s.shape, s.dtype) for s in fresh]
    buffers = [e[2] for e in arrived] + [e[3] for e in arrived] + list(fresh) + fresh_lands
    old_sems = [e[0] for e in arrived] + [e[1] for e in arrived]
    n_buf, n_old = len(buffers), len(old_sems)
    first_new = n_buf + n_old + len(order)

    def body(*refs):
        bufs, old = refs[:n_buf], refs[n_buf:n_buf + n_old]
        new = refs[first_new:first_new + 2 * n + 2 * m]
        for per_tensor in _hop1_copies(bufs[:n], bufs[n:2 * n], old[:n], old[n:]):
            for cp in per_tensor:
                cp.wait_send()
                cp.wait_recv()
        second = _hop2_copies(bufs[n:2 * n], new[:n], new[n:2 * n])
        first = _hop1_copies(bufs[2 * n:2 * n + m], bufs[2 * n + m:], new[2 * n:2 * n + m], new[2 * n + m:])
        for per_tensor in second + first:
            for cp in per_tensor:
                cp.start()
        refs[-1][...] = jnp.zeros_like(refs[-1])

    n_new = 2 * n + 2 * m
    out_shape = ([pltpu.SemaphoreType.DMA((4,)) for _ in range(2 * n)]
                 + [pltpu.SemaphoreType.DMA((3,)) for _ in range(2 * m)]
                 + [pltpu.HBM(a.shape, a.dtype) for a in buffers] + [_sds((8, 128), F32)])
    out = pl.pallas_call(
        body, name=name, out_shape=out_shape,
        in_specs=[_HBM] * n_buf + [_SEM] * n_old + [_ANY] * len(order),
        out_specs=[_SEM] * n_new + [_HBM] * n_buf + [pl.BlockSpec(memory_space=pltpu.VMEM)],
        input_output_aliases={i: n_new + i for i in range(n_buf)},
        compiler_params=pltpu.CompilerParams(has_side_effects=_EFFECT),
    )(*[_hbm(a) for a in buffers], *old_sems, *order)
    sems, bufs = out[:n_new], out[n_new:n_new + n_buf]
    second = [(sems[t], sems[n + t], bufs[t], bufs[n + t]) for t in range(n)]
    first = [(sems[2 * n + t], sems[2 * n + m + t], bufs[2 * n + t], bufs[2 * n + m + t]) for t in range(m)]
    return second, first, out[-1]


def _gather_wait(name, second, after):
    return _split_wait(name, second, lambda srcs, lands, send, recv: _hop2_copies(lands, send, recv), after)


def _sibling_copies(srcs, lands, send_sems, recv_sems):
    x, y, c = _position()
    return [[pltpu.make_async_remote_copy(
        src_ref=srcs[t], dst_ref=lands[t], send_sem=send_sems[t].at[0], recv_sem=recv_sems[t].at[0],
        device_id=(x, y, 1 - c), device_id_type=MESH)] for t in range(len(srcs))]


def _sibling_and_scatter_start(name, arrays, partials):
    k = len(arrays)

    def copies(srcs, lands, send_sems, recv_sems):
        return (_sibling_copies(srcs[:k], lands[:k], send_sems[:k], recv_sems[:k])
                + _scatter_copies(srcs[k:], lands[k:], send_sems[k:], recv_sems[k:]))

    lands = ([lax.empty(a.shape, a.dtype) for a in arrays]
             + [lax.empty((N_CHIPS - 1,) + p.shape[1:], p.dtype) for p in partials])
    return _split_start(name, list(arrays) + list(partials), lands, copies, 3)


def _sibling_wait(name, started, after):
    return _split_wait(name, started, _sibling_copies, after)[1]


def _small_copies(srcs, lands, send_sems, recv_sems):
    x, y, c = _position()
    mine = _slot(x, y, c)
    peers = [(x ^ ((k >> 2) & 1), y ^ ((k >> 1) & 1), c ^ (k & 1)) for k in range(1, N_DEV)]
    return [[pltpu.make_async_remote_copy(
        src_ref=srcs[t], dst_ref=lands[t].at[mine], send_sem=send_sems[t].at[k], recv_sem=recv_sems[t].at[k],
        device_id=peer, device_id_type=MESH) for k, peer in enumerate(peers)] for t in range(len(srcs))]


def _gather_finish(name, shards, lands, after):
    n = len(shards)

    def body(*refs):
        srcs, lands_in, outs = refs[:n], refs[n:2 * n], refs[2 * n:3 * n]
        send_sems, recv_sems, local_sems = refs[3 * n:]
        x, y, c = _position()
        local = [pltpu.make_async_copy(srcs[t], outs[t].at[_slot(x, y, c)], local_sems.at[t]) for t in range(n)]

        def diagonal(t, core):
            block = outs[t].at[_slot(1 - x, 1 - y, core)]
            return pltpu.make_async_remote_copy(
                src_ref=block, dst_ref=block, send_sem=send_sems.at[t], recv_sem=recv_sems.at[t],
                device_id=(x, y, 1 - c), device_id_type=MESH)

        for cp in local:
            cp.start()
        for t in range(n):
            diagonal(t, c).start()
        for t in range(n):
            diagonal(t, c).wait_send()
            diagonal(t, 1 - c).wait_recv()
        for cp in local:
            cp.wait()

    return _pcall(name, body, [*shards, *lands], [_ANY] * (2 * n),
                  [_sds(l.shape, l.dtype) for l in lands], [_ANY] * n,
                  scratch=[pltpu.SemaphoreType.DMA((n,)), pltpu.SemaphoreType.DMA((n,)),
                           pltpu.SemaphoreType.DMA((n,))],
                  after=after, aliases={n + t: t for t in range(n)})


def _exchange_sibling(name, fulls, after):
    n = len(fulls)

    def body(*refs):
        src = refs[:n]
        out = refs[n:2 * n]
        send_sems, recv_sems = refs[2 * n:]
        x, y, c = _position()
        copies = [pltpu.make_async_remote_copy(
            src_ref=src[t].at[:, 1 - c], dst_ref=out[t], send_sem=send_sems.at[t], recv_sem=recv_sems.at[t],
            device_id=(x, y, 1 - c), device_id_type=MESH) for t in range(n)]
        for cp in copies:
            cp.start()
        for cp in copies:
            cp.wait()

    return _pcall(name, body, fulls, [_ANY] * n, [_sds((N_CHIPS,) + f.shape[2:], f.dtype) for f in fulls],
                  [_ANY] * n, scratch=[pltpu.SemaphoreType.DMA((n,)), pltpu.SemaphoreType.DMA((n,))],
                  after=after)


def _add_sibling(name, full, recv, core, after):
    _, _, r, c = full.shape
    tr = _tile(r, max(8, (256 * 1024) // c))

    def body(core_ref, f_ref, r_ref, o_ref):
        o_ref[...] = (f_ref[...].astype(F32) + r_ref[...].astype(F32)).astype(o_ref.dtype)

    return _pcall(
        name, body, [full, recv],
        [pl.BlockSpec((None, None, tr, c), lambda p, i, core_ref: (p, core_ref[0], i, 0)),
         pl.BlockSpec((None, tr, c), lambda p, i, core_ref: (p, i, 0))],
        _sds((N_CHIPS, r, c), BF16), pl.BlockSpec((None, tr, c), lambda p, i, core_ref: (p, i, 0)),
        grid=(N_CHIPS, r // tr), sem=("parallel", "parallel"), prefetch=[core], after=after)


def _scatter_copies(srcs, lands, send_sems, recv_sems):
    x, y, c = _position()
    return [[pltpu.make_async_remote_copy(
        src_ref=srcs[t].at[2 * px + py], dst_ref=lands[t].at[j],
        send_sem=send_sems[t].at[j], recv_sem=recv_sems[t].at[j],
        device_id=(px, py, c), device_id_type=MESH) for j, (px, py) in enumerate(_other_chips(x, y))]
        for t in range(len(srcs))]


def _scatter_wait(name, started, after):
    return _split_wait(name, started, _scatter_copies, after)


class _Job:
    def __init__(self, **fields):
        self.__dict__.update(fields)


def _reduce_adam_job(partial, recv, chip, w, m, v, layer, carried):
    n_layers, r, c = w.shape
    tr = _tile(r, max(8, (256 * 1024) // c))

    def body(p_ref, r_ref, w_ref, m_ref, v_ref, *rest):
        g_ref, d_ref, nm_ref, nv_ref = rest[-4:]
        g = p_ref[...].astype(F32)
        for j in range(N_CHIPS - 1):
            g = g + r_ref[j].astype(F32)
        g_ref[...] = g
        d_ref[...], nm_ref[...], nv_ref[...] = _adamw(w_ref[...], g, m_ref[...], v_ref[...])

    layered = ((None, tr, c), lambda blk, chip_ref: (layer, blk, 0))
    in_specs = [((None, tr, c), lambda blk, chip_ref: (chip_ref[0], blk, 0)),
                ((N_CHIPS - 1, tr, c), lambda blk, chip_ref: (0, blk, 0)), layered, layered, layered]
    operands = [partial, recv, w, m, v]
    aliases = {}
    if carried is not None:
        operands += list(carried)
        in_specs += [None] * 4
        aliases = {5 + o: o for o in range(4)}
    return _Job(operands=operands, in_specs=in_specs, out_shape=[_sds((n_layers, r, c), F32)] * 4,
                out_specs=[layered] * 4, body=body, aliases=aliases, prefetch=chip, n_blocks=r // tr)


def _run_job(name, job, after):
    def spec(entry):
        if entry is None:
            return _ANY
        shape, index = entry
        return pl.BlockSpec(shape, lambda blk, pre, index=index: index(blk, pre))

    def body(pre_ref, *refs):
        job.body(*refs)

    return _pcall(name, body, job.operands, [spec(e) for e in job.in_specs], job.out_shape,
                  [spec(e) for e in job.out_specs], grid=(job.n_blocks,), sem=("parallel",),
                  prefetch=[job.prefetch], after=after,
                  aliases={1 + i: o for i, o in job.aliases.items()})


def _small_sum(name, gathered, own, device, after=None):
    r, lanes = own.shape

    def body(dev_ref, g_ref, own_ref, out_ref):
        dev = dev_ref[0]
        mine = own_ref[...]
        total = jnp.where(dev == 0, mine, g_ref[0])
        for d in range(1, N_DEV):
            total = total + jnp.where(dev == d, mine, g_ref[d])
        out_ref[...] = total

    return _pcall(name, body, [gathered, own],
                  [pl.BlockSpec((N_DEV, r, lanes), lambda i, dev_ref: (0, 0, 0)),
                   pl.BlockSpec((r, lanes), lambda i, dev_ref: (0, 0))],
                  _sds((r, lanes), F32), pl.BlockSpec((r, lanes), lambda i, dev_ref: (0, 0)),
                  grid=(1,), sem=("arbitrary",), prefetch=[device], after=after)


def _pack(arrays):
    return jnp.concatenate([a.reshape(-1, 128) for a in arrays], axis=0)


def _unpack(packed, shapes):
    out, row = [], 0
    for shape in shapes:
        rows = math.prod(shape) // 128
        out.append(packed[row:row + rows].reshape(shape))
        row += rows
    return out


class _Order:
    def __init__(self):
        self.last = None

    def __call__(self, fn, *args, **kwargs):
        out = fn(*args, after=self.last, **kwargs)
        self.last = out[0] if isinstance(out, (list, tuple)) else out
        return out


def kernel(x, a_w_in, a_ln_g, a_ln_b, a_w_s, a_b_s, a_w_out, b_w_in, b_w_grp, b_scale, b_w_out, norm_mix, norm_mlp, mlp_w1, mlp_w2, final_norm, loss_target, m_a_w_in, m_a_ln_g, m_a_ln_b, m_a_w_s, m_a_b_s, m_a_w_out, m_b_w_in, m_b_w_grp, m_b_scale, m_b_w_out, m_norm_mix, m_norm_mlp, m_mlp_w1, m_mlp_w2, m_final_norm, v_a_w_in, v_a_ln_g, v_a_ln_b, v_a_w_s, v_a_b_s, v_a_w_out, v_b_w_in, v_b_w_grp, v_b_scale, v_b_w_out, v_norm_mix, v_norm_mlp, v_mlp_w1, v_mlp_w2, v_final_norm):
    s, d = x.shape[1], x.shape[2]
    depth = mlp_w1.shape[0]
    a_slab = a_w_in.shape[2]
    ff_slab = mlp_w1.shape[2]
    ff_rows = mlp_w2.shape[1]
    bh = b_w_grp.shape[3]
    my_x, my_y, my_c = _position()
    core = jnp.reshape(my_c, (1,)).astype(jnp.int32)
    chip = jnp.reshape(2 * my_x + my_y, (1,)).astype(jnp.int32)
    device = _slot(my_x, my_y, my_c)
    run = _Order()

    w1_b, w2_b = mlp_w1.astype(BF16), mlp_w2.astype(BF16)
    shards = [a_w_in[0].astype(BF16), a_w_out[0].astype(BF16), b_scale,
              w1_b[0], w2_b[0],
              b_w_in[0].astype(BF16), b_w_grp[0].astype(BF16), b_w_out[0].astype(BF16),
              w1_b[1], w2_b[1]]
    groups = [[0], [1, 2], [3], [4], [5, 6, 7], [8], [9]]
    start_with = {1: [2, 3], 2: [4], 3: [5], 4: [6]}
    hop1, hop2 = {}, {}
    _, hop1[0], token = _gather_step("weights_group0_hop1", [], [shards[t] for t in groups[0]])
    _, hop1[1], token = _gather_step("weights_group1_hop1", [], [shards[t] for t in groups[1]], token)
    run.last = token

    def advance(g):
        if g not in hop1:
            return
        ahead = start_with.get(g, [])
        fresh = [shards[t] for a in ahead for t in groups[a]]
        hop2[g], started, tok = _gather_step(f"weights_group{g}_hop2", hop1.pop(g), fresh, run.last)
        for a in ahead:
            hop1[a], started = started[:len(groups[a])], started[len(groups[a]):]
        run.last = tok

    def gathered(g):
        advance(g)
        if g == 0:
            advance(1)
        srcs, lands = _gather_wait(f"weights_group{g}_wait", hop2.pop(g), run.last)
        run.last = srcs[0]
        return run(_gather_finish, f"weights_group{g}_finish", srcs, lands)

    h0 = x[0]
    target = loss_target[0]
    ln_g, ln_b = a_ln_g, a_ln_b
    w_s = a_w_s[0]
    b_s_col = a_b_s[0][:, :, None]
    nmix = [norm_mix[l][None, :] for l in range(depth)]
    nmlp = [norm_mlp[l][None, :] for l in range(depth)]

    def mlp_forward(l, h, up_group):
        hn = run(_rms_fwd, f"mlp{l}_norm", h, nmlp[l])
        (w1,) = gathered(up_group)
        advance(up_group + 1)
        act, act_sq = run(_mm_nn, f"mlp{l}_up", hn, w1,
                          lambda acc: (jnp.maximum(acc, 0.0), jnp.square(jnp.maximum(acc, 0.0))),
                          (BF16, BF16), slab=True)
        (w2,) = gathered(up_group + 1)
        advance(up_group + 2)
        w2 = w2.reshape(-1, d)
        (h_out,) = run(_mm_nn, f"mlp{l}_down", act_sq, w2, lambda acc, res: (acc + res,), (F32,),
                       extras=(h,), extra_kinds=("tile",))
        return h_out, (h, hn, act, act_sq, w1, w2)

    scattered = []

    pending = []

    def scatter_partials(name, partials, specs):
        pending.append((name, partials, specs))

    def start_exchanges(name, to_sibling):
        partials = [p for _, group, _ in pending for p in group]
        in_flight, tok = _sibling_and_scatter_start(name, to_sibling, partials)
        run.last = tok
        first = len(to_sibling)
        for group_name, group, specs in pending:
            scattered.append((group_name, in_flight[first:first + len(group)], specs))
            first += len(group)
        pending.clear()
        return in_flight[:len(to_sibling)]

    weights = {"a_w_in": (a_w_in, m_a_w_in, v_a_w_in), "a_w_out": (a_w_out, m_a_w_out, v_a_w_out),
               "b_w_in": (b_w_in, m_b_w_in, v_b_w_in), "b_w_grp": (b_w_grp, m_b_w_grp, v_b_w_grp),
               "b_w_out": (b_w_out, m_b_w_out, v_b_w_out), "mlp_w1": (mlp_w1, m_mlp_w1, v_mlp_w1),
               "mlp_w2": (mlp_w2, m_mlp_w2, v_mlp_w2)}
    results = {}

    def finish_group(name, in_flight, specs):
        partials, lands = _scatter_wait(name + "_scatter_wait", in_flight, run.last)
        run.last = lands[0]
        for t, (wname, layer) in enumerate(specs):
            w, m, v = weights[wname]
            shape = (w.shape[0],) + partials[t].shape[1:]
            job = _reduce_adam_job(partials[t], lands[t], chip, w.reshape(shape), m.reshape(shape),
                                   v.reshape(shape), layer, results.get(wname))
            results[wname] = run(_run_job, f"{name}_reduce_adam_{t}", job)

    def weight_grad(name, a, b, by_rows, block, between, tm=1024, tn=1024):
        other = run(_mm_tn_half, name + "_other", a, b, core, False, by_rows, block, tm=tm, tn=tn)
        sent = start_exchanges(name + "_sibling_start", [other])
        middle = between()
        (recv,) = _sibling_wait(name + "_sibling_wait", sent, run.last)
        run.last = recv
        return run(_mm_tn_half, name + "_own", a, b, core, True, by_rows, block, recv=recv, tm=tm, tn=tn), middle

    def mlp_backward(l, saved, dh, dhb):
        h, hn, act, act_sq, w1, w2 = saved
        part_w2, (dpre,) = weight_grad(
            f"mlp{l}_down_dw", act_sq, dhb, True, ff_rows,
            lambda: run(_mm_nt, f"mlp{l}_down_dx", dhb, w2, lambda acc, a: (2.0 * a.astype(F32) * acc,),
                        (BF16,), extras=(act,), extra_kinds=("tile",)))
        scatter_partials(f"mlp{l}_down_grads", [part_w2], [("mlp_w2", l)])
        part_w1, (dhn,) = weight_grad(
            f"mlp{l}_up_dw", hn, dpre, False, ff_slab,
            lambda: run(_mm_nt, f"mlp{l}_up_dx", dpre, w1, lambda acc: (acc,), (F32,), slab=True))
        scatter_partials(f"mlp{l}_up_grads", [part_w1], [("mlp_w1", l)])
        dh, dhb, g_norm = run(_rms_bwd, f"mlp{l}_norm_bwd", dhn, h, nmlp[l], dh)
        return dh, dhb, g_norm

    hn0 = run(_rms_fwd, "mix0_norm", h0, nmix[0])
    (wa_in,) = gathered(0)
    (pre,) = run(_mm_nn, "mixa_in", hn0, wa_in, lambda acc: (acc,), (F32,), slab=True)
    wa_out, scale = gathered(1)
    wa_out, scale = wa_out.reshape(d, d), scale.reshape(1, d)
    gated = run(_amix_fwd, "mixa_gate", pre, ln_g, ln_b, w_s, b_s_col)
    advance(2)
    (h1,) = run(_mm_nn, "mixa_out", gated, wa_out, lambda acc, res: (acc + res,), (F32,),
                extras=(h0,), extra_kinds=("tile",))
    h2, saved_mlp0 = mlp_forward(0, h1, 2)
    hn2 = run(_rms_fwd, "mix1_norm", h2, nmix[1])
    wb_in, wb_grp, wb_out = gathered(4)
    advance(5)
    wb_in, wb_out = wb_in.reshape(d, d), wb_out.reshape(d, d)
    wb_grp = jnp.transpose(wb_grp, (1, 0, 2, 3)).reshape(B_GROUPS, bh, bh)
    (vb,) = run(_mm_nn, "mixb_in", hn2, wb_in, lambda acc: (acc,), (F32,))
    pooled = run(_pool, "mixb_pool", vb, backward=False)
    tm = _tile(s, 1024)
    grp_tile = pl.BlockSpec((tm, bh), lambda i, j, k: (i, j))
    grp_weight = pl.BlockSpec((None, bh, bh), lambda i, j, k: (j, 0, 0))
    mixed, mixed_scaled = run(
        _matmul, "mixb_grp", pooled, wb_grp, NN, (s // tm, B_GROUPS, 1), grp_tile, grp_weight,
        [_sds((s, d), BF16), _sds((s, d), BF16)], [grp_tile] * 2,
        (tm, bh), lambda acc, sc: (acc, acc * sc), (scale,), [pl.BlockSpec((1, bh), lambda i, j, k: (0, j))])
    (h3,) = run(_mm_nn, "mixb_out", mixed_scaled, wb_out, lambda acc, res: (acc + res,), (F32,),
                extras=(h2,), extra_kinds=("tile",))
    h4, saved_mlp1 = mlp_forward(1, h3, 5)
    dh, dhb, g_final, loss_part = run(_loss_head, "loss_head", h4, final_norm[None, :], target)

    dh, dhb, g_nmlp1 = mlp_backward(1, saved_mlp1, dh, dhb)
    tks = _tile(s, 1024)
    grp_rows = pl.BlockSpec((tks, bh), lambda i, j, k: (k, j))

    def mixb_middle():
        dms_scaled, dms_mixed = run(
            _mm_nt, "mixb_out_dx", dhb, wb_out,
            lambda acc, sc, mx: (acc * sc, acc * mx.astype(F32)), (BF16, F32),
            extras=(scale, mixed), extra_kinds=("row", "tile"))
        g_scale = run(_colsum, "mixb_scale_dw", dms_mixed)
        (g_wb_grp,) = run(
            _matmul, "mixb_grp_dw", pooled, dms_scaled, TN, (1, B_GROUPS, s // tks), grp_rows, grp_rows,
            [_sds((B_GROUPS, bh, bh), BF16)], [grp_weight], (bh, bh), lambda acc: (acc,))
        (dpooled,) = run(
            _matmul, "mixb_grp_dx", dms_scaled, wb_grp, NT, (s // tm, B_GROUPS, 1), grp_tile, grp_weight,
            [_sds((s, d), F32)], [grp_tile], (tm, bh), lambda acc: (acc,))
        return g_scale, g_wb_grp, run(_pool, "mixb_pool_bwd", dpooled, backward=True)

    part_wb_out, (g_scale, g_wb_grp, dvb) = weight_grad("mixb_out_dw", mixed_scaled, dhb, True, d // N_DEV,
                                                        mixb_middle, tn=d)
    part_wb_in, (dhn2,) = weight_grad(
        "mixb_in_dw", hn2, dvb, True, d // N_DEV,
        lambda: run(_mm_nt, "mixb_in_dx", dvb, wb_in, lambda acc: (acc,), (F32,)), tn=d)
    grp_full = jnp.transpose(g_wb_grp.reshape(B_GROUPS, N_DEV, bh // N_DEV, bh), (1, 0, 2, 3))
    grp_full = grp_full.reshape(N_CHIPS, 2, B_GROUPS * bh // N_DEV, bh)
    (grp_sibling,) = run(_exchange_sibling, "mixb_grp_dw_to_sibling", [grp_full])
    part_wb_grp = run(_add_sibling, "mixb_grp_dw_add_sibling", grp_full, grp_sibling, core)
    scatter_partials("mixb_grads", [part_wb_out, part_wb_grp, part_wb_in],
                     [("b_w_out", 0), ("b_w_grp", 0), ("b_w_in", 0)])
    dh, dhb, g_nmix1 = run(_rms_bwd, "mix1_norm_bwd", dhn2, h2, nmix[1], dh)
    dh, dhb, g_nmlp0 = mlp_backward(0, saved_mlp0, dh, dhb)
    def mixa_middle():
        (dgated,) = run(_mm_nt, "mixa_out_dx", dhb, wa_out, lambda acc: (acc,), (F32,))
        return run(_amix_bwd, "mixa_gate_bwd", pre, dgated, ln_g, ln_b, w_s, b_s_col)

    part_wa_out, (dpre, g_ln_g, g_ln_b, g_w_s, g_b_s) = weight_grad("mixa_out_dw", gated, dhb, True, d // N_DEV,
                                                                     mixa_middle, tn=d)
    part_wa_in, (dhn0,) = weight_grad(
        "mixa_in_dw", hn0, dpre, False, a_slab,
        lambda: run(_mm_nt, "mixa_in_dx", dpre, wa_in, lambda acc: (acc,), (F32,), slab=True), tm=d)
    scatter_partials("mixa_grads", [part_wa_in, part_wa_out], [("a_w_in", 0), ("a_w_out", 0)])
    start_exchanges("mixa_grads_scatter_start", [])
    grad_x, _, g_nmix0 = run(_rms_bwd, "mix0_norm_bwd", dhn0, h0, nmix[0], dh)

    g_norm_mix = jnp.concatenate([g_nmix0, g_nmix1], axis=0)
    g_norm_mlp = jnp.concatenate([g_nmlp0, g_nmlp1], axis=0)
    loss_row = jnp.pad(loss_part, ((0, 0), (0, 127)))
    small_parts = [g_ln_g, g_ln_b, g_w_s, g_b_s, g_norm_mix, g_norm_mlp, g_final, g_scale, loss_row]
    packed = _pack(small_parts)
    small_sent, tok = _split_start("small_grads_start", [packed], [lax.empty((N_DEV,) + packed.shape, F32)],
                                   _small_copies, N_DEV - 1)
    run.last = tok

    for group in scattered[:-1]:
        finish_group(*group)
    own_packed, small_gathered = _split_wait("small_grads_wait", small_sent, _small_copies, run.last)
    run.last = small_gathered[0]
    small_sum = run(_small_sum, "small_grads_sum", small_gathered[0], own_packed[0],
                    jnp.reshape(device, (1,)).astype(jnp.int32))
    sg = _unpack(small_sum, [a_ln_g.shape, a_ln_b.shape, a_w_s.shape, a_b_s.shape, norm_mix.shape,
                             norm_mlp.shape, final_norm.shape, (1, d), (1, 128)])
    loss = sg.pop()[0, 0]
    shard = b_scale.shape[1]
    sg[7] = lax.dynamic_slice(sg[7], (0, device * shard), (1, shard))
    small_w = [a_ln_g, a_ln_b, a_w_s, a_b_s, norm_mix, norm_mlp, final_norm, b_scale]
    small_m = [m_a_ln_g, m_a_ln_b, m_a_w_s, m_a_b_s, m_norm_mix, m_norm_mlp, m_final_norm, m_b_scale]
    small_v = [v_a_ln_g, v_a_ln_b, v_a_w_s, v_a_b_s, v_norm_mix, v_norm_mlp, v_final_norm, v_b_scale]
    small_out = run(_adam_rows, "small_adam", _pack(sg), _pack(small_w), _pack(small_m), _pack(small_v))
    shapes = [w.shape for w in small_w]
    small_res = [sg] + [_unpack(o, shapes) for o in small_out]

    finish_group(*scattered[-1])
    big = {wname: [o.reshape(weights[wname][0].shape) for o in outs] for wname, outs in results.items()}

    def leaf(o):
        return (big["a_w_in"][o], small_res[o][0], small_res[o][1], small_res[o][2], small_res[o][3],
                big["a_w_out"][o], big["b_w_in"][o], big["b_w_grp"][o], small_res[o][7], big["b_w_out"][o],
                small_res[o][4], small_res[o][5], big["mlp_w1"][o], big["mlp_w2"][o], small_res[o][6])

    return (loss, grad_x[None], *leaf(0), *leaf(1), *leaf(2), *leaf(3))
```

```python
import math

import jax
import jax.numpy as jnp
from jax import lax
from jax.experimental import pallas as pl
from jax.experimental.pallas import tpu as pltpu

F32 = jnp.float32
BF16 = jnp.bfloat16
MESH = pl.DeviceIdType.MESH

N_DEV = 8
N_CHIPS = 4
CHUNK = 128
A_GROUPS = 8
B_WINDOWS = (2, 4, 8, 16)
B_GROUPS = len(B_WINDOWS)
EPS = 1e-6
ADAM_LR = 0.001
ADAM_B1 = 0.9
ADAM_B2 = 0.999
ADAM_EPS = 1e-08
ADAM_WD = 0.01
ADAM_STEP = 10

VMEM_LIMIT = 48 * 1024 * 1024
ROW_CHUNK = 16

NN = (((1,), (0,)), ((), ()))
NT = (((1,), (1,)), ((), ()))
TN = (((0,), (0,)), ((), ()))

_ANY = pl.BlockSpec(memory_space=pl.ANY)
_HBM = pl.BlockSpec(memory_space=pltpu.HBM)
_SEM = pl.BlockSpec(memory_space=pltpu.SEMAPHORE)
_EFFECT = pltpu.SideEffectType.DATAFLOW_SIDE_EFFECTING


def _tile(n, pref):
    return pref if n % pref == 0 else n


def _sds(shape, dtype):
    return jax.ShapeDtypeStruct(shape, dtype)


def _pcall(name, body, operands, in_specs, out_shape, out_specs, *, grid=None, sem=None, scratch=(),
           prefetch=(), after=None, aliases=None):
    after = [] if after is None else [after]
    n_lead = len(prefetch) + len(operands)
    n_after = len(after)

    def wrapped(*refs):
        body(*refs[:n_lead], *refs[n_lead + n_after:])

    in_specs = list(in_specs) + [_ANY] * n_after
    params = pltpu.CompilerParams(vmem_limit_bytes=VMEM_LIMIT) if sem is None else \
        pltpu.CompilerParams(dimension_semantics=sem, vmem_limit_bytes=VMEM_LIMIT)
    kwargs = dict(out_shape=out_shape, scratch_shapes=list(scratch), compiler_params=params, name=name,
                  input_output_aliases=aliases or {})
    if prefetch:
        kwargs["grid_spec"] = pltpu.PrefetchScalarGridSpec(
            num_scalar_prefetch=len(prefetch), grid=grid, in_specs=in_specs, out_specs=out_specs,
            scratch_shapes=list(scratch))
        kwargs.pop("scratch_shapes")
    else:
        kwargs.update(in_specs=in_specs, out_specs=out_specs)
        if grid is not None:
            kwargs["grid"] = grid
    return pl.pallas_call(wrapped, **kwargs)(*prefetch, *operands, *after)


def _matmul(name, a, b, dims, grid, a_spec, b_spec, out_shape, out_specs, acc_shape,
            epilogue, extras=(), extra_specs=(), after=None, prefetch=(), b_parts=1):
    nk = grid[2]
    n_extra = len(extras)
    n_out = len(out_shape)
    n_pre = len(prefetch)

    def body(*refs):
        refs = refs[n_pre:]
        a_ref, b_ref = refs[0], refs[1]
        extra_refs = refs[2:2 + n_extra]
        out_refs = refs[2 + n_extra:2 + n_extra + n_out]

        def finish(acc):
            outs = epilogue(acc, *[r[...] for r in extra_refs])
            for o_ref, o in zip(out_refs, outs):
                o_ref[...] = o.astype(o_ref.dtype)

        def product():
            if b_parts == 1:
                return lax.dot_general(a_ref[...], b_ref[...], dims, preferred_element_type=F32)
            width = b_ref.shape[2]
            total = None
            for p in range(b_parts):
                part = lax.dot_general(a_ref[:, p * width:(p + 1) * width], b_ref[p], dims,
                                       preferred_element_type=F32)
                total = part if total is None else total + part
            return total

        if nk == 1:
            finish(product())
        else:
            acc_ref = refs[-1]
            k = pl.program_id(2)

            @pl.when(k == 0)
            def _():
                acc_ref[...] = product()

            if nk > 2:
                @pl.when(jnp.logical_and(k > 0, k < nk - 1))
                def _():
                    acc_ref[...] += product()

            @pl.when(k == nk - 1)
            def _():
                finish(acc_ref[...] + product())

    scratch = [] if nk == 1 else [pltpu.VMEM(acc_shape, F32)]
    return _pcall(name, body, [a, b, *extras], [a_spec, b_spec, *extra_specs], out_shape, out_specs,
                  grid=grid, sem=("parallel", "parallel", "arbitrary"), scratch=scratch, after=after,
                  prefetch=prefetch)


def _mm_nn(name, a, b, epilogue, out_dtypes, extras=(), extra_kinds=(), slab=False, after=None,
           tm=1024, tn=1024, tk=2048):
    m, kd = a.shape
    if slab:
        n_slab, _, w = b.shape
        n = n_slab * w
        tn = _tile(w, min(tn, w))
        per = w // tn
        tk = _tile(kd, tk)
        b_spec = pl.BlockSpec((None, tk, tn), lambda i, j, k: (j // per, k, j % per))
    else:
        n = b.shape[1]
        tn = _tile(n, tn)
        tk = _tile(kd, tk)
        b_spec = pl.BlockSpec((tk, tn), lambda i, j, k: (k, j))
    tm = _tile(m, tm)
    grid = (m // tm, n // tn, kd // tk)
    a_spec = pl.BlockSpec((tm, tk), lambda i, j, k: (i, k))
    tile_spec = pl.BlockSpec((tm, tn), lambda i, j, k: (i, j))
    row_spec = pl.BlockSpec((1, tn), lambda i, j, k: (0, j))
    extra_specs = [tile_spec if kind == "tile" else row_spec for kind in extra_kinds]
    return _matmul(name, a, b, NN, grid, a_spec, b_spec,
                   [_sds((m, n), d) for d in out_dtypes], [tile_spec for _ in out_dtypes],
                   (tm, tn), epilogue, extras, extra_specs, after=after)


def _mm_nt(name, a, b, epilogue, out_dtypes, extras=(), extra_kinds=(), slab=False, after=None,
           tm=1024, tn=1024, tk=2048):
    m, kd = a.shape
    parts = 1
    if slab:
        n_slab, n, w = b.shape
        tn = _tile(n, tn)
        if tk > w and tk % w == 0 and n_slab % (tk // w) == 0:
            parts = tk // w
            b_spec = pl.BlockSpec((parts, tn, w), lambda i, j, k, *_: (k, j, 0))
        else:
            tk = _tile(w, min(tk, w))
            per = w // tk
            b_spec = pl.BlockSpec((None, tn, tk), lambda i, j, k, *_: (k // per, j, k % per))
    else:
        n = b.shape[0]
        tn = _tile(n, tn)
        tk = _tile(kd, tk)
        b_spec = pl.BlockSpec((tn, tk), lambda i, j, k, *_: (j, k))
    tm = _tile(m, tm)
    grid = (m // tm, n // tn, kd // tk)
    a_spec = pl.BlockSpec((tm, tk), lambda i, j, k, *_: (i, k))
    tile_spec = pl.BlockSpec((tm, tn), lambda i, j, k, *_: (i, j))
    row_spec = pl.BlockSpec((1, tn), lambda i, j, k, *_: (0, j))
    extra_specs = [tile_spec if kind == "tile" else row_spec for kind in extra_kinds]
    return _matmul(name, a, b, NT, grid, a_spec, b_spec,
                   [_sds((m, n), d) for d in out_dtypes], [tile_spec for _ in out_dtypes],
                   (tm, tn), epilogue, extras, extra_specs, after=after, b_parts=parts)


def _mm_tn_half(name, a, b, core, own, by_rows, block, recv=None, after=None, tm=1024, tn=1024, tk=2048):
    s, m = a.shape
    n = b.shape[1]
    tk = _tile(s, tk)

    def owner(chip, core_ref):
        return 2 * chip + (core_ref[0] if own else 1 - core_ref[0])

    if by_rows:
        r, c = block, n
        tm, tn = _tile(r, min(tm, r)), _tile(c, tn)
        per = r // tm
        grid = (N_CHIPS * per, c // tn, s // tk)
        a_spec = pl.BlockSpec((tk, tm), lambda i, j, k, cr, *_: (k, owner(i // per, cr) * per + i % per))
        b_spec = pl.BlockSpec((tk, tn), lambda i, j, k, cr, *_: (k, j))
        o_spec = pl.BlockSpec((None, tm, tn), lambda i, j, k, cr, *_: (i // per, i % per, j))
    else:
        r, c = m, block
        tm, tn = _tile(r, tm), _tile(c, min(tn, c))
        per = c // tn
        grid = (r // tm, N_CHIPS * per, s // tk)
        a_spec = pl.BlockSpec((tk, tm), lambda i, j, k, cr, *_: (k, i))
        b_spec = pl.BlockSpec((tk, tn), lambda i, j, k, cr, *_: (k, owner(j // per, cr) * per + j % per))
        o_spec = pl.BlockSpec((None, tm, tn), lambda i, j, k, cr, *_: (j // per, i, j % per))
    if recv is None:
        extras, epilogue = (), lambda acc: (acc,)
    else:
        extras, epilogue = (recv,), lambda acc, other: (acc + other.astype(F32),)
    return _matmul(name, a, b, TN, grid, a_spec, b_spec, [_sds((N_CHIPS, r, c), BF16)], [o_spec], (tm, tn),
                   epilogue, extras, [o_spec] * len(extras), after=after, prefetch=[core])[0]


def _rms_fwd(name, h, g, after=None):
    s, d = h.shape
    tr = _tile(s, 256)

    def body(h_ref, g_ref, o_ref):
        x = h_ref[...]
        r = lax.rsqrt(jnp.mean(x * x, axis=-1, keepdims=True) + EPS)
        o_ref[...] = (x * r * g_ref[...]).astype(o_ref.dtype)

    row = pl.BlockSpec((tr, d), lambda i: (i, 0))
    vec = pl.BlockSpec((1, d), lambda i: (0, 0))
    return _pcall(name, body, [h, g], [row, vec], _sds((s, d), BF16), row, grid=(s // tr,),
                  sem=("parallel",), after=after)


def _accumulate(ref, part, step):
    @pl.when(step == 0)
    def _():
        ref[...] = part

    @pl.when(step > 0)
    def _():
        ref[...] += part


def _rms_bwd(name, dhn, h, g, dres, after=None):
    s, d = h.shape
    tr = _tile(s, 512)
    steps = s // tr

    def body(dhn_ref, h_ref, g_ref, dres_ref, dh_ref, dhb_ref, gp_ref, acc_ref):
        step = pl.program_id(0)

        @pl.when(step == 0)
        def _():
            acc_ref[...] = jnp.zeros_like(acc_ref)

        gain = g_ref[...]

        def chunk(i, carry):
            rows = pl.ds(pl.multiple_of(i * ROW_CHUNK, ROW_CHUNK), ROW_CHUNK)
            x = h_ref[rows, :]
            r = lax.rsqrt(jnp.mean(x * x, axis=-1, keepdims=True) + EPS)
            n = x * r
            dy = dhn_ref[rows, :]
            dn = dy * gain
            dh = dres_ref[rows, :] + r * (dn - n * jnp.mean(dn * n, axis=-1, keepdims=True))
            dh_ref[rows, :] = dh
            dhb_ref[rows, :] = dh.astype(BF16)
            acc_ref[...] += dy * n
            return carry

        lax.fori_loop(0, tr // ROW_CHUNK, chunk, 0, unroll=8)

        @pl.when(step == steps - 1)
        def _():
            gp_ref[...] = jnp.sum(acc_ref[...], axis=0, keepdims=True)

    row = pl.BlockSpec((tr, d), lambda i: (i, 0))
    vec = pl.BlockSpec((1, d), lambda i: (0, 0))
    return _pcall(name, body, [dhn, h, g, dres], [row, row, vec, row],
                  [_sds((s, d), F32), _sds((s, d), BF16), _sds((1, d), F32)], [row, row, vec],
                  grid=(steps,), sem=("arbitrary",), scratch=[pltpu.VMEM((ROW_CHUNK, d), F32)], after=after)


def _loss_head(name, h, g, target, after=None):
    s, d = h.shape
    tr = _tile(s, 512)
    steps = s // tr

    def body(h_ref, g_ref, t_ref, dh_ref, dhb_ref, gp_ref, loss_ref, acc_ref, loss_acc_ref):
        step = pl.program_id(0)

        @pl.when(step == 0)
        def _():
            acc_ref[...] = jnp.zeros_like(acc_ref)
            loss_acc_ref[...] = jnp.zeros_like(loss_acc_ref)

        gg = g_ref[...]

        def chunk(i, carry):
            rows = pl.ds(pl.multiple_of(i * ROW_CHUNK, ROW_CHUNK), ROW_CHUNK)
            x = h_ref[rows, :]
            r = lax.rsqrt(jnp.mean(x * x, axis=-1, keepdims=True) + EPS)
            n = x * r
            e = n * gg - t_ref[rows, :]
            dy = e * (1.0 / d)
            dn = dy * gg
            dh = r * (dn - n * jnp.mean(dn * n, axis=-1, keepdims=True))
            dh_ref[rows, :] = dh
            dhb_ref[rows, :] = dh.astype(BF16)
            acc_ref[...] += dy * n
            loss_acc_ref[...] += jnp.mean(e * e, axis=-1, keepdims=True)
            return carry

        lax.fori_loop(0, tr // ROW_CHUNK, chunk, 0, unroll=8)

        @pl.when(step == steps - 1)
        def _():
            gp_ref[...] = jnp.sum(acc_ref[...], axis=0, keepdims=True)
            loss_ref[...] = 0.5 * jnp.sum(loss_acc_ref[...], axis=0, keepdims=True)

    row = pl.BlockSpec((tr, d), lambda i: (i, 0))
    vec = pl.BlockSpec((1, d), lambda i: (0, 0))
    one = pl.BlockSpec((1, 1), lambda i: (0, 0))
    return _pcall(name, body, [h, g, target], [row, vec, row],
                  [_sds((s, d), F32), _sds((s, d), BF16), _sds((1, d), F32), _sds((1, 1), F32)],
                  [row, row, vec, one], grid=(steps,), sem=("arbitrary",),
                  scratch=[pltpu.VMEM((ROW_CHUNK, d), F32), pltpu.VMEM((ROW_CHUNK, 1), F32)], after=after)


_SQRT_HALF = math.sqrt(0.5)
_INV_SQRT_2PI = 1.0 / math.sqrt(2.0 * math.pi)


def _gelu(x):
    return 0.5 * x * (1.0 + lax.erf(x * _SQRT_HALF))


def _gelu_grad(x):
    return 0.5 * (1.0 + lax.erf(x * _SQRT_HALF)) + x * jnp.exp(-0.5 * x * x) * _INV_SQRT_2PI


def _causal_mask():
    row = lax.broadcasted_iota(jnp.int32, (CHUNK, CHUNK), 0)
    col = lax.broadcasted_iota(jnp.int32, (CHUNK, CHUNK), 1)
    return row >= col


def _row_sum(x):
    return jnp.sum(x, axis=-1, keepdims=True)


def _masked_spatial(ws_ref, grp):
    return jnp.where(_causal_mask(), ws_ref[grp], 0.0).astype(BF16)


def _layernorm_stats(pre_ref, v_scr, w, head):
    total = jnp.zeros((CHUNK, 1), F32)
    for grp in range(A_GROUPS):
        v = _gelu(pre_ref[:, w + grp * head:w + (grp + 1) * head])
        v_scr[:, grp * head:(grp + 1) * head] = v
        total = total + _row_sum(v)
    mu = total * (1.0 / w)
    square = jnp.zeros((CHUNK, 1), F32)
    for grp in range(A_GROUPS):
        xc = v_scr[:, grp * head:(grp + 1) * head] - mu
        square = square + _row_sum(xc * xc)
    return mu, lax.rsqrt(square * (1.0 / w) + EPS)


def _amix_fwd(name, pre, ln_g, ln_b, w_s, b_s_col, after=None):
    s, w2 = pre.shape
    w = w2 // 2
    head = w // A_GROUPS

    def body(pre_ref, g_ref, b_ref, ws_ref, bs_ref, o_ref, v_scr):
        mu, rstd = _layernorm_stats(pre_ref, v_scr, w, head)
        for grp in range(A_GROUPS):
            cols = slice(grp * head, (grp + 1) * head)
            vhat = (v_scr[:, cols] - mu) * rstd
            vn = (vhat * g_ref[:, cols] + b_ref[:, cols]).astype(BF16)
            sg = jnp.dot(_masked_spatial(ws_ref, grp), vn, preferred_element_type=F32) + bs_ref[grp]
            o_ref[:, cols] = (_gelu(pre_ref[:, cols]) * sg).astype(o_ref.dtype)

    vec = pl.BlockSpec((1, w), lambda i: (0, 0))
    return _pcall(
        name, body, [pre, ln_g, ln_b, w_s, b_s_col],
        [pl.BlockSpec((CHUNK, w2), lambda i: (i, 0)), vec, vec,
         pl.BlockSpec((A_GROUPS, CHUNK, CHUNK), lambda i: (0, 0, 0)),
         pl.BlockSpec((A_GROUPS, CHUNK, 1), lambda i: (0, 0, 0))],
        _sds((s, w), BF16), pl.BlockSpec((CHUNK, w), lambda i: (i, 0)),
        grid=(s // CHUNK,), sem=("parallel",), scratch=[pltpu.VMEM((CHUNK, w), F32)], after=after)


def _amix_bwd(name, pre, dgated, ln_g, ln_b, w_s, b_s_col, after=None):
    s, w2 = pre.shape
    w = w2 // 2
    head = w // A_GROUPS

    def body(pre_ref, dg_ref, g_ref, b_ref, ws_ref, bs_ref, dpre_ref, glg_ref, glb_ref, gws_ref, gbs_ref,
             v_scr, dvn_scr):
        @pl.when(pl.program_id(0) == 0)
        def _():
            for ref in (glg_ref, glb_ref, gws_ref, gbs_ref):
                ref[...] = jnp.zeros_like(ref)

        mu, rstd = _layernorm_stats(pre_ref, v_scr, w, head)
        mask = _causal_mask()
        sum_dvhat = jnp.zeros((CHUNK, 1), F32)
        sum_dvhat_vhat = jnp.zeros((CHUNK, 1), F32)
        for grp in range(A_GROUPS):
            cols = slice(grp * head, (grp + 1) * head)
            vhat = (v_scr[:, cols] - mu) * rstd
            gain = g_ref[:, cols]
            vn = (vhat * gain + b_ref[:, cols]).astype(BF16)
            wm = _masked_spatial(ws_ref, grp)
            pre_u = pre_ref[:, cols]
            dgated = dg_ref[:, cols]
            ds = dgated * _gelu(pre_u)
            dsb = ds.astype(BF16)
            sg = jnp.dot(wm, vn, preferred_element_type=F32) + bs_ref[grp]
            dpre_ref[:, cols] = (dgated * sg * _gelu_grad(pre_u)).astype(dpre_ref.dtype)
            gws = lax.dot_general(dsb, vn, NT, preferred_element_type=F32)
            gws_ref[grp] += jnp.where(mask, gws, 0.0)
            gbs_ref[grp] += _row_sum(ds)
            dvn = lax.dot_general(wm, dsb, TN, preferred_element_type=F32)
            dvn_scr[:, cols] = dvn
            glg_ref[:, cols] += jnp.sum(dvn * vhat, axis=0, keepdims=True)
            glb_ref[:, cols] += jnp.sum(dvn, axis=0, keepdims=True)
            dvhat = dvn * gain
            sum_dvhat = sum_dvhat + _row_sum(dvhat)
            sum_dvhat_vhat = sum_dvhat_vhat + _row_sum(dvhat * vhat)
        mean_dvhat = sum_dvhat * (1.0 / w)
        mean_dvhat_vhat = sum_dvhat_vhat * (1.0 / w)
        for grp in range(A_GROUPS):
            cols = slice(grp * head, (grp + 1) * head)
            vhat = (v_scr[:, cols] - mu) * rstd
            dvhat = dvn_scr[:, cols] * g_ref[:, cols]
            dv = rstd * (dvhat - mean_dvhat - vhat * mean_dvhat_vhat)
            pre_v = pre_ref[:, w + grp * head:w + (grp + 1) * head]
            dpre_ref[:, w + grp * head:w + (grp + 1) * head] = (dv * _gelu_grad(pre_v)).astype(dpre_ref.dtype)

    vec = pl.BlockSpec((1, w), lambda i: (0, 0))
    ws_spec = pl.BlockSpec((A_GROUPS, CHUNK, CHUNK), lambda i: (0, 0, 0))
    bs_spec = pl.BlockSpec((A_GROUPS, CHUNK, 1), lambda i: (0, 0, 0))
    return _pcall(
        name, body, [pre, dgated, ln_g, ln_b, w_s, b_s_col],
        [pl.BlockSpec((CHUNK, w2), lambda i: (i, 0)), pl.BlockSpec((CHUNK, w), lambda i: (i, 0)),
         vec, vec, ws_spec, bs_spec],
        [_sds((s, w2), BF16), _sds((1, w), F32), _sds((1, w), F32),
         _sds((A_GROUPS, CHUNK, CHUNK), F32), _sds((A_GROUPS, CHUNK, 1), F32)],
        [pl.BlockSpec((CHUNK, w2), lambda i: (i, 0)), vec, vec, ws_spec, bs_spec],
        grid=(s // CHUNK,), sem=("arbitrary",),
        scratch=[pltpu.VMEM((CHUNK, w), F32), pltpu.VMEM((CHUNK, w), F32)], after=after)


def _shift_rows(x, k, forward):
    n = x.shape[0]
    row = lax.broadcasted_iota(jnp.int32, x.shape, 0)
    if forward:
        return jnp.where(row >= k, pltpu.roll(x, k, 0), 0.0)
    return jnp.where(row < n - k, pltpu.roll(x, n - k, 0), 0.0)


def _window_sum(x, window, forward):
    k = 1
    while k < window:
        x = x + _shift_rows(x, k, forward)
        k *= 2
    return x


def _pool(name, v, backward, after=None):
    s, w = v.shape
    head = w // B_GROUPS
    lane = _tile(head, 128)

    def body(v_ref, o_ref):
        grp = pl.program_id(0)
        x = v_ref[...]
        t = lax.broadcasted_iota(jnp.int32, x.shape, 0)
        for idx, window in enumerate(B_WINDOWS):
            @pl.when(grp == idx)
            def _():
                inv_count = 1.0 / jnp.minimum(t + 1, window).astype(F32)
                if backward:
                    out = _window_sum(x * inv_count, window, False) - x
                else:
                    out = _window_sum(x, window, True) * inv_count - x
                o_ref[...] = out.astype(o_ref.dtype)

    per = head // lane
    spec = pl.BlockSpec((s, lane), lambda g, j: (0, g * per + j))
    return _pcall(name, body, [v], [spec], _sds((s, w), BF16), spec, grid=(B_GROUPS, per),
                  sem=("parallel", "parallel"), after=after)


def _colsum(name, a, after=None):
    s, d = a.shape
    tr = _tile(s, 256)

    def body(a_ref, o_ref):
        _accumulate(o_ref, jnp.sum(a_ref[...], axis=0, keepdims=True), pl.program_id(0))

    return _pcall(name, body, [a], [pl.BlockSpec((tr, d), lambda i: (i, 0))], _sds((1, d), F32),
                  pl.BlockSpec((1, d), lambda i: (0, 0)), grid=(s // tr,), sem=("arbitrary",), after=after)


def _adamw(w, g, m, v):
    m = ADAM_B1 * m + (1.0 - ADAM_B1) * g
    v = ADAM_B2 * v + (1.0 - ADAM_B2) * (g * g)
    m_hat = m / (1.0 - ADAM_B1 ** ADAM_STEP)
    v_hat = v / (1.0 - ADAM_B2 ** ADAM_STEP)
    delta = -ADAM_LR * (m_hat / (jnp.sqrt(v_hat) + ADAM_EPS) + ADAM_WD * w)
    return delta, m, v


def _adam_rows(name, g, w, m, v, after=None):
    r, c = g.shape
    tr = _tile(r, 256)

    def body(g_ref, w_ref, m_ref, v_ref, d_ref, nm_ref, nv_ref):
        d_ref[...], nm_ref[...], nv_ref[...] = _adamw(w_ref[...], g_ref[...], m_ref[...], v_ref[...])

    spec = pl.BlockSpec((tr, c), lambda i: (i, 0))
    return _pcall(name, body, [g, w, m, v], [spec] * 4, [_sds((r, c), F32)] * 3, [spec] * 3,
                  grid=(r // tr,), sem=("parallel",), after=after)


def _position():
    return lax.axis_index("x"), lax.axis_index("y"), lax.axis_index("c")


def _other_chips(x, y):
    return [(1 - x, y), (x, 1 - y), (1 - x, 1 - y)]


def _slot(px, py, pc):
    return 4 * px + 2 * py + pc


def _hbm(a):
    return pltpu.with_memory_space_constraint(a, pltpu.HBM)


def _hop1_copies(srcs, lands, send_sems, recv_sems):
    x, y, c = _position()
    peers = [(x, y, 1 - c), (1 - x, y, c), (x, 1 - y, c)]
    mine = _slot(x, y, c)
    return [[pltpu.make_async_remote_copy(
        src_ref=srcs[t], dst_ref=lands[t].at[mine], send_sem=send_sems[t].at[k], recv_sem=recv_sems[t].at[k],
        device_id=peer, device_id_type=MESH) for k, peer in enumerate(peers)] for t in range(len(srcs))]


def _hop2_copies(lands, send_sems, recv_sems):
    x, y, c = _position()
    routes = [(_slot(1 - x, y, c), (x, 1 - y, c)), (_slot(x, 1 - y, c), (1 - x, y, c))]
    out = []
    for t in range(len(lands)):
        rows = lands[t].shape[1]
        halves = [(0, rows // 2), (rows // 2, rows - rows // 2)]
        per_tensor = []
        for h, ((slot, peer), (start, size)) in enumerate(zip(routes, halves)):
            if size:
                block = lands[t].at[slot, pl.ds(start, size)]
                per_tensor.append(pltpu.make_async_remote_copy(
                    src_ref=block, dst_ref=block, send_sem=send_sems[t].at[h], recv_sem=recv_sems[t].at[h],
                    device_id=peer, device_id_type=MESH))
        for j, (slot, _) in enumerate(routes):
            block = lands[t].at[slot]
            per_tensor.append(pltpu.make_async_remote_copy(
                src_ref=block, dst_ref=block, send_sem=send_sems[t].at[2 + j], recv_sem=recv_sems[t].at[2 + j],
                device_id=(x, y, 1 - c), device_id_type=MESH))
        out.append(per_tensor)
    return out


def _split_start(name, srcs, lands, copies, n_sems, after=None):
    n = len(srcs)
    order = [] if after is None else [after]
    n_in = 2 * n + len(order)

    def body(*refs):
        for per_tensor in copies(refs[:n], refs[n:2 * n], refs[n_in:n_in + n], refs[n_in + n:n_in + 2 * n]):
            for cp in per_tensor:
                cp.start()
        refs[-1][...] = jnp.zeros_like(refs[-1])

    out_shape = ([pltpu.SemaphoreType.DMA((n_sems,)) for _ in range(2 * n)]
                 + [pltpu.HBM(a.shape, a.dtype) for a in list(srcs) + list(lands)]
                 + [_sds((8, 128), F32)])
    out = pl.pallas_call(
        body, name=name, out_shape=out_shape, in_specs=[_HBM] * (2 * n) + [_ANY] * len(order),
        out_specs=[_SEM] * (2 * n) + [_HBM] * (2 * n) + [pl.BlockSpec(memory_space=pltpu.VMEM)],
        input_output_aliases={i: 2 * n + i for i in range(2 * n)},
        compiler_params=pltpu.CompilerParams(has_side_effects=_EFFECT),
    )(*[_hbm(a) for a in srcs], *[_hbm(a) for a in lands], *order)
    return [(out[t], out[n + t], out[2 * n + t], out[3 * n + t]) for t in range(n)], out[-1]


def _split_wait(name, started, copies, after):
    n = len(started)

    def body(*refs):
        for per_tensor in copies(refs[:n], refs[n:2 * n], refs[2 * n:3 * n], refs[3 * n:4 * n]):
            for cp in per_tensor:
                cp.wait_send()
                cp.wait_recv()

    srcs = [e[2] for e in started]
    lands = [e[3] for e in started]
    out = pl.pallas_call(
        body, name=name, out_shape=[pltpu.HBM(a.shape, a.dtype) for a in srcs + lands],
        in_specs=[_HBM] * (2 * n) + [_SEM] * (2 * n) + [_ANY], out_specs=[_HBM] * (2 * n),
        input_output_aliases={i: i for i in range(2 * n)},
        compiler_params=pltpu.CompilerParams(has_side_effects=_EFFECT),
    )(*srcs, *lands, *[e[0] for e in started], *[e[1] for e in started], after)
    return out[:n], out[n:]


def _gather_step(name, arrived, fresh, after=None):
    n, m = len(arrived), len(fresh)
    order = [] if after is None else [after]
    fresh_lands = [lax.empty((N_DEV,) + s.shape, s.dtype) for s in fresh]
    buffers = [e[2] for e in arrived] + [e[3] for e in arrived] + list(fresh) + fresh_lands
    old_sems = [e[0] for e in arrived] + [e[1] for e in arrived]
    n_buf, n_old = len(buffers), len(old_sems)
    first_new = n_buf + n_old + len(order)

    def body(*refs):
        bufs, old = refs[:n_buf], refs[n_buf:n_buf + n_old]
        new = refs[first_new:first_new + 2 * n + 2 * m]
        for per_tensor in _hop1_copies(bufs[:n], bufs[n:2 * n], old[:n], old[n:]):
            for cp in per_tensor:
                cp.wait_send()
                cp.wait_recv()
        second = _hop2_copies(bufs[n:2 * n], new[:n], new[n:2 * n])
        first = _hop1_copies(bufs[2 * n:2 * n + m], bufs[2 * n + m:], new[2 * n:2 * n + m], new[2 * n + m:])
        for per_tensor in second + first:
            for cp in per_tensor:
                cp.start()
        refs[-1][...] = jnp.zeros_like(refs[-1])

    n_new = 2 * n + 2 * m
    out_shape = ([pltpu.SemaphoreType.DMA((4,)) for _ in range(2 * n)]
                 + [pltpu.SemaphoreType.DMA((3,)) for _ in range(2 * m)]
                 + [pltpu.HBM(a.shape, a.dtype) for a in buffers] + [_sds((8, 128), F32)])
    out = pl.pallas_call(
        body, name=name, out_shape=out_shape,
        in_specs=[_HBM] * n_buf + [_SEM] * n_old + [_ANY] * len(order),
        out_specs=[_SEM] * n_new + [_HBM] * n_buf + [pl.BlockSpec(memory_space=pltpu.VMEM)],
        input_output_aliases={i: n_new + i for i in range(n_buf)},
        compiler_params=pltpu.CompilerParams(has_side_effects=_EFFECT),
    )(*[_hbm(a) for a in buffers], *old_sems, *order)
    sems, bufs = out[:n_new], out[n_new:n_new + n_buf]
    second = [(sems[t], sems[n + t], bufs[t], bufs[n + t]) for t in range(n)]
    first = [(sems[2 * n + t], sems[2 * n + m + t], bufs[2 * n + t], bufs[2 * n + m + t]) for t in range(m)]
    return second, first, out[-1]


def _gather_wait(name, second, after):
    return _split_wait(name, second, lambda srcs, lands, send, recv: _hop2_copies(lands, send, recv), after)


def _sibling_copies(srcs, lands, send_sems, recv_sems):
    x, y, c = _position()
    return [[pltpu.make_async_remote_copy(
        src_ref=srcs[t], dst_ref=lands[t], send_sem=send_sems[t].at[0], recv_sem=recv_sems[t].at[0],
        device_id=(x, y, 1 - c), device_id_type=MESH)] for t in range(len(srcs))]


def _sibling_and_scatter_start(name, arrays, partials):
    k = len(arrays)

    def copies(srcs, lands, send_sems, recv_sems):
        return (_sibling_copies(srcs[:k], lands[:k], send_sems[:k], recv_sems[:k])
                + _scatter_copies(srcs[k:], lands[k:], send_sems[k:], recv_sems[k:]))

    lands = ([lax.empty(a.shape, a.dtype) for a in arrays]
             + [lax.empty((N_CHIPS - 1,) + p.shape[1:], p.dtype) for p in partials])
    return _split_start(name, list(arrays) + list(partials), lands, copies, 3)


def _sibling_wait(name, started, after):
    return _split_wait(name, started, _sibling_copies, after)[1]


def _small_copies(srcs, lands, send_sems, recv_sems):
    x, y, c = _position()
    mine = _slot(x, y, c)
    peers = [(x ^ ((k >> 2) & 1), y ^ ((k >> 1) & 1), c ^ (k & 1)) for k in range(1, N_DEV)]
    return [[pltpu.make_async_remote_copy(
        src_ref=srcs[t], dst_ref=lands[t].at[mine], send_sem=send_sems[t].at[k], recv_sem=recv_sems[t].at[k],
        device_id=peer, device_id_type=MESH) for k, peer in enumerate(peers)] for t in range(len(srcs))]


def _gather_finish(name, shards, lands, after):
    n = len(shards)

    def body(*refs):
        srcs, lands_in, outs = refs[:n], refs[n:2 * n], refs[2 * n:3 * n]
        send_sems, recv_sems, local_sems = refs[3 * n:]
        x, y, c = _position()
        local = [pltpu.make_async_copy(srcs[t], outs[t].at[_slot(x, y, c)], local_sems.at[t]) for t in range(n)]

        def diagonal(t, core):
            block = outs[t].at[_slot(1 - x, 1 - y, core)]
            return pltpu.make_async_remote_copy(
                src_ref=block, dst_ref=block, send_sem=send_sems.at[t], recv_sem=recv_sems.at[t],
                device_id=(x, y, 1 - c), device_id_type=MESH)

        for cp in local:
            cp.start()
        for t in range(n):
            diagonal(t, c).start()
        for t in range(n):
            diagonal(t, c).wait_send()
            diagonal(t, 1 - c).wait_recv()
        for cp in local:
            cp.wait()

    return _pcall(name, body, [*shards, *lands], [_ANY] * (2 * n),
                  [_sds(l.shape, l.dtype) for l in lands], [_ANY] * n,
                  scratch=[pltpu.SemaphoreType.DMA((n,)), pltpu.SemaphoreType.DMA((n,)),
                           pltpu.SemaphoreType.DMA((n,))],
                  after=after, aliases={n + t: t for t in range(n)})


def _exchange_sibling(name, fulls, after):
    n = len(fulls)

    def body(*refs):
        src = refs[:n]
        out = refs[n:2 * n]
        send_sems, recv_sems = refs[2 * n:]
        x, y, c = _position()
        copies = [pltpu.make_async_remote_copy(
            src_ref=src[t].at[:, 1 - c], dst_ref=out[t], send_sem=send_sems.at[t], recv_sem=recv_sems.at[t],
            device_id=(x, y, 1 - c), device_id_type=MESH) for t in range(n)]
        for cp in copies:
            cp.start()
        for cp in copies:
            cp.wait()

    return _pcall(name, body, fulls, [_ANY] * n, [_sds((N_CHIPS,) + f.shape[2:], f.dtype) for f in fulls],
                  [_ANY] * n, scratch=[pltpu.SemaphoreType.DMA((n,)), pltpu.SemaphoreType.DMA((n,))],
                  after=after)


def _add_sibling(name, full, recv, core, after):
    _, _, r, c = full.shape
    tr = _tile(r, max(8, (256 * 1024) // c))

    def body(core_ref, f_ref, r_ref, o_ref):
        o_ref[...] = (f_ref[...].astype(F32) + r_ref[...].astype(F32)).astype(o_ref.dtype)

    return _pcall(
        name, body, [full, recv],
        [pl.BlockSpec((None, None, tr, c), lambda p, i, core_ref: (p, core_ref[0], i, 0)),
         pl.BlockSpec((None, tr, c), lambda p, i, core_ref: (p, i, 0))],
        _sds((N_CHIPS, r, c), BF16), pl.BlockSpec((None, tr, c), lambda p, i, core_ref: (p, i, 0)),
        grid=(N_CHIPS, r // tr), sem=("parallel", "parallel"), prefetch=[core], after=after)


def _scatter_copies(srcs, lands, send_sems, recv_sems):
    x, y, c = _position()
    return [[pltpu.make_async_remote_copy(
        src_ref=srcs[t].at[2 * px + py], dst_ref=lands[t].at[j],
        send_sem=send_sems[t].at[j], recv_sem=recv_sems[t].at[j],
        device_id=(px, py, c), device_id_type=MESH) for j, (px, py) in enumerate(_other_chips(x, y))]
        for t in range(len(srcs))]


def _scatter_wait(name, started, after):
    return _split_wait(name, started, _scatter_copies, after)


class _Job:
    def __init__(self, **fields):
        self.__dict__.update(fields)


def _reduce_adam_job(partial, recv, chip, w, m, v, layer, carried):
    n_layers, r, c = w.shape
    tr = _tile(r, max(ROW_CHUNK, (512 * 1024) // c))

    def body(p_ref, r_ref, w_ref, m_ref, v_ref, *rest):
        g_ref, d_ref, nm_ref, nv_ref = rest[-4:]

        def chunk(i, carry):
            rows = pl.ds(pl.multiple_of(i * ROW_CHUNK, ROW_CHUNK), ROW_CHUNK)
            g = p_ref[rows, :].astype(F32)
            for j in range(N_CHIPS - 1):
                g = g + r_ref[j, rows, :].astype(F32)
            g_ref[rows, :] = g
            d_ref[rows, :], nm_ref[rows, :], nv_ref[rows, :] = _adamw(w_ref[rows, :], g, m_ref[rows, :],
                                                                     v_ref[rows, :])
            return carry

        lax.fori_loop(0, tr // ROW_CHUNK, chunk, 0, unroll=8)

    layered = ((None, tr, c), lambda blk, chip_ref: (layer, blk, 0))
    in_specs = [((None, tr, c), lambda blk, chip_ref: (chip_ref[0], blk, 0)),
                ((N_CHIPS - 1, tr, c), lambda blk, chip_ref: (0, blk, 0)), layered, layered, layered]
    operands = [partial, recv, w, m, v]
    aliases = {}
    if carried is not None:
        operands += list(carried)
        in_specs += [None] * 4
        aliases = {5 + o: o for o in range(4)}
    return _Job(operands=operands, in_specs=in_specs, out_shape=[_sds((n_layers, r, c), F32)] * 4,
                out_specs=[layered] * 4, body=body, aliases=aliases, prefetch=chip, n_blocks=r // tr)


def _run_job(name, job, after):
    def spec(entry):
        if entry is None:
            return _ANY
        shape, index = entry
        return pl.BlockSpec(shape, lambda blk, pre, index=index: index(blk, pre))

    def body(pre_ref, *refs):
        job.body(*refs)

    return _pcall(name, body, job.operands, [spec(e) for e in job.in_specs], job.out_shape,
                  [spec(e) for e in job.out_specs], grid=(job.n_blocks,), sem=("parallel",),
                  prefetch=[job.prefetch], after=after,
                  aliases={1 + i: o for i, o in job.aliases.items()})


def _small_sum(name, gathered, own, device, after=None):
    r, lanes = own.shape

    def body(dev_ref, g_ref, own_ref, out_ref):
        dev = dev_ref[0]
        mine = own_ref[...]
        total = jnp.where(dev == 0, mine, g_ref[0])
        for d in range(1, N_DEV):
            total = total + jnp.where(dev == d, mine, g_ref[d])
        out_ref[...] = total

    return _pcall(name, body, [gathered, own],
                  [pl.BlockSpec((N_DEV, r, lanes), lambda i, dev_ref: (0, 0, 0)),
                   pl.BlockSpec((r, lanes), lambda i, dev_ref: (0, 0))],
                  _sds((r, lanes), F32), pl.BlockSpec((r, lanes), lambda i, dev_ref: (0, 0)),
                  grid=(1,), sem=("arbitrary",), prefetch=[device], after=after)


def _pack(arrays):
    return jnp.concatenate([a.reshape(-1, 128) for a in arrays], axis=0)


def _unpack(packed, shapes):
    out, row = [], 0
    for shape in shapes:
        rows = math.prod(shape) // 128
        out.append(packed[row:row + rows].reshape(shape))
        row += rows
    return out


class _Order:
    def __init__(self):
        self.last = None

    def __call__(self, fn, *args, **kwargs):
        out = fn(*args, after=self.last, **kwargs)
        self.last = out[0] if isinstance(out, (list, tuple)) else out
        return out


def kernel(x, a_w_in, a_ln_g, a_ln_b, a_w_s, a_b_s, a_w_out, b_w_in, b_w_grp, b_scale, b_w_out, norm_mix, norm_mlp, mlp_w1, mlp_w2, final_norm, loss_target, m_a_w_in, m_a_ln_g, m_a_ln_b, m_a_w_s, m_a_b_s, m_a_w_out, m_b_w_in, m_b_w_grp, m_b_scale, m_b_w_out, m_norm_mix, m_norm_mlp, m_mlp_w1, m_mlp_w2, m_final_norm, v_a_w_in, v_a_ln_g, v_a_ln_b, v_a_w_s, v_a_b_s, v_a_w_out, v_b_w_in, v_b_w_grp, v_b_scale, v_b_w_out, v_norm_mix, v_norm_mlp, v_mlp_w1, v_mlp_w2, v_final_norm):
    s, d = x.shape[1], x.shape[2]
    depth = mlp_w1.shape[0]
    a_slab = a_w_in.shape[2]
    ff_slab = mlp_w1.shape[2]
    ff_rows = mlp_w2.shape[1]
    bh = b_w_grp.shape[3]
    my_x, my_y, my_c = _position()
    core = jnp.reshape(my_c, (1,)).astype(jnp.int32)
    chip = jnp.reshape(2 * my_x + my_y, (1,)).astype(jnp.int32)
    device = _slot(my_x, my_y, my_c)
    run = _Order()

    w1_b, w2_b = mlp_w1.astype(BF16), mlp_w2.astype(BF16)
    shards = [a_w_in[0].astype(BF16), a_w_out[0].astype(BF16), b_scale,
              w1_b[0], w2_b[0],
              b_w_in[0].astype(BF16), b_w_grp[0].astype(BF16), b_w_out[0].astype(BF16),
              w1_b[1], w2_b[1]]
    groups = [[0], [1, 2], [3], [4], [5, 6, 7], [8], [9]]
    start_with = {1: [2, 3], 2: [4], 3: [5], 4: [6]}
    hop1, hop2 = {}, {}
    _, hop1[0], token = _gather_step("weights_group0_hop1", [], [shards[t] for t in groups[0]])
    _, hop1[1], token = _gather_step("weights_group1_hop1", [], [shards[t] for t in groups[1]], token)
    run.last = token

    def advance(g):
        if g not in hop1:
            return
        ahead = start_with.get(g, [])
        fresh = [shards[t] for a in ahead for t in groups[a]]
        hop2[g], started, tok = _gather_step(f"weights_group{g}_hop2", hop1.pop(g), fresh, run.last)
        for a in ahead:
            hop1[a], started = started[:len(groups[a])], started[len(groups[a]):]
        run.last = tok

    def gathered(g):
        advance(g)
        if g == 0:
            advance(1)
        srcs, lands = _gather_wait(f"weights_group{g}_wait", hop2.pop(g), run.last)
        run.last = srcs[0]
        return run(_gather_finish, f"weights_group{g}_finish", srcs, lands)

    h0 = x[0]
    target = loss_target[0]
    ln_g, ln_b = a_ln_g, a_ln_b
    w_s = a_w_s[0]
    b_s_col = a_b_s[0][:, :, None]
    nmix = [norm_mix[l][None, :] for l in range(depth)]
    nmlp = [norm_mlp[l][None, :] for l in range(depth)]

    def mlp_forward(l, h, up_group):
        hn = run(_rms_fwd, f"mlp{l}_norm", h, nmlp[l])
        (w1,) = gathered(up_group)
        advance(up_group + 1)
        act, act_sq = run(_mm_nn, f"mlp{l}_up", hn, w1,
                          lambda acc: (jnp.maximum(acc, 0.0), jnp.square(jnp.maximum(acc, 0.0))),
                          (BF16, BF16), slab=True)
        (w2,) = gathered(up_group + 1)
        advance(up_group + 2)
        w2 = w2.reshape(-1, d)
        (h_out,) = run(_mm_nn, f"mlp{l}_down", act_sq, w2, lambda acc, res: (acc + res,), (F32,),
                       extras=(h,), extra_kinds=("tile",))
        return h_out, (h, hn, act, act_sq, w1, w2)

    scattered = []

    pending = []

    def scatter_partials(name, partials, specs):
        pending.append((name, partials, specs))

    def start_exchanges(name, to_sibling):
        partials = [p for _, group, _ in pending for p in group]
        in_flight, tok = _sibling_and_scatter_start(name, to_sibling, partials)
        run.last = tok
        first = len(to_sibling)
        for group_name, group, specs in pending:
            scattered.append((group_name, in_flight[first:first + len(group)], specs))
            first += len(group)
        pending.clear()
        return in_flight[:len(to_sibling)]

    weights = {"a_w_in": (a_w_in, m_a_w_in, v_a_w_in), "a_w_out": (a_w_out, m_a_w_out, v_a_w_out),
               "b_w_in": (b_w_in, m_b_w_in, v_b_w_in), "b_w_grp": (b_w_grp, m_b_w_grp, v_b_w_grp),
               "b_w_out": (b_w_out, m_b_w_out, v_b_w_out), "mlp_w1": (mlp_w1, m_mlp_w1, v_mlp_w1),
               "mlp_w2": (mlp_w2, m_mlp_w2, v_mlp_w2)}
    results = {}

    def finish_group(name, in_flight, specs):
        partials, lands = _scatter_wait(name + "_scatter_wait", in_flight, run.last)
        run.last = lands[0]
        for t, (wname, layer) in enumerate(specs):
            w, m, v = weights[wname]
            shape = (w.shape[0],) + partials[t].shape[1:]
            job = _reduce_adam_job(partials[t], lands[t], chip, w.reshape(shape), m.reshape(shape),
                                   v.reshape(shape), layer, results.get(wname))
            results[wname] = run(_run_job, f"{name}_reduce_adam_{t}", job)

    def weight_grad(name, a, b, by_rows, block, between, tm=1024, tn=1024):
        other = run(_mm_tn_half, name + "_other", a, b, core, False, by_rows, block, tm=tm, tn=tn)
        sent = start_exchanges(name + "_sibling_start", [other])
        middle = between()
        (recv,) = _sibling_wait(name + "_sibling_wait", sent, run.last)
        run.last = recv
        return run(_mm_tn_half, name + "_own", a, b, core, True, by_rows, block, recv=recv, tm=tm, tn=tn), middle

    def mlp_backward(l, saved, dh, dhb):
        h, hn, act, act_sq, w1, w2 = saved
        part_w2, (dpre,) = weight_grad(
            f"mlp{l}_down_dw", act_sq, dhb, True, ff_rows,
            lambda: run(_mm_nt, f"mlp{l}_down_dx", dhb, w2, lambda acc, a: (2.0 * a.astype(F32) * acc,),
                        (BF16,), extras=(act,), extra_kinds=("tile",)))
        scatter_partials(f"mlp{l}_down_grads", [part_w2], [("mlp_w2", l)])
        part_w1, (dhn,) = weight_grad(
            f"mlp{l}_up_dw", hn, dpre, False, ff_slab,
            lambda: run(_mm_nt, f"mlp{l}_up_dx", dpre, w1, lambda acc: (acc,), (F32,), slab=True))
        scatter_partials(f"mlp{l}_up_grads", [part_w1], [("mlp_w1", l)])
        dh, dhb, g_norm = run(_rms_bwd, f"mlp{l}_norm_bwd", dhn, h, nmlp[l], dh)
        return dh, dhb, g_norm

    hn0 = run(_rms_fwd, "mix0_norm", h0, nmix[0])
    (wa_in,) = gathered(0)
    (pre,) = run(_mm_nn, "mixa_in", hn0, wa_in, lambda acc: (acc,), (F32,), slab=True)
    wa_out, scale = gathered(1)
    wa_out, scale = wa_out.reshape(d, d), scale.reshape(1, d)
    gated = run(_amix_fwd, "mixa_gate", pre, ln_g, ln_b, w_s, b_s_col)
    advance(2)
    (h1,) = run(_mm_nn, "mixa_out", gated, wa_out, lambda acc, res: (acc + res,), (F32,),
                extras=(h0,), extra_kinds=("tile",))
    h2, saved_mlp0 = mlp_forward(0, h1, 2)
    hn2 = run(_rms_fwd, "mix1_norm", h2, nmix[1])
    wb_in, wb_grp, wb_out = gathered(4)
    advance(5)
    wb_in, wb_out = wb_in.reshape(d, d), wb_out.reshape(d, d)
    wb_grp = jnp.transpose(wb_grp, (1, 0, 2, 3)).reshape(B_GROUPS, bh, bh)
    (vb,) = run(_mm_nn, "mixb_in", hn2, wb_in, lambda acc: (acc,), (F32,))
    pooled = run(_pool, "mixb_pool", vb, backward=False)
    tm = _tile(s, 1024)
    grp_tile = pl.BlockSpec((tm, bh), lambda i, j, k: (i, j))
    grp_weight = pl.BlockSpec((None, bh, bh), lambda i, j, k: (j, 0, 0))
    mixed, mixed_scaled = run(
        _matmul, "mixb_grp", pooled, wb_grp, NN, (s // tm, B_GROUPS, 1), grp_tile, grp_weight,
        [_sds((s, d), BF16), _sds((s, d), BF16)], [grp_tile] * 2,
        (tm, bh), lambda acc, sc: (acc, acc * sc), (scale,), [pl.BlockSpec((1, bh), lambda i, j, k: (0, j))])
    (h3,) = run(_mm_nn, "mixb_out", mixed_scaled, wb_out, lambda acc, res: (acc + res,), (F32,),
                extras=(h2,), extra_kinds=("tile",))
    h4, saved_mlp1 = mlp_forward(1, h3, 5)
    dh, dhb, g_final, loss_part = run(_loss_head, "loss_head", h4, final_norm[None, :], target)

    dh, dhb, g_nmlp1 = mlp_backward(1, saved_mlp1, dh, dhb)
    tks = _tile(s, 1024)
    grp_rows = pl.BlockSpec((tks, bh), lambda i, j, k: (k, j))

    def mixb_middle():
        dms_scaled, dms_mixed = run(
            _mm_nt, "mixb_out_dx", dhb, wb_out,
            lambda acc, sc, mx: (acc * sc, acc * mx.astype(F32)), (BF16, F32),
            extras=(scale, mixed), extra_kinds=("row", "tile"))
        g_scale = run(_colsum, "mixb_scale_dw", dms_mixed)
        (g_wb_grp,) = run(
            _matmul, "mixb_grp_dw", pooled, dms_scaled, TN, (1, B_GROUPS, s // tks), grp_rows, grp_rows,
            [_sds((B_GROUPS, bh, bh), BF16)], [grp_weight], (bh, bh), lambda acc: (acc,))
        (dpooled,) = run(
            _matmul, "mixb_grp_dx", dms_scaled, wb_grp, NT, (s // tm, B_GROUPS, 1), grp_tile, grp_weight,
            [_sds((s, d), F32)], [grp_tile], (tm, bh), lambda acc: (acc,))
        return g_scale, g_wb_grp, run(_pool, "mixb_pool_bwd", dpooled, backward=True)

    part_wb_out, (g_scale, g_wb_grp, dvb) = weight_grad("mixb_out_dw", mixed_scaled, dhb, True, d // N_DEV,
                                                        mixb_middle, tn=d)
    part_wb_in, (dhn2,) = weight_grad(
        "mixb_in_dw", hn2, dvb, True, d // N_DEV,
        lambda: run(_mm_nt, "mixb_in_dx", dvb, wb_in, lambda acc: (acc,), (F32,)), tn=d)
    grp_full = jnp.transpose(g_wb_grp.reshape(B_GROUPS, N_DEV, bh // N_DEV, bh), (1, 0, 2, 3))
    grp_full = grp_full.reshape(N_CHIPS, 2, B_GROUPS * bh // N_DEV, bh)
    (grp_sibling,) = run(_exchange_sibling, "mixb_grp_dw_to_sibling", [grp_full])
    part_wb_grp = run(_add_sibling, "mixb_grp_dw_add_sibling", grp_full, grp_sibling, core)
    scatter_partials("mixb_grads", [part_wb_out, part_wb_grp, part_wb_in],
                     [("b_w_out", 0), ("b_w_grp", 0), ("b_w_in", 0)])
    dh, dhb, g_nmix1 = run(_rms_bwd, "mix1_norm_bwd", dhn2, h2, nmix[1], dh)
    dh, dhb, g_nmlp0 = mlp_backward(0, saved_mlp0, dh, dhb)
    def mixa_middle():
        (dgated,) = run(_mm_nt, "mixa_out_dx", dhb, wa_out, lambda acc: (acc,), (F32,))
        return run(_amix_bwd, "mixa_gate_bwd", pre, dgated, ln_g, ln_b, w_s, b_s_col)

    part_wa_out, (dpre, g_ln_g, g_ln_b, g_w_s, g_b_s) = weight_grad("mixa_out_dw", gated, dhb, True, d // N_DEV,
                                                                     mixa_middle, tn=d)
    part_wa_in, (dhn0,) = weight_grad(
        "mixa_in_dw", hn0, dpre, False, a_slab,
        lambda: run(_mm_nt, "mixa_in_dx", dpre, wa_in, lambda acc: (acc,), (F32,), slab=True), tm=d)
    scatter_partials("mixa_grads", [part_wa_in, part_wa_out], [("a_w_in", 0), ("a_w_out", 0)])
    start_exchanges("mixa_grads_scatter_start", [])
    grad_x, _, g_nmix0 = run(_rms_bwd, "mix0_norm_bwd", dhn0, h0, nmix[0], dh)

    g_norm_mix = jnp.concatenate([g_nmix0, g_nmix1], axis=0)
    g_norm_mlp = jnp.concatenate([g_nmlp0, g_nmlp1], axis=0)
    loss_row = jnp.pad(loss_part, ((0, 0), (0, 127)))
    small_parts = [g_ln_g, g_ln_b, g_w_s, g_b_s, g_norm_mix, g_norm_mlp, g_final, g_scale, loss_row]
    packed = _pack(small_parts)
    small_sent, tok = _split_start("small_grads_start", [packed], [lax.empty((N_DEV,) + packed.shape, F32)],
                                   _small_copies, N_DEV - 1)
    run.last = tok

    for group in scattered[:-1]:
        finish_group(*group)
    own_packed, small_gathered = _split_wait("small_grads_wait", small_sent, _small_copies, run.last)
    run.last = small_gathered[0]
    small_sum = run(_small_sum, "small_grads_sum", small_gathered[0], own_packed[0],
                    jnp.reshape(device, (1,)).astype(jnp.int32))
    sg = _unpack(small_sum, [a_ln_g.shape, a_ln_b.shape, a_w_s.shape, a_b_s.shape, norm_mix.shape,
                             norm_mlp.shape, final_norm.shape, (1, d), (1, 128)])
    loss = sg.pop()[0, 0]
    shard = b_scale.shape[1]
    sg[7] = lax.dynamic_slice(sg[7], (0, device * shard), (1, shard))
    small_w = [a_ln_g, a_ln_b, a_w_s, a_b_s, norm_mix, norm_mlp, final_norm, b_scale]
    small_m = [m_a_ln_g, m_a_ln_b, m_a_w_s, m_a_b_s, m_norm_mix, m_norm_mlp, m_final_norm, m_b_scale]
    small_v = [v_a_ln_g, v_a_ln_b, v_a_w_s, v_a_b_s, v_norm_mix, v_norm_mlp, v_final_norm, v_b_scale]
    small_out = run(_adam_rows, "small_adam", _pack(sg), _pack(small_w), _pack(small_m), _pack(small_v))
    shapes = [w.shape for w in small_w]
    small_res = [sg] + [_unpack(o, shapes) for o in small_out]

    finish_group(*scattered[-1])
    big = {wname: [o.reshape(weights[wname][0].shape) for o in outs] for wname, outs in results.items()}

    def leaf(o):
        return (big["a_w_in"][o], small_res[o][0], small_res[o][1], small_res[o][2], small_res[o][3],
                big["a_w_out"][o], big["b_w_in"][o], big["b_w_grp"][o], small_res[o][7], big["b_w_out"][o],
                small_res[o][4], small_res[o][5], big["mlp_w1"][o], big["mlp_w2"][o], small_res[o][6])

    return (loss, grad_x[None], *leaf(0), *leaf(1), *leaf(2), *leaf(3))
```

```python
import math

import jax
import jax.numpy as jnp
from jax import lax
from jax.experimental import pallas as pl
from jax.experimental.pallas import tpu as pltpu

F32 = jnp.float32
BF16 = jnp.bfloat16
MESH = pl.DeviceIdType.MESH

N_DEV = 8
N_CHIPS = 4
CHUNK = 128
A_GROUPS = 8
B_WINDOWS = (2, 4, 8, 16)
B_GROUPS = len(B_WINDOWS)
EPS = 1e-6
ADAM_LR = 0.001
ADAM_B1 = 0.9
ADAM_B2 = 0.999
ADAM_EPS = 1e-08
ADAM_WD = 0.01
ADAM_STEP = 10

VMEM_LIMIT = 48 * 1024 * 1024
ROW_CHUNK = 16

NN = (((1,), (0,)), ((), ()))
NT = (((1,), (1,)), ((), ()))
TN = (((0,), (0,)), ((), ()))

_ANY = pl.BlockSpec(memory_space=pl.ANY)
_HBM = pl.BlockSpec(memory_space=pltpu.HBM)
_SEM = pl.BlockSpec(memory_space=pltpu.SEMAPHORE)
_EFFECT = pltpu.SideEffectType.DATAFLOW_SIDE_EFFECTING


def _tile(n, pref):
    return pref if n % pref == 0 else n


def _sds(shape, dtype):
    return jax.ShapeDtypeStruct(shape, dtype)


def _pcall(name, body, operands, in_specs, out_shape, out_specs, *, grid=None, sem=None, scratch=(),
           prefetch=(), after=None, aliases=None):
    after = [] if after is None else [after]
    n_lead = len(prefetch) + len(operands)
    n_after = len(after)

    def wrapped(*refs):
        body(*refs[:n_lead], *refs[n_lead + n_after:])

    in_specs = list(in_specs) + [_ANY] * n_after
    params = pltpu.CompilerParams(vmem_limit_bytes=VMEM_LIMIT) if sem is None else \
        pltpu.CompilerParams(dimension_semantics=sem, vmem_limit_bytes=VMEM_LIMIT)
    kwargs = dict(out_shape=out_shape, scratch_shapes=list(scratch), compiler_params=params, name=name,
                  input_output_aliases=aliases or {})
    if prefetch:
        kwargs["grid_spec"] = pltpu.PrefetchScalarGridSpec(
            num_scalar_prefetch=len(prefetch), grid=grid, in_specs=in_specs, out_specs=out_specs,
            scratch_shapes=list(scratch))
        kwargs.pop("scratch_shapes")
    else:
        kwargs.update(in_specs=in_specs, out_specs=out_specs)
        if grid is not None:
            kwargs["grid"] = grid
    return pl.pallas_call(wrapped, **kwargs)(*prefetch, *operands, *after)


def _matmul(name, a, b, dims, grid, a_spec, b_spec, out_shape, out_specs, acc_shape,
            epilogue, extras=(), extra_specs=(), after=None, prefetch=(), b_parts=1):
    nk = grid[2]
    n_extra = len(extras)
    n_out = len(out_shape)
    n_pre = len(prefetch)

    def body(*refs):
        refs = refs[n_pre:]
        a_ref, b_ref = refs[0], refs[1]
        extra_refs = refs[2:2 + n_extra]
        out_refs = refs[2 + n_extra:2 + n_extra + n_out]

        def finish(acc):
            outs = epilogue(acc, *[r[...] for r in extra_refs])
            for o_ref, o in zip(out_refs, outs):
                o_ref[...] = o.astype(o_ref.dtype)

        def product():
            if b_parts == 1:
                return lax.dot_general(a_ref[...], b_ref[...], dims, preferred_element_type=F32)
            width = b_ref.shape[2]
            total = None
            for p in range(b_parts):
                part = lax.dot_general(a_ref[:, p * width:(p + 1) * width], b_ref[p], dims,
                                       preferred_element_type=F32)
                total = part if total is None else total + part
            return total

        if nk == 1:
            finish(product())
        else:
            acc_ref = refs[-1]
            k = pl.program_id(2)

            @pl.when(k == 0)
            def _():
                acc_ref[...] = product()

            if nk > 2:
                @pl.when(jnp.logical_and(k > 0, k < nk - 1))
                def _():
                    acc_ref[...] += product()

            @pl.when(k == nk - 1)
            def _():
                finish(acc_ref[...] + product())

    scratch = [] if nk == 1 else [pltpu.VMEM(acc_shape, F32)]
    return _pcall(name, body, [a, b, *extras], [a_spec, b_spec, *extra_specs], out_shape, out_specs,
                  grid=grid, sem=("parallel", "parallel", "arbitrary"), scratch=scratch, after=after,
                  prefetch=prefetch)


def _mm_nn(name, a, b, epilogue, out_dtypes, extras=(), extra_kinds=(), slab=False, after=None,
           tm=1024, tn=1024, tk=2048):
    m, kd = a.shape
    if slab:
        n_slab, _, w = b.shape
        n = n_slab * w
        tn = _tile(w, min(tn, w))
        per = w // tn
        tk = _tile(kd, tk)
        b_spec = pl.BlockSpec((None, tk, tn), lambda i, j, k: (j // per, k, j % per))
    else:
        n = b.shape[1]
        tn = _tile(n, tn)
        tk = _tile(kd, tk)
        b_spec = pl.BlockSpec((tk, tn), lambda i, j, k: (k, j))
    tm = _tile(m, tm)
    grid = (m // tm, n // tn, kd // tk)
    a_spec = pl.BlockSpec((tm, tk), lambda i, j, k: (i, k))
    tile_spec = pl.BlockSpec((tm, tn), lambda i, j, k: (i, j))
    row_spec = pl.BlockSpec((1, tn), lambda i, j, k: (0, j))
    extra_specs = [tile_spec if kind == "tile" else row_spec for kind in extra_kinds]
    return _matmul(name, a, b, NN, grid, a_spec, b_spec,
                   [_sds((m, n), d) for d in out_dtypes], [tile_spec for _ in out_dtypes],
                   (tm, tn), epilogue, extras, extra_specs, after=after)


def _mm_nt(name, a, b, epilogue, out_dtypes, extras=(), extra_kinds=(), slab=False, after=None,
           tm=1024, tn=1024, tk=2048):
    m, kd = a.shape
    parts = 1
    if slab:
        n_slab, n, w = b.shape
        tn = _tile(n, tn)
        if tk > w and tk % w == 0 and n_slab % (tk // w) == 0:
            parts = tk // w
            b_spec = pl.BlockSpec((parts, tn, w), lambda i, j, k, *_: (k, j, 0))
        else:
            tk = _tile(w, min(tk, w))
            per = w // tk
            b_spec = pl.BlockSpec((None, tn, tk), lambda i, j, k, *_: (k // per, j, k % per))
    else:
        n = b.shape[0]
        tn = _tile(n, tn)
        tk = _tile(kd, tk)
        b_spec = pl.BlockSpec((tn, tk), lambda i, j, k, *_: (j, k))
    tm = _tile(m, tm)
    grid = (m // tm, n // tn, kd // tk)
    a_spec = pl.BlockSpec((tm, tk), lambda i, j, k, *_: (i, k))
    tile_spec = pl.BlockSpec((tm, tn), lambda i, j, k, *_: (i, j))
    row_spec = pl.BlockSpec((1, tn), lambda i, j, k, *_: (0, j))
    extra_specs = [tile_spec if kind == "tile" else row_spec for kind in extra_kinds]
    return _matmul(name, a, b, NT, grid, a_spec, b_spec,
                   [_sds((m, n), d) for d in out_dtypes], [tile_spec for _ in out_dtypes],
                   (tm, tn), epilogue, extras, extra_specs, after=after, b_parts=parts)


def _mm_tn_half(name, a, b, core, own, by_rows, block, recv=None, after=None, tm=1024, tn=1024, tk=2048):
    s, m = a.shape
    n = b.shape[1]
    tk = _tile(s, tk)

    def owner(chip, core_ref):
        return 2 * chip + (core_ref[0] if own else 1 - core_ref[0])

    if by_rows:
        r, c = block, n
        tm, tn = _tile(r, min(tm, r)), _tile(c, tn)
        per = r // tm
        grid = (N_CHIPS * per, c // tn, s // tk)
        a_spec = pl.BlockSpec((tk, tm), lambda i, j, k, cr, *_: (k, owner(i // per, cr) * per + i % per))
        b_spec = pl.BlockSpec((tk, tn), lambda i, j, k, cr, *_: (k, j))
        o_spec = pl.BlockSpec((None, tm, tn), lambda i, j, k, cr, *_: (i // per, i % per, j))
    else:
        r, c = m, block
        tm, tn = _tile(r, tm), _tile(c, min(tn, c))
        per = c // tn
        grid = (r // tm, N_CHIPS * per, s // tk)
        a_spec = pl.BlockSpec((tk, tm), lambda i, j, k, cr, *_: (k, i))
        b_spec = pl.BlockSpec((tk, tn), lambda i, j, k, cr, *_: (k, owner(j // per, cr) * per + j % per))
        o_spec = pl.BlockSpec((None, tm, tn), lambda i, j, k, cr, *_: (j // per, i, j % per))
    if recv is None:
        extras, epilogue = (), lambda acc: (acc,)
    else:
        extras, epilogue = (recv,), lambda acc, other: (acc + other.astype(F32),)
    return _matmul(name, a, b, TN, grid, a_spec, b_spec, [_sds((N_CHIPS, r, c), BF16)], [o_spec], (tm, tn),
                   epilogue, extras, [o_spec] * len(extras), after=after, prefetch=[core])[0]


def _rms_fwd(name, h, g, after=None):
    s, d = h.shape
    tr = _tile(s, 512)

    def body(h_ref, g_ref, o_ref):
        x = h_ref[...]
        r = lax.rsqrt(jnp.mean(x * x, axis=-1, keepdims=True) + EPS)
        o_ref[...] = (x * r * g_ref[...]).astype(o_ref.dtype)

    row = pl.BlockSpec((tr, d), lambda i: (i, 0))
    vec = pl.BlockSpec((1, d), lambda i: (0, 0))
    return _pcall(name, body, [h, g], [row, vec], _sds((s, d), BF16), row, grid=(s // tr,),
                  sem=("parallel",), after=after)


def _accumulate(ref, part, step):
    @pl.when(step == 0)
    def _():
        ref[...] = part

    @pl.when(step > 0)
    def _():
        ref[...] += part


def _rms_bwd(name, dhn, h, g, dres, after=None):
    s, d = h.shape
    tr = _tile(s, 512)
    steps = s // tr

    def body(dhn_ref, h_ref, g_ref, dres_ref, dh_ref, dhb_ref, gp_ref, acc_ref):
        step = pl.program_id(0)

        @pl.when(step == 0)
        def _():
            acc_ref[...] = jnp.zeros_like(acc_ref)

        gain = g_ref[...]

        def chunk(i, carry):
            rows = pl.ds(pl.multiple_of(i * ROW_CHUNK, ROW_CHUNK), ROW_CHUNK)
            x = h_ref[rows, :]
            r = lax.rsqrt(jnp.mean(x * x, axis=-1, keepdims=True) + EPS)
            n = x * r
            dy = dhn_ref[rows, :]
            dn = dy * gain
            dh = dres_ref[rows, :] + r * (dn - n * jnp.mean(dn * n, axis=-1, keepdims=True))
            dh_ref[rows, :] = dh
            dhb_ref[rows, :] = dh.astype(BF16)
            acc_ref[...] += dy * n
            return carry

        lax.fori_loop(0, tr // ROW_CHUNK, chunk, 0, unroll=8)

        @pl.when(step == steps - 1)
        def _():
            gp_ref[...] = jnp.sum(acc_ref[...], axis=0, keepdims=True)

    row = pl.BlockSpec((tr, d), lambda i: (i, 0))
    vec = pl.BlockSpec((1, d), lambda i: (0, 0))
    return _pcall(name, body, [dhn, h, g, dres], [row, row, vec, row],
                  [_sds((s, d), F32), _sds((s, d), BF16), _sds((1, d), F32)], [row, row, vec],
                  grid=(steps,), sem=("arbitrary",), scratch=[pltpu.VMEM((ROW_CHUNK, d), F32)], after=after)


def _loss_head(name, h, g, target, after=None):
    s, d = h.shape
    tr = _tile(s, 512)
    steps = s // tr

    def body(h_ref, g_ref, t_ref, dh_ref, dhb_ref, gp_ref, loss_ref, acc_ref, loss_acc_ref):
        step = pl.program_id(0)

        @pl.when(step == 0)
        def _():
            acc_ref[...] = jnp.zeros_like(acc_ref)
            loss_acc_ref[...] = jnp.zeros_like(loss_acc_ref)

        gg = g_ref[...]

        def chunk(i, carry):
            rows = pl.ds(pl.multiple_of(i * ROW_CHUNK, ROW_CHUNK), ROW_CHUNK)
            x = h_ref[rows, :]
            r = lax.rsqrt(jnp.mean(x * x, axis=-1, keepdims=True) + EPS)
            n = x * r
            e = n * gg - t_ref[rows, :]
            dy = e * (1.0 / d)
            dn = dy * gg
            dh = r * (dn - n * jnp.mean(dn * n, axis=-1, keepdims=True))
            dh_ref[rows, :] = dh
            dhb_ref[rows, :] = dh.astype(BF16)
            acc_ref[...] += dy * n
            loss_acc_ref[...] += jnp.mean(e * e, axis=-1, keepdims=True)
            return carry

        lax.fori_loop(0, tr // ROW_CHUNK, chunk, 0, unroll=8)

        @pl.when(step == steps - 1)
        def _():
            gp_ref[...] = jnp.sum(acc_ref[...], axis=0, keepdims=True)
            loss_ref[...] = 0.5 * jnp.sum(loss_acc_ref[...], axis=0, keepdims=True)

    row = pl.BlockSpec((tr, d), lambda i: (i, 0))
    vec = pl.BlockSpec((1, d), lambda i: (0, 0))
    one = pl.BlockSpec((1, 1), lambda i: (0, 0))
    return _pcall(name, body, [h, g, target], [row, vec, row],
                  [_sds((s, d), F32), _sds((s, d), BF16), _sds((1, d), F32), _sds((1, 1), F32)],
                  [row, row, vec, one], grid=(steps,), sem=("arbitrary",),
                  scratch=[pltpu.VMEM((ROW_CHUNK, d), F32), pltpu.VMEM((ROW_CHUNK, 1), F32)], after=after)


_SQRT_HALF = math.sqrt(0.5)
_INV_SQRT_2PI = 1.0 / math.sqrt(2.0 * math.pi)


def _gelu(x):
    return 0.5 * x * (1.0 + lax.erf(x * _SQRT_HALF))


def _gelu_grad(x):
    return 0.5 * (1.0 + lax.erf(x * _SQRT_HALF)) + x * jnp.exp(-0.5 * x * x) * _INV_SQRT_2PI


def _causal_mask():
    row = lax.broadcasted_iota(jnp.int32, (CHUNK, CHUNK), 0)
    col = lax.broadcasted_iota(jnp.int32, (CHUNK, CHUNK), 1)
    return row >= col


def _row_sum(x):
    return jnp.sum(x, axis=-1, keepdims=True)


def _masked_spatial(ws_ref, grp):
    return jnp.where(_causal_mask(), ws_ref[grp], 0.0).astype(BF16)


def _layernorm_stats(pre_ref, v_scr, w, head):
    total = jnp.zeros((CHUNK, 1), F32)
    for grp in range(A_GROUPS):
        v = _gelu(pre_ref[:, w + grp * head:w + (grp + 1) * head])
        v_scr[:, grp * head:(grp + 1) * head] = v
        total = total + _row_sum(v)
    mu = total * (1.0 / w)
    square = jnp.zeros((CHUNK, 1), F32)
    for grp in range(A_GROUPS):
        xc = v_scr[:, grp * head:(grp + 1) * head] - mu
        square = square + _row_sum(xc * xc)
    return mu, lax.rsqrt(square * (1.0 / w) + EPS)


def _amix_fwd(name, pre, ln_g, ln_b, w_s, b_s_col, after=None):
    s, w2 = pre.shape
    w = w2 // 2
    head = w // A_GROUPS

    def body(pre_ref, g_ref, b_ref, ws_ref, bs_ref, o_ref, v_scr):
        mu, rstd = _layernorm_stats(pre_ref, v_scr, w, head)
        for grp in range(A_GROUPS):
            cols = slice(grp * head, (grp + 1) * head)
            vhat = (v_scr[:, cols] - mu) * rstd
            vn = (vhat * g_ref[:, cols] + b_ref[:, cols]).astype(BF16)
            sg = jnp.dot(_masked_spatial(ws_ref, grp), vn, preferred_element_type=F32) + bs_ref[grp]
            o_ref[:, cols] = (_gelu(pre_ref[:, cols]) * sg).astype(o_ref.dtype)

    vec = pl.BlockSpec((1, w), lambda i: (0, 0))
    return _pcall(
        name, body, [pre, ln_g, ln_b, w_s, b_s_col],
        [pl.BlockSpec((CHUNK, w2), lambda i: (i, 0)), vec, vec,
         pl.BlockSpec((A_GROUPS, CHUNK, CHUNK), lambda i: (0, 0, 0)),
         pl.BlockSpec((A_GROUPS, CHUNK, 1), lambda i: (0, 0, 0))],
        _sds((s, w), BF16), pl.BlockSpec((CHUNK, w), lambda i: (i, 0)),
        grid=(s // CHUNK,), sem=("parallel",), scratch=[pltpu.VMEM((CHUNK, w), F32)], after=after)


def _amix_bwd(name, pre, dgated, ln_g, ln_b, w_s, b_s_col, after=None):
    s, w2 = pre.shape
    w = w2 // 2
    head = w // A_GROUPS

    def body(pre_ref, dg_ref, g_ref, b_ref, ws_ref, bs_ref, dpre_ref, glg_ref, glb_ref, gws_ref, gbs_ref,
             v_scr, dvn_scr):
        @pl.when(pl.program_id(0) == 0)
        def _():
            for ref in (glg_ref, glb_ref, gws_ref, gbs_ref):
                ref[...] = jnp.zeros_like(ref)

        mu, rstd = _layernorm_stats(pre_ref, v_scr, w, head)
        mask = _causal_mask()
        sum_dvhat = jnp.zeros((CHUNK, 1), F32)
        sum_dvhat_vhat = jnp.zeros((CHUNK, 1), F32)
        for grp in range(A_GROUPS):
            cols = slice(grp * head, (grp + 1) * head)
            vhat = (v_scr[:, cols] - mu) * rstd
            gain = g_ref[:, cols]
            vn = (vhat * gain + b_ref[:, cols]).astype(BF16)
            wm = _masked_spatial(ws_ref, grp)
            pre_u = pre_ref[:, cols]
            dgated = dg_ref[:, cols]
            ds = dgated * _gelu(pre_u)
            dsb = ds.astype(BF16)
            sg = jnp.dot(wm, vn, preferred_element_type=F32) + bs_ref[grp]
            dpre_ref[:, cols] = (dgated * sg * _gelu_grad(pre_u)).astype(dpre_ref.dtype)
            gws = lax.dot_general(dsb, vn, NT, preferred_element_type=F32)
            gws_ref[grp] += jnp.where(mask, gws, 0.0)
            gbs_ref[grp] += _row_sum(ds)
            dvn = lax.dot_general(wm, dsb, TN, preferred_element_type=F32)
            dvn_scr[:, cols] = dvn
            glg_ref[:, cols] += jnp.sum(dvn * vhat, axis=0, keepdims=True)
            glb_ref[:, cols] += jnp.sum(dvn, axis=0, keepdims=True)
            dvhat = dvn * gain
            sum_dvhat = sum_dvhat + _row_sum(dvhat)
            sum_dvhat_vhat = sum_dvhat_vhat + _row_sum(dvhat * vhat)
        mean_dvhat = sum_dvhat * (1.0 / w)
        mean_dvhat_vhat = sum_dvhat_vhat * (1.0 / w)
        for grp in range(A_GROUPS):
            cols = slice(grp * head, (grp + 1) * head)
            vhat = (v_scr[:, cols] - mu) * rstd
            dvhat = dvn_scr[:, cols] * g_ref[:, cols]
            dv = rstd * (dvhat - mean_dvhat - vhat * mean_dvhat_vhat)
            pre_v = pre_ref[:, w + grp * head:w + (grp + 1) * head]
            dpre_ref[:, w + grp * head:w + (grp + 1) * head] = (dv * _gelu_grad(pre_v)).astype(dpre_ref.dtype)

    vec = pl.BlockSpec((1, w), lambda i: (0, 0))
    ws_spec = pl.BlockSpec((A_GROUPS, CHUNK, CHUNK), lambda i: (0, 0, 0))
    bs_spec = pl.BlockSpec((A_GROUPS, CHUNK, 1), lambda i: (0, 0, 0))
    return _pcall(
        name, body, [pre, dgated, ln_g, ln_b, w_s, b_s_col],
        [pl.BlockSpec((CHUNK, w2), lambda i: (i, 0)), pl.BlockSpec((CHUNK, w), lambda i: (i, 0)),
         vec, vec, ws_spec, bs_spec],
        [_sds((s, w2), BF16), _sds((1, w), F32), _sds((1, w), F32),
         _sds((A_GROUPS, CHUNK, CHUNK), F32), _sds((A_GROUPS, CHUNK, 1), F32)],
        [pl.BlockSpec((CHUNK, w2), lambda i: (i, 0)), vec, vec, ws_spec, bs_spec],
        grid=(s // CHUNK,), sem=("arbitrary",),
        scratch=[pltpu.VMEM((CHUNK, w), F32), pltpu.VMEM((CHUNK, w), F32)], after=after)


def _shift_rows(x, k, forward):
    n = x.shape[0]
    row = lax.broadcasted_iota(jnp.int32, x.shape, 0)
    if forward:
        return jnp.where(row >= k, pltpu.roll(x, k, 0), 0.0)
    return jnp.where(row < n - k, pltpu.roll(x, n - k, 0), 0.0)


def _window_sum(x, window, forward):
    k = 1
    while k < window:
        x = x + _shift_rows(x, k, forward)
        k *= 2
    return x


def _pool(name, v, backward, after=None):
    s, w = v.shape
    head = w // B_GROUPS
    lane = _tile(head, 128)

    def body(v_ref, o_ref):
        grp = pl.program_id(0)
        t = lax.broadcasted_iota(jnp.int32, (s, lane), 0)
        for idx, window in enumerate(B_WINDOWS):
            @pl.when(grp == idx)
            def _():
                inv_count = 1.0 / jnp.minimum(t + 1, window).astype(F32)
                for strip in range(head // lane):
                    cols = slice(strip * lane, (strip + 1) * lane)
                    x = v_ref[:, cols]
                    if backward:
                        out = _window_sum(x * inv_count, window, False) - x
                    else:
                        out = _window_sum(x, window, True) * inv_count - x
                    o_ref[:, cols] = out.astype(o_ref.dtype)

    spec = pl.BlockSpec((s, head), lambda g: (0, g))
    return _pcall(name, body, [v], [spec], _sds((s, w), BF16), spec, grid=(B_GROUPS,),
                  sem=("parallel",), after=after)


def _colsum(name, a, after=None):
    s, d = a.shape
    tr = _tile(s, 256)

    def body(a_ref, o_ref):
        _accumulate(o_ref, jnp.sum(a_ref[...], axis=0, keepdims=True), pl.program_id(0))

    return _pcall(name, body, [a], [pl.BlockSpec((tr, d), lambda i: (i, 0))], _sds((1, d), F32),
                  pl.BlockSpec((1, d), lambda i: (0, 0)), grid=(s // tr,), sem=("arbitrary",), after=after)


def _adamw(w, g, m, v):
    m = ADAM_B1 * m + (1.0 - ADAM_B1) * g
    v = ADAM_B2 * v + (1.0 - ADAM_B2) * (g * g)
    m_hat = m / (1.0 - ADAM_B1 ** ADAM_STEP)
    v_hat = v / (1.0 - ADAM_B2 ** ADAM_STEP)
    delta = -ADAM_LR * (m_hat / (jnp.sqrt(v_hat) + ADAM_EPS) + ADAM_WD * w)
    return delta, m, v


def _adam_rows(name, g, w, m, v, after=None):
    r, c = g.shape
    tr = _tile(r, 256)

    def body(g_ref, w_ref, m_ref, v_ref, d_ref, nm_ref, nv_ref):
        d_ref[...], nm_ref[...], nv_ref[...] = _adamw(w_ref[...], g_ref[...], m_ref[...], v_ref[...])

    spec = pl.BlockSpec((tr, c), lambda i: (i, 0))
    return _pcall(name, body, [g, w, m, v], [spec] * 4, [_sds((r, c), F32)] * 3, [spec] * 3,
                  grid=(r // tr,), sem=("parallel",), after=after)


def _position():
    return lax.axis_index("x"), lax.axis_index("y"), lax.axis_index("c")


def _other_chips(x, y):
    return [(1 - x, y), (x, 1 - y), (1 - x, 1 - y)]


def _slot(px, py, pc):
    return 4 * px + 2 * py + pc


def _hbm(a):
    return pltpu.with_memory_space_constraint(a, pltpu.HBM)


def _hop1_copies(srcs, lands, send_sems, recv_sems):
    x, y, c = _position()
    peers = [(x, y, 1 - c), (1 - x, y, c), (x, 1 - y, c)]
    mine = _slot(x, y, c)
    return [[pltpu.make_async_remote_copy(
        src_ref=srcs[t], dst_ref=lands[t].at[mine], send_sem=send_sems[t].at[k], recv_sem=recv_sems[t].at[k],
        device_id=peer, device_id_type=MESH) for k, peer in enumerate(peers)] for t in range(len(srcs))]


def _hop2_copies(lands, send_sems, recv_sems):
    x, y, c = _position()
    routes = [(_slot(1 - x, y, c), (x, 1 - y, c)), (_slot(x, 1 - y, c), (1 - x, y, c))]
    out = []
    for t in range(len(lands)):
        rows = lands[t].shape[1]
        halves = [(0, rows // 2), (rows // 2, rows - rows // 2)]
        per_tensor = []
        for h, ((slot, peer), (start, size)) in enumerate(zip(routes, halves)):
            if size:
                block = lands[t].at[slot, pl.ds(start, size)]
                per_tensor.append(pltpu.make_async_remote_copy(
                    src_ref=block, dst_ref=block, send_sem=send_sems[t].at[h], recv_sem=recv_sems[t].at[h],
                    device_id=peer, device_id_type=MESH))
        for j, (slot, _) in enumerate(routes):
            block = lands[t].at[slot]
            per_tensor.append(pltpu.make_async_remote_copy(
                src_ref=block, dst_ref=block, send_sem=send_sems[t].at[2 + j], recv_sem=recv_sems[t].at[2 + j],
                device_id=(x, y, 1 - c), device_id_type=MESH))
        out.append(per_tensor)
    return out


def _split_start(name, srcs, lands, copies, n_sems, after=None):
    n = len(srcs)
    order = [] if after is None else [after]
    n_in = 2 * n + len(order)

    def body(*refs):
        for per_tensor in copies(refs[:n], refs[n:2 * n], refs[n_in:n_in + n], refs[n_in + n:n_in + 2 * n]):
            for cp in per_tensor:
                cp.start()
        refs[-1][...] = jnp.zeros_like(refs[-1])

    out_shape = ([pltpu.SemaphoreType.DMA((n_sems,)) for _ in range(2 * n)]
                 + [pltpu.HBM(a.shape, a.dtype) for a in list(srcs) + list(lands)]
                 + [_sds((8, 128), F32)])
    out = pl.pallas_call(
        body, name=name, out_shape=out_shape, in_specs=[_HBM] * (2 * n) + [_ANY] * len(order),
        out_specs=[_SEM] * (2 * n) + [_HBM] * (2 * n) + [pl.BlockSpec(memory_space=pltpu.VMEM)],
        input_output_aliases={i: 2 * n + i for i in range(2 * n)},
        compiler_params=pltpu.CompilerParams(has_side_effects=_EFFECT),
    )(*[_hbm(a) for a in srcs], *[_hbm(a) for a in lands], *order)
    return [(out[t], out[n + t], out[2 * n + t], out[3 * n + t]) for t in range(n)], out[-1]


def _split_wait(name, started, copies, after):
    n = len(started)

    def body(*refs):
        for per_tensor in copies(refs[:n], refs[n:2 * n], refs[2 * n:3 * n], refs[3 * n:4 * n]):
            for cp in per_tensor:
                cp.wait_send()
                cp.wait_recv()

    srcs = [e[2] for e in started]
    lands = [e[3] for e in started]
    out = pl.pallas_call(
        body, name=name, out_shape=[pltpu.HBM(a.shape, a.dtype) for a in srcs + lands],
        in_specs=[_HBM] * (2 * n) + [_SEM] * (2 * n) + [_ANY], out_specs=[_HBM] * (2 * n),
        input_output_aliases={i: i for i in range(2 * n)},
        compiler_params=pltpu.CompilerParams(has_side_effects=_EFFECT),
    )(*srcs, *lands, *[e[0] for e in started], *[e[1] for e in started], after)
    return out[:n], out[n:]


def _gather_step(name, arrived, fresh, after=None):
    n, m = len(arrived), len(fresh)
    order = [] if after is None else [after]
    fresh_lands = [lax.empty((N_DEV,) + s.shape, s.dtype) for s in fresh]
    buffers = [e[2] for e in arrived] + [e[3] for e in arrived] + list(fresh) + fresh_lands
    old_sems = [e[0] for e in arrived] + [e[1] for e in arrived]
    n_buf, n_old = len(buffers), len(old_sems)
    first_new = n_buf + n_old + len(order)

    def body(*refs):
        bufs, old = refs[:n_buf], refs[n_buf:n_buf + n_old]
        new = refs[first_new:first_new + 2 * n + 2 * m]
        for per_tensor in _hop1_copies(bufs[:n], bufs[n:2 * n], old[:n], old[n:]):
            for cp in per_tensor:
                cp.wait_send()
                cp.wait_recv()
        second = _hop2_copies(bufs[n:2 * n], new[:n], new[n:2 * n])
        first = _hop1_copies(bufs[2 * n:2 * n + m], bufs[2 * n + m:], new[2 * n:2 * n + m], new[2 * n + m:])
        for per_tensor in second + first:
            for cp in per_tensor:
                cp.start()
        refs[-1][...] = jnp.zeros_like(refs[-1])

    n_new = 2 * n + 2 * m
    out_shape = ([pltpu.SemaphoreType.DMA((4,)) for _ in range(2 * n)]
                 + [pltpu.SemaphoreType.DMA((3,)) for _ in range(2 * m)]
                 + [pltpu.HBM(a.shape, a.dtype) for a in buffers] + [_sds((8, 128), F32)])
    out = pl.pallas_call(
        body, name=name, out_shape=out_shape,
        in_specs=[_HBM] * n_buf + [_SEM] * n_old + [_ANY] * len(order),
        out_specs=[_SEM] * n_new + [_HBM] * n_buf + [pl.BlockSpec(memory_space=pltpu.VMEM)],
        input_output_aliases={i: n_new + i for i in range(n_buf)},
        compiler_params=pltpu.CompilerParams(has_side_effects=_EFFECT),
    )(*[_hbm(a) for a in buffers], *old_sems, *order)
    sems, bufs = out[:n_new], out[n_new:n_new + n_buf]
    second = [(sems[t], sems[n + t], bufs[t], bufs[n + t]) for t in range(n)]
    first = [(sems[2 * n + t], sems[2 * n + m + t], bufs[2 * n + t], bufs[2 * n + m + t]) for t in range(m)]
    return second, first, out[-1]


def _gather_wait(name, second, after):
    return _split_wait(name, second, lambda srcs, lands, send, recv: _hop2_copies(lands, send, recv), after)


def _sibling_copies(srcs, lands, send_sems, recv_sems):
    x, y, c = _position()
    return [[pltpu.make_async_remote_copy(
        src_ref=srcs[t], dst_ref=lands[t], send_sem=send_sems[t].at[0], recv_sem=recv_sems[t].at[0],
        device_id=(x, y, 1 - c), device_id_type=MESH)] for t in range(len(srcs))]


def _sibling_and_scatter_start(name, arrays, partials):
    k = len(arrays)

    def copies(srcs, lands, send_sems, recv_sems):
        return (_sibling_copies(srcs[:k], lands[:k], send_sems[:k], recv_sems[:k])
                + _scatter_copies(srcs[k:], lands[k:], send_sems[k:], recv_sems[k:]))

    lands = ([lax.empty(a.shape, a.dtype) for a in arrays]
             + [lax.empty((N_CHIPS - 1,) + p.shape[1:], p.dtype) for p in partials])
    return _split_start(name, list(arrays) + list(partials), lands, copies, 3)


def _sibling_wait(name, started, after):
    return _split_wait(name, started, _sibling_copies, after)[1]


def _small_copies(srcs, lands, send_sems, recv_sems):
    x, y, c = _position()
    mine = _slot(x, y, c)
    peers = [(x ^ ((k >> 2) & 1), y ^ ((k >> 1) & 1), c ^ (k & 1)) for k in range(1, N_DEV)]
    return [[pltpu.make_async_remote_copy(
        src_ref=srcs[t], dst_ref=lands[t].at[mine], send_sem=send_sems[t].at[k], recv_sem=recv_sems[t].at[k],
        device_id=peer, device_id_type=MESH) for k, peer in enumerate(peers)] for t in range(len(srcs))]


def _gather_finish(name, shards, lands, after):
    n = len(shards)

    def body(*refs):
        srcs, lands_in, outs = refs[:n], refs[n:2 * n], refs[2 * n:3 * n]
        send_sems, recv_sems, local_sems = refs[3 * n:]
        x, y, c = _position()
        local = [pltpu.make_async_copy(srcs[t], outs[t].at[_slot(x, y, c)], local_sems.at[t]) for t in range(n)]

        def diagonal(t, core):
            block = outs[t].at[_slot(1 - x, 1 - y, core)]
            return pltpu.make_async_remote_copy(
                src_ref=block, dst_ref=block, send_sem=send_sems.at[t], recv_sem=recv_sems.at[t],
                device_id=(x, y, 1 - c), device_id_type=MESH)

        for cp in local:
            cp.start()
        for t in range(n):
            diagonal(t, c).start()
        for t in range(n):
            diagonal(t, c).wait_send()
            diagonal(t, 1 - c).wait_recv()
        for cp in local:
            cp.wait()

    return _pcall(name, body, [*shards, *lands], [_ANY] * (2 * n),
                  [_sds(l.shape, l.dtype) for l in lands], [_ANY] * n,
                  scratch=[pltpu.SemaphoreType.DMA((n,)), pltpu.SemaphoreType.DMA((n,)),
                           pltpu.SemaphoreType.DMA((n,))],
                  after=after, aliases={n + t: t for t in range(n)})


def _exchange_sibling(name, fulls, after):
    n = len(fulls)

    def body(*refs):
        src = refs[:n]
        out = refs[n:2 * n]
        send_sems, recv_sems = refs[2 * n:]
        x, y, c = _position()
        copies = [pltpu.make_async_remote_copy(
            src_ref=src[t].at[:, 1 - c], dst_ref=out[t], send_sem=send_sems.at[t], recv_sem=recv_sems.at[t],
            device_id=(x, y, 1 - c), device_id_type=MESH) for t in range(n)]
        for cp in copies:
            cp.start()
        for cp in copies:
            cp.wait()

    return _pcall(name, body, fulls, [_ANY] * n, [_sds((N_CHIPS,) + f.shape[2:], f.dtype) for f in fulls],
                  [_ANY] * n, scratch=[pltpu.SemaphoreType.DMA((n,)), pltpu.SemaphoreType.DMA((n,))],
                  after=after)


def _add_sibling(name, full, recv, core, after):
    _, _, r, c = full.shape
    tr = _tile(r, max(8, (256 * 1024) // c))

    def body(core_ref, f_ref, r_ref, o_ref):
        o_ref[...] = (f_ref[...].astype(F32) + r_ref[...].astype(F32)).astype(o_ref.dtype)

    return _pcall(
        name, body, [full, recv],
        [pl.BlockSpec((None, None, tr, c), lambda p, i, core_ref: (p, core_ref[0], i, 0)),
         pl.BlockSpec((None, tr, c), lambda p, i, core_ref: (p, i, 0))],
        _sds((N_CHIPS, r, c), BF16), pl.BlockSpec((None, tr, c), lambda p, i, core_ref: (p, i, 0)),
        grid=(N_CHIPS, r // tr), sem=("parallel", "parallel"), prefetch=[core], after=after)


def _scatter_copies(srcs, lands, send_sems, recv_sems):
    x, y, c = _position()
    return [[pltpu.make_async_remote_copy(
        src_ref=srcs[t].at[2 * px + py], dst_ref=lands[t].at[j],
        send_sem=send_sems[t].at[j], recv_sem=recv_sems[t].at[j],
        device_id=(px, py, c), device_id_type=MESH) for j, (px, py) in enumerate(_other_chips(x, y))]
        for t in range(len(srcs))]


def _scatter_wait(name, started, after):
    return _split_wait(name, started, _scatter_copies, after)


class _Job:
    def __init__(self, **fields):
        self.__dict__.update(fields)


def _reduce_adam_job(partial, recv, chip, w, m, v, layer, carried):
    n_layers, r, c = w.shape
    tr = _tile(r, max(8, (256 * 1024) // c))

    def body(p_ref, r_ref, w_ref, m_ref, v_ref, *rest):
        g_ref, d_ref, nm_ref, nv_ref = rest[-4:]
        g = p_ref[...].astype(F32)
        for j in range(N_CHIPS - 1):
            g = g + r_ref[j].astype(F32)
        g_ref[...] = g
        d_ref[...], nm_ref[...], nv_ref[...] = _adamw(w_ref[...], g, m_ref[...], v_ref[...])

    layered = ((None, tr, c), lambda blk, chip_ref: (layer, blk, 0))
    in_specs = [((None, tr, c), lambda blk, chip_ref: (chip_ref[0], blk, 0)),
                ((N_CHIPS - 1, tr, c), lambda blk, chip_ref: (0, blk, 0)), layered, layered, layered]
    operands = [partial, recv, w, m, v]
    aliases = {}
    if carried is not None:
        operands += list(carried)
        in_specs += [None] * 4
        aliases = {5 + o: o for o in range(4)}
    return _Job(operands=operands, in_specs=in_specs, out_shape=[_sds((n_layers, r, c), F32)] * 4,
                out_specs=[layered] * 4, body=body, aliases=aliases, prefetch=chip, n_blocks=r // tr)


def _run_job(name, job, after):
    def spec(entry):
        if entry is None:
            return _ANY
        shape, index = entry
        return pl.BlockSpec(shape, lambda blk, pre, index=index: index(blk, pre))

    def body(pre_ref, *refs):
        job.body(*refs)

    return _pcall(name, body, job.operands, [spec(e) for e in job.in_specs], job.out_shape,
                  [spec(e) for e in job.out_specs], grid=(job.n_blocks,), sem=("parallel",),
                  prefetch=[job.prefetch], after=after,
                  aliases={1 + i: o for i, o in job.aliases.items()})


def _small_sum(name, gathered, own, device, after=None):
    r, lanes = own.shape

    def body(dev_ref, g_ref, own_ref, out_ref):
        dev = dev_ref[0]
        mine = own_ref[...]
        total = jnp.where(dev == 0, mine, g_ref[0])
        for d in range(1, N_DEV):
            total = total + jnp.where(dev == d, mine, g_ref[d])
        out_ref[...] = total

    return _pcall(name, body, [gathered, own],
                  [pl.BlockSpec((N_DEV, r, lanes), lambda i, dev_ref: (0, 0, 0)),
                   pl.BlockSpec((r, lanes), lambda i, dev_ref: (0, 0))],
                  _sds((r, lanes), F32), pl.BlockSpec((r, lanes), lambda i, dev_ref: (0, 0)),
                  grid=(1,), sem=("arbitrary",), prefetch=[device], after=after)


def _pack(arrays):
    return jnp.concatenate([a.reshape(-1, 128) for a in arrays], axis=0)


def _unpack(packed, shapes):
    out, row = [], 0
    for shape in shapes:
        rows = math.prod(shape) // 128
        out.append(packed[row:row + rows].reshape(shape))
        row += rows
    return out


class _Order:
    def __init__(self):
        self.last = None

    def __call__(self, fn, *args, **kwargs):
        out = fn(*args, after=self.last, **kwargs)
        self.last = out[0] if isinstance(out, (list, tuple)) else out
        return out


def kernel(x, a_w_in, a_ln_g, a_ln_b, a_w_s, a_b_s, a_w_out, b_w_in, b_w_grp, b_scale, b_w_out, norm_mix, norm_mlp, mlp_w1, mlp_w2, final_norm, loss_target, m_a_w_in, m_a_ln_g, m_a_ln_b, m_a_w_s, m_a_b_s, m_a_w_out, m_b_w_in, m_b_w_grp, m_b_scale, m_b_w_out, m_norm_mix, m_norm_mlp, m_mlp_w1, m_mlp_w2, m_final_norm, v_a_w_in, v_a_ln_g, v_a_ln_b, v_a_w_s, v_a_b_s, v_a_w_out, v_b_w_in, v_b_w_grp, v_b_scale, v_b_w_out, v_norm_mix, v_norm_mlp, v_mlp_w1, v_mlp_w2, v_final_norm):
    s, d = x.shape[1], x.shape[2]
    depth = mlp_w1.shape[0]
    a_slab = a_w_in.shape[2]
    ff_slab = mlp_w1.shape[2]
    ff_rows = mlp_w2.shape[1]
    bh = b_w_grp.shape[3]
    my_x, my_y, my_c = _position()
    core = jnp.reshape(my_c, (1,)).astype(jnp.int32)
    chip = jnp.reshape(2 * my_x + my_y, (1,)).astype(jnp.int32)
    device = _slot(my_x, my_y, my_c)
    run = _Order()

    w1_b, w2_b = mlp_w1.astype(BF16), mlp_w2.astype(BF16)
    shards = [a_w_in[0].astype(BF16), a_w_out[0].astype(BF16), b_scale,
              w1_b[0], w2_b[0],
              b_w_in[0].astype(BF16), b_w_grp[0].astype(BF16), b_w_out[0].astype(BF16),
              w1_b[1], w2_b[1]]
    groups = [[0], [1, 2], [3], [4], [5, 6, 7], [8], [9]]
    start_with = {1: [2, 3], 2: [4], 3: [5], 4: [6]}
    hop1, hop2 = {}, {}
    _, hop1[0], token = _gather_step("weights_group0_hop1", [], [shards[t] for t in groups[0]])
    _, hop1[1], token = _gather_step("weights_group1_hop1", [], [shards[t] for t in groups[1]], token)
    run.last = token

    def advance(g):
        if g not in hop1:
            return
        ahead = start_with.get(g, [])
        fresh = [shards[t] for a in ahead for t in groups[a]]
        hop2[g], started, tok = _gather_step(f"weights_group{g}_hop2", hop1.pop(g), fresh, run.last)
        for a in ahead:
            hop1[a], started = started[:len(groups[a])], started[len(groups[a]):]
        run.last = tok

    def gathered(g):
        advance(g)
        if g == 0:
            advance(1)
        srcs, lands = _gather_wait(f"weights_group{g}_wait", hop2.pop(g), run.last)
        run.last = srcs[0]
        return run(_gather_finish, f"weights_group{g}_finish", srcs, lands)

    h0 = x[0]
    target = loss_target[0]
    ln_g, ln_b = a_ln_g, a_ln_b
    w_s = a_w_s[0]
    b_s_col = a_b_s[0][:, :, None]
    nmix = [norm_mix[l][None, :] for l in range(depth)]
    nmlp = [norm_mlp[l][None, :] for l in range(depth)]

    def mlp_forward(l, h, up_group):
        hn = run(_rms_fwd, f"mlp{l}_norm", h, nmlp[l])
        (w1,) = gathered(up_group)
        advance(up_group + 1)
        act, act_sq = run(_mm_nn, f"mlp{l}_up", hn, w1,
                          lambda acc: (jnp.maximum(acc, 0.0), jnp.square(jnp.maximum(acc, 0.0))),
                          (BF16, BF16), slab=True)
        (w2,) = gathered(up_group + 1)
        advance(up_group + 2)
        w2 = w2.reshape(-1, d)
        (h_out,) = run(_mm_nn, f"mlp{l}_down", act_sq, w2, lambda acc, res: (acc + res,), (F32,),
                       extras=(h,), extra_kinds=("tile",))
        return h_out, (h, hn, act, act_sq, w1, w2)

    scattered = []

    pending = []

    def scatter_partials(name, partials, specs):
        pending.append((name, partials, specs))

    def start_exchanges(name, to_sibling):
        partials = [p for _, group, _ in pending for p in group]
        in_flight, tok = _sibling_and_scatter_start(name, to_sibling, partials)
        run.last = tok
        first = len(to_sibling)
        for group_name, group, specs in pending:
            scattered.append((group_name, in_flight[first:first + len(group)], specs))
            first += len(group)
        pending.clear()
        return in_flight[:len(to_sibling)]

    weights = {"a_w_in": (a_w_in, m_a_w_in, v_a_w_in), "a_w_out": (a_w_out, m_a_w_out, v_a_w_out),
               "b_w_in": (b_w_in, m_b_w_in, v_b_w_in), "b_w_grp": (b_w_grp, m_b_w_grp, v_b_w_grp),
               "b_w_out": (b_w_out, m_b_w_out, v_b_w_out), "mlp_w1": (mlp_w1, m_mlp_w1, v_mlp_w1),
               "mlp_w2": (mlp_w2, m_mlp_w2, v_mlp_w2)}
    results = {}

    def finish_group(name, in_flight, specs):
        partials, lands = _scatter_wait(name + "_scatter_wait", in_flight, run.last)
        run.last = lands[0]
        for t, (wname, layer) in enumerate(specs):
            w, m, v = weights[wname]
            shape = (w.shape[0],) + partials[t].shape[1:]
            job = _reduce_adam_job(partials[t], lands[t], chip, w.reshape(shape), m.reshape(shape),
                                   v.reshape(shape), layer, results.get(wname))
            results[wname] = run(_run_job, f"{name}_reduce_adam_{t}", job)

    def weight_grad(name, a, b, by_rows, block, between, tm=1024, tn=1024):
        other = run(_mm_tn_half, name + "_other", a, b, core, False, by_rows, block, tm=tm, tn=tn)
        sent = start_exchanges(name + "_sibling_start", [other])
        middle = between()
        (recv,) = _sibling_wait(name + "_sibling_wait", sent, run.last)
        run.last = recv
        return run(_mm_tn_half, name + "_own", a, b, core, True, by_rows, block, recv=recv, tm=tm, tn=tn), middle

    def mlp_backward(l, saved, dh, dhb):
        h, hn, act, act_sq, w1, w2 = saved
        part_w2, (dpre,) = weight_grad(
            f"mlp{l}_down_dw", act_sq, dhb, True, ff_rows,
            lambda: run(_mm_nt, f"mlp{l}_down_dx", dhb, w2, lambda acc, a: (2.0 * a.astype(F32) * acc,),
                        (BF16,), extras=(act,), extra_kinds=("tile",)))
        scatter_partials(f"mlp{l}_down_grads", [part_w2], [("mlp_w2", l)])
        part_w1, (dhn,) = weight_grad(
            f"mlp{l}_up_dw", hn, dpre, False, ff_slab,
            lambda: run(_mm_nt, f"mlp{l}_up_dx", dpre, w1, lambda acc: (acc,), (F32,), slab=True))
        scatter_partials(f"mlp{l}_up_grads", [part_w1], [("mlp_w1", l)])
        dh, dhb, g_norm = run(_rms_bwd, f"mlp{l}_norm_bwd", dhn, h, nmlp[l], dh)
        return dh, dhb, g_norm

    hn0 = run(_rms_fwd, "mix0_norm", h0, nmix[0])
    (wa_in,) = gathered(0)
    (pre,) = run(_mm_nn, "mixa_in", hn0, wa_in, lambda acc: (acc,), (F32,), slab=True)
    wa_out, scale = gathered(1)
    wa_out, scale = wa_out.reshape(d, d), scale.reshape(1, d)
    gated = run(_amix_fwd, "mixa_gate", pre, ln_g, ln_b, w_s, b_s_col)
    advance(2)
    (h1,) = run(_mm_nn, "mixa_out", gated, wa_out, lambda acc, res: (acc + res,), (F32,),
                extras=(h0,), extra_kinds=("tile",))
    h2, saved_mlp0 = mlp_forward(0, h1, 2)
    hn2 = run(_rms_fwd, "mix1_norm", h2, nmix[1])
    wb_in, wb_grp, wb_out = gathered(4)
    advance(5)
    wb_in, wb_out = wb_in.reshape(d, d), wb_out.reshape(d, d)
    wb_grp = jnp.transpose(wb_grp, (1, 0, 2, 3)).reshape(B_GROUPS, bh, bh)
    (vb,) = run(_mm_nn, "mixb_in", hn2, wb_in, lambda acc: (acc,), (F32,))
    pooled = run(_pool, "mixb_pool", vb, backward=False)
    tm = _tile(s, 1024)
    grp_tile = pl.BlockSpec((tm, bh), lambda i, j, k: (i, j))
    grp_weight = pl.BlockSpec((None, bh, bh), lambda i, j, k: (j, 0, 0))
    mixed, mixed_scaled = run(
        _matmul, "mixb_grp", pooled, wb_grp, NN, (s // tm, B_GROUPS, 1), grp_tile, grp_weight,
        [_sds((s, d), BF16), _sds((s, d), BF16)], [grp_tile] * 2,
        (tm, bh), lambda acc, sc: (acc, acc * sc), (scale,), [pl.BlockSpec((1, bh), lambda i, j, k: (0, j))])
    (h3,) = run(_mm_nn, "mixb_out", mixed_scaled, wb_out, lambda acc, res: (acc + res,), (F32,),
                extras=(h2,), extra_kinds=("tile",))
    h4, saved_mlp1 = mlp_forward(1, h3, 5)
    dh, dhb, g_final, loss_part = run(_loss_head, "loss_head", h4, final_norm[None, :], target)

    dh, dhb, g_nmlp1 = mlp_backward(1, saved_mlp1, dh, dhb)
    tks = _tile(s, 1024)
    grp_rows = pl.BlockSpec((tks, bh), lambda i, j, k: (k, j))

    def mixb_middle():
        dms_scaled, dms_mixed = run(
            _mm_nt, "mixb_out_dx", dhb, wb_out,
            lambda acc, sc, mx: (acc * sc, acc * mx.astype(F32)), (BF16, F32),
            extras=(scale, mixed), extra_kinds=("row", "tile"))
        g_scale = run(_colsum, "mixb_scale_dw", dms_mixed)
        (g_wb_grp,) = run(
            _matmul, "mixb_grp_dw", pooled, dms_scaled, TN, (1, B_GROUPS, s // tks), grp_rows, grp_rows,
            [_sds((B_GROUPS, bh, bh), BF16)], [grp_weight], (bh, bh), lambda acc: (acc,))
        (dpooled,) = run(
            _matmul, "mixb_grp_dx", dms_scaled, wb_grp, NT, (s // tm, B_GROUPS, 1), grp_tile, grp_weight,
            [_sds((s, d), F32)], [grp_tile], (tm, bh), lambda acc: (acc,))
        return g_scale, g_wb_grp, run(_pool, "mixb_pool_bwd", dpooled, backward=True)

    part_wb_out, (g_scale, g_wb_grp, dvb) = weight_grad("mixb_out_dw", mixed_scaled, dhb, True, d // N_DEV,
                                                        mixb_middle, tn=d)
    part_wb_in, (dhn2,) = weight_grad(
        "mixb_in_dw", hn2, dvb, True, d // N_DEV,
        lambda: run(_mm_nt, "mixb_in_dx", dvb, wb_in, lambda acc: (acc,), (F32,)), tn=d)
    grp_full = jnp.transpose(g_wb_grp.reshape(B_GROUPS, N_DEV, bh // N_DEV, bh), (1, 0, 2, 3))
    grp_full = grp_full.reshape(N_CHIPS, 2, B_GROUPS * bh // N_DEV, bh)
    (grp_sibling,) = run(_exchange_sibling, "mixb_grp_dw_to_sibling", [grp_full])
    part_wb_grp = run(_add_sibling, "mixb_grp_dw_add_sibling", grp_full, grp_sibling, core)
    scatter_partials("mixb_grads", [part_wb_out, part_wb_grp, part_wb_in],
                     [("b_w_out", 0), ("b_w_grp", 0), ("b_w_in", 0)])
    dh, dhb, g_nmix1 = run(_rms_bwd, "mix1_norm_bwd", dhn2, h2, nmix[1], dh)
    dh, dhb, g_nmlp0 = mlp_backward(0, saved_mlp0, dh, dhb)
    def mixa_middle():
        (dgated,) = run(_mm_nt, "mixa_out_dx", dhb, wa_out, lambda acc: (acc,), (F32,))
        return run(_amix_bwd, "mixa_gate_bwd", pre, dgated, ln_g, ln_b, w_s, b_s_col)

    part_wa_out, (dpre, g_ln_g, g_ln_b, g_w_s, g_b_s) = weight_grad("mixa_out_dw", gated, dhb, True, d // N_DEV,
                                                                     mixa_middle, tn=d)
    part_wa_in, (dhn0,) = weight_grad(
        "mixa_in_dw", hn0, dpre, False, a_slab,
        lambda: run(_mm_nt, "mixa_in_dx", dpre, wa_in, lambda acc: (acc,), (F32,), slab=True), tm=d)
    scatter_partials("mixa_grads", [part_wa_in, part_wa_out], [("a_w_in", 0), ("a_w_out", 0)])
    start_exchanges("mixa_grads_scatter_start", [])
    grad_x, _, g_nmix0 = run(_rms_bwd, "mix0_norm_bwd", dhn0, h0, nmix[0], dh)

    g_norm_mix = jnp.concatenate([g_nmix0, g_nmix1], axis=0)
    g_norm_mlp = jnp.concatenate([g_nmlp0, g_nmlp1], axis=0)
    loss_row = jnp.pad(loss_part, ((0, 0), (0, 127)))
    small_parts = [g_ln_g, g_ln_b, g_w_s, g_b_s, g_norm_mix, g_norm_mlp, g_final, g_scale, loss_row]
    packed = _pack(small_parts)
    small_sent, tok = _split_start("small_grads_start", [packed], [lax.empty((N_DEV,) + packed.shape, F32)],
                                   _small_copies, N_DEV - 1)
    run.last = tok

    for group in scattered[:-1]:
        finish_group(*group)
    own_packed, small_gathered = _split_wait("small_grads_wait", small_sent, _small_copies, run.last)
    run.last = small_gathered[0]
    small_sum = run(_small_sum, "small_grads_sum", small_gathered[0], own_packed[0],
                    jnp.reshape(device, (1,)).astype(jnp.int32))
    sg = _unpack(small_sum, [a_ln_g.shape, a_ln_b.shape, a_w_s.shape, a_b_s.shape, norm_mix.shape,
                             norm_mlp.shape, final_norm.shape, (1, d), (1, 128)])
    loss = sg.pop()[0, 0]
    shard = b_scale.shape[1]
    sg[7] = lax.dynamic_slice(sg[7], (0, device * shard), (1, shard))
    small_w = [a_ln_g, a_ln_b, a_w_s, a_b_s, norm_mix, norm_mlp, final_norm, b_scale]
    small_m = [m_a_ln_g, m_a_ln_b, m_a_w_s, m_a_b_s, m_norm_mix, m_norm_mlp, m_final_norm, m_b_scale]
    small_v = [v_a_ln_g, v_a_ln_b, v_a_w_s, v_a_b_s, v_norm_mix, v_norm_mlp, v_final_norm, v_b_scale]
    small_out = run(_adam_rows, "small_adam", _pack(sg), _pack(small_w), _pack(small_m), _pack(small_v))
    shapes = [w.shape for w in small_w]
    small_res = [sg] + [_unpack(o, shapes) for o in small_out]

    finish_group(*scattered[-1])
    big = {wname: [o.reshape(weights[wname][0].shape) for o in outs] for wname, outs in results.items()}

    def leaf(o):
        return (big["a_w_in"][o], small_res[o][0], small_res[o][1], small_res[o][2], small_res[o][3],
                big["a_w_out"][o], big["b_w_in"][o], big["b_w_grp"][o], small_res[o][7], big["b_w_out"][o],
                small_res[o][4], small_res[o][5], big["mlp_w1"][o], big["mlp_w2"][o], small_res[o][6])

    return (loss, grad_x[None], *leaf(0), *leaf(1), *leaf(2), *leaf(3))
```

```python
import math

import jax
import jax.numpy as jnp
from jax import lax
from jax.experimental import pallas as pl
from jax.experimental.pallas import tpu as pltpu

F32 = jnp.float32
BF16 = jnp.bfloat16
MESH = pl.DeviceIdType.MESH

N_DEV = 8
N_CHIPS = 4
CHUNK = 128
A_GROUPS = 8
B_WINDOWS = (2, 4, 8, 16)
B_GROUPS = len(B_WINDOWS)
EPS = 1e-6
ADAM_LR = 0.001
ADAM_B1 = 0.9
ADAM_B2 = 0.999
ADAM_EPS = 1e-08
ADAM_WD = 0.01
ADAM_STEP = 10

VMEM_LIMIT = 48 * 1024 * 1024
ROW_CHUNK = 16

NN = (((1,), (0,)), ((), ()))
NT = (((1,), (1,)), ((), ()))
TN = (((0,), (0,)), ((), ()))

_ANY = pl.BlockSpec(memory_space=pl.ANY)
_HBM = pl.BlockSpec(memory_space=pltpu.HBM)
_SEM = pl.BlockSpec(memory_space=pltpu.SEMAPHORE)
_EFFECT = pltpu.SideEffectType.DATAFLOW_SIDE_EFFECTING


def _tile(n, pref):
    return pref if n % pref == 0 else n


def _sds(shape, dtype):
    return jax.ShapeDtypeStruct(shape, dtype)


def _pcall(name, body, operands, in_specs, out_shape, out_specs, *, grid=None, sem=None, scratch=(),
           prefetch=(), after=None, aliases=None):
    after = [] if after is None else [after]
    n_lead = len(prefetch) + len(operands)
    n_after = len(after)

    def wrapped(*refs):
        body(*refs[:n_lead], *refs[n_lead + n_after:])

    in_specs = list(in_specs) + [_ANY] * n_after
    params = pltpu.CompilerParams(vmem_limit_bytes=VMEM_LIMIT) if sem is None else \
        pltpu.CompilerParams(dimension_semantics=sem, vmem_limit_bytes=VMEM_LIMIT)
    kwargs = dict(out_shape=out_shape, scratch_shapes=list(scratch), compiler_params=params, name=name,
                  input_output_aliases=aliases or {})
    if prefetch:
        kwargs["grid_spec"] = pltpu.PrefetchScalarGridSpec(
            num_scalar_prefetch=len(prefetch), grid=grid, in_specs=in_specs, out_specs=out_specs,
            scratch_shapes=list(scratch))
        kwargs.pop("scratch_shapes")
    else:
        kwargs.update(in_specs=in_specs, out_specs=out_specs)
        if grid is not None:
            kwargs["grid"] = grid
    return pl.pallas_call(wrapped, **kwargs)(*prefetch, *operands, *after)


def _matmul(name, a, b, dims, grid, a_spec, b_spec, out_shape, out_specs, acc_shape,
            epilogue, extras=(), extra_specs=(), after=None, prefetch=(), b_parts=1):
    nk = grid[2]
    n_extra = len(extras)
    n_out = len(out_shape)
    n_pre = len(prefetch)

    def body(*refs):
        refs = refs[n_pre:]
        a_ref, b_ref = refs[0], refs[1]
        extra_refs = refs[2:2 + n_extra]
        out_refs = refs[2 + n_extra:2 + n_extra + n_out]

        def finish(acc):
            outs = epilogue(acc, *[r[...] for r in extra_refs])
            for o_ref, o in zip(out_refs, outs):
                o_ref[...] = o.astype(o_ref.dtype)

        def product():
            if b_parts == 1:
                return lax.dot_general(a_ref[...], b_ref[...], dims, preferred_element_type=F32)
            width = b_ref.shape[2]
            total = None
            for p in range(b_parts):
                part = lax.dot_general(a_ref[:, p * width:(p + 1) * width], b_ref[p], dims,
                                       preferred_element_type=F32)
                total = part if total is None else total + part
            return total

        if nk == 1:
            finish(product())
        else:
            acc_ref = refs[-1]
            k = pl.program_id(2)

            @pl.when(k == 0)
            def _():
                acc_ref[...] = product()

            if nk > 2:
                @pl.when(jnp.logical_and(k > 0, k < nk - 1))
                def _():
                    acc_ref[...] += product()

            @pl.when(k == nk - 1)
            def _():
                finish(acc_ref[...] + product())

    scratch = [] if nk == 1 else [pltpu.VMEM(acc_shape, F32)]
    return _pcall(name, body, [a, b, *extras], [a_spec, b_spec, *extra_specs], out_shape, out_specs,
                  grid=grid, sem=("parallel", "parallel", "arbitrary"), scratch=scratch, after=after,
                  prefetch=prefetch)


def _mm_nn(name, a, b, epilogue, out_dtypes, extras=(), extra_kinds=(), slab=False, after=None,
           tm=1024, tn=1024, tk=2048):
    m, kd = a.shape
    if slab:
        n_slab, _, w = b.shape
        n = n_slab * w
        tn = _tile(w, min(tn, w))
        per = w // tn
        tk = _tile(kd, tk)
        b_spec = pl.BlockSpec((None, tk, tn), lambda i, j, k: (j // per, k, j % per))
    else:
        n = b.shape[1]
        tn = _tile(n, tn)
        tk = _tile(kd, tk)
        b_spec = pl.BlockSpec((tk, tn), lambda i, j, k: (k, j))
    tm = _tile(m, tm)
    grid = (m // tm, n // tn, kd // tk)
    a_spec = pl.BlockSpec((tm, tk), lambda i, j, k: (i, k))
    tile_spec = pl.BlockSpec((tm, tn), lambda i, j, k: (i, j))
    row_spec = pl.BlockSpec((1, tn), lambda i, j, k: (0, j))
    extra_specs = [tile_spec if kind == "tile" else row_spec for kind in extra_kinds]
    return _matmul(name, a, b, NN, grid, a_spec, b_spec,
                   [_sds((m, n), d) for d in out_dtypes], [tile_spec for _ in out_dtypes],
                   (tm, tn), epilogue, extras, extra_specs, after=after)


def _mm_nt(name, a, b, epilogue, out_dtypes, extras=(), extra_kinds=(), slab=False, after=None,
           tm=1024, tn=1024, tk=2048):
    m, kd = a.shape
    parts = 1
    if slab:
        n_slab, n, w = b.shape
        tn = _tile(n, tn)
        if tk > w and tk % w == 0 and n_slab % (tk // w) == 0:
            parts = tk // w
            b_spec = pl.BlockSpec((parts, tn, w), lambda i, j, k, *_: (k, j, 0))
        else:
            tk = _tile(w, min(tk, w))
            per = w // tk
            b_spec = pl.BlockSpec((None, tn, tk), lambda i, j, k, *_: (k // per, j, k % per))
    else:
        n = b.shape[0]
        tn = _tile(n, tn)
        tk = _tile(kd, tk)
        b_spec = pl.BlockSpec((tn, tk), lambda i, j, k, *_: (j, k))
    tm = _tile(m, tm)
    grid = (m // tm, n // tn, kd // tk)
    a_spec = pl.BlockSpec((tm, tk), lambda i, j, k, *_: (i, k))
    tile_spec = pl.BlockSpec((tm, tn), lambda i, j, k, *_: (i, j))
    row_spec = pl.BlockSpec((1, tn), lambda i, j, k, *_: (0, j))
    extra_specs = [tile_spec if kind == "tile" else row_spec for kind in extra_kinds]
    return _matmul(name, a, b, NT, grid, a_spec, b_spec,
                   [_sds((m, n), d) for d in out_dtypes], [tile_spec for _ in out_dtypes],
                   (tm, tn), epilogue, extras, extra_specs, after=after, b_parts=parts)


def _mm_tn_half(name, a, b, core, own, by_rows, block, recv=None, after=None, tm=1024, tn=1024, tk=2048):
    s, m = a.shape
    n = b.shape[1]
    tk = _tile(s, tk)

    def owner(chip, core_ref):
        return 2 * chip + (core_ref[0] if own else 1 - core_ref[0])

    if by_rows:
        r, c = block, n
        tm, tn = _tile(r, min(tm, r)), _tile(c, tn)
        per = r // tm
        grid = (N_CHIPS * per, c // tn, s // tk)
        a_spec = pl.BlockSpec((tk, tm), lambda i, j, k, cr, *_: (k, owner(i // per, cr) * per + i % per))
        b_spec = pl.BlockSpec((tk, tn), lambda i, j, k, cr, *_: (k, j))
        o_spec = pl.BlockSpec((None, tm, tn), lambda i, j, k, cr, *_: (i // per, i % per, j))
    else:
        r, c = m, block
        tm, tn = _tile(r, tm), _tile(c, min(tn, c))
        per = c // tn
        grid = (r // tm, N_CHIPS * per, s // tk)
        a_spec = pl.BlockSpec((tk, tm), lambda i, j, k, cr, *_: (k, i))
        b_spec = pl.BlockSpec((tk, tn), lambda i, j, k, cr, *_: (k, owner(j // per, cr) * per + j % per))
        o_spec = pl.BlockSpec((None, tm, tn), lambda i, j, k, cr, *_: (j // per, i, j % per))
    if recv is None:
        extras, epilogue = (), lambda acc: (acc,)
    else:
        extras, epilogue = (recv,), lambda acc, other: (acc + other.astype(F32),)
    return _matmul(name, a, b, TN, grid, a_spec, b_spec, [_sds((N_CHIPS, r, c), BF16)], [o_spec], (tm, tn),
                   epilogue, extras, [o_spec] * len(extras), after=after, prefetch=[core])[0]


def _rms_fwd(name, h, g, after=None):
    s, d = h.shape
    tr = _tile(s, 512)

    def body(h_ref, g_ref, o_ref):
        x = h_ref[...]
        r = lax.rsqrt(jnp.mean(x * x, axis=-1, keepdims=True) + EPS)
        o_ref[...] = (x * r * g_ref[...]).astype(o_ref.dtype)

    row = pl.BlockSpec((tr, d), lambda i: (i, 0))
    vec = pl.BlockSpec((1, d), lambda i: (0, 0))
    return _pcall(name, body, [h, g], [row, vec], _sds((s, d), BF16), row, grid=(s // tr,),
                  sem=("parallel",), after=after)


def _accumulate(ref, part, step):
    @pl.when(step == 0)
    def _():
        ref[...] = part

    @pl.when(step > 0)
    def _():
        ref[...] += part


def _rms_bwd(name, dhn, h, g, dres, after=None):
    s, d = h.shape
    tr = _tile(s, 512)
    steps = s // tr

    def body(dhn_ref, h_ref, g_ref, dres_ref, dh_ref, dhb_ref, gp_ref, acc_ref):
        step = pl.program_id(0)

        @pl.when(step == 0)
        def _():
            acc_ref[...] = jnp.zeros_like(acc_ref)

        gain = g_ref[...]

        def chunk(i, carry):
            rows = pl.ds(pl.multiple_of(i * ROW_CHUNK, ROW_CHUNK), ROW_CHUNK)
            x = h_ref[rows, :]
            r = lax.rsqrt(jnp.mean(x * x, axis=-1, keepdims=True) + EPS)
            n = x * r
            dy = dhn_ref[rows, :]
            dn = dy * gain
            dh = dres_ref[rows, :] + r * (dn - n * jnp.mean(dn * n, axis=-1, keepdims=True))
            dh_ref[rows, :] = dh
            dhb_ref[rows, :] = dh.astype(BF16)
            acc_ref[...] += dy * n
            return carry

        lax.fori_loop(0, tr // ROW_CHUNK, chunk, 0, unroll=8)

        @pl.when(step == steps - 1)
        def _():
            gp_ref[...] = jnp.sum(acc_ref[...], axis=0, keepdims=True)

    row = pl.BlockSpec((tr, d), lambda i: (i, 0))
    vec = pl.BlockSpec((1, d), lambda i: (0, 0))
    return _pcall(name, body, [dhn, h, g, dres], [row, row, vec, row],
                  [_sds((s, d), F32), _sds((s, d), BF16), _sds((1, d), F32)], [row, row, vec],
                  grid=(steps,), sem=("arbitrary",), scratch=[pltpu.VMEM((ROW_CHUNK, d), F32)], after=after)


def _loss_head(name, h, g, target, after=None):
    s, d = h.shape
    tr = _tile(s, 512)
    steps = s // tr

    def body(h_ref, g_ref, t_ref, dh_ref, dhb_ref, gp_ref, loss_ref, acc_ref, loss_acc_ref):
        step = pl.program_id(0)

        @pl.when(step == 0)
        def _():
            acc_ref[...] = jnp.zeros_like(acc_ref)
            loss_acc_ref[...] = jnp.zeros_like(loss_acc_ref)

        gg = g_ref[...]

        def chunk(i, carry):
            rows = pl.ds(pl.multiple_of(i * ROW_CHUNK, ROW_CHUNK), ROW_CHUNK)
            x = h_ref[rows, :]
            r = lax.rsqrt(jnp.mean(x * x, axis=-1, keepdims=True) + EPS)
            n = x * r
            e = n * gg - t_ref[rows, :]
            dy = e * (1.0 / d)
            dn = dy * gg
            dh = r * (dn - n * jnp.mean(dn * n, axis=-1, keepdims=True))
            dh_ref[rows, :] = dh
            dhb_ref[rows, :] = dh.astype(BF16)
            acc_ref[...] += dy * n
            loss_acc_ref[...] += jnp.mean(e * e, axis=-1, keepdims=True)
            return carry

        lax.fori_loop(0, tr // ROW_CHUNK, chunk, 0, unroll=8)

        @pl.when(step == steps - 1)
        def _():
            gp_ref[...] = jnp.sum(acc_ref[...], axis=0, keepdims=True)
            loss_ref[...] = 0.5 * jnp.sum(loss_acc_ref[...], axis=0, keepdims=True)

    row = pl.BlockSpec((tr, d), lambda i: (i, 0))
    vec = pl.BlockSpec((1, d), lambda i: (0, 0))
    one = pl.BlockSpec((1, 1), lambda i: (0, 0))
    return _pcall(name, body, [h, g, target], [row, vec, row],
                  [_sds((s, d), F32), _sds((s, d), BF16), _sds((1, d), F32), _sds((1, 1), F32)],
                  [row, row, vec, one], grid=(steps,), sem=("arbitrary",),
                  scratch=[pltpu.VMEM((ROW_CHUNK, d), F32), pltpu.VMEM((ROW_CHUNK, 1), F32)], after=after)


_SQRT_HALF = math.sqrt(0.5)
_INV_SQRT_2PI = 1.0 / math.sqrt(2.0 * math.pi)


def _gelu(x):
    return 0.5 * x * (1.0 + lax.erf(x * _SQRT_HALF))


def _gelu_grad(x):
    return 0.5 * (1.0 + lax.erf(x * _SQRT_HALF)) + x * jnp.exp(-0.5 * x * x) * _INV_SQRT_2PI


def _causal_mask():
    row = lax.broadcasted_iota(jnp.int32, (CHUNK, CHUNK), 0)
    col = lax.broadcasted_iota(jnp.int32, (CHUNK, CHUNK), 1)
    return row >= col


def _row_sum(x):
    return jnp.sum(x, axis=-1, keepdims=True)


def _masked_spatial(ws_ref, grp):
    return jnp.where(_causal_mask(), ws_ref[grp], 0.0).astype(BF16)


def _layernorm_stats(pre_ref, v_scr, w, head):
    total = jnp.zeros((CHUNK, 1), F32)
    for grp in range(A_GROUPS):
        v = _gelu(pre_ref[:, w + grp * head:w + (grp + 1) * head])
        v_scr[:, grp * head:(grp + 1) * head] = v
        total = total + _row_sum(v)
    mu = total * (1.0 / w)
    square = jnp.zeros((CHUNK, 1), F32)
    for grp in range(A_GROUPS):
        xc = v_scr[:, grp * head:(grp + 1) * head] - mu
        square = square + _row_sum(xc * xc)
    return mu, lax.rsqrt(square * (1.0 / w) + EPS)


def _amix_fwd(name, pre, ln_g, ln_b, w_s, b_s_col, after=None):
    s, w2 = pre.shape
    w = w2 // 2
    head = w // A_GROUPS

    def body(pre_ref, g_ref, b_ref, ws_ref, bs_ref, o_ref, v_scr):
        mu, rstd = _layernorm_stats(pre_ref, v_scr, w, head)
        for grp in range(A_GROUPS):
            cols = slice(grp * head, (grp + 1) * head)
            vhat = (v_scr[:, cols] - mu) * rstd
            vn = (vhat * g_ref[:, cols] + b_ref[:, cols]).astype(BF16)
            sg = jnp.dot(_masked_spatial(ws_ref, grp), vn, preferred_element_type=F32) + bs_ref[grp]
            o_ref[:, cols] = (_gelu(pre_ref[:, cols]) * sg).astype(o_ref.dtype)

    vec = pl.BlockSpec((1, w), lambda i: (0, 0))
    return _pcall(
        name, body, [pre, ln_g, ln_b, w_s, b_s_col],
        [pl.BlockSpec((CHUNK, w2), lambda i: (i, 0)), vec, vec,
         pl.BlockSpec((A_GROUPS, CHUNK, CHUNK), lambda i: (0, 0, 0)),
         pl.BlockSpec((A_GROUPS, CHUNK, 1), lambda i: (0, 0, 0))],
        _sds((s, w), BF16), pl.BlockSpec((CHUNK, w), lambda i: (i, 0)),
        grid=(s // CHUNK,), sem=("parallel",), scratch=[pltpu.VMEM((CHUNK, w), F32)], after=after)


def _amix_bwd(name, pre, dgated, ln_g, ln_b, w_s, b_s_col, after=None):
    s, w2 = pre.shape
    w = w2 // 2
    head = w // A_GROUPS

    def body(pre_ref, dg_ref, g_ref, b_ref, ws_ref, bs_ref, dpre_ref, glg_ref, glb_ref, gws_ref, gbs_ref,
             v_scr, dvn_scr):
        @pl.when(pl.program_id(0) == 0)
        def _():
            for ref in (glg_ref, glb_ref, gws_ref, gbs_ref):
                ref[...] = jnp.zeros_like(ref)

        mu, rstd = _layernorm_stats(pre_ref, v_scr, w, head)
        mask = _causal_mask()
        sum_dvhat = jnp.zeros((CHUNK, 1), F32)
        sum_dvhat_vhat = jnp.zeros((CHUNK, 1), F32)
        for grp in range(A_GROUPS):
            cols = slice(grp * head, (grp + 1) * head)
            vhat = (v_scr[:, cols] - mu) * rstd
            gain = g_ref[:, cols]
            vn = (vhat * gain + b_ref[:, cols]).astype(BF16)
            wm = _masked_spatial(ws_ref, grp)
            pre_u = pre_ref[:, cols]
            dgated = dg_ref[:, cols]
            ds = dgated * _gelu(pre_u)
            dsb = ds.astype(BF16)
            sg = jnp.dot(wm, vn, preferred_element_type=F32) + bs_ref[grp]
            dpre_ref[:, cols] = (dgated * sg * _gelu_grad(pre_u)).astype(dpre_ref.dtype)
            gws = lax.dot_general(dsb, vn, NT, preferred_element_type=F32)
            gws_ref[grp] += jnp.where(mask, gws, 0.0)
            gbs_ref[grp] += _row_sum(ds)
            dvn = lax.dot_general(wm, dsb, TN, preferred_element_type=F32)
            dvn_scr[:, cols] = dvn
            glg_ref[:, cols] += jnp.sum(dvn * vhat, axis=0, keepdims=True)
            glb_ref[:, cols] += jnp.sum(dvn, axis=0, keepdims=True)
            dvhat = dvn * gain
            sum_dvhat = sum_dvhat + _row_sum(dvhat)
            sum_dvhat_vhat = sum_dvhat_vhat + _row_sum(dvhat * vhat)
        mean_dvhat = sum_dvhat * (1.0 / w)
        mean_dvhat_vhat = sum_dvhat_vhat * (1.0 / w)
        for grp in range(A_GROUPS):
            cols = slice(grp * head, (grp + 1) * head)
            vhat = (v_scr[:, cols] - mu) * rstd
            dvhat = dvn_scr[:, cols] * g_ref[:, cols]
            dv = rstd * (dvhat - mean_dvhat - vhat * mean_dvhat_vhat)
            pre_v = pre_ref[:, w + grp * head:w + (grp + 1) * head]
            dpre_ref[:, w + grp * head:w + (grp + 1) * head] = (dv * _gelu_grad(pre_v)).astype(dpre_ref.dtype)

    vec = pl.BlockSpec((1, w), lambda i: (0, 0))
    ws_spec = pl.BlockSpec((A_GROUPS, CHUNK, CHUNK), lambda i: (0, 0, 0))
    bs_spec = pl.BlockSpec((A_GROUPS, CHUNK, 1), lambda i: (0, 0, 0))
    return _pcall(
        name, body, [pre, dgated, ln_g, ln_b, w_s, b_s_col],
        [pl.BlockSpec((CHUNK, w2), lambda i: (i, 0)), pl.BlockSpec((CHUNK, w), lambda i: (i, 0)),
         vec, vec, ws_spec, bs_spec],
        [_sds((s, w2), BF16), _sds((1, w), F32), _sds((1, w), F32),
         _sds((A_GROUPS, CHUNK, CHUNK), F32), _sds((A_GROUPS, CHUNK, 1), F32)],
        [pl.BlockSpec((CHUNK, w2), lambda i: (i, 0)), vec, vec, ws_spec, bs_spec],
        grid=(s // CHUNK,), sem=("arbitrary",),
        scratch=[pltpu.VMEM((CHUNK, w), F32), pltpu.VMEM((CHUNK, w), F32)], after=after)


def _shift_rows(x, k, forward):
    n = x.shape[0]
    row = lax.broadcasted_iota(jnp.int32, x.shape, 0)
    if forward:
        return jnp.where(row >= k, pltpu.roll(x, k, 0), 0.0)
    return jnp.where(row < n - k, pltpu.roll(x, n - k, 0), 0.0)


def _window_sum(x, window, forward):
    k = 1
    while k < window:
        x = x + _shift_rows(x, k, forward)
        k *= 2
    return x


def _pool(name, v, backward, after=None):
    s, w = v.shape
    head = w // B_GROUPS
    lane = _tile(head, 128)

    def body(v_ref, o_ref):
        grp = pl.program_id(0)
        t = lax.broadcasted_iota(jnp.int32, (s, lane), 0)
        for idx, window in enumerate(B_WINDOWS):
            @pl.when(grp == idx)
            def _():
                inv_count = 1.0 / jnp.minimum(t + 1, window).astype(F32)
                for strip in range(head // lane):
                    cols = slice(strip * lane, (strip + 1) * lane)
                    x = v_ref[:, cols]
                    if backward:
                        out = _window_sum(x * inv_count, window, False) - x
                    else:
                        out = _window_sum(x, window, True) * inv_count - x
                    o_ref[:, cols] = out.astype(o_ref.dtype)

    spec = pl.BlockSpec((s, head), lambda g: (0, g))
    return _pcall(name, body, [v], [spec], _sds((s, w), BF16), spec, grid=(B_GROUPS,),
                  sem=("parallel",), after=after)


def _scaled_dx(name, dy, w, scale, mixed, after=None, tm=1024, tn=1024):
    s, n = dy.shape
    k = w.shape[0]
    tm, tn = _tile(s, tm), _tile(k, tn)

    def body(dy_ref, w_ref, sc_ref, mx_ref, o_ref, gs_ref):
        p = lax.dot_general(dy_ref[...], w_ref[...], NT, preferred_element_type=F32)
        o_ref[...] = (p * sc_ref[...]).astype(o_ref.dtype)
        _accumulate(gs_ref, jnp.sum(p * mx_ref[...].astype(F32), axis=0, keepdims=True), pl.program_id(1))

    tile = pl.BlockSpec((tm, tn), lambda j, i: (i, j))
    vec = pl.BlockSpec((1, tn), lambda j, i: (0, j))
    return _pcall(name, body, [dy, w, scale, mixed],
                  [pl.BlockSpec((tm, n), lambda j, i: (i, 0)), pl.BlockSpec((tn, n), lambda j, i: (j, 0)), vec, tile],
                  [_sds((s, k), BF16), _sds((1, k), F32)], [tile, vec],
                  grid=(k // tn, s // tm), sem=("parallel", "arbitrary"), after=after)


def _adamw(w, g, m, v):
    m = ADAM_B1 * m + (1.0 - ADAM_B1) * g
    v = ADAM_B2 * v + (1.0 - ADAM_B2) * (g * g)
    m_hat = m / (1.0 - ADAM_B1 ** ADAM_STEP)
    v_hat = v / (1.0 - ADAM_B2 ** ADAM_STEP)
    delta = -ADAM_LR * (m_hat / (jnp.sqrt(v_hat) + ADAM_EPS) + ADAM_WD * w)
    return delta, m, v


def _adam_rows(name, g, w, m, v, after=None):
    r, c = g.shape
    tr = _tile(r, 256)

    def body(g_ref, w_ref, m_ref, v_ref, d_ref, nm_ref, nv_ref):
        d_ref[...], nm_ref[...], nv_ref[...] = _adamw(w_ref[...], g_ref[...], m_ref[...], v_ref[...])

    spec = pl.BlockSpec((tr, c), lambda i: (i, 0))
    return _pcall(name, body, [g, w, m, v], [spec] * 4, [_sds((r, c), F32)] * 3, [spec] * 3,
                  grid=(r // tr,), sem=("parallel",), after=after)


def _position():
    return lax.axis_index("x"), lax.axis_index("y"), lax.axis_index("c")


def _other_chips(x, y):
    return [(1 - x, y), (x, 1 - y), (1 - x, 1 - y)]


def _slot(px, py, pc):
    return 4 * px + 2 * py + pc


def _hbm(a):
    return pltpu.with_memory_space_constraint(a, pltpu.HBM)


def _hop1_copies(srcs, lands, send_sems, recv_sems):
    x, y, c = _position()
    peers = [(x, y, 1 - c), (1 - x, y, c), (x, 1 - y, c)]
    mine = _slot(x, y, c)
    return [[pltpu.make_async_remote_copy(
        src_ref=srcs[t], dst_ref=lands[t].at[mine], send_sem=send_sems[t].at[k], recv_sem=recv_sems[t].at[k],
        device_id=peer, device_id_type=MESH) for k, peer in enumerate(peers)] for t in range(len(srcs))]


def _hop2_copies(lands, send_sems, recv_sems):
    x, y, c = _position()
    routes = [(_slot(1 - x, y, c), (x, 1 - y, c)), (_slot(x, 1 - y, c), (1 - x, y, c))]
    out = []
    for t in range(len(lands)):
        rows = lands[t].shape[1]
        halves = [(0, rows // 2), (rows // 2, rows - rows // 2)]
        per_tensor = []
        for h, ((slot, peer), (start, size)) in enumerate(zip(routes, halves)):
            if size:
                block = lands[t].at[slot, pl.ds(start, size)]
                per_tensor.append(pltpu.make_async_remote_copy(
                    src_ref=block, dst_ref=block, send_sem=send_sems[t].at[h], recv_sem=recv_sems[t].at[h],
                    device_id=peer, device_id_type=MESH))
        for j, (slot, _) in enumerate(routes):
            block = lands[t].at[slot]
            per_tensor.append(pltpu.make_async_remote_copy(
                src_ref=block, dst_ref=block, send_sem=send_sems[t].at[2 + j], recv_sem=recv_sems[t].at[2 + j],
                device_id=(x, y, 1 - c), device_id_type=MESH))
        out.append(per_tensor)
    return out


def _split_start(name, srcs, lands, copies, n_sems, after=None):
    n = len(srcs)
    order = [] if after is None else [after]
    n_in = 2 * n + len(order)

    def body(*refs):
        for per_tensor in copies(refs[:n], refs[n:2 * n], refs[n_in:n_in + n], refs[n_in + n:n_in + 2 * n]):
            for cp in per_tensor:
                cp.start()
        refs[-1][...] = jnp.zeros_like(refs[-1])

    out_shape = ([pltpu.SemaphoreType.DMA((n_sems,)) for _ in range(2 * n)]
                 + [pltpu.HBM(a.shape, a.dtype) for a in list(srcs) + list(lands)]
                 + [_sds((8, 128), F32)])
    out = pl.pallas_call(
        body, name=name, out_shape=out_shape, in_specs=[_HBM] * (2 * n) + [_ANY] * len(order),
        out_specs=[_SEM] * (2 * n) + [_HBM] * (2 * n) + [pl.BlockSpec(memory_space=pltpu.VMEM)],
        input_output_aliases={i: 2 * n + i for i in range(2 * n)},
        compiler_params=pltpu.CompilerParams(has_side_effects=_EFFECT),
    )(*[_hbm(a) for a in srcs], *[_hbm(a) for a in lands], *order)
    return [(out[t], out[n + t], out[2 * n + t], out[3 * n + t]) for t in range(n)], out[-1]


def _split_wait(name, started, copies, after):
    n = len(started)

    def body(*refs):
        for per_tensor in copies(refs[:n], refs[n:2 * n], refs[2 * n:3 * n], refs[3 * n:4 * n]):
            for cp in per_tensor:
                cp.wait_send()
                cp.wait_recv()

    srcs = [e[2] for e in started]
    lands = [e[3] for e in started]
    out = pl.pallas_call(
        body, name=name, out_shape=[pltpu.HBM(a.shape, a.dtype) for a in srcs + lands],
        in_specs=[_HBM] * (2 * n) + [_SEM] * (2 * n) + [_ANY], out_specs=[_HBM] * (2 * n),
        input_output_aliases={i: i for i in range(2 * n)},
        compiler_params=pltpu.CompilerParams(has_side_effects=_EFFECT),
    )(*srcs, *lands, *[e[0] for e in started], *[e[1] for e in started], after)
    return out[:n], out[n:]


def _gather_step(name, arrived, fresh, after=None):
    n, m = len(arrived), len(fresh)
    order = [] if after is None else [after]
    fresh_lands = [lax.empty((N_DEV,) + s.shape, s.dtype) for s in fresh]
    buffers = [e[2] for e in arrived] + [e[3] for e in arrived] + list(fresh) + fresh_lands
    old_sems = [e[0] for e in arrived] + [e[1] for e in arrived]
    n_buf, n_old = len(buffers), len(old_sems)
    first_new = n_buf + n_old + len(order)

    def body(*refs):
        bufs, old = refs[:n_buf], refs[n_buf:n_buf + n_old]
        new = refs[first_new:first_new + 2 * n + 2 * m]
        for per_tensor in _hop1_copies(bufs[:n], bufs[n:2 * n], old[:n], old[n:]):
            for cp in per_tensor:
                cp.wait_send()
                cp.wait_recv()
        second = _hop2_copies(bufs[n:2 * n], new[:n], new[n:2 * n])
        first = _hop1_copies(bufs[2 * n:2 * n + m], bufs[2 * n + m:], new[2 * n:2 * n + m], new[2 * n + m:])
        for per_tensor in second + first:
            for cp in per_tensor:
                cp.start()
        refs[-1][...] = jnp.zeros_like(refs[-1])

    n_new = 2 * n + 2 * m
    out_shape = ([pltpu.SemaphoreType.DMA((4,)) for _ in range(2 * n)]
                 + [pltpu.SemaphoreType.DMA((3,)) for _ in range(2 * m)]
                 + [pltpu.HBM(a.shape, a.dtype) for a in buffers] + [_sds((8, 128), F32)])
    out = pl.pallas_call(
        body, name=name, out_shape=out_shape,
        in_specs=[_HBM] * n_buf + [_SEM] * n_old + [_ANY] * len(order),
        out_specs=[_SEM] * n_new + [_HBM] * n_buf + [pl.BlockSpec(memory_space=pltpu.VMEM)],
        input_output_aliases={i: n_new + i for i in range(n_buf)},
        compiler_params=pltpu.CompilerParams(has_side_effects=_EFFECT),
    )(*[_hbm(a) for a in buffers], *old_sems, *order)
    sems, bufs = out[:n_new], out[n_new:n_new + n_buf]
    second = [(sems[t], sems[n + t], bufs[t], bufs[n + t]) for t in range(n)]
    first = [(sems[2 * n + t], sems[2 * n + m + t], bufs[2 * n + t], bufs[2 * n + m + t]) for t in range(m)]
    return second, first, out[-1]


def _gather_wait(name, second, after):
    return _split_wait(name, second, lambda srcs, lands, send, recv: _hop2_copies(lands, send, recv), after)


def _sibling_copies(srcs, lands, send_sems, recv_sems):
    x, y, c = _position()
    return [[pltpu.make_async_remote_copy(
        src_ref=srcs[t], dst_ref=lands[t], send_sem=send_sems[t].at[0], recv_sem=recv_sems[t].at[0],
        device_id=(x, y, 1 - c), device_id_type=MESH)] for t in range(len(srcs))]


def _sibling_and_scatter_start(name, arrays, partials):
    k = len(arrays)

    def copies(srcs, lands, send_sems, recv_sems):
        return (_sibling_copies(srcs[:k], lands[:k], send_sems[:k], recv_sems[:k])
                + _scatter_copies(srcs[k:], lands[k:], send_sems[k:], recv_sems[k:]))

    lands = ([lax.empty(a.shape, a.dtype) for a in arrays]
             + [lax.empty((N_CHIPS - 1,) + p.shape[1:], p.dtype) for p in partials])
    return _split_start(name, list(arrays) + list(partials), lands, copies, 3)


def _sibling_wait(name, started, after):
    return _split_wait(name, started, _sibling_copies, after)[1]


def _small_copies(srcs, lands, send_sems, recv_sems):
    x, y, c = _position()
    mine = _slot(x, y, c)
    peers = [(x ^ ((k >> 2) & 1), y ^ ((k >> 1) & 1), c ^ (k & 1)) for k in range(1, N_DEV)]
    return [[pltpu.make_async_remote_copy(
        src_ref=srcs[t], dst_ref=lands[t].at[mine], send_sem=send_sems[t].at[k], recv_sem=recv_sems[t].at[k],
        device_id=peer, device_id_type=MESH) for k, peer in enumerate(peers)] for t in range(len(srcs))]


def _gather_finish(name, shards, lands, after):
    n = len(shards)

    def body(*refs):
        srcs, lands_in, outs = refs[:n], refs[n:2 * n], refs[2 * n:3 * n]
        send_sems, recv_sems, local_sems = refs[3 * n:]
        x, y, c = _position()
        local = [pltpu.make_async_copy(srcs[t], outs[t].at[_slot(x, y, c)], local_sems.at[t]) for t in range(n)]

        def diagonal(t, core):
            block = outs[t].at[_slot(1 - x, 1 - y, core)]
            return pltpu.make_async_remote_copy(
                src_ref=block, dst_ref=block, send_sem=send_sems.at[t], recv_sem=recv_sems.at[t],
                device_id=(x, y, 1 - c), device_id_type=MESH)

        for cp in local:
            cp.start()
        for t in range(n):
            diagonal(t, c).start()
        for t in range(n):
            diagonal(t, c).wait_send()
            diagonal(t, 1 - c).wait_recv()
        for cp in local:
            cp.wait()

    return _pcall(name, body, [*shards, *lands], [_ANY] * (2 * n),
                  [_sds(l.shape, l.dtype) for l in lands], [_ANY] * n,
                  scratch=[pltpu.SemaphoreType.DMA((n,)), pltpu.SemaphoreType.DMA((n,)),
                           pltpu.SemaphoreType.DMA((n,))],
                  after=after, aliases={n + t: t for t in range(n)})


def _exchange_sibling(name, fulls, after):
    n = len(fulls)

    def body(*refs):
        src = refs[:n]
        out = refs[n:2 * n]
        send_sems, recv_sems = refs[2 * n:]
        x, y, c = _position()
        copies = [pltpu.make_async_remote_copy(
            src_ref=src[t].at[:, 1 - c], dst_ref=out[t], send_sem=send_sems.at[t], recv_sem=recv_sems.at[t],
            device_id=(x, y, 1 - c), device_id_type=MESH) for t in range(n)]
        for cp in copies:
            cp.start()
        for cp in copies:
            cp.wait()

    return _pcall(name, body, fulls, [_ANY] * n, [_sds((N_CHIPS,) + f.shape[2:], f.dtype) for f in fulls],
                  [_ANY] * n, scratch=[pltpu.SemaphoreType.DMA((n,)), pltpu.SemaphoreType.DMA((n,))],
                  after=after)


def _add_sibling(name, full, recv, core, after):
    _, _, r, c = full.shape
    tr = _tile(r, max(8, (256 * 1024) // c))

    def body(core_ref, f_ref, r_ref, o_ref):
        o_ref[...] = (f_ref[...].astype(F32) + r_ref[...].astype(F32)).astype(o_ref.dtype)

    return _pcall(
        name, body, [full, recv],
        [pl.BlockSpec((None, None, tr, c), lambda p, i, core_ref: (p, core_ref[0], i, 0)),
         pl.BlockSpec((None, tr, c), lambda p, i, core_ref: (p, i, 0))],
        _sds((N_CHIPS, r, c), BF16), pl.BlockSpec((None, tr, c), lambda p, i, core_ref: (p, i, 0)),
        grid=(N_CHIPS, r // tr), sem=("parallel", "parallel"), prefetch=[core], after=after)


def _scatter_copies(srcs, lands, send_sems, recv_sems):
    x, y, c = _position()
    return [[pltpu.make_async_remote_copy(
        src_ref=srcs[t].at[2 * px + py], dst_ref=lands[t].at[j],
        send_sem=send_sems[t].at[j], recv_sem=recv_sems[t].at[j],
        device_id=(px, py, c), device_id_type=MESH) for j, (px, py) in enumerate(_other_chips(x, y))]
        for t in range(len(srcs))]


def _scatter_wait(name, started, after):
    return _split_wait(name, started, _scatter_copies, after)


class _Job:
    def __init__(self, **fields):
        self.__dict__.update(fields)


def _reduce_adam_job(partial, recv, chip, w, m, v, layer, carried):
    n_layers, r, c = w.shape
    tr = _tile(r, max(8, (256 * 1024) // c))

    def body(p_ref, r_ref, w_ref, m_ref, v_ref, *rest):
        g_ref, d_ref, nm_ref, nv_ref = rest[-4:]
        g = p_ref[...].astype(F32)
        for j in range(N_CHIPS - 1):
            g = g + r_ref[j].astype(F32)
        g_ref[...] = g
        d_ref[...], nm_ref[...], nv_ref[...] = _adamw(w_ref[...], g, m_ref[...], v_ref[...])

    layered = ((None, tr, c), lambda blk, chip_ref: (layer, blk, 0))
    in_specs = [((None, tr, c), lambda blk, chip_ref: (chip_ref[0], blk, 0)),
                ((N_CHIPS - 1, tr, c), lambda blk, chip_ref: (0, blk, 0)), layered, layered, layered]
    operands = [partial, recv, w, m, v]
    aliases = {}
    if carried is not None:
        operands += list(carried)
        in_specs += [None] * 4
        aliases = {5 + o: o for o in range(4)}
    return _Job(operands=operands, in_specs=in_specs, out_shape=[_sds((n_layers, r, c), F32)] * 4,
                out_specs=[layered] * 4, body=body, aliases=aliases, prefetch=chip, n_blocks=r // tr)


def _run_job(name, job, after):
    def spec(entry):
        if entry is None:
            return _ANY
        shape, index = entry
        return pl.BlockSpec(shape, lambda blk, pre, index=index: index(blk, pre))

    def body(pre_ref, *refs):
        job.body(*refs)

    return _pcall(name, body, job.operands, [spec(e) for e in job.in_specs], job.out_shape,
                  [spec(e) for e in job.out_specs], grid=(job.n_blocks,), sem=("parallel",),
                  prefetch=[job.prefetch], after=after,
                  aliases={1 + i: o for i, o in job.aliases.items()})


def _small_sum(name, gathered, own, device, after=None):
    r, lanes = own.shape

    def body(dev_ref, g_ref, own_ref, out_ref):
        dev = dev_ref[0]
        mine = own_ref[...]
        total = jnp.where(dev == 0, mine, g_ref[0])
        for d in range(1, N_DEV):
            total = total + jnp.where(dev == d, mine, g_ref[d])
        out_ref[...] = total

    return _pcall(name, body, [gathered, own],
                  [pl.BlockSpec((N_DEV, r, lanes), lambda i, dev_ref: (0, 0, 0)),
                   pl.BlockSpec((r, lanes), lambda i, dev_ref: (0, 0))],
                  _sds((r, lanes), F32), pl.BlockSpec((r, lanes), lambda i, dev_ref: (0, 0)),
                  grid=(1,), sem=("arbitrary",), prefetch=[device], after=after)


def _pack(arrays):
    return jnp.concatenate([a.reshape(-1, 128) for a in arrays], axis=0)


def _unpack(packed, shapes):
    out, row = [], 0
    for shape in shapes:
        rows = math.prod(shape) // 128
        out.append(packed[row:row + rows].reshape(shape))
        row += rows
    return out


class _Order:
    def __init__(self):
        self.last = None

    def __call__(self, fn, *args, **kwargs):
        out = fn(*args, after=self.last, **kwargs)
        self.last = out[0] if isinstance(out, (list, tuple)) else out
        return out


def kernel(x, a_w_in, a_ln_g, a_ln_b, a_w_s, a_b_s, a_w_out, b_w_in, b_w_grp, b_scale, b_w_out, norm_mix, norm_mlp, mlp_w1, mlp_w2, final_norm, loss_target, m_a_w_in, m_a_ln_g, m_a_ln_b, m_a_w_s, m_a_b_s, m_a_w_out, m_b_w_in, m_b_w_grp, m_b_scale, m_b_w_out, m_norm_mix, m_norm_mlp, m_mlp_w1, m_mlp_w2, m_final_norm, v_a_w_in, v_a_ln_g, v_a_ln_b, v_a_w_s, v_a_b_s, v_a_w_out, v_b_w_in, v_b_w_grp, v_b_scale, v_b_w_out, v_norm_mix, v_norm_mlp, v_mlp_w1, v_mlp_w2, v_final_norm):
    s, d = x.shape[1], x.shape[2]
    depth = mlp_w1.shape[0]
    a_slab = a_w_in.shape[2]
    ff_slab = mlp_w1.shape[2]
    ff_rows = mlp_w2.shape[1]
    bh = b_w_grp.shape[3]
    my_x, my_y, my_c = _position()
    core = jnp.reshape(my_c, (1,)).astype(jnp.int32)
    chip = jnp.reshape(2 * my_x + my_y, (1,)).astype(jnp.int32)
    device = _slot(my_x, my_y, my_c)
    run = _Order()

    w1_b, w2_b = mlp_w1.astype(BF16), mlp_w2.astype(BF16)
    shards = [a_w_in[0].astype(BF16), a_w_out[0].astype(BF16), b_scale,
              w1_b[0], w2_b[0],
              b_w_in[0].astype(BF16), b_w_grp[0].astype(BF16), b_w_out[0].astype(BF16),
              w1_b[1], w2_b[1]]
    groups = [[0], [1, 2], [3], [4], [5, 6, 7], [8], [9]]
    start_with = {1: [2, 3], 2: [4], 3: [5], 4: [6]}
    hop1, hop2 = {}, {}
    _, hop1[0], token = _gather_step("weights_group0_hop1", [], [shards[t] for t in groups[0]])
    _, hop1[1], token = _gather_step("weights_group1_hop1", [], [shards[t] for t in groups[1]], token)
    run.last = token

    def advance(g):
        if g not in hop1:
            return
        ahead = start_with.get(g, [])
        fresh = [shards[t] for a in ahead for t in groups[a]]
        hop2[g], started, tok = _gather_step(f"weights_group{g}_hop2", hop1.pop(g), fresh, run.last)
        for a in ahead:
            hop1[a], started = started[:len(groups[a])], started[len(groups[a]):]
        run.last = tok

    def gathered(g):
        advance(g)
        if g == 0:
            advance(1)
        srcs, lands = _gather_wait(f"weights_group{g}_wait", hop2.pop(g), run.last)
        run.last = srcs[0]
        return run(_gather_finish, f"weights_group{g}_finish", srcs, lands)

    h0 = x[0]
    target = loss_target[0]
    ln_g, ln_b = a_ln_g, a_ln_b
    w_s = a_w_s[0]
    b_s_col = a_b_s[0][:, :, None]
    nmix = [norm_mix[l][None, :] for l in range(depth)]
    nmlp = [norm_mlp[l][None, :] for l in range(depth)]

    def mlp_forward(l, h, up_group):
        hn = run(_rms_fwd, f"mlp{l}_norm", h, nmlp[l])
        (w1,) = gathered(up_group)
        advance(up_group + 1)
        act, act_sq = run(_mm_nn, f"mlp{l}_up", hn, w1,
                          lambda acc: (jnp.maximum(acc, 0.0), jnp.square(jnp.maximum(acc, 0.0))),
                          (BF16, BF16), slab=True)
        (w2,) = gathered(up_group + 1)
        advance(up_group + 2)
        w2 = w2.reshape(-1, d)
        (h_out,) = run(_mm_nn, f"mlp{l}_down", act_sq, w2, lambda acc, res: (acc + res,), (F32,),
                       extras=(h,), extra_kinds=("tile",))
        return h_out, (h, hn, act, act_sq, w1, w2)

    scattered = []

    pending = []

    def scatter_partials(name, partials, specs):
        pending.append((name, partials, specs))

    def start_exchanges(name, to_sibling):
        partials = [p for _, group, _ in pending for p in group]
        in_flight, tok = _sibling_and_scatter_start(name, to_sibling, partials)
        run.last = tok
        first = len(to_sibling)
        for group_name, group, specs in pending:
            scattered.append((group_name, in_flight[first:first + len(group)], specs))
            first += len(group)
        pending.clear()
        return in_flight[:len(to_sibling)]

    weights = {"a_w_in": (a_w_in, m_a_w_in, v_a_w_in), "a_w_out": (a_w_out, m_a_w_out, v_a_w_out),
               "b_w_in": (b_w_in, m_b_w_in, v_b_w_in), "b_w_grp": (b_w_grp, m_b_w_grp, v_b_w_grp),
               "b_w_out": (b_w_out, m_b_w_out, v_b_w_out), "mlp_w1": (mlp_w1, m_mlp_w1, v_mlp_w1),
               "mlp_w2": (mlp_w2, m_mlp_w2, v_mlp_w2)}
    results = {}

    def finish_group(name, in_flight, specs):
        partials, lands = _scatter_wait(name + "_scatter_wait", in_flight, run.last)
        run.last = lands[0]
        for t, (wname, layer) in enumerate(specs):
            w, m, v = weights[wname]
            shape = (w.shape[0],) + partials[t].shape[1:]
            job = _reduce_adam_job(partials[t], lands[t], chip, w.reshape(shape), m.reshape(shape),
                                   v.reshape(shape), layer, results.get(wname))
            results[wname] = run(_run_job, f"{name}_reduce_adam_{t}", job)

    def weight_grad(name, a, b, by_rows, block, between, tm=1024, tn=1024):
        other = run(_mm_tn_half, name + "_other", a, b, core, False, by_rows, block, tm=tm, tn=tn)
        sent = start_exchanges(name + "_sibling_start", [other])
        middle = between()
        (recv,) = _sibling_wait(name + "_sibling_wait", sent, run.last)
        run.last = recv
        return run(_mm_tn_half, name + "_own", a, b, core, True, by_rows, block, recv=recv, tm=tm, tn=tn), middle

    def mlp_backward(l, saved, dh, dhb):
        h, hn, act, act_sq, w1, w2 = saved
        part_w2, (dpre,) = weight_grad(
            f"mlp{l}_down_dw", act_sq, dhb, True, ff_rows,
            lambda: run(_mm_nt, f"mlp{l}_down_dx", dhb, w2, lambda acc, a: (2.0 * a.astype(F32) * acc,),
                        (BF16,), extras=(act,), extra_kinds=("tile",)))
        scatter_partials(f"mlp{l}_down_grads", [part_w2], [("mlp_w2", l)])
        part_w1, (dhn,) = weight_grad(
            f"mlp{l}_up_dw", hn, dpre, False, ff_slab,
            lambda: run(_mm_nt, f"mlp{l}_up_dx", dpre, w1, lambda acc: (acc,), (F32,), slab=True))
        scatter_partials(f"mlp{l}_up_grads", [part_w1], [("mlp_w1", l)])
        dh, dhb, g_norm = run(_rms_bwd, f"mlp{l}_norm_bwd", dhn, h, nmlp[l], dh)
        return dh, dhb, g_norm

    hn0 = run(_rms_fwd, "mix0_norm", h0, nmix[0])
    (wa_in,) = gathered(0)
    (pre,) = run(_mm_nn, "mixa_in", hn0, wa_in, lambda acc: (acc,), (F32,), slab=True)
    wa_out, scale = gathered(1)
    wa_out, scale = wa_out.reshape(d, d), scale.reshape(1, d)
    gated = run(_amix_fwd, "mixa_gate", pre, ln_g, ln_b, w_s, b_s_col)
    advance(2)
    (h1,) = run(_mm_nn, "mixa_out", gated, wa_out, lambda acc, res: (acc + res,), (F32,),
                extras=(h0,), extra_kinds=("tile",))
    h2, saved_mlp0 = mlp_forward(0, h1, 2)
    hn2 = run(_rms_fwd, "mix1_norm", h2, nmix[1])
    wb_in, wb_grp, wb_out = gathered(4)
    advance(5)
    wb_in, wb_out = wb_in.reshape(d, d), wb_out.reshape(d, d)
    wb_grp = jnp.transpose(wb_grp, (1, 0, 2, 3)).reshape(B_GROUPS, bh, bh)
    (vb,) = run(_mm_nn, "mixb_in", hn2, wb_in, lambda acc: (acc,), (F32,))
    pooled = run(_pool, "mixb_pool", vb, backward=False)
    tm = _tile(s, 1024)
    grp_tile = pl.BlockSpec((tm, bh), lambda i, j, k: (i, j))
    grp_weight = pl.BlockSpec((None, bh, bh), lambda i, j, k: (j, 0, 0))
    mixed, mixed_scaled = run(
        _matmul, "mixb_grp", pooled, wb_grp, NN, (s // tm, B_GROUPS, 1), grp_tile, grp_weight,
        [_sds((s, d), BF16), _sds((s, d), BF16)], [grp_tile] * 2,
        (tm, bh), lambda acc, sc: (acc, acc * sc), (scale,), [pl.BlockSpec((1, bh), lambda i, j, k: (0, j))])
    (h3,) = run(_mm_nn, "mixb_out", mixed_scaled, wb_out, lambda acc, res: (acc + res,), (F32,),
                extras=(h2,), extra_kinds=("tile",))
    h4, saved_mlp1 = mlp_forward(1, h3, 5)
    dh, dhb, g_final, loss_part = run(_loss_head, "loss_head", h4, final_norm[None, :], target)

    dh, dhb, g_nmlp1 = mlp_backward(1, saved_mlp1, dh, dhb)
    tks = _tile(s, 1024)
    grp_rows = pl.BlockSpec((tks, bh), lambda i, j, k: (k, j))

    def mixb_middle():
        dms_scaled, g_scale = run(_scaled_dx, "mixb_out_dx", dhb, wb_out, scale, mixed)
        (g_wb_grp,) = run(
            _matmul, "mixb_grp_dw", pooled, dms_scaled, TN, (1, B_GROUPS, s // tks), grp_rows, grp_rows,
            [_sds((B_GROUPS, bh, bh), BF16)], [grp_weight], (bh, bh), lambda acc: (acc,))
        (dpooled,) = run(
            _matmul, "mixb_grp_dx", dms_scaled, wb_grp, NT, (s // tm, B_GROUPS, 1), grp_tile, grp_weight,
            [_sds((s, d), F32)], [grp_tile], (tm, bh), lambda acc: (acc,))
        return g_scale, g_wb_grp, run(_pool, "mixb_pool_bwd", dpooled, backward=True)

    part_wb_out, (g_scale, g_wb_grp, dvb) = weight_grad("mixb_out_dw", mixed_scaled, dhb, True, d // N_DEV,
                                                        mixb_middle, tn=d)
    part_wb_in, (dhn2,) = weight_grad(
        "mixb_in_dw", hn2, dvb, True, d // N_DEV,
        lambda: run(_mm_nt, "mixb_in_dx", dvb, wb_in, lambda acc: (acc,), (F32,)), tn=d)
    grp_full = jnp.transpose(g_wb_grp.reshape(B_GROUPS, N_DEV, bh // N_DEV, bh), (1, 0, 2, 3))
    grp_full = grp_full.reshape(N_CHIPS, 2, B_GROUPS * bh // N_DEV, bh)
    (grp_sibling,) = run(_exchange_sibling, "mixb_grp_dw_to_sibling", [grp_full])
    part_wb_grp = run(_add_sibling, "mixb_grp_dw_add_sibling", grp_full, grp_sibling, core)
    scatter_partials("mixb_grads", [part_wb_out, part_wb_grp, part_wb_in],
                     [("b_w_out", 0), ("b_w_grp", 0), ("b_w_in", 0)])
    dh, dhb, g_nmix1 = run(_rms_bwd, "mix1_norm_bwd", dhn2, h2, nmix[1], dh)
    dh, dhb, g_nmlp0 = mlp_backward(0, saved_mlp0, dh, dhb)
    def mixa_middle():
        (dgated,) = run(_mm_nt, "mixa_out_dx", dhb, wa_out, lambda acc: (acc,), (F32,))
        return run(_amix_bwd, "mixa_gate_bwd", pre, dgated, ln_g, ln_b, w_s, b_s_col)

    part_wa_out, (dpre, g_ln_g, g_ln_b, g_w_s, g_b_s) = weight_grad("mixa_out_dw", gated, dhb, True, d // N_DEV,
                                                                     mixa_middle, tn=d)
    part_wa_in, (dhn0,) = weight_grad(
        "mixa_in_dw", hn0, dpre, False, a_slab,
        lambda: run(_mm_nt, "mixa_in_dx", dpre, wa_in, lambda acc: (acc,), (F32,), slab=True), tm=d)
    scatter_partials("mixa_grads", [part_wa_in, part_wa_out], [("a_w_in", 0), ("a_w_out", 0)])
    start_exchanges("mixa_grads_scatter_start", [])
    grad_x, _, g_nmix0 = run(_rms_bwd, "mix0_norm_bwd", dhn0, h0, nmix[0], dh)

    g_norm_mix = jnp.concatenate([g_nmix0, g_nmix1], axis=0)
    g_norm_mlp = jnp.concatenate([g_nmlp0, g_nmlp1], axis=0)
    loss_row = jnp.pad(loss_part, ((0, 0), (0, 127)))
    small_parts = [g_ln_g, g_ln_b, g_w_s, g_b_s, g_norm_mix, g_norm_mlp, g_final, g_scale, loss_row]
    packed = _pack(small_parts)
    small_sent, tok = _split_start("small_grads_start", [packed], [lax.empty((N_DEV,) + packed.shape, F32)],
                                   _small_copies, N_DEV - 1)
    run.last = tok

    for group in scattered[:-1]:
        finish_group(*group)
    own_packed, small_gathered = _split_wait("small_grads_wait", small_sent, _small_copies, run.last)
    run.last = small_gathered[0]
    small_sum = run(_small_sum, "small_grads_sum", small_gathered[0], own_packed[0],
                    jnp.reshape(device, (1,)).astype(jnp.int32))
    sg = _unpack(small_sum, [a_ln_g.shape, a_ln_b.shape, a_w_s.shape, a_b_s.shape, norm_mix.shape,
                             norm_mlp.shape, final_norm.shape, (1, d), (1, 128)])
    loss = sg.pop()[0, 0]
    shard = b_scale.shape[1]
    sg[7] = lax.dynamic_slice(sg[7], (0, device * shard), (1, shard))
    small_w = [a_ln_g, a_ln_b, a_w_s, a_b_s, norm_mix, norm_mlp, final_norm, b_scale]
    small_m = [m_a_ln_g, m_a_ln_b, m_a_w_s, m_a_b_s, m_norm_mix, m_norm_mlp, m_final_norm, m_b_scale]
    small_v = [v_a_ln_g, v_a_ln_b, v_a_w_s, v_a_b_s, v_norm_mix, v_norm_mlp, v_final_norm, v_b_scale]
    small_out = run(_adam_rows, "small_adam", _pack(sg), _pack(small_w), _pack(small_m), _pack(small_v))
    shapes = [w.shape for w in small_w]
    small_res = [sg] + [_unpack(o, shapes) for o in small_out]

    finish_group(*scattered[-1])
    big = {wname: [o.reshape(weights[wname][0].shape) for o in outs] for wname, outs in results.items()}

    def leaf(o):
        return (big["a_w_in"][o], small_res[o][0], small_res[o][1], small_res[o][2], small_res[o][3],
                big["a_w_out"][o], big["b_w_in"][o], big["b_w_grp"][o], small_res[o][7], big["b_w_out"][o],
                small_res[o][4], small_res[o][5], big["mlp_w1"][o], big["mlp_w2"][o], small_res[o][6])

    return (loss, grad_x[None], *leaf(0), *leaf(1), *leaf(2), *leaf(3))
```

```python
import math

import jax
import jax.numpy as jnp
from jax import lax
from jax.experimental import pallas as pl
from jax.experimental.pallas import tpu as pltpu

F32 = jnp.float32
BF16 = jnp.bfloat16
MESH = pl.DeviceIdType.MESH

N_DEV = 8
N_CHIPS = 4
CHUNK = 128
A_GROUPS = 8
B_WINDOWS = (2, 4, 8, 16)
B_GROUPS = len(B_WINDOWS)
EPS = 1e-6
ADAM_LR = 0.001
ADAM_B1 = 0.9
ADAM_B2 = 0.999
ADAM_EPS = 1e-08
ADAM_WD = 0.01
ADAM_STEP = 10

VMEM_LIMIT = 48 * 1024 * 1024
ROW_CHUNK = 16

NN = (((1,), (0,)), ((), ()))
NT = (((1,), (1,)), ((), ()))
TN = (((0,), (0,)), ((), ()))

_ANY = pl.BlockSpec(memory_space=pl.ANY)
_HBM = pl.BlockSpec(memory_space=pltpu.HBM)
_SEM = pl.BlockSpec(memory_space=pltpu.SEMAPHORE)
_EFFECT = pltpu.SideEffectType.DATAFLOW_SIDE_EFFECTING


def _tile(n, pref):
    return pref if n % pref == 0 else n


def _sds(shape, dtype):
    return jax.ShapeDtypeStruct(shape, dtype)


def _pcall(name, body, operands, in_specs, out_shape, out_specs, *, grid=None, sem=None, scratch=(),
           prefetch=(), after=None, aliases=None):
    after = [] if after is None else [after]
    n_lead = len(prefetch) + len(operands)
    n_after = len(after)

    def wrapped(*refs):
        body(*refs[:n_lead], *refs[n_lead + n_after:])

    in_specs = list(in_specs) + [_ANY] * n_after
    params = pltpu.CompilerParams(vmem_limit_bytes=VMEM_LIMIT) if sem is None else \
        pltpu.CompilerParams(dimension_semantics=sem, vmem_limit_bytes=VMEM_LIMIT)
    kwargs = dict(out_shape=out_shape, scratch_shapes=list(scratch), compiler_params=params, name=name,
                  input_output_aliases=aliases or {})
    if prefetch:
        kwargs["grid_spec"] = pltpu.PrefetchScalarGridSpec(
            num_scalar_prefetch=len(prefetch), grid=grid, in_specs=in_specs, out_specs=out_specs,
            scratch_shapes=list(scratch))
        kwargs.pop("scratch_shapes")
    else:
        kwargs.update(in_specs=in_specs, out_specs=out_specs)
        if grid is not None:
            kwargs["grid"] = grid
    return pl.pallas_call(wrapped, **kwargs)(*prefetch, *operands, *after)


def _matmul(name, a, b, dims, grid, a_spec, b_spec, out_shape, out_specs, acc_shape,
            epilogue, extras=(), extra_specs=(), after=None, prefetch=(), b_parts=1):
    nk = grid[2]
    n_extra = len(extras)
    n_out = len(out_shape)
    n_pre = len(prefetch)

    def body(*refs):
        refs = refs[n_pre:]
        a_ref, b_ref = refs[0], refs[1]
        extra_refs = refs[2:2 + n_extra]
        out_refs = refs[2 + n_extra:2 + n_extra + n_out]

        def finish(acc):
            outs = epilogue(acc, *[r[...] for r in extra_refs])
            for o_ref, o in zip(out_refs, outs):
                o_ref[...] = o.astype(o_ref.dtype)

        def product():
            if b_parts == 1:
                return lax.dot_general(a_ref[...], b_ref[...], dims, preferred_element_type=F32)
            width = b_ref.shape[2]
            total = None
            for p in range(b_parts):
                part = lax.dot_general(a_ref[:, p * width:(p + 1) * width], b_ref[p], dims,
                                       preferred_element_type=F32)
                total = part if total is None else total + part
            return total

        if nk == 1:
            finish(product())
        else:
            acc_ref = refs[-1]
            k = pl.program_id(2)

            @pl.when(k == 0)
            def _():
                acc_ref[...] = product()

            if nk > 2:
                @pl.when(jnp.logical_and(k > 0, k < nk - 1))
                def _():
                    acc_ref[...] += product()

            @pl.when(k == nk - 1)
            def _():
                finish(acc_ref[...] + product())

    scratch = [] if nk == 1 else [pltpu.VMEM(acc_shape, F32)]
    return _pcall(name, body, [a, b, *extras], [a_spec, b_spec, *extra_specs], out_shape, out_specs,
                  grid=grid, sem=("parallel", "parallel", "arbitrary"), scratch=scratch, after=after,
                  prefetch=prefetch)


def _mm_nn(name, a, b, epilogue, out_dtypes, extras=(), extra_kinds=(), slab=False, after=None,
           tm=1024, tn=1024, tk=2048):
    m, kd = a.shape
    if slab:
        n_slab, _, w = b.shape
        n = n_slab * w
        tn = _tile(w, min(tn, w))
        per = w // tn
        tk = _tile(kd, tk)
        b_spec = pl.BlockSpec((None, tk, tn), lambda i, j, k: (j // per, k, j % per))
    else:
        n = b.shape[1]
        tn = _tile(n, tn)
        tk = _tile(kd, tk)
        b_spec = pl.BlockSpec((tk, tn), lambda i, j, k: (k, j))
    tm = _tile(m, tm)
    grid = (m // tm, n // tn, kd // tk)
    a_spec = pl.BlockSpec((tm, tk), lambda i, j, k: (i, k))
    tile_spec = pl.BlockSpec((tm, tn), lambda i, j, k: (i, j))
    row_spec = pl.BlockSpec((1, tn), lambda i, j, k: (0, j))
    extra_specs = [tile_spec if kind == "tile" else row_spec for kind in extra_kinds]
    return _matmul(name, a, b, NN, grid, a_spec, b_spec,
                   [_sds((m, n), d) for d in out_dtypes], [tile_spec for _ in out_dtypes],
                   (tm, tn), epilogue, extras, extra_specs, after=after)


def _mm_nt(name, a, b, epilogue, out_dtypes, extras=(), extra_kinds=(), slab=False, after=None,
           tm=1024, tn=1024, tk=2048):
    m, kd = a.shape
    parts = 1
    if slab:
        n_slab, n, w = b.shape
        tn = _tile(n, tn)
        if tk > w and tk % w == 0 and n_slab % (tk // w) == 0:
            parts = tk // w
            b_spec = pl.BlockSpec((parts, tn, w), lambda i, j, k, *_: (k, j, 0))
        else:
            tk = _tile(w, min(tk, w))
            per = w // tk
            b_spec = pl.BlockSpec((None, tn, tk), lambda i, j, k, *_: (k // per, j, k % per))
    else:
        n = b.shape[0]
        tn = _tile(n, tn)
        tk = _tile(kd, tk)
        b_spec = pl.BlockSpec((tn, tk), lambda i, j, k, *_: (j, k))
    tm = _tile(m, tm)
    grid = (m // tm, n // tn, kd // tk)
    a_spec = pl.BlockSpec((tm, tk), lambda i, j, k, *_: (i, k))
    tile_spec = pl.BlockSpec((tm, tn), lambda i, j, k, *_: (i, j))
    row_spec = pl.BlockSpec((1, tn), lambda i, j, k, *_: (0, j))
    extra_specs = [tile_spec if kind == "tile" else row_spec for kind in extra_kinds]
    return _matmul(name, a, b, NT, grid, a_spec, b_spec,
                   [_sds((m, n), d) for d in out_dtypes], [tile_spec for _ in out_dtypes],
                   (tm, tn), epilogue, extras, extra_specs, after=after, b_parts=parts)


def _mm_tn_half(name, a, b, core, own, by_rows, block, recv=None, after=None, tm=1024, tn=1024, tk=2048):
    s, m = a.shape
    n = b.shape[1]
    tk = _tile(s, tk)

    def owner(chip, core_ref):
        return 2 * chip + (core_ref[0] if own else 1 - core_ref[0])

    if by_rows:
        r, c = block, n
        tm, tn = _tile(r, min(tm, r)), _tile(c, tn)
        per = r // tm
        grid = (N_CHIPS * per, c // tn, s // tk)
        a_spec = pl.BlockSpec((tk, tm), lambda i, j, k, cr, *_: (k, owner(i // per, cr) * per + i % per))
        b_spec = pl.BlockSpec((tk, tn), lambda i, j, k, cr, *_: (k, j))
        o_spec = pl.BlockSpec((None, tm, tn), lambda i, j, k, cr, *_: (i // per, i % per, j))
    else:
        r, c = m, block
        tm, tn = _tile(r, tm), _tile(c, min(tn, c))
        per = c // tn
        grid = (r // tm, N_CHIPS * per, s // tk)
        a_spec = pl.BlockSpec((tk, tm), lambda i, j, k, cr, *_: (k, i))
        b_spec = pl.BlockSpec((tk, tn), lambda i, j, k, cr, *_: (k, owner(j // per, cr) * per + j % per))
        o_spec = pl.BlockSpec((None, tm, tn), lambda i, j, k, cr, *_: (j // per, i, j % per))
    if recv is None:
        extras, epilogue = (), lambda acc: (acc,)
    else:
        extras, epilogue = (recv,), lambda acc, other: (acc + other.astype(F32),)
    return _matmul(name, a, b, TN, grid, a_spec, b_spec, [_sds((N_CHIPS, r, c), BF16)], [o_spec], (tm, tn),
                   epilogue, extras, [o_spec] * len(extras), after=after, prefetch=[core])[0]


def _rms_fwd(name, h, g, after=None):
    s, d = h.shape
    tr = _tile(s, 512)

    def body(h_ref, g_ref, o_ref):
        x = h_ref[...]
        r = lax.rsqrt(jnp.mean(x * x, axis=-1, keepdims=True) + EPS)
        o_ref[...] = (x * r * g_ref[...]).astype(o_ref.dtype)

    row = pl.BlockSpec((tr, d), lambda i: (i, 0))
    vec = pl.BlockSpec((1, d), lambda i: (0, 0))
    return _pcall(name, body, [h, g], [row, vec], _sds((s, d), BF16), row, grid=(s // tr,),
                  sem=("parallel",), after=after)


def _accumulate(ref, part, step):
    @pl.when(step == 0)
    def _():
        ref[...] = part

    @pl.when(step > 0)
    def _():
        ref[...] += part


def _rms_bwd(name, dhn, h, g, dres, after=None):
    s, d = h.shape
    tr = _tile(s, 512)
    steps = s // tr

    def body(dhn_ref, h_ref, g_ref, dres_ref, dh_ref, dhb_ref, gp_ref, acc_ref):
        step = pl.program_id(0)

        @pl.when(step == 0)
        def _():
            acc_ref[...] = jnp.zeros_like(acc_ref)

        gain = g_ref[...]

        def chunk(i, carry):
            rows = pl.ds(pl.multiple_of(i * ROW_CHUNK, ROW_CHUNK), ROW_CHUNK)
            x = h_ref[rows, :]
            r = lax.rsqrt(jnp.mean(x * x, axis=-1, keepdims=True) + EPS)
            n = x * r
            dy = dhn_ref[rows, :]
            dn = dy * gain
            dh = dres_ref[rows, :] + r * (dn - n * jnp.mean(dn * n, axis=-1, keepdims=True))
            dh_ref[rows, :] = dh
            dhb_ref[rows, :] = dh.astype(BF16)
            acc_ref[...] += dy * n
            return carry

        lax.fori_loop(0, tr // ROW_CHUNK, chunk, 0, unroll=8)

        @pl.when(step == steps - 1)
        def _():
            gp_ref[...] = jnp.sum(acc_ref[...], axis=0, keepdims=True)

    row = pl.BlockSpec((tr, d), lambda i: (i, 0))
    vec = pl.BlockSpec((1, d), lambda i: (0, 0))
    return _pcall(name, body, [dhn, h, g, dres], [row, row, vec, row],
                  [_sds((s, d), F32), _sds((s, d), BF16), _sds((1, d), F32)], [row, row, vec],
                  grid=(steps,), sem=("arbitrary",), scratch=[pltpu.VMEM((ROW_CHUNK, d), F32)], after=after)


def _loss_head(name, h, g, target, after=None):
    s, d = h.shape
    tr = _tile(s, 512)
    steps = s // tr

    def body(h_ref, g_ref, t_ref, dh_ref, dhb_ref, gp_ref, loss_ref, acc_ref, loss_acc_ref):
        step = pl.program_id(0)

        @pl.when(step == 0)
        def _():
            acc_ref[...] = jnp.zeros_like(acc_ref)
            loss_acc_ref[...] = jnp.zeros_like(loss_acc_ref)

        gg = g_ref[...]

        def chunk(i, carry):
            rows = pl.ds(pl.multiple_of(i * ROW_CHUNK, ROW_CHUNK), ROW_CHUNK)
            x = h_ref[rows, :]
            r = lax.rsqrt(jnp.mean(x * x, axis=-1, keepdims=True) + EPS)
            n = x * r
            e = n * gg - t_ref[rows, :]
            dy = e * (1.0 / d)
            dn = dy * gg
            dh = r * (dn - n * jnp.mean(dn * n, axis=-1, keepdims=True))
            dh_ref[rows, :] = dh
            dhb_ref[rows, :] = dh.astype(BF16)
            acc_ref[...] += dy * n
            loss_acc_ref[...] += jnp.mean(e * e, axis=-1, keepdims=True)
            return carry

        lax.fori_loop(0, tr // ROW_CHUNK, chunk, 0, unroll=8)

        @pl.when(step == steps - 1)
        def _():
            gp_ref[...] = jnp.sum(acc_ref[...], axis=0, keepdims=True)
            loss_ref[...] = 0.5 * jnp.sum(loss_acc_ref[...], axis=0, keepdims=True)

    row = pl.BlockSpec((tr, d), lambda i: (i, 0))
    vec = pl.BlockSpec((1, d), lambda i: (0, 0))
    one = pl.BlockSpec((1, 1), lambda i: (0, 0))
    return _pcall(name, body, [h, g, target], [row, vec, row],
                  [_sds((s, d), F32), _sds((s, d), BF16), _sds((1, d), F32), _sds((1, 1), F32)],
                  [row, row, vec, one], grid=(steps,), sem=("arbitrary",),
                  scratch=[pltpu.VMEM((ROW_CHUNK, d), F32), pltpu.VMEM((ROW_CHUNK, 1), F32)], after=after)


_SQRT_HALF = math.sqrt(0.5)
_INV_SQRT_2PI = 1.0 / math.sqrt(2.0 * math.pi)


def _gelu(x):
    return 0.5 * x * (1.0 + lax.erf(x * _SQRT_HALF))


def _gelu_grad(x):
    return 0.5 * (1.0 + lax.erf(x * _SQRT_HALF)) + x * jnp.exp(-0.5 * x * x) * _INV_SQRT_2PI


def _causal_mask():
    row = lax.broadcasted_iota(jnp.int32, (CHUNK, CHUNK), 0)
    col = lax.broadcasted_iota(jnp.int32, (CHUNK, CHUNK), 1)
    return row >= col


def _row_sum(x):
    return jnp.sum(x, axis=-1, keepdims=True)


def _masked_spatial(ws_ref, grp):
    return jnp.where(_causal_mask(), ws_ref[grp], 0.0).astype(BF16)


def _layernorm_stats(pre_ref, v_scr, w, head):
    total = jnp.zeros((CHUNK, 1), F32)
    for grp in range(A_GROUPS):
        v = _gelu(pre_ref[:, w + grp * head:w + (grp + 1) * head])
        v_scr[:, grp * head:(grp + 1) * head] = v
        total = total + _row_sum(v)
    mu = total * (1.0 / w)
    square = jnp.zeros((CHUNK, 1), F32)
    for grp in range(A_GROUPS):
        xc = v_scr[:, grp * head:(grp + 1) * head] - mu
        square = square + _row_sum(xc * xc)
    return mu, lax.rsqrt(square * (1.0 / w) + EPS)


def _amix_fwd(name, pre, ln_g, ln_b, w_s, b_s_col, after=None):
    s, w2 = pre.shape
    w = w2 // 2
    head = w // A_GROUPS

    def body(pre_ref, g_ref, b_ref, ws_ref, bs_ref, o_ref, v_scr):
        mu, rstd = _layernorm_stats(pre_ref, v_scr, w, head)
        for grp in range(A_GROUPS):
            cols = slice(grp * head, (grp + 1) * head)
            vhat = (v_scr[:, cols] - mu) * rstd
            vn = (vhat * g_ref[:, cols] + b_ref[:, cols]).astype(BF16)
            sg = jnp.dot(_masked_spatial(ws_ref, grp), vn, preferred_element_type=F32) + bs_ref[grp]
            o_ref[:, cols] = (_gelu(pre_ref[:, cols]) * sg).astype(o_ref.dtype)

    vec = pl.BlockSpec((1, w), lambda i: (0, 0))
    return _pcall(
        name, body, [pre, ln_g, ln_b, w_s, b_s_col],
        [pl.BlockSpec((CHUNK, w2), lambda i: (i, 0)), vec, vec,
         pl.BlockSpec((A_GROUPS, CHUNK, CHUNK), lambda i: (0, 0, 0)),
         pl.BlockSpec((A_GROUPS, CHUNK, 1), lambda i: (0, 0, 0))],
        _sds((s, w), BF16), pl.BlockSpec((CHUNK, w), lambda i: (i, 0)),
        grid=(s // CHUNK,), sem=("parallel",), scratch=[pltpu.VMEM((CHUNK, w), F32)], after=after)


def _amix_bwd(name, pre, dgated, ln_g, ln_b, w_s, b_s_col, after=None):
    s, w2 = pre.shape
    w = w2 // 2
    head = w // A_GROUPS

    def body(pre_ref, dg_ref, g_ref, b_ref, ws_ref, bs_ref, dpre_ref, glg_ref, glb_ref, gws_ref, gbs_ref,
             v_scr, dvn_scr):
        @pl.when(pl.program_id(0) == 0)
        def _():
            for ref in (glg_ref, glb_ref, gws_ref, gbs_ref):
                ref[...] = jnp.zeros_like(ref)

        mu, rstd = _layernorm_stats(pre_ref, v_scr, w, head)
        mask = _causal_mask()
        sum_dvhat = jnp.zeros((CHUNK, 1), F32)
        sum_dvhat_vhat = jnp.zeros((CHUNK, 1), F32)
        for grp in range(A_GROUPS):
            cols = slice(grp * head, (grp + 1) * head)
            vhat = (v_scr[:, cols] - mu) * rstd
            gain = g_ref[:, cols]
            vn = (vhat * gain + b_ref[:, cols]).astype(BF16)
            wm = _masked_spatial(ws_ref, grp)
            pre_u = pre_ref[:, cols]
            dgated = dg_ref[:, cols]
            ds = dgated * _gelu(pre_u)
            dsb = ds.astype(BF16)
            sg = jnp.dot(wm, vn, preferred_element_type=F32) + bs_ref[grp]
            dpre_ref[:, cols] = (dgated * sg * _gelu_grad(pre_u)).astype(dpre_ref.dtype)
            gws = lax.dot_general(dsb, vn, NT, preferred_element_type=F32)
            gws_ref[grp] += jnp.where(mask, gws, 0.0)
            gbs_ref[grp] += _row_sum(ds)
            dvn = lax.dot_general(wm, dsb, TN, preferred_element_type=F32)
            dvn_scr[:, cols] = dvn
            glg_ref[:, cols] += jnp.sum(dvn * vhat, axis=0, keepdims=True)
            glb_ref[:, cols] += jnp.sum(dvn, axis=0, keepdims=True)
            dvhat = dvn * gain
            sum_dvhat = sum_dvhat + _row_sum(dvhat)
            sum_dvhat_vhat = sum_dvhat_vhat + _row_sum(dvhat * vhat)
        mean_dvhat = sum_dvhat * (1.0 / w)
        mean_dvhat_vhat = sum_dvhat_vhat * (1.0 / w)
        for grp in range(A_GROUPS):
            cols = slice(grp * head, (grp + 1) * head)
            vhat = (v_scr[:, cols] - mu) * rstd
            dvhat = dvn_scr[:, cols] * g_ref[:, cols]
            dv = rstd * (dvhat - mean_dvhat - vhat * mean_dvhat_vhat)
            pre_v = pre_ref[:, w + grp * head:w + (grp + 1) * head]
            dpre_ref[:, w + grp * head:w + (grp + 1) * head] = (dv * _gelu_grad(pre_v)).astype(dpre_ref.dtype)

    vec = pl.BlockSpec((1, w), lambda i: (0, 0))
    ws_spec = pl.BlockSpec((A_GROUPS, CHUNK, CHUNK), lambda i: (0, 0, 0))
    bs_spec = pl.BlockSpec((A_GROUPS, CHUNK, 1), lambda i: (0, 0, 0))
    return _pcall(
        name, body, [pre, dgated, ln_g, ln_b, w_s, b_s_col],
        [pl.BlockSpec((CHUNK, w2), lambda i: (i, 0)), pl.BlockSpec((CHUNK, w), lambda i: (i, 0)),
         vec, vec, ws_spec, bs_spec],
        [_sds((s, w2), BF16), _sds((1, w), F32), _sds((1, w), F32),
         _sds((A_GROUPS, CHUNK, CHUNK), F32), _sds((A_GROUPS, CHUNK, 1), F32)],
        [pl.BlockSpec((CHUNK, w2), lambda i: (i, 0)), vec, vec, ws_spec, bs_spec],
        grid=(s // CHUNK,), sem=("arbitrary",),
        scratch=[pltpu.VMEM((CHUNK, w), F32), pltpu.VMEM((CHUNK, w), F32)], after=after)


def _shift_rows(x, k, forward):
    n = x.shape[0]
    row = lax.broadcasted_iota(jnp.int32, x.shape, 0)
    if forward:
        return jnp.where(row >= k, pltpu.roll(x, k, 0), 0.0)
    return jnp.where(row < n - k, pltpu.roll(x, n - k, 0), 0.0)


def _window_sum(x, window, forward):
    k = 1
    while k < window:
        x = x + _shift_rows(x, k, forward)
        k *= 2
    return x


def _pool(name, v, backward, after=None):
    s, w = v.shape
    head = w // B_GROUPS
    lane = _tile(head, 128)

    def body(v_ref, o_ref):
        grp = pl.program_id(0)
        t = lax.broadcasted_iota(jnp.int32, (s, lane), 0)
        for idx, window in enumerate(B_WINDOWS):
            @pl.when(grp == idx)
            def _():
                inv_count = 1.0 / jnp.minimum(t + 1, window).astype(F32)
                for strip in range(head // lane):
                    cols = slice(strip * lane, (strip + 1) * lane)
                    x = v_ref[:, cols]
                    if backward:
                        out = _window_sum(x * inv_count, window, False) - x
                    else:
                        out = _window_sum(x, window, True) * inv_count - x
                    o_ref[:, cols] = out.astype(o_ref.dtype)

    spec = pl.BlockSpec((s, head), lambda g: (0, g))
    return _pcall(name, body, [v], [spec], _sds((s, w), BF16), spec, grid=(B_GROUPS,),
                  sem=("parallel",), after=after)


def _scaled_dx(name, dy, w, scale, mixed, after=None, tm=1024, tn=1024):
    s, n = dy.shape
    k = w.shape[0]
    tm, tn = _tile(s, tm), _tile(k, tn)

    def body(dy_ref, w_ref, sc_ref, mx_ref, o_ref, gs_ref):
        p = lax.dot_general(dy_ref[...], w_ref[...], NT, preferred_element_type=F32)
        o_ref[...] = (p * sc_ref[...]).astype(o_ref.dtype)
        _accumulate(gs_ref, jnp.sum(p * mx_ref[...].astype(F32), axis=0, keepdims=True), pl.program_id(1))

    tile = pl.BlockSpec((tm, tn), lambda j, i: (i, j))
    vec = pl.BlockSpec((1, tn), lambda j, i: (0, j))
    return _pcall(name, body, [dy, w, scale, mixed],
                  [pl.BlockSpec((tm, n), lambda j, i: (i, 0)), pl.BlockSpec((tn, n), lambda j, i: (j, 0)), vec, tile],
                  [_sds((s, k), BF16), _sds((1, k), F32)], [tile, vec],
                  grid=(k // tn, s // tm), sem=("parallel", "arbitrary"), after=after)


def _adamw(w, g, m, v):
    m = ADAM_B1 * m + (1.0 - ADAM_B1) * g
    v = ADAM_B2 * v + (1.0 - ADAM_B2) * (g * g)
    m_hat = m / (1.0 - ADAM_B1 ** ADAM_STEP)
    v_hat = v / (1.0 - ADAM_B2 ** ADAM_STEP)
    delta = -ADAM_LR * (m_hat / (jnp.sqrt(v_hat) + ADAM_EPS) + ADAM_WD * w)
    return delta, m, v


def _adam_rows(name, g, w, m, v, after=None):
    r, c = g.shape
    tr = _tile(r, 256)

    def body(g_ref, w_ref, m_ref, v_ref, d_ref, nm_ref, nv_ref):
        d_ref[...], nm_ref[...], nv_ref[...] = _adamw(w_ref[...], g_ref[...], m_ref[...], v_ref[...])

    spec = pl.BlockSpec((tr, c), lambda i: (i, 0))
    return _pcall(name, body, [g, w, m, v], [spec] * 4, [_sds((r, c), F32)] * 3, [spec] * 3,
                  grid=(r // tr,), sem=("parallel",), after=after)


def _position():
    return lax.axis_index("x"), lax.axis_index("y"), lax.axis_index("c")


def _other_chips(x, y):
    return [(1 - x, y), (x, 1 - y), (1 - x, 1 - y)]


def _slot(px, py, pc):
    return 4 * px + 2 * py + pc


def _hbm(a):
    return pltpu.with_memory_space_constraint(a, pltpu.HBM)


def _hop1_copies(srcs, lands, send_sems, recv_sems):
    x, y, c = _position()
    peers = [(x, y, 1 - c), (1 - x, y, c), (x, 1 - y, c)]
    mine = _slot(x, y, c)
    return [[pltpu.make_async_remote_copy(
        src_ref=srcs[t], dst_ref=lands[t].at[mine], send_sem=send_sems[t].at[k], recv_sem=recv_sems[t].at[k],
        device_id=peer, device_id_type=MESH) for k, peer in enumerate(peers)] for t in range(len(srcs))]


def _hop2_copies(lands, send_sems, recv_sems):
    x, y, c = _position()
    routes = [(_slot(1 - x, y, c), (x, 1 - y, c)), (_slot(x, 1 - y, c), (1 - x, y, c))]
    out = []
    for t in range(len(lands)):
        rows = lands[t].shape[1]
        halves = [(0, rows // 2), (rows // 2, rows - rows // 2)]
        per_tensor = []
        for h, ((slot, peer), (start, size)) in enumerate(zip(routes, halves)):
            if size:
                block = lands[t].at[slot, pl.ds(start, size)]
                per_tensor.append(pltpu.make_async_remote_copy(
                    src_ref=block, dst_ref=block, send_sem=send_sems[t].at[h], recv_sem=recv_sems[t].at[h],
                    device_id=peer, device_id_type=MESH))
        for j, (slot, _) in enumerate(routes):
            block = lands[t].at[slot]
            per_tensor.append(pltpu.make_async_remote_copy(
                src_ref=block, dst_ref=block, send_sem=send_sems[t].at[2 + j], recv_sem=recv_sems[t].at[2 + j],
                device_id=(x, y, 1 - c), device_id_type=MESH))
        out.append(per_tensor)
    return out


def _split_start(name, srcs, lands, copies, n_sems, after=None):
    n = len(srcs)
    order = [] if after is None else [after]
    n_in = 2 * n + len(order)

    def body(*refs):
        for per_tensor in copies(refs[:n], refs[n:2 * n], refs[n_in:n_in + n], refs[n_in + n:n_in + 2 * n]):
            for cp in per_tensor:
                cp.start()
        refs[-1][...] = jnp.zeros_like(refs[-1])

    out_shape = ([pltpu.SemaphoreType.DMA((n_sems,)) for _ in range(2 * n)]
                 + [pltpu.HBM(a.shape, a.dtype) for a in list(srcs) + list(lands)]
                 + [_sds((8, 128), F32)])
    out = pl.pallas_call(
        body, name=name, out_shape=out_shape, in_specs=[_HBM] * (2 * n) + [_ANY] * len(order),
        out_specs=[_SEM] * (2 * n) + [_HBM] * (2 * n) + [pl.BlockSpec(memory_space=pltpu.VMEM)],
        input_output_aliases={i: 2 * n + i for i in range(2 * n)},
        compiler_params=pltpu.CompilerParams(has_side_effects=_EFFECT),
    )(*[_hbm(a) for a in srcs], *[_hbm(a) for a in lands], *order)
    return [(out[t], out[n + t], out[2 * n + t], out[3 * n + t]) for t in range(n)], out[-1]


def _split_wait(name, started, copies, after):
    n = len(started)

    def body(*refs):
        for per_tensor in copies(refs[:n], refs[n:2 * n], refs[2 * n:3 * n], refs[3 * n:4 * n]):
            for cp in per_tensor:
                cp.wait_send()
                cp.wait_recv()

    srcs = [e[2] for e in started]
    lands = [e[3] for e in started]
    out = pl.pallas_call(
        body, name=name, out_shape=[pltpu.HBM(a.shape, a.dtype) for a in srcs + lands],
        in_specs=[_HBM] * (2 * n) + [_SEM] * (2 * n) + [_ANY], out_specs=[_HBM] * (2 * n),
        input_output_aliases={i: i for i in range(2 * n)},
        compiler_params=pltpu.CompilerParams(has_side_effects=_EFFECT),
    )(*srcs, *lands, *[e[0] for e in started], *[e[1] for e in started], after)
    return out[:n], out[n:]


def _gather_step(name, arrived, fresh, after=None):
    n, m = len(arrived), len(fresh)
    order = [] if after is None else [after]
    fresh_lands = [lax.empty((N_DEV,) + s.shape, s.dtype) for s in fresh]
    buffers = [e[2] for e in arrived] + [e[3] for e in arrived] + list(fresh) + fresh_lands
    old_sems = [e[0] for e in arrived] + [e[1] for e in arrived]
    n_buf, n_old = len(buffers), len(old_sems)
    first_new = n_buf + n_old + len(order)

    def body(*refs):
        bufs, old = refs[:n_buf], refs[n_buf:n_buf + n_old]
        new = refs[first_new:first_new + 2 * n + 2 * m]
        for per_tensor in _hop1_copies(bufs[:n], bufs[n:2 * n], old[:n], old[n:]):
            for cp in per_tensor:
                cp.wait_send()
                cp.wait_recv()
        second = _hop2_copies(bufs[n:2 * n], new[:n], new[n:2 * n])
        first = _hop1_copies(bufs[2 * n:2 * n + m], bufs[2 * n + m:], new[2 * n:2 * n + m], new[2 * n + m:])
        for per_tensor in second + first:
            for cp in per_tensor:
                cp.start()
        refs[-1][...] = jnp.zeros_like(refs[-1])

    n_new = 2 * n + 2 * m
    out_shape = ([pltpu.SemaphoreType.DMA((4,)) for _ in range(2 * n)]
                 + [pltpu.SemaphoreType.DMA((3,)) for _ in range(2 * m)]
                 + [pltpu.HBM(a.shape, a.dtype) for a in buffers] + [_sds((8, 128), F32)])
    out = pl.pallas_call(
        body, name=name, out_shape=out_shape,
        in_specs=[_HBM] * n_buf + [_SEM] * n_old + [_ANY] * len(order),
        out_specs=[_SEM] * n_new + [_HBM] * n_buf + [pl.BlockSpec(memory_space=pltpu.VMEM)],
        input_output_aliases={i: n_new + i for i in range(n_buf)},
        compiler_params=pltpu.CompilerParams(has_side_effects=_EFFECT),
    )(*[_hbm(a) for a in buffers], *old_sems, *order)
    sems, bufs = out[:n_new], out[n_new:n_new + n_buf]
    second = [(sems[t], sems[n + t], bufs[t], bufs[n + t]) for t in range(n)]
    first = [(sems[2 * n + t], sems[2 * n + m + t], bufs[2 * n + t], bufs[2 * n + m + t]) for t in range(m)]
    return second, first, out[-1]


def _gather_wait(name, second, after):
    return _split_wait(name, second, lambda srcs, lands, send, recv: _hop2_copies(lands, send, recv), after)


def _sibling_copies(srcs, lands, send_sems, recv_sems):
    x, y, c = _position()
    return [[pltpu.make_async_remote_copy(
        src_ref=srcs[t], dst_ref=lands[t], send_sem=send_sems[t].at[0], recv_sem=recv_sems[t].at[0],
        device_id=(x, y, 1 - c), device_id_type=MESH)] for t in range(len(srcs))]


def _sibling_and_scatter_start(name, arrays, partials):
    k = len(arrays)

    def copies(srcs, lands, send_sems, recv_sems):
        return (_sibling_copies(srcs[:k], lands[:k], send_sems[:k], recv_sems[:k])
                + _scatter_copies(srcs[k:], lands[k:], send_sems[k:], recv_sems[k:]))

    lands = ([lax.empty(a.shape, a.dtype) for a in arrays]
             + [lax.empty((N_CHIPS - 1,) + p.shape[1:], p.dtype) for p in partials])
    return _split_start(name, list(arrays) + list(partials), lands, copies, 3)


def _sibling_wait(name, started, after):
    return _split_wait(name, started, _sibling_copies, after)[1]


def _small_copies(srcs, lands, send_sems, recv_sems):
    x, y, c = _position()
    mine = _slot(x, y, c)
    peers = [(x ^ ((k >> 2) & 1), y ^ ((k >> 1) & 1), c ^ (k & 1)) for k in range(1, N_DEV)]
    return [[pltpu.make_async_remote_copy(
        src_ref=srcs[t], dst_ref=lands[t].at[mine], send_sem=send_sems[t].at[k], recv_sem=recv_sems[t].at[k],
        device_id=peer, device_id_type=MESH) for k, peer in enumerate(peers)] for t in range(len(srcs))]


def _gather_finish(name, shards, lands, after):
    n = len(shards)

    def body(*refs):
        srcs, lands_in, outs = refs[:n], refs[n:2 * n], refs[2 * n:3 * n]
        send_sems, recv_sems, local_sems = refs[3 * n:]
        x, y, c = _position()
        local = [pltpu.make_async_copy(srcs[t], outs[t].at[_slot(x, y, c)], local_sems.at[t]) for t in range(n)]

        def diagonal(t, core):
            block = outs[t].at[_slot(1 - x, 1 - y, core)]
            return pltpu.make_async_remote_copy(
                src_ref=block, dst_ref=block, send_sem=send_sems.at[t], recv_sem=recv_sems.at[t],
                device_id=(x, y, 1 - c), device_id_type=MESH)

        for cp in local:
            cp.start()
        for t in range(n):
            diagonal(t, c).start()
        for t in range(n):
            diagonal(t, c).wait_send()
            diagonal(t, 1 - c).wait_recv()
        for cp in local:
            cp.wait()

    return _pcall(name, body, [*shards, *lands], [_ANY] * (2 * n),
                  [_sds(l.shape, l.dtype) for l in lands], [_ANY] * n,
                  scratch=[pltpu.SemaphoreType.DMA((n,)), pltpu.SemaphoreType.DMA((n,)),
                           pltpu.SemaphoreType.DMA((n,))],
                  after=after, aliases={n + t: t for t in range(n)})


def _exchange_sibling(name, fulls, after):
    n = len(fulls)

    def body(*refs):
        src = refs[:n]
        out = refs[n:2 * n]
        send_sems, recv_sems = refs[2 * n:]
        x, y, c = _position()
        copies = [pltpu.make_async_remote_copy(
            src_ref=src[t].at[:, 1 - c], dst_ref=out[t], send_sem=send_sems.at[t], recv_sem=recv_sems.at[t],
            device_id=(x, y, 1 - c), device_id_type=MESH) for t in range(n)]
        for cp in copies:
            cp.start()
        for cp in copies:
            cp.wait()

    return _pcall(name, body, fulls, [_ANY] * n, [_sds((N_CHIPS,) + f.shape[2:], f.dtype) for f in fulls],
                  [_ANY] * n, scratch=[pltpu.SemaphoreType.DMA((n,)), pltpu.SemaphoreType.DMA((n,))],
                  after=after)


def _add_sibling(name, full, recv, core, after):
    _, _, r, c = full.shape
    tr = _tile(r, max(8, (256 * 1024) // c))

    def body(core_ref, f_ref, r_ref, o_ref):
        o_ref[...] = (f_ref[...].astype(F32) + r_ref[...].astype(F32)).astype(o_ref.dtype)

    return _pcall(
        name, body, [full, recv],
        [pl.BlockSpec((None, None, tr, c), lambda p, i, core_ref: (p, core_ref[0], i, 0)),
         pl.BlockSpec((None, tr, c), lambda p, i, core_ref: (p, i, 0))],
        _sds((N_CHIPS, r, c), BF16), pl.BlockSpec((None, tr, c), lambda p, i, core_ref: (p, i, 0)),
        grid=(N_CHIPS, r // tr), sem=("parallel", "parallel"), prefetch=[core], after=after)


def _scatter_copies(srcs, lands, send_sems, recv_sems):
    x, y, c = _position()
    return [[pltpu.make_async_remote_copy(
        src_ref=srcs[t].at[2 * px + py], dst_ref=lands[t].at[j],
        send_sem=send_sems[t].at[j], recv_sem=recv_sems[t].at[j],
        device_id=(px, py, c), device_id_type=MESH) for j, (px, py) in enumerate(_other_chips(x, y))]
        for t in range(len(srcs))]


def _scatter_wait(name, started, after):
    return _split_wait(name, started, _scatter_copies, after)


class _Job:
    def __init__(self, **fields):
        self.__dict__.update(fields)


def _reduce_adam_job(partial, recv, chip, w, m, v, layer, carried):
    n_layers, r, c = w.shape
    tr = _tile(r, max(8, (256 * 1024) // c))

    def body(p_ref, r_ref, w_ref, m_ref, v_ref, *rest):
        g_ref, d_ref, nm_ref, nv_ref = rest[-4:]
        g = p_ref[...].astype(F32)
        for j in range(N_CHIPS - 1):
            g = g + r_ref[j].astype(F32)
        g_ref[...] = g
        d_ref[...], nm_ref[...], nv_ref[...] = _adamw(w_ref[...], g, m_ref[...], v_ref[...])

    layered = ((None, tr, c), lambda blk, chip_ref: (layer, blk, 0))
    in_specs = [((None, tr, c), lambda blk, chip_ref: (chip_ref[0], blk, 0)),
                ((N_CHIPS - 1, tr, c), lambda blk, chip_ref: (0, blk, 0)), layered, layered, layered]
    operands = [partial, recv, w, m, v]
    aliases = {}
    if carried is not None:
        operands += list(carried)
        in_specs += [None] * 4
        aliases = {5 + o: o for o in range(4)}
    return _Job(operands=operands, in_specs=in_specs, out_shape=[_sds((n_layers, r, c), F32)] * 4,
                out_specs=[layered] * 4, body=body, aliases=aliases, prefetch=chip, n_blocks=r // tr)


def _run_job(name, job, after):
    def spec(entry):
        if entry is None:
            return _ANY
        shape, index = entry
        return pl.BlockSpec(shape, lambda blk, pre, index=index: index(blk, pre))

    def body(pre_ref, *refs):
        job.body(*refs)

    return _pcall(name, body, job.operands, [spec(e) for e in job.in_specs], job.out_shape,
                  [spec(e) for e in job.out_specs], grid=(job.n_blocks,), sem=("parallel",),
                  prefetch=[job.prefetch], after=after,
                  aliases={1 + i: o for i, o in job.aliases.items()})


def _small_sum(name, gathered, own, device, after=None):
    r, lanes = own.shape

    def body(dev_ref, g_ref, own_ref, out_ref):
        dev = dev_ref[0]
        mine = own_ref[...]
        total = jnp.where(dev == 0, mine, g_ref[0])
        for d in range(1, N_DEV):
            total = total + jnp.where(dev == d, mine, g_ref[d])
        out_ref[...] = total

    return _pcall(name, body, [gathered, own],
                  [pl.BlockSpec((N_DEV, r, lanes), lambda i, dev_ref: (0, 0, 0)),
                   pl.BlockSpec((r, lanes), lambda i, dev_ref: (0, 0))],
                  _sds((r, lanes), F32), pl.BlockSpec((r, lanes), lambda i, dev_ref: (0, 0)),
                  grid=(1,), sem=("arbitrary",), prefetch=[device], after=after)


def _pack(arrays):
    return jnp.concatenate([a.reshape(-1, 128) for a in arrays], axis=0)


def _unpack(packed, shapes):
    out, row = [], 0
    for shape in shapes:
        rows = math.prod(shape) // 128
        out.append(packed[row:row + rows].reshape(shape))
        row += rows
    return out


class _Order:
    def __init__(self):
        self.last = None

    def __call__(self, fn, *args, **kwargs):
        out = fn(*args, after=self.last, **kwargs)
        self.last = out[0] if isinstance(out, (list, tuple)) else out
        return out


def kernel(x, a_w_in, a_ln_g, a_ln_b, a_w_s, a_b_s, a_w_out, b_w_in, b_w_grp, b_scale, b_w_out, norm_mix, norm_mlp, mlp_w1, mlp_w2, final_norm, loss_target, m_a_w_in, m_a_ln_g, m_a_ln_b, m_a_w_s, m_a_b_s, m_a_w_out, m_b_w_in, m_b_w_grp, m_b_scale, m_b_w_out, m_norm_mix, m_norm_mlp, m_mlp_w1, m_mlp_w2, m_final_norm, v_a_w_in, v_a_ln_g, v_a_ln_b, v_a_w_s, v_a_b_s, v_a_w_out, v_b_w_in, v_b_w_grp, v_b_scale, v_b_w_out, v_norm_mix, v_norm_mlp, v_mlp_w1, v_mlp_w2, v_final_norm):
    s, d = x.shape[1], x.shape[2]
    depth = mlp_w1.shape[0]
    a_slab = a_w_in.shape[2]
    ff_slab = mlp_w1.shape[2]
    ff_rows = mlp_w2.shape[1]
    bh = b_w_grp.shape[3]
    my_x, my_y, my_c = _position()
    core = jnp.reshape(my_c, (1,)).astype(jnp.int32)
    chip = jnp.reshape(2 * my_x + my_y, (1,)).astype(jnp.int32)
    device = _slot(my_x, my_y, my_c)
    run = _Order()

    w1_b, w2_b = mlp_w1.astype(BF16), mlp_w2.astype(BF16)
    shards = [a_w_in[0].astype(BF16), a_w_out[0].astype(BF16), b_scale,
              w1_b[0], w2_b[0],
              b_w_in[0].astype(BF16), b_w_grp[0].astype(BF16), b_w_out[0].astype(BF16),
              w1_b[1], w2_b[1]]
    groups = [[0], [1, 2], [3], [4], [5, 6, 7], [8], [9]]
    start_with = {1: [2, 3], 2: [4], 3: [5], 4: [6]}
    hop1, hop2 = {}, {}
    _, hop1[0], token = _gather_step("weights_group0_hop1", [], [shards[t] for t in groups[0]])
    _, hop1[1], token = _gather_step("weights_group1_hop1", [], [shards[t] for t in groups[1]], token)
    run.last = token

    def advance(g):
        if g not in hop1:
            return
        ahead = start_with.get(g, [])
        fresh = [shards[t] for a in ahead for t in groups[a]]
        hop2[g], started, tok = _gather_step(f"weights_group{g}_hop2", hop1.pop(g), fresh, run.last)
        for a in ahead:
            hop1[a], started = started[:len(groups[a])], started[len(groups[a]):]
        run.last = tok

    def gathered(g):
        advance(g)
        if g == 0:
            advance(1)
        srcs, lands = _gather_wait(f"weights_group{g}_wait", hop2.pop(g), run.last)
        run.last = srcs[0]
        return run(_gather_finish, f"weights_group{g}_finish", srcs, lands)

    h0 = x[0]
    target = loss_target[0]
    ln_g, ln_b = a_ln_g, a_ln_b
    w_s = a_w_s[0]
    b_s_col = a_b_s[0][:, :, None]
    nmix = [norm_mix[l][None, :] for l in range(depth)]
    nmlp = [norm_mlp[l][None, :] for l in range(depth)]

    def mlp_forward(l, h, up_group):
        hn = run(_rms_fwd, f"mlp{l}_norm", h, nmlp[l])
        (w1,) = gathered(up_group)
        advance(up_group + 1)
        act, act_sq = run(_mm_nn, f"mlp{l}_up", hn, w1,
                          lambda acc: (jnp.maximum(acc, 0.0), jnp.square(jnp.maximum(acc, 0.0))),
                          (BF16, BF16), slab=True)
        (w2,) = gathered(up_group + 1)
        advance(up_group + 2)
        w2 = w2.reshape(-1, d)
        (h_out,) = run(_mm_nn, f"mlp{l}_down", act_sq, w2, lambda acc, res: (acc + res,), (F32,),
                       extras=(h,), extra_kinds=("tile",))
        return h_out, (h, hn, act, act_sq, w1, w2)

    scattered = []

    pending = []

    def scatter_partials(name, partials, specs):
        pending.append((name, partials, specs))

    def start_exchanges(name, to_sibling):
        partials = [p for _, group, _ in pending for p in group]
        in_flight, tok = _sibling_and_scatter_start(name, to_sibling, partials)
        run.last = tok
        first = len(to_sibling)
        for group_name, group, specs in pending:
            scattered.append((group_name, in_flight[first:first + len(group)], specs))
            first += len(group)
        pending.clear()
        return in_flight[:len(to_sibling)]

    weights = {"a_w_in": (a_w_in, m_a_w_in, v_a_w_in), "a_w_out": (a_w_out, m_a_w_out, v_a_w_out),
               "b_w_in": (b_w_in, m_b_w_in, v_b_w_in), "b_w_grp": (b_w_grp, m_b_w_grp, v_b_w_grp),
               "b_w_out": (b_w_out, m_b_w_out, v_b_w_out), "mlp_w1": (mlp_w1, m_mlp_w1, v_mlp_w1),
               "mlp_w2": (mlp_w2, m_mlp_w2, v_mlp_w2)}
    results = {}

    def finish_group(name, in_flight, specs):
        partials, lands = _scatter_wait(name + "_scatter_wait", in_flight, run.last)
        run.last = lands[0]
        for t, (wname, layer) in enumerate(specs):
            w, m, v = weights[wname]
            shape = (w.shape[0],) + partials[t].shape[1:]
            job = _reduce_adam_job(partials[t], lands[t], chip, w.reshape(shape), m.reshape(shape),
                                   v.reshape(shape), layer, results.get(wname))
            results[wname] = run(_run_job, f"{name}_reduce_adam_{t}", job)

    def weight_grad(name, a, b, by_rows, block, between, tm=1024, tn=1024):
        other = run(_mm_tn_half, name + "_other", a, b, core, False, by_rows, block, tm=tm, tn=tn)
        sent = start_exchanges(name + "_sibling_start", [other])
        middle = between()
        (recv,) = _sibling_wait(name + "_sibling_wait", sent, run.last)
        run.last = recv
        return run(_mm_tn_half, name + "_own", a, b, core, True, by_rows, block, recv=recv, tm=tm, tn=tn), middle

    def mlp_backward(l, saved, dh, dhb):
        h, hn, act, act_sq, w1, w2 = saved
        part_w2, (dpre,) = weight_grad(
            f"mlp{l}_down_dw", act_sq, dhb, True, ff_rows,
            lambda: run(_mm_nt, f"mlp{l}_down_dx", dhb, w2, lambda acc, a: (2.0 * a.astype(F32) * acc,),
                        (BF16,), extras=(act,), extra_kinds=("tile",)))
        scatter_partials(f"mlp{l}_down_grads", [part_w2], [("mlp_w2", l)])
        part_w1, (dhn,) = weight_grad(
            f"mlp{l}_up_dw", hn, dpre, False, ff_slab,
            lambda: run(_mm_nt, f"mlp{l}_up_dx", dpre, w1, lambda acc: (acc,), (F32,), slab=True))
        scatter_partials(f"mlp{l}_up_grads", [part_w1], [("mlp_w1", l)])
        dh, dhb, g_norm = run(_rms_bwd, f"mlp{l}_norm_bwd", dhn, h, nmlp[l], dh)
        return dh, dhb, g_norm

    hn0 = run(_rms_fwd, "mix0_norm", h0, nmix[0])
    (wa_in,) = gathered(0)
    (pre,) = run(_mm_nn, "mixa_in", hn0, wa_in, lambda acc: (acc,), (F32,), slab=True)
    wa_out, scale = gathered(1)
    wa_out, scale = wa_out.reshape(d, d), scale.reshape(1, d)
    gated = run(_amix_fwd, "mixa_gate", pre, ln_g, ln_b, w_s, b_s_col)
    advance(2)
    (h1,) = run(_mm_nn, "mixa_out", gated, wa_out, lambda acc, res: (acc + res,), (F32,),
                extras=(h0,), extra_kinds=("tile",))
    h2, saved_mlp0 = mlp_forward(0, h1, 2)
    hn2 = run(_rms_fwd, "mix1_norm", h2, nmix[1])
    wb_in, wb_grp, wb_out = gathered(4)
    advance(5)
    wb_in, wb_out = wb_in.reshape(d, d), wb_out.reshape(d, d)
    wb_grp = jnp.transpose(wb_grp, (1, 0, 2, 3)).reshape(B_GROUPS, bh, bh)
    (vb,) = run(_mm_nn, "mixb_in", hn2, wb_in, lambda acc: (acc,), (F32,))
    pooled = run(_pool, "mixb_pool", vb, backward=False)
    tm = _tile(s, 1024)
    grp_tile = pl.BlockSpec((tm, bh), lambda i, j, k: (i, j))
    grp_weight = pl.BlockSpec((None, bh, bh), lambda i, j, k: (j, 0, 0))
    mixed, mixed_scaled = run(
        _matmul, "mixb_grp", pooled, wb_grp, NN, (s // tm, B_GROUPS, 1), grp_tile, grp_weight,
        [_sds((s, d), BF16), _sds((s, d), BF16)], [grp_tile] * 2,
        (tm, bh), lambda acc, sc: (acc, acc * sc), (scale,), [pl.BlockSpec((1, bh), lambda i, j, k: (0, j))])
    (h3,) = run(_mm_nn, "mixb_out", mixed_scaled, wb_out, lambda acc, res: (acc + res,), (F32,),
                extras=(h2,), extra_kinds=("tile",))
    h4, saved_mlp1 = mlp_forward(1, h3, 5)
    dh, dhb, g_final, loss_part = run(_loss_head, "loss_head", h4, final_norm[None, :], target)

    dh, dhb, g_nmlp1 = mlp_backward(1, saved_mlp1, dh, dhb)
    tks = _tile(s, 1024)
    grp_rows = pl.BlockSpec((tks, bh), lambda i, j, k: (k, j))

    def mixb_middle():
        dms_scaled, g_scale = run(_scaled_dx, "mixb_out_dx", dhb, wb_out, scale, mixed)
        (g_wb_grp,) = run(
            _matmul, "mixb_grp_dw", pooled, dms_scaled, TN, (1, B_GROUPS, s // tks), grp_rows, grp_rows,
            [_sds((B_GROUPS, bh, bh), BF16)], [grp_weight], (bh, bh), lambda acc: (acc,))
        (dpooled,) = run(
            _matmul, "mixb_grp_dx", dms_scaled, wb_grp, NT, (s // tm, B_GROUPS, 1), grp_tile, grp_weight,
            [_sds((s, d), F32)], [grp_tile], (tm, bh), lambda acc: (acc,))
        return g_scale, g_wb_grp, run(_pool, "mixb_pool_bwd", dpooled, backward=True)

    part_wb_out, (g_scale, g_wb_grp, dvb) = weight_grad("mixb_out_dw", mixed_scaled, dhb, True, d // N_DEV,
                                                        mixb_middle, tn=d)
    part_wb_in, (dhn2,) = weight_grad(
        "mixb_in_dw", hn2, dvb, True, d // N_DEV,
        lambda: run(_mm_nt, "mixb_in_dx", dvb, wb_in, lambda acc: (acc,), (F32,)), tn=d)
    grp_full = jnp.transpose(g_wb_grp.reshape(B_GROUPS, N_DEV, bh // N_DEV, bh), (1, 0, 2, 3))
    grp_full = grp_full.reshape(N_CHIPS, 2, B_GROUPS * bh // N_DEV, bh)
    (grp_sibling,) = run(_exchange_sibling, "mixb_grp_dw_to_sibling", [grp_full])
    part_wb_grp = run(_add_sibling, "mixb_grp_dw_add_sibling", grp_full, grp_sibling, core)
    scatter_partials("mixb_grads", [part_wb_out, part_wb_grp, part_wb_in],
                     [("b_w_out", 0), ("b_w_grp", 0), ("b_w_in", 0)])
    dh, dhb, g_nmix1 = run(_rms_bwd, "mix1_norm_bwd", dhn2, h2, nmix[1], dh)
    dh, dhb, g_nmlp0 = mlp_backward(0, saved_mlp0, dh, dhb)
    def mixa_middle():
        (dgated,) = run(_mm_nt, "mixa_out_dx", dhb, wa_out, lambda acc: (acc,), (F32,))
        return run(_amix_bwd, "mixa_gate_bwd", pre, dgated, ln_g, ln_b, w_s, b_s_col)

    part_wa_out, (dpre, g_ln_g, g_ln_b, g_w_s, g_b_s) = weight_grad("mixa_out_dw", gated, dhb, True, d // N_DEV,
                                                                     mixa_middle, tn=d)
    part_wa_in, (dhn0,) = weight_grad(
        "mixa_in_dw", hn0, dpre, False, a_slab,
        lambda: run(_mm_nt, "mixa_in_dx", dpre, wa_in, lambda acc: (acc,), (F32,), slab=True), tm=d)
    scatter_partials("mixa_grads", [part_wa_in, part_wa_out], [("a_w_in", 0), ("a_w_out", 0)])
    start_exchanges("mixa_grads_scatter_start", [])
    grad_x, _, g_nmix0 = run(_rms_bwd, "mix0_norm_bwd", dhn0, h0, nmix[0], dh)

    g_norm_mix = jnp.concatenate([g_nmix0, g_nmix1], axis=0)
    g_norm_mlp = jnp.concatenate([g_nmlp0, g_nmlp1], axis=0)
    loss_row = jnp.pad(loss_part, ((0, 0), (0, 127)))
    small_parts = [g_ln_g, g_ln_b, g_w_s, g_b_s, g_norm_mix, g_norm_mlp, g_final, g_scale, loss_row]
    packed = _pack(small_parts)
    small_sent, tok = _split_start("small_grads_start", [packed], [lax.empty((N_DEV,) + packed.shape, F32)],
                                   _small_copies, N_DEV - 1)
    run.last = tok

    for group in scattered:
        finish_group(*group)
    own_packed, small_gathered = _split_wait("small_grads_wait", small_sent, _small_copies, run.last)
    run.last = small_gathered[0]
    small_sum = run(_small_sum, "small_grads_sum", small_gathered[0], own_packed[0],
                    jnp.reshape(device, (1,)).astype(jnp.int32))
    sg = _unpack(small_sum, [a_ln_g.shape, a_ln_b.shape, a_w_s.shape, a_b_s.shape, norm_mix.shape,
                             norm_mlp.shape, final_norm.shape, (1, d), (1, 128)])
    loss = sg.pop()[0, 0]
    shard = b_scale.shape[1]
    sg[7] = lax.dynamic_slice(sg[7], (0, device * shard), (1, shard))
    small_w = [a_ln_g, a_ln_b, a_w_s, a_b_s, norm_mix, norm_mlp, final_norm, b_scale]
    small_m = [m_a_ln_g, m_a_ln_b, m_a_w_s, m_a_b_s, m_norm_mix, m_norm_mlp, m_final_norm, m_b_scale]
    small_v = [v_a_ln_g, v_a_ln_b, v_a_w_s, v_a_b_s, v_norm_mix, v_norm_mlp, v_final_norm, v_b_scale]
    small_out = run(_adam_rows, "small_adam", _pack(sg), _pack(small_w), _pack(small_m), _pack(small_v))
    shapes = [w.shape for w in small_w]
    small_res = [sg] + [_unpack(o, shapes) for o in small_out]
    big = {wname: [o.reshape(weights[wname][0].shape) for o in outs] for wname, outs in results.items()}

    def leaf(o):
        return (big["a_w_in"][o], small_res[o][0], small_res[o][1], small_res[o][2], small_res[o][3],
                big["a_w_out"][o], big["b_w_in"][o], big["b_w_grp"][o], small_res[o][7], big["b_w_out"][o],
                small_res[o][4], small_res[o][5], big["mlp_w1"][o], big["mlp_w2"][o], small_res[o][6])

    return (loss, grad_x[None], *leaf(0), *leaf(1), *leaf(2), *leaf(3))
```

```python
import math

import jax
import jax.numpy as jnp
from jax import lax
from jax.experimental import pallas as pl
from jax.experimental.pallas import tpu as pltpu

F32 = jnp.float32
BF16 = jnp.bfloat16
MESH = pl.DeviceIdType.MESH

N_DEV = 8
N_CHIPS = 4
CHUNK = 128
A_GROUPS = 8
B_WINDOWS = (2, 4, 8, 16)
B_GROUPS = len(B_WINDOWS)
EPS = 1e-6
ADAM_LR = 0.001
ADAM_B1 = 0.9
ADAM_B2 = 0.999
ADAM_EPS = 1e-08
ADAM_WD = 0.01
ADAM_STEP = 10

VMEM_LIMIT = 48 * 1024 * 1024
ROW_CHUNK = 16

NN = (((1,), (0,)), ((), ()))
NT = (((1,), (1,)), ((), ()))
TN = (((0,), (0,)), ((), ()))

_ANY = pl.BlockSpec(memory_space=pl.ANY)
_HBM = pl.BlockSpec(memory_space=pltpu.HBM)
_SEM = pl.BlockSpec(memory_space=pltpu.SEMAPHORE)
_EFFECT = pltpu.SideEffectType.DATAFLOW_SIDE_EFFECTING


def _tile(n, pref):
    return pref if n % pref == 0 else n


def _sds(shape, dtype):
    return jax.ShapeDtypeStruct(shape, dtype)


def _pcall(name, body, operands, in_specs, out_shape, out_specs, *, grid=None, sem=None, scratch=(),
           prefetch=(), after=None, aliases=None):
    after = [] if after is None else [after]
    n_lead = len(prefetch) + len(operands)
    n_after = len(after)

    def wrapped(*refs):
        body(*refs[:n_lead], *refs[n_lead + n_after:])

    in_specs = list(in_specs) + [_ANY] * n_after
    params = pltpu.CompilerParams(vmem_limit_bytes=VMEM_LIMIT) if sem is None else \
        pltpu.CompilerParams(dimension_semantics=sem, vmem_limit_bytes=VMEM_LIMIT)
    kwargs = dict(out_shape=out_shape, scratch_shapes=list(scratch), compiler_params=params, name=name,
                  input_output_aliases=aliases or {})
    if prefetch:
        kwargs["grid_spec"] = pltpu.PrefetchScalarGridSpec(
            num_scalar_prefetch=len(prefetch), grid=grid, in_specs=in_specs, out_specs=out_specs,
            scratch_shapes=list(scratch))
        kwargs.pop("scratch_shapes")
    else:
        kwargs.update(in_specs=in_specs, out_specs=out_specs)
        if grid is not None:
            kwargs["grid"] = grid
    return pl.pallas_call(wrapped, **kwargs)(*prefetch, *operands, *after)


def _matmul(name, a, b, dims, grid, a_spec, b_spec, out_shape, out_specs, acc_shape,
            epilogue, extras=(), extra_specs=(), after=None, prefetch=(), b_parts=1):
    nk = grid[2]
    n_extra = len(extras)
    n_out = len(out_shape)
    n_pre = len(prefetch)

    def body(*refs):
        refs = refs[n_pre:]
        a_ref, b_ref = refs[0], refs[1]
        extra_refs = refs[2:2 + n_extra]
        out_refs = refs[2 + n_extra:2 + n_extra + n_out]

        def finish(acc):
            outs = epilogue(acc, *[r[...] for r in extra_refs])
            for o_ref, o in zip(out_refs, outs):
                o_ref[...] = o.astype(o_ref.dtype)

        def product():
            if b_parts == 1:
                return lax.dot_general(a_ref[...], b_ref[...], dims, preferred_element_type=F32)
            width = b_ref.shape[2]
            total = None
            for p in range(b_parts):
                part = lax.dot_general(a_ref[:, p * width:(p + 1) * width], b_ref[p], dims,
                                       preferred_element_type=F32)
                total = part if total is None else total + part
            return total

        if nk == 1:
            finish(product())
        else:
            acc_ref = refs[-1]
            k = pl.program_id(2)

            @pl.when(k == 0)
            def _():
                acc_ref[...] = product()

            if nk > 2:
                @pl.when(jnp.logical_and(k > 0, k < nk - 1))
                def _():
                    acc_ref[...] += product()

            @pl.when(k == nk - 1)
            def _():
                finish(acc_ref[...] + product())

    scratch = [] if nk == 1 else [pltpu.VMEM(acc_shape, F32)]
    return _pcall(name, body, [a, b, *extras], [a_spec, b_spec, *extra_specs], out_shape, out_specs,
                  grid=grid, sem=("parallel", "parallel", "arbitrary"), scratch=scratch, after=after,
                  prefetch=prefetch)


def _mm_nn(name, a, b, epilogue, out_dtypes, extras=(), extra_kinds=(), slab=False, after=None,
           tm=1024, tn=1024, tk=2048):
    m, kd = a.shape
    if slab:
        n_slab, _, w = b.shape
        n = n_slab * w
        tn = _tile(w, min(tn, w))
        per = w // tn
        tk = _tile(kd, tk)
        b_spec = pl.BlockSpec((None, tk, tn), lambda i, j, k: (j // per, k, j % per))
    else:
        n = b.shape[1]
        tn = _tile(n, tn)
        tk = _tile(kd, tk)
        b_spec = pl.BlockSpec((tk, tn), lambda i, j, k: (k, j))
    tm = _tile(m, tm)
    grid = (m // tm, n // tn, kd // tk)
    a_spec = pl.BlockSpec((tm, tk), lambda i, j, k: (i, k))
    tile_spec = pl.BlockSpec((tm, tn), lambda i, j, k: (i, j))
    row_spec = pl.BlockSpec((1, tn), lambda i, j, k: (0, j))
    extra_specs = [tile_spec if kind == "tile" else row_spec for kind in extra_kinds]
    return _matmul(name, a, b, NN, grid, a_spec, b_spec,
                   [_sds((m, n), d) for d in out_dtypes], [tile_spec for _ in out_dtypes],
                   (tm, tn), epilogue, extras, extra_specs, after=after)


def _mm_nt(name, a, b, epilogue, out_dtypes, extras=(), extra_kinds=(), slab=False, after=None,
           tm=1024, tn=1024, tk=2048):
    m, kd = a.shape
    parts = 1
    if slab:
        n_slab, n, w = b.shape
        tn = _tile(n, tn)
        if tk > w and tk % w == 0 and n_slab % (tk // w) == 0:
            parts = tk // w
            b_spec = pl.BlockSpec((parts, tn, w), lambda i, j, k, *_: (k, j, 0))
        else:
            tk = _tile(w, min(tk, w))
            per = w // tk
            b_spec = pl.BlockSpec((None, tn, tk), lambda i, j, k, *_: (k // per, j, k % per))
    else:
        n = b.shape[0]
        tn = _tile(n, tn)
        tk = _tile(kd, tk)
        b_spec = pl.BlockSpec((tn, tk), lambda i, j, k, *_: (j, k))
    tm = _tile(m, tm)
    grid = (m // tm, n // tn, kd // tk)
    a_spec = pl.BlockSpec((tm, tk), lambda i, j, k, *_: (i, k))
    tile_spec = pl.BlockSpec((tm, tn), lambda i, j, k, *_: (i, j))
    row_spec = pl.BlockSpec((1, tn), lambda i, j, k, *_: (0, j))
    extra_specs = [tile_spec if kind == "tile" else row_spec for kind in extra_kinds]
    return _matmul(name, a, b, NT, grid, a_spec, b_spec,
                   [_sds((m, n), d) for d in out_dtypes], [tile_spec for _ in out_dtypes],
                   (tm, tn), epilogue, extras, extra_specs, after=after, b_parts=parts)


def _mm_tn_half(name, a, b, core, own, by_rows, block, recv=None, after=None, tm=1024, tn=1024, tk=2048):
    s, m = a.shape
    n = b.shape[1]
    tk = _tile(s, tk)

    def owner(chip, core_ref):
        return 2 * chip + (core_ref[0] if own else 1 - core_ref[0])

    if by_rows:
        r, c = block, n
        tm, tn = _tile(r, min(tm, r)), _tile(c, tn)
        per = r // tm
        grid = (N_CHIPS * per, c // tn, s // tk)
        a_spec = pl.BlockSpec((tk, tm), lambda i, j, k, cr, *_: (k, owner(i // per, cr) * per + i % per))
        b_spec = pl.BlockSpec((tk, tn), lambda i, j, k, cr, *_: (k, j))
        o_spec = pl.BlockSpec((None, tm, tn), lambda i, j, k, cr, *_: (i // per, i % per, j))
    else:
        r, c = m, block
        tm, tn = _tile(r, tm), _tile(c, min(tn, c))
        per = c // tn
        grid = (r // tm, N_CHIPS * per, s // tk)
        a_spec = pl.BlockSpec((tk, tm), lambda i, j, k, cr, *_: (k, i))
        b_spec = pl.BlockSpec((tk, tn), lambda i, j, k, cr, *_: (k, owner(j // per, cr) * per + j % per))
        o_spec = pl.BlockSpec((None, tm, tn), lambda i, j, k, cr, *_: (j // per, i, j % per))
    if recv is None:
        extras, epilogue = (), lambda acc: (acc,)
    else:
        extras, epilogue = (recv,), lambda acc, other: (acc + other.astype(F32),)
    return _matmul(name, a, b, TN, grid, a_spec, b_spec, [_sds((N_CHIPS, r, c), BF16)], [o_spec], (tm, tn),
                   epilogue, extras, [o_spec] * len(extras), after=after, prefetch=[core])[0]


def _rms_fwd(name, h, g, after=None):
    s, d = h.shape
    tr = _tile(s, 512)

    def body(h_ref, g_ref, o_ref):
        x = h_ref[...]
        r = lax.rsqrt(jnp.mean(x * x, axis=-1, keepdims=True) + EPS)
        o_ref[...] = (x * r * g_ref[...]).astype(o_ref.dtype)

    row = pl.BlockSpec((tr, d), lambda i: (i, 0))
    vec = pl.BlockSpec((1, d), lambda i: (0, 0))
    return _pcall(name, body, [h, g], [row, vec], _sds((s, d), BF16), row, grid=(s // tr,),
                  sem=("parallel",), after=after)


def _accumulate(ref, part, step):
    @pl.when(step == 0)
    def _():
        ref[...] = part

    @pl.when(step > 0)
    def _():
        ref[...] += part


def _rms_bwd(name, dhn, h, g, dres, after=None):
    s, d = h.shape
    tr = _tile(s, 512)
    steps = s // tr

    def body(dhn_ref, h_ref, g_ref, dres_ref, dh_ref, dhb_ref, gp_ref, acc_ref):
        step = pl.program_id(0)

        @pl.when(step == 0)
        def _():
            acc_ref[...] = jnp.zeros_like(acc_ref)

        gain = g_ref[...]

        def chunk(i, carry):
            rows = pl.ds(pl.multiple_of(i * ROW_CHUNK, ROW_CHUNK), ROW_CHUNK)
            x = h_ref[rows, :]
            r = lax.rsqrt(jnp.mean(x * x, axis=-1, keepdims=True) + EPS)
            n = x * r
            dy = dhn_ref[rows, :]
            dn = dy * gain
            dh = dres_ref[rows, :] + r * (dn - n * jnp.mean(dn * n, axis=-1, keepdims=True))
            dh_ref[rows, :] = dh
            dhb_ref[rows, :] = dh.astype(BF16)
            acc_ref[...] += dy * n
            return carry

        lax.fori_loop(0, tr // ROW_CHUNK, chunk, 0, unroll=8)

        @pl.when(step == steps - 1)
        def _():
            gp_ref[...] = jnp.sum(acc_ref[...], axis=0, keepdims=True)

    row = pl.BlockSpec((tr, d), lambda i: (i, 0))
    vec = pl.BlockSpec((1, d), lambda i: (0, 0))
    return _pcall(name, body, [dhn, h, g, dres], [row, row, vec, row],
                  [_sds((s, d), F32), _sds((s, d), BF16), _sds((1, d), F32)], [row, row, vec],
                  grid=(steps,), sem=("arbitrary",), scratch=[pltpu.VMEM((ROW_CHUNK, d), F32)], after=after)


def _loss_head(name, h, g, target, after=None):
    s, d = h.shape
    tr = _tile(s, 512)
    steps = s // tr

    def body(h_ref, g_ref, t_ref, dh_ref, dhb_ref, gp_ref, loss_ref, acc_ref, loss_acc_ref):
        step = pl.program_id(0)

        @pl.when(step == 0)
        def _():
            acc_ref[...] = jnp.zeros_like(acc_ref)
            loss_acc_ref[...] = jnp.zeros_like(loss_acc_ref)

        gg = g_ref[...]

        def chunk(i, carry):
            rows = pl.ds(pl.multiple_of(i * ROW_CHUNK, ROW_CHUNK), ROW_CHUNK)
            x = h_ref[rows, :]
            r = lax.rsqrt(jnp.mean(x * x, axis=-1, keepdims=True) + EPS)
            n = x * r
            e = n * gg - t_ref[rows, :]
            dy = e * (1.0 / d)
            dn = dy * gg
            dh = r * (dn - n * jnp.mean(dn * n, axis=-1, keepdims=True))
            dh_ref[rows, :] = dh
            dhb_ref[rows, :] = dh.astype(BF16)
            acc_ref[...] += dy * n
            loss_acc_ref[...] += jnp.mean(e * e, axis=-1, keepdims=True)
            return carry

        lax.fori_loop(0, tr // ROW_CHUNK, chunk, 0, unroll=8)

        @pl.when(step == steps - 1)
        def _():
            gp_ref[...] = jnp.sum(acc_ref[...], axis=0, keepdims=True)
            loss_ref[...] = 0.5 * jnp.sum(loss_acc_ref[...], axis=0, keepdims=True)

    row = pl.BlockSpec((tr, d), lambda i: (i, 0))
    vec = pl.BlockSpec((1, d), lambda i: (0, 0))
    one = pl.BlockSpec((1, 1), lambda i: (0, 0))
    return _pcall(name, body, [h, g, target], [row, vec, row],
                  [_sds((s, d), F32), _sds((s, d), BF16), _sds((1, d), F32), _sds((1, 1), F32)],
                  [row, row, vec, one], grid=(steps,), sem=("arbitrary",),
                  scratch=[pltpu.VMEM((ROW_CHUNK, d), F32), pltpu.VMEM((ROW_CHUNK, 1), F32)], after=after)


_SQRT_HALF = math.sqrt(0.5)
_INV_SQRT_2PI = 1.0 / math.sqrt(2.0 * math.pi)


def _gelu(x):
    return 0.5 * x * (1.0 + lax.erf(x * _SQRT_HALF))


def _gelu_grad(x):
    return 0.5 * (1.0 + lax.erf(x * _SQRT_HALF)) + x * jnp.exp(-0.5 * x * x) * _INV_SQRT_2PI


def _causal_mask():
    row = lax.broadcasted_iota(jnp.int32, (CHUNK, CHUNK), 0)
    col = lax.broadcasted_iota(jnp.int32, (CHUNK, CHUNK), 1)
    return row >= col


def _row_sum(x):
    return jnp.sum(x, axis=-1, keepdims=True)


def _masked_spatial(ws_ref, grp):
    return jnp.where(_causal_mask(), ws_ref[grp], 0.0).astype(BF16)


def _layernorm_stats(pre_ref, v_scr, w, head):
    total = jnp.zeros((CHUNK, 1), F32)
    for grp in range(A_GROUPS):
        v = _gelu(pre_ref[:, w + grp * head:w + (grp + 1) * head])
        v_scr[:, grp * head:(grp + 1) * head] = v
        total = total + _row_sum(v)
    mu = total * (1.0 / w)
    square = jnp.zeros((CHUNK, 1), F32)
    for grp in range(A_GROUPS):
        xc = v_scr[:, grp * head:(grp + 1) * head] - mu
        square = square + _row_sum(xc * xc)
    return mu, lax.rsqrt(square * (1.0 / w) + EPS)


def _amix_fwd(name, pre, ln_g, ln_b, w_s, b_s_col, after=None):
    s, w2 = pre.shape
    w = w2 // 2
    head = w // A_GROUPS

    def body(pre_ref, g_ref, b_ref, ws_ref, bs_ref, o_ref, v_scr):
        mu, rstd = _layernorm_stats(pre_ref, v_scr, w, head)
        for grp in range(A_GROUPS):
            cols = slice(grp * head, (grp + 1) * head)
            vhat = (v_scr[:, cols] - mu) * rstd
            vn = (vhat * g_ref[:, cols] + b_ref[:, cols]).astype(BF16)
            sg = jnp.dot(_masked_spatial(ws_ref, grp), vn, preferred_element_type=F32) + bs_ref[grp]
            o_ref[:, cols] = (_gelu(pre_ref[:, cols]) * sg).astype(o_ref.dtype)

    vec = pl.BlockSpec((1, w), lambda i: (0, 0))
    return _pcall(
        name, body, [pre, ln_g, ln_b, w_s, b_s_col],
        [pl.BlockSpec((CHUNK, w2), lambda i: (i, 0)), vec, vec,
         pl.BlockSpec((A_GROUPS, CHUNK, CHUNK), lambda i: (0, 0, 0)),
         pl.BlockSpec((A_GROUPS, CHUNK, 1), lambda i: (0, 0, 0))],
        _sds((s, w), BF16), pl.BlockSpec((CHUNK, w), lambda i: (i, 0)),
        grid=(s // CHUNK,), sem=("parallel",), scratch=[pltpu.VMEM((CHUNK, w), F32)], after=after)


def _amix_bwd(name, pre, dgated, ln_g, ln_b, w_s, b_s_col, after=None):
    s, w2 = pre.shape
    w = w2 // 2
    head = w // A_GROUPS

    def body(pre_ref, dg_ref, g_ref, b_ref, ws_ref, bs_ref, dpre_ref, glg_ref, glb_ref, gws_ref, gbs_ref,
             v_scr, dvn_scr):
        @pl.when(pl.program_id(0) == 0)
        def _():
            for ref in (glg_ref, glb_ref, gws_ref, gbs_ref):
                ref[...] = jnp.zeros_like(ref)

        mu, rstd = _layernorm_stats(pre_ref, v_scr, w, head)
        mask = _causal_mask()
        sum_dvhat = jnp.zeros((CHUNK, 1), F32)
        sum_dvhat_vhat = jnp.zeros((CHUNK, 1), F32)
        for grp in range(A_GROUPS):
            cols = slice(grp * head, (grp + 1) * head)
            vhat = (v_scr[:, cols] - mu) * rstd
            gain = g_ref[:, cols]
            vn = (vhat * gain + b_ref[:, cols]).astype(BF16)
            wm = _masked_spatial(ws_ref, grp)
            pre_u = pre_ref[:, cols]
            dgated = dg_ref[:, cols]
            ds = dgated * _gelu(pre_u)
            dsb = ds.astype(BF16)
            sg = jnp.dot(wm, vn, preferred_element_type=F32) + bs_ref[grp]
            dpre_ref[:, cols] = (dgated * sg * _gelu_grad(pre_u)).astype(dpre_ref.dtype)
            gws = lax.dot_general(dsb, vn, NT, preferred_element_type=F32)
            gws_ref[grp] += jnp.where(mask, gws, 0.0)
            gbs_ref[grp] += _row_sum(ds)
            dvn = lax.dot_general(wm, dsb, TN, preferred_element_type=F32)
            dvn_scr[:, cols] = dvn
            glg_ref[:, cols] += jnp.sum(dvn * vhat, axis=0, keepdims=True)
            glb_ref[:, cols] += jnp.sum(dvn, axis=0, keepdims=True)
            dvhat = dvn * gain
            sum_dvhat = sum_dvhat + _row_sum(dvhat)
            sum_dvhat_vhat = sum_dvhat_vhat + _row_sum(dvhat * vhat)
        mean_dvhat = sum_dvhat * (1.0 / w)
        mean_dvhat_vhat = sum_dvhat_vhat * (1.0 / w)
        for grp in range(A_GROUPS):
            cols = slice(grp * head, (grp + 1) * head)
            vhat = (v_scr[:, cols] - mu) * rstd
            dvhat = dvn_scr[:, cols] * g_ref[:, cols]
            dv = rstd * (dvhat - mean_dvhat - vhat * mean_dvhat_vhat)
            pre_v = pre_ref[:, w + grp * head:w + (grp + 1) * head]
            dpre_ref[:, w + grp * head:w + (grp + 1) * head] = (dv * _gelu_grad(pre_v)).astype(dpre_ref.dtype)

    vec = pl.BlockSpec((1, w), lambda i: (0, 0))
    ws_spec = pl.BlockSpec((A_GROUPS, CHUNK, CHUNK), lambda i: (0, 0, 0))
    bs_spec = pl.BlockSpec((A_GROUPS, CHUNK, 1), lambda i: (0, 0, 0))
    return _pcall(
        name, body, [pre, dgated, ln_g, ln_b, w_s, b_s_col],
        [pl.BlockSpec((CHUNK, w2), lambda i: (i, 0)), pl.BlockSpec((CHUNK, w), lambda i: (i, 0)),
         vec, vec, ws_spec, bs_spec],
        [_sds((s, w2), BF16), _sds((1, w), F32), _sds((1, w), F32),
         _sds((A_GROUPS, CHUNK, CHUNK), F32), _sds((A_GROUPS, CHUNK, 1), F32)],
        [pl.BlockSpec((CHUNK, w2), lambda i: (i, 0)), vec, vec, ws_spec, bs_spec],
        grid=(s // CHUNK,), sem=("arbitrary",),
        scratch=[pltpu.VMEM((CHUNK, w), F32), pltpu.VMEM((CHUNK, w), F32)], after=after)


def _shift_rows(x, k, forward):
    n = x.shape[0]
    row = lax.broadcasted_iota(jnp.int32, x.shape, 0)
    if forward:
        return jnp.where(row >= k, pltpu.roll(x, k, 0), 0.0)
    return jnp.where(row < n - k, pltpu.roll(x, n - k, 0), 0.0)


def _window_sum(x, window, forward):
    k = 1
    while k < window:
        x = x + _shift_rows(x, k, forward)
        k *= 2
    return x


def _pool(name, v, backward, after=None):
    s, w = v.shape
    head = w // B_GROUPS
    lane = _tile(head, 128)

    def body(v_ref, o_ref):
        grp = pl.program_id(0)
        t = lax.broadcasted_iota(jnp.int32, (s, lane), 0)
        for idx, window in enumerate(B_WINDOWS):
            @pl.when(grp == idx)
            def _():
                inv_count = 1.0 / jnp.minimum(t + 1, window).astype(F32)
                for strip in range(head // lane):
                    cols = slice(strip * lane, (strip + 1) * lane)
                    x = v_ref[:, cols]
                    if backward:
                        out = _window_sum(x * inv_count, window, False) - x
                    else:
                        out = _window_sum(x, window, True) * inv_count - x
                    o_ref[:, cols] = out.astype(o_ref.dtype)

    spec = pl.BlockSpec((s, head), lambda g: (0, g))
    return _pcall(name, body, [v], [spec], _sds((s, w), BF16), spec, grid=(B_GROUPS,),
                  sem=("parallel",), after=after)


def _scaled_dx(name, dy, w, scale, mixed, after=None, tm=1024, tn=1024):
    s, n = dy.shape
    k = w.shape[0]
    tm, tn = _tile(s, tm), _tile(k, tn)

    def body(dy_ref, w_ref, sc_ref, mx_ref, o_ref, gs_ref):
        p = lax.dot_general(dy_ref[...], w_ref[...], NT, preferred_element_type=F32)
        o_ref[...] = (p * sc_ref[...]).astype(o_ref.dtype)
        _accumulate(gs_ref, jnp.sum(p * mx_ref[...].astype(F32), axis=0, keepdims=True), pl.program_id(1))

    tile = pl.BlockSpec((tm, tn), lambda j, i: (i, j))
    vec = pl.BlockSpec((1, tn), lambda j, i: (0, j))
    return _pcall(name, body, [dy, w, scale, mixed],
                  [pl.BlockSpec((tm, n), lambda j, i: (i, 0)), pl.BlockSpec((tn, n), lambda j, i: (j, 0)), vec, tile],
                  [_sds((s, k), BF16), _sds((1, k), F32)], [tile, vec],
                  grid=(k // tn, s // tm), sem=("parallel", "arbitrary"), after=after)


def _adamw(w, g, m, v):
    m = ADAM_B1 * m + (1.0 - ADAM_B1) * g
    v = ADAM_B2 * v + (1.0 - ADAM_B2) * (g * g)
    m_hat = m / (1.0 - ADAM_B1 ** ADAM_STEP)
    v_hat = v / (1.0 - ADAM_B2 ** ADAM_STEP)
    delta = -ADAM_LR * (m_hat / (jnp.sqrt(v_hat) + ADAM_EPS) + ADAM_WD * w)
    return delta, m, v


def _adam_rows(name, g, w, m, v, after=None):
    r, c = g.shape
    tr = _tile(r, 256)

    def body(g_ref, w_ref, m_ref, v_ref, d_ref, nm_ref, nv_ref):
        d_ref[...], nm_ref[...], nv_ref[...] = _adamw(w_ref[...], g_ref[...], m_ref[...], v_ref[...])

    spec = pl.BlockSpec((tr, c), lambda i: (i, 0))
    return _pcall(name, body, [g, w, m, v], [spec] * 4, [_sds((r, c), F32)] * 3, [spec] * 3,
                  grid=(r // tr,), sem=("parallel",), after=after)


def _position():
    return lax.axis_index("x"), lax.axis_index("y"), lax.axis_index("c")


def _other_chips(x, y):
    return [(1 - x, y), (x, 1 - y), (1 - x, 1 - y)]


def _slot(px, py, pc):
    return 4 * px + 2 * py + pc


def _hbm(a):
    return pltpu.with_memory_space_constraint(a, pltpu.HBM)


def _hop1_copies(srcs, lands, send_sems, recv_sems):
    x, y, c = _position()
    peers = [(x, y, 1 - c), (1 - x, y, c), (x, 1 - y, c)]
    mine = _slot(x, y, c)
    return [[pltpu.make_async_remote_copy(
        src_ref=srcs[t], dst_ref=lands[t].at[mine], send_sem=send_sems[t].at[k], recv_sem=recv_sems[t].at[k],
        device_id=peer, device_id_type=MESH) for k, peer in enumerate(peers)] for t in range(len(srcs))]


def _hop2_copies(lands, send_sems, recv_sems):
    x, y, c = _position()
    routes = [(_slot(1 - x, y, c), (x, 1 - y, c)), (_slot(x, 1 - y, c), (1 - x, y, c))]
    out = []
    for t in range(len(lands)):
        rows = lands[t].shape[1]
        halves = [(0, rows // 2), (rows // 2, rows - rows // 2)]
        per_tensor = []
        for h, ((slot, peer), (start, size)) in enumerate(zip(routes, halves)):
            if size:
                block = lands[t].at[slot, pl.ds(start, size)]
                per_tensor.append(pltpu.make_async_remote_copy(
                    src_ref=block, dst_ref=block, send_sem=send_sems[t].at[h], recv_sem=recv_sems[t].at[h],
                    device_id=peer, device_id_type=MESH))
        for j, (slot, _) in enumerate(routes):
            block = lands[t].at[slot]
            per_tensor.append(pltpu.make_async_remote_copy(
                src_ref=block, dst_ref=block, send_sem=send_sems[t].at[2 + j], recv_sem=recv_sems[t].at[2 + j],
                device_id=(x, y, 1 - c), device_id_type=MESH))
        out.append(per_tensor)
    return out


def _split_start(name, srcs, lands, copies, n_sems, after=None):
    n = len(srcs)
    order = [] if after is None else [after]
    n_in = 2 * n + len(order)

    def body(*refs):
        for per_tensor in copies(refs[:n], refs[n:2 * n], refs[n_in:n_in + n], refs[n_in + n:n_in + 2 * n]):
            for cp in per_tensor:
                cp.start()
        refs[-1][...] = jnp.zeros_like(refs[-1])

    out_shape = ([pltpu.SemaphoreType.DMA((n_sems,)) for _ in range(2 * n)]
                 + [pltpu.HBM(a.shape, a.dtype) for a in list(srcs) + list(lands)]
                 + [_sds((8, 128), F32)])
    out = pl.pallas_call(
        body, name=name, out_shape=out_shape, in_specs=[_HBM] * (2 * n) + [_ANY] * len(order),
        out_specs=[_SEM] * (2 * n) + [_HBM] * (2 * n) + [pl.BlockSpec(memory_space=pltpu.VMEM)],
        input_output_aliases={i: 2 * n + i for i in range(2 * n)},
        compiler_params=pltpu.CompilerParams(has_side_effects=_EFFECT),
    )(*[_hbm(a) for a in srcs], *[_hbm(a) for a in lands], *order)
    return [(out[t], out[n + t], out[2 * n + t], out[3 * n + t]) for t in range(n)], out[-1]


def _split_wait(name, started, copies, after):
    n = len(started)

    def body(*refs):
        for per_tensor in copies(refs[:n], refs[n:2 * n], refs[2 * n:3 * n], refs[3 * n:4 * n]):
            for cp in per_tensor:
                cp.wait_send()
                cp.wait_recv()

    srcs = [e[2] for e in started]
    lands = [e[3] for e in started]
    out = pl.pallas_call(
        body, name=name, out_shape=[pltpu.HBM(a.shape, a.dtype) for a in srcs + lands],
        in_specs=[_HBM] * (2 * n) + [_SEM] * (2 * n) + [_ANY], out_specs=[_HBM] * (2 * n),
        input_output_aliases={i: i for i in range(2 * n)},
        compiler_params=pltpu.CompilerParams(has_side_effects=_EFFECT),
    )(*srcs, *lands, *[e[0] for e in started], *[e[1] for e in started], after)
    return out[:n], out[n:]


def _gather_step(name, arrived, fresh, after=None):
    n, m = len(arrived), len(fresh)
    order = [] if after is None else [after]
    fresh_lands = [lax.empty((N_DEV,) + s.shape, s.dtype) for s in fresh]
    buffers = [e[2] for e in arrived] + [e[3] for e in arrived] + list(fresh) + fresh_lands
    old_sems = [e[0] for e in arrived] + [e[1] for e in arrived]
    n_buf, n_old = len(buffers), len(old_sems)
    first_new = n_buf + n_old + len(order)

    def body(*refs):
        bufs, old = refs[:n_buf], refs[n_buf:n_buf + n_old]
        new = refs[first_new:first_new + 2 * n + 2 * m]
        for per_tensor in _hop1_copies(bufs[:n], bufs[n:2 * n], old[:n], old[n:]):
            for cp in per_tensor:
                cp.wait_send()
                cp.wait_recv()
        second = _hop2_copies(bufs[n:2 * n], new[:n], new[n:2 * n])
        first = _hop1_copies(bufs[2 * n:2 * n + m], bufs[2 * n + m:], new[2 * n:2 * n + m], new[2 * n + m:])
        for per_tensor in second + first:
            for cp in per_tensor:
                cp.start()
        refs[-1][...] = jnp.zeros_like(refs[-1])

    n_new = 2 * n + 2 * m
    out_shape = ([pltpu.SemaphoreType.DMA((4,)) for _ in range(2 * n)]
                 + [pltpu.SemaphoreType.DMA((3,)) for _ in range(2 * m)]
                 + [pltpu.HBM(a.shape, a.dtype) for a in buffers] + [_sds((8, 128), F32)])
    out = pl.pallas_call(
        body, name=name, out_shape=out_shape,
        in_specs=[_HBM] * n_buf + [_SEM] * n_old + [_ANY] * len(order),
        out_specs=[_SEM] * n_new + [_HBM] * n_buf + [pl.BlockSpec(memory_space=pltpu.VMEM)],
        input_output_aliases={i: n_new + i for i in range(n_buf)},
        compiler_params=pltpu.CompilerParams(has_side_effects=_EFFECT),
    )(*[_hbm(a) for a in buffers], *old_sems, *order)
    sems, bufs = out[:n_new], out[n_new:n_new + n_buf]
    second = [(sems[t], sems[n + t], bufs[t], bufs[n + t]) for t in range(n)]
    first = [(sems[2 * n + t], sems[2 * n + m + t], bufs[2 * n + t], bufs[2 * n + m + t]) for t in range(m)]
    return second, first, out[-1]


def _gather_wait(name, second, after):
    return _split_wait(name, second, lambda srcs, lands, send, recv: _hop2_copies(lands, send, recv), after)


def _sibling_copies(srcs, lands, send_sems, recv_sems):
    x, y, c = _position()
    return [[pltpu.make_async_remote_copy(
        src_ref=srcs[t], dst_ref=lands[t], send_sem=send_sems[t].at[0], recv_sem=recv_sems[t].at[0],
        device_id=(x, y, 1 - c), device_id_type=MESH)] for t in range(len(srcs))]


def _sibling_and_scatter_start(name, arrays, partials):
    k = len(arrays)

    def copies(srcs, lands, send_sems, recv_sems):
        return (_sibling_copies(srcs[:k], lands[:k], send_sems[:k], recv_sems[:k])
                + _scatter_copies(srcs[k:], lands[k:], send_sems[k:], recv_sems[k:]))

    lands = ([lax.empty(a.shape, a.dtype) for a in arrays]
             + [lax.empty((N_CHIPS - 1,) + p.shape[1:], p.dtype) for p in partials])
    return _split_start(name, list(arrays) + list(partials), lands, copies, 3)


def _sibling_wait(name, started, after):
    return _split_wait(name, started, _sibling_copies, after)[1]


def _small_copies(srcs, lands, send_sems, recv_sems):
    x, y, c = _position()
    mine = _slot(x, y, c)
    peers = [(x ^ ((k >> 2) & 1), y ^ ((k >> 1) & 1), c ^ (k & 1)) for k in range(1, N_DEV)]
    return [[pltpu.make_async_remote_copy(
        src_ref=srcs[t], dst_ref=lands[t].at[mine], send_sem=send_sems[t].at[k], recv_sem=recv_sems[t].at[k],
        device_id=peer, device_id_type=MESH) for k, peer in enumerate(peers)] for t in range(len(srcs))]


def _gather_finish(name, shards, lands, after):
    n = len(shards)

    def body(*refs):
        srcs, lands_in, outs = refs[:n], refs[n:2 * n], refs[2 * n:3 * n]
        send_sems, recv_sems, local_sems = refs[3 * n:]
        x, y, c = _position()
        local = [pltpu.make_async_copy(srcs[t], outs[t].at[_slot(x, y, c)], local_sems.at[t]) for t in range(n)]

        def diagonal(t, core):
            block = outs[t].at[_slot(1 - x, 1 - y, core)]
            return pltpu.make_async_remote_copy(
                src_ref=block, dst_ref=block, send_sem=send_sems.at[t], recv_sem=recv_sems.at[t],
                device_id=(x, y, 1 - c), device_id_type=MESH)

        for cp in local:
            cp.start()
        for t in range(n):
            diagonal(t, c).start()
        for t in range(n):
            diagonal(t, c).wait_send()
            diagonal(t, 1 - c).wait_recv()
        for cp in local:
            cp.wait()

    return _pcall(name, body, [*shards, *lands], [_ANY] * (2 * n),
                  [_sds(l.shape, l.dtype) for l in lands], [_ANY] * n,
                  scratch=[pltpu.SemaphoreType.DMA((n,)), pltpu.SemaphoreType.DMA((n,)),
                           pltpu.SemaphoreType.DMA((n,))],
                  after=after, aliases={n + t: t for t in range(n)})


def _exchange_sibling(name, fulls, after):
    n = len(fulls)

    def body(*refs):
        src = refs[:n]
        out = refs[n:2 * n]
        send_sems, recv_sems = refs[2 * n:]
        x, y, c = _position()
        copies = [pltpu.make_async_remote_copy(
            src_ref=src[t].at[:, 1 - c], dst_ref=out[t], send_sem=send_sems.at[t], recv_sem=recv_sems.at[t],
            device_id=(x, y, 1 - c), device_id_type=MESH) for t in range(n)]
        for cp in copies:
            cp.start()
        for cp in copies:
            cp.wait()

    return _pcall(name, body, fulls, [_ANY] * n, [_sds((N_CHIPS,) + f.shape[2:], f.dtype) for f in fulls],
                  [_ANY] * n, scratch=[pltpu.SemaphoreType.DMA((n,)), pltpu.SemaphoreType.DMA((n,))],
                  after=after)


def _add_sibling(name, full, recv, core, after):
    _, _, r, c = full.shape
    tr = _tile(r, max(8, (256 * 1024) // c))

    def body(core_ref, f_ref, r_ref, o_ref):
        o_ref[...] = (f_ref[...].astype(F32) + r_ref[...].astype(F32)).astype(o_ref.dtype)

    return _pcall(
        name, body, [full, recv],
        [pl.BlockSpec((None, None, tr, c), lambda p, i, core_ref: (p, core_ref[0], i, 0)),
         pl.BlockSpec((None, tr, c), lambda p, i, core_ref: (p, i, 0))],
        _sds((N_CHIPS, r, c), BF16), pl.BlockSpec((None, tr, c), lambda p, i, core_ref: (p, i, 0)),
        grid=(N_CHIPS, r // tr), sem=("parallel", "parallel"), prefetch=[core], after=after)


def _scatter_copies(srcs, lands, send_sems, recv_sems):
    x, y, c = _position()
    return [[pltpu.make_async_remote_copy(
        src_ref=srcs[t].at[2 * px + py], dst_ref=lands[t].at[j],
        send_sem=send_sems[t].at[j], recv_sem=recv_sems[t].at[j],
        device_id=(px, py, c), device_id_type=MESH) for j, (px, py) in enumerate(_other_chips(x, y))]
        for t in range(len(srcs))]


def _scatter_wait(name, started, after):
    return _split_wait(name, started, _scatter_copies, after)


class _Job:
    def __init__(self, **fields):
        self.__dict__.update(fields)


def _reduce_adam_job(partial, recv, chip, w, m, v, layer, carried):
    n_layers, r, c = w.shape
    tr = _tile(r, max(8, (256 * 1024) // c))

    def body(p_ref, r_ref, w_ref, m_ref, v_ref, *rest):
        g_ref, d_ref, nm_ref, nv_ref = rest[-4:]
        g = p_ref[...].astype(F32)
        for j in range(N_CHIPS - 1):
            g = g + r_ref[j].astype(F32)
        g_ref[...] = g
        d_ref[...], nm_ref[...], nv_ref[...] = _adamw(w_ref[...], g, m_ref[...], v_ref[...])

    layered = ((None, tr, c), lambda blk, chip_ref: (layer, blk, 0))
    in_specs = [((None, tr, c), lambda blk, chip_ref: (chip_ref[0], blk, 0)),
                ((N_CHIPS - 1, tr, c), lambda blk, chip_ref: (0, blk, 0)), layered, layered, layered]
    operands = [partial, recv, w, m, v]
    aliases = {}
    if carried is not None:
        operands += list(carried)
        in_specs += [None] * 4
        aliases = {5 + o: o for o in range(4)}
    return _Job(operands=operands, in_specs=in_specs, out_shape=[_sds((n_layers, r, c), F32)] * 4,
                out_specs=[layered] * 4, body=body, aliases=aliases, prefetch=chip, n_blocks=r // tr)


def _run_job(name, job, after):
    def spec(entry):
        if entry is None:
            return _ANY
        shape, index = entry
        return pl.BlockSpec(shape, lambda blk, pre, index=index: index(blk, pre))

    def body(pre_ref, *refs):
        job.body(*refs)

    return _pcall(name, body, job.operands, [spec(e) for e in job.in_specs], job.out_shape,
                  [spec(e) for e in job.out_specs], grid=(job.n_blocks,), sem=("parallel",),
                  prefetch=[job.prefetch], after=after,
                  aliases={1 + i: o for i, o in job.aliases.items()})


def _small_sum(name, gathered, own, device, after=None):
    r, lanes = own.shape

    def body(dev_ref, g_ref, own_ref, out_ref):
        dev = dev_ref[0]
        mine = own_ref[...]
        total = jnp.where(dev == 0, mine, g_ref[0])
        for d in range(1, N_DEV):
            total = total + jnp.where(dev == d, mine, g_ref[d])
        out_ref[...] = total

    return _pcall(name, body, [gathered, own],
                  [pl.BlockSpec((N_DEV, r, lanes), lambda i, dev_ref: (0, 0, 0)),
                   pl.BlockSpec((r, lanes), lambda i, dev_ref: (0, 0))],
                  _sds((r, lanes), F32), pl.BlockSpec((r, lanes), lambda i, dev_ref: (0, 0)),
                  grid=(1,), sem=("arbitrary",), prefetch=[device], after=after)


def _pack(arrays):
    return jnp.concatenate([a.reshape(-1, 128) for a in arrays], axis=0)


def _unpack(packed, shapes):
    out, row = [], 0
    for shape in shapes:
        rows = math.prod(shape) // 128
        out.append(packed[row:row + rows].reshape(shape))
        row += rows
    return out


class _Order:
    def __init__(self):
        self.last = None

    def __call__(self, fn, *args, **kwargs):
        out = fn(*args, after=self.last, **kwargs)
        self.last = out[0] if isinstance(out, (list, tuple)) else out
        return out


def kernel(x, a_w_in, a_ln_g, a_ln_b, a_w_s, a_b_s, a_w_out, b_w_in, b_w_grp, b_scale, b_w_out, norm_mix, norm_mlp, mlp_w1, mlp_w2, final_norm, loss_target, m_a_w_in, m_a_ln_g, m_a_ln_b, m_a_w_s, m_a_b_s, m_a_w_out, m_b_w_in, m_b_w_grp, m_b_scale, m_b_w_out, m_norm_mix, m_norm_mlp, m_mlp_w1, m_mlp_w2, m_final_norm, v_a_w_in, v_a_ln_g, v_a_ln_b, v_a_w_s, v_a_b_s, v_a_w_out, v_b_w_in, v_b_w_grp, v_b_scale, v_b_w_out, v_norm_mix, v_norm_mlp, v_mlp_w1, v_mlp_w2, v_final_norm):
    s, d = x.shape[1], x.shape[2]
    depth = mlp_w1.shape[0]
    a_slab = a_w_in.shape[2]
    ff_slab = mlp_w1.shape[2]
    ff_rows = mlp_w2.shape[1]
    bh = b_w_grp.shape[3]
    my_x, my_y, my_c = _position()
    core = jnp.reshape(my_c, (1,)).astype(jnp.int32)
    chip = jnp.reshape(2 * my_x + my_y, (1,)).astype(jnp.int32)
    device = _slot(my_x, my_y, my_c)
    run = _Order()

    w1_b, w2_b = mlp_w1.astype(BF16), mlp_w2.astype(BF16)
    shards = [a_w_in[0].astype(BF16), a_w_out[0].astype(BF16), b_scale,
              w1_b[0], w2_b[0],
              b_w_in[0].astype(BF16), b_w_grp[0].astype(BF16), b_w_out[0].astype(BF16),
              w1_b[1], w2_b[1]]
    groups = [[0], [1, 2], [3], [4], [5, 6, 7], [8], [9]]
    start_with = {1: [2], 2: [3, 4], 4: [5], 5: [6]}
    hop1, hop2 = {}, {}
    _, hop1[0], token = _gather_step("weights_group0_hop1", [], [shards[t] for t in groups[0]])
    _, hop1[1], token = _gather_step("weights_group1_hop1", [], [shards[t] for t in groups[1]], token)
    run.last = token

    def advance(g):
        if g not in hop1:
            return
        ahead = start_with.get(g, [])
        fresh = [shards[t] for a in ahead for t in groups[a]]
        hop2[g], started, tok = _gather_step(f"weights_group{g}_hop2", hop1.pop(g), fresh, run.last)
        for a in ahead:
            hop1[a], started = started[:len(groups[a])], started[len(groups[a]):]
        run.last = tok

    def gathered(g):
        advance(g)
        if g == 0:
            advance(1)
        srcs, lands = _gather_wait(f"weights_group{g}_wait", hop2.pop(g), run.last)
        run.last = srcs[0]
        return run(_gather_finish, f"weights_group{g}_finish", srcs, lands)

    h0 = x[0]
    target = loss_target[0]
    ln_g, ln_b = a_ln_g, a_ln_b
    w_s = a_w_s[0]
    b_s_col = a_b_s[0][:, :, None]
    nmix = [norm_mix[l][None, :] for l in range(depth)]
    nmlp = [norm_mlp[l][None, :] for l in range(depth)]

    def mlp_forward(l, h, up_group):
        hn = run(_rms_fwd, f"mlp{l}_norm", h, nmlp[l])
        (w1,) = gathered(up_group)
        if l > 0:
            advance(up_group + 1)
        act, act_sq = run(_mm_nn, f"mlp{l}_up", hn, w1,
                          lambda acc: (jnp.maximum(acc, 0.0), jnp.square(jnp.maximum(acc, 0.0))),
                          (BF16, BF16), slab=True)
        (w2,) = gathered(up_group + 1)
        advance(up_group + 2)
        w2 = w2.reshape(-1, d)
        (h_out,) = run(_mm_nn, f"mlp{l}_down", act_sq, w2, lambda acc, res: (acc + res,), (F32,),
                       extras=(h,), extra_kinds=("tile",))
        return h_out, (h, hn, act, act_sq, w1, w2)

    scattered = []

    pending = []

    def scatter_partials(name, partials, specs):
        pending.append((name, partials, specs))

    def start_exchanges(name, to_sibling):
        partials = [p for _, group, _ in pending for p in group]
        in_flight, tok = _sibling_and_scatter_start(name, to_sibling, partials)
        run.last = tok
        first = len(to_sibling)
        for group_name, group, specs in pending:
            scattered.append((group_name, in_flight[first:first + len(group)], specs))
            first += len(group)
        pending.clear()
        return in_flight[:len(to_sibling)]

    weights = {"a_w_in": (a_w_in, m_a_w_in, v_a_w_in), "a_w_out": (a_w_out, m_a_w_out, v_a_w_out),
               "b_w_in": (b_w_in, m_b_w_in, v_b_w_in), "b_w_grp": (b_w_grp, m_b_w_grp, v_b_w_grp),
               "b_w_out": (b_w_out, m_b_w_out, v_b_w_out), "mlp_w1": (mlp_w1, m_mlp_w1, v_mlp_w1),
               "mlp_w2": (mlp_w2, m_mlp_w2, v_mlp_w2)}
    results = {}

    def finish_group(name, in_flight, specs):
        partials, lands = _scatter_wait(name + "_scatter_wait", in_flight, run.last)
        run.last = lands[0]
        for t, (wname, layer) in enumerate(specs):
            w, m, v = weights[wname]
            shape = (w.shape[0],) + partials[t].shape[1:]
            job = _reduce_adam_job(partials[t], lands[t], chip, w.reshape(shape), m.reshape(shape),
                                   v.reshape(shape), layer, results.get(wname))
            results[wname] = run(_run_job, f"{name}_reduce_adam_{t}", job)

    def weight_grad(name, a, b, by_rows, block, between, tm=1024, tn=1024):
        other = run(_mm_tn_half, name + "_other", a, b, core, False, by_rows, block, tm=tm, tn=tn)
        sent = start_exchanges(name + "_sibling_start", [other])
        middle = between()
        (recv,) = _sibling_wait(name + "_sibling_wait", sent, run.last)
        run.last = recv
        return run(_mm_tn_half, name + "_own", a, b, core, True, by_rows, block, recv=recv, tm=tm, tn=tn), middle

    def mlp_backward(l, saved, dh, dhb):
        h, hn, act, act_sq, w1, w2 = saved
        part_w2, (dpre,) = weight_grad(
            f"mlp{l}_down_dw", act_sq, dhb, True, ff_rows,
            lambda: run(_mm_nt, f"mlp{l}_down_dx", dhb, w2, lambda acc, a: (2.0 * a.astype(F32) * acc,),
                        (BF16,), extras=(act,), extra_kinds=("tile",)))
        scatter_partials(f"mlp{l}_down_grads", [part_w2], [("mlp_w2", l)])
        part_w1, (dhn,) = weight_grad(
            f"mlp{l}_up_dw", hn, dpre, False, ff_slab,
            lambda: run(_mm_nt, f"mlp{l}_up_dx", dpre, w1, lambda acc: (acc,), (F32,), slab=True))
        scatter_partials(f"mlp{l}_up_grads", [part_w1], [("mlp_w1", l)])
        dh, dhb, g_norm = run(_rms_bwd, f"mlp{l}_norm_bwd", dhn, h, nmlp[l], dh)
        return dh, dhb, g_norm

    hn0 = run(_rms_fwd, "mix0_norm", h0, nmix[0])
    (wa_in,) = gathered(0)
    (pre,) = run(_mm_nn, "mixa_in", hn0, wa_in, lambda acc: (acc,), (F32,), slab=True)
    wa_out, scale = gathered(1)
    wa_out, scale = wa_out.reshape(d, d), scale.reshape(1, d)
    gated = run(_amix_fwd, "mixa_gate", pre, ln_g, ln_b, w_s, b_s_col)
    advance(2)
    (h1,) = run(_mm_nn, "mixa_out", gated, wa_out, lambda acc, res: (acc + res,), (F32,),
                extras=(h0,), extra_kinds=("tile",))
    h2, saved_mlp0 = mlp_forward(0, h1, 2)
    hn2 = run(_rms_fwd, "mix1_norm", h2, nmix[1])
    wb_in, wb_grp, wb_out = gathered(4)
    advance(5)
    wb_in, wb_out = wb_in.reshape(d, d), wb_out.reshape(d, d)
    wb_grp = jnp.transpose(wb_grp, (1, 0, 2, 3)).reshape(B_GROUPS, bh, bh)
    (vb,) = run(_mm_nn, "mixb_in", hn2, wb_in, lambda acc: (acc,), (F32,))
    pooled = run(_pool, "mixb_pool", vb, backward=False)
    tm = _tile(s, 1024)
    grp_tile = pl.BlockSpec((tm, bh), lambda i, j, k: (i, j))
    grp_weight = pl.BlockSpec((None, bh, bh), lambda i, j, k: (j, 0, 0))
    mixed, mixed_scaled = run(
        _matmul, "mixb_grp", pooled, wb_grp, NN, (s // tm, B_GROUPS, 1), grp_tile, grp_weight,
        [_sds((s, d), BF16), _sds((s, d), BF16)], [grp_tile] * 2,
        (tm, bh), lambda acc, sc: (acc, acc * sc), (scale,), [pl.BlockSpec((1, bh), lambda i, j, k: (0, j))])
    (h3,) = run(_mm_nn, "mixb_out", mixed_scaled, wb_out, lambda acc, res: (acc + res,), (F32,),
                extras=(h2,), extra_kinds=("tile",))
    h4, saved_mlp1 = mlp_forward(1, h3, 5)
    dh, dhb, g_final, loss_part = run(_loss_head, "loss_head", h4, final_norm[None, :], target)

    dh, dhb, g_nmlp1 = mlp_backward(1, saved_mlp1, dh, dhb)
    tks = _tile(s, 1024)
    grp_rows = pl.BlockSpec((tks, bh), lambda i, j, k: (k, j))

    def mixb_middle():
        dms_scaled, g_scale = run(_scaled_dx, "mixb_out_dx", dhb, wb_out, scale, mixed)
        (g_wb_grp,) = run(
            _matmul, "mixb_grp_dw", pooled, dms_scaled, TN, (1, B_GROUPS, s // tks), grp_rows, grp_rows,
            [_sds((B_GROUPS, bh, bh), BF16)], [grp_weight], (bh, bh), lambda acc: (acc,))
        (dpooled,) = run(
            _matmul, "mixb_grp_dx", dms_scaled, wb_grp, NT, (s // tm, B_GROUPS, 1), grp_tile, grp_weight,
            [_sds((s, d), F32)], [grp_tile], (tm, bh), lambda acc: (acc,))
        return g_scale, g_wb_grp, run(_pool, "mixb_pool_bwd", dpooled, backward=True)

    part_wb_out, (g_scale, g_wb_grp, dvb) = weight_grad("mixb_out_dw", mixed_scaled, dhb, True, d // N_DEV,
                                                        mixb_middle, tn=d)
    part_wb_in, (dhn2,) = weight_grad(
        "mixb_in_dw", hn2, dvb, True, d // N_DEV,
        lambda: run(_mm_nt, "mixb_in_dx", dvb, wb_in, lambda acc: (acc,), (F32,)), tn=d)
    grp_full = jnp.transpose(g_wb_grp.reshape(B_GROUPS, N_DEV, bh // N_DEV, bh), (1, 0, 2, 3))
    grp_full = grp_full.reshape(N_CHIPS, 2, B_GROUPS * bh // N_DEV, bh)
    (grp_sibling,) = run(_exchange_sibling, "mixb_grp_dw_to_sibling", [grp_full])
    part_wb_grp = run(_add_sibling, "mixb_grp_dw_add_sibling", grp_full, grp_sibling, core)
    scatter_partials("mixb_grads", [part_wb_out, part_wb_grp, part_wb_in],
                     [("b_w_out", 0), ("b_w_grp", 0), ("b_w_in", 0)])
    dh, dhb, g_nmix1 = run(_rms_bwd, "mix1_norm_bwd", dhn2, h2, nmix[1], dh)
    dh, dhb, g_nmlp0 = mlp_backward(0, saved_mlp0, dh, dhb)
    def mixa_middle():
        (dgated,) = run(_mm_nt, "mixa_out_dx", dhb, wa_out, lambda acc: (acc,), (F32,))
        return run(_amix_bwd, "mixa_gate_bwd", pre, dgated, ln_g, ln_b, w_s, b_s_col)

    part_wa_out, (dpre, g_ln_g, g_ln_b, g_w_s, g_b_s) = weight_grad("mixa_out_dw", gated, dhb, True, d // N_DEV,
                                                                     mixa_middle, tn=d)
    part_wa_in, (dhn0,) = weight_grad(
        "mixa_in_dw", hn0, dpre, False, a_slab,
        lambda: run(_mm_nt, "mixa_in_dx", dpre, wa_in, lambda acc: (acc,), (F32,), slab=True), tm=d)
    scatter_partials("mixa_grads", [part_wa_in, part_wa_out], [("a_w_in", 0), ("a_w_out", 0)])
    start_exchanges("mixa_grads_scatter_start", [])
    grad_x, _, g_nmix0 = run(_rms_bwd, "mix0_norm_bwd", dhn0, h0, nmix[0], dh)

    g_norm_mix = jnp.concatenate([g_nmix0, g_nmix1], axis=0)
    g_norm_mlp = jnp.concatenate([g_nmlp0, g_nmlp1], axis=0)
    loss_row = jnp.pad(loss_part, ((0, 0), (0, 127)))
    small_parts = [g_ln_g, g_ln_b, g_w_s, g_b_s, g_norm_mix, g_norm_mlp, g_final, g_scale, loss_row]
    packed = _pack(small_parts)
    small_sent, tok = _split_start("small_grads_start", [packed], [lax.empty((N_DEV,) + packed.shape, F32)],
                                   _small_copies, N_DEV - 1)
    run.last = tok

    for group in scattered:
        finish_group(*group)
    own_packed, small_gathered = _split_wait("small_grads_wait", small_sent, _small_copies, run.last)
    run.last = small_gathered[0]
    small_sum = run(_small_sum, "small_grads_sum", small_gathered[0], own_packed[0],
                    jnp.reshape(device, (1,)).astype(jnp.int32))
    sg = _unpack(small_sum, [a_ln_g.shape, a_ln_b.shape, a_w_s.shape, a_b_s.shape, norm_mix.shape,
                             norm_mlp.shape, final_norm.shape, (1, d), (1, 128)])
    loss = sg.pop()[0, 0]
    shard = b_scale.shape[1]
    sg[7] = lax.dynamic_slice(sg[7], (0, device * shard), (1, shard))
    small_w = [a_ln_g, a_ln_b, a_w_s, a_b_s, norm_mix, norm_mlp, final_norm, b_scale]
    small_m = [m_a_ln_g, m_a_ln_b, m_a_w_s, m_a_b_s, m_norm_mix, m_norm_mlp, m_final_norm, m_b_scale]
    small_v = [v_a_ln_g, v_a_ln_b, v_a_w_s, v_a_b_s, v_norm_mix, v_norm_mlp, v_final_norm, v_b_scale]
    small_out = run(_adam_rows, "small_adam", _pack(sg), _pack(small_w), _pack(small_m), _pack(small_v))
    shapes = [w.shape for w in small_w]
    small_res = [sg] + [_unpack(o, shapes) for o in small_out]
    big = {wname: [o.reshape(weights[wname][0].shape) for o in outs] for wname, outs in results.items()}

    def leaf(o):
        return (big["a_w_in"][o], small_res[o][0], small_res[o][1], small_res[o][2], small_res[o][3],
                big["a_w_out"][o], big["b_w_in"][o], big["b_w_grp"][o], small_res[o][7], big["b_w_out"][o],
                small_res[o][4], small_res[o][5], big["mlp_w1"][o], big["mlp_w2"][o], small_res[o][6])

    return (loss, grad_x[None], *leaf(0), *leaf(1), *leaf(2), *leaf(3))
```

```python
import math

import jax
import jax.numpy as jnp
from jax import lax
from jax.experimental import pallas as pl
from jax.experimental.pallas import tpu as pltpu

F32 = jnp.float32
BF16 = jnp.bfloat16
MESH = pl.DeviceIdType.MESH

N_DEV = 8
N_CHIPS = 4
CHUNK = 128
A_GROUPS = 8
B_WINDOWS = (2, 4, 8, 16)
B_GROUPS = len(B_WINDOWS)
EPS = 1e-6
ADAM_LR = 0.001
ADAM_B1 = 0.9
ADAM_B2 = 0.999
ADAM_EPS = 1e-08
ADAM_WD = 0.01
ADAM_STEP = 10

VMEM_LIMIT = 48 * 1024 * 1024
ROW_CHUNK = 16

NN = (((1,), (0,)), ((), ()))
NT = (((1,), (1,)), ((), ()))
TN = (((0,), (0,)), ((), ()))

_ANY = pl.BlockSpec(memory_space=pl.ANY)
_HBM = pl.BlockSpec(memory_space=pltpu.HBM)
_SEM = pl.BlockSpec(memory_space=pltpu.SEMAPHORE)
_EFFECT = pltpu.SideEffectType.DATAFLOW_SIDE_EFFECTING


def _tile(n, pref):
    return pref if n % pref == 0 else n


def _sds(shape, dtype):
    return jax.ShapeDtypeStruct(shape, dtype)


def _pcall(name, body, operands, in_specs, out_shape, out_specs, *, grid=None, sem=None, scratch=(),
           prefetch=(), after=None, aliases=None):
    after = [] if after is None else [after]
    n_lead = len(prefetch) + len(operands)
    n_after = len(after)

    def wrapped(*refs):
        body(*refs[:n_lead], *refs[n_lead + n_after:])

    in_specs = list(in_specs) + [_ANY] * n_after
    params = pltpu.CompilerParams(vmem_limit_bytes=VMEM_LIMIT) if sem is None else \
        pltpu.CompilerParams(dimension_semantics=sem, vmem_limit_bytes=VMEM_LIMIT)
    kwargs = dict(out_shape=out_shape, scratch_shapes=list(scratch), compiler_params=params, name=name,
                  input_output_aliases=aliases or {})
    if prefetch:
        kwargs["grid_spec"] = pltpu.PrefetchScalarGridSpec(
            num_scalar_prefetch=len(prefetch), grid=grid, in_specs=in_specs, out_specs=out_specs,
            scratch_shapes=list(scratch))
        kwargs.pop("scratch_shapes")
    else:
        kwargs.update(in_specs=in_specs, out_specs=out_specs)
        if grid is not None:
            kwargs["grid"] = grid
    return pl.pallas_call(wrapped, **kwargs)(*prefetch, *operands, *after)


def _matmul(name, a, b, dims, grid, a_spec, b_spec, out_shape, out_specs, acc_shape,
            epilogue, extras=(), extra_specs=(), after=None, prefetch=(), b_parts=1):
    nk = grid[2]
    n_extra = len(extras)
    n_out = len(out_shape)
    n_pre = len(prefetch)

    def body(*refs):
        refs = refs[n_pre:]
        a_ref, b_ref = refs[0], refs[1]
        extra_refs = refs[2:2 + n_extra]
        out_refs = refs[2 + n_extra:2 + n_extra + n_out]

        def finish(acc):
            outs = epilogue(acc, *[r[...] for r in extra_refs])
            for o_ref, o in zip(out_refs, outs):
                o_ref[...] = o.astype(o_ref.dtype)

        def product():
            if b_parts == 1:
                return lax.dot_general(a_ref[...], b_ref[...], dims, preferred_element_type=F32)
            width = b_ref.shape[2]
            total = None
            for p in range(b_parts):
                part = lax.dot_general(a_ref[:, p * width:(p + 1) * width], b_ref[p], dims,
                                       preferred_element_type=F32)
                total = part if total is None else total + part
            return total

        if nk == 1:
            finish(product())
        else:
            acc_ref = refs[-1]
            k = pl.program_id(2)

            @pl.when(k == 0)
            def _():
                acc_ref[...] = product()

            if nk > 2:
                @pl.when(jnp.logical_and(k > 0, k < nk - 1))
                def _():
                    acc_ref[...] += product()

            @pl.when(k == nk - 1)
            def _():
                finish(acc_ref[...] + product())

    scratch = [] if nk == 1 else [pltpu.VMEM(acc_shape, F32)]
    return _pcall(name, body, [a, b, *extras], [a_spec, b_spec, *extra_specs], out_shape, out_specs,
                  grid=grid, sem=("parallel", "parallel", "arbitrary"), scratch=scratch, after=after,
                  prefetch=prefetch)


def _mm_nn(name, a, b, epilogue, out_dtypes, extras=(), extra_kinds=(), slab=False, after=None,
           tm=1024, tn=1024, tk=2048):
    m, kd = a.shape
    if slab:
        n_slab, _, w = b.shape
        n = n_slab * w
        tn = _tile(w, min(tn, w))
        per = w // tn
        tk = _tile(kd, tk)
        b_spec = pl.BlockSpec((None, tk, tn), lambda i, j, k: (j // per, k, j % per))
    else:
        n = b.shape[1]
        tn = _tile(n, tn)
        tk = _tile(kd, tk)
        b_spec = pl.BlockSpec((tk, tn), lambda i, j, k: (k, j))
    tm = _tile(m, tm)
    grid = (m // tm, n // tn, kd // tk)
    a_spec = pl.BlockSpec((tm, tk), lambda i, j, k: (i, k))
    tile_spec = pl.BlockSpec((tm, tn), lambda i, j, k: (i, j))
    row_spec = pl.BlockSpec((1, tn), lambda i, j, k: (0, j))
    extra_specs = [tile_spec if kind == "tile" else row_spec for kind in extra_kinds]
    return _matmul(name, a, b, NN, grid, a_spec, b_spec,
                   [_sds((m, n), d) for d in out_dtypes], [tile_spec for _ in out_dtypes],
                   (tm, tn), epilogue, extras, extra_specs, after=after)


def _mm_nt(name, a, b, epilogue, out_dtypes, extras=(), extra_kinds=(), slab=False, after=None,
           tm=1024, tn=1024, tk=2048):
    m, kd = a.shape
    parts = 1
    if slab:
        n_slab, n, w = b.shape
        tn = _tile(n, tn)
        if tk > w and tk % w == 0 and n_slab % (tk // w) == 0:
            parts = tk // w
            b_spec = pl.BlockSpec((parts, tn, w), lambda i, j, k, *_: (k, j, 0))
        else:
            tk = _tile(w, min(tk, w))
            per = w // tk
            b_spec = pl.BlockSpec((None, tn, tk), lambda i, j, k, *_: (k // per, j, k % per))
    else:
        n = b.shape[0]
        tn = _tile(n, tn)
        tk = _tile(kd, tk)
        b_spec = pl.BlockSpec((tn, tk), lambda i, j, k, *_: (j, k))
    tm = _tile(m, tm)
    grid = (m // tm, n // tn, kd // tk)
    a_spec = pl.BlockSpec((tm, tk), lambda i, j, k, *_: (i, k))
    tile_spec = pl.BlockSpec((tm, tn), lambda i, j, k, *_: (i, j))
    row_spec = pl.BlockSpec((1, tn), lambda i, j, k, *_: (0, j))
    extra_specs = [tile_spec if kind == "tile" else row_spec for kind in extra_kinds]
    return _matmul(name, a, b, NT, grid, a_spec, b_spec,
                   [_sds((m, n), d) for d in out_dtypes], [tile_spec for _ in out_dtypes],
                   (tm, tn), epilogue, extras, extra_specs, after=after, b_parts=parts)


def _mm_tn_half(name, a, b, core, own, by_rows, block, recv=None, after=None, tm=1024, tn=1024, tk=2048):
    s, m = a.shape
    n = b.shape[1]
    tk = _tile(s, tk)

    def owner(chip, core_ref):
        return 2 * chip + (core_ref[0] if own else 1 - core_ref[0])

    if by_rows:
        r, c = block, n
        tm, tn = _tile(r, min(tm, r)), _tile(c, tn)
        per = r // tm
        grid = (N_CHIPS * per, c // tn, s // tk)
        a_spec = pl.BlockSpec((tk, tm), lambda i, j, k, cr, *_: (k, owner(i // per, cr) * per + i % per))
        b_spec = pl.BlockSpec((tk, tn), lambda i, j, k, cr, *_: (k, j))
        o_spec = pl.BlockSpec((None, tm, tn), lambda i, j, k, cr, *_: (i // per, i % per, j))
    else:
        r, c = m, block
        tm, tn = _tile(r, tm), _tile(c, min(tn, c))
        per = c // tn
        grid = (r // tm, N_CHIPS * per, s // tk)
        a_spec = pl.BlockSpec((tk, tm), lambda i, j, k, cr, *_: (k, i))
        b_spec = pl.BlockSpec((tk, tn), lambda i, j, k, cr, *_: (k, owner(j // per, cr) * per + j % per))
        o_spec = pl.BlockSpec((None, tm, tn), lambda i, j, k, cr, *_: (j // per, i, j % per))
    if recv is None:
        extras, epilogue = (), lambda acc: (acc,)
    else:
        extras, epilogue = (recv,), lambda acc, other: (acc + other.astype(F32),)
    return _matmul(name, a, b, TN, grid, a_spec, b_spec, [_sds((N_CHIPS, r, c), BF16)], [o_spec], (tm, tn),
                   epilogue, extras, [o_spec] * len(extras), after=after, prefetch=[core])[0]


def _rms_fwd(name, h, g, after=None):
    s, d = h.shape
    tr = _tile(s, 512)

    def body(h_ref, g_ref, o_ref):
        x = h_ref[...]
        r = lax.rsqrt(jnp.mean(x * x, axis=-1, keepdims=True) + EPS)
        o_ref[...] = (x * r * g_ref[...]).astype(o_ref.dtype)

    row = pl.BlockSpec((tr, d), lambda i: (i, 0))
    vec = pl.BlockSpec((1, d), lambda i: (0, 0))
    return _pcall(name, body, [h, g], [row, vec], _sds((s, d), BF16), row, grid=(s // tr,),
                  sem=("parallel",), after=after)


def _accumulate(ref, part, step):
    @pl.when(step == 0)
    def _():
        ref[...] = part

    @pl.when(step > 0)
    def _():
        ref[...] += part


def _rms_bwd(name, dhn, h, g, dres, after=None):
    s, d = h.shape
    tr = _tile(s, 512)
    steps = s // tr

    def body(dhn_ref, h_ref, g_ref, dres_ref, dh_ref, dhb_ref, gp_ref, acc_ref):
        step = pl.program_id(0)

        @pl.when(step == 0)
        def _():
            acc_ref[...] = jnp.zeros_like(acc_ref)

        gain = g_ref[...]

        def chunk(i, carry):
            rows = pl.ds(pl.multiple_of(i * ROW_CHUNK, ROW_CHUNK), ROW_CHUNK)
            x = h_ref[rows, :]
            r = lax.rsqrt(jnp.mean(x * x, axis=-1, keepdims=True) + EPS)
            n = x * r
            dy = dhn_ref[rows, :]
            dn = dy * gain
            dh = dres_ref[rows, :] + r * (dn - n * jnp.mean(dn * n, axis=-1, keepdims=True))
            dh_ref[rows, :] = dh
            dhb_ref[rows, :] = dh.astype(BF16)
            acc_ref[...] += dy * n
            return carry

        lax.fori_loop(0, tr // ROW_CHUNK, chunk, 0, unroll=8)

        @pl.when(step == steps - 1)
        def _():
            gp_ref[...] = jnp.sum(acc_ref[...], axis=0, keepdims=True)

    row = pl.BlockSpec((tr, d), lambda i: (i, 0))
    vec = pl.BlockSpec((1, d), lambda i: (0, 0))
    return _pcall(name, body, [dhn, h, g, dres], [row, row, vec, row],
                  [_sds((s, d), F32), _sds((s, d), BF16), _sds((1, d), F32)], [row, row, vec],
                  grid=(steps,), sem=("arbitrary",), scratch=[pltpu.VMEM((ROW_CHUNK, d), F32)], after=after)


def _loss_head(name, h, g, target, after=None):
    s, d = h.shape
    tr = _tile(s, 512)
    steps = s // tr

    def body(h_ref, g_ref, t_ref, dh_ref, dhb_ref, gp_ref, loss_ref, acc_ref, loss_acc_ref):
        step = pl.program_id(0)

        @pl.when(step == 0)
        def _():
            acc_ref[...] = jnp.zeros_like(acc_ref)
            loss_acc_ref[...] = jnp.zeros_like(loss_acc_ref)

        gg = g_ref[...]

        def chunk(i, carry):
            rows = pl.ds(pl.multiple_of(i * ROW_CHUNK, ROW_CHUNK), ROW_CHUNK)
            x = h_ref[rows, :]
            r = lax.rsqrt(jnp.mean(x * x, axis=-1, keepdims=True) + EPS)
            n = x * r
            e = n * gg - t_ref[rows, :]
            dy = e * (1.0 / d)
            dn = dy * gg
            dh = r * (dn - n * jnp.mean(dn * n, axis=-1, keepdims=True))
            dh_ref[rows, :] = dh
            dhb_ref[rows, :] = dh.astype(BF16)
            acc_ref[...] += dy * n
            loss_acc_ref[...] += jnp.mean(e * e, axis=-1, keepdims=True)
            return carry

        lax.fori_loop(0, tr // ROW_CHUNK, chunk, 0, unroll=8)

        @pl.when(step == steps - 1)
        def _():
            gp_ref[...] = jnp.sum(acc_ref[...], axis=0, keepdims=True)
            loss_ref[...] = 0.5 * jnp.sum(loss_acc_ref[...], axis=0, keepdims=True)

    row = pl.BlockSpec((tr, d), lambda i: (i, 0))
    vec = pl.BlockSpec((1, d), lambda i: (0, 0))
    one = pl.BlockSpec((1, 1), lambda i: (0, 0))
    return _pcall(name, body, [h, g, target], [row, vec, row],
                  [_sds((s, d), F32), _sds((s, d), BF16), _sds((1, d), F32), _sds((1, 1), F32)],
                  [row, row, vec, one], grid=(steps,), sem=("arbitrary",),
                  scratch=[pltpu.VMEM((ROW_CHUNK, d), F32), pltpu.VMEM((ROW_CHUNK, 1), F32)], after=after)


_SQRT_HALF = math.sqrt(0.5)
_INV_SQRT_2PI = 1.0 / math.sqrt(2.0 * math.pi)


def _gelu(x):
    return 0.5 * x * (1.0 + lax.erf(x * _SQRT_HALF))


def _gelu_grad(x):
    return 0.5 * (1.0 + lax.erf(x * _SQRT_HALF)) + x * jnp.exp(-0.5 * x * x) * _INV_SQRT_2PI


def _causal_mask():
    row = lax.broadcasted_iota(jnp.int32, (CHUNK, CHUNK), 0)
    col = lax.broadcasted_iota(jnp.int32, (CHUNK, CHUNK), 1)
    return row >= col


def _row_sum(x):
    return jnp.sum(x, axis=-1, keepdims=True)


def _masked_spatial(ws_ref, grp):
    return jnp.where(_causal_mask(), ws_ref[grp], 0.0).astype(BF16)


def _layernorm_stats(pre_ref, v_scr, w, head):
    total = jnp.zeros((CHUNK, 1), F32)
    for grp in range(A_GROUPS):
        v = _gelu(pre_ref[:, w + grp * head:w + (grp + 1) * head])
        v_scr[:, grp * head:(grp + 1) * head] = v
        total = total + _row_sum(v)
    mu = total * (1.0 / w)
    square = jnp.zeros((CHUNK, 1), F32)
    for grp in range(A_GROUPS):
        xc = v_scr[:, grp * head:(grp + 1) * head] - mu
        square = square + _row_sum(xc * xc)
    return mu, lax.rsqrt(square * (1.0 / w) + EPS)


def _amix_fwd(name, pre, ln_g, ln_b, w_s, b_s_col, after=None):
    s, w2 = pre.shape
    w = w2 // 2
    head = w // A_GROUPS

    def body(pre_ref, g_ref, b_ref, ws_ref, bs_ref, o_ref, v_scr):
        mu, rstd = _layernorm_stats(pre_ref, v_scr, w, head)
        for grp in range(A_GROUPS):
            cols = slice(grp * head, (grp + 1) * head)
            vhat = (v_scr[:, cols] - mu) * rstd
            vn = (vhat * g_ref[:, cols] + b_ref[:, cols]).astype(BF16)
            sg = jnp.dot(_masked_spatial(ws_ref, grp), vn, preferred_element_type=F32) + bs_ref[grp]
            o_ref[:, cols] = (_gelu(pre_ref[:, cols]) * sg).astype(o_ref.dtype)

    vec = pl.BlockSpec((1, w), lambda i: (0, 0))
    return _pcall(
        name, body, [pre, ln_g, ln_b, w_s, b_s_col],
        [pl.BlockSpec((CHUNK, w2), lambda i: (i, 0)), vec, vec,
         pl.BlockSpec((A_GROUPS, CHUNK, CHUNK), lambda i: (0, 0, 0)),
         pl.BlockSpec((A_GROUPS, CHUNK, 1), lambda i: (0, 0, 0))],
        _sds((s, w), BF16), pl.BlockSpec((CHUNK, w), lambda i: (i, 0)),
        grid=(s // CHUNK,), sem=("parallel",), scratch=[pltpu.VMEM((CHUNK, w), F32)], after=after)


def _amix_bwd(name, pre, dgated, ln_g, ln_b, w_s, b_s_col, after=None):
    s, w2 = pre.shape
    w = w2 // 2
    head = w // A_GROUPS

    def body(pre_ref, dg_ref, g_ref, b_ref, ws_ref, bs_ref, dpre_ref, glg_ref, glb_ref, gws_ref, gbs_ref,
             v_scr, dvn_scr):
        @pl.when(pl.program_id(0) == 0)
        def _():
            for ref in (glg_ref, glb_ref, gws_ref, gbs_ref):
                ref[...] = jnp.zeros_like(ref)

        mu, rstd = _layernorm_stats(pre_ref, v_scr, w, head)
        mask = _causal_mask()
        sum_dvhat = jnp.zeros((CHUNK, 1), F32)
        sum_dvhat_vhat = jnp.zeros((CHUNK, 1), F32)
        for grp in range(A_GROUPS):
            cols = slice(grp * head, (grp + 1) * head)
            vhat = (v_scr[:, cols] - mu) * rstd
            gain = g_ref[:, cols]
            vn = (vhat * gain + b_ref[:, cols]).astype(BF16)
            wm = _masked_spatial(ws_ref, grp)
            pre_u = pre_ref[:, cols]
            dgated = dg_ref[:, cols]
            ds = dgated * _gelu(pre_u)
            dsb = ds.astype(BF16)
            sg = jnp.dot(wm, vn, preferred_element_type=F32) + bs_ref[grp]
            dpre_ref[:, cols] = (dgated * sg * _gelu_grad(pre_u)).astype(dpre_ref.dtype)
            gws = lax.dot_general(dsb, vn, NT, preferred_element_type=F32)
            gws_ref[grp] += jnp.where(mask, gws, 0.0)
            gbs_ref[grp] += _row_sum(ds)
            dvn = lax.dot_general(wm, dsb, TN, preferred_element_type=F32)
            dvn_scr[:, cols] = dvn
            glg_ref[:, cols] += jnp.sum(dvn * vhat, axis=0, keepdims=True)
            glb_ref[:, cols] += jnp.sum(dvn, axis=0, keepdims=True)
            dvhat = dvn * gain
            sum_dvhat = sum_dvhat + _row_sum(dvhat)
            sum_dvhat_vhat = sum_dvhat_vhat + _row_sum(dvhat * vhat)
        mean_dvhat = sum_dvhat * (1.0 / w)
        mean_dvhat_vhat = sum_dvhat_vhat * (1.0 / w)
        for grp in range(A_GROUPS):
            cols = slice(grp * head, (grp + 1) * head)
            vhat = (v_scr[:, cols] - mu) * rstd
            dvhat = dvn_scr[:, cols] * g_ref[:, cols]
            dv = rstd * (dvhat - mean_dvhat - vhat * mean_dvhat_vhat)
            pre_v = pre_ref[:, w + grp * head:w + (grp + 1) * head]
            dpre_ref[:, w + grp * head:w + (grp + 1) * head] = (dv * _gelu_grad(pre_v)).astype(dpre_ref.dtype)

    vec = pl.BlockSpec((1, w), lambda i: (0, 0))
    ws_spec = pl.BlockSpec((A_GROUPS, CHUNK, CHUNK), lambda i: (0, 0, 0))
    bs_spec = pl.BlockSpec((A_GROUPS, CHUNK, 1), lambda i: (0, 0, 0))
    return _pcall(
        name, body, [pre, dgated, ln_g, ln_b, w_s, b_s_col],
        [pl.BlockSpec((CHUNK, w2), lambda i: (i, 0)), pl.BlockSpec((CHUNK, w), lambda i: (i, 0)),
         vec, vec, ws_spec, bs_spec],
        [_sds((s, w2), BF16), _sds((1, w), F32), _sds((1, w), F32),
         _sds((A_GROUPS, CHUNK, CHUNK), F32), _sds((A_GROUPS, CHUNK, 1), F32)],
        [pl.BlockSpec((CHUNK, w2), lambda i: (i, 0)), vec, vec, ws_spec, bs_spec],
        grid=(s // CHUNK,), sem=("arbitrary",),
        scratch=[pltpu.VMEM((CHUNK, w), F32), pltpu.VMEM((CHUNK, w), F32)], after=after)


def _shift_rows(x, k, forward):
    n = x.shape[0]
    row = lax.broadcasted_iota(jnp.int32, x.shape, 0)
    if forward:
        return jnp.where(row >= k, pltpu.roll(x, k, 0), 0.0)
    return jnp.where(row < n - k, pltpu.roll(x, n - k, 0), 0.0)


def _window_sum(x, window, forward):
    k = 1
    while k < window:
        x = x + _shift_rows(x, k, forward)
        k *= 2
    return x


def _pool(name, v, backward, after=None):
    s, w = v.shape
    head = w // B_GROUPS
    lane = _tile(head, 128)

    def body(v_ref, o_ref):
        grp = pl.program_id(0)
        t = lax.broadcasted_iota(jnp.int32, (s, lane), 0)
        for idx, window in enumerate(B_WINDOWS):
            @pl.when(grp == idx)
            def _():
                inv_count = 1.0 / jnp.minimum(t + 1, window).astype(F32)
                for strip in range(head // lane):
                    cols = slice(strip * lane, (strip + 1) * lane)
                    x = v_ref[:, cols]
                    if backward:
                        out = _window_sum(x * inv_count, window, False) - x
                    else:
                        out = _window_sum(x, window, True) * inv_count - x
                    o_ref[:, cols] = out.astype(o_ref.dtype)

    spec = pl.BlockSpec((s, head), lambda g: (0, g))
    return _pcall(name, body, [v], [spec], _sds((s, w), BF16), spec, grid=(B_GROUPS,),
                  sem=("parallel",), after=after)


def _scaled_dx(name, dy, w, scale, mixed, after=None, tm=1024, tn=1024):
    s, n = dy.shape
    k = w.shape[0]
    tm, tn = _tile(s, tm), _tile(k, tn)

    def body(dy_ref, w_ref, sc_ref, mx_ref, o_ref, gs_ref):
        p = lax.dot_general(dy_ref[...], w_ref[...], NT, preferred_element_type=F32)
        o_ref[...] = (p * sc_ref[...]).astype(o_ref.dtype)
        _accumulate(gs_ref, jnp.sum(p * mx_ref[...].astype(F32), axis=0, keepdims=True), pl.program_id(1))

    tile = pl.BlockSpec((tm, tn), lambda j, i: (i, j))
    vec = pl.BlockSpec((1, tn), lambda j, i: (0, j))
    return _pcall(name, body, [dy, w, scale, mixed],
                  [pl.BlockSpec((tm, n), lambda j, i: (i, 0)), pl.BlockSpec((tn, n), lambda j, i: (j, 0)), vec, tile],
                  [_sds((s, k), BF16), _sds((1, k), F32)], [tile, vec],
                  grid=(k // tn, s // tm), sem=("parallel", "arbitrary"), after=after)


def _adamw(w, g, m, v):
    m = ADAM_B1 * m + (1.0 - ADAM_B1) * g
    v = ADAM_B2 * v + (1.0 - ADAM_B2) * (g * g)
    m_hat = m / (1.0 - ADAM_B1 ** ADAM_STEP)
    v_hat = v / (1.0 - ADAM_B2 ** ADAM_STEP)
    delta = -ADAM_LR * (m_hat / (jnp.sqrt(v_hat) + ADAM_EPS) + ADAM_WD * w)
    return delta, m, v


def _adam_rows(name, g, w, m, v, rows, after=None):
    lanes = g.shape[1]
    n = len(rows)

    def body(g_ref, w_ref, m_ref, v_ref, *out_refs):
        first = 0
        for p, count in enumerate(rows):
            part = slice(first, first + count)
            outs = _adamw(w_ref[part, :], g_ref[part, :], m_ref[part, :], v_ref[part, :])
            for o_ref, o in zip(out_refs[3 * p:3 * p + 3], outs):
                o_ref[...] = o
            first += count

    vmem = pl.BlockSpec(memory_space=pltpu.VMEM)
    return _pcall(name, body, [g, w, m, v], [vmem] * 4,
                  [_sds((count, lanes), F32) for count in rows for _ in range(3)], [vmem] * (3 * n), after=after)


def _position():
    return lax.axis_index("x"), lax.axis_index("y"), lax.axis_index("c")


def _other_chips(x, y):
    return [(1 - x, y), (x, 1 - y), (1 - x, 1 - y)]


def _slot(px, py, pc):
    return 4 * px + 2 * py + pc


def _hbm(a):
    return pltpu.with_memory_space_constraint(a, pltpu.HBM)


def _hop1_copies(srcs, lands, send_sems, recv_sems):
    x, y, c = _position()
    peers = [(x, y, 1 - c), (1 - x, y, c), (x, 1 - y, c)]
    mine = _slot(x, y, c)
    return [[pltpu.make_async_remote_copy(
        src_ref=srcs[t], dst_ref=lands[t].at[mine], send_sem=send_sems[t].at[k], recv_sem=recv_sems[t].at[k],
        device_id=peer, device_id_type=MESH) for k, peer in enumerate(peers)] for t in range(len(srcs))]


def _hop2_copies(lands, send_sems, recv_sems):
    x, y, c = _position()
    routes = [(_slot(1 - x, y, c), (x, 1 - y, c)), (_slot(x, 1 - y, c), (1 - x, y, c))]
    out = []
    for t in range(len(lands)):
        rows = lands[t].shape[1]
        halves = [(0, rows // 2), (rows // 2, rows - rows // 2)]
        per_tensor = []
        for h, ((slot, peer), (start, size)) in enumerate(zip(routes, halves)):
            if size:
                block = lands[t].at[slot, pl.ds(start, size)]
                per_tensor.append(pltpu.make_async_remote_copy(
                    src_ref=block, dst_ref=block, send_sem=send_sems[t].at[h], recv_sem=recv_sems[t].at[h],
                    device_id=peer, device_id_type=MESH))
        for j, (slot, _) in enumerate(routes):
            block = lands[t].at[slot]
            per_tensor.append(pltpu.make_async_remote_copy(
                src_ref=block, dst_ref=block, send_sem=send_sems[t].at[2 + j], recv_sem=recv_sems[t].at[2 + j],
                device_id=(x, y, 1 - c), device_id_type=MESH))
        out.append(per_tensor)
    return out


def _split_start(name, srcs, lands, copies, n_sems, after=None):
    n = len(srcs)
    order = [] if after is None else [after]
    n_in = 2 * n + len(order)

    def body(*refs):
        for per_tensor in copies(refs[:n], refs[n:2 * n], refs[n_in:n_in + n], refs[n_in + n:n_in + 2 * n]):
            for cp in per_tensor:
                cp.start()
        refs[-1][...] = jnp.zeros_like(refs[-1])

    out_shape = ([pltpu.SemaphoreType.DMA((n_sems,)) for _ in range(2 * n)]
                 + [pltpu.HBM(a.shape, a.dtype) for a in list(srcs) + list(lands)]
                 + [_sds((8, 128), F32)])
    out = pl.pallas_call(
        body, name=name, out_shape=out_shape, in_specs=[_HBM] * (2 * n) + [_ANY] * len(order),
        out_specs=[_SEM] * (2 * n) + [_HBM] * (2 * n) + [pl.BlockSpec(memory_space=pltpu.VMEM)],
        input_output_aliases={i: 2 * n + i for i in range(2 * n)},
        compiler_params=pltpu.CompilerParams(has_side_effects=_EFFECT),
    )(*[_hbm(a) for a in srcs], *[_hbm(a) for a in lands], *order)
    return [(out[t], out[n + t], out[2 * n + t], out[3 * n + t]) for t in range(n)], out[-1]


def _split_wait(name, started, copies, after):
    n = len(started)

    def body(*refs):
        for per_tensor in copies(refs[:n], refs[n:2 * n], refs[2 * n:3 * n], refs[3 * n:4 * n]):
            for cp in per_tensor:
                cp.wait_send()
                cp.wait_recv()

    srcs = [e[2] for e in started]
    lands = [e[3] for e in started]
    out = pl.pallas_call(
        body, name=name, out_shape=[pltpu.HBM(a.shape, a.dtype) for a in srcs + lands],
        in_specs=[_HBM] * (2 * n) + [_SEM] * (2 * n) + [_ANY], out_specs=[_HBM] * (2 * n),
        input_output_aliases={i: i for i in range(2 * n)},
        compiler_params=pltpu.CompilerParams(has_side_effects=_EFFECT),
    )(*srcs, *lands, *[e[0] for e in started], *[e[1] for e in started], after)
    return out[:n], out[n:]


def _gather_step(name, arrived, fresh, after=None):
    n, m = len(arrived), len(fresh)
    order = [] if after is None else [after]
    fresh_lands = [lax.empty((N_DEV,) + s.shape, s.dtype) for s in fresh]
    buffers = [e[2] for e in arrived] + [e[3] for e in arrived] + list(fresh) + fresh_lands
    old_sems = [e[0] for e in arrived] + [e[1] for e in arrived]
    n_buf, n_old = len(buffers), len(old_sems)
    first_new = n_buf + n_old + len(order)

    def body(*refs):
        bufs, old = refs[:n_buf], refs[n_buf:n_buf + n_old]
        new = refs[first_new:first_new + 2 * n + 2 * m]
        for per_tensor in _hop1_copies(bufs[:n], bufs[n:2 * n], old[:n], old[n:]):
            for cp in per_tensor:
                cp.wait_send()
                cp.wait_recv()
        second = _hop2_copies(bufs[n:2 * n], new[:n], new[n:2 * n])
        first = _hop1_copies(bufs[2 * n:2 * n + m], bufs[2 * n + m:], new[2 * n:2 * n + m], new[2 * n + m:])
        for per_tensor in second + first:
            for cp in per_tensor:
                cp.start()
        refs[-1][...] = jnp.zeros_like(refs[-1])

    n_new = 2 * n + 2 * m
    out_shape = ([pltpu.SemaphoreType.DMA((4,)) for _ in range(2 * n)]
                 + [pltpu.SemaphoreType.DMA((3,)) for _ in range(2 * m)]
                 + [pltpu.HBM(a.shape, a.dtype) for a in buffers] + [_sds((8, 128), F32)])
    out = pl.pallas_call(
        body, name=name, out_shape=out_shape,
        in_specs=[_HBM] * n_buf + [_SEM] * n_old + [_ANY] * len(order),
        out_specs=[_SEM] * n_new + [_HBM] * n_buf + [pl.BlockSpec(memory_space=pltpu.VMEM)],
        input_output_aliases={i: n_new + i for i in range(n_buf)},
        compiler_params=pltpu.CompilerParams(has_side_effects=_EFFECT),
    )(*[_hbm(a) for a in buffers], *old_sems, *order)
    sems, bufs = out[:n_new], out[n_new:n_new + n_buf]
    second = [(sems[t], sems[n + t], bufs[t], bufs[n + t]) for t in range(n)]
    first = [(sems[2 * n + t], sems[2 * n + m + t], bufs[2 * n + t], bufs[2 * n + m + t]) for t in range(m)]
    return second, first, out[-1]


def _gather_wait(name, second, after):
    return _split_wait(name, second, lambda srcs, lands, send, recv: _hop2_copies(lands, send, recv), after)


def _sibling_copies(srcs, lands, send_sems, recv_sems):
    x, y, c = _position()
    return [[pltpu.make_async_remote_copy(
        src_ref=srcs[t], dst_ref=lands[t], send_sem=send_sems[t].at[0], recv_sem=recv_sems[t].at[0],
        device_id=(x, y, 1 - c), device_id_type=MESH)] for t in range(len(srcs))]


def _sibling_and_scatter_start(name, arrays, partials):
    k = len(arrays)

    def copies(srcs, lands, send_sems, recv_sems):
        return (_sibling_copies(srcs[:k], lands[:k], send_sems[:k], recv_sems[:k])
                + _scatter_copies(srcs[k:], lands[k:], send_sems[k:], recv_sems[k:]))

    lands = ([lax.empty(a.shape, a.dtype) for a in arrays]
             + [lax.empty((N_CHIPS - 1,) + p.shape[1:], p.dtype) for p in partials])
    return _split_start(name, list(arrays) + list(partials), lands, copies, 3)


def _sibling_wait(name, started, after):
    return _split_wait(name, started, _sibling_copies, after)[1]


def _small_copies(srcs, lands, send_sems, recv_sems):
    x, y, c = _position()
    mine = _slot(x, y, c)
    peers = [(x ^ ((k >> 2) & 1), y ^ ((k >> 1) & 1), c ^ (k & 1)) for k in range(1, N_DEV)]
    return [[pltpu.make_async_remote_copy(
        src_ref=srcs[t], dst_ref=lands[t].at[mine], send_sem=send_sems[t].at[k], recv_sem=recv_sems[t].at[k],
        device_id=peer, device_id_type=MESH) for k, peer in enumerate(peers)] for t in range(len(srcs))]


def _gather_finish(name, shards, lands, after):
    n = len(shards)

    def body(*refs):
        srcs, lands_in, outs = refs[:n], refs[n:2 * n], refs[2 * n:3 * n]
        send_sems, recv_sems, local_sems = refs[3 * n:]
        x, y, c = _position()
        local = [pltpu.make_async_copy(srcs[t], outs[t].at[_slot(x, y, c)], local_sems.at[t]) for t in range(n)]

        def diagonal(t, core):
            block = outs[t].at[_slot(1 - x, 1 - y, core)]
            return pltpu.make_async_remote_copy(
                src_ref=block, dst_ref=block, send_sem=send_sems.at[t], recv_sem=recv_sems.at[t],
                device_id=(x, y, 1 - c), device_id_type=MESH)

        for cp in local:
            cp.start()
        for t in range(n):
            diagonal(t, c).start()
        for t in range(n):
            diagonal(t, c).wait_send()
            diagonal(t, 1 - c).wait_recv()
        for cp in local:
            cp.wait()

    return _pcall(name, body, [*shards, *lands], [_ANY] * (2 * n),
                  [_sds(l.shape, l.dtype) for l in lands], [_ANY] * n,
                  scratch=[pltpu.SemaphoreType.DMA((n,)), pltpu.SemaphoreType.DMA((n,)),
                           pltpu.SemaphoreType.DMA((n,))],
                  after=after, aliases={n + t: t for t in range(n)})


def _exchange_sibling(name, fulls, after):
    n = len(fulls)

    def body(*refs):
        src = refs[:n]
        out = refs[n:2 * n]
        send_sems, recv_sems = refs[2 * n:]
        x, y, c = _position()
        copies = [pltpu.make_async_remote_copy(
            src_ref=src[t].at[:, 1 - c], dst_ref=out[t], send_sem=send_sems.at[t], recv_sem=recv_sems.at[t],
            device_id=(x, y, 1 - c), device_id_type=MESH) for t in range(n)]
        for cp in copies:
            cp.start()
        for cp in copies:
            cp.wait()

    return _pcall(name, body, fulls, [_ANY] * n, [_sds((N_CHIPS,) + f.shape[2:], f.dtype) for f in fulls],
                  [_ANY] * n, scratch=[pltpu.SemaphoreType.DMA((n,)), pltpu.SemaphoreType.DMA((n,))],
                  after=after)


def _add_sibling(name, full, recv, core, after):
    _, _, r, c = full.shape
    tr = _tile(r, max(8, (256 * 1024) // c))

    def body(core_ref, f_ref, r_ref, o_ref):
        o_ref[...] = (f_ref[...].astype(F32) + r_ref[...].astype(F32)).astype(o_ref.dtype)

    return _pcall(
        name, body, [full, recv],
        [pl.BlockSpec((None, None, tr, c), lambda p, i, core_ref: (p, core_ref[0], i, 0)),
         pl.BlockSpec((None, tr, c), lambda p, i, core_ref: (p, i, 0))],
        _sds((N_CHIPS, r, c), BF16), pl.BlockSpec((None, tr, c), lambda p, i, core_ref: (p, i, 0)),
        grid=(N_CHIPS, r // tr), sem=("parallel", "parallel"), prefetch=[core], after=after)


def _scatter_copies(srcs, lands, send_sems, recv_sems):
    x, y, c = _position()
    return [[pltpu.make_async_remote_copy(
        src_ref=srcs[t].at[2 * px + py], dst_ref=lands[t].at[j],
        send_sem=send_sems[t].at[j], recv_sem=recv_sems[t].at[j],
        device_id=(px, py, c), device_id_type=MESH) for j, (px, py) in enumerate(_other_chips(x, y))]
        for t in range(len(srcs))]


def _scatter_wait(name, started, after):
    return _split_wait(name, started, _scatter_copies, after)


class _Job:
    def __init__(self, **fields):
        self.__dict__.update(fields)


def _reduce_adam_job(partial, recv, chip, w, m, v, layer, carried):
    n_layers, r, c = w.shape
    tr = _tile(r, max(8, (256 * 1024) // c))

    def body(p_ref, r_ref, w_ref, m_ref, v_ref, *rest):
        g_ref, d_ref, nm_ref, nv_ref = rest[-4:]
        g = p_ref[...].astype(F32)
        for j in range(N_CHIPS - 1):
            g = g + r_ref[j].astype(F32)
        g_ref[...] = g
        d_ref[...], nm_ref[...], nv_ref[...] = _adamw(w_ref[...], g, m_ref[...], v_ref[...])

    layered = ((None, tr, c), lambda blk, chip_ref: (layer, blk, 0))
    in_specs = [((None, tr, c), lambda blk, chip_ref: (chip_ref[0], blk, 0)),
                ((N_CHIPS - 1, tr, c), lambda blk, chip_ref: (0, blk, 0)), layered, layered, layered]
    operands = [partial, recv, w, m, v]
    aliases = {}
    if carried is not None:
        operands += list(carried)
        in_specs += [None] * 4
        aliases = {5 + o: o for o in range(4)}
    return _Job(operands=operands, in_specs=in_specs, out_shape=[_sds((n_layers, r, c), F32)] * 4,
                out_specs=[layered] * 4, body=body, aliases=aliases, prefetch=chip, n_blocks=r // tr)


def _run_job(name, job, after):
    def spec(entry):
        if entry is None:
            return _ANY
        shape, index = entry
        return pl.BlockSpec(shape, lambda blk, pre, index=index: index(blk, pre))

    def body(pre_ref, *refs):
        job.body(*refs)

    return _pcall(name, body, job.operands, [spec(e) for e in job.in_specs], job.out_shape,
                  [spec(e) for e in job.out_specs], grid=(job.n_blocks,), sem=("parallel",),
                  prefetch=[job.prefetch], after=after,
                  aliases={1 + i: o for i, o in job.aliases.items()})


def _small_sum(name, gathered, own, device, after=None):
    r, lanes = own.shape

    def body(dev_ref, g_ref, own_ref, out_ref):
        dev = dev_ref[0]
        mine = own_ref[...]
        total = jnp.where(dev == 0, mine, g_ref[0])
        for d in range(1, N_DEV):
            total = total + jnp.where(dev == d, mine, g_ref[d])
        out_ref[...] = total

    return _pcall(name, body, [gathered, own],
                  [pl.BlockSpec((N_DEV, r, lanes), lambda i, dev_ref: (0, 0, 0)),
                   pl.BlockSpec((r, lanes), lambda i, dev_ref: (0, 0))],
                  _sds((r, lanes), F32), pl.BlockSpec((r, lanes), lambda i, dev_ref: (0, 0)),
                  grid=(1,), sem=("arbitrary",), prefetch=[device], after=after)


def _pack(arrays):
    return jnp.concatenate([a.reshape(-1, 128) for a in arrays], axis=0)


def _unpack(packed, shapes):
    out, row = [], 0
    for shape in shapes:
        rows = math.prod(shape) // 128
        out.append(packed[row:row + rows].reshape(shape))
        row += rows
    return out


class _Order:
    def __init__(self):
        self.last = None

    def __call__(self, fn, *args, **kwargs):
        out = fn(*args, after=self.last, **kwargs)
        self.last = out[0] if isinstance(out, (list, tuple)) else out
        return out


def kernel(x, a_w_in, a_ln_g, a_ln_b, a_w_s, a_b_s, a_w_out, b_w_in, b_w_grp, b_scale, b_w_out, norm_mix, norm_mlp, mlp_w1, mlp_w2, final_norm, loss_target, m_a_w_in, m_a_ln_g, m_a_ln_b, m_a_w_s, m_a_b_s, m_a_w_out, m_b_w_in, m_b_w_grp, m_b_scale, m_b_w_out, m_norm_mix, m_norm_mlp, m_mlp_w1, m_mlp_w2, m_final_norm, v_a_w_in, v_a_ln_g, v_a_ln_b, v_a_w_s, v_a_b_s, v_a_w_out, v_b_w_in, v_b_w_grp, v_b_scale, v_b_w_out, v_norm_mix, v_norm_mlp, v_mlp_w1, v_mlp_w2, v_final_norm):
    s, d = x.shape[1], x.shape[2]
    depth = mlp_w1.shape[0]
    a_slab = a_w_in.shape[2]
    ff_slab = mlp_w1.shape[2]
    ff_rows = mlp_w2.shape[1]
    bh = b_w_grp.shape[3]
    my_x, my_y, my_c = _position()
    core = jnp.reshape(my_c, (1,)).astype(jnp.int32)
    chip = jnp.reshape(2 * my_x + my_y, (1,)).astype(jnp.int32)
    device = _slot(my_x, my_y, my_c)
    run = _Order()

    w1_b, w2_b = mlp_w1.astype(BF16), mlp_w2.astype(BF16)
    shards = [a_w_in[0].astype(BF16), a_w_out[0].astype(BF16), b_scale,
              w1_b[0], w2_b[0],
              b_w_in[0].astype(BF16), b_w_grp[0].astype(BF16), b_w_out[0].astype(BF16),
              w1_b[1], w2_b[1]]
    groups = [[0], [1, 2], [3], [4], [5, 6, 7], [8], [9]]
    start_with = {1: [2, 3], 2: [4], 3: [5], 4: [6]}
    hop1, hop2 = {}, {}
    _, hop1[0], token = _gather_step("weights_group0_hop1", [], [shards[t] for t in groups[0]])
    _, hop1[1], token = _gather_step("weights_group1_hop1", [], [shards[t] for t in groups[1]], token)
    run.last = token

    def advance(g):
        if g not in hop1:
            return
        ahead = start_with.get(g, [])
        fresh = [shards[t] for a in ahead for t in groups[a]]
        hop2[g], started, tok = _gather_step(f"weights_group{g}_hop2", hop1.pop(g), fresh, run.last)
        for a in ahead:
            hop1[a], started = started[:len(groups[a])], started[len(groups[a]):]
        run.last = tok

    def gathered(g):
        advance(g)
        if g == 0:
            advance(1)
        srcs, lands = _gather_wait(f"weights_group{g}_wait", hop2.pop(g), run.last)
        run.last = srcs[0]
        return run(_gather_finish, f"weights_group{g}_finish", srcs, lands)

    h0 = x[0]
    target = loss_target[0]
    ln_g, ln_b = a_ln_g, a_ln_b
    w_s = a_w_s[0]
    b_s_col = a_b_s[0][:, :, None]
    nmix = [norm_mix[l][None, :] for l in range(depth)]
    nmlp = [norm_mlp[l][None, :] for l in range(depth)]

    def mlp_forward(l, h, up_group):
        hn = run(_rms_fwd, f"mlp{l}_norm", h, nmlp[l])
        (w1,) = gathered(up_group)
        advance(up_group + 1)
        act, act_sq = run(_mm_nn, f"mlp{l}_up", hn, w1,
                          lambda acc: (jnp.maximum(acc, 0.0), jnp.square(jnp.maximum(acc, 0.0))),
                          (BF16, BF16), slab=True)
        (w2,) = gathered(up_group + 1)
        advance(up_group + 2)
        w2 = w2.reshape(-1, d)
        (h_out,) = run(_mm_nn, f"mlp{l}_down", act_sq, w2, lambda acc, res: (acc + res,), (F32,),
                       extras=(h,), extra_kinds=("tile",))
        return h_out, (h, hn, act, act_sq, w1, w2)

    scattered = []

    pending = []

    def scatter_partials(name, partials, specs):
        pending.append((name, partials, specs))

    def start_exchanges(name, to_sibling):
        partials = [p for _, group, _ in pending for p in group]
        in_flight, tok = _sibling_and_scatter_start(name, to_sibling, partials)
        run.last = tok
        first = len(to_sibling)
        for group_name, group, specs in pending:
            scattered.append((group_name, in_flight[first:first + len(group)], specs))
            first += len(group)
        pending.clear()
        return in_flight[:len(to_sibling)]

    weights = {"a_w_in": (a_w_in, m_a_w_in, v_a_w_in), "a_w_out": (a_w_out, m_a_w_out, v_a_w_out),
               "b_w_in": (b_w_in, m_b_w_in, v_b_w_in), "b_w_grp": (b_w_grp, m_b_w_grp, v_b_w_grp),
               "b_w_out": (b_w_out, m_b_w_out, v_b_w_out), "mlp_w1": (mlp_w1, m_mlp_w1, v_mlp_w1),
               "mlp_w2": (mlp_w2, m_mlp_w2, v_mlp_w2)}
    results = {}

    def finish_group(name, in_flight, specs):
        partials, lands = _scatter_wait(name + "_scatter_wait", in_flight, run.last)
        run.last = lands[0]
        for t, (wname, layer) in enumerate(specs):
            w, m, v = weights[wname]
            shape = (w.shape[0],) + partials[t].shape[1:]
            job = _reduce_adam_job(partials[t], lands[t], chip, w.reshape(shape), m.reshape(shape),
                                   v.reshape(shape), layer, results.get(wname))
            results[wname] = run(_run_job, f"{name}_reduce_adam_{t}", job)

    def weight_grad(name, a, b, by_rows, block, between, tm=1024, tn=1024):
        other = run(_mm_tn_half, name + "_other", a, b, core, False, by_rows, block, tm=tm, tn=tn)
        sent = start_exchanges(name + "_sibling_start", [other])
        middle = between()
        (recv,) = _sibling_wait(name + "_sibling_wait", sent, run.last)
        run.last = recv
        return run(_mm_tn_half, name + "_own", a, b, core, True, by_rows, block, recv=recv, tm=tm, tn=tn), middle

    def mlp_backward(l, saved, dh, dhb):
        h, hn, act, act_sq, w1, w2 = saved
        part_w2, (dpre,) = weight_grad(
            f"mlp{l}_down_dw", act_sq, dhb, True, ff_rows,
            lambda: run(_mm_nt, f"mlp{l}_down_dx", dhb, w2, lambda acc, a: (2.0 * a.astype(F32) * acc,),
                        (BF16,), extras=(act,), extra_kinds=("tile",)))
        scatter_partials(f"mlp{l}_down_grads", [part_w2], [("mlp_w2", l)])
        part_w1, (dhn,) = weight_grad(
            f"mlp{l}_up_dw", hn, dpre, False, ff_slab,
            lambda: run(_mm_nt, f"mlp{l}_up_dx", dpre, w1, lambda acc: (acc,), (F32,), slab=True))
        scatter_partials(f"mlp{l}_up_grads", [part_w1], [("mlp_w1", l)])
        dh, dhb, g_norm = run(_rms_bwd, f"mlp{l}_norm_bwd", dhn, h, nmlp[l], dh)
        return dh, dhb, g_norm

    hn0 = run(_rms_fwd, "mix0_norm", h0, nmix[0])
    (wa_in,) = gathered(0)
    (pre,) = run(_mm_nn, "mixa_in", hn0, wa_in, lambda acc: (acc,), (F32,), slab=True)
    wa_out, scale = gathered(1)
    wa_out, scale = wa_out.reshape(d, d), scale.reshape(1, d)
    gated = run(_amix_fwd, "mixa_gate", pre, ln_g, ln_b, w_s, b_s_col)
    advance(2)
    (h1,) = run(_mm_nn, "mixa_out", gated, wa_out, lambda acc, res: (acc + res,), (F32,),
                extras=(h0,), extra_kinds=("tile",))
    h2, saved_mlp0 = mlp_forward(0, h1, 2)
    hn2 = run(_rms_fwd, "mix1_norm", h2, nmix[1])
    wb_in, wb_grp, wb_out = gathered(4)
    advance(5)
    wb_in, wb_out = wb_in.reshape(d, d), wb_out.reshape(d, d)
    wb_grp = jnp.transpose(wb_grp, (1, 0, 2, 3)).reshape(B_GROUPS, bh, bh)
    (vb,) = run(_mm_nn, "mixb_in", hn2, wb_in, lambda acc: (acc,), (F32,))
    pooled = run(_pool, "mixb_pool", vb, backward=False)
    tm = _tile(s, 1024)
    grp_tile = pl.BlockSpec((tm, bh), lambda i, j, k: (i, j))
    grp_weight = pl.BlockSpec((None, bh, bh), lambda i, j, k: (j, 0, 0))
    mixed, mixed_scaled = run(
        _matmul, "mixb_grp", pooled, wb_grp, NN, (s // tm, B_GROUPS, 1), grp_tile, grp_weight,
        [_sds((s, d), BF16), _sds((s, d), BF16)], [grp_tile] * 2,
        (tm, bh), lambda acc, sc: (acc, acc * sc), (scale,), [pl.BlockSpec((1, bh), lambda i, j, k: (0, j))])
    (h3,) = run(_mm_nn, "mixb_out", mixed_scaled, wb_out, lambda acc, res: (acc + res,), (F32,),
                extras=(h2,), extra_kinds=("tile",))
    h4, saved_mlp1 = mlp_forward(1, h3, 5)
    dh, dhb, g_final, loss_part = run(_loss_head, "loss_head", h4, final_norm[None, :], target)

    dh, dhb, g_nmlp1 = mlp_backward(1, saved_mlp1, dh, dhb)
    tks = _tile(s, 1024)
    grp_rows = pl.BlockSpec((tks, bh), lambda i, j, k: (k, j))

    def mixb_middle():
        dms_scaled, g_scale = run(_scaled_dx, "mixb_out_dx", dhb, wb_out, scale, mixed)
        (g_wb_grp,) = run(
            _matmul, "mixb_grp_dw", pooled, dms_scaled, TN, (1, B_GROUPS, s // tks), grp_rows, grp_rows,
            [_sds((B_GROUPS, bh, bh), BF16)], [grp_weight], (bh, bh), lambda acc: (acc,))
        (dpooled,) = run(
            _matmul, "mixb_grp_dx", dms_scaled, wb_grp, NT, (s // tm, B_GROUPS, 1), grp_tile, grp_weight,
            [_sds((s, d), F32)], [grp_tile], (tm, bh), lambda acc: (acc,))
        return g_scale, g_wb_grp, run(_pool, "mixb_pool_bwd", dpooled, backward=True)

    part_wb_out, (g_scale, g_wb_grp, dvb) = weight_grad("mixb_out_dw", mixed_scaled, dhb, True, d // N_DEV,
                                                        mixb_middle, tn=d)
    part_wb_in, (dhn2,) = weight_grad(
        "mixb_in_dw", hn2, dvb, True, d // N_DEV,
        lambda: run(_mm_nt, "mixb_in_dx", dvb, wb_in, lambda acc: (acc,), (F32,)), tn=d)
    grp_full = jnp.transpose(g_wb_grp.reshape(B_GROUPS, N_DEV, bh // N_DEV, bh), (1, 0, 2, 3))
    grp_full = grp_full.reshape(N_CHIPS, 2, B_GROUPS * bh // N_DEV, bh)
    (grp_sibling,) = run(_exchange_sibling, "mixb_grp_dw_to_sibling", [grp_full])
    part_wb_grp = run(_add_sibling, "mixb_grp_dw_add_sibling", grp_full, grp_sibling, core)
    scatter_partials("mixb_grads", [part_wb_out, part_wb_grp, part_wb_in],
                     [("b_w_out", 0), ("b_w_grp", 0), ("b_w_in", 0)])
    dh, dhb, g_nmix1 = run(_rms_bwd, "mix1_norm_bwd", dhn2, h2, nmix[1], dh)
    dh, dhb, g_nmlp0 = mlp_backward(0, saved_mlp0, dh, dhb)
    def mixa_middle():
        (dgated,) = run(_mm_nt, "mixa_out_dx", dhb, wa_out, lambda acc: (acc,), (F32,))
        return run(_amix_bwd, "mixa_gate_bwd", pre, dgated, ln_g, ln_b, w_s, b_s_col)

    part_wa_out, (dpre, g_ln_g, g_ln_b, g_w_s, g_b_s) = weight_grad("mixa_out_dw", gated, dhb, True, d // N_DEV,
                                                                     mixa_middle, tn=d)
    part_wa_in, (dhn0,) = weight_grad(
        "mixa_in_dw", hn0, dpre, False, a_slab,
        lambda: run(_mm_nt, "mixa_in_dx", dpre, wa_in, lambda acc: (acc,), (F32,), slab=True), tm=d)
    scatter_partials("mixa_grads", [part_wa_in, part_wa_out], [("a_w_in", 0), ("a_w_out", 0)])
    start_exchanges("mixa_grads_scatter_start", [])
    grad_x, _, g_nmix0 = run(_rms_bwd, "mix0_norm_bwd", dhn0, h0, nmix[0], dh)

    g_norm_mix = jnp.concatenate([g_nmix0, g_nmix1], axis=0)
    g_norm_mlp = jnp.concatenate([g_nmlp0, g_nmlp1], axis=0)
    loss_row = jnp.pad(loss_part, ((0, 0), (0, 127)))
    small_parts = [g_ln_g, g_ln_b, g_w_s, g_b_s, g_norm_mix, g_norm_mlp, g_final, g_scale, loss_row]
    packed = _pack(small_parts)
    small_sent, tok = _split_start("small_grads_start", [packed], [lax.empty((N_DEV,) + packed.shape, F32)],
                                   _small_copies, N_DEV - 1)
    run.last = tok

    for group in scattered:
        finish_group(*group)
    own_packed, small_gathered = _split_wait("small_grads_wait", small_sent, _small_copies, run.last)
    run.last = small_gathered[0]
    small_sum = run(_small_sum, "small_grads_sum", small_gathered[0], own_packed[0],
                    jnp.reshape(device, (1,)).astype(jnp.int32))
    sg = _unpack(small_sum, [a_ln_g.shape, a_ln_b.shape, a_w_s.shape, a_b_s.shape, norm_mix.shape,
                             norm_mlp.shape, final_norm.shape, (1, d), (1, 128)])
    loss = sg.pop()[0, 0]
    shard = b_scale.shape[1]
    sg[7] = lax.dynamic_slice(sg[7], (0, device * shard), (1, shard))
    small_w = [a_ln_g, a_ln_b, a_w_s, a_b_s, norm_mix, norm_mlp, final_norm, b_scale]
    small_m = [m_a_ln_g, m_a_ln_b, m_a_w_s, m_a_b_s, m_norm_mix, m_norm_mlp, m_final_norm, m_b_scale]
    small_v = [v_a_ln_g, v_a_ln_b, v_a_w_s, v_a_b_s, v_norm_mix, v_norm_mlp, v_final_norm, v_b_scale]
    small_out = run(_adam_rows, "small_adam", _pack(sg), _pack(small_w), _pack(small_m), _pack(small_v),
                    [w.size // 128 for w in small_w])
    small_res = [sg] + [[small_out[3 * p + o].reshape(w.shape) for p, w in enumerate(small_w)]
                        for o in range(3)]
    big = {wname: [o.reshape(weights[wname][0].shape) for o in outs] for wname, outs in results.items()}

    def leaf(o):
        return (big["a_w_in"][o], small_res[o][0], small_res[o][1], small_res[o][2], small_res[o][3],
                big["a_w_out"][o], big["b_w_in"][o], big["b_w_grp"][o], small_res[o][7], big["b_w_out"][o],
                small_res[o][4], small_res[o][5], big["mlp_w1"][o], big["mlp_w2"][o], small_res[o][6])

    return (loss, grad_x[None], *leaf(0), *leaf(1), *leaf(2), *leaf(3))
```

```python
import math

import jax
import jax.numpy as jnp
from jax import lax
from jax.experimental import pallas as pl
from jax.experimental.pallas import tpu as pltpu

F32 = jnp.float32
BF16 = jnp.bfloat16
MESH = pl.DeviceIdType.MESH

N_DEV = 8
N_CHIPS = 4
CHUNK = 128
A_GROUPS = 8
B_WINDOWS = (2, 4, 8, 16)
B_GROUPS = len(B_WINDOWS)
EPS = 1e-6
ADAM_LR = 0.001
ADAM_B1 = 0.9
ADAM_B2 = 0.999
ADAM_EPS = 1e-08
ADAM_WD = 0.01
ADAM_STEP = 10

VMEM_LIMIT = 48 * 1024 * 1024
ROW_CHUNK = 16

NN = (((1,), (0,)), ((), ()))
NT = (((1,), (1,)), ((), ()))
TN = (((0,), (0,)), ((), ()))

_ANY = pl.BlockSpec(memory_space=pl.ANY)
_HBM = pl.BlockSpec(memory_space=pltpu.HBM)
_SEM = pl.BlockSpec(memory_space=pltpu.SEMAPHORE)
_EFFECT = pltpu.SideEffectType.DATAFLOW_SIDE_EFFECTING


def _tile(n, pref):
    return pref if n % pref == 0 else n


def _sds(shape, dtype):
    return jax.ShapeDtypeStruct(shape, dtype)


def _pcall(name, body, operands, in_specs, out_shape, out_specs, *, grid=None, sem=None, scratch=(),
           prefetch=(), after=None, aliases=None):
    after = [] if after is None else [after]
    n_lead = len(prefetch) + len(operands)
    n_after = len(after)

    def wrapped(*refs):
        body(*refs[:n_lead], *refs[n_lead + n_after:])

    in_specs = list(in_specs) + [_ANY] * n_after
    params = pltpu.CompilerParams(vmem_limit_bytes=VMEM_LIMIT) if sem is None else \
        pltpu.CompilerParams(dimension_semantics=sem, vmem_limit_bytes=VMEM_LIMIT)
    kwargs = dict(out_shape=out_shape, scratch_shapes=list(scratch), compiler_params=params, name=name,
                  input_output_aliases=aliases or {})
    if prefetch:
        kwargs["grid_spec"] = pltpu.PrefetchScalarGridSpec(
            num_scalar_prefetch=len(prefetch), grid=grid, in_specs=in_specs, out_specs=out_specs,
            scratch_shapes=list(scratch))
        kwargs.pop("scratch_shapes")
    else:
        kwargs.update(in_specs=in_specs, out_specs=out_specs)
        if grid is not None:
            kwargs["grid"] = grid
    return pl.pallas_call(wrapped, **kwargs)(*prefetch, *operands, *after)


def _matmul(name, a, b, dims, grid, a_spec, b_spec, out_shape, out_specs, acc_shape,
            epilogue, extras=(), extra_specs=(), after=None, prefetch=(), b_parts=1):
    nk = grid[2]
    n_extra = len(extras)
    n_out = len(out_shape)
    n_pre = len(prefetch)

    def body(*refs):
        refs = refs[n_pre:]
        a_ref, b_ref = refs[0], refs[1]
        extra_refs = refs[2:2 + n_extra]
        out_refs = refs[2 + n_extra:2 + n_extra + n_out]

        def finish(acc):
            outs = epilogue(acc, *[r[...] for r in extra_refs])
            for o_ref, o in zip(out_refs, outs):
                o_ref[...] = o.astype(o_ref.dtype)

        def product():
            if b_parts == 1:
                return lax.dot_general(a_ref[...], b_ref[...], dims, preferred_element_type=F32)
            width = b_ref.shape[2]
            total = None
            for p in range(b_parts):
                part = lax.dot_general(a_ref[:, p * width:(p + 1) * width], b_ref[p], dims,
                                       preferred_element_type=F32)
                total = part if total is None else total + part
            return total

        if nk == 1:
            finish(product())
        else:
            acc_ref = refs[-1]
            k = pl.program_id(2)

            @pl.when(k == 0)
            def _():
                acc_ref[...] = product()

            if nk > 2:
                @pl.when(jnp.logical_and(k > 0, k < nk - 1))
                def _():
                    acc_ref[...] += product()

            @pl.when(k == nk - 1)
            def _():
                finish(acc_ref[...] + product())

    scratch = [] if nk == 1 else [pltpu.VMEM(acc_shape, F32)]
    return _pcall(name, body, [a, b, *extras], [a_spec, b_spec, *extra_specs], out_shape, out_specs,
                  grid=grid, sem=("parallel", "parallel", "arbitrary"), scratch=scratch, after=after,
                  prefetch=prefetch)


def _mm_nn(name, a, b, epilogue, out_dtypes, extras=(), extra_kinds=(), slab=False, after=None,
           tm=1024, tn=1024, tk=2048):
    m, kd = a.shape
    if slab:
        n_slab, _, w = b.shape
        n = n_slab * w
        tn = _tile(w, min(tn, w))
        per = w // tn
        tk = _tile(kd, tk)
        b_spec = pl.BlockSpec((None, tk, tn), lambda i, j, k: (j // per, k, j % per))
    else:
        n = b.shape[1]
        tn = _tile(n, tn)
        tk = _tile(kd, tk)
        b_spec = pl.BlockSpec((tk, tn), lambda i, j, k: (k, j))
    tm = _tile(m, tm)
    grid = (m // tm, n // tn, kd // tk)
    a_spec = pl.BlockSpec((tm, tk), lambda i, j, k: (i, k))
    tile_spec = pl.BlockSpec((tm, tn), lambda i, j, k: (i, j))
    row_spec = pl.BlockSpec((1, tn), lambda i, j, k: (0, j))
    extra_specs = [tile_spec if kind == "tile" else row_spec for kind in extra_kinds]
    return _matmul(name, a, b, NN, grid, a_spec, b_spec,
                   [_sds((m, n), d) for d in out_dtypes], [tile_spec for _ in out_dtypes],
                   (tm, tn), epilogue, extras, extra_specs, after=after)


def _mm_nt(name, a, b, epilogue, out_dtypes, extras=(), extra_kinds=(), slab=False, after=None,
           tm=1024, tn=1024, tk=2048):
    m, kd = a.shape
    parts = 1
    if slab:
        n_slab, n, w = b.shape
        tn = _tile(n, tn)
        if tk > w and tk % w == 0 and n_slab % (tk // w) == 0:
            parts = tk // w
            b_spec = pl.BlockSpec((parts, tn, w), lambda i, j, k, *_: (k, j, 0))
        else:
            tk = _tile(w, min(tk, w))
            per = w // tk
            b_spec = pl.BlockSpec((None, tn, tk), lambda i, j, k, *_: (k // per, j, k % per))
    else:
        n = b.shape[0]
        tn = _tile(n, tn)
        tk = _tile(kd, tk)
        b_spec = pl.BlockSpec((tn, tk), lambda i, j, k, *_: (j, k))
    tm = _tile(m, tm)
    grid = (m // tm, n // tn, kd // tk)
    a_spec = pl.BlockSpec((tm, tk), lambda i, j, k, *_: (i, k))
    tile_spec = pl.BlockSpec((tm, tn), lambda i, j, k, *_: (i, j))
    row_spec = pl.BlockSpec((1, tn), lambda i, j, k, *_: (0, j))
    extra_specs = [tile_spec if kind == "tile" else row_spec for kind in extra_kinds]
    return _matmul(name, a, b, NT, grid, a_spec, b_spec,
                   [_sds((m, n), d) for d in out_dtypes], [tile_spec for _ in out_dtypes],
                   (tm, tn), epilogue, extras, extra_specs, after=after, b_parts=parts)


def _mm_tn_half(name, a, b, core, own, by_rows, block, recv=None, after=None, tm=1024, tn=1024, tk=2048):
    s, m = a.shape
    n = b.shape[1]
    tk = _tile(s, tk)

    def owner(chip, core_ref):
        return 2 * chip + (core_ref[0] if own else 1 - core_ref[0])

    if by_rows:
        r, c = block, n
        tm, tn = _tile(r, min(tm, r)), _tile(c, tn)
        per = r // tm
        grid = (N_CHIPS * per, c // tn, s // tk)
        a_spec = pl.BlockSpec((tk, tm), lambda i, j, k, cr, *_: (k, owner(i // per, cr) * per + i % per))
        b_spec = pl.BlockSpec((tk, tn), lambda i, j, k, cr, *_: (k, j))
        o_spec = pl.BlockSpec((None, tm, tn), lambda i, j, k, cr, *_: (i // per, i % per, j))
    else:
        r, c = m, block
        tm, tn = _tile(r, tm), _tile(c, min(tn, c))
        per = c // tn
        grid = (r // tm, N_CHIPS * per, s // tk)
        a_spec = pl.BlockSpec((tk, tm), lambda i, j, k, cr, *_: (k, i))
        b_spec = pl.BlockSpec((tk, tn), lambda i, j, k, cr, *_: (k, owner(j // per, cr) * per + j % per))
        o_spec = pl.BlockSpec((None, tm, tn), lambda i, j, k, cr, *_: (j // per, i, j % per))
    if recv is None:
        extras, epilogue = (), lambda acc: (acc,)
    else:
        extras, epilogue = (recv,), lambda acc, other: (acc + other.astype(F32),)
    return _matmul(name, a, b, TN, grid, a_spec, b_spec, [_sds((N_CHIPS, r, c), BF16)], [o_spec], (tm, tn),
                   epilogue, extras, [o_spec] * len(extras), after=after, prefetch=[core])[0]


def _rms_fwd(name, h, g, after=None):
    s, d = h.shape
    tr = _tile(s, 512)

    def body(h_ref, g_ref, o_ref):
        x = h_ref[...]
        r = lax.rsqrt(jnp.mean(x * x, axis=-1, keepdims=True) + EPS)
        o_ref[...] = (x * r * g_ref[...]).astype(o_ref.dtype)

    row = pl.BlockSpec((tr, d), lambda i: (i, 0))
    vec = pl.BlockSpec((1, d), lambda i: (0, 0))
    return _pcall(name, body, [h, g], [row, vec], _sds((s, d), BF16), row, grid=(s // tr,),
                  sem=("parallel",), after=after)


def _accumulate(ref, part, step):
    @pl.when(step == 0)
    def _():
        ref[...] = part

    @pl.when(step > 0)
    def _():
        ref[...] += part


def _rms_bwd(name, dhn, h, g, dres, after=None):
    s, d = h.shape
    tr = _tile(s, 512)
    steps = s // tr

    def body(dhn_ref, h_ref, g_ref, dres_ref, dh_ref, dhb_ref, gp_ref, acc_ref):
        step = pl.program_id(0)

        @pl.when(step == 0)
        def _():
            acc_ref[...] = jnp.zeros_like(acc_ref)

        gain = g_ref[...]

        def chunk(i, carry):
            rows = pl.ds(pl.multiple_of(i * ROW_CHUNK, ROW_CHUNK), ROW_CHUNK)
            x = h_ref[rows, :]
            r = lax.rsqrt(jnp.mean(x * x, axis=-1, keepdims=True) + EPS)
            n = x * r
            dy = dhn_ref[rows, :]
            dn = dy * gain
            dh = dres_ref[rows, :] + r * (dn - n * jnp.mean(dn * n, axis=-1, keepdims=True))
            dh_ref[rows, :] = dh
            dhb_ref[rows, :] = dh.astype(BF16)
            acc_ref[...] += dy * n
            return carry

        lax.fori_loop(0, tr // ROW_CHUNK, chunk, 0, unroll=8)

        @pl.when(step == steps - 1)
        def _():
            gp_ref[...] = jnp.sum(acc_ref[...], axis=0, keepdims=True)

    row = pl.BlockSpec((tr, d), lambda i: (i, 0))
    vec = pl.BlockSpec((1, d), lambda i: (0, 0))
    return _pcall(name, body, [dhn, h, g, dres], [row, row, vec, row],
                  [_sds((s, d), F32), _sds((s, d), BF16), _sds((1, d), F32)], [row, row, vec],
                  grid=(steps,), sem=("arbitrary",), scratch=[pltpu.VMEM((ROW_CHUNK, d), F32)], after=after)


def _loss_head(name, h, g, target, after=None):
    s, d = h.shape
    tr = _tile(s, 512)
    steps = s // tr

    def body(h_ref, g_ref, t_ref, dh_ref, dhb_ref, gp_ref, loss_ref, acc_ref, loss_acc_ref):
        step = pl.program_id(0)

        @pl.when(step == 0)
        def _():
            acc_ref[...] = jnp.zeros_like(acc_ref)
            loss_acc_ref[...] = jnp.zeros_like(loss_acc_ref)

        gg = g_ref[...]

        def chunk(i, carry):
            rows = pl.ds(pl.multiple_of(i * ROW_CHUNK, ROW_CHUNK), ROW_CHUNK)
            x = h_ref[rows, :]
            r = lax.rsqrt(jnp.mean(x * x, axis=-1, keepdims=True) + EPS)
            n = x * r
            e = n * gg - t_ref[rows, :]
            dy = e * (1.0 / d)
            dn = dy * gg
            dh = r * (dn - n * jnp.mean(dn * n, axis=-1, keepdims=True))
            dh_ref[rows, :] = dh
            dhb_ref[rows, :] = dh.astype(BF16)
            acc_ref[...] += dy * n
            loss_acc_ref[...] += jnp.mean(e * e, axis=-1, keepdims=True)
            return carry

        lax.fori_loop(0, tr // ROW_CHUNK, chunk, 0, unroll=8)

        @pl.when(step == steps - 1)
        def _():
            gp_ref[...] = jnp.sum(acc_ref[...], axis=0, keepdims=True)
            loss_ref[...] = 0.5 * jnp.sum(loss_acc_ref[...], axis=0, keepdims=True)

    row = pl.BlockSpec((tr, d), lambda i: (i, 0))
    vec = pl.BlockSpec((1, d), lambda i: (0, 0))
    one = pl.BlockSpec((1, 1), lambda i: (0, 0))
    return _pcall(name, body, [h, g, target], [row, vec, row],
                  [_sds((s, d), F32), _sds((s, d), BF16), _sds((1, d), F32), _sds((1, 1), F32)],
                  [row, row, vec, one], grid=(steps,), sem=("arbitrary",),
                  scratch=[pltpu.VMEM((ROW_CHUNK, d), F32), pltpu.VMEM((ROW_CHUNK, 1), F32)], after=after)


_SQRT_HALF = math.sqrt(0.5)
_INV_SQRT_2PI = 1.0 / math.sqrt(2.0 * math.pi)


def _gelu(x):
    return 0.5 * x * (1.0 + lax.erf(x * _SQRT_HALF))


def _gelu_grad(x):
    return 0.5 * (1.0 + lax.erf(x * _SQRT_HALF)) + x * jnp.exp(-0.5 * x * x) * _INV_SQRT_2PI


def _causal_mask():
    row = lax.broadcasted_iota(jnp.int32, (CHUNK, CHUNK), 0)
    col = lax.broadcasted_iota(jnp.int32, (CHUNK, CHUNK), 1)
    return row >= col


def _row_sum(x):
    return jnp.sum(x, axis=-1, keepdims=True)


def _masked_spatial(ws_ref, grp):
    return jnp.where(_causal_mask(), ws_ref[grp], 0.0).astype(BF16)


def _layernorm_stats(pre_ref, v_scr, w, head):
    total = jnp.zeros((CHUNK, 1), F32)
    for grp in range(A_GROUPS):
        v = _gelu(pre_ref[:, w + grp * head:w + (grp + 1) * head])
        v_scr[:, grp * head:(grp + 1) * head] = v
        total = total + _row_sum(v)
    mu = total * (1.0 / w)
    square = jnp.zeros((CHUNK, 1), F32)
    for grp in range(A_GROUPS):
        xc = v_scr[:, grp * head:(grp + 1) * head] - mu
        square = square + _row_sum(xc * xc)
    return mu, lax.rsqrt(square * (1.0 / w) + EPS)


def _amix_fwd(name, pre, ln_g, ln_b, w_s, b_s_col, after=None):
    s, w2 = pre.shape
    w = w2 // 2
    head = w // A_GROUPS

    def body(pre_ref, g_ref, b_ref, ws_ref, bs_ref, o_ref, v_scr):
        mu, rstd = _layernorm_stats(pre_ref, v_scr, w, head)
        for grp in range(A_GROUPS):
            cols = slice(grp * head, (grp + 1) * head)
            vhat = (v_scr[:, cols] - mu) * rstd
            vn = (vhat * g_ref[:, cols] + b_ref[:, cols]).astype(BF16)
            sg = jnp.dot(_masked_spatial(ws_ref, grp), vn, preferred_element_type=F32) + bs_ref[grp]
            o_ref[:, cols] = (_gelu(pre_ref[:, cols]) * sg).astype(o_ref.dtype)

    vec = pl.BlockSpec((1, w), lambda i: (0, 0))
    return _pcall(
        name, body, [pre, ln_g, ln_b, w_s, b_s_col],
        [pl.BlockSpec((CHUNK, w2), lambda i: (i, 0)), vec, vec,
         pl.BlockSpec((A_GROUPS, CHUNK, CHUNK), lambda i: (0, 0, 0)),
         pl.BlockSpec((A_GROUPS, CHUNK, 1), lambda i: (0, 0, 0))],
        _sds((s, w), BF16), pl.BlockSpec((CHUNK, w), lambda i: (i, 0)),
        grid=(s // CHUNK,), sem=("parallel",), scratch=[pltpu.VMEM((CHUNK, w), F32)], after=after)


def _amix_bwd(name, pre, dgated, ln_g, ln_b, w_s, b_s_col, after=None):
    s, w2 = pre.shape
    w = w2 // 2
    head = w // A_GROUPS

    def body(pre_ref, dg_ref, g_ref, b_ref, ws_ref, bs_ref, dpre_ref, glg_ref, glb_ref, gws_ref, gbs_ref,
             v_scr, dvn_scr):
        @pl.when(pl.program_id(0) == 0)
        def _():
            for ref in (glg_ref, glb_ref, gws_ref, gbs_ref):
                ref[...] = jnp.zeros_like(ref)

        mu, rstd = _layernorm_stats(pre_ref, v_scr, w, head)
        mask = _causal_mask()
        sum_dvhat = jnp.zeros((CHUNK, 1), F32)
        sum_dvhat_vhat = jnp.zeros((CHUNK, 1), F32)
        for grp in range(A_GROUPS):
            cols = slice(grp * head, (grp + 1) * head)
            vhat = (v_scr[:, cols] - mu) * rstd
            gain = g_ref[:, cols]
            vn = (vhat * gain + b_ref[:, cols]).astype(BF16)
            wm = _masked_spatial(ws_ref, grp)
            pre_u = pre_ref[:, cols]
            dgated = dg_ref[:, cols]
            ds = dgated * _gelu(pre_u)
            dsb = ds.astype(BF16)
            sg = jnp.dot(wm, vn, preferred_element_type=F32) + bs_ref[grp]
            dpre_ref[:, cols] = (dgated * sg * _gelu_grad(pre_u)).astype(dpre_ref.dtype)
            gws = lax.dot_general(dsb, vn, NT, preferred_element_type=F32)
            gws_ref[grp] += jnp.where(mask, gws, 0.0)
            gbs_ref[grp] += _row_sum(ds)
            dvn = lax.dot_general(wm, dsb, TN, preferred_element_type=F32)
            dvn_scr[:, cols] = dvn
            glg_ref[:, cols] += jnp.sum(dvn * vhat, axis=0, keepdims=True)
            glb_ref[:, cols] += jnp.sum(dvn, axis=0, keepdims=True)
            dvhat = dvn * gain
            sum_dvhat = sum_dvhat + _row_sum(dvhat)
            sum_dvhat_vhat = sum_dvhat_vhat + _row_sum(dvhat * vhat)
        mean_dvhat = sum_dvhat * (1.0 / w)
        mean_dvhat_vhat = sum_dvhat_vhat * (1.0 / w)
        for grp in range(A_GROUPS):
            cols = slice(grp * head, (grp + 1) * head)
            vhat = (v_scr[:, cols] - mu) * rstd
            dvhat = dvn_scr[:, cols] * g_ref[:, cols]
            dv = rstd * (dvhat - mean_dvhat - vhat * mean_dvhat_vhat)
            pre_v = pre_ref[:, w + grp * head:w + (grp + 1) * head]
            dpre_ref[:, w + grp * head:w + (grp + 1) * head] = (dv * _gelu_grad(pre_v)).astype(dpre_ref.dtype)

    vec = pl.BlockSpec((1, w), lambda i: (0, 0))
    ws_spec = pl.BlockSpec((A_GROUPS, CHUNK, CHUNK), lambda i: (0, 0, 0))
    bs_spec = pl.BlockSpec((A_GROUPS, CHUNK, 1), lambda i: (0, 0, 0))
    return _pcall(
        name, body, [pre, dgated, ln_g, ln_b, w_s, b_s_col],
        [pl.BlockSpec((CHUNK, w2), lambda i: (i, 0)), pl.BlockSpec((CHUNK, w), lambda i: (i, 0)),
         vec, vec, ws_spec, bs_spec],
        [_sds((s, w2), BF16), _sds((1, w), F32), _sds((1, w), F32),
         _sds((A_GROUPS, CHUNK, CHUNK), F32), _sds((A_GROUPS, CHUNK, 1), F32)],
        [pl.BlockSpec((CHUNK, w2), lambda i: (i, 0)), vec, vec, ws_spec, bs_spec],
        grid=(s // CHUNK,), sem=("arbitrary",),
        scratch=[pltpu.VMEM((CHUNK, w), F32), pltpu.VMEM((CHUNK, w), F32)], after=after)


def _shift_rows(x, k, forward):
    n = x.shape[0]
    row = lax.broadcasted_iota(jnp.int32, x.shape, 0)
    if forward:
        return jnp.where(row >= k, pltpu.roll(x, k, 0), 0.0)
    return jnp.where(row < n - k, pltpu.roll(x, n - k, 0), 0.0)


def _window_sum(x, window, forward):
    k = 1
    while k < window:
        x = x + _shift_rows(x, k, forward)
        k *= 2
    return x


def _pool(name, v, backward, after=None):
    s, w = v.shape
    head = w // B_GROUPS
    lane = _tile(head, 128)

    def body(v_ref, o_ref):
        grp = pl.program_id(0)
        t = lax.broadcasted_iota(jnp.int32, (s, lane), 0)
        for idx, window in enumerate(B_WINDOWS):
            @pl.when(grp == idx)
            def _():
                inv_count = 1.0 / jnp.minimum(t + 1, window).astype(F32)
                for strip in range(head // lane):
                    cols = slice(strip * lane, (strip + 1) * lane)
                    x = v_ref[:, cols]
                    if backward:
                        out = _window_sum(x * inv_count, window, False) - x
                    else:
                        out = _window_sum(x, window, True) * inv_count - x
                    o_ref[:, cols] = out.astype(o_ref.dtype)

    spec = pl.BlockSpec((s, head), lambda g: (0, g))
    return _pcall(name, body, [v], [spec], _sds((s, w), BF16), spec, grid=(B_GROUPS,),
                  sem=("parallel",), after=after)


def _scaled_dx(name, dy, w, scale, mixed, after=None, tm=1024, tn=1024):
    s, n = dy.shape
    k = w.shape[0]
    tm, tn = _tile(s, tm), _tile(k, tn)

    def body(dy_ref, w_ref, sc_ref, mx_ref, o_ref, gs_ref):
        p = lax.dot_general(dy_ref[...], w_ref[...], NT, preferred_element_type=F32)
        o_ref[...] = (p * sc_ref[...]).astype(o_ref.dtype)
        _accumulate(gs_ref, jnp.sum(p * mx_ref[...].astype(F32), axis=0, keepdims=True), pl.program_id(1))

    tile = pl.BlockSpec((tm, tn), lambda j, i: (i, j))
    vec = pl.BlockSpec((1, tn), lambda j, i: (0, j))
    return _pcall(name, body, [dy, w, scale, mixed],
                  [pl.BlockSpec((tm, n), lambda j, i: (i, 0)), pl.BlockSpec((tn, n), lambda j, i: (j, 0)), vec, tile],
                  [_sds((s, k), BF16), _sds((1, k), F32)], [tile, vec],
                  grid=(k // tn, s // tm), sem=("parallel", "arbitrary"), after=after)


def _adamw(w, g, m, v):
    m = ADAM_B1 * m + (1.0 - ADAM_B1) * g
    v = ADAM_B2 * v + (1.0 - ADAM_B2) * (g * g)
    m_hat = m / (1.0 - ADAM_B1 ** ADAM_STEP)
    v_hat = v / (1.0 - ADAM_B2 ** ADAM_STEP)
    delta = -ADAM_LR * (m_hat / (jnp.sqrt(v_hat) + ADAM_EPS) + ADAM_WD * w)
    return delta, m, v


def _adam_rows(name, g, w, m, v, rows, after=None):
    lanes = g.shape[1]
    n = len(rows)

    def body(g_ref, w_ref, m_ref, v_ref, *out_refs):
        first = 0
        for p, count in enumerate(rows):
            part = slice(first, first + count)
            outs = _adamw(w_ref[part, :], g_ref[part, :], m_ref[part, :], v_ref[part, :])
            for o_ref, o in zip(out_refs[3 * p:3 * p + 3], outs):
                o_ref[...] = o
            first += count

    vmem = pl.BlockSpec(memory_space=pltpu.VMEM)
    return _pcall(name, body, [g, w, m, v], [vmem] * 4,
                  [_sds((count, lanes), F32) for count in rows for _ in range(3)], [vmem] * (3 * n), after=after)


def _position():
    return lax.axis_index("x"), lax.axis_index("y"), lax.axis_index("c")


def _other_chips(x, y):
    return [(1 - x, y), (x, 1 - y), (1 - x, 1 - y)]


def _slot(px, py, pc):
    return 4 * px + 2 * py + pc


def _hbm(a):
    return pltpu.with_memory_space_constraint(a, pltpu.HBM)


def _hop1_copies(srcs, lands, send_sems, recv_sems):
    x, y, c = _position()
    peers = [(x, y, 1 - c), (1 - x, y, c), (x, 1 - y, c)]
    mine = _slot(x, y, c)
    return [[pltpu.make_async_remote_copy(
        src_ref=srcs[t], dst_ref=lands[t].at[mine], send_sem=send_sems[t].at[k], recv_sem=recv_sems[t].at[k],
        device_id=peer, device_id_type=MESH) for k, peer in enumerate(peers)] for t in range(len(srcs))]


def _hop2_copies(lands, send_sems, recv_sems):
    x, y, c = _position()
    routes = [(_slot(1 - x, y, c), (x, 1 - y, c)), (_slot(x, 1 - y, c), (1 - x, y, c))]
    out = []
    for t in range(len(lands)):
        rows = lands[t].shape[1]
        halves = [(0, rows // 2), (rows // 2, rows - rows // 2)]
        per_tensor = []
        for h, ((slot, peer), (start, size)) in enumerate(zip(routes, halves)):
            if size:
                block = lands[t].at[slot, pl.ds(start, size)]
                per_tensor.append(pltpu.make_async_remote_copy(
                    src_ref=block, dst_ref=block, send_sem=send_sems[t].at[h], recv_sem=recv_sems[t].at[h],
                    device_id=peer, device_id_type=MESH))
        for j, (slot, _) in enumerate(routes):
            block = lands[t].at[slot]
            per_tensor.append(pltpu.make_async_remote_copy(
                src_ref=block, dst_ref=block, send_sem=send_sems[t].at[2 + j], recv_sem=recv_sems[t].at[2 + j],
                device_id=(x, y, 1 - c), device_id_type=MESH))
        out.append(per_tensor)
    return out


def _split_start(name, srcs, lands, copies, n_sems, after=None):
    n = len(srcs)
    order = [] if after is None else [after]
    n_in = 2 * n + len(order)

    def body(*refs):
        for per_tensor in copies(refs[:n], refs[n:2 * n], refs[n_in:n_in + n], refs[n_in + n:n_in + 2 * n]):
            for cp in per_tensor:
                cp.start()
        refs[-1][...] = jnp.zeros_like(refs[-1])

    out_shape = ([pltpu.SemaphoreType.DMA((n_sems,)) for _ in range(2 * n)]
                 + [pltpu.HBM(a.shape, a.dtype) for a in list(srcs) + list(lands)]
                 + [_sds((8, 128), F32)])
    out = pl.pallas_call(
        body, name=name, out_shape=out_shape, in_specs=[_HBM] * (2 * n) + [_ANY] * len(order),
        out_specs=[_SEM] * (2 * n) + [_HBM] * (2 * n) + [pl.BlockSpec(memory_space=pltpu.VMEM)],
        input_output_aliases={i: 2 * n + i for i in range(2 * n)},
        compiler_params=pltpu.CompilerParams(has_side_effects=_EFFECT),
    )(*[_hbm(a) for a in srcs], *[_hbm(a) for a in lands], *order)
    return [(out[t], out[n + t], out[2 * n + t], out[3 * n + t]) for t in range(n)], out[-1]


def _split_wait(name, started, copies, after):
    n = len(started)

    def body(*refs):
        for per_tensor in copies(refs[:n], refs[n:2 * n], refs[2 * n:3 * n], refs[3 * n:4 * n]):
            for cp in per_tensor:
                cp.wait_send()
                cp.wait_recv()

    srcs = [e[2] for e in started]
    lands = [e[3] for e in started]
    out = pl.pallas_call(
        body, name=name, out_shape=[pltpu.HBM(a.shape, a.dtype) for a in srcs + lands],
        in_specs=[_HBM] * (2 * n) + [_SEM] * (2 * n) + [_ANY], out_specs=[_HBM] * (2 * n),
        input_output_aliases={i: i for i in range(2 * n)},
        compiler_params=pltpu.CompilerParams(has_side_effects=_EFFECT),
    )(*srcs, *lands, *[e[0] for e in started], *[e[1] for e in started], after)
    return out[:n], out[n:]


def _gather_step(name, arrived, fresh, after=None):
    n, m = len(arrived), len(fresh)
    order = [] if after is None else [after]
    fresh_lands = [lax.empty((N_DEV,) + s.shape, s.dtype) for s in fresh]
    buffers = [e[2] for e in arrived] + [e[3] for e in arrived] + list(fresh) + fresh_lands
    old_sems = [e[0] for e in arrived] + [e[1] for e in arrived]
    n_buf, n_old = len(buffers), len(old_sems)
    first_new = n_buf + n_old + len(order)

    def body(*refs):
        bufs, old = refs[:n_buf], refs[n_buf:n_buf + n_old]
        new = refs[first_new:first_new + 2 * n + 2 * m]
        for per_tensor in _hop1_copies(bufs[:n], bufs[n:2 * n], old[:n], old[n:]):
            for cp in per_tensor:
                cp.wait_send()
                cp.wait_recv()
        second = _hop2_copies(bufs[n:2 * n], new[:n], new[n:2 * n])
        first = _hop1_copies(bufs[2 * n:2 * n + m], bufs[2 * n + m:], new[2 * n:2 * n + m], new[2 * n + m:])
        for per_tensor in second + first:
            for cp in per_tensor:
                cp.start()
        refs[-1][...] = jnp.zeros_like(refs[-1])

    n_new = 2 * n + 2 * m
    out_shape = ([pltpu.SemaphoreType.DMA((4,)) for _ in range(2 * n)]
                 + [pltpu.SemaphoreType.DMA((3,)) for _ in range(2 * m)]
                 + [pltpu.HBM(a.shape, a.dtype) for a in buffers] + [_sds((8, 128), F32)])
    out = pl.pallas_call(
        body, name=name, out_shape=out_shape,
        in_specs=[_HBM] * n_buf + [_SEM] * n_old + [_ANY] * len(order),
        out_specs=[_SEM] * n_new + [_HBM] * n_buf + [pl.BlockSpec(memory_space=pltpu.VMEM)],
        input_output_aliases={i: n_new + i for i in range(n_buf)},
        compiler_params=pltpu.CompilerParams(has_side_effects=_EFFECT),
    )(*[_hbm(a) for a in buffers], *old_sems, *order)
    sems, bufs = out[:n_new], out[n_new:n_new + n_buf]
    second = [(sems[t], sems[n + t], bufs[t], bufs[n + t]) for t in range(n)]
    first = [(sems[2 * n + t], sems[2 * n + m + t], bufs[2 * n + t], bufs[2 * n + m + t]) for t in range(m)]
    return second, first, out[-1]


def _gather_wait(name, second, after):
    return _split_wait(name, second, lambda srcs, lands, send, recv: _hop2_copies(lands, send, recv), after)


def _sibling_copies(srcs, lands, send_sems, recv_sems):
    x, y, c = _position()
    return [[pltpu.make_async_remote_copy(
        src_ref=srcs[t], dst_ref=lands[t], send_sem=send_sems[t].at[0], recv_sem=recv_sems[t].at[0],
        device_id=(x, y, 1 - c), device_id_type=MESH)] for t in range(len(srcs))]


def _sibling_and_scatter_start(name, arrays, partials):
    k = len(arrays)

    def copies(srcs, lands, send_sems, recv_sems):
        return (_sibling_copies(srcs[:k], lands[:k], send_sems[:k], recv_sems[:k])
                + _scatter_copies(srcs[k:], lands[k:], send_sems[k:], recv_sems[k:]))

    lands = ([lax.empty(a.shape, a.dtype) for a in arrays]
             + [lax.empty((N_CHIPS - 1,) + p.shape[1:], p.dtype) for p in partials])
    return _split_start(name, list(arrays) + list(partials), lands, copies, 3)


def _sibling_wait(name, started, after):
    return _split_wait(name, started, _sibling_copies, after)[1]


def _small_copies(srcs, lands, send_sems, recv_sems):
    x, y, c = _position()
    mine = _slot(x, y, c)
    peers = [(x ^ ((k >> 2) & 1), y ^ ((k >> 1) & 1), c ^ (k & 1)) for k in range(1, N_DEV)]
    return [[pltpu.make_async_remote_copy(
        src_ref=srcs[t], dst_ref=lands[t].at[mine], send_sem=send_sems[t].at[k], recv_sem=recv_sems[t].at[k],
        device_id=peer, device_id_type=MESH) for k, peer in enumerate(peers)] for t in range(len(srcs))]


def _gather_finish(name, shards, lands, after):
    n = len(shards)

    def body(*refs):
        srcs, lands_in, outs = refs[:n], refs[n:2 * n], refs[2 * n:3 * n]
        send_sems, recv_sems, local_sems = refs[3 * n:]
        x, y, c = _position()
        local = [pltpu.make_async_copy(srcs[t], outs[t].at[_slot(x, y, c)], local_sems.at[t]) for t in range(n)]

        def diagonal(t, core):
            block = outs[t].at[_slot(1 - x, 1 - y, core)]
            return pltpu.make_async_remote_copy(
                src_ref=block, dst_ref=block, send_sem=send_sems.at[t], recv_sem=recv_sems.at[t],
                device_id=(x, y, 1 - c), device_id_type=MESH)

        for cp in local:
            cp.start()
        for t in range(n):
            diagonal(t, c).start()
        for t in range(n):
            diagonal(t, c).wait_send()
            diagonal(t, 1 - c).wait_recv()
        for cp in local:
            cp.wait()

    return _pcall(name, body, [*shards, *lands], [_ANY] * (2 * n),
                  [_sds(l.shape, l.dtype) for l in lands], [_ANY] * n,
                  scratch=[pltpu.SemaphoreType.DMA((n,)), pltpu.SemaphoreType.DMA((n,)),
                           pltpu.SemaphoreType.DMA((n,))],
                  after=after, aliases={n + t: t for t in range(n)})


def _exchange_sibling(name, fulls, after):
    n = len(fulls)

    def body(*refs):
        src = refs[:n]
        out = refs[n:2 * n]
        send_sems, recv_sems = refs[2 * n:]
        x, y, c = _position()
        copies = [pltpu.make_async_remote_copy(
            src_ref=src[t].at[:, 1 - c], dst_ref=out[t], send_sem=send_sems.at[t], recv_sem=recv_sems.at[t],
            device_id=(x, y, 1 - c), device_id_type=MESH) for t in range(n)]
        for cp in copies:
            cp.start()
        for cp in copies:
            cp.wait()

    return _pcall(name, body, fulls, [_ANY] * n, [_sds((N_CHIPS,) + f.shape[2:], f.dtype) for f in fulls],
                  [_ANY] * n, scratch=[pltpu.SemaphoreType.DMA((n,)), pltpu.SemaphoreType.DMA((n,))],
                  after=after)


def _add_sibling(name, full, recv, core, after):
    _, _, r, c = full.shape
    tr = _tile(r, max(8, (256 * 1024) // c))

    def body(core_ref, f_ref, r_ref, o_ref):
        o_ref[...] = (f_ref[...].astype(F32) + r_ref[...].astype(F32)).astype(o_ref.dtype)

    return _pcall(
        name, body, [full, recv],
        [pl.BlockSpec((None, None, tr, c), lambda p, i, core_ref: (p, core_ref[0], i, 0)),
         pl.BlockSpec((None, tr, c), lambda p, i, core_ref: (p, i, 0))],
        _sds((N_CHIPS, r, c), BF16), pl.BlockSpec((None, tr, c), lambda p, i, core_ref: (p, i, 0)),
        grid=(N_CHIPS, r // tr), sem=("parallel", "parallel"), prefetch=[core], after=after)


def _scatter_copies(srcs, lands, send_sems, recv_sems):
    x, y, c = _position()
    return [[pltpu.make_async_remote_copy(
        src_ref=srcs[t].at[2 * px + py], dst_ref=lands[t].at[j],
        send_sem=send_sems[t].at[j], recv_sem=recv_sems[t].at[j],
        device_id=(px, py, c), device_id_type=MESH) for j, (px, py) in enumerate(_other_chips(x, y))]
        for t in range(len(srcs))]


def _scatter_wait(name, started, after):
    return _split_wait(name, started, _scatter_copies, after)


N_BUF = 3


def _reduce_adam(name, partial, recv, chip, w, m, v, layer, carried, after=None):
    n_layers, r, c = w.shape
    tr = _tile(r, max(8, (256 * 1024) // c))
    n_blocks = r // tr
    n_carried = 0 if carried is None else 4

    def body(chip_ref, p_hbm, r_hbm, w_hbm, m_hbm, v_hbm, *rest):
        outs_hbm = rest[n_carried:n_carried + 4]
        p_buf, r_buf, w_buf, m_buf, v_buf, g_out, d_out, nm_out, nv_out, in_sems, out_sems = rest[n_carried + 4:]
        in_bufs = (p_buf, r_buf, w_buf, m_buf, v_buf)
        out_bufs = (g_out, d_out, nm_out, nv_out)

        def fetches(blk):
            slot, rows = blk % N_BUF, pl.ds(blk * tr, tr)
            srcs = (p_hbm.at[chip_ref[0], rows], r_hbm.at[:, rows], w_hbm.at[layer, rows], m_hbm.at[layer, rows],
                    v_hbm.at[layer, rows])
            return [pltpu.make_async_copy(src, buf.at[slot], in_sems.at[slot, k])
                    for k, (src, buf) in enumerate(zip(srcs, in_bufs))]

        def stores(blk):
            slot, rows = blk % 2, pl.ds(blk * tr, tr)
            return [pltpu.make_async_copy(buf.at[slot], dst.at[layer, rows], out_sems.at[slot, k])
                    for k, (buf, dst) in enumerate(zip(out_bufs, outs_hbm))]

        for blk in range(min(N_BUF, n_blocks)):
            for cp in fetches(blk):
                cp.start()
        for blk in range(n_blocks):
            slot, out_slot = blk % N_BUF, blk % 2
            for cp in fetches(blk):
                cp.wait()
            if blk >= 2:
                for cp in stores(blk - 2):
                    cp.wait()
            g = p_buf[slot].astype(F32)
            for j in range(N_CHIPS - 1):
                g = g + r_buf[slot, j].astype(F32)
            g_out[out_slot] = g
            d_out[out_slot], nm_out[out_slot], nv_out[out_slot] = _adamw(w_buf[slot], g, m_buf[slot], v_buf[slot])
            for cp in stores(blk):
                cp.start()
            if blk + N_BUF < n_blocks:
                for cp in fetches(blk + N_BUF):
                    cp.start()
        for blk in range(max(0, n_blocks - 2), n_blocks):
            for cp in stores(blk):
                cp.wait()

    operands = [partial, recv, w, m, v] + ([] if carried is None else list(carried))
    block = (tr, c)
    scratch = [pltpu.VMEM((N_BUF,) + block, BF16), pltpu.VMEM((N_BUF, N_CHIPS - 1) + block, BF16),
               pltpu.VMEM((N_BUF,) + block, F32), pltpu.VMEM((N_BUF,) + block, F32), pltpu.VMEM((N_BUF,) + block, F32),
               pltpu.VMEM((2,) + block, F32), pltpu.VMEM((2,) + block, F32), pltpu.VMEM((2,) + block, F32),
               pltpu.VMEM((2,) + block, F32), pltpu.SemaphoreType.DMA((N_BUF, 5)), pltpu.SemaphoreType.DMA((2, 4))]
    return _pcall(name, body, operands, [_ANY] * len(operands), [_sds((n_layers, r, c), F32)] * 4, [_ANY] * 4,
                  grid=(1,), sem=("arbitrary",), scratch=scratch, prefetch=[chip], after=after,
                  aliases={1 + 5 + o: o for o in range(n_carried)})


def _small_sum(name, gathered, own, device, after=None):
    r, lanes = own.shape

    def body(dev_ref, g_ref, own_ref, out_ref):
        dev = dev_ref[0]
        mine = own_ref[...]
        total = jnp.where(dev == 0, mine, g_ref[0])
        for d in range(1, N_DEV):
            total = total + jnp.where(dev == d, mine, g_ref[d])
        out_ref[...] = total

    return _pcall(name, body, [gathered, own],
                  [pl.BlockSpec((N_DEV, r, lanes), lambda i, dev_ref: (0, 0, 0)),
                   pl.BlockSpec((r, lanes), lambda i, dev_ref: (0, 0))],
                  _sds((r, lanes), F32), pl.BlockSpec((r, lanes), lambda i, dev_ref: (0, 0)),
                  grid=(1,), sem=("arbitrary",), prefetch=[device], after=after)


def _pack(arrays):
    return jnp.concatenate([a.reshape(-1, 128) for a in arrays], axis=0)


def _unpack(packed, shapes):
    out, row = [], 0
    for shape in shapes:
        rows = math.prod(shape) // 128
        out.append(packed[row:row + rows].reshape(shape))
        row += rows
    return out


class _Order:
    def __init__(self):
        self.last = None

    def __call__(self, fn, *args, **kwargs):
        out = fn(*args, after=self.last, **kwargs)
        self.last = out[0] if isinstance(out, (list, tuple)) else out
        return out


def kernel(x, a_w_in, a_ln_g, a_ln_b, a_w_s, a_b_s, a_w_out, b_w_in, b_w_grp, b_scale, b_w_out, norm_mix, norm_mlp, mlp_w1, mlp_w2, final_norm, loss_target, m_a_w_in, m_a_ln_g, m_a_ln_b, m_a_w_s, m_a_b_s, m_a_w_out, m_b_w_in, m_b_w_grp, m_b_scale, m_b_w_out, m_norm_mix, m_norm_mlp, m_mlp_w1, m_mlp_w2, m_final_norm, v_a_w_in, v_a_ln_g, v_a_ln_b, v_a_w_s, v_a_b_s, v_a_w_out, v_b_w_in, v_b_w_grp, v_b_scale, v_b_w_out, v_norm_mix, v_norm_mlp, v_mlp_w1, v_mlp_w2, v_final_norm):
    s, d = x.shape[1], x.shape[2]
    depth = mlp_w1.shape[0]
    a_slab = a_w_in.shape[2]
    ff_slab = mlp_w1.shape[2]
    ff_rows = mlp_w2.shape[1]
    bh = b_w_grp.shape[3]
    my_x, my_y, my_c = _position()
    core = jnp.reshape(my_c, (1,)).astype(jnp.int32)
    chip = jnp.reshape(2 * my_x + my_y, (1,)).astype(jnp.int32)
    device = _slot(my_x, my_y, my_c)
    run = _Order()

    w1_b, w2_b = mlp_w1.astype(BF16), mlp_w2.astype(BF16)
    shards = [a_w_in[0].astype(BF16), a_w_out[0].astype(BF16), b_scale,
              w1_b[0], w2_b[0],
              b_w_in[0].astype(BF16), b_w_grp[0].astype(BF16), b_w_out[0].astype(BF16),
              w1_b[1], w2_b[1]]
    groups = [[0], [1, 2], [3], [4], [5, 6, 7], [8], [9]]
    start_with = {1: [2, 3], 2: [4], 3: [5], 4: [6]}
    hop1, hop2 = {}, {}
    _, hop1[0], token = _gather_step("weights_group0_hop1", [], [shards[t] for t in groups[0]])
    _, hop1[1], token = _gather_step("weights_group1_hop1", [], [shards[t] for t in groups[1]], token)
    run.last = token

    def advance(g):
        if g not in hop1:
            return
        ahead = start_with.get(g, [])
        fresh = [shards[t] for a in ahead for t in groups[a]]
        hop2[g], started, tok = _gather_step(f"weights_group{g}_hop2", hop1.pop(g), fresh, run.last)
        for a in ahead:
            hop1[a], started = started[:len(groups[a])], started[len(groups[a]):]
        run.last = tok

    def gathered(g):
        advance(g)
        if g == 0:
            advance(1)
        srcs, lands = _gather_wait(f"weights_group{g}_wait", hop2.pop(g), run.last)
        run.last = srcs[0]
        return run(_gather_finish, f"weights_group{g}_finish", srcs, lands)

    h0 = x[0]
    target = loss_target[0]
    ln_g, ln_b = a_ln_g, a_ln_b
    w_s = a_w_s[0]
    b_s_col = a_b_s[0][:, :, None]
    nmix = [norm_mix[l][None, :] for l in range(depth)]
    nmlp = [norm_mlp[l][None, :] for l in range(depth)]

    def mlp_forward(l, h, up_group):
        hn = run(_rms_fwd, f"mlp{l}_norm", h, nmlp[l])
        (w1,) = gathered(up_group)
        advance(up_group + 1)
        act, act_sq = run(_mm_nn, f"mlp{l}_up", hn, w1,
                          lambda acc: (jnp.maximum(acc, 0.0), jnp.square(jnp.maximum(acc, 0.0))),
                          (BF16, BF16), slab=True)
        (w2,) = gathered(up_group + 1)
        advance(up_group + 2)
        w2 = w2.reshape(-1, d)
        (h_out,) = run(_mm_nn, f"mlp{l}_down", act_sq, w2, lambda acc, res: (acc + res,), (F32,),
                       extras=(h,), extra_kinds=("tile",))
        return h_out, (h, hn, act, act_sq, w1, w2)

    scattered = []

    pending = []

    def scatter_partials(name, partials, specs):
        pending.append((name, partials, specs))

    def start_exchanges(name, to_sibling):
        partials = [p for _, group, _ in pending for p in group]
        in_flight, tok = _sibling_and_scatter_start(name, to_sibling, partials)
        run.last = tok
        first = len(to_sibling)
        for group_name, group, specs in pending:
            scattered.append((group_name, in_flight[first:first + len(group)], specs))
            first += len(group)
        pending.clear()
        return in_flight[:len(to_sibling)]

    weights = {"a_w_in": (a_w_in, m_a_w_in, v_a_w_in), "a_w_out": (a_w_out, m_a_w_out, v_a_w_out),
               "b_w_in": (b_w_in, m_b_w_in, v_b_w_in), "b_w_grp": (b_w_grp, m_b_w_grp, v_b_w_grp),
               "b_w_out": (b_w_out, m_b_w_out, v_b_w_out), "mlp_w1": (mlp_w1, m_mlp_w1, v_mlp_w1),
               "mlp_w2": (mlp_w2, m_mlp_w2, v_mlp_w2)}
    results = {}

    def finish_group(name, in_flight, specs):
        partials, lands = _scatter_wait(name + "_scatter_wait", in_flight, run.last)
        run.last = lands[0]
        for t, (wname, layer) in enumerate(specs):
            w, m, v = weights[wname]
            shape = (w.shape[0],) + partials[t].shape[1:]
            results[wname] = run(_reduce_adam, f"{name}_reduce_adam_{t}", partials[t], lands[t], chip,
                                 w.reshape(shape), m.reshape(shape), v.reshape(shape), layer, results.get(wname))

    def weight_grad(name, a, b, by_rows, block, between, tm=1024, tn=1024):
        other = run(_mm_tn_half, name + "_other", a, b, core, False, by_rows, block, tm=tm, tn=tn)
        sent = start_exchanges(name + "_sibling_start", [other])
        middle = between()
        (recv,) = _sibling_wait(name + "_sibling_wait", sent, run.last)
        run.last = recv
        return run(_mm_tn_half, name + "_own", a, b, core, True, by_rows, block, recv=recv, tm=tm, tn=tn), middle

    def mlp_backward(l, saved, dh, dhb):
        h, hn, act, act_sq, w1, w2 = saved
        part_w2, (dpre,) = weight_grad(
            f"mlp{l}_down_dw", act_sq, dhb, True, ff_rows,
            lambda: run(_mm_nt, f"mlp{l}_down_dx", dhb, w2, lambda acc, a: (2.0 * a.astype(F32) * acc,),
                        (BF16,), extras=(act,), extra_kinds=("tile",)))
        scatter_partials(f"mlp{l}_down_grads", [part_w2], [("mlp_w2", l)])
        part_w1, (dhn,) = weight_grad(
            f"mlp{l}_up_dw", hn, dpre, False, ff_slab,
            lambda: run(_mm_nt, f"mlp{l}_up_dx", dpre, w1, lambda acc: (acc,), (F32,), slab=True))
        scatter_partials(f"mlp{l}_up_grads", [part_w1], [("mlp_w1", l)])
        dh, dhb, g_norm = run(_rms_bwd, f"mlp{l}_norm_bwd", dhn, h, nmlp[l], dh)
        return dh, dhb, g_norm

    hn0 = run(_rms_fwd, "mix0_norm", h0, nmix[0])
    (wa_in,) = gathered(0)
    (pre,) = run(_mm_nn, "mixa_in", hn0, wa_in, lambda acc: (acc,), (F32,), slab=True)
    wa_out, scale = gathered(1)
    wa_out, scale = wa_out.reshape(d, d), scale.reshape(1, d)
    gated = run(_amix_fwd, "mixa_gate", pre, ln_g, ln_b, w_s, b_s_col)
    advance(2)
    (h1,) = run(_mm_nn, "mixa_out", gated, wa_out, lambda acc, res: (acc + res,), (F32,),
                extras=(h0,), extra_kinds=("tile",))
    h2, saved_mlp0 = mlp_forward(0, h1, 2)
    hn2 = run(_rms_fwd, "mix1_norm", h2, nmix[1])
    wb_in, wb_grp, wb_out = gathered(4)
    advance(5)
    wb_in, wb_out = wb_in.reshape(d, d), wb_out.reshape(d, d)
    wb_grp = jnp.transpose(wb_grp, (1, 0, 2, 3)).reshape(B_GROUPS, bh, bh)
    (vb,) = run(_mm_nn, "mixb_in", hn2, wb_in, lambda acc: (acc,), (F32,))
    pooled = run(_pool, "mixb_pool", vb, backward=False)
    tm = _tile(s, 1024)
    grp_tile = pl.BlockSpec((tm, bh), lambda i, j, k: (i, j))
    grp_weight = pl.BlockSpec((None, bh, bh), lambda i, j, k: (j, 0, 0))
    mixed, mixed_scaled = run(
        _matmul, "mixb_grp", pooled, wb_grp, NN, (s // tm, B_GROUPS, 1), grp_tile, grp_weight,
        [_sds((s, d), BF16), _sds((s, d), BF16)], [grp_tile] * 2,
        (tm, bh), lambda acc, sc: (acc, acc * sc), (scale,), [pl.BlockSpec((1, bh), lambda i, j, k: (0, j))])
    (h3,) = run(_mm_nn, "mixb_out", mixed_scaled, wb_out, lambda acc, res: (acc + res,), (F32,),
                extras=(h2,), extra_kinds=("tile",))
    h4, saved_mlp1 = mlp_forward(1, h3, 5)
    dh, dhb, g_final, loss_part = run(_loss_head, "loss_head", h4, final_norm[None, :], target)

    dh, dhb, g_nmlp1 = mlp_backward(1, saved_mlp1, dh, dhb)
    tks = _tile(s, 1024)
    grp_rows = pl.BlockSpec((tks, bh), lambda i, j, k: (k, j))

    def mixb_middle():
        dms_scaled, g_scale = run(_scaled_dx, "mixb_out_dx", dhb, wb_out, scale, mixed)
        (g_wb_grp,) = run(
            _matmul, "mixb_grp_dw", pooled, dms_scaled, TN, (1, B_GROUPS, s // tks), grp_rows, grp_rows,
            [_sds((B_GROUPS, bh, bh), BF16)], [grp_weight], (bh, bh), lambda acc: (acc,))
        (dpooled,) = run(
            _matmul, "mixb_grp_dx", dms_scaled, wb_grp, NT, (s // tm, B_GROUPS, 1), grp_tile, grp_weight,
            [_sds((s, d), F32)], [grp_tile], (tm, bh), lambda acc: (acc,))
        return g_scale, g_wb_grp, run(_pool, "mixb_pool_bwd", dpooled, backward=True)

    part_wb_out, (g_scale, g_wb_grp, dvb) = weight_grad("mixb_out_dw", mixed_scaled, dhb, True, d // N_DEV,
                                                        mixb_middle, tn=d)
    part_wb_in, (dhn2,) = weight_grad(
        "mixb_in_dw", hn2, dvb, True, d // N_DEV,
        lambda: run(_mm_nt, "mixb_in_dx", dvb, wb_in, lambda acc: (acc,), (F32,)), tn=d)
    grp_full = jnp.transpose(g_wb_grp.reshape(B_GROUPS, N_DEV, bh // N_DEV, bh), (1, 0, 2, 3))
    grp_full = grp_full.reshape(N_CHIPS, 2, B_GROUPS * bh // N_DEV, bh)
    (grp_sibling,) = run(_exchange_sibling, "mixb_grp_dw_to_sibling", [grp_full])
    part_wb_grp = run(_add_sibling, "mixb_grp_dw_add_sibling", grp_full, grp_sibling, core)
    scatter_partials("mixb_grads", [part_wb_out, part_wb_grp, part_wb_in],
                     [("b_w_out", 0), ("b_w_grp", 0), ("b_w_in", 0)])
    dh, dhb, g_nmix1 = run(_rms_bwd, "mix1_norm_bwd", dhn2, h2, nmix[1], dh)
    dh, dhb, g_nmlp0 = mlp_backward(0, saved_mlp0, dh, dhb)
    def mixa_middle():
        (dgated,) = run(_mm_nt, "mixa_out_dx", dhb, wa_out, lambda acc: (acc,), (F32,))
        return run(_amix_bwd, "mixa_gate_bwd", pre, dgated, ln_g, ln_b, w_s, b_s_col)

    part_wa_out, (dpre, g_ln_g, g_ln_b, g_w_s, g_b_s) = weight_grad("mixa_out_dw", gated, dhb, True, d // N_DEV,
                                                                     mixa_middle, tn=d)
    part_wa_in, (dhn0,) = weight_grad(
        "mixa_in_dw", hn0, dpre, False, a_slab,
        lambda: run(_mm_nt, "mixa_in_dx", dpre, wa_in, lambda acc: (acc,), (F32,), slab=True), tm=d)
    scatter_partials("mixa_grads", [part_wa_in, part_wa_out], [("a_w_in", 0), ("a_w_out", 0)])
    start_exchanges("mixa_grads_scatter_start", [])
    grad_x, _, g_nmix0 = run(_rms_bwd, "mix0_norm_bwd", dhn0, h0, nmix[0], dh)

    g_norm_mix = jnp.concatenate([g_nmix0, g_nmix1], axis=0)
    g_norm_mlp = jnp.concatenate([g_nmlp0, g_nmlp1], axis=0)
    loss_row = jnp.pad(loss_part, ((0, 0), (0, 127)))
    small_parts = [g_ln_g, g_ln_b, g_w_s, g_b_s, g_norm_mix, g_norm_mlp, g_final, g_scale, loss_row]
    packed = _pack(small_parts)
    small_sent, tok = _split_start("small_grads_start", [packed], [lax.empty((N_DEV,) + packed.shape, F32)],
                                   _small_copies, N_DEV - 1)
    run.last = tok

    for group in scattered:
        finish_group(*group)
    own_packed, small_gathered = _split_wait("small_grads_wait", small_sent, _small_copies, run.last)
    run.last = small_gathered[0]
    small_sum = run(_small_sum, "small_grads_sum", small_gathered[0], own_packed[0],
                    jnp.reshape(device, (1,)).astype(jnp.int32))
    sg = _unpack(small_sum, [a_ln_g.shape, a_ln_b.shape, a_w_s.shape, a_b_s.shape, norm_mix.shape,
                             norm_mlp.shape, final_norm.shape, (1, d), (1, 128)])
    loss = sg.pop()[0, 0]
    shard = b_scale.shape[1]
    sg[7] = lax.dynamic_slice(sg[7], (0, device * shard), (1, shard))
    small_w = [a_ln_g, a_ln_b, a_w_s, a_b_s, norm_mix, norm_mlp, final_norm, b_scale]
    small_m = [m_a_ln_g, m_a_ln_b, m_a_w_s, m_a_b_s, m_norm_mix, m_norm_mlp, m_final_norm, m_b_scale]
    small_v = [v_a_ln_g, v_a_ln_b, v_a_w_s, v_a_b_s, v_norm_mix, v_norm_mlp, v_final_norm, v_b_scale]
    small_out = run(_adam_rows, "small_adam", _pack(sg), _pack(small_w), _pack(small_m), _pack(small_v),
                    [w.size // 128 for w in small_w])
    small_res = [sg] + [[small_out[3 * p + o].reshape(w.shape) for p, w in enumerate(small_w)]
                        for o in range(3)]
    big = {wname: [o.reshape(weights[wname][0].shape) for o in outs] for wname, outs in results.items()}

    def leaf(o):
        return (big["a_w_in"][o], small_res[o][0], small_res[o][1], small_res[o][2], small_res[o][3],
                big["a_w_out"][o], big["b_w_in"][o], big["b_w_grp"][o], small_res[o][7], big["b_w_out"][o],
                small_res[o][4], small_res[o][5], big["mlp_w1"][o], big["mlp_w2"][o], small_res[o][6])

    return (loss, grad_x[None], *leaf(0), *leaf(1), *leaf(2), *leaf(3))
```

```python
import math

import jax
import jax.numpy as jnp
from jax import lax
from jax.experimental import pallas as pl
from jax.experimental.pallas import tpu as pltpu

F32 = jnp.float32
BF16 = jnp.bfloat16
MESH = pl.DeviceIdType.MESH

N_DEV = 8
N_CHIPS = 4
CHUNK = 128
A_GROUPS = 8
B_WINDOWS = (2, 4, 8, 16)
B_GROUPS = len(B_WINDOWS)
EPS = 1e-6
ADAM_LR = 0.001
ADAM_B1 = 0.9
ADAM_B2 = 0.999
ADAM_EPS = 1e-08
ADAM_WD = 0.01
ADAM_STEP = 10

VMEM_LIMIT = 48 * 1024 * 1024
ROW_CHUNK = 16

NN = (((1,), (0,)), ((), ()))
NT = (((1,), (1,)), ((), ()))
TN = (((0,), (0,)), ((), ()))

_ANY = pl.BlockSpec(memory_space=pl.ANY)
_HBM = pl.BlockSpec(memory_space=pltpu.HBM)
_SEM = pl.BlockSpec(memory_space=pltpu.SEMAPHORE)
_EFFECT = pltpu.SideEffectType.DATAFLOW_SIDE_EFFECTING


def _tile(n, pref):
    return pref if n % pref == 0 else n


def _sds(shape, dtype):
    return jax.ShapeDtypeStruct(shape, dtype)


def _pcall(name, body, operands, in_specs, out_shape, out_specs, *, grid=None, sem=None, scratch=(),
           prefetch=(), after=None, aliases=None):
    after = [] if after is None else [after]
    n_lead = len(prefetch) + len(operands)
    n_after = len(after)

    def wrapped(*refs):
        body(*refs[:n_lead], *refs[n_lead + n_after:])

    in_specs = list(in_specs) + [_ANY] * n_after
    params = pltpu.CompilerParams(vmem_limit_bytes=VMEM_LIMIT) if sem is None else \
        pltpu.CompilerParams(dimension_semantics=sem, vmem_limit_bytes=VMEM_LIMIT)
    kwargs = dict(out_shape=out_shape, scratch_shapes=list(scratch), compiler_params=params, name=name,
                  input_output_aliases=aliases or {})
    if prefetch:
        kwargs["grid_spec"] = pltpu.PrefetchScalarGridSpec(
            num_scalar_prefetch=len(prefetch), grid=grid, in_specs=in_specs, out_specs=out_specs,
            scratch_shapes=list(scratch))
        kwargs.pop("scratch_shapes")
    else:
        kwargs.update(in_specs=in_specs, out_specs=out_specs)
        if grid is not None:
            kwargs["grid"] = grid
    return pl.pallas_call(wrapped, **kwargs)(*prefetch, *operands, *after)


def _matmul(name, a, b, dims, grid, a_spec, b_spec, out_shape, out_specs, acc_shape,
            epilogue, extras=(), extra_specs=(), after=None, prefetch=(), b_parts=1):
    nk = grid[2]
    n_extra = len(extras)
    n_out = len(out_shape)
    n_pre = len(prefetch)

    def body(*refs):
        refs = refs[n_pre:]
        a_ref, b_ref = refs[0], refs[1]
        extra_refs = refs[2:2 + n_extra]
        out_refs = refs[2 + n_extra:2 + n_extra + n_out]

        def finish(acc):
            outs = epilogue(acc, *[r[...] for r in extra_refs])
            for o_ref, o in zip(out_refs, outs):
                o_ref[...] = o.astype(o_ref.dtype)

        def product():
            if b_parts == 1:
                return lax.dot_general(a_ref[...], b_ref[...], dims, preferred_element_type=F32)
            width = b_ref.shape[2]
            total = None
            for p in range(b_parts):
                part = lax.dot_general(a_ref[:, p * width:(p + 1) * width], b_ref[p], dims,
                                       preferred_element_type=F32)
                total = part if total is None else total + part
            return total

        if nk == 1:
            finish(product())
        else:
            acc_ref = refs[-1]
            k = pl.program_id(2)

            @pl.when(k == 0)
            def _():
                acc_ref[...] = product()

            if nk > 2:
                @pl.when(jnp.logical_and(k > 0, k < nk - 1))
                def _():
                    acc_ref[...] += product()

            @pl.when(k == nk - 1)
            def _():
                finish(acc_ref[...] + product())

    scratch = [] if nk == 1 else [pltpu.VMEM(acc_shape, F32)]
    return _pcall(name, body, [a, b, *extras], [a_spec, b_spec, *extra_specs], out_shape, out_specs,
                  grid=grid, sem=("parallel", "parallel", "arbitrary"), scratch=scratch, after=after,
                  prefetch=prefetch)


def _mm_nn(name, a, b, epilogue, out_dtypes, extras=(), extra_kinds=(), slab=False, after=None,
           tm=1024, tn=1024, tk=2048):
    m, kd = a.shape
    if slab:
        n_slab, _, w = b.shape
        n = n_slab * w
        tn = _tile(w, min(tn, w))
        per = w // tn
        tk = _tile(kd, tk)
        b_spec = pl.BlockSpec((None, tk, tn), lambda i, j, k: (j // per, k, j % per))
    else:
        n = b.shape[1]
        tn = _tile(n, tn)
        tk = _tile(kd, tk)
        b_spec = pl.BlockSpec((tk, tn), lambda i, j, k: (k, j))
    tm = _tile(m, tm)
    grid = (m // tm, n // tn, kd // tk)
    a_spec = pl.BlockSpec((tm, tk), lambda i, j, k: (i, k))
    tile_spec = pl.BlockSpec((tm, tn), lambda i, j, k: (i, j))
    row_spec = pl.BlockSpec((1, tn), lambda i, j, k: (0, j))
    extra_specs = [tile_spec if kind == "tile" else row_spec for kind in extra_kinds]
    return _matmul(name, a, b, NN, grid, a_spec, b_spec,
                   [_sds((m, n), d) for d in out_dtypes], [tile_spec for _ in out_dtypes],
                   (tm, tn), epilogue, extras, extra_specs, after=after)


def _mm_nt(name, a, b, epilogue, out_dtypes, extras=(), extra_kinds=(), slab=False, after=None,
           tm=1024, tn=1024, tk=2048):
    m, kd = a.shape
    parts = 1
    if slab:
        n_slab, n, w = b.shape
        tn = _tile(n, tn)
        if tk > w and tk % w == 0 and n_slab % (tk // w) == 0:
            parts = tk // w
            b_spec = pl.BlockSpec((parts, tn, w), lambda i, j, k, *_: (k, j, 0))
        else:
            tk = _tile(w, min(tk, w))
            per = w // tk
            b_spec = pl.BlockSpec((None, tn, tk), lambda i, j, k, *_: (k // per, j, k % per))
    else:
        n = b.shape[0]
        tn = _tile(n, tn)
        tk = _tile(kd, tk)
        b_spec = pl.BlockSpec((tn, tk), lambda i, j, k, *_: (j, k))
    tm = _tile(m, tm)
    grid = (m // tm, n // tn, kd // tk)
    a_spec = pl.BlockSpec((tm, tk), lambda i, j, k, *_: (i, k))
    tile_spec = pl.BlockSpec((tm, tn), lambda i, j, k, *_: (i, j))
    row_spec = pl.BlockSpec((1, tn), lambda i, j, k, *_: (0, j))
    extra_specs = [tile_spec if kind == "tile" else row_spec for kind in extra_kinds]
    return _matmul(name, a, b, NT, grid, a_spec, b_spec,
                   [_sds((m, n), d) for d in out_dtypes], [tile_spec for _ in out_dtypes],
                   (tm, tn), epilogue, extras, extra_specs, after=after, b_parts=parts)


def _mm_tn_half(name, a, b, core, own, by_rows, block, recv=None, after=None, tm=1024, tn=1024, tk=2048):
    s, m = a.shape
    n = b.shape[1]
    tk = _tile(s, tk)

    def owner(chip, core_ref):
        return 2 * chip + (core_ref[0] if own else 1 - core_ref[0])

    if by_rows:
        r, c = block, n
        tm, tn = _tile(r, min(tm, r)), _tile(c, tn)
        per = r // tm
        grid = (N_CHIPS * per, c // tn, s // tk)
        a_spec = pl.BlockSpec((tk, tm), lambda i, j, k, cr, *_: (k, owner(i // per, cr) * per + i % per))
        b_spec = pl.BlockSpec((tk, tn), lambda i, j, k, cr, *_: (k, j))
        o_spec = pl.BlockSpec((None, tm, tn), lambda i, j, k, cr, *_: (i // per, i % per, j))
    else:
        r, c = m, block
        tm, tn = _tile(r, tm), _tile(c, min(tn, c))
        per = c // tn
        grid = (r // tm, N_CHIPS * per, s // tk)
        a_spec = pl.BlockSpec((tk, tm), lambda i, j, k, cr, *_: (k, i))
        b_spec = pl.BlockSpec((tk, tn), lambda i, j, k, cr, *_: (k, owner(j // per, cr) * per + j % per))
        o_spec = pl.BlockSpec((None, tm, tn), lambda i, j, k, cr, *_: (j // per, i, j % per))
    if recv is None:
        extras, epilogue = (), lambda acc: (acc,)
    else:
        extras, epilogue = (recv,), lambda acc, other: (acc + other.astype(F32),)
    return _matmul(name, a, b, TN, grid, a_spec, b_spec, [_sds((N_CHIPS, r, c), BF16)], [o_spec], (tm, tn),
                   epilogue, extras, [o_spec] * len(extras), after=after, prefetch=[core])[0]


def _rms_fwd(name, h, g, after=None):
    s, d = h.shape
    tr = _tile(s, 512)

    def body(h_ref, g_ref, o_ref):
        x = h_ref[...]
        r = lax.rsqrt(jnp.mean(x * x, axis=-1, keepdims=True) + EPS)
        o_ref[...] = (x * r * g_ref[...]).astype(o_ref.dtype)

    row = pl.BlockSpec((tr, d), lambda i: (i, 0))
    vec = pl.BlockSpec((1, d), lambda i: (0, 0))
    return _pcall(name, body, [h, g], [row, vec], _sds((s, d), BF16), row, grid=(s // tr,),
                  sem=("parallel",), after=after)


def _accumulate(ref, part, step):
    @pl.when(step == 0)
    def _():
        ref[...] = part

    @pl.when(step > 0)
    def _():
        ref[...] += part


def _rms_bwd(name, dhn, h, g, dres, after=None):
    s, d = h.shape
    tr = _tile(s, 512)
    steps = s // tr

    def body(dhn_ref, h_ref, g_ref, dres_ref, dh_ref, dhb_ref, gp_ref, acc_ref):
        step = pl.program_id(0)

        @pl.when(step == 0)
        def _():
            acc_ref[...] = jnp.zeros_like(acc_ref)

        gain = g_ref[...]

        def chunk(i, carry):
            rows = pl.ds(pl.multiple_of(i * ROW_CHUNK, ROW_CHUNK), ROW_CHUNK)
            x = h_ref[rows, :]
            r = lax.rsqrt(jnp.mean(x * x, axis=-1, keepdims=True) + EPS)
            n = x * r
            dy = dhn_ref[rows, :]
            dn = dy * gain
            dh = dres_ref[rows, :] + r * (dn - n * jnp.mean(dn * n, axis=-1, keepdims=True))
            dh_ref[rows, :] = dh
            dhb_ref[rows, :] = dh.astype(BF16)
            acc_ref[...] += dy * n
            return carry

        lax.fori_loop(0, tr // ROW_CHUNK, chunk, 0, unroll=8)

        @pl.when(step == steps - 1)
        def _():
            gp_ref[...] = jnp.sum(acc_ref[...], axis=0, keepdims=True)

    row = pl.BlockSpec((tr, d), lambda i: (i, 0))
    vec = pl.BlockSpec((1, d), lambda i: (0, 0))
    return _pcall(name, body, [dhn, h, g, dres], [row, row, vec, row],
                  [_sds((s, d), F32), _sds((s, d), BF16), _sds((1, d), F32)], [row, row, vec],
                  grid=(steps,), sem=("arbitrary",), scratch=[pltpu.VMEM((ROW_CHUNK, d), F32)], after=after)


def _loss_head(name, h, g, target, after=None):
    s, d = h.shape
    tr = _tile(s, 512)
    steps = s // tr

    def body(h_ref, g_ref, t_ref, dh_ref, dhb_ref, gp_ref, loss_ref, acc_ref, loss_acc_ref):
        step = pl.program_id(0)

        @pl.when(step == 0)
        def _():
            acc_ref[...] = jnp.zeros_like(acc_ref)
            loss_acc_ref[...] = jnp.zeros_like(loss_acc_ref)

        gg = g_ref[...]

        def chunk(i, carry):
            rows = pl.ds(pl.multiple_of(i * ROW_CHUNK, ROW_CHUNK), ROW_CHUNK)
            x = h_ref[rows, :]
            r = lax.rsqrt(jnp.mean(x * x, axis=-1, keepdims=True) + EPS)
            n = x * r
            e = n * gg - t_ref[rows, :]
            dy = e * (1.0 / d)
            dn = dy * gg
            dh = r * (dn - n * jnp.mean(dn * n, axis=-1, keepdims=True))
            dh_ref[rows, :] = dh
            dhb_ref[rows, :] = dh.astype(BF16)
            acc_ref[...] += dy * n
            loss_acc_ref[...] += jnp.mean(e * e, axis=-1, keepdims=True)
            return carry

        lax.fori_loop(0, tr // ROW_CHUNK, chunk, 0, unroll=8)

        @pl.when(step == steps - 1)
        def _():
            gp_ref[...] = jnp.sum(acc_ref[...], axis=0, keepdims=True)
            loss_ref[...] = 0.5 * jnp.sum(loss_acc_ref[...], axis=0, keepdims=True)

    row = pl.BlockSpec((tr, d), lambda i: (i, 0))
    vec = pl.BlockSpec((1, d), lambda i: (0, 0))
    one = pl.BlockSpec((1, 1), lambda i: (0, 0))
    return _pcall(name, body, [h, g, target], [row, vec, row],
                  [_sds((s, d), F32), _sds((s, d), BF16), _sds((1, d), F32), _sds((1, 1), F32)],
                  [row, row, vec, one], grid=(steps,), sem=("arbitrary",),
                  scratch=[pltpu.VMEM((ROW_CHUNK, d), F32), pltpu.VMEM((ROW_CHUNK, 1), F32)], after=after)


_SQRT_HALF = math.sqrt(0.5)
_INV_SQRT_2PI = 1.0 / math.sqrt(2.0 * math.pi)


def _gelu(x):
    return 0.5 * x * (1.0 + lax.erf(x * _SQRT_HALF))


def _gelu_grad(x):
    return 0.5 * (1.0 + lax.erf(x * _SQRT_HALF)) + x * jnp.exp(-0.5 * x * x) * _INV_SQRT_2PI


def _causal_mask():
    row = lax.broadcasted_iota(jnp.int32, (CHUNK, CHUNK), 0)
    col = lax.broadcasted_iota(jnp.int32, (CHUNK, CHUNK), 1)
    return row >= col


def _row_sum(x):
    return jnp.sum(x, axis=-1, keepdims=True)


def _masked_spatial(ws_ref, grp):
    return jnp.where(_causal_mask(), ws_ref[grp], 0.0).astype(BF16)


def _layernorm_stats(pre_ref, v_scr, w, head):
    total = jnp.zeros((CHUNK, 1), F32)
    for grp in range(A_GROUPS):
        v = _gelu(pre_ref[:, w + grp * head:w + (grp + 1) * head])
        v_scr[:, grp * head:(grp + 1) * head] = v
        total = total + _row_sum(v)
    mu = total * (1.0 / w)
    square = jnp.zeros((CHUNK, 1), F32)
    for grp in range(A_GROUPS):
        xc = v_scr[:, grp * head:(grp + 1) * head] - mu
        square = square + _row_sum(xc * xc)
    return mu, lax.rsqrt(square * (1.0 / w) + EPS)


def _amix_fwd(name, pre, ln_g, ln_b, w_s, b_s_col, after=None):
    s, w2 = pre.shape
    w = w2 // 2
    head = w // A_GROUPS

    def body(pre_ref, g_ref, b_ref, ws_ref, bs_ref, o_ref, v_scr):
        mu, rstd = _layernorm_stats(pre_ref, v_scr, w, head)
        for grp in range(A_GROUPS):
            cols = slice(grp * head, (grp + 1) * head)
            vhat = (v_scr[:, cols] - mu) * rstd
            vn = (vhat * g_ref[:, cols] + b_ref[:, cols]).astype(BF16)
            sg = jnp.dot(_masked_spatial(ws_ref, grp), vn, preferred_element_type=F32) + bs_ref[grp]
            o_ref[:, cols] = (_gelu(pre_ref[:, cols]) * sg).astype(o_ref.dtype)

    vec = pl.BlockSpec((1, w), lambda i: (0, 0))
    return _pcall(
        name, body, [pre, ln_g, ln_b, w_s, b_s_col],
        [pl.BlockSpec((CHUNK, w2), lambda i: (i, 0)), vec, vec,
         pl.BlockSpec((A_GROUPS, CHUNK, CHUNK), lambda i: (0, 0, 0)),
         pl.BlockSpec((A_GROUPS, CHUNK, 1), lambda i: (0, 0, 0))],
        _sds((s, w), BF16), pl.BlockSpec((CHUNK, w), lambda i: (i, 0)),
        grid=(s // CHUNK,), sem=("parallel",), scratch=[pltpu.VMEM((CHUNK, w), F32)], after=after)


def _amix_bwd(name, pre, dgated, ln_g, ln_b, w_s, b_s_col, after=None):
    s, w2 = pre.shape
    w = w2 // 2
    head = w // A_GROUPS

    def body(pre_ref, dg_ref, g_ref, b_ref, ws_ref, bs_ref, dpre_ref, glg_ref, glb_ref, gws_ref, gbs_ref,
             v_scr, dvn_scr):
        @pl.when(pl.program_id(0) == 0)
        def _():
            for ref in (glg_ref, glb_ref, gws_ref, gbs_ref):
                ref[...] = jnp.zeros_like(ref)

        mu, rstd = _layernorm_stats(pre_ref, v_scr, w, head)
        mask = _causal_mask()
        sum_dvhat = jnp.zeros((CHUNK, 1), F32)
        sum_dvhat_vhat = jnp.zeros((CHUNK, 1), F32)
        for grp in range(A_GROUPS):
            cols = slice(grp * head, (grp + 1) * head)
            vhat = (v_scr[:, cols] - mu) * rstd
            gain = g_ref[:, cols]
            vn = (vhat * gain + b_ref[:, cols]).astype(BF16)
            wm = _masked_spatial(ws_ref, grp)
            pre_u = pre_ref[:, cols]
            dgated = dg_ref[:, cols]
            ds = dgated * _gelu(pre_u)
            dsb = ds.astype(BF16)
            sg = jnp.dot(wm, vn, preferred_element_type=F32) + bs_ref[grp]
            dpre_ref[:, cols] = (dgated * sg * _gelu_grad(pre_u)).astype(dpre_ref.dtype)
            gws = lax.dot_general(dsb, vn, NT, preferred_element_type=F32)
            gws_ref[grp] += jnp.where(mask, gws, 0.0)
            gbs_ref[grp] += _row_sum(ds)
            dvn = lax.dot_general(wm, dsb, TN, preferred_element_type=F32)
            dvn_scr[:, cols] = dvn
            glg_ref[:, cols] += jnp.sum(dvn * vhat, axis=0, keepdims=True)
            glb_ref[:, cols] += jnp.sum(dvn, axis=0, keepdims=True)
            dvhat = dvn * gain
            sum_dvhat = sum_dvhat + _row_sum(dvhat)
            sum_dvhat_vhat = sum_dvhat_vhat + _row_sum(dvhat * vhat)
        mean_dvhat = sum_dvhat * (1.0 / w)
        mean_dvhat_vhat = sum_dvhat_vhat * (1.0 / w)
        for grp in range(A_GROUPS):
            cols = slice(grp * head, (grp + 1) * head)
            vhat = (v_scr[:, cols] - mu) * rstd
            dvhat = dvn_scr[:, cols] * g_ref[:, cols]
            dv = rstd * (dvhat - mean_dvhat - vhat * mean_dvhat_vhat)
            pre_v = pre_ref[:, w + grp * head:w + (grp + 1) * head]
            dpre_ref[:, w + grp * head:w + (grp + 1) * head] = (dv * _gelu_grad(pre_v)).astype(dpre_ref.dtype)

    vec = pl.BlockSpec((1, w), lambda i: (0, 0))
    ws_spec = pl.BlockSpec((A_GROUPS, CHUNK, CHUNK), lambda i: (0, 0, 0))
    bs_spec = pl.BlockSpec((A_GROUPS, CHUNK, 1), lambda i: (0, 0, 0))
    return _pcall(
        name, body, [pre, dgated, ln_g, ln_b, w_s, b_s_col],
        [pl.BlockSpec((CHUNK, w2), lambda i: (i, 0)), pl.BlockSpec((CHUNK, w), lambda i: (i, 0)),
         vec, vec, ws_spec, bs_spec],
        [_sds((s, w2), BF16), _sds((1, w), F32), _sds((1, w), F32),
         _sds((A_GROUPS, CHUNK, CHUNK), F32), _sds((A_GROUPS, CHUNK, 1), F32)],
        [pl.BlockSpec((CHUNK, w2), lambda i: (i, 0)), vec, vec, ws_spec, bs_spec],
        grid=(s // CHUNK,), sem=("arbitrary",),
        scratch=[pltpu.VMEM((CHUNK, w), F32), pltpu.VMEM((CHUNK, w), F32)], after=after)


def _shift_rows(x, k, forward):
    n = x.shape[0]
    row = lax.broadcasted_iota(jnp.int32, x.shape, 0)
    if forward:
        return jnp.where(row >= k, pltpu.roll(x, k, 0), 0.0)
    return jnp.where(row < n - k, pltpu.roll(x, n - k, 0), 0.0)


def _window_sum(x, window, forward):
    k = 1
    while k < window:
        x = x + _shift_rows(x, k, forward)
        k *= 2
    return x


def _pool(name, v, backward, after=None):
    s, w = v.shape
    head = w // B_GROUPS
    lane = _tile(head, 128)

    def body(v_ref, o_ref):
        grp = pl.program_id(0)
        t = lax.broadcasted_iota(jnp.int32, (s, lane), 0)
        for idx, window in enumerate(B_WINDOWS):
            @pl.when(grp == idx)
            def _():
                inv_count = 1.0 / jnp.minimum(t + 1, window).astype(F32)
                for strip in range(head // lane):
                    cols = slice(strip * lane, (strip + 1) * lane)
                    x = v_ref[:, cols]
                    if backward:
                        out = _window_sum(x * inv_count, window, False) - x
                    else:
                        out = _window_sum(x, window, True) * inv_count - x
                    o_ref[:, cols] = out.astype(o_ref.dtype)

    spec = pl.BlockSpec((s, head), lambda g: (0, g))
    return _pcall(name, body, [v], [spec], _sds((s, w), BF16), spec, grid=(B_GROUPS,),
                  sem=("parallel",), after=after)


def _scaled_dx(name, dy, w, scale, mixed, after=None, tm=1024, tn=1024):
    s, n = dy.shape
    k = w.shape[0]
    tm, tn = _tile(s, tm), _tile(k, tn)

    def body(dy_ref, w_ref, sc_ref, mx_ref, o_ref, gs_ref):
        p = lax.dot_general(dy_ref[...], w_ref[...], NT, preferred_element_type=F32)
        o_ref[...] = (p * sc_ref[...]).astype(o_ref.dtype)
        _accumulate(gs_ref, jnp.sum(p * mx_ref[...].astype(F32), axis=0, keepdims=True), pl.program_id(1))

    tile = pl.BlockSpec((tm, tn), lambda j, i: (i, j))
    vec = pl.BlockSpec((1, tn), lambda j, i: (0, j))
    return _pcall(name, body, [dy, w, scale, mixed],
                  [pl.BlockSpec((tm, n), lambda j, i: (i, 0)), pl.BlockSpec((tn, n), lambda j, i: (j, 0)), vec, tile],
                  [_sds((s, k), BF16), _sds((1, k), F32)], [tile, vec],
                  grid=(k // tn, s // tm), sem=("parallel", "arbitrary"), after=after)


def _adamw(w, g, m, v):
    m = ADAM_B1 * m + (1.0 - ADAM_B1) * g
    v = ADAM_B2 * v + (1.0 - ADAM_B2) * (g * g)
    m_hat = m / (1.0 - ADAM_B1 ** ADAM_STEP)
    v_hat = v / (1.0 - ADAM_B2 ** ADAM_STEP)
    delta = -ADAM_LR * (m_hat / (jnp.sqrt(v_hat) + ADAM_EPS) + ADAM_WD * w)
    return delta, m, v


def _adam_rows(name, g, w, m, v, rows, after=None):
    lanes = g.shape[1]
    n = len(rows)

    def body(g_ref, w_ref, m_ref, v_ref, *out_refs):
        first = 0
        for p, count in enumerate(rows):
            part = slice(first, first + count)
            outs = _adamw(w_ref[part, :], g_ref[part, :], m_ref[part, :], v_ref[part, :])
            for o_ref, o in zip(out_refs[3 * p:3 * p + 3], outs):
                o_ref[...] = o
            first += count

    vmem = pl.BlockSpec(memory_space=pltpu.VMEM)
    return _pcall(name, body, [g, w, m, v], [vmem] * 4,
                  [_sds((count, lanes), F32) for count in rows for _ in range(3)], [vmem] * (3 * n), after=after)


def _position():
    return lax.axis_index("x"), lax.axis_index("y"), lax.axis_index("c")


def _other_chips(x, y):
    return [(1 - x, y), (x, 1 - y), (1 - x, 1 - y)]


def _slot(px, py, pc):
    return 4 * px + 2 * py + pc


def _hbm(a):
    return pltpu.with_memory_space_constraint(a, pltpu.HBM)


def _hop1_copies(srcs, lands, send_sems, recv_sems):
    x, y, c = _position()
    peers = [(x, y, 1 - c), (1 - x, y, c), (x, 1 - y, c)]
    mine = _slot(x, y, c)
    return [[pltpu.make_async_remote_copy(
        src_ref=srcs[t], dst_ref=lands[t].at[mine], send_sem=send_sems[t].at[k], recv_sem=recv_sems[t].at[k],
        device_id=peer, device_id_type=MESH) for k, peer in enumerate(peers)] for t in range(len(srcs))]


def _hop2_copies(lands, send_sems, recv_sems):
    x, y, c = _position()
    routes = [(_slot(1 - x, y, c), (x, 1 - y, c)), (_slot(x, 1 - y, c), (1 - x, y, c))]
    out = []
    for t in range(len(lands)):
        rows = lands[t].shape[1]
        halves = [(0, rows // 2), (rows // 2, rows - rows // 2)]
        per_tensor = []
        for h, ((slot, peer), (start, size)) in enumerate(zip(routes, halves)):
            if size:
                block = lands[t].at[slot, pl.ds(start, size)]
                per_tensor.append(pltpu.make_async_remote_copy(
                    src_ref=block, dst_ref=block, send_sem=send_sems[t].at[h], recv_sem=recv_sems[t].at[h],
                    device_id=peer, device_id_type=MESH))
        for j, (slot, _) in enumerate(routes):
            block = lands[t].at[slot]
            per_tensor.append(pltpu.make_async_remote_copy(
                src_ref=block, dst_ref=block, send_sem=send_sems[t].at[2 + j], recv_sem=recv_sems[t].at[2 + j],
                device_id=(x, y, 1 - c), device_id_type=MESH))
        out.append(per_tensor)
    return out


def _split_start(name, srcs, lands, copies, n_sems, after=None):
    n = len(srcs)
    order = [] if after is None else [after]
    n_in = 2 * n + len(order)

    def body(*refs):
        for per_tensor in copies(refs[:n], refs[n:2 * n], refs[n_in:n_in + n], refs[n_in + n:n_in + 2 * n]):
            for cp in per_tensor:
                cp.start()
        refs[-1][...] = jnp.zeros_like(refs[-1])

    out_shape = ([pltpu.SemaphoreType.DMA((n_sems,)) for _ in range(2 * n)]
                 + [pltpu.HBM(a.shape, a.dtype) for a in list(srcs) + list(lands)]
                 + [_sds((8, 128), F32)])
    out = pl.pallas_call(
        body, name=name, out_shape=out_shape, in_specs=[_HBM] * (2 * n) + [_ANY] * len(order),
        out_specs=[_SEM] * (2 * n) + [_HBM] * (2 * n) + [pl.BlockSpec(memory_space=pltpu.VMEM)],
        input_output_aliases={i: 2 * n + i for i in range(2 * n)},
        compiler_params=pltpu.CompilerParams(has_side_effects=_EFFECT),
    )(*[_hbm(a) for a in srcs], *[_hbm(a) for a in lands], *order)
    return [(out[t], out[n + t], out[2 * n + t], out[3 * n + t]) for t in range(n)], out[-1]


def _split_wait(name, started, copies, after):
    n = len(started)

    def body(*refs):
        for per_tensor in copies(refs[:n], refs[n:2 * n], refs[2 * n:3 * n], refs[3 * n:4 * n]):
            for cp in per_tensor:
                cp.wait_send()
                cp.wait_recv()

    srcs = [e[2] for e in started]
    lands = [e[3] for e in started]
    out = pl.pallas_call(
        body, name=name, out_shape=[pltpu.HBM(a.shape, a.dtype) for a in srcs + lands],
        in_specs=[_HBM] * (2 * n) + [_SEM] * (2 * n) + [_ANY], out_specs=[_HBM] * (2 * n),
        input_output_aliases={i: i for i in range(2 * n)},
        compiler_params=pltpu.CompilerParams(has_side_effects=_EFFECT),
    )(*srcs, *lands, *[e[0] for e in started], *[e[1] for e in started], after)
    return out[:n], out[n:]


def _gather_step(name, arrived, fresh, after=None):
    n, m = len(arrived), len(fresh)
    order = [] if after is None else [after]
    fresh_lands = [lax.empty((N_DEV,) + s.shape, s.dtype) for s in fresh]
    buffers = [e[2] for e in arrived] + [e[3] for e in arrived] + list(fresh) + fresh_lands
    old_sems = [e[0] for e in arrived] + [e[1] for e in arrived]
    n_buf, n_old = len(buffers), len(old_sems)
    first_new = n_buf + n_old + len(order)

    def body(*refs):
        bufs, old = refs[:n_buf], refs[n_buf:n_buf + n_old]
        new = refs[first_new:first_new + 2 * n + 2 * m]
        for per_tensor in _hop1_copies(bufs[:n], bufs[n:2 * n], old[:n], old[n:]):
            for cp in per_tensor:
                cp.wait_send()
                cp.wait_recv()
        second = _hop2_copies(bufs[n:2 * n], new[:n], new[n:2 * n])
        first = _hop1_copies(bufs[2 * n:2 * n + m], bufs[2 * n + m:], new[2 * n:2 * n + m], new[2 * n + m:])
        for per_tensor in second + first:
            for cp in per_tensor:
                cp.start()
        refs[-1][...] = jnp.zeros_like(refs[-1])

    n_new = 2 * n + 2 * m
    out_shape = ([pltpu.SemaphoreType.DMA((4,)) for _ in range(2 * n)]
                 + [pltpu.SemaphoreType.DMA((3,)) for _ in range(2 * m)]
                 + [pltpu.HBM(a.shape, a.dtype) for a in buffers] + [_sds((8, 128), F32)])
    out = pl.pallas_call(
        body, name=name, out_shape=out_shape,
        in_specs=[_HBM] * n_buf + [_SEM] * n_old + [_ANY] * len(order),
        out_specs=[_SEM] * n_new + [_HBM] * n_buf + [pl.BlockSpec(memory_space=pltpu.VMEM)],
        input_output_aliases={i: n_new + i for i in range(n_buf)},
        compiler_params=pltpu.CompilerParams(has_side_effects=_EFFECT),
    )(*[_hbm(a) for a in buffers], *old_sems, *order)
    sems, bufs = out[:n_new], out[n_new:n_new + n_buf]
    second = [(sems[t], sems[n + t], bufs[t], bufs[n + t]) for t in range(n)]
    first = [(sems[2 * n + t], sems[2 * n + m + t], bufs[2 * n + t], bufs[2 * n + m + t]) for t in range(m)]
    return second, first, out[-1]


def _gather_wait(name, second, after):
    return _split_wait(name, second, lambda srcs, lands, send, recv: _hop2_copies(lands, send, recv), after)


def _sibling_copies(srcs, lands, send_sems, recv_sems):
    x, y, c = _position()
    return [[pltpu.make_async_remote_copy(
        src_ref=srcs[t], dst_ref=lands[t], send_sem=send_sems[t].at[0], recv_sem=recv_sems[t].at[0],
        device_id=(x, y, 1 - c), device_id_type=MESH)] for t in range(len(srcs))]


def _sibling_and_scatter_start(name, arrays, partials):
    k = len(arrays)

    def copies(srcs, lands, send_sems, recv_sems):
        return (_sibling_copies(srcs[:k], lands[:k], send_sems[:k], recv_sems[:k])
                + _scatter_copies(srcs[k:], lands[k:], send_sems[k:], recv_sems[k:]))

    lands = ([lax.empty(a.shape, a.dtype) for a in arrays]
             + [lax.empty((N_CHIPS - 1,) + p.shape[1:], p.dtype) for p in partials])
    return _split_start(name, list(arrays) + list(partials), lands, copies, 3)


def _sibling_wait(name, started, after):
    return _split_wait(name, started, _sibling_copies, after)[1]


def _small_copies(srcs, lands, send_sems, recv_sems):
    x, y, c = _position()
    mine = _slot(x, y, c)
    peers = [(x ^ ((k >> 2) & 1), y ^ ((k >> 1) & 1), c ^ (k & 1)) for k in range(1, N_DEV)]
    return [[pltpu.make_async_remote_copy(
        src_ref=srcs[t], dst_ref=lands[t].at[mine], send_sem=send_sems[t].at[k], recv_sem=recv_sems[t].at[k],
        device_id=peer, device_id_type=MESH) for k, peer in enumerate(peers)] for t in range(len(srcs))]


def _gather_finish(name, shards, lands, after):
    n = len(shards)

    def body(*refs):
        srcs, lands_in, outs = refs[:n], refs[n:2 * n], refs[2 * n:3 * n]
        send_sems, recv_sems, local_sems = refs[3 * n:]
        x, y, c = _position()
        local = [pltpu.make_async_copy(srcs[t], outs[t].at[_slot(x, y, c)], local_sems.at[t]) for t in range(n)]

        def diagonal(t, core):
            block = outs[t].at[_slot(1 - x, 1 - y, core)]
            return pltpu.make_async_remote_copy(
                src_ref=block, dst_ref=block, send_sem=send_sems.at[t], recv_sem=recv_sems.at[t],
                device_id=(x, y, 1 - c), device_id_type=MESH)

        for cp in local:
            cp.start()
        for t in range(n):
            diagonal(t, c).start()
        for t in range(n):
            diagonal(t, c).wait_send()
            diagonal(t, 1 - c).wait_recv()
        for cp in local:
            cp.wait()

    return _pcall(name, body, [*shards, *lands], [_ANY] * (2 * n),
                  [_sds(l.shape, l.dtype) for l in lands], [_ANY] * n,
                  scratch=[pltpu.SemaphoreType.DMA((n,)), pltpu.SemaphoreType.DMA((n,)),
                           pltpu.SemaphoreType.DMA((n,))],
                  after=after, aliases={n + t: t for t in range(n)})


def _exchange_sibling(name, fulls, after):
    n = len(fulls)

    def body(*refs):
        src = refs[:n]
        out = refs[n:2 * n]
        send_sems, recv_sems = refs[2 * n:]
        x, y, c = _position()
        copies = [pltpu.make_async_remote_copy(
            src_ref=src[t].at[:, 1 - c], dst_ref=out[t], send_sem=send_sems.at[t], recv_sem=recv_sems.at[t],
            device_id=(x, y, 1 - c), device_id_type=MESH) for t in range(n)]
        for cp in copies:
            cp.start()
        for cp in copies:
            cp.wait()

    return _pcall(name, body, fulls, [_ANY] * n, [_sds((N_CHIPS,) + f.shape[2:], f.dtype) for f in fulls],
                  [_ANY] * n, scratch=[pltpu.SemaphoreType.DMA((n,)), pltpu.SemaphoreType.DMA((n,))],
                  after=after)


def _add_sibling(name, full, recv, core, after):
    _, _, r, c = full.shape
    tr = _tile(r, max(8, (256 * 1024) // c))

    def body(core_ref, f_ref, r_ref, o_ref):
        o_ref[...] = (f_ref[...].astype(F32) + r_ref[...].astype(F32)).astype(o_ref.dtype)

    return _pcall(
        name, body, [full, recv],
        [pl.BlockSpec((None, None, tr, c), lambda p, i, core_ref: (p, core_ref[0], i, 0)),
         pl.BlockSpec((None, tr, c), lambda p, i, core_ref: (p, i, 0))],
        _sds((N_CHIPS, r, c), BF16), pl.BlockSpec((None, tr, c), lambda p, i, core_ref: (p, i, 0)),
        grid=(N_CHIPS, r // tr), sem=("parallel", "parallel"), prefetch=[core], after=after)


def _scatter_copies(srcs, lands, send_sems, recv_sems):
    x, y, c = _position()
    return [[pltpu.make_async_remote_copy(
        src_ref=srcs[t].at[2 * px + py], dst_ref=lands[t].at[j],
        send_sem=send_sems[t].at[j], recv_sem=recv_sems[t].at[j],
        device_id=(px, py, c), device_id_type=MESH) for j, (px, py) in enumerate(_other_chips(x, y))]
        for t in range(len(srcs))]


def _scatter_wait(name, started, after):
    return _split_wait(name, started, _scatter_copies, after)


N_BUF = 3


def _reduce_adam(name, partial, recv, chip, w, m, v, layer, carried, after=None):
    n_layers, r, c = w.shape
    tr = _tile(r, max(8, (256 * 1024) // c))
    n_blocks = r // tr
    n_carried = 0 if carried is None else 4

    def body(chip_ref, p_hbm, r_hbm, w_hbm, m_hbm, v_hbm, *rest):
        outs_hbm = rest[n_carried:n_carried + 4]
        p_buf, r_buf, w_buf, m_buf, v_buf, g_out, d_out, nm_out, nv_out, in_sems, out_sems = rest[n_carried + 4:]
        in_bufs = (p_buf, r_buf, w_buf, m_buf, v_buf)
        out_bufs = (g_out, d_out, nm_out, nv_out)

        def fetches(blk):
            slot, rows = blk % N_BUF, pl.ds(blk * tr, tr)
            srcs = (p_hbm.at[chip_ref[0], rows], r_hbm.at[:, rows], w_hbm.at[layer, rows], m_hbm.at[layer, rows],
                    v_hbm.at[layer, rows])
            return [pltpu.make_async_copy(src, buf.at[slot], in_sems.at[slot, k])
                    for k, (src, buf) in enumerate(zip(srcs, in_bufs))]

        def stores(blk):
            slot, rows = blk % 2, pl.ds(blk * tr, tr)
            return [pltpu.make_async_copy(buf.at[slot], dst.at[layer, rows], out_sems.at[slot, k])
                    for k, (buf, dst) in enumerate(zip(out_bufs, outs_hbm))]

        for blk in range(min(N_BUF, n_blocks)):
            for cp in fetches(blk):
                cp.start()
        for blk in range(n_blocks):
            slot, out_slot = blk % N_BUF, blk % 2
            for cp in fetches(blk):
                cp.wait()
            if blk >= 2:
                for cp in stores(blk - 2):
                    cp.wait()
            g = p_buf[slot].astype(F32)
            for j in range(N_CHIPS - 1):
                g = g + r_buf[slot, j].astype(F32)
            g_out[out_slot] = g
            d_out[out_slot], nm_out[out_slot], nv_out[out_slot] = _adamw(w_buf[slot], g, m_buf[slot], v_buf[slot])
            for cp in stores(blk):
                cp.start(priority=1)
            if blk + N_BUF < n_blocks:
                for cp in fetches(blk + N_BUF):
                    cp.start()
        for blk in range(max(0, n_blocks - 2), n_blocks):
            for cp in stores(blk):
                cp.wait()

    operands = [partial, recv, w, m, v] + ([] if carried is None else list(carried))
    block = (tr, c)
    scratch = [pltpu.VMEM((N_BUF,) + block, BF16), pltpu.VMEM((N_BUF, N_CHIPS - 1) + block, BF16),
               pltpu.VMEM((N_BUF,) + block, F32), pltpu.VMEM((N_BUF,) + block, F32), pltpu.VMEM((N_BUF,) + block, F32),
               pltpu.VMEM((2,) + block, F32), pltpu.VMEM((2,) + block, F32), pltpu.VMEM((2,) + block, F32),
               pltpu.VMEM((2,) + block, F32), pltpu.SemaphoreType.DMA((N_BUF, 5)), pltpu.SemaphoreType.DMA((2, 4))]
    return _pcall(name, body, operands, [_ANY] * len(operands), [_sds((n_layers, r, c), F32)] * 4, [_ANY] * 4,
                  grid=(1,), sem=("arbitrary",), scratch=scratch, prefetch=[chip], after=after,
                  aliases={1 + 5 + o: o for o in range(n_carried)})


def _small_sum(name, gathered, own, device, after=None):
    r, lanes = own.shape

    def body(dev_ref, g_ref, own_ref, out_ref):
        dev = dev_ref[0]
        mine = own_ref[...]
        total = jnp.where(dev == 0, mine, g_ref[0])
        for d in range(1, N_DEV):
            total = total + jnp.where(dev == d, mine, g_ref[d])
        out_ref[...] = total

    return _pcall(name, body, [gathered, own],
                  [pl.BlockSpec((N_DEV, r, lanes), lambda i, dev_ref: (0, 0, 0)),
                   pl.BlockSpec((r, lanes), lambda i, dev_ref: (0, 0))],
                  _sds((r, lanes), F32), pl.BlockSpec((r, lanes), lambda i, dev_ref: (0, 0)),
                  grid=(1,), sem=("arbitrary",), prefetch=[device], after=after)


def _pack(arrays):
    return jnp.concatenate([a.reshape(-1, 128) for a in arrays], axis=0)


def _unpack(packed, shapes):
    out, row = [], 0
    for shape in shapes:
        rows = math.prod(shape) // 128
        out.append(packed[row:row + rows].reshape(shape))
        row += rows
    return out


class _Order:
    def __init__(self):
        self.last = None

    def __call__(self, fn, *args, **kwargs):
        out = fn(*args, after=self.last, **kwargs)
        self.last = out[0] if isinstance(out, (list, tuple)) else out
        return out


def kernel(x, a_w_in, a_ln_g, a_ln_b, a_w_s, a_b_s, a_w_out, b_w_in, b_w_grp, b_scale, b_w_out, norm_mix, norm_mlp, mlp_w1, mlp_w2, final_norm, loss_target, m_a_w_in, m_a_ln_g, m_a_ln_b, m_a_w_s, m_a_b_s, m_a_w_out, m_b_w_in, m_b_w_grp, m_b_scale, m_b_w_out, m_norm_mix, m_norm_mlp, m_mlp_w1, m_mlp_w2, m_final_norm, v_a_w_in, v_a_ln_g, v_a_ln_b, v_a_w_s, v_a_b_s, v_a_w_out, v_b_w_in, v_b_w_grp, v_b_scale, v_b_w_out, v_norm_mix, v_norm_mlp, v_mlp_w1, v_mlp_w2, v_final_norm):
    s, d = x.shape[1], x.shape[2]
    depth = mlp_w1.shape[0]
    a_slab = a_w_in.shape[2]
    ff_slab = mlp_w1.shape[2]
    ff_rows = mlp_w2.shape[1]
    bh = b_w_grp.shape[3]
    my_x, my_y, my_c = _position()
    core = jnp.reshape(my_c, (1,)).astype(jnp.int32)
    chip = jnp.reshape(2 * my_x + my_y, (1,)).astype(jnp.int32)
    device = _slot(my_x, my_y, my_c)
    run = _Order()

    w1_b, w2_b = mlp_w1.astype(BF16), mlp_w2.astype(BF16)
    shards = [a_w_in[0].astype(BF16), a_w_out[0].astype(BF16), b_scale,
              w1_b[0], w2_b[0],
              b_w_in[0].astype(BF16), b_w_grp[0].astype(BF16), b_w_out[0].astype(BF16),
              w1_b[1], w2_b[1]]
    groups = [[0], [1, 2], [3], [4], [5, 6, 7], [8], [9]]
    start_with = {1: [2, 3], 2: [4], 3: [5], 4: [6]}
    hop1, hop2 = {}, {}
    _, hop1[0], token = _gather_step("weights_group0_hop1", [], [shards[t] for t in groups[0]])
    _, hop1[1], token = _gather_step("weights_group1_hop1", [], [shards[t] for t in groups[1]], token)
    run.last = token

    def advance(g):
        if g not in hop1:
            return
        ahead = start_with.get(g, [])
        fresh = [shards[t] for a in ahead for t in groups[a]]
        hop2[g], started, tok = _gather_step(f"weights_group{g}_hop2", hop1.pop(g), fresh, run.last)
        for a in ahead:
            hop1[a], started = started[:len(groups[a])], started[len(groups[a]):]
        run.last = tok

    def gathered(g):
        advance(g)
        if g == 0:
            advance(1)
        srcs, lands = _gather_wait(f"weights_group{g}_wait", hop2.pop(g), run.last)
        run.last = srcs[0]
        return run(_gather_finish, f"weights_group{g}_finish", srcs, lands)

    h0 = x[0]
    target = loss_target[0]
    ln_g, ln_b = a_ln_g, a_ln_b
    w_s = a_w_s[0]
    b_s_col = a_b_s[0][:, :, None]
    nmix = [norm_mix[l][None, :] for l in range(depth)]
    nmlp = [norm_mlp[l][None, :] for l in range(depth)]

    def mlp_forward(l, h, up_group):
        hn = run(_rms_fwd, f"mlp{l}_norm", h, nmlp[l])
        (w1,) = gathered(up_group)
        advance(up_group + 1)
        act, act_sq = run(_mm_nn, f"mlp{l}_up", hn, w1,
                          lambda acc: (jnp.maximum(acc, 0.0), jnp.square(jnp.maximum(acc, 0.0))),
                          (BF16, BF16), slab=True)
        (w2,) = gathered(up_group + 1)
        advance(up_group + 2)
        w2 = w2.reshape(-1, d)
        (h_out,) = run(_mm_nn, f"mlp{l}_down", act_sq, w2, lambda acc, res: (acc + res,), (F32,),
                       extras=(h,), extra_kinds=("tile",))
        return h_out, (h, hn, act, act_sq, w1, w2)

    scattered = []

    pending = []

    def scatter_partials(name, partials, specs):
        pending.append((name, partials, specs))

    def start_exchanges(name, to_sibling):
        partials = [p for _, group, _ in pending for p in group]
        in_flight, tok = _sibling_and_scatter_start(name, to_sibling, partials)
        run.last = tok
        first = len(to_sibling)
        for group_name, group, specs in pending:
            scattered.append((group_name, in_flight[first:first + len(group)], specs))
            first += len(group)
        pending.clear()
        return in_flight[:len(to_sibling)]

    weights = {"a_w_in": (a_w_in, m_a_w_in, v_a_w_in), "a_w_out": (a_w_out, m_a_w_out, v_a_w_out),
               "b_w_in": (b_w_in, m_b_w_in, v_b_w_in), "b_w_grp": (b_w_grp, m_b_w_grp, v_b_w_grp),
               "b_w_out": (b_w_out, m_b_w_out, v_b_w_out), "mlp_w1": (mlp_w1, m_mlp_w1, v_mlp_w1),
               "mlp_w2": (mlp_w2, m_mlp_w2, v_mlp_w2)}
    results = {}

    def finish_group(name, in_flight, specs):
        partials, lands = _scatter_wait(name + "_scatter_wait", in_flight, run.last)
        run.last = lands[0]
        for t, (wname, layer) in enumerate(specs):
            w, m, v = weights[wname]
            shape = (w.shape[0],) + partials[t].shape[1:]
            results[wname] = run(_reduce_adam, f"{name}_reduce_adam_{t}", partials[t], lands[t], chip,
                                 w.reshape(shape), m.reshape(shape), v.reshape(shape), layer, results.get(wname))

    def weight_grad(name, a, b, by_rows, block, between, tm=1024, tn=1024):
        other = run(_mm_tn_half, name + "_other", a, b, core, False, by_rows, block, tm=tm, tn=tn)
        sent = start_exchanges(name + "_sibling_start", [other])
        middle = between()
        (recv,) = _sibling_wait(name + "_sibling_wait", sent, run.last)
        run.last = recv
        return run(_mm_tn_half, name + "_own", a, b, core, True, by_rows, block, recv=recv, tm=tm, tn=tn), middle

    def mlp_backward(l, saved, dh, dhb):
        h, hn, act, act_sq, w1, w2 = saved
        part_w2, (dpre,) = weight_grad(
            f"mlp{l}_down_dw", act_sq, dhb, True, ff_rows,
            lambda: run(_mm_nt, f"mlp{l}_down_dx", dhb, w2, lambda acc, a: (2.0 * a.astype(F32) * acc,),
                        (BF16,), extras=(act,), extra_kinds=("tile",)))
        scatter_partials(f"mlp{l}_down_grads", [part_w2], [("mlp_w2", l)])
        part_w1, (dhn,) = weight_grad(
            f"mlp{l}_up_dw", hn, dpre, False, ff_slab,
            lambda: run(_mm_nt, f"mlp{l}_up_dx", dpre, w1, lambda acc: (acc,), (F32,), slab=True))
        scatter_partials(f"mlp{l}_up_grads", [part_w1], [("mlp_w1", l)])
        dh, dhb, g_norm = run(_rms_bwd, f"mlp{l}_norm_bwd", dhn, h, nmlp[l], dh)
        return dh, dhb, g_norm

    hn0 = run(_rms_fwd, "mix0_norm", h0, nmix[0])
    (wa_in,) = gathered(0)
    (pre,) = run(_mm_nn, "mixa_in", hn0, wa_in, lambda acc: (acc,), (F32,), slab=True)
    wa_out, scale = gathered(1)
    wa_out, scale = wa_out.reshape(d, d), scale.reshape(1, d)
    gated = run(_amix_fwd, "mixa_gate", pre, ln_g, ln_b, w_s, b_s_col)
    advance(2)
    (h1,) = run(_mm_nn, "mixa_out", gated, wa_out, lambda acc, res: (acc + res,), (F32,),
                extras=(h0,), extra_kinds=("tile",))
    h2, saved_mlp0 = mlp_forward(0, h1, 2)
    hn2 = run(_rms_fwd, "mix1_norm", h2, nmix[1])
    wb_in, wb_grp, wb_out = gathered(4)
    advance(5)
    wb_in, wb_out = wb_in.reshape(d, d), wb_out.reshape(d, d)
    wb_grp = jnp.transpose(wb_grp, (1, 0, 2, 3)).reshape(B_GROUPS, bh, bh)
    (vb,) = run(_mm_nn, "mixb_in", hn2, wb_in, lambda acc: (acc,), (F32,))
    pooled = run(_pool, "mixb_pool", vb, backward=False)
    tm = _tile(s, 1024)
    grp_tile = pl.BlockSpec((tm, bh), lambda i, j, k: (i, j))
    grp_weight = pl.BlockSpec((None, bh, bh), lambda i, j, k: (j, 0, 0))
    mixed, mixed_scaled = run(
        _matmul, "mixb_grp", pooled, wb_grp, NN, (s // tm, B_GROUPS, 1), grp_tile, grp_weight,
        [_sds((s, d), BF16), _sds((s, d), BF16)], [grp_tile] * 2,
        (tm, bh), lambda acc, sc: (acc, acc * sc), (scale,), [pl.BlockSpec((1, bh), lambda i, j, k: (0, j))])
    (h3,) = run(_mm_nn, "mixb_out", mixed_scaled, wb_out, lambda acc, res: (acc + res,), (F32,),
                extras=(h2,), extra_kinds=("tile",))
    h4, saved_mlp1 = mlp_forward(1, h3, 5)
    dh, dhb, g_final, loss_part = run(_loss_head, "loss_head", h4, final_norm[None, :], target)

    dh, dhb, g_nmlp1 = mlp_backward(1, saved_mlp1, dh, dhb)
    tks = _tile(s, 1024)
    grp_rows = pl.BlockSpec((tks, bh), lambda i, j, k: (k, j))

    def mixb_middle():
        dms_scaled, g_scale = run(_scaled_dx, "mixb_out_dx", dhb, wb_out, scale, mixed)
        (g_wb_grp,) = run(
            _matmul, "mixb_grp_dw", pooled, dms_scaled, TN, (1, B_GROUPS, s // tks), grp_rows, grp_rows,
            [_sds((B_GROUPS, bh, bh), BF16)], [grp_weight], (bh, bh), lambda acc: (acc,))
        (dpooled,) = run(
            _matmul, "mixb_grp_dx", dms_scaled, wb_grp, NT, (s // tm, B_GROUPS, 1), grp_tile, grp_weight,
            [_sds((s, d), F32)], [grp_tile], (tm, bh), lambda acc: (acc,))
        return g_scale, g_wb_grp, run(_pool, "mixb_pool_bwd", dpooled, backward=True)

    part_wb_out, (g_scale, g_wb_grp, dvb) = weight_grad("mixb_out_dw", mixed_scaled, dhb, True, d // N_DEV,
                                                        mixb_middle, tn=d)
    part_wb_in, (dhn2,) = weight_grad(
        "mixb_in_dw", hn2, dvb, True, d // N_DEV,
        lambda: run(_mm_nt, "mixb_in_dx", dvb, wb_in, lambda acc: (acc,), (F32,)), tn=d)
    grp_full = jnp.transpose(g_wb_grp.reshape(B_GROUPS, N_DEV, bh // N_DEV, bh), (1, 0, 2, 3))
    grp_full = grp_full.reshape(N_CHIPS, 2, B_GROUPS * bh // N_DEV, bh)
    (grp_sibling,) = run(_exchange_sibling, "mixb_grp_dw_to_sibling", [grp_full])
    part_wb_grp = run(_add_sibling, "mixb_grp_dw_add_sibling", grp_full, grp_sibling, core)
    scatter_partials("mixb_grads", [part_wb_out, part_wb_grp, part_wb_in],
                     [("b_w_out", 0), ("b_w_grp", 0), ("b_w_in", 0)])
    dh, dhb, g_nmix1 = run(_rms_bwd, "mix1_norm_bwd", dhn2, h2, nmix[1], dh)
    dh, dhb, g_nmlp0 = mlp_backward(0, saved_mlp0, dh, dhb)
    def mixa_middle():
        (dgated,) = run(_mm_nt, "mixa_out_dx", dhb, wa_out, lambda acc: (acc,), (F32,))
        return run(_amix_bwd, "mixa_gate_bwd", pre, dgated, ln_g, ln_b, w_s, b_s_col)

    part_wa_out, (dpre, g_ln_g, g_ln_b, g_w_s, g_b_s) = weight_grad("mixa_out_dw", gated, dhb, True, d // N_DEV,
                                                                     mixa_middle, tn=d)
    part_wa_in, (dhn0,) = weight_grad(
        "mixa_in_dw", hn0, dpre, False, a_slab,
        lambda: run(_mm_nt, "mixa_in_dx", dpre, wa_in, lambda acc: (acc,), (F32,), slab=True), tm=d)
    scatter_partials("mixa_grads", [part_wa_in, part_wa_out], [("a_w_in", 0), ("a_w_out", 0)])
    start_exchanges("mixa_grads_scatter_start", [])
    grad_x, _, g_nmix0 = run(_rms_bwd, "mix0_norm_bwd", dhn0, h0, nmix[0], dh)

    g_norm_mix = jnp.concatenate([g_nmix0, g_nmix1], axis=0)
    g_norm_mlp = jnp.concatenate([g_nmlp0, g_nmlp1], axis=0)
    loss_row = jnp.pad(loss_part, ((0, 0), (0, 127)))
    small_parts = [g_ln_g, g_ln_b, g_w_s, g_b_s, g_norm_mix, g_norm_mlp, g_final, g_scale, loss_row]
    packed = _pack(small_parts)
    small_sent, tok = _split_start("small_grads_start", [packed], [lax.empty((N_DEV,) + packed.shape, F32)],
                                   _small_copies, N_DEV - 1)
    run.last = tok

    for group in scattered:
        finish_group(*group)
    own_packed, small_gathered = _split_wait("small_grads_wait", small_sent, _small_copies, run.last)
    run.last = small_gathered[0]
    small_sum = run(_small_sum, "small_grads_sum", small_gathered[0], own_packed[0],
                    jnp.reshape(device, (1,)).astype(jnp.int32))
    sg = _unpack(small_sum, [a_ln_g.shape, a_ln_b.shape, a_w_s.shape, a_b_s.shape, norm_mix.shape,
                             norm_mlp.shape, final_norm.shape, (1, d), (1, 128)])
    loss = sg.pop()[0, 0]
    shard = b_scale.shape[1]
    sg[7] = lax.dynamic_slice(sg[7], (0, device * shard), (1, shard))
    small_w = [a_ln_g, a_ln_b, a_w_s, a_b_s, norm_mix, norm_mlp, final_norm, b_scale]
    small_m = [m_a_ln_g, m_a_ln_b, m_a_w_s, m_a_b_s, m_norm_mix, m_norm_mlp, m_final_norm, m_b_scale]
    small_v = [v_a_ln_g, v_a_ln_b, v_a_w_s, v_a_b_s, v_norm_mix, v_norm_mlp, v_final_norm, v_b_scale]
    small_out = run(_adam_rows, "small_adam", _pack(sg), _pack(small_w), _pack(small_m), _pack(small_v),
                    [w.size // 128 for w in small_w])
    small_res = [sg] + [[small_out[3 * p + o].reshape(w.shape) for p, w in enumerate(small_w)]
                        for o in range(3)]
    big = {wname: [o.reshape(weights[wname][0].shape) for o in outs] for wname, outs in results.items()}

    def leaf(o):
        return (big["a_w_in"][o], small_res[o][0], small_res[o][1], small_res[o][2], small_res[o][3],
                big["a_w_out"][o], big["b_w_in"][o], big["b_w_grp"][o], small_res[o][7], big["b_w_out"][o],
                small_res[o][4], small_res[o][5], big["mlp_w1"][o], big["mlp_w2"][o], small_res[o][6])

    return (loss, grad_x[None], *leaf(0), *leaf(1), *leaf(2), *leaf(3))
```
